```python
import math
import jax, jax.numpy as jnp
from jax import lax
import numpy as np

D_MODEL = 1024
BATCH = 8
SEQ = 4096
DEPTH = 4

N_MIXERS = 3
D_FF = 4 * D_MODEL
D_MIX = D_MODEL
EPS = 1e-6
CONV_WIDTH = 3
S5_GROUP = 16
S5_GROUPS = D_MIX // S5_GROUP
S5_STATE = 64
DT_MIN = 1e-3
DT_MAX = 1e-1
CHUNK = 128
SG_HEADS = 8
SG_HEAD_DIM = D_MIX // SG_HEADS
N_A = (DEPTH + 2) // 3
N_B = (DEPTH + 1) // 3
N_C = DEPTH // 3

kernel_name = "hybrid_conv_s5_sgmlp_trunk"


def rmsnorm(x, g):
    xf = x.astype(jnp.float32)
    y = xf * lax.rsqrt(jnp.mean(xf * xf, axis=-1, keepdims=True) + EPS)
    return (y * g.astype(jnp.float32)).astype(x.dtype)


def short_conv_mixer(h, w_in, conv_w, conv_b, w_out):
    bcx = h @ w_in
    b_gate, c_gate, xh = jnp.split(bcx, 3, axis=-1)
    z = c_gate * xh
    conv = lax.conv_general_dilated(
        z, conv_w[:, None, :].astype(z.dtype), window_strides=(1,),
        padding=[(CONV_WIDTH - 1, 0)], dimension_numbers=("NWC", "WIO", "NWC"),
        feature_group_count=D_MIX) + conv_b
    return (b_gate * conv) @ w_out


def _ssm_combine(e1, e2):
    a1r, a1i, b1r, b1i = e1
    a2r, a2i, b2r, b2i = e2
    ar = a2r * a1r - a2i * a1i
    ai = a2r * a1i + a2i * a1r
    br = a2r * b1r - a2i * b1i + b2r
    bi = a2r * b1i + a2i * b1r + b2i
    return (ar, ai, br, bi)


def s5_mixer(h, w_in, a_re, a_im, log_dt, b_re, b_im, c_re, c_im, d_skip, glu_w, glu_b, w_out):
    bsz, seq_len, _ = h.shape
    f32 = jnp.float32
    u = (h @ w_in).astype(f32).reshape(bsz, seq_len, S5_GROUPS, S5_GROUP)
    a_re = a_re.astype(f32); a_im = a_im.astype(f32)
    dt = jnp.exp(log_dt.astype(f32))[:, None]
    mag = jnp.exp(a_re * dt)
    abar_re = mag * jnp.cos(a_im * dt)
    abar_im = mag * jnp.sin(a_im * dt)
    den = a_re * a_re + a_im * a_im
    nr = abar_re - 1.0
    ni = abar_im
    f_re = ((nr * a_re + ni * a_im) / den)[..., None]
    f_im = ((ni * a_re - nr * a_im) / den)[..., None]
    b_re = b_re.astype(f32); b_im = b_im.astype(f32)
    bbar_re = f_re * b_re - f_im * b_im
    bbar_im = f_re * b_im + f_im * b_re
    bu_re = jnp.einsum("blgh,gph->blgp", u, bbar_re)
    bu_im = jnp.einsum("blgh,gph->blgp", u, bbar_im)
    a_seq_re = jnp.broadcast_to(abar_re, (1, seq_len, S5_GROUPS, S5_STATE))
    a_seq_im = jnp.broadcast_to(abar_im, (1, seq_len, S5_GROUPS, S5_STATE))
    _, _, s_re, s_im = lax.associative_scan(
        _ssm_combine, (a_seq_re, a_seq_im, bu_re, bu_im), axis=1)
    y = (jnp.einsum("blgp,ghp->blgh", s_re, c_re.astype(f32))
         - jnp.einsum("blgp,ghp->blgh", s_im, c_im.astype(f32)))
    y = y + d_skip.astype(f32).reshape(S5_GROUPS, S5_GROUP) * u
    y = jax.nn.gelu(y.reshape(bsz, seq_len, D_MIX))
    y = y * jax.nn.sigmoid(y @ glu_w.astype(f32) + glu_b.astype(f32))
    return y.astype(h.dtype) @ w_out


def spatial_gating_mixer(h, w_in, v_gain, w_s, b_s, w_out):
    bsz, seq_len, _ = h.shape
    u, v = jnp.split(h @ w_in, 2, axis=-1)
    v = rmsnorm(v, v_gain)
    vc = v.reshape(bsz, seq_len // CHUNK, CHUNK, SG_HEADS, SG_HEAD_DIM)
    causal = jnp.tril(jnp.ones((CHUNK, CHUNK), dtype=bool))
    ws = jnp.where(causal[None], w_s, jnp.zeros_like(w_s))
    vm = jnp.einsum("hts,bnshd->bnthd", ws, vc) + b_s.T[:, :, None]
    return (u * vm.reshape(bsz, seq_len, D_MIX)) @ w_out


def squared_relu_mlp(h, w1, w2):
    return jnp.square(jax.nn.relu(h @ w1)) @ w2


def _fwd_setup_inputs(seed: int = 0) -> dict:
    key = jax.random.key(seed)
    ks = iter(jax.random.split(key, 40))
    f32 = jnp.float32

    def nrm(shape, std):
        return std * jax.random.normal(next(ks), shape, f32)

    D = D_MODEL
    G, P = S5_GROUPS, S5_STATE
    x = nrm((BATCH, SEQ, D), 1.0)
    c = nrm((BATCH, D), 1.0)
    ada_w = nrm((DEPTH, D, 6 * D), 0.5 * D ** -0.5)
    ada_b = nrm((DEPTH, 6 * D), 0.02)
    norm1_g = 1.0 + nrm((DEPTH, D), 0.02)
    norm2_g = 1.0 + nrm((DEPTH, D), 0.02)
    ff_w1 = nrm((DEPTH, D, D_FF), D ** -0.5)
    ff_w2 = nrm((DEPTH, D_FF, D), D_FF ** -0.5)
    final_g = 1.0 + nrm((D,), 0.02)
    conv_w_in = nrm((N_A, D, 3 * D_MIX), D ** -0.5)
    conv_w = nrm((N_A, CONV_WIDTH, D_MIX), CONV_WIDTH ** -0.5)
    conv_b = nrm((N_A, D_MIX), 0.02)
    conv_w_out = nrm((N_A, D_MIX, D), D_MIX ** -0.5)
    ssm_w_in = nrm((N_B, D, D_MIX), D ** -0.5)
    ssm_a_re = -0.5 + nrm((N_B, G, P), 0.01)
    ssm_a_im = math.pi * jnp.arange(P, dtype=f32) + nrm((N_B, G, P), 0.01)
    ssm_log_dt = jax.random.uniform(next(ks), (N_B, G), f32,
                                    minval=math.log(DT_MIN), maxval=math.log(DT_MAX))
    ssm_b_re = nrm((N_B, G, P, S5_GROUP), S5_GROUP ** -0.5)
    ssm_b_im = nrm((N_B, G, P, S5_GROUP), S5_GROUP ** -0.5)
    ssm_c_re = nrm((N_B, G, S5_GROUP, P), P ** -0.5)
    ssm_c_im = nrm((N_B, G, S5_GROUP, P), P ** -0.5)
    ssm_d = nrm((N_B, D_MIX), 0.5)
    ssm_glu_w = nrm((N_B, D_MIX, D_MIX), D_MIX ** -0.5)
    ssm_glu_b = nrm((N_B, D_MIX), 0.02)
    ssm_w_out = nrm((N_B, D_MIX, D), D_MIX ** -0.5)
    sg_w_in = nrm((N_C, D, 2 * D_MIX), D ** -0.5)
    sg_v_g = 1.0 + nrm((N_C, D_MIX), 0.02)
    sg_w_s = nrm((N_C, SG_HEADS, CHUNK, CHUNK), CHUNK ** -0.5)
    sg_b_s = 1.0 + nrm((N_C, SG_HEADS, CHUNK), 0.02)
    sg_w_out = nrm((N_C, D_MIX, D), D_MIX ** -0.5)
    return {
        "x": x, "c": c, "ada_w": ada_w, "ada_b": ada_b,
        "norm1_g": norm1_g, "norm2_g": norm2_g, "ff_w1": ff_w1, "ff_w2": ff_w2,
        "final_g": final_g,
        "conv_w_in": conv_w_in, "conv_w": conv_w, "conv_b": conv_b, "conv_w_out": conv_w_out,
        "ssm_w_in": ssm_w_in, "ssm_a_re": ssm_a_re, "ssm_a_im": ssm_a_im,
        "ssm_log_dt": ssm_log_dt, "ssm_b_re": ssm_b_re, "ssm_b_im": ssm_b_im,
        "ssm_c_re": ssm_c_re, "ssm_c_im": ssm_c_im, "ssm_d": ssm_d,
        "ssm_glu_w": ssm_glu_w, "ssm_glu_b": ssm_glu_b, "ssm_w_out": ssm_w_out,
        "sg_w_in": sg_w_in, "sg_v_g": sg_v_g, "sg_w_s": sg_w_s, "sg_b_s": sg_b_s,
        "sg_w_out": sg_w_out,
    }


def _fwd_reference(x, c, ada_w, ada_b, norm1_g, norm2_g, ff_w1, ff_w2, final_g,
              conv_w_in, conv_w, conv_b, conv_w_out,
              ssm_w_in, ssm_a_re, ssm_a_im, ssm_log_dt, ssm_b_re, ssm_b_im,
              ssm_c_re, ssm_c_im, ssm_d, ssm_glu_w, ssm_glu_b, ssm_w_out,
              sg_w_in, sg_v_g, sg_w_s, sg_b_s, sg_w_out):
    c_act = jax.nn.silu(c)
    for i in range(DEPTH):
        kind = i % N_MIXERS
        j = i // N_MIXERS
        mod = (c_act @ ada_w[i] + ada_b[i])[:, None, :]
        sh1, sc1, g1, sh2, sc2, g2 = jnp.split(mod, 6, axis=-1)
        h = rmsnorm(x, norm1_g[i]) * (1.0 + sc1) + sh1
        if kind == 0:
            y = short_conv_mixer(h, conv_w_in[j], conv_w[j], conv_b[j], conv_w_out[j])
        elif kind == 1:
            y = s5_mixer(h, ssm_w_in[j], ssm_a_re[j], ssm_a_im[j], ssm_log_dt[j],
                         ssm_b_re[j], ssm_b_im[j], ssm_c_re[j], ssm_c_im[j], ssm_d[j],
                         ssm_glu_w[j], ssm_glu_b[j], ssm_w_out[j])
        else:
            y = spatial_gating_mixer(h, sg_w_in[j], sg_v_g[j], sg_w_s[j], sg_b_s[j], sg_w_out[j])
        x = x + g1 * y
        h = rmsnorm(x, norm2_g[i]) * (1.0 + sc2) + sh2
        x = x + g2 * squared_relu_mlp(h, ff_w1[i], ff_w2[i])
    return rmsnorm(x, final_g)


import jax as _jax
import jax.numpy as _jnp

TWIN_FORMAT = 'train_step'
FWD_PARAMS = ['x', 'c', 'ada_w', 'ada_b', 'norm1_g', 'norm2_g', 'ff_w1', 'ff_w2', 'final_g', 'conv_w_in', 'conv_w', 'conv_b', 'conv_w_out', 'ssm_w_in', 'ssm_a_re', 'ssm_a_im', 'ssm_log_dt', 'ssm_b_re', 'ssm_b_im', 'ssm_c_re', 'ssm_c_im', 'ssm_d', 'ssm_glu_w', 'ssm_glu_b', 'ssm_w_out', 'sg_w_in', 'sg_v_g', 'sg_w_s', 'sg_b_s', 'sg_w_out']
TWIN_WEIGHTS = ['ada_w', 'ada_b', 'norm1_g', 'norm2_g', 'ff_w1', 'ff_w2', 'final_g', 'conv_w_in', 'conv_w', 'conv_b', 'conv_w_out', 'ssm_w_in', 'ssm_a_re', 'ssm_a_im', 'ssm_log_dt', 'ssm_b_re', 'ssm_b_im', 'ssm_c_re', 'ssm_c_im', 'ssm_d', 'ssm_glu_w', 'ssm_glu_b', 'ssm_w_out', 'sg_w_in', 'sg_v_g', 'sg_w_s', 'sg_b_s', 'sg_w_out']
TWIN_DIFF_INPUT = 'x'
TWIN_INPUTS = ['x', 'c', 'ada_w', 'ada_b', 'norm1_g', 'norm2_g', 'ff_w1', 'ff_w2', 'final_g', 'conv_w_in', 'conv_w', 'conv_b', 'conv_w_out', 'ssm_w_in', 'ssm_a_re', 'ssm_a_im', 'ssm_log_dt', 'ssm_b_re', 'ssm_b_im', 'ssm_c_re', 'ssm_c_im', 'ssm_d', 'ssm_glu_w', 'ssm_glu_b', 'ssm_w_out', 'sg_w_in', 'sg_v_g', 'sg_w_s', 'sg_b_s', 'sg_w_out', 'loss_target', 'm_ada_w', 'm_ada_b', 'm_norm1_g', 'm_norm2_g', 'm_ff_w1', 'm_ff_w2', 'm_final_g', 'm_conv_w_in', 'm_conv_w', 'm_conv_b', 'm_conv_w_out', 'm_ssm_w_in', 'm_ssm_a_re', 'm_ssm_a_im', 'm_ssm_log_dt', 'm_ssm_b_re', 'm_ssm_b_im', 'm_ssm_c_re', 'm_ssm_c_im', 'm_ssm_d', 'm_ssm_glu_w', 'm_ssm_glu_b', 'm_ssm_w_out', 'm_sg_w_in', 'm_sg_v_g', 'm_sg_w_s', 'm_sg_b_s', 'm_sg_w_out', 'v_ada_w', 'v_ada_b', 'v_norm1_g', 'v_norm2_g', 'v_ff_w1', 'v_ff_w2', 'v_final_g', 'v_conv_w_in', 'v_conv_w', 'v_conv_b', 'v_conv_w_out', 'v_ssm_w_in', 'v_ssm_a_re', 'v_ssm_a_im', 'v_ssm_log_dt', 'v_ssm_b_re', 'v_ssm_b_im', 'v_ssm_c_re', 'v_ssm_c_im', 'v_ssm_d', 'v_ssm_glu_w', 'v_ssm_glu_b', 'v_ssm_w_out', 'v_sg_w_in', 'v_sg_v_g', 'v_sg_w_s', 'v_sg_b_s', 'v_sg_w_out']
TWIN_OUTPUTS = ['loss', 'grad_x', 'grad_ada_w', 'grad_ada_b', 'grad_norm1_g', 'grad_norm2_g', 'grad_ff_w1', 'grad_ff_w2', 'grad_final_g', 'grad_conv_w_in', 'grad_conv_w', 'grad_conv_b', 'grad_conv_w_out', 'grad_ssm_w_in', 'grad_ssm_a_re', 'grad_ssm_a_im', 'grad_ssm_log_dt', 'grad_ssm_b_re', 'grad_ssm_b_im', 'grad_ssm_c_re', 'grad_ssm_c_im', 'grad_ssm_d', 'grad_ssm_glu_w', 'grad_ssm_glu_b', 'grad_ssm_w_out', 'grad_sg_w_in', 'grad_sg_v_g', 'grad_sg_w_s', 'grad_sg_b_s', 'grad_sg_w_out', 'delta_ada_w', 'delta_ada_b', 'delta_norm1_g', 'delta_norm2_g', 'delta_ff_w1', 'delta_ff_w2', 'delta_final_g', 'delta_conv_w_in', 'delta_conv_w', 'delta_conv_b', 'delta_conv_w_out', 'delta_ssm_w_in', 'delta_ssm_a_re', 'delta_ssm_a_im', 'delta_ssm_log_dt', 'delta_ssm_b_re', 'delta_ssm_b_im', 'delta_ssm_c_re', 'delta_ssm_c_im', 'delta_ssm_d', 'delta_ssm_glu_w', 'delta_ssm_glu_b', 'delta_ssm_w_out', 'delta_sg_w_in', 'delta_sg_v_g', 'delta_sg_w_s', 'delta_sg_b_s', 'delta_sg_w_out', 'new_m_ada_w', 'new_m_ada_b', 'new_m_norm1_g', 'new_m_norm2_g', 'new_m_ff_w1', 'new_m_ff_w2', 'new_m_final_g', 'new_m_conv_w_in', 'new_m_conv_w', 'new_m_conv_b', 'new_m_conv_w_out', 'new_m_ssm_w_in', 'new_m_ssm_a_re', 'new_m_ssm_a_im', 'new_m_ssm_log_dt', 'new_m_ssm_b_re', 'new_m_ssm_b_im', 'new_m_ssm_c_re', 'new_m_ssm_c_im', 'new_m_ssm_d', 'new_m_ssm_glu_w', 'new_m_ssm_glu_b', 'new_m_ssm_w_out', 'new_m_sg_w_in', 'new_m_sg_v_g', 'new_m_sg_w_s', 'new_m_sg_b_s', 'new_m_sg_w_out', 'new_v_ada_w', 'new_v_ada_b', 'new_v_norm1_g', 'new_v_norm2_g', 'new_v_ff_w1', 'new_v_ff_w2', 'new_v_final_g', 'new_v_conv_w_in', 'new_v_conv_w', 'new_v_conv_b', 'new_v_conv_w_out', 'new_v_ssm_w_in', 'new_v_ssm_a_re', 'new_v_ssm_a_im', 'new_v_ssm_log_dt', 'new_v_ssm_b_re', 'new_v_ssm_b_im', 'new_v_ssm_c_re', 'new_v_ssm_c_im', 'new_v_ssm_d', 'new_v_ssm_glu_w', 'new_v_ssm_glu_b', 'new_v_ssm_w_out', 'new_v_sg_w_in', 'new_v_sg_v_g', 'new_v_sg_w_s', 'new_v_sg_b_s', 'new_v_sg_w_out']
TWIN_LEAF_KINDS = {'loss': 'loss', 'grad_x': 'grad_x', 'grad_ada_w': 'grad_w', 'grad_ada_b': 'grad_w', 'grad_norm1_g': 'grad_w', 'grad_norm2_g': 'grad_w', 'grad_ff_w1': 'grad_w', 'grad_ff_w2': 'grad_w', 'grad_final_g': 'grad_w', 'grad_conv_w_in': 'grad_w', 'grad_conv_w': 'grad_w', 'grad_conv_b': 'grad_w', 'grad_conv_w_out': 'grad_w', 'grad_ssm_w_in': 'grad_w', 'grad_ssm_a_re': 'grad_w', 'grad_ssm_a_im': 'grad_w', 'grad_ssm_log_dt': 'grad_w', 'grad_ssm_b_re': 'grad_w', 'grad_ssm_b_im': 'grad_w', 'grad_ssm_c_re': 'grad_w', 'grad_ssm_c_im': 'grad_w', 'grad_ssm_d': 'grad_w', 'grad_ssm_glu_w': 'grad_w', 'grad_ssm_glu_b': 'grad_w', 'grad_ssm_w_out': 'grad_w', 'grad_sg_w_in': 'grad_w', 'grad_sg_v_g': 'grad_w', 'grad_sg_w_s': 'grad_w', 'grad_sg_b_s': 'grad_w', 'grad_sg_w_out': 'grad_w', 'delta_ada_w': 'delta_w', 'delta_ada_b': 'delta_w', 'delta_norm1_g': 'delta_w', 'delta_norm2_g': 'delta_w', 'delta_ff_w1': 'delta_w', 'delta_ff_w2': 'delta_w', 'delta_final_g': 'delta_w', 'delta_conv_w_in': 'delta_w', 'delta_conv_w': 'delta_w', 'delta_conv_b': 'delta_w', 'delta_conv_w_out': 'delta_w', 'delta_ssm_w_in': 'delta_w', 'delta_ssm_a_re': 'delta_w', 'delta_ssm_a_im': 'delta_w', 'delta_ssm_log_dt': 'delta_w', 'delta_ssm_b_re': 'delta_w', 'delta_ssm_b_im': 'delta_w', 'delta_ssm_c_re': 'delta_w', 'delta_ssm_c_im': 'delta_w', 'delta_ssm_d': 'delta_w', 'delta_ssm_glu_w': 'delta_w', 'delta_ssm_glu_b': 'delta_w', 'delta_ssm_w_out': 'delta_w', 'delta_sg_w_in': 'delta_w', 'delta_sg_v_g': 'delta_w', 'delta_sg_w_s': 'delta_w', 'delta_sg_b_s': 'delta_w', 'delta_sg_w_out': 'delta_w', 'new_m_ada_w': 'new_m', 'new_m_ada_b': 'new_m', 'new_m_norm1_g': 'new_m', 'new_m_norm2_g': 'new_m', 'new_m_ff_w1': 'new_m', 'new_m_ff_w2': 'new_m', 'new_m_final_g': 'new_m', 'new_m_conv_w_in': 'new_m', 'new_m_conv_w': 'new_m', 'new_m_conv_b': 'new_m', 'new_m_conv_w_out': 'new_m', 'new_m_ssm_w_in': 'new_m', 'new_m_ssm_a_re': 'new_m', 'new_m_ssm_a_im': 'new_m', 'new_m_ssm_log_dt': 'new_m', 'new_m_ssm_b_re': 'new_m', 'new_m_ssm_b_im': 'new_m', 'new_m_ssm_c_re': 'new_m', 'new_m_ssm_c_im': 'new_m', 'new_m_ssm_d': 'new_m', 'new_m_ssm_glu_w': 'new_m', 'new_m_ssm_glu_b': 'new_m', 'new_m_ssm_w_out': 'new_m', 'new_m_sg_w_in': 'new_m', 'new_m_sg_v_g': 'new_m', 'new_m_sg_w_s': 'new_m', 'new_m_sg_b_s': 'new_m', 'new_m_sg_w_out': 'new_m', 'new_v_ada_w': 'new_v', 'new_v_ada_b': 'new_v', 'new_v_norm1_g': 'new_v', 'new_v_norm2_g': 'new_v', 'new_v_ff_w1': 'new_v', 'new_v_ff_w2': 'new_v', 'new_v_final_g': 'new_v', 'new_v_conv_w_in': 'new_v', 'new_v_conv_w': 'new_v', 'new_v_conv_b': 'new_v', 'new_v_conv_w_out': 'new_v', 'new_v_ssm_w_in': 'new_v', 'new_v_ssm_a_re': 'new_v', 'new_v_ssm_a_im': 'new_v', 'new_v_ssm_log_dt': 'new_v', 'new_v_ssm_b_re': 'new_v', 'new_v_ssm_b_im': 'new_v', 'new_v_ssm_c_re': 'new_v', 'new_v_ssm_c_im': 'new_v', 'new_v_ssm_d': 'new_v', 'new_v_ssm_glu_w': 'new_v', 'new_v_ssm_glu_b': 'new_v', 'new_v_ssm_w_out': 'new_v', 'new_v_sg_w_in': 'new_v', 'new_v_sg_v_g': 'new_v', 'new_v_sg_w_s': 'new_v', 'new_v_sg_b_s': 'new_v', 'new_v_sg_w_out': 'new_v'}


def _forward(args):
    return _fwd_reference(*[args[k] for k in FWD_PARAMS])


def _output_shape():
    out = _jax.eval_shape(lambda: _forward(_fwd_setup_inputs(0)))
    return out.shape, out.dtype

N_MICROBATCH = 1
ADAM_LR = 0.001
ADAM_B1 = 0.9
ADAM_B2 = 0.999
ADAM_EPS = 1e-08
ADAM_WD = 0.01
ADAM_STEP = 10
PER_EXAMPLE_BATCH_AXIS = {'x': 0, 'c': 0, 'loss_target': 0}
SHARED_INPUTS = []
_WEIGHT_DTYPES = {'ada_w': _jnp.float32, 'ada_b': _jnp.float32, 'norm1_g': _jnp.float32, 'norm2_g': _jnp.float32, 'ff_w1': _jnp.float32, 'ff_w2': _jnp.float32, 'final_g': _jnp.float32, 'conv_w_in': _jnp.float32, 'conv_w': _jnp.float32, 'conv_b': _jnp.float32, 'conv_w_out': _jnp.float32, 'ssm_w_in': _jnp.float32, 'ssm_a_re': _jnp.float32, 'ssm_a_im': _jnp.float32, 'ssm_log_dt': _jnp.float32, 'ssm_b_re': _jnp.float32, 'ssm_b_im': _jnp.float32, 'ssm_c_re': _jnp.float32, 'ssm_c_im': _jnp.float32, 'ssm_d': _jnp.float32, 'ssm_glu_w': _jnp.float32, 'ssm_glu_b': _jnp.float32, 'ssm_w_out': _jnp.float32, 'sg_w_in': _jnp.float32, 'sg_v_g': _jnp.float32, 'sg_w_s': _jnp.float32, 'sg_b_s': _jnp.float32, 'sg_w_out': _jnp.float32}
MOMENT_SCALE = {'ada_w': 8.662009e-02, 'ada_b': 1.425808e-01, 'norm1_g': 8.487000e-02, 'norm2_g': 7.310706e-02, 'ff_w1': 3.827410e-02, 'ff_w2': 6.901223e-02, 'final_g': 3.252286e+01, 'conv_w_in': 6.489666e-02, 'conv_w': 6.490052e-02, 'conv_b': 5.471934e-02, 'conv_w_out': 6.446942e-02, 'ssm_w_in': 1.023359e-02, 'ssm_a_re': 4.403179e-03, 'ssm_a_im': 3.896311e-03, 'ssm_log_dt': 1.447080e+00, 'ssm_b_re': 1.200285e-03, 'ssm_b_im': 1.274780e-03, 'ssm_c_re': 2.564255e-03, 'ssm_c_im': 2.501648e-03, 'ssm_d': 1.991142e-02, 'ssm_glu_w': 1.775394e-03, 'ssm_glu_b': 5.023228e-03, 'ssm_w_out': 1.021187e-02, 'sg_w_in': 5.001907e-02, 'sg_v_g': 3.556822e-02, 'sg_w_s': 3.515199e-02, 'sg_b_s': 5.138062e-02, 'sg_w_out': 6.137036e-02}


def _to_microbatches(a, axis):
    t = _jnp.moveaxis(a, axis, 0)
    t = t.reshape((N_MICROBATCH, t.shape[0] // N_MICROBATCH) + t.shape[1:])
    return _jnp.moveaxis(t, 1, axis + 1)


def setup_inputs(seed: int = 0) -> dict:
    inp = _fwd_setup_inputs(seed)
    key = _jax.random.fold_in(_jax.random.key(seed), 7919)
    shape, _ = _output_shape()
    out = dict(inp)
    out["loss_target"] = _jax.random.normal(_jax.random.fold_in(key, 0), shape, _jnp.float32)
    for i, name in enumerate(TWIN_WEIGHTS):
        w = inp[name].astype(_jnp.float32)
        if MOMENT_SCALE is None:
            s = _jnp.sqrt(_jnp.mean(_jnp.square(w)) + 1e-30)
        else:
            s = MOMENT_SCALE[name]
        km, kv = _jax.random.split(_jax.random.fold_in(key, i + 1))
        out[name] = w
        out["m_" + name] = s * _jax.random.normal(km, w.shape, _jnp.float32)
        out["v_" + name] = (s * s) * _jax.random.uniform(kv, w.shape, _jnp.float32, 0.5, 1.5)
    if N_MICROBATCH > 1:
        for name, axis in PER_EXAMPLE_BATCH_AXIS.items():
            out[name] = _to_microbatches(out[name], axis)
    return {'x': out['x'], 'c': out['c'], 'ada_w': out['ada_w'], 'ada_b': out['ada_b'], 'norm1_g': out['norm1_g'], 'norm2_g': out['norm2_g'], 'ff_w1': out['ff_w1'], 'ff_w2': out['ff_w2'], 'final_g': out['final_g'], 'conv_w_in': out['conv_w_in'], 'conv_w': out['conv_w'], 'conv_b': out['conv_b'], 'conv_w_out': out['conv_w_out'], 'ssm_w_in': out['ssm_w_in'], 'ssm_a_re': out['ssm_a_re'], 'ssm_a_im': out['ssm_a_im'], 'ssm_log_dt': out['ssm_log_dt'], 'ssm_b_re': out['ssm_b_re'], 'ssm_b_im': out['ssm_b_im'], 'ssm_c_re': out['ssm_c_re'], 'ssm_c_im': out['ssm_c_im'], 'ssm_d': out['ssm_d'], 'ssm_glu_w': out['ssm_glu_w'], 'ssm_glu_b': out['ssm_glu_b'], 'ssm_w_out': out['ssm_w_out'], 'sg_w_in': out['sg_w_in'], 'sg_v_g': out['sg_v_g'], 'sg_w_s': out['sg_w_s'], 'sg_b_s': out['sg_b_s'], 'sg_w_out': out['sg_w_out'], 'loss_target': out['loss_target'], 'm_ada_w': out['m_ada_w'], 'm_ada_b': out['m_ada_b'], 'm_norm1_g': out['m_norm1_g'], 'm_norm2_g': out['m_norm2_g'], 'm_ff_w1': out['m_ff_w1'], 'm_ff_w2': out['m_ff_w2'], 'm_final_g': out['m_final_g'], 'm_conv_w_in': out['m_conv_w_in'], 'm_conv_w': out['m_conv_w'], 'm_conv_b': out['m_conv_b'], 'm_conv_w_out': out['m_conv_w_out'], 'm_ssm_w_in': out['m_ssm_w_in'], 'm_ssm_a_re': out['m_ssm_a_re'], 'm_ssm_a_im': out['m_ssm_a_im'], 'm_ssm_log_dt': out['m_ssm_log_dt'], 'm_ssm_b_re': out['m_ssm_b_re'], 'm_ssm_b_im': out['m_ssm_b_im'], 'm_ssm_c_re': out['m_ssm_c_re'], 'm_ssm_c_im': out['m_ssm_c_im'], 'm_ssm_d': out['m_ssm_d'], 'm_ssm_glu_w': out['m_ssm_glu_w'], 'm_ssm_glu_b': out['m_ssm_glu_b'], 'm_ssm_w_out': out['m_ssm_w_out'], 'm_sg_w_in': out['m_sg_w_in'], 'm_sg_v_g': out['m_sg_v_g'], 'm_sg_w_s': out['m_sg_w_s'], 'm_sg_b_s': out['m_sg_b_s'], 'm_sg_w_out': out['m_sg_w_out'], 'v_ada_w': out['v_ada_w'], 'v_ada_b': out['v_ada_b'], 'v_norm1_g': out['v_norm1_g'], 'v_norm2_g': out['v_norm2_g'], 'v_ff_w1': out['v_ff_w1'], 'v_ff_w2': out['v_ff_w2'], 'v_final_g': out['v_final_g'], 'v_conv_w_in': out['v_conv_w_in'], 'v_conv_w': out['v_conv_w'], 'v_conv_b': out['v_conv_b'], 'v_conv_w_out': out['v_conv_w_out'], 'v_ssm_w_in': out['v_ssm_w_in'], 'v_ssm_a_re': out['v_ssm_a_re'], 'v_ssm_a_im': out['v_ssm_a_im'], 'v_ssm_log_dt': out['v_ssm_log_dt'], 'v_ssm_b_re': out['v_ssm_b_re'], 'v_ssm_b_im': out['v_ssm_b_im'], 'v_ssm_c_re': out['v_ssm_c_re'], 'v_ssm_c_im': out['v_ssm_c_im'], 'v_ssm_d': out['v_ssm_d'], 'v_ssm_glu_w': out['v_ssm_glu_w'], 'v_ssm_glu_b': out['v_ssm_glu_b'], 'v_ssm_w_out': out['v_ssm_w_out'], 'v_sg_w_in': out['v_sg_w_in'], 'v_sg_v_g': out['v_sg_v_g'], 'v_sg_w_s': out['v_sg_w_s'], 'v_sg_b_s': out['v_sg_b_s'], 'v_sg_w_out': out['v_sg_w_out']}


def _loss(weights, diff, rest, loss_target):
    with _jax.named_scope("forward"):
        args = {**rest, TWIN_DIFF_INPUT: diff, **{k: w.astype(_WEIGHT_DTYPES[k]) for k, w in weights.items()}}
        y = _forward(args)
    with _jax.named_scope("loss_head"):
        err = _jnp.square(y.astype(_jnp.float32) - loss_target)
        return 0.5 * _jnp.sum(_jnp.mean(err, axis=-1)) if err.ndim else 0.5 * err


def _adamw(w, g, m, v):
    m = ADAM_B1 * m + (1.0 - ADAM_B1) * g
    v = ADAM_B2 * v + (1.0 - ADAM_B2) * _jnp.square(g)
    m_hat = m / (1.0 - ADAM_B1 ** ADAM_STEP)
    v_hat = v / (1.0 - ADAM_B2 ** ADAM_STEP)
    delta = -ADAM_LR * (m_hat / (_jnp.sqrt(v_hat) + ADAM_EPS) + ADAM_WD * w)
    return delta, m, v


def reference(x, c, ada_w, ada_b, norm1_g, norm2_g, ff_w1, ff_w2, final_g, conv_w_in, conv_w, conv_b, conv_w_out, ssm_w_in, ssm_a_re, ssm_a_im, ssm_log_dt, ssm_b_re, ssm_b_im, ssm_c_re, ssm_c_im, ssm_d, ssm_glu_w, ssm_glu_b, ssm_w_out, sg_w_in, sg_v_g, sg_w_s, sg_b_s, sg_w_out, loss_target, m_ada_w, m_ada_b, m_norm1_g, m_norm2_g, m_ff_w1, m_ff_w2, m_final_g, m_conv_w_in, m_conv_w, m_conv_b, m_conv_w_out, m_ssm_w_in, m_ssm_a_re, m_ssm_a_im, m_ssm_log_dt, m_ssm_b_re, m_ssm_b_im, m_ssm_c_re, m_ssm_c_im, m_ssm_d, m_ssm_glu_w, m_ssm_glu_b, m_ssm_w_out, m_sg_w_in, m_sg_v_g, m_sg_w_s, m_sg_b_s, m_sg_w_out, v_ada_w, v_ada_b, v_norm1_g, v_norm2_g, v_ff_w1, v_ff_w2, v_final_g, v_conv_w_in, v_conv_w, v_conv_b, v_conv_w_out, v_ssm_w_in, v_ssm_a_re, v_ssm_a_im, v_ssm_log_dt, v_ssm_b_re, v_ssm_b_im, v_ssm_c_re, v_ssm_c_im, v_ssm_d, v_ssm_glu_w, v_ssm_glu_b, v_ssm_w_out, v_sg_w_in, v_sg_v_g, v_sg_w_s, v_sg_b_s, v_sg_w_out):
    given = dict(x=x, c=c, ada_w=ada_w, ada_b=ada_b, norm1_g=norm1_g, norm2_g=norm2_g, ff_w1=ff_w1, ff_w2=ff_w2, final_g=final_g, conv_w_in=conv_w_in, conv_w=conv_w, conv_b=conv_b, conv_w_out=conv_w_out, ssm_w_in=ssm_w_in, ssm_a_re=ssm_a_re, ssm_a_im=ssm_a_im, ssm_log_dt=ssm_log_dt, ssm_b_re=ssm_b_re, ssm_b_im=ssm_b_im, ssm_c_re=ssm_c_re, ssm_c_im=ssm_c_im, ssm_d=ssm_d, ssm_glu_w=ssm_glu_w, ssm_glu_b=ssm_glu_b, ssm_w_out=ssm_w_out, sg_w_in=sg_w_in, sg_v_g=sg_v_g, sg_w_s=sg_w_s, sg_b_s=sg_b_s, sg_w_out=sg_w_out, loss_target=loss_target, m_ada_w=m_ada_w, m_ada_b=m_ada_b, m_norm1_g=m_norm1_g, m_norm2_g=m_norm2_g, m_ff_w1=m_ff_w1, m_ff_w2=m_ff_w2, m_final_g=m_final_g, m_conv_w_in=m_conv_w_in, m_conv_w=m_conv_w, m_conv_b=m_conv_b, m_conv_w_out=m_conv_w_out, m_ssm_w_in=m_ssm_w_in, m_ssm_a_re=m_ssm_a_re, m_ssm_a_im=m_ssm_a_im, m_ssm_log_dt=m_ssm_log_dt, m_ssm_b_re=m_ssm_b_re, m_ssm_b_im=m_ssm_b_im, m_ssm_c_re=m_ssm_c_re, m_ssm_c_im=m_ssm_c_im, m_ssm_d=m_ssm_d, m_ssm_glu_w=m_ssm_glu_w, m_ssm_glu_b=m_ssm_glu_b, m_ssm_w_out=m_ssm_w_out, m_sg_w_in=m_sg_w_in, m_sg_v_g=m_sg_v_g, m_sg_w_s=m_sg_w_s, m_sg_b_s=m_sg_b_s, m_sg_w_out=m_sg_w_out, v_ada_w=v_ada_w, v_ada_b=v_ada_b, v_norm1_g=v_norm1_g, v_norm2_g=v_norm2_g, v_ff_w1=v_ff_w1, v_ff_w2=v_ff_w2, v_final_g=v_final_g, v_conv_w_in=v_conv_w_in, v_conv_w=v_conv_w, v_conv_b=v_conv_b, v_conv_w_out=v_conv_w_out, v_ssm_w_in=v_ssm_w_in, v_ssm_a_re=v_ssm_a_re, v_ssm_a_im=v_ssm_a_im, v_ssm_log_dt=v_ssm_log_dt, v_ssm_b_re=v_ssm_b_re, v_ssm_b_im=v_ssm_b_im, v_ssm_c_re=v_ssm_c_re, v_ssm_c_im=v_ssm_c_im, v_ssm_d=v_ssm_d, v_ssm_glu_w=v_ssm_glu_w, v_ssm_glu_b=v_ssm_glu_b, v_ssm_w_out=v_ssm_w_out, v_sg_w_in=v_sg_w_in, v_sg_v_g=v_sg_v_g, v_sg_w_s=v_sg_w_s, v_sg_b_s=v_sg_b_s, v_sg_w_out=v_sg_w_out)
    weights = {n: given[n] for n in TWIN_WEIGHTS}
    shared = {n: given[n] for n in SHARED_INPUTS}
    per_example = {n: given[n] for n in ['x', 'c']}
    grad_fn = _jax.value_and_grad(_loss, argnums=(0, 1))

    def one_microbatch(ex, loss_target):
        ex = dict(ex)
        diff = ex.pop(TWIN_DIFF_INPUT)
        return grad_fn(weights, diff, {**shared, **ex}, loss_target)

    if N_MICROBATCH == 1:
        loss, (grad_w, grad_x) = one_microbatch(per_example, given["loss_target"])
    else:
        def body(carry, xs):
            loss_sum, grad_sum = carry
            l_k, (gw_k, gx_k) = one_microbatch(xs[0], xs[1])
            with _jax.named_scope("update"):
                return (loss_sum + l_k, _jax.tree.map(_jnp.add, grad_sum, gw_k)), gx_k

        init = (_jnp.zeros((), _jnp.float32), _jax.tree.map(_jnp.zeros_like, weights))
        (loss, grad_w), grad_x = _jax.lax.scan(body, init, (per_example, given["loss_target"]))
    with _jax.named_scope("update"):
        delta_w, new_m, new_v = {}, {}, {}
        for n in TWIN_WEIGHTS:
            delta_w[n], new_m[n], new_v[n] = _adamw(weights[n], grad_w[n], given["m_" + n], given["v_" + n])
    return (loss, grad_x, *[grad_w[n] for n in TWIN_WEIGHTS], *[delta_w[n] for n in TWIN_WEIGHTS],
            *[new_m[n] for n in TWIN_WEIGHTS], *[new_v[n] for n in TWIN_WEIGHTS])
```

```python
import math

import jax
import jax.numpy as jnp
from jax import lax
from jax.experimental import pallas as pl
from jax.experimental.pallas import tpu as pltpu

F32 = jnp.float32
BF16 = jnp.bfloat16

N_DEV = 8
MESH_ID = pl.DeviceIdType.MESH
DEPTH = 4
EPS = 1e-6
S5_GROUPS, S5_GROUP, S5_STATE = 64, 16, 64
S5_LANES = S5_GROUPS * S5_STATE
S5_BLOCKS = 8
SG_HEADS, SG_CHUNK = 8, 128
LANE = 128
SUBLANE = 8
VMEM_LIMIT = 48 * 1024 * 1024
ADAM_LR, ADAM_B1, ADAM_B2, ADAM_EPS, ADAM_WD, ADAM_STEP = 0.001, 0.9, 0.999, 1e-08, 0.01, 10
GELU_C = math.sqrt(2.0 / math.pi)
GELU_A = 0.044715

WEIGHTS = ['ada_w', 'ada_b', 'norm1_g', 'norm2_g', 'ff_w1', 'ff_w2', 'final_g', 'conv_w_in', 'conv_w', 'conv_b',
           'conv_w_out', 'ssm_w_in', 'ssm_a_re', 'ssm_a_im', 'ssm_log_dt', 'ssm_b_re', 'ssm_b_im', 'ssm_c_re',
           'ssm_c_im', 'ssm_d', 'ssm_glu_w', 'ssm_glu_b', 'ssm_w_out', 'sg_w_in', 'sg_v_g', 'sg_w_s', 'sg_b_s',
           'sg_w_out']
INPUTS = ['x', 'c'] + WEIGHTS + ['loss_target'] + ['m_' + n for n in WEIGHTS] + ['v_' + n for n in WEIGHTS]
SMALL_REPL = ['ada_b', 'norm1_g', 'norm2_g', 'final_g', 'ssm_a_re', 'ssm_a_im', 'ssm_log_dt', 'ssm_b_re', 'ssm_b_im',
              'ssm_c_re', 'ssm_c_im', 'ssm_d', 'ssm_glu_b', 'sg_w_s', 'sg_b_s']
SMALL_SHARD = ['conv_w', 'conv_b', 'sg_v_g']


def _params(*sem):
    return pltpu.CompilerParams(dimension_semantics=sem or None, vmem_limit_bytes=VMEM_LIMIT)


def _my_pos():
    return lax.axis_index("x"), lax.axis_index("y"), lax.axis_index("c")


def _my_index():
    x, y, c = _my_pos()
    return 4 * x + 2 * y + c


def _mm(a, b, *, name, ta=False, tb=False, out_dtypes=(F32,), epi=None, extras=(), bm=512, bn=1024, bk=1024):
    m, k = (a.shape[1], a.shape[0]) if ta else a.shape
    k2, n = (b.shape[1], b.shape[0]) if tb else b.shape
    assert k == k2, (a.shape, b.shape, ta, tb)
    bm, bn, bk = min(bm, m), min(bn, n), min(bk, k)
    assert m % bm == 0 and n % bn == 0 and k % bk == 0, (m, n, k, bm, bn, bk)
    nk = k // bk
    n_ex, n_out = len(extras), len(out_dtypes)
    dims = (((0 if ta else 1,), (1 if tb else 0,)), ((), ()))

    def body(*refs):
        a_ref, b_ref = refs[0], refs[1]
        ex_refs = refs[2:2 + n_ex]
        out_refs = refs[2 + n_ex:2 + n_ex + n_out]
        acc_ref = refs[-1]
        kk = pl.program_id(2)

        @pl.when(kk == 0)
        def _():
            acc_ref[...] = jnp.zeros_like(acc_ref)

        acc_ref[...] += lax.dot_general(a_ref[...].astype(BF16), b_ref[...].astype(BF16), dims,
                                        preferred_element_type=F32)

        @pl.when(kk == nk - 1)
        def _():
            acc = acc_ref[...]
            outs = epi(acc, *[r[...] for r in ex_refs]) if epi is not None else (acc,)
            for r, o in zip(out_refs, outs):
                r[...] = o.astype(r.dtype)

    a_spec = pl.BlockSpec((bk, bm), lambda i, j, q: (q, i)) if ta else pl.BlockSpec((bm, bk), lambda i, j, q: (i, q))
    b_spec = pl.BlockSpec((bn, bk), lambda i, j, q: (j, q)) if tb else pl.BlockSpec((bk, bn), lambda i, j, q: (q, j))
    ex_specs = []
    for arr, kind in extras:
        if kind == 'mn':
            assert arr.shape == (m, n), (arr.shape, m, n)
            ex_specs.append(pl.BlockSpec((bm, bn), lambda i, j, q: (i, j)))
        else:
            assert arr.shape == (1, n), (arr.shape, n)
            ex_specs.append(pl.BlockSpec((1, bn), lambda i, j, q: (0, j)))
    outs = pl.pallas_call(
        body, name=name,
        out_shape=tuple(jax.ShapeDtypeStruct((m, n), d) for d in out_dtypes),
        grid=(m // bm, n // bn, nk),
        in_specs=[a_spec, b_spec] + ex_specs,
        out_specs=tuple(pl.BlockSpec((bm, bn), lambda i, j, q: (i, j)) for _ in out_dtypes),
        scratch_shapes=[pltpu.VMEM((bm, bn), F32)],
        compiler_params=_params("parallel", "parallel", "arbitrary"),
    )(a, b, *[arr for arr, _ in extras])
    return outs if n_out > 1 else outs[0]


def _epi_residual(acc, res, gate):
    return res + gate * acc, acc


def _rstd(xv):
    return lax.rsqrt(jnp.mean(xv * xv, axis=-1, keepdims=True) + EPS)


def _normmod_fwd(x, w, sh, *, name, tm=512):
    L, D = x.shape

    def body(x_ref, w_ref, s_ref, h_ref):
        xv = x_ref[...]
        h_ref[...] = (xv * _rstd(xv) * w_ref[...] + s_ref[...]).astype(h_ref.dtype)

    row = pl.BlockSpec((tm, D), lambda i: (i, 0))
    vec = pl.BlockSpec((1, D), lambda i: (0, 0))
    return pl.pallas_call(body, name=name, out_shape=jax.ShapeDtypeStruct((L, D), BF16), grid=(L // tm,),
                          in_specs=[row, vec, vec], out_specs=row, compiler_params=_params("parallel"))(x, w, sh)


def _normmod_bwd(dh, x, w, dres, gate, *, name, tm=256):
    L, D = x.shape
    has_gate = gate is not None

    def body(*refs):
        if has_gate:
            dh_ref, x_ref, w_ref, r_ref, y_ref, g_ref, dx_ref, st_ref, dy_ref = refs
        else:
            dh_ref, x_ref, w_ref, r_ref, dx_ref, st_ref = refs
        i = pl.program_id(0)

        @pl.when(i == 0)
        def _():
            st_ref[...] = jnp.zeros_like(st_ref)

        xv = x_ref[...]
        dhv = dh_ref[...].astype(F32)
        rstd = _rstd(xv)
        xn = xv * rstd
        dxn = dhv * w_ref[...]
        dx = rstd * (dxn - xn * jnp.mean(dxn * xn, axis=-1, keepdims=True)) + r_ref[...]
        dx_ref[...] = dx
        st_ref[0:1, :] += jnp.sum(dhv * xn, axis=0, keepdims=True)
        st_ref[1:2, :] += jnp.sum(dhv, axis=0, keepdims=True)
        if has_gate:
            dy_ref[...] = (dx * g_ref[...]).astype(dy_ref.dtype)
            st_ref[2:3, :] += jnp.sum(dx * y_ref[...], axis=0, keepdims=True)

    row = pl.BlockSpec((tm, D), lambda i: (i, 0))
    vec = pl.BlockSpec((1, D), lambda i: (0, 0))
    st = pl.BlockSpec((SUBLANE, D), lambda i: (0, 0))
    in_specs = [row, row, vec, row] + ([row, vec] if has_gate else [])
    out_shape = [jax.ShapeDtypeStruct((L, D), F32), jax.ShapeDtypeStruct((SUBLANE, D), F32)]
    out_specs = [row, st]
    if has_gate:
        out_shape.append(jax.ShapeDtypeStruct((L, D), BF16))
        out_specs.append(row)
    args = (dh, x, w, dres) + (tuple(gate) if has_gate else ())
    return pl.pallas_call(body, name=name, out_shape=tuple(out_shape), grid=(L // tm,), in_specs=in_specs,
                          out_specs=tuple(out_specs), compiler_params=_params("arbitrary"))(*args)


def _loss_head(x, tgt, fg, y, g, *, name, tm=256):
    L, D = x.shape

    def body(x_ref, t_ref, fg_ref, y_ref, g_ref, dx_ref, st_ref, dy_ref, loss_ref):
        i = pl.program_id(0)

        @pl.when(i == 0)
        def _():
            st_ref[...] = jnp.zeros_like(st_ref)
            loss_ref[...] = jnp.zeros_like(loss_ref)

        xv = x_ref[...]
        rstd = _rstd(xv)
        xn = xv * rstd
        err = xn * fg_ref[...] - t_ref[...]
        loss_ref[...] += 0.5 * jnp.sum(jnp.mean(err * err, axis=-1, keepdims=True))
        dout = err * (1.0 / D)
        dxn = dout * fg_ref[...]
        dx = rstd * (dxn - xn * jnp.mean(dxn * xn, axis=-1, keepdims=True))
        dx_ref[...] = dx
        dy_ref[...] = (dx * g_ref[...]).astype(dy_ref.dtype)
        st_ref[0:1, :] += jnp.sum(dout * xn, axis=0, keepdims=True)
        st_ref[2:3, :] += jnp.sum(dx * y_ref[...], axis=0, keepdims=True)

    row = pl.BlockSpec((tm, D), lambda i: (i, 0))
    vec = pl.BlockSpec((1, D), lambda i: (0, 0))
    return pl.pallas_call(
        body, name=name,
        out_shape=(jax.ShapeDtypeStruct((L, D), F32), jax.ShapeDtypeStruct((SUBLANE, D), F32),
                   jax.ShapeDtypeStruct((L, D), BF16), jax.ShapeDtypeStruct((SUBLANE, LANE), F32)),
        grid=(L // tm,), in_specs=[row, row, vec, row, vec],
        out_specs=(row, pl.BlockSpec((SUBLANE, D), lambda i: (0, 0)), row,
                   pl.BlockSpec((SUBLANE, LANE), lambda i: (0, 0))),
        compiler_params=_params("arbitrary"))(x, tgt, fg, y, g)


def _shift_down(v, k):
    row = lax.broadcasted_iota(jnp.int32, v.shape, 0)
    return jnp.where(row >= k, pltpu.roll(v, k, 0), 0.0)


def _shift_up(v, k):
    n = v.shape[0]
    row = lax.broadcasted_iota(jnp.int32, v.shape, 0)
    return jnp.where(row < n - k, pltpu.roll(v, n - k, 0), 0.0)


def _conv_views(L, D):
    return [pl.BlockSpec((L, LANE), lambda j, s=s: (0, s * (D // LANE) + j)) for s in range(3)]


def _conv_fwd(bcx, wb, *, name):
    L, D = bcx.shape[0], bcx.shape[1] // 3

    def body(b_ref, c_ref, x_ref, wb_ref, p_ref):
        z = c_ref[...] * x_ref[...]
        conv = (wb_ref[0:1, :] * _shift_down(z, 2) + wb_ref[1:2, :] * _shift_down(z, 1)
                + wb_ref[2:3, :] * z + wb_ref[3:4, :])
        p_ref[...] = (b_ref[...] * conv).astype(p_ref.dtype)

    col = pl.BlockSpec((L, LANE), lambda j: (0, j))
    return pl.pallas_call(body, name=name, out_shape=jax.ShapeDtypeStruct((L, D), BF16), grid=(D // LANE,),
                          in_specs=_conv_views(L, D) + [pl.BlockSpec((SUBLANE, LANE), lambda j: (0, j))],
                          out_specs=col, compiler_params=_params("parallel"))(bcx, bcx, bcx, wb)


def _conv_bwd(dp, bcx, wb, *, name):
    L, D = dp.shape

    def body(dp_ref, b_ref, c_ref, x_ref, wb_ref, db_ref, dc_ref, dxh_ref, st_ref):
        cv, xv = c_ref[...], x_ref[...]
        z = cv * xv
        z1, z2 = _shift_down(z, 1), _shift_down(z, 2)
        w0, w1, w2 = wb_ref[0:1, :], wb_ref[1:2, :], wb_ref[2:3, :]
        conv = w0 * z2 + w1 * z1 + w2 * z + wb_ref[3:4, :]
        dpv = dp_ref[...]
        db_ref[...] = (dpv * conv).astype(db_ref.dtype)
        dconv = dpv * b_ref[...]
        dz = w2 * dconv + w1 * _shift_up(dconv, 1) + w0 * _shift_up(dconv, 2)
        dc_ref[...] = (dz * xv).astype(dc_ref.dtype)
        dxh_ref[...] = (dz * cv).astype(dxh_ref.dtype)
        st_ref[...] = jnp.zeros_like(st_ref)
        st_ref[0:1, :] = jnp.sum(dconv * z2, axis=0, keepdims=True)
        st_ref[1:2, :] = jnp.sum(dconv * z1, axis=0, keepdims=True)
        st_ref[2:3, :] = jnp.sum(dconv * z, axis=0, keepdims=True)
        st_ref[3:4, :] = jnp.sum(dconv, axis=0, keepdims=True)

    col = pl.BlockSpec((L, LANE), lambda j: (0, j))
    vec = pl.BlockSpec((SUBLANE, LANE), lambda j: (0, j))
    act = jax.ShapeDtypeStruct((L, D), BF16)
    return pl.pallas_call(body, name=name, out_shape=(act, act, act, jax.ShapeDtypeStruct((SUBLANE, D), F32)),
                          grid=(D // LANE,), in_specs=[col] + _conv_views(L, D) + [vec],
                          out_specs=(col, col, col, vec), compiler_params=_params("parallel"))(dp, bcx, bcx, bcx, wb)


def _sg_fwd(uv, vg, ws, bsb, *, name, tr=512):
    L, D = uv.shape[0], uv.shape[1] // 2

    def body(uv_ref, vg_ref, ws_ref, bsb_ref, p_ref):
        for ci in range(tr // SG_CHUNK):
            rows = slice(ci * SG_CHUNK, (ci + 1) * SG_CHUNK)
            v = uv_ref[rows, D:2 * D]
            vn = (v * _rstd(v) * vg_ref[...]).astype(BF16)
            for h in range(SG_HEADS):
                cols = slice(h * LANE, (h + 1) * LANE)
                vm = jnp.dot(ws_ref[h], vn[:, cols], preferred_element_type=F32) + bsb_ref[h]
                p_ref[rows, cols] = (uv_ref[rows, cols] * vm).astype(p_ref.dtype)

    full3 = pl.BlockSpec((SG_HEADS, SG_CHUNK, LANE), lambda i: (0, 0, 0))
    return pl.pallas_call(body, name=name, out_shape=jax.ShapeDtypeStruct((L, D), BF16), grid=(L // tr,),
                          in_specs=[pl.BlockSpec((tr, 2 * D), lambda i: (i, 0)), pl.BlockSpec((1, D), lambda i: (0, 0)),
                                    full3, full3],
                          out_specs=pl.BlockSpec((tr, D), lambda i: (i, 0)),
                          compiler_params=_params("parallel"))(uv, vg, ws, bsb)


def _sg_bwd(dp, uv, vg, ws, wst, bsb, *, name, tr=512):
    L, D = dp.shape

    def body(dp_ref, uv_ref, vg_ref, ws_ref, wst_ref, bsb_ref, duv_ref, dws_ref, dbs_ref, st_ref, dvn_ref):
        i = pl.program_id(0)

        @pl.when(i == 0)
        def _():
            dws_ref[...] = jnp.zeros_like(dws_ref)
            dbs_ref[...] = jnp.zeros_like(dbs_ref)
            st_ref[...] = jnp.zeros_like(st_ref)

        for ci in range(tr // SG_CHUNK):
            rows = slice(ci * SG_CHUNK, (ci + 1) * SG_CHUNK)
            v = uv_ref[rows, D:2 * D]
            rstd = _rstd(v)
            vhat = v * rstd
            vn = (vhat * vg_ref[...]).astype(BF16)
            for h in range(SG_HEADS):
                cols = slice(h * LANE, (h + 1) * LANE)
                vm = jnp.dot(ws_ref[h], vn[:, cols], preferred_element_type=F32) + bsb_ref[h]
                dph = dp_ref[rows, cols]
                duv_ref[rows, cols] = (dph * vm).astype(duv_ref.dtype)
                dvm = dph * uv_ref[rows, cols]
                dbs_ref[h] += dvm
                dvmb = dvm.astype(BF16)
                dws_ref[h] += lax.dot_general(dvmb, vn[:, cols], (((1,), (1,)), ((), ())),
                                              preferred_element_type=F32)
                dvn_ref[rows, cols] = jnp.dot(wst_ref[h], dvmb, preferred_element_type=F32)
            dvn = dvn_ref[rows, :]
            gv = dvn * vg_ref[...]
            dv = rstd * (gv - vhat * jnp.mean(gv * vhat, axis=-1, keepdims=True))
            duv_ref[rows, D:2 * D] = dv.astype(duv_ref.dtype)
            st_ref[0:1, :] += jnp.sum(dvn * vhat, axis=0, keepdims=True)

    full3 = pl.BlockSpec((SG_HEADS, SG_CHUNK, LANE), lambda i: (0, 0, 0))
    acc3 = jax.ShapeDtypeStruct((SG_HEADS, SG_CHUNK, LANE), F32)
    return pl.pallas_call(
        body, name=name,
        out_shape=(jax.ShapeDtypeStruct((L, 2 * D), BF16), acc3, acc3, jax.ShapeDtypeStruct((SUBLANE, D), F32)),
        grid=(L // tr,),
        in_specs=[pl.BlockSpec((tr, D), lambda i: (i, 0)), pl.BlockSpec((tr, 2 * D), lambda i: (i, 0)),
                  pl.BlockSpec((1, D), lambda i: (0, 0)), full3, full3, full3],
        out_specs=(pl.BlockSpec((tr, 2 * D), lambda i: (i, 0)), full3, full3,
                   pl.BlockSpec((SUBLANE, D), lambda i: (0, 0))),
        scratch_shapes=[pltpu.VMEM((tr, D), F32)],
        compiler_params=_params("arbitrary"))(dp, uv, vg, ws, wst, bsb)


def _gelu(x):
    return 0.5 * x * (1.0 + jnp.tanh(GELU_C * (x + GELU_A * x * x * x)))


def _gelu_grad(x):
    th = jnp.tanh(GELU_C * (x + GELU_A * x * x * x))
    return 0.5 * (1.0 + th) + 0.5 * x * (1.0 - th * th) * GELU_C * (1.0 + 3.0 * GELU_A * x * x)


def _cmul_add(xr, xi, ar, ai, br, bi):
    return xr + ar * br - ai * bi, xi + ar * bi + ai * br


def _s5_fwd(u, bre, bim, cre, cim, pw, dsk, *, name, tc=512):
    L, D = u.shape
    W = S5_LANES // S5_BLOCKS
    nt = L // tc

    def body(u_ref, bre_ref, bim_ref, cre_ref, cim_ref, pw_ref, d_ref, sre_ref, sim_ref, ypre_ref, yg_ref, carry):
        t = pl.program_id(1)

        @pl.when(t == 0)
        def _():
            carry[...] = jnp.zeros_like(carry)

        uv = u_ref[...]
        ub = uv.astype(BF16)
        sre_ref[...] = jnp.dot(ub, bre_ref[...], preferred_element_type=F32)
        sim_ref[...] = jnp.dot(ub, bim_ref[...], preferred_element_type=F32)

        def tile(i, c):
            cr, ci = c
            rows = pl.ds(pl.multiple_of(i * SUBLANE, SUBLANE), SUBLANE)
            xr, xi = sre_ref[rows, :], sim_ref[rows, :]
            for k, d in enumerate((1, 2, 4)):
                xr, xi = _cmul_add(xr, xi, pw_ref[2 * k], pw_ref[2 * k + 1], pltpu.roll(xr, d, 0),
                                   pltpu.roll(xi, d, 0))
            xr, xi = _cmul_add(xr, xi, pw_ref[6], pw_ref[7], cr, ci)
            sre_ref[rows, :] = xr
            sim_ref[rows, :] = xi
            last = slice(SUBLANE - 1, SUBLANE)
            return jnp.broadcast_to(xr[last, :], (SUBLANE, W)), jnp.broadcast_to(xi[last, :], (SUBLANE, W))

        cr, ci = lax.fori_loop(0, tc // SUBLANE, tile, (carry[0], carry[1]))
        carry[0] = cr
        carry[1] = ci
        y = (jnp.dot(sre_ref[...].astype(BF16), cre_ref[...], preferred_element_type=F32)
             - jnp.dot(sim_ref[...].astype(BF16), cim_ref[...], preferred_element_type=F32) + d_ref[...] * uv)
        ypre_ref[...] = y
        yg_ref[...] = _gelu(y)

    ch = pl.BlockSpec((tc, LANE), lambda j, t: (t, j))
    st = pl.BlockSpec((tc, W), lambda j, t: (t, j))
    bsp = pl.BlockSpec((None, LANE, W), lambda j, t: (j, 0, 0))
    csp = pl.BlockSpec((None, W, LANE), lambda j, t: (j, 0, 0))
    return pl.pallas_call(
        body, name=name,
        out_shape=(jax.ShapeDtypeStruct((L, S5_LANES), F32), jax.ShapeDtypeStruct((L, S5_LANES), F32),
                   jax.ShapeDtypeStruct((L, D), F32), jax.ShapeDtypeStruct((L, D), F32)),
        grid=(S5_BLOCKS, nt),
        in_specs=[ch, bsp, bsp, csp, csp, pl.BlockSpec((8, SUBLANE, W), lambda j, t: (0, 0, j)),
                  pl.BlockSpec((1, LANE), lambda j, t: (0, j))],
        out_specs=(st, st, ch, ch),
        scratch_shapes=[pltpu.VMEM((2, SUBLANE, W), F32)],
        compiler_params=_params("parallel", "arbitrary"))(u, bre, bim, cre, cim, pw, dsk)


def _s5_bwd(dy, u, sre, sim, bre, bim, cre, cim, pwr, dsk, *, name, tc=512):
    L, D = u.shape
    W = S5_LANES // S5_BLOCKS
    nt = L // tc
    ntile = tc // SUBLANE
    nt_dims = (((1,), (1,)), ((), ()))
    tn_dims = (((0,), (0,)), ((), ()))

    def body(dy_ref, u_ref, sre_ref, sim_ref, bre_ref, bim_ref, cre_ref, cim_ref, pw_ref, d_ref,
             du_ref, dbre_ref, dbim_ref, dcre_ref, dcim_ref, ga_ref, dd_ref, gre, gim, carry, gacc):
        t = pl.program_id(1)

        @pl.when(t == 0)
        def _():
            for r in (carry, gacc, dbre_ref, dbim_ref, dcre_ref, dcim_ref, ga_ref, dd_ref):
                r[...] = jnp.zeros_like(r)

        dyv, uv = dy_ref[...], u_ref[...]
        dyb, ub = dyv.astype(BF16), uv.astype(BF16)
        gre[...] = lax.dot_general(dyb, cre_ref[...], nt_dims, preferred_element_type=F32)
        gim[...] = -lax.dot_general(dyb, cim_ref[...], nt_dims, preferred_element_type=F32)
        top = lax.broadcasted_iota(jnp.int32, (SUBLANE, W), 0) == SUBLANE - 1

        def tile(k, c):
            cr, ci = c
            rows = pl.ds(pl.multiple_of((ntile - 1 - k) * SUBLANE, SUBLANE), SUBLANE)
            xr, xi = gre[rows, :], gim[rows, :]
            for q, d in enumerate((1, 2, 4)):
                xr, xi = _cmul_add(xr, xi, pw_ref[2 * q], pw_ref[2 * q + 1], pltpu.roll(xr, SUBLANE - d, 0),
                                   pltpu.roll(xi, SUBLANE - d, 0))
            xr, xi = _cmul_add(xr, xi, pw_ref[6], pw_ref[7], cr, ci)
            gre[rows, :] = xr
            gim[rows, :] = xi
            nr = jnp.where(top, cr, pltpu.roll(xr, SUBLANE - 1, 0))
            ni = jnp.where(top, ci, pltpu.roll(xi, SUBLANE - 1, 0))
            sr, si = sre_ref[rows, :], sim_ref[rows, :]
            gacc[0] += sr * nr + si * ni
            gacc[1] += sr * ni - si * nr
            return jnp.broadcast_to(xr[0:1, :], (SUBLANE, W)), jnp.broadcast_to(xi[0:1, :], (SUBLANE, W))

        cr, ci = lax.fori_loop(0, ntile, tile, (carry[0], carry[1]))
        carry[0] = cr
        carry[1] = ci
        grb, gib = gre[...].astype(BF16), gim[...].astype(BF16)
        du = (lax.dot_general(grb, bre_ref[...], nt_dims, preferred_element_type=F32)
              + lax.dot_general(gib, bim_ref[...], nt_dims, preferred_element_type=F32) + d_ref[...] * dyv)
        du_ref[...] = du.astype(du_ref.dtype)
        dbre_ref[...] += lax.dot_general(ub, grb, tn_dims, preferred_element_type=F32)
        dbim_ref[...] += lax.dot_general(ub, gib, tn_dims, preferred_element_type=F32)
        dcre_ref[...] += lax.dot_general(sre_ref[...].astype(BF16), dyb, tn_dims, preferred_element_type=F32)
        dcim_ref[...] -= lax.dot_general(sim_ref[...].astype(BF16), dyb, tn_dims, preferred_element_type=F32)
        dd_ref[0:1, :] += jnp.sum(dyv * uv, axis=0, keepdims=True)

        @pl.when(t == nt - 1)
        def _():
            ga_ref[0:1, :] = jnp.sum(gacc[0], axis=0, keepdims=True)
            ga_ref[1:2, :] = jnp.sum(gacc[1], axis=0, keepdims=True)

    ch = pl.BlockSpec((tc, LANE), lambda j, t: (nt - 1 - t, j))
    st = pl.BlockSpec((tc, W), lambda j, t: (nt - 1 - t, j))
    bsp = pl.BlockSpec((None, LANE, W), lambda j, t: (j, 0, 0))
    csp = pl.BlockSpec((None, W, LANE), lambda j, t: (j, 0, 0))
    return pl.pallas_call(
        body, name=name,
        out_shape=(jax.ShapeDtypeStruct((L, D), BF16),
                   jax.ShapeDtypeStruct((S5_BLOCKS, LANE, W), F32), jax.ShapeDtypeStruct((S5_BLOCKS, LANE, W), F32),
                   jax.ShapeDtypeStruct((S5_BLOCKS, W, LANE), F32), jax.ShapeDtypeStruct((S5_BLOCKS, W, LANE), F32),
                   jax.ShapeDtypeStruct((SUBLANE, S5_LANES), F32), jax.ShapeDtypeStruct((SUBLANE, D), F32)),
        grid=(S5_BLOCKS, nt),
        in_specs=[ch, ch, st, st, bsp, bsp, csp, csp, pl.BlockSpec((8, SUBLANE, W), lambda j, t: (0, 0, j)),
                  pl.BlockSpec((1, LANE), lambda j, t: (0, j))],
        out_specs=(ch, bsp, bsp, csp, csp, pl.BlockSpec((SUBLANE, W), lambda j, t: (0, j)),
                   pl.BlockSpec((SUBLANE, LANE), lambda j, t: (0, j))),
        scratch_shapes=[pltpu.VMEM((tc, W), F32), pltpu.VMEM((tc, W), F32), pltpu.VMEM((2, SUBLANE, W), F32),
                        pltpu.VMEM((2, SUBLANE, W), F32)],
        compiler_params=_params("parallel", "arbitrary"))(dy, u, sre, sim, bre, bim, cre, cim, pwr, dsk)


def _glu_bwd(dy2, y, t, *, name, tm=256):
    L, D = y.shape

    def body(dy2_ref, y_ref, t_ref, dt_ref, dya_ref, st_ref):
        i = pl.program_id(0)

        @pl.when(i == 0)
        def _():
            st_ref[...] = jnp.zeros_like(st_ref)

        sig = 1.0 / (1.0 + jnp.exp(-t_ref[...]))
        dy2v = dy2_ref[...]
        dt = dy2v * y_ref[...] * sig * (1.0 - sig)
        dt_ref[...] = dt.astype(dt_ref.dtype)
        dya_ref[...] = dy2v * sig
        st_ref[0:1, :] += jnp.sum(dt, axis=0, keepdims=True)

    row = pl.BlockSpec((tm, D), lambda i: (i, 0))
    return pl.pallas_call(
        body, name=name,
        out_shape=(jax.ShapeDtypeStruct((L, D), BF16), jax.ShapeDtypeStruct((L, D), F32),
                   jax.ShapeDtypeStruct((SUBLANE, D), F32)),
        grid=(L // tm,), in_specs=[row, row, row],
        out_specs=(row, row, pl.BlockSpec((SUBLANE, D), lambda i: (0, 0))),
        compiler_params=_params("arbitrary"))(dy2, y, t)


def _s5_prep(a_re, a_im, log_dt, b_re, b_im, c_re, c_im):
    dt = jnp.exp(log_dt)[:, None]
    mag = jnp.exp(a_re * dt)
    abar_re = mag * jnp.cos(a_im * dt)
    abar_im = mag * jnp.sin(a_im * dt)
    den = a_re * a_re + a_im * a_im
    nr = abar_re - 1.0
    ni = abar_im
    f_re = ((nr * a_re + ni * a_im) / den)[..., None]
    f_im = ((ni * a_re - nr * a_im) / den)[..., None]
    bbar_re = f_re * b_re - f_im * b_im
    bbar_im = f_re * b_im + f_im * b_re
    eye = jnp.eye(S5_GROUPS // S5_BLOCKS, dtype=F32)
    gb = S5_GROUPS // S5_BLOCKS

    def blk_b(bb):
        t = bb.reshape(S5_BLOCKS, gb, S5_STATE, S5_GROUP)
        return jnp.einsum('jgph,gk->jghkp', t, eye).reshape(S5_BLOCKS, gb * S5_GROUP, gb * S5_STATE)

    def blk_c(cc):
        t = cc.reshape(S5_BLOCKS, gb, S5_GROUP, S5_STATE)
        return jnp.einsum('jghp,gk->jgpkh', t, eye).reshape(S5_BLOCKS, gb * S5_STATE, gb * S5_GROUP)

    return (abar_re.reshape(1, S5_LANES), abar_im.reshape(1, S5_LANES), blk_b(bbar_re), blk_b(bbar_im),
            blk_c(c_re), blk_c(c_im))


def _s5_power_tables(ar, ai):
    pr, pi = [jnp.ones_like(ar)], [jnp.zeros_like(ai)]
    for _ in range(SUBLANE):
        pr, pi = pr + [pr[-1] * ar - pi[-1] * ai], pi + [pr[-1] * ai + pi[-1] * ar]
    row = jnp.arange(SUBLANE)[:, None]

    def tables(sign, keep, carry_pow):
        out = []
        for d in (1, 2, 4):
            out += [jnp.where(keep(d), pr[d], 0.0), jnp.where(keep(d), sign * pi[d], 0.0)]
        out += [jnp.concatenate([pr[p] for p in carry_pow], 0), sign * jnp.concatenate([pi[p] for p in carry_pow], 0)]
        return jnp.stack([jnp.broadcast_to(o, (SUBLANE, ar.shape[1])) for o in out])

    fwd = tables(1.0, lambda d: row >= d, [r + 1 for r in range(SUBLANE)])
    rev = tables(-1.0, lambda d: row + d <= SUBLANE - 1, [SUBLANE - r for r in range(SUBLANE)])
    return fwd, rev


def _adamw(w, parts, m, v, *, name, tr=256):
    R, C = w.shape
    P = parts.shape[0]
    tr = min(tr, R)
    assert R % tr == 0, (R, tr)
    c1 = 1.0 / (1.0 - ADAM_B1 ** ADAM_STEP)
    c2 = 1.0 / (1.0 - ADAM_B2 ** ADAM_STEP)

    def body(w_ref, p_ref, m_ref, v_ref, g_ref, d_ref, nm_ref, nv_ref):
        g = p_ref[0].astype(F32)
        for q in range(1, P):
            g = g + p_ref[q].astype(F32)
        nm = ADAM_B1 * m_ref[...] + (1.0 - ADAM_B1) * g
        nv = ADAM_B2 * v_ref[...] + (1.0 - ADAM_B2) * (g * g)
        g_ref[...] = g
        nm_ref[...] = nm
        nv_ref[...] = nv
        d_ref[...] = -ADAM_LR * ((nm * c1) / (jnp.sqrt(nv * c2) + ADAM_EPS) + ADAM_WD * w_ref[...])

    row = pl.BlockSpec((tr, C), lambda i: (i, 0))
    out = jax.ShapeDtypeStruct((R, C), F32)
    return pl.pallas_call(body, name=name, out_shape=(out, out, out, out), grid=(R // tr,),
                          in_specs=[row, pl.BlockSpec((P, tr, C), lambda i: (0, i, 0)), row, row],
                          out_specs=(row, row, row, row), compiler_params=_params("parallel"))(w, parts, m, v)


def _all_gather(xs, axis, *, name):
    m = xs.shape[axis]
    out_shape = list(xs.shape)
    out_shape[axis] = N_DEV * m

    def body(x_ref, out_ref, send_sems, recv_sems, local_sem):
        x, y, c = _my_pos()
        me, sibling = (x, y, c), (x, y, 1 - c)
        chips = [(1 - x, y), (x, 1 - y), (1 - x, 1 - y)]

        def blk(px, py, pc):
            idx = [slice(None)] * 3
            idx[axis] = pl.ds((4 * px + 2 * py + pc) * m, m)
            return out_ref.at[tuple(idx)]

        def copy(k, block, to, src=None):
            return pltpu.make_async_remote_copy(src_ref=blk(*block) if src is None else src, dst_ref=blk(*block),
                                                send_sem=send_sems.at[k], recv_sem=recv_sems.at[k],
                                                device_id=to, device_id_type=MESH_ID)

        mine = pltpu.make_async_copy(x_ref, blk(*me), local_sem)
        mine.start()
        first = [copy(0, me, sibling, src=x_ref)]
        first += [copy(1 + j, me, (*chip, c), src=x_ref) for j, chip in enumerate(chips)]
        for cp in first:
            cp.start()
        passed = [copy(4 + j, (*chip, c), sibling) for j, chip in enumerate(chips)]
        for j, chip in enumerate(chips):
            copy(1 + j, (*chip, c), me).wait_recv()
            passed[j].start()
        copy(0, sibling, me).wait_recv()
        for j, chip in enumerate(chips):
            copy(4 + j, (*chip, 1 - c), me).wait_recv()
        for cp in first + passed:
            cp.wait_send()
        mine.wait()

    hbm = pl.BlockSpec(memory_space=pl.ANY)
    return pl.pallas_call(body, name=name, out_shape=jax.ShapeDtypeStruct(tuple(out_shape), xs.dtype),
                          in_specs=[hbm], out_specs=hbm,
                          scratch_shapes=[pltpu.SemaphoreType.DMA((N_DEV - 1,)), pltpu.SemaphoreType.DMA((N_DEV - 1,)),
                                          pltpu.SemaphoreType.DMA],
                          compiler_params=pltpu.CompilerParams(has_side_effects=True))(xs)


def _grad_exchange(gs, axis, *, name):
    n = len(gs)
    R, C = gs[0].shape
    m = (R if axis == 0 else C) // N_DEV
    shard = (m, C) if axis == 0 else (R, m)

    def body(*refs):
        g_refs, out_ref = refs[:n], refs[n]
        send_sems, recv_sems, local_sems = refs[n + 1:]
        x, y, c = _my_pos()
        me = 4 * x + 2 * y + c

        def src(l, dev):
            return g_refs[l].at[pl.ds(dev * m, m), :] if axis == 0 else g_refs[l].at[:, pl.ds(dev * m, m)]

        local = [pltpu.make_async_copy(src(l, me), out_ref.at[me, l], local_sems.at[l]) for l in range(n)]
        for cp in local:
            cp.start()
        sent = []
        for r in range(1, N_DEV):
            px = 1 - x if r & 4 else x
            py = 1 - y if r & 2 else y
            pc = 1 - c if r & 1 else c
            peer = 4 * px + 2 * py + pc
            for l in range(n):
                k = l * (N_DEV - 1) + r - 1
                cp = pltpu.make_async_remote_copy(src_ref=src(l, peer), dst_ref=out_ref.at[me, l],
                                                  send_sem=send_sems.at[k], recv_sem=recv_sems.at[k],
                                                  device_id=(px, py, pc), device_id_type=MESH_ID)
                cp.start()
                sent.append((cp, pltpu.make_async_remote_copy(
                    src_ref=src(l, peer), dst_ref=out_ref.at[peer, l], send_sem=send_sems.at[k],
                    recv_sem=recv_sems.at[k], device_id=(px, py, pc), device_id_type=MESH_ID)))
        for cp, landing in sent:
            landing.wait_recv()
        for cp, landing in sent:
            cp.wait_send()
        for cp in local:
            cp.wait()

    hbm = pl.BlockSpec(memory_space=pl.ANY)
    nsem = n * (N_DEV - 1)
    return pl.pallas_call(body, name=name, out_shape=jax.ShapeDtypeStruct((N_DEV, n) + shard, gs[0].dtype),
                          in_specs=[hbm] * n, out_specs=hbm,
                          scratch_shapes=[pltpu.SemaphoreType.DMA((nsem,)), pltpu.SemaphoreType.DMA((nsem,)),
                                          pltpu.SemaphoreType.DMA((n,))],
                          compiler_params=pltpu.CompilerParams(has_side_effects=True))(*gs)


def _pad_rows(a, rows):
    return jnp.pad(a, ((0, rows - a.shape[0]), (0, 0)))


def _stat_row(st, r):
    return st[r:r + 1, :]


def kernel(x, c, ada_w, ada_b, norm1_g, norm2_g, ff_w1, ff_w2, final_g, conv_w_in, conv_w, conv_b, conv_w_out, ssm_w_in, ssm_a_re, ssm_a_im, ssm_log_dt, ssm_b_re, ssm_b_im, ssm_c_re, ssm_c_im, ssm_d, ssm_glu_w, ssm_glu_b, ssm_w_out, sg_w_in, sg_v_g, sg_w_s, sg_b_s, sg_w_out, loss_target, m_ada_w, m_ada_b, m_norm1_g, m_norm2_g, m_ff_w1, m_ff_w2, m_final_g, m_conv_w_in, m_conv_w, m_conv_b, m_conv_w_out, m_ssm_w_in, m_ssm_a_re, m_ssm_a_im, m_ssm_log_dt, m_ssm_b_re, m_ssm_b_im, m_ssm_c_re, m_ssm_c_im, m_ssm_d, m_ssm_glu_w, m_ssm_glu_b, m_ssm_w_out, m_sg_w_in, m_sg_v_g, m_sg_w_s, m_sg_b_s, m_sg_w_out, v_ada_w, v_ada_b, v_norm1_g, v_norm2_g, v_ff_w1, v_ff_w2, v_final_g, v_conv_w_in, v_conv_w, v_conv_b, v_conv_w_out, v_ssm_w_in, v_ssm_a_re, v_ssm_a_im, v_ssm_log_dt, v_ssm_b_re, v_ssm_b_im, v_ssm_c_re, v_ssm_c_im, v_ssm_d, v_ssm_glu_w, v_ssm_glu_b, v_ssm_w_out, v_sg_w_in, v_sg_v_g, v_sg_w_s, v_sg_b_s, v_sg_w_out):
    P = dict(zip(INPUTS, (x, c, ada_w, ada_b, norm1_g, norm2_g, ff_w1, ff_w2, final_g, conv_w_in, conv_w, conv_b, conv_w_out, ssm_w_in, ssm_a_re, ssm_a_im, ssm_log_dt, ssm_b_re, ssm_b_im, ssm_c_re, ssm_c_im, ssm_d, ssm_glu_w, ssm_glu_b, ssm_w_out, sg_w_in, sg_v_g, sg_w_s, sg_b_s, sg_w_out, loss_target, m_ada_w, m_ada_b, m_norm1_g, m_norm2_g, m_ff_w1, m_ff_w2, m_final_g, m_conv_w_in, m_conv_w, m_conv_b, m_conv_w_out, m_ssm_w_in, m_ssm_a_re, m_ssm_a_im, m_ssm_log_dt, m_ssm_b_re, m_ssm_b_im, m_ssm_c_re, m_ssm_c_im, m_ssm_d, m_ssm_glu_w, m_ssm_glu_b, m_ssm_w_out, m_sg_w_in, m_sg_v_g, m_sg_w_s, m_sg_b_s, m_sg_w_out, v_ada_w, v_ada_b, v_norm1_g, v_norm2_g, v_ff_w1, v_ff_w2, v_final_g, v_conv_w_in, v_conv_w, v_conv_b, v_conv_w_out, v_ssm_w_in, v_ssm_a_re, v_ssm_a_im, v_ssm_log_dt, v_ssm_b_re, v_ssm_b_im, v_ssm_c_re, v_ssm_c_im, v_ssm_d, v_ssm_glu_w, v_ssm_glu_b, v_ssm_w_out, v_sg_w_in, v_sg_v_g, v_sg_w_s, v_sg_b_s, v_sg_w_out)))
    L, D = x.shape[1], x.shape[2]
    me = _my_index()
    xs = x[0]
    tgt = loss_target[0]
    n_conv = conv_w_in.shape[0]

    c_act = c * (1.0 / (1.0 + jnp.exp(-c)))
    vec_rows = jnp.concatenate([c_act.reshape(D // LANE, LANE), conv_w.reshape(-1, LANE), conv_b.reshape(-1, LANE),
                                sg_v_g.reshape(-1, LANE)], 0)
    n_vec = vec_rows.shape[0]
    vec_all = _all_gather(_pad_rows(vec_rows, 24)[None], 0, name="gather_vectors")
    c_all = vec_all[:, :D // LANE, :].reshape(N_DEV, D)
    sharded_full = vec_all[:, D // LANE:n_vec, :].transpose(1, 0, 2).reshape(n_vec - D // LANE, D)
    conv_w_full = sharded_full[:3 * n_conv].reshape(n_conv, 3, D)
    conv_b_full = sharded_full[3 * n_conv:4 * n_conv]
    sg_vg_full = sharded_full[4 * n_conv:4 * n_conv + 1]

    c_pad = _pad_rows(c_all, LANE)
    ncol = ada_w.shape[2]
    mod_part = jnp.stack([_mm(c_pad, ada_w[i], name=f"ada_fwd{i}")[:N_DEV] for i in range(DEPTH)])
    mod_all = _all_gather(mod_part.reshape(1, DEPTH * N_DEV, ncol), 0, name="gather_mod")
    mod_all = mod_all.reshape(N_DEV, DEPTH, N_DEV, ncol)
    mod_me = lax.dynamic_index_in_dim(mod_all, me, 2, keepdims=False)
    mod = mod_me.transpose(1, 0, 2).reshape(DEPTH, N_DEV * ncol) + ada_b

    w1_full = _all_gather(ff_w1.astype(BF16), 2, name="gather_ff_w1")
    w2_full = _all_gather(ff_w2.astype(BF16), 1, name="gather_ff_w2")
    cwi_full = _all_gather(conv_w_in.astype(BF16), 2, name="gather_conv_w_in")
    sgi_full = _all_gather(sg_w_in.astype(BF16), 2, name="gather_sg_w_in")
    row_names = ['conv_w_out', 'ssm_w_in', 'ssm_glu_w', 'ssm_w_out', 'sg_w_out']
    row_pack = jnp.concatenate([P[n] for n in row_names], 0)
    row_full = _all_gather(row_pack.astype(BF16), 1, name="gather_row_sharded")
    cwo_full = row_full[:n_conv]
    ssm_in_full, glu_full, ssm_out_full, sgo_full = (row_full[n_conv + q] for q in range(4))

    s5_args = (ssm_a_re[0], ssm_a_im[0], ssm_log_dt[0], ssm_b_re[0], ssm_b_im[0], ssm_c_re[0], ssm_c_im[0])
    (abar_re, abar_im, bblk_re, bblk_im, cblk_re, cblk_im), s5_vjp = jax.vjp(_s5_prep, *s5_args)
    pw_fwd, pw_rev = _s5_power_tables(abar_re, abar_im)
    s5_w = tuple(t.astype(BF16) for t in (bblk_re, bblk_im, cblk_re, cblk_im))
    causal = jnp.tril(jnp.ones((SG_CHUNK, SG_CHUNK), dtype=bool))
    ws_m = jnp.where(causal[None], sg_w_s[0], 0.0)
    ws_b = ws_m.astype(BF16)
    wst_b = ws_m.transpose(0, 2, 1).astype(BF16)
    bsb = jnp.broadcast_to(sg_b_s[0][:, :, None], (SG_HEADS, SG_CHUNK, LANE))

    saved = []
    xa = xs
    for i in range(DEPTH):
        kind, j = i % 3, i // 3
        sh1, sc1, g1, sh2, sc2, g2 = (mod[i:i + 1, q * D:(q + 1) * D] for q in range(6))
        wn1 = norm1_g[i:i + 1] * (1.0 + sc1)
        wn2 = norm2_g[i:i + 1] * (1.0 + sc2)
        S = dict(x_in=xa, g1=g1, g2=g2, sc1=sc1, sc2=sc2, wn1=wn1, wn2=wn2)
        h1 = _normmod_fwd(xa, wn1, sh1, name=f"norm1_fwd{i}")
        S['h1'] = h1
        if kind == 0:
            bcx = _mm(h1, cwi_full[j], name=f"conv_in{i}")
            wb = _pad_rows(jnp.concatenate([conv_w_full[j], conv_b_full[j:j + 1]], 0), SUBLANE)
            pb = _conv_fwd(bcx, wb, name=f"conv_mix{i}")
            S.update(bcx=bcx, wb=wb, pb=pb)
            w_out = cwo_full[j]
        elif kind == 1:
            u = _mm(h1, ssm_in_full, name=f"ssm_in{i}")
            sre, sim, ypre, yg = _s5_fwd(u, *s5_w, pw_fwd, ssm_d, name=f"s5_scan{i}")

            def glu_epi(acc, yv, bias):
                t = acc + bias
                return yv * (1.0 / (1.0 + jnp.exp(-t))), t

            pb, tt = _mm(yg, glu_full, name=f"ssm_glu{i}", out_dtypes=(BF16, F32), epi=glu_epi,
                         extras=[(yg, 'mn'), (ssm_glu_b, 'n')])
            S.update(u=u, sre=sre, sim=sim, ypre=ypre, yg=yg, pb=pb, tt=tt)
            w_out = ssm_out_full
        else:
            uv = _mm(h1, sgi_full[j], name=f"sg_in{i}")
            pb = _sg_fwd(uv, sg_vg_full, ws_b, bsb, name=f"sg_mix{i}")
            S.update(uv=uv, pb=pb)
            w_out = sgo_full
        x_mid, y_mix = _mm(pb, w_out, name=f"mix_out{i}", out_dtypes=(F32, F32), epi=_epi_residual,
                           extras=[(xa, 'mn'), (g1, 'n')])
        h2 = _normmod_fwd(x_mid, wn2, sh2, name=f"norm2_fwd{i}")

        def relu2_epi(acc):
            r = jnp.maximum(acc, 0.0)
            return r, r * r

        ra, r2 = _mm(h2, w1_full[i], name=f"ff_up{i}", out_dtypes=(BF16, BF16), epi=relu2_epi)
        xa, f_out = _mm(r2, w2_full[i], name=f"ff_down{i}", out_dtypes=(F32, F32), epi=_epi_residual,
                        extras=[(x_mid, 'mn'), (g2, 'n')])
        S.update(x_mid=x_mid, y_mix=y_mix, h2=h2, ra=ra, r2=r2, f_out=f_out)
        saved.append(S)

    S = saved[-1]
    dx, st, dfb, loss_tile = _loss_head(xa, tgt, final_g[None], S['f_out'], S['g2'], name="loss_head")
    loss = lax.psum(loss_tile[0, 0], ("x", "y", "c"))
    d_final_g = _stat_row(st, 0)
    dg2_next = _stat_row(st, 2)

    dmod = [None] * DEPTH
    dn1g, dn2g = [None] * DEPTH, [None] * DEPTH
    dw1, dw2 = [None] * DEPTH, [None] * DEPTH
    d_cwi, d_cwo = [None] * n_conv, [None] * n_conv
    d_conv_w, d_conv_b = [None] * n_conv, [None] * n_conv
    small = {}
    for i in reversed(range(DEPTH)):
        kind, j = i % 3, i // 3
        S = saved[i]
        dg2 = dg2_next
        da = _mm(dfb, w2_full[i], tb=True, name=f"ff_down_bwd{i}", out_dtypes=(BF16,),
                 epi=lambda acc, rav: (acc * (2.0 * rav.astype(F32)),), extras=[(S['ra'], 'mn')])
        dw2[i] = _mm(S['r2'], dfb, ta=True, name=f"ff_w2_grad{i}", out_dtypes=(BF16,))
        dh2 = _mm(da, w1_full[i], tb=True, name=f"ff_up_bwd{i}")
        dw1[i] = _mm(S['h2'], da, ta=True, name=f"ff_w1_grad{i}", out_dtypes=(BF16,))
        dx_mid, st2, dyb = _normmod_bwd(dh2, S['x_mid'], S['wn2'], dx, (S['y_mix'], S['g1']), name=f"norm2_bwd{i}")
        dsc2 = _stat_row(st2, 0) * norm2_g[i:i + 1]
        dn2g[i] = _stat_row(st2, 0) * (1.0 + S['sc2'])
        dsh2 = _stat_row(st2, 1)
        dg1 = _stat_row(st2, 2)
        if kind == 0:
            dp = _mm(dyb, cwo_full[j], tb=True, name=f"conv_out_bwd{i}")
            d_cwo[j] = _mm(S['pb'], dyb, ta=True, name=f"conv_w_out_grad{i}", out_dtypes=(BF16,))
            db, dc, dxh, stc = _conv_bwd(dp, S['bcx'], S['wb'], name=f"conv_mix_bwd{i}")
            dbcx = jnp.concatenate([db, dc, dxh], 1)
            d_conv_w[j] = stc[0:3]
            d_conv_b[j] = stc[3:4]
            dh1 = _mm(dbcx, cwi_full[j], tb=True, name=f"conv_in_bwd{i}")
            d_cwi[j] = _mm(S['h1'], dbcx, ta=True, name=f"conv_w_in_grad{i}", out_dtypes=(BF16,))
        elif kind == 1:
            dy2 = _mm(dyb, ssm_out_full, tb=True, name=f"ssm_out_bwd{i}")
            d_ssm_out = _mm(S['pb'], dyb, ta=True, name=f"ssm_w_out_grad{i}", out_dtypes=(BF16,))
            dtb, dya, stg = _glu_bwd(dy2, S['yg'], S['tt'], name=f"ssm_glu_bwd{i}")
            dypre = _mm(dtb, glu_full, tb=True, name=f"ssm_glu_in_bwd{i}",
                        epi=lambda acc, a, yp: ((a + acc) * _gelu_grad(yp),),
                        extras=[(dya, 'mn'), (S['ypre'], 'mn')])
            d_glu = _mm(S['yg'], dtb, ta=True, name=f"ssm_glu_w_grad{i}", out_dtypes=(BF16,))
            dub, dbre, dbim, dcre, dcim, ga, dd = _s5_bwd(dypre, S['u'], S['sre'], S['sim'], *s5_w, pw_rev, ssm_d,
                                                           name=f"s5_scan_bwd{i}")
            dh1 = _mm(dub, ssm_in_full, tb=True, name=f"ssm_in_bwd{i}")
            d_ssm_in = _mm(S['h1'], dub, ta=True, name=f"ssm_w_in_grad{i}", out_dtypes=(BF16,))
            da_re, da_im, dlog_dt, db_re, db_im, dc_re, dc_im = s5_vjp((ga[0:1], ga[1:2], dbre, dbim, dcre, dcim))
            small.update(ssm_a_re=da_re, ssm_a_im=da_im, ssm_log_dt=dlog_dt, ssm_b_re=db_re, ssm_b_im=db_im,
                         ssm_c_re=dc_re, ssm_c_im=dc_im, ssm_d=dd[0], ssm_glu_b=stg[0])
        else:
            dp = _mm(dyb, sgo_full, tb=True, name=f"sg_out_bwd{i}")
            d_sgo = _mm(S['pb'], dyb, ta=True, name=f"sg_w_out_grad{i}", out_dtypes=(BF16,))
            duv, dws, dbs, stv = _sg_bwd(dp, S['uv'], sg_vg_full, ws_b, wst_b, bsb, name=f"sg_mix_bwd{i}")
            dh1 = _mm(duv, sgi_full[j], tb=True, name=f"sg_in_bwd{i}")
            d_sgi = _mm(S['h1'], duv, ta=True, name=f"sg_w_in_grad{i}", out_dtypes=(BF16,))
            small.update(sg_w_s=jnp.where(causal[None], dws, 0.0), sg_b_s=jnp.sum(dbs, axis=-1))
            d_sg_vg = stv[0:1]
        if i > 0:
            prev = saved[i - 1]
            dx, st1, dfb = _normmod_bwd(dh1, S['x_in'], S['wn1'], dx_mid, (prev['f_out'], prev['g2']),
                                        name=f"norm1_bwd{i}")
            dg2_next = _stat_row(st1, 2)
        else:
            dx, st1 = _normmod_bwd(dh1, S['x_in'], S['wn1'], dx_mid, None, name=f"norm1_bwd{i}")
        dsc1 = _stat_row(st1, 0) * norm1_g[i:i + 1]
        dn1g[i] = _stat_row(st1, 0) * (1.0 + S['sc1'])
        dsh1 = _stat_row(st1, 1)
        dmod[i] = jnp.concatenate([dsh1, dsc1, dg1, dsh2, dsc2, dg2], 1)
    grad_x = dx[None]

    small.update(ada_b=jnp.concatenate(dmod, 0), norm1_g=jnp.concatenate(dn1g, 0), norm2_g=jnp.concatenate(dn2g, 0),
                 final_g=d_final_g, conv_w=jnp.stack(d_conv_w), conv_b=jnp.concatenate(d_conv_b, 0), sg_v_g=d_sg_vg)
    order = SMALL_REPL + SMALL_SHARD
    sizes = [math.prod(P[n].shape) * (N_DEV if n in SMALL_SHARD else 1) for n in order]
    n_repl = sum(sizes[:len(SMALL_REPL)])
    total = sum(sizes)
    pack_rows = -(-total // (LANE * 512)) * 512
    flat = jnp.concatenate([small[n].reshape(-1).astype(F32) for n in order])
    pack = jnp.pad(flat, (0, pack_rows * LANE - total)).reshape(1, pack_rows, LANE)
    pack_all = _all_gather(pack, 0, name="gather_small_grads")

    def pack_repl(prefix):
        v = jnp.concatenate([P[prefix + n].reshape(-1) for n in SMALL_REPL])
        return jnp.pad(v, (0, pack_rows * LANE - n_repl)).reshape(pack_rows, LANE)

    rg, rd, rm, rv = _adamw(pack_repl(''), pack_all, pack_repl('m_'), pack_repl('v_'), name="adamw_small", tr=512)
    out = {}
    off = 0
    for n, sz in zip(SMALL_REPL, sizes):
        out[n] = tuple(t.reshape(-1)[off:off + sz].reshape(P[n].shape) for t in (rg, rd, rm, rv))
        off += sz
    sh_rows = (total - n_repl) // D
    sh_parts = pack_all.reshape(N_DEV, pack_rows * LANE)[:, n_repl:total].reshape(N_DEV, sh_rows, D)
    sh_parts = lax.dynamic_slice_in_dim(sh_parts, me * LANE, LANE, 2)
    sh_parts = jnp.pad(sh_parts, ((0, 0), (0, 16 - sh_rows), (0, 0)))

    def pack_shard(prefix):
        return _pad_rows(jnp.concatenate([P[prefix + n].reshape(-1, LANE) for n in SMALL_SHARD], 0), 16)

    sg_, sd_, sm_, sv_ = _adamw(pack_shard(''), sh_parts, pack_shard('m_'), pack_shard('v_'), name="adamw_channel")
    off = 0
    for n in SMALL_SHARD:
        rows = math.prod(P[n].shape) // LANE
        out[n] = tuple(t[off:off + rows].reshape(P[n].shape) for t in (sg_, sd_, sm_, sv_))
        off += rows

    dmod_all = pack_all.reshape(N_DEV, pack_rows * LANE)[:, :DEPTH * 6 * D].reshape(N_DEV, DEPTH, 6 * D)
    dmod_cols = lax.dynamic_slice_in_dim(dmod_all, me * ncol, ncol, 2)
    g_ada = jnp.concatenate([_mm(c_pad, _pad_rows(dmod_cols[:, i], LANE), ta=True, name=f"ada_w_grad{i}")
                             for i in range(DEPTH)], 0)

    def big(name, parts, tr=256):
        shp = P[name].shape
        cols = shp[-1]
        res = _adamw(P[name].reshape(-1, cols), parts.reshape(parts.shape[0], -1, cols),
                     P['m_' + name].reshape(-1, cols), P['v_' + name].reshape(-1, cols), name="adamw_" + name, tr=tr)
        out[name] = tuple(t.reshape(shp) for t in res)

    big('ada_w', g_ada[None])
    big('ff_w1', _grad_exchange(dw1, 1, name="exchange_ff_w1"))
    big('ff_w2', _grad_exchange(dw2, 0, name="exchange_ff_w2"))
    big('conv_w_in', _grad_exchange(d_cwi, 1, name="exchange_conv_w_in"))
    big('sg_w_in', _grad_exchange([d_sgi], 1, name="exchange_sg_w_in"))
    row_parts = _grad_exchange(d_cwo + [d_ssm_in, d_glu, d_ssm_out, d_sgo], 0, name="exchange_row_sharded")
    rw = _adamw(row_pack.reshape(-1, D), row_parts.reshape(N_DEV, -1, D),
                jnp.concatenate([P['m_' + n] for n in row_names], 0).reshape(-1, D),
                jnp.concatenate([P['v_' + n] for n in row_names], 0).reshape(-1, D), name="adamw_row_sharded")
    off = 0
    for n in row_names:
        rows = P[n].shape[0] * P[n].shape[1]
        out[n] = tuple(t[off:off + rows].reshape(P[n].shape) for t in rw)
        off += rows

    return (loss, grad_x, *[out[n][0] for n in WEIGHTS], *[out[n][1] for n in WEIGHTS],
            *[out[n][2] for n in WEIGHTS], *[out[n][3] for n in WEIGHTS])
```

```python
import math

import jax
import jax.numpy as jnp
from jax import lax
from jax.experimental import pallas as pl
from jax.experimental.pallas import tpu as pltpu

F32 = jnp.float32
BF16 = jnp.bfloat16

N_DEV = 8
MESH_ID = pl.DeviceIdType.MESH
DEPTH = 4
EPS = 1e-6
S5_GROUPS, S5_GROUP, S5_STATE = 64, 16, 64
S5_LANES = S5_GROUPS * S5_STATE
S5_BLOCKS = 8
SG_HEADS, SG_CHUNK = 8, 128
LANE = 128
SUBLANE = 8
VMEM_LIMIT = 48 * 1024 * 1024
ADAM_LR, ADAM_B1, ADAM_B2, ADAM_EPS, ADAM_WD, ADAM_STEP = 0.001, 0.9, 0.999, 1e-08, 0.01, 10
GELU_C = math.sqrt(2.0 / math.pi)
GELU_A = 0.044715

WEIGHTS = ['ada_w', 'ada_b', 'norm1_g', 'norm2_g', 'ff_w1', 'ff_w2', 'final_g', 'conv_w_in', 'conv_w', 'conv_b',
           'conv_w_out', 'ssm_w_in', 'ssm_a_re', 'ssm_a_im', 'ssm_log_dt', 'ssm_b_re', 'ssm_b_im', 'ssm_c_re',
           'ssm_c_im', 'ssm_d', 'ssm_glu_w', 'ssm_glu_b', 'ssm_w_out', 'sg_w_in', 'sg_v_g', 'sg_w_s', 'sg_b_s',
           'sg_w_out']
INPUTS = ['x', 'c'] + WEIGHTS + ['loss_target'] + ['m_' + n for n in WEIGHTS] + ['v_' + n for n in WEIGHTS]
SMALL_REPL = ['ada_b', 'norm1_g', 'norm2_g', 'final_g', 'ssm_a_re', 'ssm_a_im', 'ssm_log_dt', 'ssm_b_re', 'ssm_b_im',
              'ssm_c_re', 'ssm_c_im', 'ssm_d', 'ssm_glu_b', 'sg_w_s', 'sg_b_s']
SMALL_SHARD = ['conv_w', 'conv_b', 'sg_v_g']


def _params(*sem):
    return pltpu.CompilerParams(dimension_semantics=sem or None, vmem_limit_bytes=VMEM_LIMIT)


def _my_pos():
    return lax.axis_index("x"), lax.axis_index("y"), lax.axis_index("c")


def _my_index():
    x, y, c = _my_pos()
    return 4 * x + 2 * y + c


def _mm(a, b, *, name, ta=False, tb=False, out_dtypes=(F32,), epi=None, extras=(), a_fn=None, bm=1024, bn=1024,
        bk=1024):
    m, k = (a.shape[1], a.shape[0]) if ta else a.shape
    k2, n = (b.shape[1], b.shape[0]) if tb else b.shape
    assert k == k2, (a.shape, b.shape, ta, tb)
    bm, bn, bk = min(bm, m), min(bn, n), min(bk, k)
    assert m % bm == 0 and n % bn == 0 and k % bk == 0, (m, n, k, bm, bn, bk)
    nk = k // bk
    n_ex, n_out = len(extras), len(out_dtypes)
    dims = (((0 if ta else 1,), (1 if tb else 0,)), ((), ()))

    def body(*refs):
        a_ref, b_ref = refs[0], refs[1]
        ex_refs = refs[2:2 + n_ex]
        out_refs = refs[2 + n_ex:2 + n_ex + n_out]
        acc_ref = refs[-1]
        kk = pl.program_id(2)

        @pl.when(kk == 0)
        def _():
            acc_ref[...] = jnp.zeros_like(acc_ref)

        av = a_ref[...] if a_fn is None else a_fn(a_ref[...])
        acc_ref[...] += lax.dot_general(av.astype(BF16), b_ref[...].astype(BF16), dims, preferred_element_type=F32)

        @pl.when(kk == nk - 1)
        def _():
            acc = acc_ref[...]
            outs = epi(acc, *[r[...] for r in ex_refs]) if epi is not None else (acc,)
            for r, o in zip(out_refs, outs):
                r[...] = o.astype(r.dtype)

    a_spec = pl.BlockSpec((bk, bm), lambda i, j, q: (q, i)) if ta else pl.BlockSpec((bm, bk), lambda i, j, q: (i, q))
    b_spec = pl.BlockSpec((bn, bk), lambda i, j, q: (j, q)) if tb else pl.BlockSpec((bk, bn), lambda i, j, q: (q, j))
    ex_specs = []
    for arr, kind in extras:
        if kind == 'mn':
            assert arr.shape == (m, n), (arr.shape, m, n)
            ex_specs.append(pl.BlockSpec((bm, bn), lambda i, j, q: (i, j)))
        else:
            assert arr.shape == (1, n), (arr.shape, n)
            ex_specs.append(pl.BlockSpec((1, bn), lambda i, j, q: (0, j)))
    outs = pl.pallas_call(
        body, name=name,
        out_shape=tuple(jax.ShapeDtypeStruct((m, n), d) for d in out_dtypes),
        grid=(m // bm, n // bn, nk),
        in_specs=[a_spec, b_spec] + ex_specs,
        out_specs=tuple(pl.BlockSpec((bm, bn), lambda i, j, q: (i, j)) for _ in out_dtypes),
        scratch_shapes=[pltpu.VMEM((bm, bn), F32)],
        compiler_params=_params("parallel", "parallel", "arbitrary"),
    )(a, b, *[arr for arr, _ in extras])
    return outs if n_out > 1 else outs[0]


def _epi_residual(acc, res, gate):
    return res + gate * acc, acc


def _square(a):
    af = a.astype(F32)
    return af * af


def _rstd(xv):
    return lax.rsqrt(jnp.mean(xv * xv, axis=-1, keepdims=True) + EPS)


def _normmod_fwd(x, w, sh, *, name, tm=512):
    L, D = x.shape

    def body(x_ref, w_ref, s_ref, h_ref):
        xv = x_ref[...]
        h_ref[...] = (xv * _rstd(xv) * w_ref[...] + s_ref[...]).astype(h_ref.dtype)

    row = pl.BlockSpec((tm, D), lambda i: (i, 0))
    vec = pl.BlockSpec((1, D), lambda i: (0, 0))
    return pl.pallas_call(body, name=name, out_shape=jax.ShapeDtypeStruct((L, D), BF16), grid=(L // tm,),
                          in_specs=[row, vec, vec], out_specs=row, compiler_params=_params("parallel"))(x, w, sh)


def _normmod_bwd(dh, x, w, dres, gate, *, name, tm=256):
    L, D = x.shape
    has_gate = gate is not None

    def body(*refs):
        if has_gate:
            dh_ref, x_ref, w_ref, r_ref, y_ref, g_ref, dx_ref, st_ref, dy_ref = refs
        else:
            dh_ref, x_ref, w_ref, r_ref, dx_ref, st_ref = refs
        i = pl.program_id(0)

        @pl.when(i == 0)
        def _():
            st_ref[...] = jnp.zeros_like(st_ref)

        xv = x_ref[...]
        dhv = dh_ref[...].astype(F32)
        rstd = _rstd(xv)
        xn = xv * rstd
        dxn = dhv * w_ref[...]
        dx = rstd * (dxn - xn * jnp.mean(dxn * xn, axis=-1, keepdims=True)) + r_ref[...]
        dx_ref[...] = dx
        st_ref[0:1, :] += jnp.sum(dhv * xn, axis=0, keepdims=True)
        st_ref[1:2, :] += jnp.sum(dhv, axis=0, keepdims=True)
        if has_gate:
            dy_ref[...] = (dx * g_ref[...]).astype(dy_ref.dtype)
            st_ref[2:3, :] += jnp.sum(dx * y_ref[...].astype(F32), axis=0, keepdims=True)

    row = pl.BlockSpec((tm, D), lambda i: (i, 0))
    vec = pl.BlockSpec((1, D), lambda i: (0, 0))
    st = pl.BlockSpec((SUBLANE, D), lambda i: (0, 0))
    in_specs = [row, row, vec, row] + ([row, vec] if has_gate else [])
    out_shape = [jax.ShapeDtypeStruct((L, D), F32), jax.ShapeDtypeStruct((SUBLANE, D), F32)]
    out_specs = [row, st]
    if has_gate:
        out_shape.append(jax.ShapeDtypeStruct((L, D), BF16))
        out_specs.append(row)
    args = (dh, x, w, dres) + (tuple(gate) if has_gate else ())
    return pl.pallas_call(body, name=name, out_shape=tuple(out_shape), grid=(L // tm,), in_specs=in_specs,
                          out_specs=tuple(out_specs), compiler_params=_params("arbitrary"))(*args)


def _loss_head(x, tgt, fg, y, g, *, name, tm=256):
    L, D = x.shape

    def body(x_ref, t_ref, fg_ref, y_ref, g_ref, dx_ref, st_ref, dy_ref, loss_ref):
        i = pl.program_id(0)

        @pl.when(i == 0)
        def _():
            st_ref[...] = jnp.zeros_like(st_ref)
            loss_ref[...] = jnp.zeros_like(loss_ref)

        xv = x_ref[...]
        rstd = _rstd(xv)
        xn = xv * rstd
        err = xn * fg_ref[...] - t_ref[...]
        loss_ref[...] += 0.5 * jnp.sum(jnp.mean(err * err, axis=-1, keepdims=True))
        dout = err * (1.0 / D)
        dxn = dout * fg_ref[...]
        dx = rstd * (dxn - xn * jnp.mean(dxn * xn, axis=-1, keepdims=True))
        dx_ref[...] = dx
        dy_ref[...] = (dx * g_ref[...]).astype(dy_ref.dtype)
        st_ref[0:1, :] += jnp.sum(dout * xn, axis=0, keepdims=True)
        st_ref[2:3, :] += jnp.sum(dx * y_ref[...].astype(F32), axis=0, keepdims=True)

    row = pl.BlockSpec((tm, D), lambda i: (i, 0))
    vec = pl.BlockSpec((1, D), lambda i: (0, 0))
    return pl.pallas_call(
        body, name=name,
        out_shape=(jax.ShapeDtypeStruct((L, D), F32), jax.ShapeDtypeStruct((SUBLANE, D), F32),
                   jax.ShapeDtypeStruct((L, D), BF16), jax.ShapeDtypeStruct((SUBLANE, LANE), F32)),
        grid=(L // tm,), in_specs=[row, row, vec, row, vec],
        out_specs=(row, pl.BlockSpec((SUBLANE, D), lambda i: (0, 0)), row,
                   pl.BlockSpec((SUBLANE, LANE), lambda i: (0, 0))),
        compiler_params=_params("arbitrary"))(x, tgt, fg, y, g)


def _shift_down(v, k):
    row = lax.broadcasted_iota(jnp.int32, v.shape, 0)
    return jnp.where(row >= k, pltpu.roll(v, k, 0), 0.0)


def _shift_up(v, k):
    n = v.shape[0]
    row = lax.broadcasted_iota(jnp.int32, v.shape, 0)
    return jnp.where(row < n - k, pltpu.roll(v, n - k, 0), 0.0)


def _conv_views(L, D):
    return [pl.BlockSpec((L, LANE), lambda j, s=s: (0, s * (D // LANE) + j)) for s in range(3)]


def _conv_fwd(bcx, wb, *, name):
    L, D = bcx.shape[0], bcx.shape[1] // 3

    def body(b_ref, c_ref, x_ref, wb_ref, p_ref):
        z = c_ref[...] * x_ref[...]
        conv = (wb_ref[0:1, :] * _shift_down(z, 2) + wb_ref[1:2, :] * _shift_down(z, 1)
                + wb_ref[2:3, :] * z + wb_ref[3:4, :])
        p_ref[...] = (b_ref[...] * conv).astype(p_ref.dtype)

    col = pl.BlockSpec((L, LANE), lambda j: (0, j))
    return pl.pallas_call(body, name=name, out_shape=jax.ShapeDtypeStruct((L, D), BF16), grid=(D // LANE,),
                          in_specs=_conv_views(L, D) + [pl.BlockSpec((SUBLANE, LANE), lambda j: (0, j))],
                          out_specs=col, compiler_params=_params("parallel"))(bcx, bcx, bcx, wb)


def _conv_bwd(dp, bcx, wb, *, name):
    L, D = dp.shape

    def body(dp_ref, b_ref, c_ref, x_ref, wb_ref, db_ref, dc_ref, dxh_ref, st_ref):
        cv, xv = c_ref[...], x_ref[...]
        z = cv * xv
        z1, z2 = _shift_down(z, 1), _shift_down(z, 2)
        w0, w1, w2 = wb_ref[0:1, :], wb_ref[1:2, :], wb_ref[2:3, :]
        conv = w0 * z2 + w1 * z1 + w2 * z + wb_ref[3:4, :]
        dpv = dp_ref[...]
        db_ref[...] = (dpv * conv).astype(db_ref.dtype)
        dconv = dpv * b_ref[...]
        dz = w2 * dconv + w1 * _shift_up(dconv, 1) + w0 * _shift_up(dconv, 2)
        dc_ref[...] = (dz * xv).astype(dc_ref.dtype)
        dxh_ref[...] = (dz * cv).astype(dxh_ref.dtype)
        st_ref[...] = jnp.zeros_like(st_ref)
        st_ref[0:1, :] = jnp.sum(dconv * z2, axis=0, keepdims=True)
        st_ref[1:2, :] = jnp.sum(dconv * z1, axis=0, keepdims=True)
        st_ref[2:3, :] = jnp.sum(dconv * z, axis=0, keepdims=True)
        st_ref[3:4, :] = jnp.sum(dconv, axis=0, keepdims=True)

    col = pl.BlockSpec((L, LANE), lambda j: (0, j))
    vec = pl.BlockSpec((SUBLANE, LANE), lambda j: (0, j))
    act = jax.ShapeDtypeStruct((L, D), BF16)
    return pl.pallas_call(body, name=name, out_shape=(act, act, act, jax.ShapeDtypeStruct((SUBLANE, D), F32)),
                          grid=(D // LANE,), in_specs=[col] + _conv_views(L, D) + [vec],
                          out_specs=(col, col, col, vec), compiler_params=_params("parallel"))(dp, bcx, bcx, bcx, wb)


def _sg_fwd(uv, vg, ws, bsb, *, name, tr=512):
    L, D = uv.shape[0], uv.shape[1] // 2

    def body(uv_ref, vg_ref, ws_ref, bsb_ref, p_ref):
        for ci in range(tr // SG_CHUNK):
            rows = slice(ci * SG_CHUNK, (ci + 1) * SG_CHUNK)
            v = uv_ref[rows, D:2 * D]
            vn = (v * _rstd(v) * vg_ref[...]).astype(BF16)
            for h in range(SG_HEADS):
                cols = slice(h * LANE, (h + 1) * LANE)
                vm = jnp.dot(ws_ref[h], vn[:, cols], preferred_element_type=F32) + bsb_ref[h]
                p_ref[rows, cols] = (uv_ref[rows, cols] * vm).astype(p_ref.dtype)

    full3 = pl.BlockSpec((SG_HEADS, SG_CHUNK, LANE), lambda i: (0, 0, 0))
    return pl.pallas_call(body, name=name, out_shape=jax.ShapeDtypeStruct((L, D), BF16), grid=(L // tr,),
                          in_specs=[pl.BlockSpec((tr, 2 * D), lambda i: (i, 0)), pl.BlockSpec((1, D), lambda i: (0, 0)),
                                    full3, full3],
                          out_specs=pl.BlockSpec((tr, D), lambda i: (i, 0)),
                          compiler_params=_params("parallel"))(uv, vg, ws, bsb)


def _sg_bwd(dp, uv, vg, ws, wst, bsb, *, name, tr=512):
    L, D = dp.shape

    def body(dp_ref, uv_ref, vg_ref, ws_ref, wst_ref, bsb_ref, duv_ref, dws_ref, dbs_ref, st_ref, dvn_ref):
        i = pl.program_id(0)

        @pl.when(i == 0)
        def _():
            dws_ref[...] = jnp.zeros_like(dws_ref)
            dbs_ref[...] = jnp.zeros_like(dbs_ref)
            st_ref[...] = jnp.zeros_like(st_ref)

        for ci in range(tr // SG_CHUNK):
            rows = slice(ci * SG_CHUNK, (ci + 1) * SG_CHUNK)
            v = uv_ref[rows, D:2 * D]
            rstd = _rstd(v)
            vhat = v * rstd
            vn = (vhat * vg_ref[...]).astype(BF16)
            for h in range(SG_HEADS):
                cols = slice(h * LANE, (h + 1) * LANE)
                vm = jnp.dot(ws_ref[h], vn[:, cols], preferred_element_type=F32) + bsb_ref[h]
                dph = dp_ref[rows, cols]
                duv_ref[rows, cols] = (dph * vm).astype(duv_ref.dtype)
                dvm = dph * uv_ref[rows, cols]
                dbs_ref[h] += dvm
                dvmb = dvm.astype(BF16)
                dws_ref[h] += lax.dot_general(dvmb, vn[:, cols], (((1,), (1,)), ((), ())),
                                              preferred_element_type=F32)
                dvn_ref[rows, cols] = jnp.dot(wst_ref[h], dvmb, preferred_element_type=F32)
            dvn = dvn_ref[rows, :]
            gv = dvn * vg_ref[...]
            dv = rstd * (gv - vhat * jnp.mean(gv * vhat, axis=-1, keepdims=True))
            duv_ref[rows, D:2 * D] = dv.astype(duv_ref.dtype)
            st_ref[0:1, :] += jnp.sum(dvn * vhat, axis=0, keepdims=True)

    full3 = pl.BlockSpec((SG_HEADS, SG_CHUNK, LANE), lambda i: (0, 0, 0))
    acc3 = jax.ShapeDtypeStruct((SG_HEADS, SG_CHUNK, LANE), F32)
    return pl.pallas_call(
        body, name=name,
        out_shape=(jax.ShapeDtypeStruct((L, 2 * D), BF16), acc3, acc3, jax.ShapeDtypeStruct((SUBLANE, D), F32)),
        grid=(L // tr,),
        in_specs=[pl.BlockSpec((tr, D), lambda i: (i, 0)), pl.BlockSpec((tr, 2 * D), lambda i: (i, 0)),
                  pl.BlockSpec((1, D), lambda i: (0, 0)), full3, full3, full3],
        out_specs=(pl.BlockSpec((tr, 2 * D), lambda i: (i, 0)), full3, full3,
                   pl.BlockSpec((SUBLANE, D), lambda i: (0, 0))),
        scratch_shapes=[pltpu.VMEM((tr, D), F32)],
        compiler_params=_params("arbitrary"))(dp, uv, vg, ws, wst, bsb)


def _gelu(x):
    return 0.5 * x * (1.0 + jnp.tanh(GELU_C * (x + GELU_A * x * x * x)))


def _gelu_grad(x):
    th = jnp.tanh(GELU_C * (x + GELU_A * x * x * x))
    return 0.5 * (1.0 + th) + 0.5 * x * (1.0 - th * th) * GELU_C * (1.0 + 3.0 * GELU_A * x * x)


def _cmul_add(xr, xi, ar, ai, br, bi):
    return xr + ar * br - ai * bi, xi + ar * bi + ai * br


def _s5_fwd(u, bre, bim, cre, cim, pw, dsk, *, name, tc=512):
    L, D = u.shape
    W = S5_LANES // S5_BLOCKS
    nt = L // tc

    def body(u_ref, bre_ref, bim_ref, cre_ref, cim_ref, pw_ref, d_ref, sre_ref, sim_ref, ypre_ref, yg_ref, carry):
        t = pl.program_id(1)

        @pl.when(t == 0)
        def _():
            carry[...] = jnp.zeros_like(carry)

        uv = u_ref[...]
        ub = uv.astype(BF16)
        sre_ref[...] = jnp.dot(ub, bre_ref[...], preferred_element_type=F32)
        sim_ref[...] = jnp.dot(ub, bim_ref[...], preferred_element_type=F32)

        def tile(i, c):
            cr, ci = c
            rows = pl.ds(pl.multiple_of(i * SUBLANE, SUBLANE), SUBLANE)
            xr, xi = sre_ref[rows, :], sim_ref[rows, :]
            for k, d in enumerate((1, 2, 4)):
                xr, xi = _cmul_add(xr, xi, pw_ref[2 * k], pw_ref[2 * k + 1], pltpu.roll(xr, d, 0),
                                   pltpu.roll(xi, d, 0))
            xr, xi = _cmul_add(xr, xi, pw_ref[6], pw_ref[7], cr, ci)
            sre_ref[rows, :] = xr
            sim_ref[rows, :] = xi
            last = slice(SUBLANE - 1, SUBLANE)
            return jnp.broadcast_to(xr[last, :], (SUBLANE, W)), jnp.broadcast_to(xi[last, :], (SUBLANE, W))

        cr, ci = lax.fori_loop(0, tc // SUBLANE, tile, (carry[0], carry[1]))
        carry[0] = cr
        carry[1] = ci
        y = (jnp.dot(sre_ref[...].astype(BF16), cre_ref[...], preferred_element_type=F32)
             - jnp.dot(sim_ref[...].astype(BF16), cim_ref[...], preferred_element_type=F32) + d_ref[...] * uv)
        ypre_ref[...] = y
        yg_ref[...] = _gelu(y)

    ch = pl.BlockSpec((tc, LANE), lambda j, t: (t, j))
    st = pl.BlockSpec((tc, W), lambda j, t: (t, j))
    bsp = pl.BlockSpec((None, LANE, W), lambda j, t: (j, 0, 0))
    csp = pl.BlockSpec((None, W, LANE), lambda j, t: (j, 0, 0))
    return pl.pallas_call(
        body, name=name,
        out_shape=(jax.ShapeDtypeStruct((L, S5_LANES), F32), jax.ShapeDtypeStruct((L, S5_LANES), F32),
                   jax.ShapeDtypeStruct((L, D), F32), jax.ShapeDtypeStruct((L, D), F32)),
        grid=(S5_BLOCKS, nt),
        in_specs=[ch, bsp, bsp, csp, csp, pl.BlockSpec((8, SUBLANE, W), lambda j, t: (0, 0, j)),
                  pl.BlockSpec((1, LANE), lambda j, t: (0, j))],
        out_specs=(st, st, ch, ch),
        scratch_shapes=[pltpu.VMEM((2, SUBLANE, W), F32)],
        compiler_params=_params("parallel", "arbitrary"))(u, bre, bim, cre, cim, pw, dsk)


def _s5_bwd(dy, u, sre, sim, bre, bim, cre, cim, pwr, dsk, *, name, tc=512):
    L, D = u.shape
    W = S5_LANES // S5_BLOCKS
    nt = L // tc
    ntile = tc // SUBLANE
    nt_dims = (((1,), (1,)), ((), ()))
    tn_dims = (((0,), (0,)), ((), ()))

    def body(dy_ref, u_ref, sre_ref, sim_ref, bre_ref, bim_ref, cre_ref, cim_ref, pw_ref, d_ref,
             du_ref, dbre_ref, dbim_ref, dcre_ref, dcim_ref, ga_ref, dd_ref, gre, gim, carry, gacc):
        t = pl.program_id(1)

        @pl.when(t == 0)
        def _():
            for r in (carry, gacc, dbre_ref, dbim_ref, dcre_ref, dcim_ref, ga_ref, dd_ref):
                r[...] = jnp.zeros_like(r)

        dyv, uv = dy_ref[...], u_ref[...]
        dyb, ub = dyv.astype(BF16), uv.astype(BF16)
        gre[...] = lax.dot_general(dyb, cre_ref[...], nt_dims, preferred_element_type=F32)
        gim[...] = -lax.dot_general(dyb, cim_ref[...], nt_dims, preferred_element_type=F32)
        top = lax.broadcasted_iota(jnp.int32, (SUBLANE, W), 0) == SUBLANE - 1

        def tile(k, c):
            cr, ci = c
            rows = pl.ds(pl.multiple_of((ntile - 1 - k) * SUBLANE, SUBLANE), SUBLANE)
            xr, xi = gre[rows, :], gim[rows, :]
            for q, d in enumerate((1, 2, 4)):
                xr, xi = _cmul_add(xr, xi, pw_ref[2 * q], pw_ref[2 * q + 1], pltpu.roll(xr, SUBLANE - d, 0),
                                   pltpu.roll(xi, SUBLANE - d, 0))
            xr, xi = _cmul_add(xr, xi, pw_ref[6], pw_ref[7], cr, ci)
            gre[rows, :] = xr
            gim[rows, :] = xi
            nr = jnp.where(top, cr, pltpu.roll(xr, SUBLANE - 1, 0))
            ni = jnp.where(top, ci, pltpu.roll(xi, SUBLANE - 1, 0))
            sr, si = sre_ref[rows, :], sim_ref[rows, :]
            gacc[0] += sr * nr + si * ni
            gacc[1] += sr * ni - si * nr
            return jnp.broadcast_to(xr[0:1, :], (SUBLANE, W)), jnp.broadcast_to(xi[0:1, :], (SUBLANE, W))

        cr, ci = lax.fori_loop(0, ntile, tile, (carry[0], carry[1]))
        carry[0] = cr
        carry[1] = ci
        grb, gib = gre[...].astype(BF16), gim[...].astype(BF16)
        du = (lax.dot_general(grb, bre_ref[...], nt_dims, preferred_element_type=F32)
              + lax.dot_general(gib, bim_ref[...], nt_dims, preferred_element_type=F32) + d_ref[...] * dyv)
        du_ref[...] = du.astype(du_ref.dtype)
        dbre_ref[...] += lax.dot_general(ub, grb, tn_dims, preferred_element_type=F32)
        dbim_ref[...] += lax.dot_general(ub, gib, tn_dims, preferred_element_type=F32)
        dcre_ref[...] += lax.dot_general(sre_ref[...].astype(BF16), dyb, tn_dims, preferred_element_type=F32)
        dcim_ref[...] -= lax.dot_general(sim_ref[...].astype(BF16), dyb, tn_dims, preferred_element_type=F32)
        dd_ref[0:1, :] += jnp.sum(dyv * uv, axis=0, keepdims=True)

        @pl.when(t == nt - 1)
        def _():
            ga_ref[0:1, :] = jnp.sum(gacc[0], axis=0, keepdims=True)
            ga_ref[1:2, :] = jnp.sum(gacc[1], axis=0, keepdims=True)

    ch = pl.BlockSpec((tc, LANE), lambda j, t: (nt - 1 - t, j))
    st = pl.BlockSpec((tc, W), lambda j, t: (nt - 1 - t, j))
    bsp = pl.BlockSpec((None, LANE, W), lambda j, t: (j, 0, 0))
    csp = pl.BlockSpec((None, W, LANE), lambda j, t: (j, 0, 0))
    return pl.pallas_call(
        body, name=name,
        out_shape=(jax.ShapeDtypeStruct((L, D), BF16),
                   jax.ShapeDtypeStruct((S5_BLOCKS, LANE, W), F32), jax.ShapeDtypeStruct((S5_BLOCKS, LANE, W), F32),
                   jax.ShapeDtypeStruct((S5_BLOCKS, W, LANE), F32), jax.ShapeDtypeStruct((S5_BLOCKS, W, LANE), F32),
                   jax.ShapeDtypeStruct((SUBLANE, S5_LANES), F32), jax.ShapeDtypeStruct((SUBLANE, D), F32)),
        grid=(S5_BLOCKS, nt),
        in_specs=[ch, ch, st, st, bsp, bsp, csp, csp, pl.BlockSpec((8, SUBLANE, W), lambda j, t: (0, 0, j)),
                  pl.BlockSpec((1, LANE), lambda j, t: (0, j))],
        out_specs=(ch, bsp, bsp, csp, csp, pl.BlockSpec((SUBLANE, W), lambda j, t: (0, j)),
                   pl.BlockSpec((SUBLANE, LANE), lambda j, t: (0, j))),
        scratch_shapes=[pltpu.VMEM((tc, W), F32), pltpu.VMEM((tc, W), F32), pltpu.VMEM((2, SUBLANE, W), F32),
                        pltpu.VMEM((2, SUBLANE, W), F32)],
        compiler_params=_params("parallel", "arbitrary"))(dy, u, sre, sim, bre, bim, cre, cim, pwr, dsk)


def _glu_bwd(dy2, y, t, *, name, tm=256):
    L, D = y.shape

    def body(dy2_ref, y_ref, t_ref, dt_ref, dya_ref, st_ref):
        i = pl.program_id(0)

        @pl.when(i == 0)
        def _():
            st_ref[...] = jnp.zeros_like(st_ref)

        sig = 1.0 / (1.0 + jnp.exp(-t_ref[...]))
        dy2v = dy2_ref[...]
        dt = dy2v * y_ref[...] * sig * (1.0 - sig)
        dt_ref[...] = dt.astype(dt_ref.dtype)
        dya_ref[...] = dy2v * sig
        st_ref[0:1, :] += jnp.sum(dt, axis=0, keepdims=True)

    row = pl.BlockSpec((tm, D), lambda i: (i, 0))
    return pl.pallas_call(
        body, name=name,
        out_shape=(jax.ShapeDtypeStruct((L, D), BF16), jax.ShapeDtypeStruct((L, D), F32),
                   jax.ShapeDtypeStruct((SUBLANE, D), F32)),
        grid=(L // tm,), in_specs=[row, row, row],
        out_specs=(row, row, pl.BlockSpec((SUBLANE, D), lambda i: (0, 0))),
        compiler_params=_params("arbitrary"))(dy2, y, t)


def _s5_prep(a_re, a_im, log_dt, b_re, b_im, c_re, c_im):
    dt = jnp.exp(log_dt)[:, None]
    mag = jnp.exp(a_re * dt)
    abar_re = mag * jnp.cos(a_im * dt)
    abar_im = mag * jnp.sin(a_im * dt)
    den = a_re * a_re + a_im * a_im
    nr = abar_re - 1.0
    ni = abar_im
    f_re = ((nr * a_re + ni * a_im) / den)[..., None]
    f_im = ((ni * a_re - nr * a_im) / den)[..., None]
    bbar_re = f_re * b_re - f_im * b_im
    bbar_im = f_re * b_im + f_im * b_re
    eye = jnp.eye(S5_GROUPS // S5_BLOCKS, dtype=F32)
    gb = S5_GROUPS // S5_BLOCKS

    def blk_b(bb):
        t = bb.reshape(S5_BLOCKS, gb, S5_STATE, S5_GROUP)
        return jnp.einsum('jgph,gk->jghkp', t, eye).reshape(S5_BLOCKS, gb * S5_GROUP, gb * S5_STATE)

    def blk_c(cc):
        t = cc.reshape(S5_BLOCKS, gb, S5_GROUP, S5_STATE)
        return jnp.einsum('jghp,gk->jgpkh', t, eye).reshape(S5_BLOCKS, gb * S5_STATE, gb * S5_GROUP)

    return (abar_re.reshape(1, S5_LANES), abar_im.reshape(1, S5_LANES), blk_b(bbar_re), blk_b(bbar_im),
            blk_c(c_re), blk_c(c_im))


def _s5_power_tables(ar, ai):
    pr, pi = [jnp.ones_like(ar)], [jnp.zeros_like(ai)]
    for _ in range(SUBLANE):
        pr, pi = pr + [pr[-1] * ar - pi[-1] * ai], pi + [pr[-1] * ai + pi[-1] * ar]
    row = jnp.arange(SUBLANE)[:, None]

    def tables(sign, keep, carry_pow):
        out = []
        for d in (1, 2, 4):
            out += [jnp.where(keep(d), pr[d], 0.0), jnp.where(keep(d), sign * pi[d], 0.0)]
        out += [jnp.concatenate([pr[p] for p in carry_pow], 0), sign * jnp.concatenate([pi[p] for p in carry_pow], 0)]
        return jnp.stack([jnp.broadcast_to(o, (SUBLANE, ar.shape[1])) for o in out])

    fwd = tables(1.0, lambda d: row >= d, [r + 1 for r in range(SUBLANE)])
    rev = tables(-1.0, lambda d: row + d <= SUBLANE - 1, [SUBLANE - r for r in range(SUBLANE)])
    return fwd, rev


ADAMW_PART_BLOCK_BYTES = 2 * 1024 * 1024


def _adamw(w, parts, m, v, *, name):
    n, R, C = w.shape
    assert len(parts) == n
    P = parts[0].shape[0]
    tr = R
    while P * tr * C * parts[0].dtype.itemsize > ADAMW_PART_BLOCK_BYTES and tr % 16 == 0:
        tr //= 2
    c1 = 1.0 / (1.0 - ADAM_B1 ** ADAM_STEP)
    c2 = 1.0 / (1.0 - ADAM_B2 ** ADAM_STEP)

    def body(*refs):
        w_ref, m_ref, v_ref = refs[:3]
        p_refs = refs[3:3 + n]
        g_ref, d_ref, nm_ref, nv_ref = refs[3 + n:]
        layer = pl.program_id(0)
        for q, p_ref in enumerate(p_refs):
            @pl.when(layer == q)
            def _(p_ref=p_ref):
                g = p_ref[0].astype(F32)
                for s in range(1, P):
                    g = g + p_ref[s].astype(F32)
                nm = ADAM_B1 * m_ref[...] + (1.0 - ADAM_B1) * g
                nv = ADAM_B2 * v_ref[...] + (1.0 - ADAM_B2) * (g * g)
                g_ref[...] = g
                nm_ref[...] = nm
                nv_ref[...] = nv
                d_ref[...] = -ADAM_LR * ((nm * c1) / (jnp.sqrt(nv * c2) + ADAM_EPS) + ADAM_WD * w_ref[...])

    row = pl.BlockSpec((None, tr, C), lambda l, i: (l, i, 0))
    part_specs = [pl.BlockSpec((P, tr, C), lambda l, i, q=q: (0, jnp.where(l == q, i, 0), 0)) for q in range(n)]
    out = jax.ShapeDtypeStruct((n, R, C), F32)
    return pl.pallas_call(body, name=name, out_shape=(out, out, out, out), grid=(n, R // tr),
                          in_specs=[row, row, row] + part_specs, out_specs=(row, row, row, row),
                          compiler_params=_params("arbitrary", "arbitrary"))(w, m, v, *parts)


def _all_gather(xs, axis, *, name):
    m = xs.shape[axis]
    out_shape = list(xs.shape)
    out_shape[axis] = N_DEV * m

    def body(x_ref, out_ref, send_sems, recv_sems, local_sem):
        x, y, c = _my_pos()
        me, sibling = (x, y, c), (x, y, 1 - c)
        chips = [(1 - x, y), (x, 1 - y), (1 - x, 1 - y)]

        def blk(px, py, pc):
            idx = [slice(None)] * 3
            idx[axis] = pl.ds((4 * px + 2 * py + pc) * m, m)
            return out_ref.at[tuple(idx)]

        def copy(k, block, to, src=None):
            return pltpu.make_async_remote_copy(src_ref=blk(*block) if src is None else src, dst_ref=blk(*block),
                                                send_sem=send_sems.at[k], recv_sem=recv_sems.at[k],
                                                device_id=to, device_id_type=MESH_ID)

        mine = pltpu.make_async_copy(x_ref, blk(*me), local_sem)
        mine.start()
        first = [copy(0, me, sibling, src=x_ref)]
        first += [copy(1 + j, me, (*chip, c), src=x_ref) for j, chip in enumerate(chips)]
        for cp in first:
            cp.start()
        passed = [copy(4 + j, (*chip, c), sibling) for j, chip in enumerate(chips)]
        for j, chip in enumerate(chips):
            copy(1 + j, (*chip, c), me).wait_recv()
            passed[j].start()
        copy(0, sibling, me).wait_recv()
        for j, chip in enumerate(chips):
            copy(4 + j, (*chip, 1 - c), me).wait_recv()
        for cp in first + passed:
            cp.wait_send()
        mine.wait()

    hbm = pl.BlockSpec(memory_space=pl.ANY)
    return pl.pallas_call(body, name=name, out_shape=jax.ShapeDtypeStruct(tuple(out_shape), xs.dtype),
                          in_specs=[hbm], out_specs=hbm,
                          scratch_shapes=[pltpu.SemaphoreType.DMA((N_DEV - 1,)), pltpu.SemaphoreType.DMA((N_DEV - 1,)),
                                          pltpu.SemaphoreType.DMA],
                          compiler_params=pltpu.CompilerParams(has_side_effects=True))(xs)


def _block(ref, axis, idx, m):
    return ref.at[pl.ds(idx * m, m), :] if axis == 0 else ref.at[:, pl.ds(idx * m, m)]


def _exchange_copies(metas, src_refs, land_refs, send_sems, recv_sems):
    x, y, c = _my_pos()
    me = 4 * x + 2 * y + c
    pairs = []
    for r in range(1, N_DEV):
        pos = (1 - x if r & 4 else x, 1 - y if r & 2 else y, 1 - c if r & 1 else c)
        peer = 4 * pos[0] + 2 * pos[1] + pos[2]
        for (kind, axis, m), s_ref, l_ref in zip(metas, src_refs, land_refs):
            if kind == 'gather':
                src, dst, arrival = s_ref, _block(l_ref, axis, me, m), _block(l_ref, axis, peer, m)
            else:
                src, dst, arrival = _block(s_ref, axis, peer, m), l_ref.at[me], l_ref.at[peer]
            pairs.append(tuple(
                pltpu.make_async_remote_copy(src_ref=src, dst_ref=d, send_sem=send_sems.at[r - 1],
                                             recv_sem=recv_sems.at[r - 1], device_id=pos, device_id_type=MESH_ID)
                for d in (dst, arrival)))
    return pairs


def _exchange_start(items, *, name):
    n = len(items)
    metas = [it[2] for it in items]

    def body(*refs):
        src_refs, land_refs = refs[:n], refs[n:2 * n]
        send_sems, recv_sems = refs[2 * n], refs[2 * n + 1]
        token = refs[-1]
        for outgoing, _ in _exchange_copies(metas, src_refs, land_refs, send_sems, recv_sems):
            outgoing.start()
        token[...] = jnp.zeros_like(token)

    hbm = pl.BlockSpec(memory_space=pltpu.HBM)
    sem = pl.BlockSpec(memory_space=pltpu.SEMAPHORE)
    arrays = [it[0] for it in items] + [it[1] for it in items]
    res = pl.pallas_call(
        body, name=name,
        out_shape=(pltpu.SemaphoreType.DMA((N_DEV - 1,)), pltpu.SemaphoreType.DMA((N_DEV - 1,)),
                   *[pltpu.HBM(a.shape, a.dtype) for a in arrays], jax.ShapeDtypeStruct((SUBLANE, LANE), F32)),
        in_specs=[hbm] * (2 * n),
        out_specs=(sem, sem, *[hbm] * (2 * n), pl.BlockSpec(memory_space=pltpu.VMEM)),
        input_output_aliases={q: 2 + q for q in range(2 * n)},
        compiler_params=pltpu.CompilerParams(has_side_effects=pltpu.SideEffectType.DATAFLOW_SIDE_EFFECTING),
    )(*[pltpu.with_memory_space_constraint(a, pltpu.HBM) for a in arrays])
    return res[0], res[1], list(res[2:2 + n]), list(res[2 + n:2 + 2 * n]), res[-1], metas


def _exchange_wait(started, after, *, name):
    send_sems, recv_sems, srcs, lands, _, metas = started
    n = len(srcs)

    def body(*refs):
        src_refs, land_refs = refs[:n], refs[n:2 * n]
        s_sems, r_sems = refs[2 * n], refs[2 * n + 1]
        for outgoing, incoming in _exchange_copies(metas, src_refs, land_refs, s_sems, r_sems):
            outgoing.wait_send()
            incoming.wait_recv()

    hbm = pl.BlockSpec(memory_space=pltpu.HBM)
    sem = pl.BlockSpec(memory_space=pltpu.SEMAPHORE)
    arrays = srcs + lands
    res = pl.pallas_call(
        body, name=name,
        out_shape=tuple(pltpu.HBM(a.shape, a.dtype) for a in arrays),
        in_specs=[hbm] * (2 * n) + [sem, sem, pl.BlockSpec(memory_space=pl.ANY)],
        out_specs=tuple([hbm] * (2 * n)),
        input_output_aliases={q: q for q in range(2 * n)},
        compiler_params=pltpu.CompilerParams(has_side_effects=pltpu.SideEffectType.DATAFLOW_SIDE_EFFECTING),
    )(*arrays, send_sems, recv_sems, after)
    return list(res[n:])


def _pad_rows(a, rows):
    return jnp.pad(a, ((0, rows - a.shape[0]), (0, 0)))


def _stat_row(st, r):
    return st[r:r + 1, :]


def kernel(x, c, ada_w, ada_b, norm1_g, norm2_g, ff_w1, ff_w2, final_g, conv_w_in, conv_w, conv_b, conv_w_out, ssm_w_in, ssm_a_re, ssm_a_im, ssm_log_dt, ssm_b_re, ssm_b_im, ssm_c_re, ssm_c_im, ssm_d, ssm_glu_w, ssm_glu_b, ssm_w_out, sg_w_in, sg_v_g, sg_w_s, sg_b_s, sg_w_out, loss_target, m_ada_w, m_ada_b, m_norm1_g, m_norm2_g, m_ff_w1, m_ff_w2, m_final_g, m_conv_w_in, m_conv_w, m_conv_b, m_conv_w_out, m_ssm_w_in, m_ssm_a_re, m_ssm_a_im, m_ssm_log_dt, m_ssm_b_re, m_ssm_b_im, m_ssm_c_re, m_ssm_c_im, m_ssm_d, m_ssm_glu_w, m_ssm_glu_b, m_ssm_w_out, m_sg_w_in, m_sg_v_g, m_sg_w_s, m_sg_b_s, m_sg_w_out, v_ada_w, v_ada_b, v_norm1_g, v_norm2_g, v_ff_w1, v_ff_w2, v_final_g, v_conv_w_in, v_conv_w, v_conv_b, v_conv_w_out, v_ssm_w_in, v_ssm_a_re, v_ssm_a_im, v_ssm_log_dt, v_ssm_b_re, v_ssm_b_im, v_ssm_c_re, v_ssm_c_im, v_ssm_d, v_ssm_glu_w, v_ssm_glu_b, v_ssm_w_out, v_sg_w_in, v_sg_v_g, v_sg_w_s, v_sg_b_s, v_sg_w_out):
    P = dict(zip(INPUTS, (x, c, ada_w, ada_b, norm1_g, norm2_g, ff_w1, ff_w2, final_g, conv_w_in, conv_w, conv_b, conv_w_out, ssm_w_in, ssm_a_re, ssm_a_im, ssm_log_dt, ssm_b_re, ssm_b_im, ssm_c_re, ssm_c_im, ssm_d, ssm_glu_w, ssm_glu_b, ssm_w_out, sg_w_in, sg_v_g, sg_w_s, sg_b_s, sg_w_out, loss_target, m_ada_w, m_ada_b, m_norm1_g, m_norm2_g, m_ff_w1, m_ff_w2, m_final_g, m_conv_w_in, m_conv_w, m_conv_b, m_conv_w_out, m_ssm_w_in, m_ssm_a_re, m_ssm_a_im, m_ssm_log_dt, m_ssm_b_re, m_ssm_b_im, m_ssm_c_re, m_ssm_c_im, m_ssm_d, m_ssm_glu_w, m_ssm_glu_b, m_ssm_w_out, m_sg_w_in, m_sg_v_g, m_sg_w_s, m_sg_b_s, m_sg_w_out, v_ada_w, v_ada_b, v_norm1_g, v_norm2_g, v_ff_w1, v_ff_w2, v_final_g, v_conv_w_in, v_conv_w, v_conv_b, v_conv_w_out, v_ssm_w_in, v_ssm_a_re, v_ssm_a_im, v_ssm_log_dt, v_ssm_b_re, v_ssm_b_im, v_ssm_c_re, v_ssm_c_im, v_ssm_d, v_ssm_glu_w, v_ssm_glu_b, v_ssm_w_out, v_sg_w_in, v_sg_v_g, v_sg_w_s, v_sg_b_s, v_sg_w_out)))
    L, D = x.shape[1], x.shape[2]
    me = _my_index()
    xs = x[0]
    tgt = loss_target[0]
    n_conv = conv_w_in.shape[0]

    c_act = c * (1.0 / (1.0 + jnp.exp(-c)))
    vec_rows = jnp.concatenate([c_act.reshape(D // LANE, LANE), conv_w.reshape(-1, LANE), conv_b.reshape(-1, LANE),
                                sg_v_g.reshape(-1, LANE)], 0)
    n_vec = vec_rows.shape[0]
    vec_all = _all_gather(_pad_rows(vec_rows, 24)[None], 0, name="gather_vectors")
    c_all = vec_all[:, :D // LANE, :].reshape(N_DEV, D)
    sharded_full = vec_all[:, D // LANE:n_vec, :].transpose(1, 0, 2).reshape(n_vec - D // LANE, D)
    conv_w_full = sharded_full[:3 * n_conv].reshape(n_conv, 3, D)
    conv_b_full = sharded_full[3 * n_conv:4 * n_conv]
    sg_vg_full = sharded_full[4 * n_conv:4 * n_conv + 1]

    c_pad = _pad_rows(c_all, LANE)
    ncol = ada_w.shape[2]
    mod_part = jnp.stack([_mm(c_pad, ada_w[i], name=f"ada_fwd{i}")[:N_DEV] for i in range(DEPTH)])
    mod_all = _all_gather(mod_part.reshape(1, DEPTH * N_DEV, ncol), 0, name="gather_mod")
    mod_all = mod_all.reshape(N_DEV, DEPTH, N_DEV, ncol)
    mod_me = lax.dynamic_index_in_dim(mod_all, me, 2, keepdims=False)
    mod = mod_me.transpose(1, 0, 2).reshape(DEPTH, N_DEV * ncol) + ada_b

    def gather_item(shard, axis):
        m = shard.shape[axis]
        full = tuple(N_DEV * s if a == axis else s for a, s in enumerate(shard.shape))
        zone = lax.dynamic_update_slice_in_dim(lax.empty(full, shard.dtype), shard, me * m, axis)
        return shard, zone, ('gather', axis, m)

    def mixer_shards(i):
        kind, j = i % 3, i // 3
        if kind == 0:
            return [(conv_w_in[j], 1), (conv_w_out[j], 0)]
        if kind == 1:
            return [(ssm_w_in[j], 0), (ssm_glu_w[j], 0), (ssm_w_out[j], 0)]
        return [(sg_w_in[j], 1), (sg_w_out[j], 0)]

    gathers = [_exchange_start([gather_item(w.astype(BF16), ax)
                                for w, ax in mixer_shards(i) + [(ff_w1[i], 1), (ff_w2[i], 0)]],
                               name=f"gather_start{i}") for i in range(DEPTH)]
    mod = mod + sum(g[4][0:1, 0:1] for g in gathers)

    s5_args = (ssm_a_re[0], ssm_a_im[0], ssm_log_dt[0], ssm_b_re[0], ssm_b_im[0], ssm_c_re[0], ssm_c_im[0])
    (abar_re, abar_im, bblk_re, bblk_im, cblk_re, cblk_im), s5_vjp = jax.vjp(_s5_prep, *s5_args)
    pw_fwd, pw_rev = _s5_power_tables(abar_re, abar_im)
    s5_w = tuple(t.astype(BF16) for t in (bblk_re, bblk_im, cblk_re, cblk_im))
    causal = jnp.tril(jnp.ones((SG_CHUNK, SG_CHUNK), dtype=bool))
    ws_m = jnp.where(causal[None], sg_w_s[0], 0.0)
    ws_b = ws_m.astype(BF16)
    wst_b = ws_m.transpose(0, 2, 1).astype(BF16)
    bsb = jnp.broadcast_to(sg_b_s[0][:, :, None], (SG_HEADS, SG_CHUNK, LANE))

    saved = []
    xa = xs
    for i in range(DEPTH):
        kind, j = i % 3, i // 3
        sh1, sc1, g1, sh2, sc2, g2 = (mod[i:i + 1, q * D:(q + 1) * D] for q in range(6))
        wn1 = norm1_g[i:i + 1] * (1.0 + sc1)
        wn2 = norm2_g[i:i + 1] * (1.0 + sc2)
        S = dict(x_in=xa, g1=g1, g2=g2, sc1=sc1, sc2=sc2, wn1=wn1, wn2=wn2)
        *w_mix, w1_full, w2_full = _exchange_wait(gathers[i], xa if i else mod, name=f"gather_wait{i}")
        S.update(w_mix=w_mix, w1=w1_full, w2=w2_full)
        h1 = _normmod_fwd(xa, wn1, sh1, name=f"norm1_fwd{i}")
        S['h1'] = h1
        if kind == 0:
            bcx = _mm(h1, w_mix[0], name=f"conv_in{i}", bm=2048)
            wb = _pad_rows(jnp.concatenate([conv_w_full[j], conv_b_full[j:j + 1]], 0), SUBLANE)
            pb = _conv_fwd(bcx, wb, name=f"conv_mix{i}")
            S.update(bcx=bcx, wb=wb, pb=pb)
        elif kind == 1:
            u = _mm(h1, w_mix[0], name=f"ssm_in{i}")
            sre, sim, ypre, yg = _s5_fwd(u, *s5_w, pw_fwd, ssm_d, name=f"s5_scan{i}")

            def glu_epi(acc, yv, bias):
                t = acc + bias
                return yv * (1.0 / (1.0 + jnp.exp(-t))), t

            pb, tt = _mm(yg, w_mix[1], name=f"ssm_glu{i}", out_dtypes=(BF16, F32), epi=glu_epi,
                         extras=[(yg, 'mn'), (ssm_glu_b, 'n')])
            S.update(u=u, sre=sre, sim=sim, ypre=ypre, yg=yg, pb=pb, tt=tt)
        else:
            uv = _mm(h1, w_mix[0], name=f"sg_in{i}", bm=2048)
            pb = _sg_fwd(uv, sg_vg_full, ws_b, bsb, name=f"sg_mix{i}")
            S.update(uv=uv, pb=pb)
        x_mid, y_mix = _mm(pb, w_mix[-1], name=f"mix_out{i}", out_dtypes=(F32, BF16), epi=_epi_residual,
                           extras=[(xa, 'mn'), (g1, 'n')])
        h2 = _normmod_fwd(x_mid, wn2, sh2, name=f"norm2_fwd{i}")
        ra = _mm(h2, w1_full, name=f"ff_up{i}", out_dtypes=(BF16,), epi=lambda acc: (jnp.maximum(acc, 0.0),), bm=2048)
        xa, f_out = _mm(ra, w2_full, name=f"ff_down{i}", out_dtypes=(F32, BF16), epi=_epi_residual, a_fn=_square,
                        extras=[(x_mid, 'mn'), (g2, 'n')])
        S.update(x_mid=x_mid, y_mix=y_mix, h2=h2, ra=ra, f_out=f_out)
        saved.append(S)

    S = saved[-1]
    dx, st, dfb, loss_tile = _loss_head(xa, tgt, final_g[None], S['f_out'], S['g2'], name="loss_head")
    loss = lax.psum(loss_tile[0, 0], ("x", "y", "c"))
    d_final_g = _stat_row(st, 0)
    dg2_next = _stat_row(st, 2)

    def scatter_item(g, axis):
        m = g.shape[axis] // N_DEV
        own = lax.dynamic_slice_in_dim(g, me * m, m, axis)
        zone = lax.dynamic_update_slice_in_dim(lax.empty((N_DEV,) + own.shape, g.dtype), own[None], me, 0)
        return g, zone, ('scatter', axis, m)

    dmod = [None] * DEPTH
    dn1g, dn2g = [None] * DEPTH, [None] * DEPTH
    d_conv_w, d_conv_b = [None] * n_conv, [None] * n_conv
    ff_sent, mix_sent = [None] * DEPTH, [None] * DEPTH
    small = {}
    for i in reversed(range(DEPTH)):
        kind, j = i % 3, i // 3
        S = saved[i]
        w_mix = S['w_mix']
        dg2 = dg2_next
        da = _mm(dfb, S['w2'], tb=True, name=f"ff_down_bwd{i}", out_dtypes=(BF16,), bm=2048,
                 epi=lambda acc, rav: (acc * (2.0 * rav.astype(F32)),), extras=[(S['ra'], 'mn')])
        dw2 = _mm(S['ra'], dfb, ta=True, name=f"ff_w2_grad{i}", out_dtypes=(BF16,), a_fn=_square)
        dh2 = _mm(da, S['w1'], tb=True, name=f"ff_up_bwd{i}", out_dtypes=(BF16,))
        dw1 = _mm(S['h2'], da, ta=True, name=f"ff_w1_grad{i}", out_dtypes=(BF16,), bn=2048)
        ff_sent[i] = _exchange_start([scatter_item(dw1, 1), scatter_item(dw2, 0)], name=f"ff_grads_start{i}")
        dx_mid, st2, dyb = _normmod_bwd(dh2, S['x_mid'], S['wn2'] + ff_sent[i][4][0:1, 0:1], dx,
                                        (S['y_mix'], S['g1']), name=f"norm2_bwd{i}")
        dsc2 = _stat_row(st2, 0) * norm2_g[i:i + 1]
        dn2g[i] = _stat_row(st2, 0) * (1.0 + S['sc2'])
        dsh2 = _stat_row(st2, 1)
        dg1 = _stat_row(st2, 2)
        if kind == 0:
            dp = _mm(dyb, w_mix[1], tb=True, name=f"conv_out_bwd{i}")
            d_cwo = _mm(S['pb'], dyb, ta=True, name=f"conv_w_out_grad{i}", out_dtypes=(BF16,))
            db, dc, dxh, stc = _conv_bwd(dp, S['bcx'], S['wb'], name=f"conv_mix_bwd{i}")
            dbcx = jnp.concatenate([db, dc, dxh], 1)
            d_conv_w[j] = stc[0:3]
            d_conv_b[j] = stc[3:4]
            dh1 = _mm(dbcx, w_mix[0], tb=True, name=f"conv_in_bwd{i}", out_dtypes=(BF16,))
            d_cwi = _mm(S['h1'], dbcx, ta=True, name=f"conv_w_in_grad{i}", out_dtypes=(BF16,))
            mix_grads = [scatter_item(d_cwi, 1), scatter_item(d_cwo, 0)]
        elif kind == 1:
            dy2 = _mm(dyb, w_mix[2], tb=True, name=f"ssm_out_bwd{i}")
            d_ssm_out = _mm(S['pb'], dyb, ta=True, name=f"ssm_w_out_grad{i}", out_dtypes=(BF16,))
            dtb, dya, stg = _glu_bwd(dy2, S['yg'], S['tt'], name=f"ssm_glu_bwd{i}")
            dypre = _mm(dtb, w_mix[1], tb=True, name=f"ssm_glu_in_bwd{i}",
                        epi=lambda acc, a, yp: ((a + acc) * _gelu_grad(yp),),
                        extras=[(dya, 'mn'), (S['ypre'], 'mn')])
            d_glu = _mm(S['yg'], dtb, ta=True, name=f"ssm_glu_w_grad{i}", out_dtypes=(BF16,))
            dub, dbre, dbim, dcre, dcim, ga, dd = _s5_bwd(dypre, S['u'], S['sre'], S['sim'], *s5_w, pw_rev, ssm_d,
                                                           name=f"s5_scan_bwd{i}")
            dh1 = _mm(dub, w_mix[0], tb=True, name=f"ssm_in_bwd{i}", out_dtypes=(BF16,))
            d_ssm_in = _mm(S['h1'], dub, ta=True, name=f"ssm_w_in_grad{i}", out_dtypes=(BF16,))
            da_re, da_im, dlog_dt, db_re, db_im, dc_re, dc_im = s5_vjp((ga[0:1], ga[1:2], dbre, dbim, dcre, dcim))
            small.update(ssm_a_re=da_re, ssm_a_im=da_im, ssm_log_dt=dlog_dt, ssm_b_re=db_re, ssm_b_im=db_im,
                         ssm_c_re=dc_re, ssm_c_im=dc_im, ssm_d=dd[0], ssm_glu_b=stg[0])
            mix_grads = [scatter_item(d_ssm_in, 0), scatter_item(d_glu, 0), scatter_item(d_ssm_out, 0)]
        else:
            dp = _mm(dyb, w_mix[1], tb=True, name=f"sg_out_bwd{i}")
            d_sgo = _mm(S['pb'], dyb, ta=True, name=f"sg_w_out_grad{i}", out_dtypes=(BF16,))
            duv, dws, dbs, stv = _sg_bwd(dp, S['uv'], sg_vg_full, ws_b, wst_b, bsb, name=f"sg_mix_bwd{i}")
            dh1 = _mm(duv, w_mix[0], tb=True, name=f"sg_in_bwd{i}", out_dtypes=(BF16,))
            d_sgi = _mm(S['h1'], duv, ta=True, name=f"sg_w_in_grad{i}", out_dtypes=(BF16,))
            small.update(sg_w_s=jnp.where(causal[None], dws, 0.0), sg_b_s=jnp.sum(dbs, axis=-1))
            d_sg_vg = stv[0:1]
            mix_grads = [scatter_item(d_sgi, 1), scatter_item(d_sgo, 0)]
        mix_sent[i] = _exchange_start(mix_grads, name=f"mix_grads_start{i}")
        wn1 = S['wn1'] + mix_sent[i][4][0:1, 0:1]
        if i > 0:
            prev = saved[i - 1]
            dx, st1, dfb = _normmod_bwd(dh1, S['x_in'], wn1, dx_mid, (prev['f_out'], prev['g2']),
                                        name=f"norm1_bwd{i}")
            dg2_next = _stat_row(st1, 2)
        else:
            dx, st1 = _normmod_bwd(dh1, S['x_in'], wn1, dx_mid, None, name=f"norm1_bwd{i}")
        dsc1 = _stat_row(st1, 0) * norm1_g[i:i + 1]
        dn1g[i] = _stat_row(st1, 0) * (1.0 + S['sc1'])
        dsh1 = _stat_row(st1, 1)
        dmod[i] = jnp.concatenate([dsh1, dsc1, dg1, dsh2, dsc2, dg2], 1)
    grad_x = dx[None]

    small.update(ada_b=jnp.concatenate(dmod, 0), norm1_g=jnp.concatenate(dn1g, 0), norm2_g=jnp.concatenate(dn2g, 0),
                 final_g=d_final_g, conv_w=jnp.stack(d_conv_w), conv_b=jnp.concatenate(d_conv_b, 0), sg_v_g=d_sg_vg)
    order = SMALL_REPL + SMALL_SHARD
    sizes = [math.prod(P[n].shape) * (N_DEV if n in SMALL_SHARD else 1) for n in order]
    n_repl = sum(sizes[:len(SMALL_REPL)])
    total = sum(sizes)
    pack_rows = -(-total // (LANE * 512)) * 512
    flat = jnp.concatenate([small[n].reshape(-1).astype(F32) for n in order])
    pack = jnp.pad(flat, (0, pack_rows * LANE - total)).reshape(1, pack_rows, LANE)
    pack_all = _all_gather(pack, 0, name="gather_small_grads")

    def pack_repl(prefix):
        v = jnp.concatenate([P[prefix + n].reshape(-1) for n in SMALL_REPL])
        return jnp.pad(v, (0, pack_rows * LANE - n_repl)).reshape(1, pack_rows, LANE)

    rg, rd, rm, rv = _adamw(pack_repl(''), [pack_all], pack_repl('m_'), pack_repl('v_'), name="adamw_small")
    out = {}
    off = 0
    for n, sz in zip(SMALL_REPL, sizes):
        out[n] = tuple(t.reshape(-1)[off:off + sz].reshape(P[n].shape) for t in (rg, rd, rm, rv))
        off += sz
    sh_rows = (total - n_repl) // D
    sh_parts = pack_all.reshape(N_DEV, pack_rows * LANE)[:, n_repl:total].reshape(N_DEV, sh_rows, D)
    sh_parts = lax.dynamic_slice_in_dim(sh_parts, me * LANE, LANE, 2)
    sh_parts = jnp.pad(sh_parts, ((0, 0), (0, 16 - sh_rows), (0, 0)))

    def pack_shard(prefix):
        return _pad_rows(jnp.concatenate([P[prefix + n].reshape(-1, LANE) for n in SMALL_SHARD], 0), 16)[None]

    sg_, sd_, sm_, sv_ = _adamw(pack_shard(''), [sh_parts], pack_shard('m_'), pack_shard('v_'), name="adamw_channel")
    off = 0
    for n in SMALL_SHARD:
        rows = math.prod(P[n].shape) // LANE
        out[n] = tuple(t[0, off:off + rows].reshape(P[n].shape) for t in (sg_, sd_, sm_, sv_))
        off += rows

    dmod_all = pack_all.reshape(N_DEV, pack_rows * LANE)[:, :DEPTH * 6 * D].reshape(N_DEV, DEPTH, 6 * D)
    dmod_cols = lax.dynamic_slice_in_dim(dmod_all, me * ncol, ncol, 2)
    g_ada = [_mm(c_pad, _pad_rows(dmod_cols[:, i], LANE), ta=True, name=f"ada_w_grad{i}")[None] for i in range(DEPTH)]

    def big(name, parts):
        res = _adamw(P[name], parts, P['m_' + name], P['v_' + name], name="adamw_" + name)
        out[name] = res
        return res[1]

    ff_parts = [_exchange_wait(ff_sent[i], dx, name=f"ff_grads_wait{i}") for i in range(DEPTH)]
    mix_parts = [None] + [_exchange_wait(mix_sent[i], dx, name=f"mix_grads_wait{i}") for i in range(1, DEPTH)]
    big('ada_w', g_ada)
    big('ff_w1', [p[0] for p in ff_parts])
    big('ff_w2', [p[1] for p in ff_parts])
    done = big('sg_w_in', [mix_parts[2][0]])
    mix_parts[0] = _exchange_wait(mix_sent[0], done, name="mix_grads_wait0")
    big('conv_w_in', [mix_parts[i][0] for i in range(DEPTH) if i % 3 == 0])
    row_names = ['conv_w_out', 'ssm_w_in', 'ssm_glu_w', 'ssm_w_out', 'sg_w_out']
    row_parts = ([mix_parts[i][1] for i in range(DEPTH) if i % 3 == 0] + mix_parts[1] + [mix_parts[2][1]])
    row_w, row_m, row_v = (jnp.concatenate([P[pre + n] for n in row_names], 0) for pre in ('', 'm_', 'v_'))
    rw = _adamw(row_w, row_parts, row_m, row_v, name="adamw_row_sharded")
    off = 0
    for n in row_names:
        cnt = P[n].shape[0]
        out[n] = tuple(t[off:off + cnt] for t in rw)
        off += cnt

    return (loss, grad_x, *[out[n][0] for n in WEIGHTS], *[out[n][1] for n in WEIGHTS],
            *[out[n][2] for n in WEIGHTS], *[out[n][3] for n in WEIGHTS])
```

```python
import math

import jax
import jax.numpy as jnp
from jax import lax
from jax.experimental import pallas as pl
from jax.experimental.pallas import tpu as pltpu

F32 = jnp.float32
BF16 = jnp.bfloat16

N_DEV = 8
MESH_ID = pl.DeviceIdType.MESH
DEPTH = 4
EPS = 1e-6
S5_GROUPS, S5_GROUP, S5_STATE = 64, 16, 64
S5_LANES = S5_GROUPS * S5_STATE
S5_BLOCKS = 8
SG_HEADS, SG_CHUNK = 8, 128
LANE = 128
SUBLANE = 8
VMEM_LIMIT = 48 * 1024 * 1024
ADAM_LR, ADAM_B1, ADAM_B2, ADAM_EPS, ADAM_WD, ADAM_STEP = 0.001, 0.9, 0.999, 1e-08, 0.01, 10
GELU_C = math.sqrt(2.0 / math.pi)
GELU_A = 0.044715

WEIGHTS = ['ada_w', 'ada_b', 'norm1_g', 'norm2_g', 'ff_w1', 'ff_w2', 'final_g', 'conv_w_in', 'conv_w', 'conv_b',
           'conv_w_out', 'ssm_w_in', 'ssm_a_re', 'ssm_a_im', 'ssm_log_dt', 'ssm_b_re', 'ssm_b_im', 'ssm_c_re',
           'ssm_c_im', 'ssm_d', 'ssm_glu_w', 'ssm_glu_b', 'ssm_w_out', 'sg_w_in', 'sg_v_g', 'sg_w_s', 'sg_b_s',
           'sg_w_out']
INPUTS = ['x', 'c'] + WEIGHTS + ['loss_target'] + ['m_' + n for n in WEIGHTS] + ['v_' + n for n in WEIGHTS]
SMALL_REPL = ['ada_b', 'norm1_g', 'norm2_g', 'final_g', 'ssm_a_re', 'ssm_a_im', 'ssm_log_dt', 'ssm_b_re', 'ssm_b_im',
              'ssm_c_re', 'ssm_c_im', 'ssm_d', 'ssm_glu_b', 'sg_w_s', 'sg_b_s']
SMALL_SHARD = ['conv_w', 'conv_b', 'sg_v_g']


def _params(*sem):
    return pltpu.CompilerParams(dimension_semantics=sem or None, vmem_limit_bytes=VMEM_LIMIT)


def _my_pos():
    return lax.axis_index("x"), lax.axis_index("y"), lax.axis_index("c")


def _my_index():
    x, y, c = _my_pos()
    return 4 * x + 2 * y + c


def _mm(a, b, *, name, ta=False, tb=False, out_dtypes=(F32,), epi=None, extras=(), a_fn=None, bm=1024, bn=1024,
        bk=1024):
    m, k = (a.shape[1], a.shape[0]) if ta else a.shape
    k2, n = (b.shape[1], b.shape[0]) if tb else b.shape
    assert k == k2, (a.shape, b.shape, ta, tb)
    bm, bn, bk = min(bm, m), min(bn, n), min(bk, k)
    assert m % bm == 0 and n % bn == 0 and k % bk == 0, (m, n, k, bm, bn, bk)
    nk = k // bk
    n_ex, n_out = len(extras), len(out_dtypes)
    dims = (((0 if ta else 1,), (1 if tb else 0,)), ((), ()))

    def body(*refs):
        a_ref, b_ref = refs[0], refs[1]
        ex_refs = refs[2:2 + n_ex]
        out_refs = refs[2 + n_ex:2 + n_ex + n_out]

        def finish(acc):
            outs = epi(acc, *[r[...] for r in ex_refs]) if epi is not None else (acc,)
            for r, o in zip(out_refs, outs):
                r[...] = o.astype(r.dtype)

        av = a_ref[...] if a_fn is None else a_fn(a_ref[...])
        part = lax.dot_general(av.astype(BF16), b_ref[...].astype(BF16), dims, preferred_element_type=F32)
        if nk == 1:
            finish(part)
            return
        acc_ref = refs[-1]
        kk = pl.program_id(2)

        @pl.when(kk == 0)
        def _():
            acc_ref[...] = part

        @pl.when(kk > 0)
        def _():
            acc_ref[...] += part

        @pl.when(kk == nk - 1)
        def _():
            finish(acc_ref[...])

    a_spec = pl.BlockSpec((bk, bm), lambda i, j, q: (q, i)) if ta else pl.BlockSpec((bm, bk), lambda i, j, q: (i, q))
    b_spec = pl.BlockSpec((bn, bk), lambda i, j, q: (j, q)) if tb else pl.BlockSpec((bk, bn), lambda i, j, q: (q, j))
    ex_specs = []
    for arr, kind in extras:
        if kind == 'mn':
            assert arr.shape == (m, n), (arr.shape, m, n)
            ex_specs.append(pl.BlockSpec((bm, bn), lambda i, j, q: (i, j)))
        else:
            assert arr.shape == (1, n), (arr.shape, n)
            ex_specs.append(pl.BlockSpec((1, bn), lambda i, j, q: (0, j)))
    outs = pl.pallas_call(
        body, name=name,
        out_shape=tuple(jax.ShapeDtypeStruct((m, n), d) for d in out_dtypes),
        grid=(m // bm, n // bn, nk),
        in_specs=[a_spec, b_spec] + ex_specs,
        out_specs=tuple(pl.BlockSpec((bm, bn), lambda i, j, q: (i, j)) for _ in out_dtypes),
        scratch_shapes=[pltpu.VMEM((bm, bn), F32)] if nk > 1 else [],
        compiler_params=_params("parallel", "parallel", "arbitrary"),
    )(a, b, *[arr for arr, _ in extras])
    return outs if n_out > 1 else outs[0]


def _epi_residual(acc, res, gate):
    return res + gate * acc, acc


def _wgrad(acts, cots, *, name, a_fn=None, bm=1024, bn=512):
    return _mm(acts, cots, ta=True, name=name, out_dtypes=(BF16,), a_fn=a_fn, bm=bm, bn=bn, bk=acts.shape[0])


def _square(a):
    af = a.astype(F32)
    return af * af


def _rstd(xv):
    return lax.rsqrt(jnp.mean(xv * xv, axis=-1, keepdims=True) + EPS)


def _normmod_fwd(x, w, sh, *, name, tm=512):
    L, D = x.shape

    def body(x_ref, w_ref, s_ref, h_ref):
        xv = x_ref[...]
        h_ref[...] = (xv * _rstd(xv) * w_ref[...] + s_ref[...]).astype(h_ref.dtype)

    row = pl.BlockSpec((tm, D), lambda i: (i, 0))
    vec = pl.BlockSpec((1, D), lambda i: (0, 0))
    return pl.pallas_call(body, name=name, out_shape=jax.ShapeDtypeStruct((L, D), BF16), grid=(L // tm,),
                          in_specs=[row, vec, vec], out_specs=row, compiler_params=_params("parallel"))(x, w, sh)


def _normmod_bwd(dh, x, w, dres, gate, *, name, tm=256):
    L, D = x.shape
    has_gate = gate is not None

    def body(*refs):
        if has_gate:
            dh_ref, x_ref, w_ref, r_ref, y_ref, g_ref, dx_ref, st_ref, dy_ref = refs
        else:
            dh_ref, x_ref, w_ref, r_ref, dx_ref, st_ref = refs
        i = pl.program_id(0)

        @pl.when(i == 0)
        def _():
            st_ref[...] = jnp.zeros_like(st_ref)

        xv = x_ref[...]
        dhv = dh_ref[...].astype(F32)
        rstd = _rstd(xv)
        xn = xv * rstd
        dxn = dhv * w_ref[...]
        dx = rstd * (dxn - xn * jnp.mean(dxn * xn, axis=-1, keepdims=True)) + r_ref[...]
        dx_ref[...] = dx
        st_ref[0:1, :] += jnp.sum(dhv * xn, axis=0, keepdims=True)
        st_ref[1:2, :] += jnp.sum(dhv, axis=0, keepdims=True)
        if has_gate:
            dy_ref[...] = (dx * g_ref[...]).astype(dy_ref.dtype)
            st_ref[2:3, :] += jnp.sum(dx * y_ref[...].astype(F32), axis=0, keepdims=True)

    row = pl.BlockSpec((tm, D), lambda i: (i, 0))
    vec = pl.BlockSpec((1, D), lambda i: (0, 0))
    st = pl.BlockSpec((SUBLANE, D), lambda i: (0, 0))
    in_specs = [row, row, vec, row] + ([row, vec] if has_gate else [])
    out_shape = [jax.ShapeDtypeStruct((L, D), F32), jax.ShapeDtypeStruct((SUBLANE, D), F32)]
    out_specs = [row, st]
    if has_gate:
        out_shape.append(jax.ShapeDtypeStruct((L, D), BF16))
        out_specs.append(row)
    args = (dh, x, w, dres) + (tuple(gate) if has_gate else ())
    return pl.pallas_call(body, name=name, out_shape=tuple(out_shape), grid=(L // tm,), in_specs=in_specs,
                          out_specs=tuple(out_specs), compiler_params=_params("arbitrary"))(*args)


def _loss_head(x, tgt, fg, y, g, *, name, tm=256):
    L, D = x.shape

    def body(x_ref, t_ref, fg_ref, y_ref, g_ref, dx_ref, st_ref, dy_ref, loss_ref):
        i = pl.program_id(0)

        @pl.when(i == 0)
        def _():
            st_ref[...] = jnp.zeros_like(st_ref)
            loss_ref[...] = jnp.zeros_like(loss_ref)

        xv = x_ref[...]
        rstd = _rstd(xv)
        xn = xv * rstd
        err = xn * fg_ref[...] - t_ref[...]
        loss_ref[...] += 0.5 * jnp.sum(jnp.mean(err * err, axis=-1, keepdims=True))
        dout = err * (1.0 / D)
        dxn = dout * fg_ref[...]
        dx = rstd * (dxn - xn * jnp.mean(dxn * xn, axis=-1, keepdims=True))
        dx_ref[...] = dx
        dy_ref[...] = (dx * g_ref[...]).astype(dy_ref.dtype)
        st_ref[0:1, :] += jnp.sum(dout * xn, axis=0, keepdims=True)
        st_ref[2:3, :] += jnp.sum(dx * y_ref[...].astype(F32), axis=0, keepdims=True)

    row = pl.BlockSpec((tm, D), lambda i: (i, 0))
    vec = pl.BlockSpec((1, D), lambda i: (0, 0))
    return pl.pallas_call(
        body, name=name,
        out_shape=(jax.ShapeDtypeStruct((L, D), F32), jax.ShapeDtypeStruct((SUBLANE, D), F32),
                   jax.ShapeDtypeStruct((L, D), BF16), jax.ShapeDtypeStruct((SUBLANE, LANE), F32)),
        grid=(L // tm,), in_specs=[row, row, vec, row, vec],
        out_specs=(row, pl.BlockSpec((SUBLANE, D), lambda i: (0, 0)), row,
                   pl.BlockSpec((SUBLANE, LANE), lambda i: (0, 0))),
        compiler_params=_params("arbitrary"))(x, tgt, fg, y, g)


def _shift_down(v, k):
    row = lax.broadcasted_iota(jnp.int32, v.shape, 0)
    return jnp.where(row >= k, pltpu.roll(v, k, 0), 0.0)


def _shift_up(v, k):
    n = v.shape[0]
    row = lax.broadcasted_iota(jnp.int32, v.shape, 0)
    return jnp.where(row < n - k, pltpu.roll(v, n - k, 0), 0.0)


def _conv_views(L, D):
    return [pl.BlockSpec((L, LANE), lambda j, s=s: (0, s * (D // LANE) + j)) for s in range(3)]


def _conv_fwd(bcx, wb, *, name):
    L, D = bcx.shape[0], bcx.shape[1] // 3

    def body(b_ref, c_ref, x_ref, wb_ref, p_ref):
        z = c_ref[...] * x_ref[...]
        conv = (wb_ref[0:1, :] * _shift_down(z, 2) + wb_ref[1:2, :] * _shift_down(z, 1)
                + wb_ref[2:3, :] * z + wb_ref[3:4, :])
        p_ref[...] = (b_ref[...] * conv).astype(p_ref.dtype)

    col = pl.BlockSpec((L, LANE), lambda j: (0, j))
    return pl.pallas_call(body, name=name, out_shape=jax.ShapeDtypeStruct((L, D), BF16), grid=(D // LANE,),
                          in_specs=_conv_views(L, D) + [pl.BlockSpec((SUBLANE, LANE), lambda j: (0, j))],
                          out_specs=col, compiler_params=_params("parallel"))(bcx, bcx, bcx, wb)


def _conv_bwd(dp, bcx, wb, *, name):
    L, D = dp.shape

    def body(dp_ref, b_ref, c_ref, x_ref, wb_ref, db_ref, dc_ref, dxh_ref, st_ref):
        cv, xv = c_ref[...], x_ref[...]
        z = cv * xv
        z1, z2 = _shift_down(z, 1), _shift_down(z, 2)
        w0, w1, w2 = wb_ref[0:1, :], wb_ref[1:2, :], wb_ref[2:3, :]
        conv = w0 * z2 + w1 * z1 + w2 * z + wb_ref[3:4, :]
        dpv = dp_ref[...]
        db_ref[...] = (dpv * conv).astype(db_ref.dtype)
        dconv = dpv * b_ref[...]
        dz = w2 * dconv + w1 * _shift_up(dconv, 1) + w0 * _shift_up(dconv, 2)
        dc_ref[...] = (dz * xv).astype(dc_ref.dtype)
        dxh_ref[...] = (dz * cv).astype(dxh_ref.dtype)
        st_ref[...] = jnp.zeros_like(st_ref)
        st_ref[0:1, :] = jnp.sum(dconv * z2, axis=0, keepdims=True)
        st_ref[1:2, :] = jnp.sum(dconv * z1, axis=0, keepdims=True)
        st_ref[2:3, :] = jnp.sum(dconv * z, axis=0, keepdims=True)
        st_ref[3:4, :] = jnp.sum(dconv, axis=0, keepdims=True)

    col = pl.BlockSpec((L, LANE), lambda j: (0, j))
    vec = pl.BlockSpec((SUBLANE, LANE), lambda j: (0, j))
    act = jax.ShapeDtypeStruct((L, D), BF16)
    return pl.pallas_call(body, name=name, out_shape=(act, act, act, jax.ShapeDtypeStruct((SUBLANE, D), F32)),
                          grid=(D // LANE,), in_specs=[col] + _conv_views(L, D) + [vec],
                          out_specs=(col, col, col, vec), compiler_params=_params("parallel"))(dp, bcx, bcx, bcx, wb)


def _sg_fwd(uv, vg, ws, bsb, *, name, tr=512):
    L, D = uv.shape[0], uv.shape[1] // 2

    def body(uv_ref, vg_ref, ws_ref, bsb_ref, p_ref):
        for ci in range(tr // SG_CHUNK):
            rows = slice(ci * SG_CHUNK, (ci + 1) * SG_CHUNK)
            v = uv_ref[rows, D:2 * D]
            vn = (v * _rstd(v) * vg_ref[...]).astype(BF16)
            for h in range(SG_HEADS):
                cols = slice(h * LANE, (h + 1) * LANE)
                vm = jnp.dot(ws_ref[h], vn[:, cols], preferred_element_type=F32) + bsb_ref[h]
                p_ref[rows, cols] = (uv_ref[rows, cols] * vm).astype(p_ref.dtype)

    full3 = pl.BlockSpec((SG_HEADS, SG_CHUNK, LANE), lambda i: (0, 0, 0))
    return pl.pallas_call(body, name=name, out_shape=jax.ShapeDtypeStruct((L, D), BF16), grid=(L // tr,),
                          in_specs=[pl.BlockSpec((tr, 2 * D), lambda i: (i, 0)), pl.BlockSpec((1, D), lambda i: (0, 0)),
                                    full3, full3],
                          out_specs=pl.BlockSpec((tr, D), lambda i: (i, 0)),
                          compiler_params=_params("parallel"))(uv, vg, ws, bsb)


def _sg_bwd(dp, uv, vg, ws, wst, bsb, *, name, tr=512):
    L, D = dp.shape

    def body(dp_ref, uv_ref, vg_ref, ws_ref, wst_ref, bsb_ref, duv_ref, dws_ref, dbs_ref, st_ref, dvn_ref):
        i = pl.program_id(0)

        @pl.when(i == 0)
        def _():
            dws_ref[...] = jnp.zeros_like(dws_ref)
            dbs_ref[...] = jnp.zeros_like(dbs_ref)
            st_ref[...] = jnp.zeros_like(st_ref)

        for ci in range(tr // SG_CHUNK):
            rows = slice(ci * SG_CHUNK, (ci + 1) * SG_CHUNK)
            v = uv_ref[rows, D:2 * D]
            rstd = _rstd(v)
            vhat = v * rstd
            vn = (vhat * vg_ref[...]).astype(BF16)
            for h in range(SG_HEADS):
                cols = slice(h * LANE, (h + 1) * LANE)
                vm = jnp.dot(ws_ref[h], vn[:, cols], preferred_element_type=F32) + bsb_ref[h]
                dph = dp_ref[rows, cols]
                duv_ref[rows, cols] = (dph * vm).astype(duv_ref.dtype)
                dvm = dph * uv_ref[rows, cols]
                dbs_ref[h] += dvm
                dvmb = dvm.astype(BF16)
                dws_ref[h] += lax.dot_general(dvmb, vn[:, cols], (((1,), (1,)), ((), ())),
                                              preferred_element_type=F32)
                dvn_ref[rows, cols] = jnp.dot(wst_ref[h], dvmb, preferred_element_type=F32)
            dvn = dvn_ref[rows, :]
            gv = dvn * vg_ref[...]
            dv = rstd * (gv - vhat * jnp.mean(gv * vhat, axis=-1, keepdims=True))
            duv_ref[rows, D:2 * D] = dv.astype(duv_ref.dtype)
            st_ref[0:1, :] += jnp.sum(dvn * vhat, axis=0, keepdims=True)

    full3 = pl.BlockSpec((SG_HEADS, SG_CHUNK, LANE), lambda i: (0, 0, 0))
    acc3 = jax.ShapeDtypeStruct((SG_HEADS, SG_CHUNK, LANE), F32)
    return pl.pallas_call(
        body, name=name,
        out_shape=(jax.ShapeDtypeStruct((L, 2 * D), BF16), acc3, acc3, jax.ShapeDtypeStruct((SUBLANE, D), F32)),
        grid=(L // tr,),
        in_specs=[pl.BlockSpec((tr, D), lambda i: (i, 0)), pl.BlockSpec((tr, 2 * D), lambda i: (i, 0)),
                  pl.BlockSpec((1, D), lambda i: (0, 0)), full3, full3, full3],
        out_specs=(pl.BlockSpec((tr, 2 * D), lambda i: (i, 0)), full3, full3,
                   pl.BlockSpec((SUBLANE, D), lambda i: (0, 0))),
        scratch_shapes=[pltpu.VMEM((tr, D), F32)],
        compiler_params=_params("arbitrary"))(dp, uv, vg, ws, wst, bsb)


def _gelu(x):
    return 0.5 * x * (1.0 + jnp.tanh(GELU_C * (x + GELU_A * x * x * x)))


def _gelu_grad(x):
    th = jnp.tanh(GELU_C * (x + GELU_A * x * x * x))
    return 0.5 * (1.0 + th) + 0.5 * x * (1.0 - th * th) * GELU_C * (1.0 + 3.0 * GELU_A * x * x)


def _cmul_add(xr, xi, ar, ai, br, bi):
    return xr + ar * br - ai * bi, xi + ar * bi + ai * br


def _s5_fwd(u, bre, bim, cre, cim, pw, dsk, *, name, tc=512):
    L, D = u.shape
    W = S5_LANES // S5_BLOCKS
    nt = L // tc

    def body(u_ref, bre_ref, bim_ref, cre_ref, cim_ref, pw_ref, d_ref, sre_ref, sim_ref, ypre_ref, yg_ref, carry):
        t = pl.program_id(1)

        @pl.when(t == 0)
        def _():
            carry[...] = jnp.zeros_like(carry)

        uv = u_ref[...]
        ub = uv.astype(BF16)
        sre_ref[...] = jnp.dot(ub, bre_ref[...], preferred_element_type=F32)
        sim_ref[...] = jnp.dot(ub, bim_ref[...], preferred_element_type=F32)

        def tile(i, c):
            cr, ci = c
            rows = pl.ds(pl.multiple_of(i * SUBLANE, SUBLANE), SUBLANE)
            xr, xi = sre_ref[rows, :], sim_ref[rows, :]
            for k, d in enumerate((1, 2, 4)):
                xr, xi = _cmul_add(xr, xi, pw_ref[2 * k], pw_ref[2 * k + 1], pltpu.roll(xr, d, 0),
                                   pltpu.roll(xi, d, 0))
            xr, xi = _cmul_add(xr, xi, pw_ref[6], pw_ref[7], cr, ci)
            sre_ref[rows, :] = xr
            sim_ref[rows, :] = xi
            last = slice(SUBLANE - 1, SUBLANE)
            return jnp.broadcast_to(xr[last, :], (SUBLANE, W)), jnp.broadcast_to(xi[last, :], (SUBLANE, W))

        cr, ci = lax.fori_loop(0, tc // SUBLANE, tile, (carry[0], carry[1]))
        carry[0] = cr
        carry[1] = ci
        y = (jnp.dot(sre_ref[...].astype(BF16), cre_ref[...], preferred_element_type=F32)
             - jnp.dot(sim_ref[...].astype(BF16), cim_ref[...], preferred_element_type=F32) + d_ref[...] * uv)
        ypre_ref[...] = y
        yg_ref[...] = _gelu(y)

    ch = pl.BlockSpec((tc, LANE), lambda j, t: (t, j))
    st = pl.BlockSpec((tc, W), lambda j, t: (t, j))
    bsp = pl.BlockSpec((None, LANE, W), lambda j, t: (j, 0, 0))
    csp = pl.BlockSpec((None, W, LANE), lambda j, t: (j, 0, 0))
    return pl.pallas_call(
        body, name=name,
        out_shape=(jax.ShapeDtypeStruct((L, S5_LANES), F32), jax.ShapeDtypeStruct((L, S5_LANES), F32),
                   jax.ShapeDtypeStruct((L, D), F32), jax.ShapeDtypeStruct((L, D), F32)),
        grid=(S5_BLOCKS, nt),
        in_specs=[ch, bsp, bsp, csp, csp, pl.BlockSpec((8, SUBLANE, W), lambda j, t: (0, 0, j)),
                  pl.BlockSpec((1, LANE), lambda j, t: (0, j))],
        out_specs=(st, st, ch, ch),
        scratch_shapes=[pltpu.VMEM((2, SUBLANE, W), F32)],
        compiler_params=_params("parallel", "arbitrary"))(u, bre, bim, cre, cim, pw, dsk)


def _s5_bwd(dy, u, sre, sim, bre, bim, cre, cim, pwr, dsk, *, name, tc=512):
    L, D = u.shape
    W = S5_LANES // S5_BLOCKS
    nt = L // tc
    ntile = tc // SUBLANE
    nt_dims = (((1,), (1,)), ((), ()))
    tn_dims = (((0,), (0,)), ((), ()))

    def body(dy_ref, u_ref, sre_ref, sim_ref, bre_ref, bim_ref, cre_ref, cim_ref, pw_ref, d_ref,
             du_ref, dbre_ref, dbim_ref, dcre_ref, dcim_ref, ga_ref, dd_ref, gre, gim, carry, gacc):
        t = pl.program_id(1)

        @pl.when(t == 0)
        def _():
            for r in (carry, gacc, dbre_ref, dbim_ref, dcre_ref, dcim_ref, ga_ref, dd_ref):
                r[...] = jnp.zeros_like(r)

        dyv, uv = dy_ref[...], u_ref[...]
        dyb, ub = dyv.astype(BF16), uv.astype(BF16)
        gre[...] = lax.dot_general(dyb, cre_ref[...], nt_dims, preferred_element_type=F32)
        gim[...] = -lax.dot_general(dyb, cim_ref[...], nt_dims, preferred_element_type=F32)
        top = lax.broadcasted_iota(jnp.int32, (SUBLANE, W), 0) == SUBLANE - 1

        def tile(k, c):
            cr, ci = c
            rows = pl.ds(pl.multiple_of((ntile - 1 - k) * SUBLANE, SUBLANE), SUBLANE)
            xr, xi = gre[rows, :], gim[rows, :]
            for q, d in enumerate((1, 2, 4)):
                xr, xi = _cmul_add(xr, xi, pw_ref[2 * q], pw_ref[2 * q + 1], pltpu.roll(xr, SUBLANE - d, 0),
                                   pltpu.roll(xi, SUBLANE - d, 0))
            xr, xi = _cmul_add(xr, xi, pw_ref[6], pw_ref[7], cr, ci)
            gre[rows, :] = xr
            gim[rows, :] = xi
            nr = jnp.where(top, cr, pltpu.roll(xr, SUBLANE - 1, 0))
            ni = jnp.where(top, ci, pltpu.roll(xi, SUBLANE - 1, 0))
            sr, si = sre_ref[rows, :], sim_ref[rows, :]
            gacc[0] += sr * nr + si * ni
            gacc[1] += sr * ni - si * nr
            return jnp.broadcast_to(xr[0:1, :], (SUBLANE, W)), jnp.broadcast_to(xi[0:1, :], (SUBLANE, W))

        cr, ci = lax.fori_loop(0, ntile, tile, (carry[0], carry[1]))
        carry[0] = cr
        carry[1] = ci
        grb, gib = gre[...].astype(BF16), gim[...].astype(BF16)
        du = (lax.dot_general(grb, bre_ref[...], nt_dims, preferred_element_type=F32)
              + lax.dot_general(gib, bim_ref[...], nt_dims, preferred_element_type=F32) + d_ref[...] * dyv)
        du_ref[...] = du.astype(du_ref.dtype)
        dbre_ref[...] += lax.dot_general(ub, grb, tn_dims, preferred_element_type=F32)
        dbim_ref[...] += lax.dot_general(ub, gib, tn_dims, preferred_element_type=F32)
        dcre_ref[...] += lax.dot_general(sre_ref[...].astype(BF16), dyb, tn_dims, preferred_element_type=F32)
        dcim_ref[...] -= lax.dot_general(sim_ref[...].astype(BF16), dyb, tn_dims, preferred_element_type=F32)
        dd_ref[0:1, :] += jnp.sum(dyv * uv, axis=0, keepdims=True)

        @pl.when(t == nt - 1)
        def _():
            ga_ref[0:1, :] = jnp.sum(gacc[0], axis=0, keepdims=True)
            ga_ref[1:2, :] = jnp.sum(gacc[1], axis=0, keepdims=True)

    ch = pl.BlockSpec((tc, LANE), lambda j, t: (nt - 1 - t, j))
    st = pl.BlockSpec((tc, W), lambda j, t: (nt - 1 - t, j))
    bsp = pl.BlockSpec((None, LANE, W), lambda j, t: (j, 0, 0))
    csp = pl.BlockSpec((None, W, LANE), lambda j, t: (j, 0, 0))
    return pl.pallas_call(
        body, name=name,
        out_shape=(jax.ShapeDtypeStruct((L, D), BF16),
                   jax.ShapeDtypeStruct((S5_BLOCKS, LANE, W), F32), jax.ShapeDtypeStruct((S5_BLOCKS, LANE, W), F32),
                   jax.ShapeDtypeStruct((S5_BLOCKS, W, LANE), F32), jax.ShapeDtypeStruct((S5_BLOCKS, W, LANE), F32),
                   jax.ShapeDtypeStruct((SUBLANE, S5_LANES), F32), jax.ShapeDtypeStruct((SUBLANE, D), F32)),
        grid=(S5_BLOCKS, nt),
        in_specs=[ch, ch, st, st, bsp, bsp, csp, csp, pl.BlockSpec((8, SUBLANE, W), lambda j, t: (0, 0, j)),
                  pl.BlockSpec((1, LANE), lambda j, t: (0, j))],
        out_specs=(ch, bsp, bsp, csp, csp, pl.BlockSpec((SUBLANE, W), lambda j, t: (0, j)),
                   pl.BlockSpec((SUBLANE, LANE), lambda j, t: (0, j))),
        scratch_shapes=[pltpu.VMEM((tc, W), F32), pltpu.VMEM((tc, W), F32), pltpu.VMEM((2, SUBLANE, W), F32),
                        pltpu.VMEM((2, SUBLANE, W), F32)],
        compiler_params=_params("parallel", "arbitrary"))(dy, u, sre, sim, bre, bim, cre, cim, pwr, dsk)


def _glu_bwd(dy2, y, t, *, name, tm=256):
    L, D = y.shape

    def body(dy2_ref, y_ref, t_ref, dt_ref, dya_ref, st_ref):
        i = pl.program_id(0)

        @pl.when(i == 0)
        def _():
            st_ref[...] = jnp.zeros_like(st_ref)

        sig = 1.0 / (1.0 + jnp.exp(-t_ref[...]))
        dy2v = dy2_ref[...]
        dt = dy2v * y_ref[...] * sig * (1.0 - sig)
        dt_ref[...] = dt.astype(dt_ref.dtype)
        dya_ref[...] = dy2v * sig
        st_ref[0:1, :] += jnp.sum(dt, axis=0, keepdims=True)

    row = pl.BlockSpec((tm, D), lambda i: (i, 0))
    return pl.pallas_call(
        body, name=name,
        out_shape=(jax.ShapeDtypeStruct((L, D), BF16), jax.ShapeDtypeStruct((L, D), F32),
                   jax.ShapeDtypeStruct((SUBLANE, D), F32)),
        grid=(L // tm,), in_specs=[row, row, row],
        out_specs=(row, row, pl.BlockSpec((SUBLANE, D), lambda i: (0, 0))),
        compiler_params=_params("arbitrary"))(dy2, y, t)


def _s5_prep(a_re, a_im, log_dt, b_re, b_im, c_re, c_im):
    dt = jnp.exp(log_dt)[:, None]
    mag = jnp.exp(a_re * dt)
    abar_re = mag * jnp.cos(a_im * dt)
    abar_im = mag * jnp.sin(a_im * dt)
    den = a_re * a_re + a_im * a_im
    nr = abar_re - 1.0
    ni = abar_im
    f_re = ((nr * a_re + ni * a_im) / den)[..., None]
    f_im = ((ni * a_re - nr * a_im) / den)[..., None]
    bbar_re = f_re * b_re - f_im * b_im
    bbar_im = f_re * b_im + f_im * b_re
    eye = jnp.eye(S5_GROUPS // S5_BLOCKS, dtype=F32)
    gb = S5_GROUPS // S5_BLOCKS

    def blk_b(bb):
        t = bb.reshape(S5_BLOCKS, gb, S5_STATE, S5_GROUP)
        return jnp.einsum('jgph,gk->jghkp', t, eye).reshape(S5_BLOCKS, gb * S5_GROUP, gb * S5_STATE)

    def blk_c(cc):
        t = cc.reshape(S5_BLOCKS, gb, S5_GROUP, S5_STATE)
        return jnp.einsum('jghp,gk->jgpkh', t, eye).reshape(S5_BLOCKS, gb * S5_STATE, gb * S5_GROUP)

    return (abar_re.reshape(1, S5_LANES), abar_im.reshape(1, S5_LANES), blk_b(bbar_re), blk_b(bbar_im),
            blk_c(c_re), blk_c(c_im))


def _s5_power_tables(ar, ai):
    pr, pi = [jnp.ones_like(ar)], [jnp.zeros_like(ai)]
    for _ in range(SUBLANE):
        pr, pi = pr + [pr[-1] * ar - pi[-1] * ai], pi + [pr[-1] * ai + pi[-1] * ar]
    row = jnp.arange(SUBLANE)[:, None]

    def tables(sign, keep, carry_pow):
        out = []
        for d in (1, 2, 4):
            out += [jnp.where(keep(d), pr[d], 0.0), jnp.where(keep(d), sign * pi[d], 0.0)]
        out += [jnp.concatenate([pr[p] for p in carry_pow], 0), sign * jnp.concatenate([pi[p] for p in carry_pow], 0)]
        return jnp.stack([jnp.broadcast_to(o, (SUBLANE, ar.shape[1])) for o in out])

    fwd = tables(1.0, lambda d: row >= d, [r + 1 for r in range(SUBLANE)])
    rev = tables(-1.0, lambda d: row + d <= SUBLANE - 1, [SUBLANE - r for r in range(SUBLANE)])
    return fwd, rev


ADAMW_PART_BLOCK_BYTES = 2 * 1024 * 1024


def _adamw(w, parts, m, v, *, name):
    n, R, C = w.shape
    assert len(parts) == n
    P = parts[0].shape[0]
    tr = R
    while P * tr * C * parts[0].dtype.itemsize > ADAMW_PART_BLOCK_BYTES and tr % 16 == 0:
        tr //= 2
    c1 = 1.0 / (1.0 - ADAM_B1 ** ADAM_STEP)
    c2 = 1.0 / (1.0 - ADAM_B2 ** ADAM_STEP)

    def body(*refs):
        w_ref, m_ref, v_ref = refs[:3]
        p_refs = refs[3:3 + n]
        g_ref, d_ref, nm_ref, nv_ref = refs[3 + n:]
        layer = pl.program_id(0)
        for q, p_ref in enumerate(p_refs):
            @pl.when(layer == q)
            def _(p_ref=p_ref):
                g = p_ref[0].astype(F32)
                for s in range(1, P):
                    g = g + p_ref[s].astype(F32)
                nm = ADAM_B1 * m_ref[...] + (1.0 - ADAM_B1) * g
                nv = ADAM_B2 * v_ref[...] + (1.0 - ADAM_B2) * (g * g)
                g_ref[...] = g
                nm_ref[...] = nm
                nv_ref[...] = nv
                d_ref[...] = -ADAM_LR * ((nm * c1) / (jnp.sqrt(nv * c2) + ADAM_EPS) + ADAM_WD * w_ref[...])

    row = pl.BlockSpec((None, tr, C), lambda l, i: (l, i, 0))
    part_specs = [pl.BlockSpec((P, tr, C), lambda l, i, q=q: (0, jnp.where(l == q, i, 0), 0)) for q in range(n)]
    out = jax.ShapeDtypeStruct((n, R, C), F32)
    return pl.pallas_call(body, name=name, out_shape=(out, out, out, out), grid=(n, R // tr),
                          in_specs=[row, row, row] + part_specs, out_specs=(row, row, row, row),
                          compiler_params=_params("arbitrary", "arbitrary"))(w, m, v, *parts)


def _all_gather(xs, axis, *, name):
    m = xs.shape[axis]
    out_shape = list(xs.shape)
    out_shape[axis] = N_DEV * m

    def body(x_ref, out_ref, send_sems, recv_sems, local_sem):
        x, y, c = _my_pos()
        me, sibling = (x, y, c), (x, y, 1 - c)
        chips = [(1 - x, y), (x, 1 - y), (1 - x, 1 - y)]

        def blk(px, py, pc):
            idx = [slice(None)] * 3
            idx[axis] = pl.ds((4 * px + 2 * py + pc) * m, m)
            return out_ref.at[tuple(idx)]

        def copy(k, block, to, src=None):
            return pltpu.make_async_remote_copy(src_ref=blk(*block) if src is None else src, dst_ref=blk(*block),
                                                send_sem=send_sems.at[k], recv_sem=recv_sems.at[k],
                                                device_id=to, device_id_type=MESH_ID)

        mine = pltpu.make_async_copy(x_ref, blk(*me), local_sem)
        mine.start()
        first = [copy(0, me, sibling, src=x_ref)]
        first += [copy(1 + j, me, (*chip, c), src=x_ref) for j, chip in enumerate(chips)]
        for cp in first:
            cp.start()
        passed = [copy(4 + j, (*chip, c), sibling) for j, chip in enumerate(chips)]
        for j, chip in enumerate(chips):
            copy(1 + j, (*chip, c), me).wait_recv()
            passed[j].start()
        copy(0, sibling, me).wait_recv()
        for j, chip in enumerate(chips):
            copy(4 + j, (*chip, 1 - c), me).wait_recv()
        for cp in first + passed:
            cp.wait_send()
        mine.wait()

    hbm = pl.BlockSpec(memory_space=pl.ANY)
    return pl.pallas_call(body, name=name, out_shape=jax.ShapeDtypeStruct(tuple(out_shape), xs.dtype),
                          in_specs=[hbm], out_specs=hbm,
                          scratch_shapes=[pltpu.SemaphoreType.DMA((N_DEV - 1,)), pltpu.SemaphoreType.DMA((N_DEV - 1,)),
                                          pltpu.SemaphoreType.DMA],
                          compiler_params=pltpu.CompilerParams(has_side_effects=True))(xs)


def _block(ref, axis, idx, m):
    return ref.at[pl.ds(idx * m, m), :] if axis == 0 else ref.at[:, pl.ds(idx * m, m)]


def _exchange_copies(metas, src_refs, zone_refs, send_sems, recv_sems, group):
    x, y, c = _my_pos()
    me = 4 * x + 2 * y + c
    base = group * N_DEV
    pairs = []
    for r in range(1, N_DEV):
        pos = (1 - x if r & 4 else x, 1 - y if r & 2 else y, 1 - c if r & 1 else c)
        peer = 4 * pos[0] + 2 * pos[1] + pos[2]
        for (kind, axis, m), s_ref, z_ref in zip(metas, src_refs, zone_refs):
            if kind == 'gather':
                src, dst, arrival = s_ref, _block(z_ref, axis, me, m), _block(z_ref, axis, peer, m)
            else:
                src, dst, arrival = _block(s_ref, axis, peer, m), z_ref.at[me], z_ref.at[peer]
            pairs.append(tuple(
                pltpu.make_async_remote_copy(src_ref=src, dst_ref=d, send_sem=send_sems.at[base + r - 1],
                                             recv_sem=recv_sems.at[base + r - 1], device_id=pos,
                                             device_id_type=MESH_ID)
                for d in (dst, arrival)))
    own = []
    for (kind, axis, m), s_ref, z_ref in zip(metas, src_refs, zone_refs):
        src, dst = (s_ref, _block(z_ref, axis, me, m)) if kind == 'gather' else (_block(s_ref, axis, me, m), z_ref.at[me])
        own.append(pltpu.make_async_copy(src, dst, recv_sems.at[base + N_DEV - 1]))
    return pairs, own


def _exchange_start(groups, *, name):
    flat = [it for g in groups for it in g]
    n, ng = len(flat), len(groups)
    metas = [it[2] for it in flat]
    bounds = [(sum(len(g) for g in groups[:q]), sum(len(g) for g in groups[:q + 1])) for q in range(ng)]

    def body(*refs):
        src_refs = refs[:n]
        send_sems, recv_sems = refs[n], refs[n + 1]
        zone_refs = refs[2 * n + 2:3 * n + 2]
        token = refs[-1]
        for q, (lo, hi) in enumerate(bounds):
            pairs, own = _exchange_copies(metas[lo:hi], src_refs[lo:hi], zone_refs[lo:hi], send_sems, recv_sems, q)
            for outgoing, _ in pairs:
                outgoing.start()
            for cp in own:
                cp.start()
        token[...] = jnp.zeros_like(token)

    hbm = pl.BlockSpec(memory_space=pltpu.HBM)
    sem = pl.BlockSpec(memory_space=pltpu.SEMAPHORE)
    srcs = [it[0] for it in flat]
    res = pl.pallas_call(
        body, name=name,
        out_shape=(pltpu.SemaphoreType.DMA((ng * N_DEV,)), pltpu.SemaphoreType.DMA((ng * N_DEV,)),
                   *[pltpu.HBM(a.shape, a.dtype) for a in srcs], *[pltpu.HBM(it[1], it[0].dtype) for it in flat],
                   jax.ShapeDtypeStruct((SUBLANE, LANE), F32)),
        in_specs=[hbm] * n,
        out_specs=(sem, sem, *[hbm] * (2 * n), pl.BlockSpec(memory_space=pltpu.VMEM)),
        input_output_aliases={q: 2 + q for q in range(n)},
        compiler_params=pltpu.CompilerParams(has_side_effects=pltpu.SideEffectType.DATAFLOW_SIDE_EFFECTING),
    )(*[pltpu.with_memory_space_constraint(a, pltpu.HBM) for a in srcs])
    handles = [(res[0], res[1], q, list(res[2 + lo:2 + hi]), list(res[2 + n + lo:2 + n + hi]), metas[lo:hi])
               for q, (lo, hi) in enumerate(bounds)]
    return handles, res[-1]


def _exchange_wait(handle, after, *, name):
    send_sems, recv_sems, group, srcs, zones, metas = handle
    n = len(srcs)

    def body(*refs):
        src_refs, zone_refs = refs[:n], refs[n:2 * n]
        s_sems, r_sems = refs[2 * n], refs[2 * n + 1]
        pairs, own = _exchange_copies(metas, src_refs, zone_refs, s_sems, r_sems, group)
        for outgoing, incoming in pairs:
            outgoing.wait_send()
            incoming.wait_recv()
        for cp in own:
            cp.wait()

    hbm = pl.BlockSpec(memory_space=pltpu.HBM)
    sem = pl.BlockSpec(memory_space=pltpu.SEMAPHORE)
    arrays = srcs + zones
    res = pl.pallas_call(
        body, name=name,
        out_shape=tuple(pltpu.HBM(a.shape, a.dtype) for a in arrays),
        in_specs=[hbm] * (2 * n) + [sem, sem, pl.BlockSpec(memory_space=pl.ANY)],
        out_specs=tuple([hbm] * (2 * n)),
        input_output_aliases={q: q for q in range(2 * n)},
        compiler_params=pltpu.CompilerParams(has_side_effects=pltpu.SideEffectType.DATAFLOW_SIDE_EFFECTING),
    )(*arrays, send_sems, recv_sems, after)
    return list(res[n:])


def _pad_rows(a, rows):
    return jnp.pad(a, ((0, rows - a.shape[0]), (0, 0)))


def _stat_row(st, r):
    return st[r:r + 1, :]


def kernel(x, c, ada_w, ada_b, norm1_g, norm2_g, ff_w1, ff_w2, final_g, conv_w_in, conv_w, conv_b, conv_w_out, ssm_w_in, ssm_a_re, ssm_a_im, ssm_log_dt, ssm_b_re, ssm_b_im, ssm_c_re, ssm_c_im, ssm_d, ssm_glu_w, ssm_glu_b, ssm_w_out, sg_w_in, sg_v_g, sg_w_s, sg_b_s, sg_w_out, loss_target, m_ada_w, m_ada_b, m_norm1_g, m_norm2_g, m_ff_w1, m_ff_w2, m_final_g, m_conv_w_in, m_conv_w, m_conv_b, m_conv_w_out, m_ssm_w_in, m_ssm_a_re, m_ssm_a_im, m_ssm_log_dt, m_ssm_b_re, m_ssm_b_im, m_ssm_c_re, m_ssm_c_im, m_ssm_d, m_ssm_glu_w, m_ssm_glu_b, m_ssm_w_out, m_sg_w_in, m_sg_v_g, m_sg_w_s, m_sg_b_s, m_sg_w_out, v_ada_w, v_ada_b, v_norm1_g, v_norm2_g, v_ff_w1, v_ff_w2, v_final_g, v_conv_w_in, v_conv_w, v_conv_b, v_conv_w_out, v_ssm_w_in, v_ssm_a_re, v_ssm_a_im, v_ssm_log_dt, v_ssm_b_re, v_ssm_b_im, v_ssm_c_re, v_ssm_c_im, v_ssm_d, v_ssm_glu_w, v_ssm_glu_b, v_ssm_w_out, v_sg_w_in, v_sg_v_g, v_sg_w_s, v_sg_b_s, v_sg_w_out):
    P = dict(zip(INPUTS, (x, c, ada_w, ada_b, norm1_g, norm2_g, ff_w1, ff_w2, final_g, conv_w_in, conv_w, conv_b, conv_w_out, ssm_w_in, ssm_a_re, ssm_a_im, ssm_log_dt, ssm_b_re, ssm_b_im, ssm_c_re, ssm_c_im, ssm_d, ssm_glu_w, ssm_glu_b, ssm_w_out, sg_w_in, sg_v_g, sg_w_s, sg_b_s, sg_w_out, loss_target, m_ada_w, m_ada_b, m_norm1_g, m_norm2_g, m_ff_w1, m_ff_w2, m_final_g, m_conv_w_in, m_conv_w, m_conv_b, m_conv_w_out, m_ssm_w_in, m_ssm_a_re, m_ssm_a_im, m_ssm_log_dt, m_ssm_b_re, m_ssm_b_im, m_ssm_c_re, m_ssm_c_im, m_ssm_d, m_ssm_glu_w, m_ssm_glu_b, m_ssm_w_out, m_sg_w_in, m_sg_v_g, m_sg_w_s, m_sg_b_s, m_sg_w_out, v_ada_w, v_ada_b, v_norm1_g, v_norm2_g, v_ff_w1, v_ff_w2, v_final_g, v_conv_w_in, v_conv_w, v_conv_b, v_conv_w_out, v_ssm_w_in, v_ssm_a_re, v_ssm_a_im, v_ssm_log_dt, v_ssm_b_re, v_ssm_b_im, v_ssm_c_re, v_ssm_c_im, v_ssm_d, v_ssm_glu_w, v_ssm_glu_b, v_ssm_w_out, v_sg_w_in, v_sg_v_g, v_sg_w_s, v_sg_b_s, v_sg_w_out)))
    L, D = x.shape[1], x.shape[2]
    me = _my_index()
    xs = x[0]
    tgt = loss_target[0]
    n_conv = conv_w_in.shape[0]

    def gather_item(shard, axis):
        full = tuple(N_DEV * s if a == axis else s for a, s in enumerate(shard.shape))
        return shard, full, ('gather', axis, shard.shape[axis])

    def mixer_shards(i):
        kind, j = i % 3, i // 3
        if kind == 0:
            return [(conv_w_in[j], 1), (conv_w_out[j], 0)]
        if kind == 1:
            return [(ssm_w_in[j], 0), (ssm_glu_w[j], 0), (ssm_w_out[j], 0)]
        return [(sg_w_in[j], 1), (sg_w_out[j], 0)]

    gathers, gather_token = _exchange_start(
        [[gather_item(w.astype(BF16), ax) for w, ax in shards]
         for i in range(DEPTH) for shards in (mixer_shards(i), [(ff_w1[i], 1), (ff_w2[i], 0)])],
        name="gather_start")
    started = gather_token[0:1, 0:1]

    c_act = c * (1.0 / (1.0 + jnp.exp(-c))) + started
    vec_rows = jnp.concatenate([c_act.reshape(D // LANE, LANE), conv_w.reshape(-1, LANE), conv_b.reshape(-1, LANE),
                                sg_v_g.reshape(-1, LANE)], 0)
    n_vec = vec_rows.shape[0]
    vec_all = _all_gather(_pad_rows(vec_rows, 24)[None], 0, name="gather_vectors")
    c_all = vec_all[:, :D // LANE, :].reshape(N_DEV, D)
    sharded_full = vec_all[:, D // LANE:n_vec, :].transpose(1, 0, 2).reshape(n_vec - D // LANE, D)
    conv_w_full = sharded_full[:3 * n_conv].reshape(n_conv, 3, D)
    conv_b_full = sharded_full[3 * n_conv:4 * n_conv]
    sg_vg_full = sharded_full[4 * n_conv:4 * n_conv + 1]

    c_pad = _pad_rows(c_all, LANE)
    ncol = ada_w.shape[2]
    mod_part = jnp.stack([_mm(c_pad, ada_w[i], name=f"ada_fwd{i}")[:N_DEV] for i in range(DEPTH)])
    mod_all = _all_gather(mod_part.reshape(1, DEPTH * N_DEV, ncol), 0, name="gather_mod")
    mod_all = mod_all.reshape(N_DEV, DEPTH, N_DEV, ncol)
    mod_me = lax.dynamic_index_in_dim(mod_all, me, 2, keepdims=False)
    mod = mod_me.transpose(1, 0, 2).reshape(DEPTH, N_DEV * ncol) + ada_b


    s5_args = (ssm_a_re[0], ssm_a_im[0], ssm_log_dt[0], ssm_b_re[0], ssm_b_im[0], ssm_c_re[0], ssm_c_im[0])
    (abar_re, abar_im, bblk_re, bblk_im, cblk_re, cblk_im), s5_vjp = jax.vjp(_s5_prep, *s5_args)
    pw_fwd, pw_rev = _s5_power_tables(abar_re, abar_im)
    s5_w = tuple(t.astype(BF16) for t in (bblk_re, bblk_im, cblk_re, cblk_im))
    causal = jnp.tril(jnp.ones((SG_CHUNK, SG_CHUNK), dtype=bool))
    ws_m = jnp.where(causal[None], sg_w_s[0], 0.0)
    ws_b = ws_m.astype(BF16)
    wst_b = ws_m.transpose(0, 2, 1).astype(BF16)
    bsb = jnp.broadcast_to(sg_b_s[0][:, :, None], (SG_HEADS, SG_CHUNK, LANE))

    saved = []
    xa = xs
    for i in range(DEPTH):
        kind, j = i % 3, i // 3
        sh1, sc1, g1, sh2, sc2, g2 = (mod[i:i + 1, q * D:(q + 1) * D] for q in range(6))
        wn1 = norm1_g[i:i + 1] * (1.0 + sc1)
        wn2 = norm2_g[i:i + 1] * (1.0 + sc2)
        S = dict(x_in=xa, g1=g1, g2=g2, sc1=sc1, sc2=sc2, wn1=wn1, wn2=wn2)
        h1 = _normmod_fwd(xa, wn1, sh1, name=f"norm1_fwd{i}")
        w_mix = _exchange_wait(gathers[2 * i], h1, name=f"gather_mix_wait{i}")
        S['h1'] = h1
        if kind == 0:
            bcx = _mm(h1, w_mix[0], name=f"conv_in{i}", bm=2048)
            wb = _pad_rows(jnp.concatenate([conv_w_full[j], conv_b_full[j:j + 1]], 0), SUBLANE)
            pb = _conv_fwd(bcx, wb, name=f"conv_mix{i}")
            S.update(bcx=bcx, wb=wb, pb=pb)
        elif kind == 1:
            u = _mm(h1, w_mix[0], name=f"ssm_in{i}")
            sre, sim, ypre, yg = _s5_fwd(u, *s5_w, pw_fwd, ssm_d, name=f"s5_scan{i}")

            def glu_epi(acc, yv, bias):
                t = acc + bias
                return yv * (1.0 / (1.0 + jnp.exp(-t))), t

            pb, tt = _mm(yg, w_mix[1], name=f"ssm_glu{i}", out_dtypes=(BF16, F32), epi=glu_epi,
                         extras=[(yg, 'mn'), (ssm_glu_b, 'n')])
            S.update(u=u, sre=sre, sim=sim, ypre=ypre, yg=yg, pb=pb, tt=tt)
        else:
            uv = _mm(h1, w_mix[0], name=f"sg_in{i}", bm=2048)
            pb = _sg_fwd(uv, sg_vg_full, ws_b, bsb, name=f"sg_mix{i}")
            S.update(uv=uv, pb=pb)
        x_mid, y_mix = _mm(pb, w_mix[-1], name=f"mix_out{i}", out_dtypes=(F32, BF16), epi=_epi_residual,
                           extras=[(xa, 'mn'), (g1, 'n')])
        h2 = _normmod_fwd(x_mid, wn2, sh2, name=f"norm2_fwd{i}")
        w1_full, w2_full = _exchange_wait(gathers[2 * i + 1], h2, name=f"gather_ff_wait{i}")
        S.update(w_mix=w_mix, w1=w1_full, w2=w2_full)
        ra = _mm(h2, w1_full, name=f"ff_up{i}", out_dtypes=(BF16,), epi=lambda acc: (jnp.maximum(acc, 0.0),), bm=2048)
        xa, f_out = _mm(ra, w2_full, name=f"ff_down{i}", out_dtypes=(F32, BF16), epi=_epi_residual, a_fn=_square,
                        extras=[(x_mid, 'mn'), (g2, 'n')], bm=256, bk=w2_full.shape[0])
        S.update(x_mid=x_mid, y_mix=y_mix, h2=h2, ra=ra, f_out=f_out)
        saved.append(S)

    S = saved[-1]
    dx, st, dfb, loss_tile = _loss_head(xa, tgt, final_g[None], S['f_out'], S['g2'], name="loss_head")
    loss = lax.psum(loss_tile[0, 0], ("x", "y", "c"))
    d_final_g = _stat_row(st, 0)
    dg2_next = _stat_row(st, 2)

    def scatter_item(g, axis):
        m = g.shape[axis] // N_DEV
        shard = tuple(m if a == axis else s for a, s in enumerate(g.shape))
        return g, (N_DEV,) + shard, ('scatter', axis, m)

    dmod = [None] * DEPTH
    dn1g, dn2g = [None] * DEPTH, [None] * DEPTH
    d_conv_w, d_conv_b = [None] * n_conv, [None] * n_conv
    ff_sent, mix_sent = [None] * DEPTH, [None] * DEPTH
    small = {}
    for i in reversed(range(DEPTH)):
        kind, j = i % 3, i // 3
        S = saved[i]
        w_mix = S['w_mix']
        dg2 = dg2_next
        da = _mm(dfb, S['w2'], tb=True, name=f"ff_down_bwd{i}", out_dtypes=(BF16,), bm=2048,
                 epi=lambda acc, rav: (acc * (2.0 * rav.astype(F32)),), extras=[(S['ra'], 'mn')])
        dw2 = _wgrad(S['ra'], dfb, name=f"ff_w2_grad{i}", a_fn=_square, bm=256, bn=1024)
        dh2 = _mm(da, S['w1'], tb=True, name=f"ff_up_bwd{i}", out_dtypes=(BF16,), bm=512, bk=da.shape[1])
        dw1 = _wgrad(S['h2'], da, name=f"ff_w1_grad{i}")
        (ff_sent[i],), token = _exchange_start([[scatter_item(dw1, 1), scatter_item(dw2, 0)]],
                                               name=f"ff_grads_start{i}")
        dx_mid, st2, dyb = _normmod_bwd(dh2, S['x_mid'], S['wn2'] + token[0:1, 0:1], dx,
                                        (S['y_mix'], S['g1']), name=f"norm2_bwd{i}")
        dsc2 = _stat_row(st2, 0) * norm2_g[i:i + 1]
        dn2g[i] = _stat_row(st2, 0) * (1.0 + S['sc2'])
        dsh2 = _stat_row(st2, 1)
        dg1 = _stat_row(st2, 2)
        if kind == 0:
            dp = _mm(dyb, w_mix[1], tb=True, name=f"conv_out_bwd{i}")
            d_cwo = _wgrad(S['pb'], dyb, name=f"conv_w_out_grad{i}")
            db, dc, dxh, stc = _conv_bwd(dp, S['bcx'], S['wb'], name=f"conv_mix_bwd{i}")
            dbcx = jnp.concatenate([db, dc, dxh], 1)
            d_conv_w[j] = stc[0:3]
            d_conv_b[j] = stc[3:4]
            dh1 = _mm(dbcx, w_mix[0], tb=True, name=f"conv_in_bwd{i}", out_dtypes=(BF16,), bm=512, bk=dbcx.shape[1])
            d_cwi = _wgrad(S['h1'], dbcx, name=f"conv_w_in_grad{i}")
            mix_grads = [scatter_item(d_cwi, 1), scatter_item(d_cwo, 0)]
        elif kind == 1:
            dy2 = _mm(dyb, w_mix[2], tb=True, name=f"ssm_out_bwd{i}")
            d_ssm_out = _wgrad(S['pb'], dyb, name=f"ssm_w_out_grad{i}")
            dtb, dya, stg = _glu_bwd(dy2, S['yg'], S['tt'], name=f"ssm_glu_bwd{i}")
            dypre = _mm(dtb, w_mix[1], tb=True, name=f"ssm_glu_in_bwd{i}",
                        epi=lambda acc, a, yp: ((a + acc) * _gelu_grad(yp),),
                        extras=[(dya, 'mn'), (S['ypre'], 'mn')])
            d_glu = _wgrad(S['yg'], dtb, name=f"ssm_glu_w_grad{i}", bm=512)
            dub, dbre, dbim, dcre, dcim, ga, dd = _s5_bwd(dypre, S['u'], S['sre'], S['sim'], *s5_w, pw_rev, ssm_d,
                                                           name=f"s5_scan_bwd{i}")
            dh1 = _mm(dub, w_mix[0], tb=True, name=f"ssm_in_bwd{i}", out_dtypes=(BF16,))
            d_ssm_in = _wgrad(S['h1'], dub, name=f"ssm_w_in_grad{i}")
            da_re, da_im, dlog_dt, db_re, db_im, dc_re, dc_im = s5_vjp((ga[0:1], ga[1:2], dbre, dbim, dcre, dcim))
            small.update(ssm_a_re=da_re, ssm_a_im=da_im, ssm_log_dt=dlog_dt, ssm_b_re=db_re, ssm_b_im=db_im,
                         ssm_c_re=dc_re, ssm_c_im=dc_im, ssm_d=dd[0], ssm_glu_b=stg[0])
            mix_grads = [scatter_item(d_ssm_in, 0), scatter_item(d_glu, 0), scatter_item(d_ssm_out, 0)]
        else:
            dp = _mm(dyb, w_mix[1], tb=True, name=f"sg_out_bwd{i}")
            d_sgo = _wgrad(S['pb'], dyb, name=f"sg_w_out_grad{i}")
            duv, dws, dbs, stv = _sg_bwd(dp, S['uv'], sg_vg_full, ws_b, wst_b, bsb, name=f"sg_mix_bwd{i}")
            dh1 = _mm(duv, w_mix[0], tb=True, name=f"sg_in_bwd{i}", out_dtypes=(BF16,), bm=512, bk=duv.shape[1])
            d_sgi = _wgrad(S['h1'], duv, name=f"sg_w_in_grad{i}")
            small.update(sg_w_s=jnp.where(causal[None], dws, 0.0), sg_b_s=jnp.sum(dbs, axis=-1))
            d_sg_vg = stv[0:1]
            mix_grads = [scatter_item(d_sgi, 1), scatter_item(d_sgo, 0)]
        (mix_sent[i],), token = _exchange_start([mix_grads], name=f"mix_grads_start{i}")
        wn1 = S['wn1'] + token[0:1, 0:1]
        if i > 0:
            prev = saved[i - 1]
            dx, st1, dfb = _normmod_bwd(dh1, S['x_in'], wn1, dx_mid, (prev['f_out'], prev['g2']),
                                        name=f"norm1_bwd{i}")
            dg2_next = _stat_row(st1, 2)
        else:
            dx, st1 = _normmod_bwd(dh1, S['x_in'], wn1, dx_mid, None, name=f"norm1_bwd{i}")
        dsc1 = _stat_row(st1, 0) * norm1_g[i:i + 1]
        dn1g[i] = _stat_row(st1, 0) * (1.0 + S['sc1'])
        dsh1 = _stat_row(st1, 1)
        dmod[i] = jnp.concatenate([dsh1, dsc1, dg1, dsh2, dsc2, dg2], 1)
    grad_x = dx[None]

    small.update(ada_b=jnp.concatenate(dmod, 0), norm1_g=jnp.concatenate(dn1g, 0), norm2_g=jnp.concatenate(dn2g, 0),
                 final_g=d_final_g, conv_w=jnp.stack(d_conv_w), conv_b=jnp.concatenate(d_conv_b, 0), sg_v_g=d_sg_vg)
    order = SMALL_REPL + SMALL_SHARD
    sizes = [math.prod(P[n].shape) * (N_DEV if n in SMALL_SHARD else 1) for n in order]
    n_repl = sum(sizes[:len(SMALL_REPL)])
    total = sum(sizes)
    pack_rows = -(-total // (LANE * 512)) * 512
    flat = jnp.concatenate([small[n].reshape(-1).astype(F32) for n in order])
    pack = jnp.pad(flat, (0, pack_rows * LANE - total)).reshape(1, pack_rows, LANE)
    pack_all = _all_gather(pack, 0, name="gather_small_grads")

    def pack_repl(prefix):
        v = jnp.concatenate([P[prefix + n].reshape(-1) for n in SMALL_REPL])
        return jnp.pad(v, (0, pack_rows * LANE - n_repl)).reshape(1, pack_rows, LANE)

    rg, rd, rm, rv = _adamw(pack_repl(''), [pack_all], pack_repl('m_'), pack_repl('v_'), name="adamw_small")
    out = {}
    off = 0
    for n, sz in zip(SMALL_REPL, sizes):
        out[n] = tuple(t.reshape(-1)[off:off + sz].reshape(P[n].shape) for t in (rg, rd, rm, rv))
        off += sz
    sh_rows = (total - n_repl) // D
    sh_parts = pack_all.reshape(N_DEV, pack_rows * LANE)[:, n_repl:total].reshape(N_DEV, sh_rows, D)
    sh_parts = lax.dynamic_slice_in_dim(sh_parts, me * LANE, LANE, 2)
    sh_parts = jnp.pad(sh_parts, ((0, 0), (0, 16 - sh_rows), (0, 0)))

    def pack_shard(prefix):
        return _pad_rows(jnp.concatenate([P[prefix + n].reshape(-1, LANE) for n in SMALL_SHARD], 0), 16)[None]

    sg_, sd_, sm_, sv_ = _adamw(pack_shard(''), [sh_parts], pack_shard('m_'), pack_shard('v_'), name="adamw_channel")
    off = 0
    for n in SMALL_SHARD:
        rows = math.prod(P[n].shape) // LANE
        out[n] = tuple(t[0, off:off + rows].reshape(P[n].shape) for t in (sg_, sd_, sm_, sv_))
        off += rows

    dmod_all = pack_all.reshape(N_DEV, pack_rows * LANE)[:, :DEPTH * 6 * D].reshape(N_DEV, DEPTH, 6 * D)
    dmod_cols = lax.dynamic_slice_in_dim(dmod_all, me * ncol, ncol, 2)
    g_ada = [_mm(c_pad, _pad_rows(dmod_cols[:, i], LANE), ta=True, name=f"ada_w_grad{i}")[None] for i in range(DEPTH)]

    def big(name, parts):
        res = _adamw(P[name], parts, P['m_' + name], P['v_' + name], name="adamw_" + name)
        out[name] = res
        return res[1]

    ff_parts = [_exchange_wait(ff_sent[i], dx, name=f"ff_grads_wait{i}") for i in range(DEPTH)]
    mix_parts = [None] + [_exchange_wait(mix_sent[i], dx, name=f"mix_grads_wait{i}") for i in range(1, DEPTH)]
    big('ada_w', g_ada)
    big('ff_w1', [p[0] for p in ff_parts])
    big('ff_w2', [p[1] for p in ff_parts])
    done = big('sg_w_in', [mix_parts[2][0]])
    mix_parts[0] = _exchange_wait(mix_sent[0], done, name="mix_grads_wait0")
    big('conv_w_in', [mix_parts[i][0] for i in range(DEPTH) if i % 3 == 0])
    row_names = ['conv_w_out', 'ssm_w_in', 'ssm_glu_w', 'ssm_w_out', 'sg_w_out']
    row_parts = ([mix_parts[i][1] for i in range(DEPTH) if i % 3 == 0] + mix_parts[1] + [mix_parts[2][1]])
    row_w, row_m, row_v = (jnp.concatenate([P[pre + n] for n in row_names], 0) for pre in ('', 'm_', 'v_'))
    rw = _adamw(row_w, row_parts, row_m, row_v, name="adamw_row_sharded")
    off = 0
    for n in row_names:
        cnt = P[n].shape[0]
        out[n] = tuple(t[off:off + cnt] for t in rw)
        off += cnt

    return (loss, grad_x, *[out[n][0] for n in WEIGHTS], *[out[n][1] for n in WEIGHTS],
            *[out[n][2] for n in WEIGHTS], *[out[n][3] for n in WEIGHTS])
```

```python
import math

import jax
import jax.numpy as jnp
from jax import lax
from jax.experimental import pallas as pl
from jax.experimental.pallas import tpu as pltpu

F32 = jnp.float32
BF16 = jnp.bfloat16

N_DEV = 8
MESH_ID = pl.DeviceIdType.MESH
DEPTH = 4
EPS = 1e-6
S5_GROUPS, S5_GROUP, S5_STATE = 64, 16, 64
S5_LANES = S5_GROUPS * S5_STATE
S5_BLOCKS = 8
SG_HEADS, SG_CHUNK = 8, 128
LANE = 128
SUBLANE = 8
VMEM_LIMIT = 48 * 1024 * 1024
ADAM_LR, ADAM_B1, ADAM_B2, ADAM_EPS, ADAM_WD, ADAM_STEP = 0.001, 0.9, 0.999, 1e-08, 0.01, 10
GELU_C = math.sqrt(2.0 / math.pi)
GELU_A = 0.044715

WEIGHTS = ['ada_w', 'ada_b', 'norm1_g', 'norm2_g', 'ff_w1', 'ff_w2', 'final_g', 'conv_w_in', 'conv_w', 'conv_b',
           'conv_w_out', 'ssm_w_in', 'ssm_a_re', 'ssm_a_im', 'ssm_log_dt', 'ssm_b_re', 'ssm_b_im', 'ssm_c_re',
           'ssm_c_im', 'ssm_d', 'ssm_glu_w', 'ssm_glu_b', 'ssm_w_out', 'sg_w_in', 'sg_v_g', 'sg_w_s', 'sg_b_s',
           'sg_w_out']
INPUTS = ['x', 'c'] + WEIGHTS + ['loss_target'] + ['m_' + n for n in WEIGHTS] + ['v_' + n for n in WEIGHTS]
S5_SMALL = ['ssm_a_re', 'ssm_a_im', 'ssm_log_dt', 'ssm_b_re', 'ssm_b_im', 'ssm_c_re', 'ssm_c_im', 'ssm_d', 'ssm_glu_b']
SG_SMALL = ['sg_w_s', 'sg_b_s']
LAST_SMALL = ['ada_b', 'norm1_g', 'norm2_g', 'final_g']
SMALL_SHARD = ['conv_w', 'conv_b', 'sg_v_g']


def _params(*sem):
    return pltpu.CompilerParams(dimension_semantics=sem or None, vmem_limit_bytes=VMEM_LIMIT)


def _my_pos():
    return lax.axis_index("x"), lax.axis_index("y"), lax.axis_index("c")


def _my_index():
    x, y, c = _my_pos()
    return 4 * x + 2 * y + c


def _mm(a, b, *, name, ta=False, tb=False, out_dtypes=(F32,), epi=None, extras=(), a_fn=None, bm=1024, bn=1024,
        bk=1024):
    a_chunks = a.shape[0] if a.ndim == 3 else 0
    b_chunks = b.shape[0] if b.ndim == 3 else 0
    assert not (a_chunks and ta) and not (b_chunks and tb)
    if a_chunks:
        m, k = a.shape[1], a_chunks * a.shape[2]
        bk = k
    else:
        m, k = (a.shape[1], a.shape[0]) if ta else a.shape
    if b_chunks:
        k2, n = b.shape[1], b_chunks * b.shape[2]
        bn = min(bn, b.shape[2])
    else:
        k2, n = (b.shape[1], b.shape[0]) if tb else b.shape
    assert k == k2, (a.shape, b.shape, ta, tb)
    bm, bn, bk = min(bm, m), min(bn, n), min(bk, k)
    assert m % bm == 0 and n % bn == 0 and k % bk == 0, (m, n, k, bm, bn, bk)
    nk = k // bk
    n_ex, n_out = len(extras), len(out_dtypes)
    dims = (((0 if ta else 1,), (1 if tb else 0,)), ((), ()))

    def body(*refs):
        a_ref, b_ref = refs[0], refs[1]
        ex_refs = refs[2:2 + n_ex]
        out_refs = refs[2 + n_ex:2 + n_ex + n_out]

        def finish(acc):
            outs = epi(acc, *[r[...] for r in ex_refs]) if epi is not None else (acc,)
            for r, o in zip(out_refs, outs):
                r[...] = o.astype(r.dtype)

        av = jnp.concatenate([a_ref[t] for t in range(a_chunks)], axis=1) if a_chunks else a_ref[...]
        if a_fn is not None:
            av = a_fn(av)
        part = lax.dot_general(av.astype(BF16), b_ref[...].astype(BF16), dims, preferred_element_type=F32)
        if nk == 1:
            finish(part)
            return
        acc_ref = refs[-1]
        kk = pl.program_id(2)

        @pl.when(kk == 0)
        def _():
            acc_ref[...] = part

        @pl.when(kk > 0)
        def _():
            acc_ref[...] += part

        @pl.when(kk == nk - 1)
        def _():
            finish(acc_ref[...])

    if a_chunks:
        a_spec = pl.BlockSpec((a_chunks, bm, a.shape[2]), lambda i, j, q: (0, i, 0))
    elif ta:
        a_spec = pl.BlockSpec((bk, bm), lambda i, j, q: (q, i))
    else:
        a_spec = pl.BlockSpec((bm, bk), lambda i, j, q: (i, q))
    if b_chunks:
        per = b.shape[2] // bn
        b_spec = pl.BlockSpec((None, bk, bn), lambda i, j, q: (j // per, q, j % per))
    elif tb:
        b_spec = pl.BlockSpec((bn, bk), lambda i, j, q: (j, q))
    else:
        b_spec = pl.BlockSpec((bk, bn), lambda i, j, q: (q, j))
    ex_specs = []
    for arr, kind in extras:
        if kind == 'mn':
            assert arr.shape == (m, n), (arr.shape, m, n)
            ex_specs.append(pl.BlockSpec((bm, bn), lambda i, j, q: (i, j)))
        else:
            assert arr.shape == (1, n), (arr.shape, n)
            ex_specs.append(pl.BlockSpec((1, bn), lambda i, j, q: (0, j)))
    outs = pl.pallas_call(
        body, name=name,
        out_shape=tuple(jax.ShapeDtypeStruct((m, n), d) for d in out_dtypes),
        grid=(m // bm, n // bn, nk),
        in_specs=[a_spec, b_spec] + ex_specs,
        out_specs=tuple(pl.BlockSpec((bm, bn), lambda i, j, q: (i, j)) for _ in out_dtypes),
        scratch_shapes=[pltpu.VMEM((bm, bn), F32)] if nk > 1 else [],
        compiler_params=_params("parallel", "parallel", "arbitrary"),
    )(a, b, *[arr for arr, _ in extras])
    return outs if n_out > 1 else outs[0]


def _epi_residual(acc, res, gate):
    return res + gate * acc, acc


def _wgrad(acts, cots, *, name, a_fn=None, bm=1024, bn=512):
    return _mm(acts, cots, ta=True, name=name, out_dtypes=(BF16,), a_fn=a_fn, bm=bm, bn=bn, bk=acts.shape[0])


def _square(a):
    af = a.astype(F32)
    return af * af


def _rstd(xv):
    return lax.rsqrt(jnp.mean(xv * xv, axis=-1, keepdims=True) + EPS)


def _normmod_fwd(x, w, sh, *, name, tm=512):
    L, D = x.shape

    def body(x_ref, w_ref, s_ref, h_ref):
        xv = x_ref[...]
        h_ref[...] = (xv * _rstd(xv) * w_ref[...] + s_ref[...]).astype(h_ref.dtype)

    row = pl.BlockSpec((tm, D), lambda i: (i, 0))
    vec = pl.BlockSpec((1, D), lambda i: (0, 0))
    return pl.pallas_call(body, name=name, out_shape=jax.ShapeDtypeStruct((L, D), BF16), grid=(L // tm,),
                          in_specs=[row, vec, vec], out_specs=row, compiler_params=_params("parallel"))(x, w, sh)


def _normmod_bwd(dh, x, w, dres, gate, *, name, tm=256):
    L, D = x.shape
    has_gate = gate is not None

    def body(*refs):
        if has_gate:
            dh_ref, x_ref, w_ref, r_ref, y_ref, g_ref, dx_ref, st_ref, dy_ref = refs
        else:
            dh_ref, x_ref, w_ref, r_ref, dx_ref, st_ref = refs
        i = pl.program_id(0)

        @pl.when(i == 0)
        def _():
            st_ref[...] = jnp.zeros_like(st_ref)

        xv = x_ref[...]
        dhv = dh_ref[...].astype(F32)
        rstd = _rstd(xv)
        xn = xv * rstd
        dxn = dhv * w_ref[...]
        dx = rstd * (dxn - xn * jnp.mean(dxn * xn, axis=-1, keepdims=True)) + r_ref[...]
        dx_ref[...] = dx
        st_ref[0:1, :] += jnp.sum(dhv * xn, axis=0, keepdims=True)
        st_ref[1:2, :] += jnp.sum(dhv, axis=0, keepdims=True)
        if has_gate:
            dy_ref[...] = (dx * g_ref[...]).astype(dy_ref.dtype)
            st_ref[2:3, :] += jnp.sum(dx * y_ref[...].astype(F32), axis=0, keepdims=True)

    row = pl.BlockSpec((tm, D), lambda i: (i, 0))
    vec = pl.BlockSpec((1, D), lambda i: (0, 0))
    st = pl.BlockSpec((SUBLANE, D), lambda i: (0, 0))
    in_specs = [row, row, vec, row] + ([row, vec] if has_gate else [])
    out_shape = [jax.ShapeDtypeStruct((L, D), F32), jax.ShapeDtypeStruct((SUBLANE, D), F32)]
    out_specs = [row, st]
    if has_gate:
        out_shape.append(jax.ShapeDtypeStruct((L, D), BF16))
        out_specs.append(row)
    args = (dh, x, w, dres) + (tuple(gate) if has_gate else ())
    return pl.pallas_call(body, name=name, out_shape=tuple(out_shape), grid=(L // tm,), in_specs=in_specs,
                          out_specs=tuple(out_specs), compiler_params=_params("arbitrary"))(*args)


def _loss_head(x, tgt, fg, y, g, *, name, tm=256):
    L, D = x.shape

    def body(x_ref, t_ref, fg_ref, y_ref, g_ref, dx_ref, st_ref, dy_ref, loss_ref):
        i = pl.program_id(0)

        @pl.when(i == 0)
        def _():
            st_ref[...] = jnp.zeros_like(st_ref)
            loss_ref[...] = jnp.zeros_like(loss_ref)

        xv = x_ref[...]
        rstd = _rstd(xv)
        xn = xv * rstd
        err = xn * fg_ref[...] - t_ref[...]
        loss_ref[...] += 0.5 * jnp.sum(jnp.mean(err * err, axis=-1, keepdims=True))
        dout = err * (1.0 / D)
        dxn = dout * fg_ref[...]
        dx = rstd * (dxn - xn * jnp.mean(dxn * xn, axis=-1, keepdims=True))
        dx_ref[...] = dx
        dy_ref[...] = (dx * g_ref[...]).astype(dy_ref.dtype)
        st_ref[0:1, :] += jnp.sum(dout * xn, axis=0, keepdims=True)
        st_ref[2:3, :] += jnp.sum(dx * y_ref[...].astype(F32), axis=0, keepdims=True)

    row = pl.BlockSpec((tm, D), lambda i: (i, 0))
    vec = pl.BlockSpec((1, D), lambda i: (0, 0))
    return pl.pallas_call(
        body, name=name,
        out_shape=(jax.ShapeDtypeStruct((L, D), F32), jax.ShapeDtypeStruct((SUBLANE, D), F32),
                   jax.ShapeDtypeStruct((L, D), BF16), jax.ShapeDtypeStruct((SUBLANE, LANE), F32)),
        grid=(L // tm,), in_specs=[row, row, vec, row, vec],
        out_specs=(row, pl.BlockSpec((SUBLANE, D), lambda i: (0, 0)), row,
                   pl.BlockSpec((SUBLANE, LANE), lambda i: (0, 0))),
        compiler_params=_params("arbitrary"))(x, tgt, fg, y, g)


def _shift_down(v, k):
    row = lax.broadcasted_iota(jnp.int32, v.shape, 0)
    return jnp.where(row >= k, pltpu.roll(v, k, 0), 0.0)


def _shift_up(v, k):
    n = v.shape[0]
    row = lax.broadcasted_iota(jnp.int32, v.shape, 0)
    return jnp.where(row < n - k, pltpu.roll(v, n - k, 0), 0.0)


def _conv_views(L, D):
    return [pl.BlockSpec((L, LANE), lambda j, s=s: (0, s * (D // LANE) + j)) for s in range(3)]


def _conv_fwd(bcx, wb, *, name):
    L, D = bcx.shape[0], bcx.shape[1] // 3

    def body(b_ref, c_ref, x_ref, wb_ref, p_ref):
        z = c_ref[...].astype(F32) * x_ref[...].astype(F32)
        conv = (wb_ref[0:1, :] * _shift_down(z, 2) + wb_ref[1:2, :] * _shift_down(z, 1)
                + wb_ref[2:3, :] * z + wb_ref[3:4, :])
        p_ref[...] = (b_ref[...].astype(F32) * conv).astype(p_ref.dtype)

    col = pl.BlockSpec((L, LANE), lambda j: (0, j))
    return pl.pallas_call(body, name=name, out_shape=jax.ShapeDtypeStruct((L, D), BF16), grid=(D // LANE,),
                          in_specs=_conv_views(L, D) + [pl.BlockSpec((SUBLANE, LANE), lambda j: (0, j))],
                          out_specs=col, compiler_params=_params("parallel"))(bcx, bcx, bcx, wb)


def _conv_bwd(dp, bcx, wb, *, name):
    L, D = dp.shape

    def body(dp_ref, b_ref, c_ref, x_ref, wb_ref, d3_ref, st_ref):
        cv, xv = c_ref[...].astype(F32), x_ref[...].astype(F32)
        z = cv * xv
        z1, z2 = _shift_down(z, 1), _shift_down(z, 2)
        w0, w1, w2 = wb_ref[0:1, :], wb_ref[1:2, :], wb_ref[2:3, :]
        conv = w0 * z2 + w1 * z1 + w2 * z + wb_ref[3:4, :]
        dpv = dp_ref[...].astype(F32)
        d3_ref[0] = (dpv * conv).astype(d3_ref.dtype)
        dconv = dpv * b_ref[...].astype(F32)
        dz = w2 * dconv + w1 * _shift_up(dconv, 1) + w0 * _shift_up(dconv, 2)
        d3_ref[1] = (dz * xv).astype(d3_ref.dtype)
        d3_ref[2] = (dz * cv).astype(d3_ref.dtype)
        st_ref[...] = jnp.zeros_like(st_ref)
        st_ref[0:1, :] = jnp.sum(dconv * z2, axis=0, keepdims=True)
        st_ref[1:2, :] = jnp.sum(dconv * z1, axis=0, keepdims=True)
        st_ref[2:3, :] = jnp.sum(dconv * z, axis=0, keepdims=True)
        st_ref[3:4, :] = jnp.sum(dconv, axis=0, keepdims=True)

    col = pl.BlockSpec((L, LANE), lambda j: (0, j))
    vec = pl.BlockSpec((SUBLANE, LANE), lambda j: (0, j))
    return pl.pallas_call(body, name=name,
                          out_shape=(jax.ShapeDtypeStruct((3, L, D), BF16), jax.ShapeDtypeStruct((SUBLANE, D), F32)),
                          grid=(D // LANE,), in_specs=[col] + _conv_views(L, D) + [vec],
                          out_specs=(pl.BlockSpec((3, L, LANE), lambda j: (0, 0, j)), vec),
                          compiler_params=_params("parallel"))(dp, bcx, bcx, bcx, wb)


def _sg_fwd(uv, vg, ws, bsb, *, name, tr=512):
    L, D = uv.shape[0], uv.shape[1] // 2

    def body(uv_ref, vg_ref, ws_ref, bsb_ref, p_ref):
        for ci in range(tr // SG_CHUNK):
            rows = slice(ci * SG_CHUNK, (ci + 1) * SG_CHUNK)
            v = uv_ref[rows, D:2 * D]
            vn = (v * _rstd(v) * vg_ref[...]).astype(BF16)
            for h in range(SG_HEADS):
                cols = slice(h * LANE, (h + 1) * LANE)
                vm = jnp.dot(ws_ref[h], vn[:, cols], preferred_element_type=F32) + bsb_ref[h]
                p_ref[rows, cols] = (uv_ref[rows, cols] * vm).astype(p_ref.dtype)

    full3 = pl.BlockSpec((SG_HEADS, SG_CHUNK, LANE), lambda i: (0, 0, 0))
    return pl.pallas_call(body, name=name, out_shape=jax.ShapeDtypeStruct((L, D), BF16), grid=(L // tr,),
                          in_specs=[pl.BlockSpec((tr, 2 * D), lambda i: (i, 0)), pl.BlockSpec((1, D), lambda i: (0, 0)),
                                    full3, full3],
                          out_specs=pl.BlockSpec((tr, D), lambda i: (i, 0)),
                          compiler_params=_params("parallel"))(uv, vg, ws, bsb)


def _sg_bwd(dp, uv, vg, ws, wst, bsb, *, name, tr=512):
    L, D = dp.shape

    def body(dp_ref, uv_ref, vg_ref, ws_ref, wst_ref, bsb_ref, duv_ref, dws_ref, dbs_ref, st_ref, dvn_ref):
        i = pl.program_id(0)

        @pl.when(i == 0)
        def _():
            dws_ref[...] = jnp.zeros_like(dws_ref)
            dbs_ref[...] = jnp.zeros_like(dbs_ref)
            st_ref[...] = jnp.zeros_like(st_ref)

        for ci in range(tr // SG_CHUNK):
            rows = slice(ci * SG_CHUNK, (ci + 1) * SG_CHUNK)
            v = uv_ref[rows, D:2 * D]
            rstd = _rstd(v)
            vhat = v * rstd
            vn = (vhat * vg_ref[...]).astype(BF16)
            for h in range(SG_HEADS):
                cols = slice(h * LANE, (h + 1) * LANE)
                vm = jnp.dot(ws_ref[h], vn[:, cols], preferred_element_type=F32) + bsb_ref[h]
                dph = dp_ref[rows, cols]
                duv_ref[rows, cols] = (dph * vm).astype(duv_ref.dtype)
                dvm = dph * uv_ref[rows, cols]
                dbs_ref[h] += dvm
                dvmb = dvm.astype(BF16)
                dws_ref[h] += lax.dot_general(dvmb, vn[:, cols], (((1,), (1,)), ((), ())),
                                              preferred_element_type=F32)
                dvn_ref[rows, cols] = jnp.dot(wst_ref[h], dvmb, preferred_element_type=F32)
            dvn = dvn_ref[rows, :]
            gv = dvn * vg_ref[...]
            dv = rstd * (gv - vhat * jnp.mean(gv * vhat, axis=-1, keepdims=True))
            duv_ref[rows, D:2 * D] = dv.astype(duv_ref.dtype)
            st_ref[0:1, :] += jnp.sum(dvn * vhat, axis=0, keepdims=True)

    full3 = pl.BlockSpec((SG_HEADS, SG_CHUNK, LANE), lambda i: (0, 0, 0))
    acc3 = jax.ShapeDtypeStruct((SG_HEADS, SG_CHUNK, LANE), F32)
    return pl.pallas_call(
        body, name=name,
        out_shape=(jax.ShapeDtypeStruct((L, 2 * D), BF16), acc3, acc3, jax.ShapeDtypeStruct((SUBLANE, D), F32)),
        grid=(L // tr,),
        in_specs=[pl.BlockSpec((tr, D), lambda i: (i, 0)), pl.BlockSpec((tr, 2 * D), lambda i: (i, 0)),
                  pl.BlockSpec((1, D), lambda i: (0, 0)), full3, full3, full3],
        out_specs=(pl.BlockSpec((tr, 2 * D), lambda i: (i, 0)), full3, full3,
                   pl.BlockSpec((SUBLANE, D), lambda i: (0, 0))),
        scratch_shapes=[pltpu.VMEM((tr, D), F32)],
        compiler_params=_params("arbitrary"))(dp, uv, vg, ws, wst, bsb)


def _gelu(x):
    return 0.5 * x * (1.0 + jnp.tanh(GELU_C * (x + GELU_A * x * x * x)))


def _gelu_grad(x):
    th = jnp.tanh(GELU_C * (x + GELU_A * x * x * x))
    return 0.5 * (1.0 + th) + 0.5 * x * (1.0 - th * th) * GELU_C * (1.0 + 3.0 * GELU_A * x * x)


def _cmul_add(xr, xi, ar, ai, br, bi):
    return xr + ar * br - ai * bi, xi + ar * bi + ai * br


def _s5_fwd(u, bre, bim, cre, cim, pw, dsk, *, name, tc=512):
    L, D = u.shape
    W = S5_LANES // S5_BLOCKS
    nt = L // tc

    def body(u_ref, bre_ref, bim_ref, cre_ref, cim_ref, pw_ref, d_ref, sre_ref, sim_ref, ypre_ref, yg_ref, carry):
        t = pl.program_id(1)

        @pl.when(t == 0)
        def _():
            carry[...] = jnp.zeros_like(carry)

        uv = u_ref[...]
        ub = uv.astype(BF16)
        sre_ref[...] = jnp.dot(ub, bre_ref[...], preferred_element_type=F32)
        sim_ref[...] = jnp.dot(ub, bim_ref[...], preferred_element_type=F32)

        def tile(i, c):
            cr, ci = c
            rows = pl.ds(pl.multiple_of(i * SUBLANE, SUBLANE), SUBLANE)
            xr, xi = sre_ref[rows, :], sim_ref[rows, :]
            for k, d in enumerate((1, 2, 4)):
                xr, xi = _cmul_add(xr, xi, pw_ref[2 * k], pw_ref[2 * k + 1], pltpu.roll(xr, d, 0),
                                   pltpu.roll(xi, d, 0))
            xr, xi = _cmul_add(xr, xi, pw_ref[6], pw_ref[7], cr, ci)
            sre_ref[rows, :] = xr
            sim_ref[rows, :] = xi
            last = slice(SUBLANE - 1, SUBLANE)
            return jnp.broadcast_to(xr[last, :], (SUBLANE, W)), jnp.broadcast_to(xi[last, :], (SUBLANE, W))

        cr, ci = lax.fori_loop(0, tc // SUBLANE, tile, (carry[0], carry[1]))
        carry[0] = cr
        carry[1] = ci
        y = (jnp.dot(sre_ref[...].astype(BF16), cre_ref[...], preferred_element_type=F32)
             - jnp.dot(sim_ref[...].astype(BF16), cim_ref[...], preferred_element_type=F32) + d_ref[...] * uv)
        ypre_ref[...] = y
        yg_ref[...] = _gelu(y)

    ch = pl.BlockSpec((tc, LANE), lambda j, t: (t, j))
    st = pl.BlockSpec((tc, W), lambda j, t: (t, j))
    bsp = pl.BlockSpec((None, LANE, W), lambda j, t: (j, 0, 0))
    csp = pl.BlockSpec((None, W, LANE), lambda j, t: (j, 0, 0))
    return pl.pallas_call(
        body, name=name,
        out_shape=(jax.ShapeDtypeStruct((L, S5_LANES), F32), jax.ShapeDtypeStruct((L, S5_LANES), F32),
                   jax.ShapeDtypeStruct((L, D), F32), jax.ShapeDtypeStruct((L, D), F32)),
        grid=(S5_BLOCKS, nt),
        in_specs=[ch, bsp, bsp, csp, csp, pl.BlockSpec((8, SUBLANE, W), lambda j, t: (0, 0, j)),
                  pl.BlockSpec((1, LANE), lambda j, t: (0, j))],
        out_specs=(st, st, ch, ch),
        scratch_shapes=[pltpu.VMEM((2, SUBLANE, W), F32)],
        compiler_params=_params("parallel", "arbitrary"))(u, bre, bim, cre, cim, pw, dsk)


def _s5_bwd(dy, u, sre, sim, bre, bim, cre, cim, pwr, dsk, *, name, tc=512):
    L, D = u.shape
    W = S5_LANES // S5_BLOCKS
    nt = L // tc
    ntile = tc // SUBLANE
    nt_dims = (((1,), (1,)), ((), ()))
    tn_dims = (((0,), (0,)), ((), ()))

    def body(dy_ref, u_ref, sre_ref, sim_ref, bre_ref, bim_ref, cre_ref, cim_ref, pw_ref, d_ref,
             du_ref, dbre_ref, dbim_ref, dcre_ref, dcim_ref, ga_ref, dd_ref, gre, gim, carry, gacc):
        t = pl.program_id(1)

        @pl.when(t == 0)
        def _():
            for r in (carry, gacc, dbre_ref, dbim_ref, dcre_ref, dcim_ref, ga_ref, dd_ref):
                r[...] = jnp.zeros_like(r)

        dyv, uv = dy_ref[...], u_ref[...]
        dyb, ub = dyv.astype(BF16), uv.astype(BF16)
        gre[...] = lax.dot_general(dyb, cre_ref[...], nt_dims, preferred_element_type=F32)
        gim[...] = -lax.dot_general(dyb, cim_ref[...], nt_dims, preferred_element_type=F32)
        top = lax.broadcasted_iota(jnp.int32, (SUBLANE, W), 0) == SUBLANE - 1

        def tile(k, c):
            cr, ci = c
            rows = pl.ds(pl.multiple_of((ntile - 1 - k) * SUBLANE, SUBLANE), SUBLANE)
            xr, xi = gre[rows, :], gim[rows, :]
            for q, d in enumerate((1, 2, 4)):
                xr, xi = _cmul_add(xr, xi, pw_ref[2 * q], pw_ref[2 * q + 1], pltpu.roll(xr, SUBLANE - d, 0),
                                   pltpu.roll(xi, SUBLANE - d, 0))
            xr, xi = _cmul_add(xr, xi, pw_ref[6], pw_ref[7], cr, ci)
            gre[rows, :] = xr
            gim[rows, :] = xi
            nr = jnp.where(top, cr, pltpu.roll(xr, SUBLANE - 1, 0))
            ni = jnp.where(top, ci, pltpu.roll(xi, SUBLANE - 1, 0))
            sr, si = sre_ref[rows, :], sim_ref[rows, :]
            gacc[0] += sr * nr + si * ni
            gacc[1] += sr * ni - si * nr
            return jnp.broadcast_to(xr[0:1, :], (SUBLANE, W)), jnp.broadcast_to(xi[0:1, :], (SUBLANE, W))

        cr, ci = lax.fori_loop(0, ntile, tile, (carry[0], carry[1]))
        carry[0] = cr
        carry[1] = ci
        grb, gib = gre[...].astype(BF16), gim[...].astype(BF16)
        du = (lax.dot_general(grb, bre_ref[...], nt_dims, preferred_element_type=F32)
              + lax.dot_general(gib, bim_ref[...], nt_dims, preferred_element_type=F32) + d_ref[...] * dyv)
        du_ref[...] = du.astype(du_ref.dtype)
        dbre_ref[...] += lax.dot_general(ub, grb, tn_dims, preferred_element_type=F32)
        dbim_ref[...] += lax.dot_general(ub, gib, tn_dims, preferred_element_type=F32)
        dcre_ref[...] += lax.dot_general(sre_ref[...].astype(BF16), dyb, tn_dims, preferred_element_type=F32)
        dcim_ref[...] -= lax.dot_general(sim_ref[...].astype(BF16), dyb, tn_dims, preferred_element_type=F32)
        dd_ref[0:1, :] += jnp.sum(dyv * uv, axis=0, keepdims=True)

        @pl.when(t == nt - 1)
        def _():
            ga_ref[0:1, :] = jnp.sum(gacc[0], axis=0, keepdims=True)
            ga_ref[1:2, :] = jnp.sum(gacc[1], axis=0, keepdims=True)

    ch = pl.BlockSpec((tc, LANE), lambda j, t: (nt - 1 - t, j))
    st = pl.BlockSpec((tc, W), lambda j, t: (nt - 1 - t, j))
    bsp = pl.BlockSpec((None, LANE, W), lambda j, t: (j, 0, 0))
    csp = pl.BlockSpec((None, W, LANE), lambda j, t: (j, 0, 0))
    return pl.pallas_call(
        body, name=name,
        out_shape=(jax.ShapeDtypeStruct((L, D), BF16),
                   jax.ShapeDtypeStruct((S5_BLOCKS, LANE, W), F32), jax.ShapeDtypeStruct((S5_BLOCKS, LANE, W), F32),
                   jax.ShapeDtypeStruct((S5_BLOCKS, W, LANE), F32), jax.ShapeDtypeStruct((S5_BLOCKS, W, LANE), F32),
                   jax.ShapeDtypeStruct((SUBLANE, S5_LANES), F32), jax.ShapeDtypeStruct((SUBLANE, D), F32)),
        grid=(S5_BLOCKS, nt),
        in_specs=[ch, ch, st, st, bsp, bsp, csp, csp, pl.BlockSpec((8, SUBLANE, W), lambda j, t: (0, 0, j)),
                  pl.BlockSpec((1, LANE), lambda j, t: (0, j))],
        out_specs=(ch, bsp, bsp, csp, csp, pl.BlockSpec((SUBLANE, W), lambda j, t: (0, j)),
                   pl.BlockSpec((SUBLANE, LANE), lambda j, t: (0, j))),
        scratch_shapes=[pltpu.VMEM((tc, W), F32), pltpu.VMEM((tc, W), F32), pltpu.VMEM((2, SUBLANE, W), F32),
                        pltpu.VMEM((2, SUBLANE, W), F32)],
        compiler_params=_params("parallel", "arbitrary"))(dy, u, sre, sim, bre, bim, cre, cim, pwr, dsk)


def _glu_bwd(dy2, y, t, *, name, tm=256):
    L, D = y.shape

    def body(dy2_ref, y_ref, t_ref, dt_ref, dya_ref, st_ref):
        i = pl.program_id(0)

        @pl.when(i == 0)
        def _():
            st_ref[...] = jnp.zeros_like(st_ref)

        sig = 1.0 / (1.0 + jnp.exp(-t_ref[...]))
        dy2v = dy2_ref[...]
        dt = dy2v * y_ref[...] * sig * (1.0 - sig)
        dt_ref[...] = dt.astype(dt_ref.dtype)
        dya_ref[...] = dy2v * sig
        st_ref[0:1, :] += jnp.sum(dt, axis=0, keepdims=True)

    row = pl.BlockSpec((tm, D), lambda i: (i, 0))
    return pl.pallas_call(
        body, name=name,
        out_shape=(jax.ShapeDtypeStruct((L, D), BF16), jax.ShapeDtypeStruct((L, D), F32),
                   jax.ShapeDtypeStruct((SUBLANE, D), F32)),
        grid=(L // tm,), in_specs=[row, row, row],
        out_specs=(row, row, pl.BlockSpec((SUBLANE, D), lambda i: (0, 0))),
        compiler_params=_params("arbitrary"))(dy2, y, t)


def _s5_prep(a_re, a_im, log_dt, b_re, b_im, c_re, c_im):
    dt = jnp.exp(log_dt)[:, None]
    mag = jnp.exp(a_re * dt)
    abar_re = mag * jnp.cos(a_im * dt)
    abar_im = mag * jnp.sin(a_im * dt)
    den = a_re * a_re + a_im * a_im
    nr = abar_re - 1.0
    ni = abar_im
    f_re = ((nr * a_re + ni * a_im) / den)[..., None]
    f_im = ((ni * a_re - nr * a_im) / den)[..., None]
    bbar_re = f_re * b_re - f_im * b_im
    bbar_im = f_re * b_im + f_im * b_re
    eye = jnp.eye(S5_GROUPS // S5_BLOCKS, dtype=F32)
    gb = S5_GROUPS // S5_BLOCKS

    def blk_b(bb):
        t = bb.reshape(S5_BLOCKS, gb, S5_STATE, S5_GROUP)
        return jnp.einsum('jgph,gk->jghkp', t, eye).reshape(S5_BLOCKS, gb * S5_GROUP, gb * S5_STATE)

    def blk_c(cc):
        t = cc.reshape(S5_BLOCKS, gb, S5_GROUP, S5_STATE)
        return jnp.einsum('jghp,gk->jgpkh', t, eye).reshape(S5_BLOCKS, gb * S5_STATE, gb * S5_GROUP)

    return (abar_re.reshape(1, S5_LANES), abar_im.reshape(1, S5_LANES), blk_b(bbar_re), blk_b(bbar_im),
            blk_c(c_re), blk_c(c_im))


def _s5_power_tables(ar, ai):
    pr, pi = [jnp.ones_like(ar)], [jnp.zeros_like(ai)]
    for _ in range(SUBLANE):
        pr, pi = pr + [pr[-1] * ar - pi[-1] * ai], pi + [pr[-1] * ai + pi[-1] * ar]
    row = jnp.arange(SUBLANE)[:, None]

    def tables(sign, keep, carry_pow):
        out = []
        for d in (1, 2, 4):
            out += [jnp.where(keep(d), pr[d], 0.0), jnp.where(keep(d), sign * pi[d], 0.0)]
        out += [jnp.concatenate([pr[p] for p in carry_pow], 0), sign * jnp.concatenate([pi[p] for p in carry_pow], 0)]
        return jnp.stack([jnp.broadcast_to(o, (SUBLANE, ar.shape[1])) for o in out])

    fwd = tables(1.0, lambda d: row >= d, [r + 1 for r in range(SUBLANE)])
    rev = tables(-1.0, lambda d: row + d <= SUBLANE - 1, [SUBLANE - r for r in range(SUBLANE)])
    return fwd, rev


ADAMW_PART_BLOCK_BYTES = 2 * 1024 * 1024


def _adamw(w, parts, m, v, *, name):
    n, R, C = w.shape
    assert len(parts) == n
    P = parts[0].shape[0]
    tr = R
    while P * tr * C * parts[0].dtype.itemsize > ADAMW_PART_BLOCK_BYTES and tr % 16 == 0:
        tr //= 2
    c1 = 1.0 / (1.0 - ADAM_B1 ** ADAM_STEP)
    c2 = 1.0 / (1.0 - ADAM_B2 ** ADAM_STEP)

    def body(*refs):
        w_ref, m_ref, v_ref = refs[:3]
        p_refs = refs[3:3 + n]
        g_ref, d_ref, nm_ref, nv_ref = refs[3 + n:]
        layer = pl.program_id(0)
        for q, p_ref in enumerate(p_refs):
            @pl.when(layer == q)
            def _(p_ref=p_ref):
                g = p_ref[0].astype(F32)
                for s in range(1, P):
                    g = g + p_ref[s].astype(F32)
                nm = ADAM_B1 * m_ref[...] + (1.0 - ADAM_B1) * g
                nv = ADAM_B2 * v_ref[...] + (1.0 - ADAM_B2) * (g * g)
                g_ref[...] = g
                nm_ref[...] = nm
                nv_ref[...] = nv
                d_ref[...] = -ADAM_LR * ((nm * c1) / (jnp.sqrt(nv * c2) + ADAM_EPS) + ADAM_WD * w_ref[...])

    row = pl.BlockSpec((None, tr, C), lambda l, i: (l, i, 0))
    part_specs = [pl.BlockSpec((P, tr, C), lambda l, i, q=q: (0, jnp.where(l == q, i, 0), 0)) for q in range(n)]
    out = jax.ShapeDtypeStruct((n, R, C), F32)
    return pl.pallas_call(body, name=name, out_shape=(out, out, out, out), grid=(n, R // tr),
                          in_specs=[row, row, row] + part_specs, out_specs=(row, row, row, row),
                          compiler_params=_params("arbitrary", "arbitrary"))(w, m, v, *parts)


def _all_gather(xs, axis, *, name):
    m = xs.shape[axis]
    out_shape = list(xs.shape)
    out_shape[axis] = N_DEV * m

    def body(x_ref, out_ref, send_sems, recv_sems, local_sem):
        x, y, c = _my_pos()
        me, sibling = (x, y, c), (x, y, 1 - c)
        chips = [(1 - x, y), (x, 1 - y), (1 - x, 1 - y)]

        def blk(px, py, pc):
            idx = [slice(None)] * 3
            idx[axis] = pl.ds((4 * px + 2 * py + pc) * m, m)
            return out_ref.at[tuple(idx)]

        def copy(k, block, to, src=None):
            return pltpu.make_async_remote_copy(src_ref=blk(*block) if src is None else src, dst_ref=blk(*block),
                                                send_sem=send_sems.at[k], recv_sem=recv_sems.at[k],
                                                device_id=to, device_id_type=MESH_ID)

        mine = pltpu.make_async_copy(x_ref, blk(*me), local_sem)
        mine.start()
        first = [copy(0, me, sibling, src=x_ref)]
        first += [copy(1 + j, me, (*chip, c), src=x_ref) for j, chip in enumerate(chips)]
        for cp in first:
            cp.start()
        passed = [copy(4 + j, (*chip, c), sibling) for j, chip in enumerate(chips)]
        for j, chip in enumerate(chips):
            copy(1 + j, (*chip, c), me).wait_recv()
            passed[j].start()
        copy(0, sibling, me).wait_recv()
        for j, chip in enumerate(chips):
            copy(4 + j, (*chip, 1 - c), me).wait_recv()
        for cp in first + passed:
            cp.wait_send()
        mine.wait()

    hbm = pl.BlockSpec(memory_space=pl.ANY)
    return pl.pallas_call(body, name=name, out_shape=jax.ShapeDtypeStruct(tuple(out_shape), xs.dtype),
                          in_specs=[hbm], out_specs=hbm,
                          scratch_shapes=[pltpu.SemaphoreType.DMA((N_DEV - 1,)), pltpu.SemaphoreType.DMA((N_DEV - 1,)),
                                          pltpu.SemaphoreType.DMA],
                          compiler_params=pltpu.CompilerParams(has_side_effects=True))(xs)


def _block(ref, axis, idx, m):
    return ref.at[pl.ds(idx * m, m), :] if axis == 0 else ref.at[:, pl.ds(idx * m, m)]


def _exchange_copies(metas, src_refs, zone_refs, send_sems, recv_sems, group):
    x, y, c = _my_pos()
    me = 4 * x + 2 * y + c
    base = group * N_DEV
    pairs = []
    for r in range(1, N_DEV):
        pos = (1 - x if r & 4 else x, 1 - y if r & 2 else y, 1 - c if r & 1 else c)
        peer = 4 * pos[0] + 2 * pos[1] + pos[2]
        for (kind, axis, m), s_ref, z_ref in zip(metas, src_refs, zone_refs):
            if kind == 'gather':
                src, dst, arrival = s_ref, _block(z_ref, axis, me, m), _block(z_ref, axis, peer, m)
            else:
                src, dst, arrival = _block(s_ref, axis, peer, m), z_ref.at[me], z_ref.at[peer]
            pairs.append(tuple(
                pltpu.make_async_remote_copy(src_ref=src, dst_ref=d, send_sem=send_sems.at[base + r - 1],
                                             recv_sem=recv_sems.at[base + r - 1], device_id=pos,
                                             device_id_type=MESH_ID)
                for d in (dst, arrival)))
    own = []
    for (kind, axis, m), s_ref, z_ref in zip(metas, src_refs, zone_refs):
        src, dst = (s_ref, _block(z_ref, axis, me, m)) if kind == 'gather' else (_block(s_ref, axis, me, m), z_ref.at[me])
        own.append(pltpu.make_async_copy(src, dst, recv_sems.at[base + N_DEV - 1]))
    return pairs, own


def _exchange_start(groups, after, *, name):
    flat = [it for g in groups for it in g]
    n, ng = len(flat), len(groups)
    metas = [it[2] for it in flat]
    bounds = [(sum(len(g) for g in groups[:q]), sum(len(g) for g in groups[:q + 1])) for q in range(ng)]

    def body(*refs):
        src_refs = refs[:n]
        send_sems, recv_sems = refs[n + 1], refs[n + 2]
        zone_refs = refs[2 * n + 3:3 * n + 3]
        token = refs[-1]
        for q, (lo, hi) in enumerate(bounds):
            pairs, own = _exchange_copies(metas[lo:hi], src_refs[lo:hi], zone_refs[lo:hi], send_sems, recv_sems, q)
            for outgoing, _ in pairs:
                outgoing.start()
            for cp in own:
                cp.start()
        token[...] = jnp.zeros_like(token)

    hbm = pl.BlockSpec(memory_space=pltpu.HBM)
    sem = pl.BlockSpec(memory_space=pltpu.SEMAPHORE)
    srcs = [it[0] for it in flat]
    res = pl.pallas_call(
        body, name=name,
        out_shape=(pltpu.SemaphoreType.DMA((ng * N_DEV,)), pltpu.SemaphoreType.DMA((ng * N_DEV,)),
                   *[pltpu.HBM(a.shape, a.dtype) for a in srcs], *[pltpu.HBM(it[1], it[0].dtype) for it in flat],
                   jax.ShapeDtypeStruct((SUBLANE, LANE), F32)),
        in_specs=[hbm] * n + [pl.BlockSpec(memory_space=pl.ANY)],
        out_specs=(sem, sem, *[hbm] * (2 * n), pl.BlockSpec(memory_space=pltpu.VMEM)),
        input_output_aliases={q: 2 + q for q in range(n)},
        compiler_params=pltpu.CompilerParams(has_side_effects=pltpu.SideEffectType.DATAFLOW_SIDE_EFFECTING),
    )(*[pltpu.with_memory_space_constraint(a, pltpu.HBM) for a in srcs], after)
    handles = [(res[0], res[1], q, list(res[2 + lo:2 + hi]), list(res[2 + n + lo:2 + n + hi]), metas[lo:hi])
               for q, (lo, hi) in enumerate(bounds)]
    return handles, res[-1]


def _exchange_wait(handle, after, *, name):
    send_sems, recv_sems, group, srcs, zones, metas = handle
    n = len(srcs)

    def body(*refs):
        src_refs, zone_refs = refs[:n], refs[n:2 * n]
        s_sems, r_sems = refs[2 * n], refs[2 * n + 1]
        pairs, own = _exchange_copies(metas, src_refs, zone_refs, s_sems, r_sems, group)
        for outgoing, incoming in pairs:
            outgoing.wait_send()
            incoming.wait_recv()
        for cp in own:
            cp.wait()

    hbm = pl.BlockSpec(memory_space=pltpu.HBM)
    sem = pl.BlockSpec(memory_space=pltpu.SEMAPHORE)
    arrays = srcs + zones
    res = pl.pallas_call(
        body, name=name,
        out_shape=tuple(pltpu.HBM(a.shape, a.dtype) for a in arrays),
        in_specs=[hbm] * (2 * n) + [sem, sem, pl.BlockSpec(memory_space=pl.ANY)],
        out_specs=tuple([hbm] * (2 * n)),
        input_output_aliases={q: q for q in range(2 * n)},
        compiler_params=pltpu.CompilerParams(has_side_effects=pltpu.SideEffectType.DATAFLOW_SIDE_EFFECTING),
    )(*arrays, send_sems, recv_sems, after)
    return list(res[n:])


def _pad_rows(a, rows):
    return jnp.pad(a, ((0, rows - a.shape[0]), (0, 0)))


def _rows(a):
    flat = a.reshape(-1).astype(F32)
    pad = -flat.shape[0] % (SUBLANE * LANE)
    return (jnp.pad(flat, (0, pad)) if pad else flat).reshape(-1, LANE)


def _pack_rows(arrays):
    return jnp.concatenate([_rows(a) for a in arrays], 0)


def _unpack_rows(t, shapes):
    out, off = [], 0
    for shp in shapes:
        size = math.prod(shp)
        rows = -(-size // (SUBLANE * LANE)) * SUBLANE
        out.append(t[off:off + rows].reshape(-1)[:size].reshape(shp))
        off += rows
    return out


def _stat_row(st, r):
    return st[r:r + 1, :]


def kernel(x, c, ada_w, ada_b, norm1_g, norm2_g, ff_w1, ff_w2, final_g, conv_w_in, conv_w, conv_b, conv_w_out, ssm_w_in, ssm_a_re, ssm_a_im, ssm_log_dt, ssm_b_re, ssm_b_im, ssm_c_re, ssm_c_im, ssm_d, ssm_glu_w, ssm_glu_b, ssm_w_out, sg_w_in, sg_v_g, sg_w_s, sg_b_s, sg_w_out, loss_target, m_ada_w, m_ada_b, m_norm1_g, m_norm2_g, m_ff_w1, m_ff_w2, m_final_g, m_conv_w_in, m_conv_w, m_conv_b, m_conv_w_out, m_ssm_w_in, m_ssm_a_re, m_ssm_a_im, m_ssm_log_dt, m_ssm_b_re, m_ssm_b_im, m_ssm_c_re, m_ssm_c_im, m_ssm_d, m_ssm_glu_w, m_ssm_glu_b, m_ssm_w_out, m_sg_w_in, m_sg_v_g, m_sg_w_s, m_sg_b_s, m_sg_w_out, v_ada_w, v_ada_b, v_norm1_g, v_norm2_g, v_ff_w1, v_ff_w2, v_final_g, v_conv_w_in, v_conv_w, v_conv_b, v_conv_w_out, v_ssm_w_in, v_ssm_a_re, v_ssm_a_im, v_ssm_log_dt, v_ssm_b_re, v_ssm_b_im, v_ssm_c_re, v_ssm_c_im, v_ssm_d, v_ssm_glu_w, v_ssm_glu_b, v_ssm_w_out, v_sg_w_in, v_sg_v_g, v_sg_w_s, v_sg_b_s, v_sg_w_out):
    P = dict(zip(INPUTS, (x, c, ada_w, ada_b, norm1_g, norm2_g, ff_w1, ff_w2, final_g, conv_w_in, conv_w, conv_b, conv_w_out, ssm_w_in, ssm_a_re, ssm_a_im, ssm_log_dt, ssm_b_re, ssm_b_im, ssm_c_re, ssm_c_im, ssm_d, ssm_glu_w, ssm_glu_b, ssm_w_out, sg_w_in, sg_v_g, sg_w_s, sg_b_s, sg_w_out, loss_target, m_ada_w, m_ada_b, m_norm1_g, m_norm2_g, m_ff_w1, m_ff_w2, m_final_g, m_conv_w_in, m_conv_w, m_conv_b, m_conv_w_out, m_ssm_w_in, m_ssm_a_re, m_ssm_a_im, m_ssm_log_dt, m_ssm_b_re, m_ssm_b_im, m_ssm_c_re, m_ssm_c_im, m_ssm_d, m_ssm_glu_w, m_ssm_glu_b, m_ssm_w_out, m_sg_w_in, m_sg_v_g, m_sg_w_s, m_sg_b_s, m_sg_w_out, v_ada_w, v_ada_b, v_norm1_g, v_norm2_g, v_ff_w1, v_ff_w2, v_final_g, v_conv_w_in, v_conv_w, v_conv_b, v_conv_w_out, v_ssm_w_in, v_ssm_a_re, v_ssm_a_im, v_ssm_log_dt, v_ssm_b_re, v_ssm_b_im, v_ssm_c_re, v_ssm_c_im, v_ssm_d, v_ssm_glu_w, v_ssm_glu_b, v_ssm_w_out, v_sg_w_in, v_sg_v_g, v_sg_w_s, v_sg_b_s, v_sg_w_out)))
    L, D = x.shape[1], x.shape[2]
    me = _my_index()
    xs = x[0]
    tgt = loss_target[0]
    n_conv = conv_w_in.shape[0]

    def gather_item(shard, axis):
        full = tuple(N_DEV * s if a == axis else s for a, s in enumerate(shard.shape))
        return shard, full, ('gather', axis, shard.shape[axis])

    def mixer_shards(i):
        kind, j = i % 3, i // 3
        if kind == 0:
            return [(conv_w_in[j], 1), (conv_w_out[j], 0)]
        if kind == 1:
            return [(ssm_w_in[j], 0), (ssm_glu_w[j], 0), (ssm_w_out[j], 0)]
        return [(sg_w_in[j], 1), (sg_w_out[j], 0)]

    c_act = c * (1.0 / (1.0 + jnp.exp(-c)))
    vec_rows = jnp.concatenate([c_act.reshape(D // LANE, LANE), conv_w.reshape(-1, LANE), conv_b.reshape(-1, LANE),
                                sg_v_g.reshape(-1, LANE)], 0)
    n_vec = vec_rows.shape[0]
    vec_all = _all_gather(_pad_rows(vec_rows, 24)[None], 0, name="gather_vectors")
    c_all = vec_all[:, :D // LANE, :].reshape(N_DEV, D)
    sharded_full = vec_all[:, D // LANE:n_vec, :].transpose(1, 0, 2).reshape(n_vec - D // LANE, D)
    conv_w_full = sharded_full[:3 * n_conv].reshape(n_conv, 3, D)
    conv_b_full = sharded_full[3 * n_conv:4 * n_conv]
    sg_vg_full = sharded_full[4 * n_conv:4 * n_conv + 1]

    c_pad = _pad_rows(c_all, LANE)
    ncol = ada_w.shape[2]
    mod_part = jnp.stack([_mm(c_pad, ada_w[i], name=f"ada_fwd{i}")[:N_DEV] for i in range(DEPTH)])
    mod_all = _all_gather(mod_part.reshape(1, DEPTH * N_DEV, ncol), 0, name="gather_mod")
    mod_all = mod_all.reshape(N_DEV, DEPTH, N_DEV, ncol)
    mod_me = lax.dynamic_index_in_dim(mod_all, me, 2, keepdims=False)
    mod = mod_me.transpose(1, 0, 2).reshape(DEPTH, N_DEV * ncol) + ada_b
    gathers, gather_token = _exchange_start(
        [[gather_item(w.astype(BF16), ax) for w, ax in shards]
         for i in range(DEPTH) for shards in (mixer_shards(i), [(ff_w1[i], 1), (ff_w2[i], 0)])],
        mod, name="gather_start")
    mod = mod + gather_token[0:1, 0:1]

    s5_args = (ssm_a_re[0], ssm_a_im[0], ssm_log_dt[0], ssm_b_re[0], ssm_b_im[0], ssm_c_re[0], ssm_c_im[0])
    (abar_re, abar_im, bblk_re, bblk_im, cblk_re, cblk_im), s5_vjp = jax.vjp(_s5_prep, *s5_args)
    pw_fwd, pw_rev = _s5_power_tables(abar_re, abar_im)
    s5_w = tuple(t.astype(BF16) for t in (bblk_re, bblk_im, cblk_re, cblk_im))
    causal = jnp.tril(jnp.ones((SG_CHUNK, SG_CHUNK), dtype=bool))
    ws_m = jnp.where(causal[None], sg_w_s[0], 0.0)
    ws_b = ws_m.astype(BF16)
    wst_b = ws_m.transpose(0, 2, 1).astype(BF16)
    bsb = jnp.broadcast_to(sg_b_s[0][:, :, None], (SG_HEADS, SG_CHUNK, LANE))

    saved = []
    xa = xs
    for i in range(DEPTH):
        kind, j = i % 3, i // 3
        sh1, sc1, g1, sh2, sc2, g2 = (mod[i:i + 1, q * D:(q + 1) * D] for q in range(6))
        wn1 = norm1_g[i:i + 1] * (1.0 + sc1)
        wn2 = norm2_g[i:i + 1] * (1.0 + sc2)
        S = dict(x_in=xa, g1=g1, g2=g2, sc1=sc1, sc2=sc2, wn1=wn1, wn2=wn2)
        h1 = _normmod_fwd(xa, wn1, sh1, name=f"norm1_fwd{i}")
        w_mix = _exchange_wait(gathers[2 * i], h1, name=f"gather_mix_wait{i}")
        S['h1'] = h1
        if kind == 0:
            bcx = _mm(h1, w_mix[0], name=f"conv_in{i}", out_dtypes=(BF16,), bm=2048)
            wb = _pad_rows(jnp.concatenate([conv_w_full[j], conv_b_full[j:j + 1]], 0), SUBLANE)
            pb = _conv_fwd(bcx, wb, name=f"conv_mix{i}")
            S.update(bcx=bcx, wb=wb, pb=pb)
        elif kind == 1:
            u = _mm(h1, w_mix[0], name=f"ssm_in{i}")
            sre, sim, ypre, yg = _s5_fwd(u, *s5_w, pw_fwd, ssm_d, name=f"s5_scan{i}")

            def glu_epi(acc, yv, bias):
                t = acc + bias
                return yv * (1.0 / (1.0 + jnp.exp(-t))), t

            pb, tt = _mm(yg, w_mix[1], name=f"ssm_glu{i}", out_dtypes=(BF16, F32), epi=glu_epi,
                         extras=[(yg, 'mn'), (ssm_glu_b, 'n')])
            S.update(u=u, sre=sre, sim=sim, ypre=ypre, yg=yg, pb=pb, tt=tt)
        else:
            uv = _mm(h1, w_mix[0], name=f"sg_in{i}", bm=2048)
            pb = _sg_fwd(uv, sg_vg_full, ws_b, bsb, name=f"sg_mix{i}")
            S.update(uv=uv, pb=pb)
        x_mid, y_mix = _mm(pb, w_mix[-1], name=f"mix_out{i}", out_dtypes=(F32, BF16), epi=_epi_residual,
                           extras=[(xa, 'mn'), (g1, 'n')])
        h2 = _normmod_fwd(x_mid, wn2, sh2, name=f"norm2_fwd{i}")
        w1_full, w2_full = _exchange_wait(gathers[2 * i + 1], h2, name=f"gather_ff_wait{i}")
        S.update(w_mix=w_mix, w1=w1_full, w2=w2_full)
        ra = _mm(h2, w1_full, name=f"ff_up{i}", out_dtypes=(BF16,), epi=lambda acc: (jnp.maximum(acc, 0.0),), bm=2048)
        xa, f_out = _mm(ra, w2_full, name=f"ff_down{i}", out_dtypes=(F32, BF16), epi=_epi_residual, a_fn=_square,
                        extras=[(x_mid, 'mn'), (g2, 'n')], bm=256, bk=w2_full.shape[0])
        S.update(x_mid=x_mid, y_mix=y_mix, h2=h2, ra=ra, f_out=f_out)
        saved.append(S)

    S = saved[-1]
    dx, st, dfb, loss_tile = _loss_head(xa, tgt, final_g[None], S['f_out'], S['g2'], name="loss_head")
    loss = lax.psum(loss_tile[0, 0], ("x", "y", "c"))
    d_final_g = _stat_row(st, 0)
    dg2_next = _stat_row(st, 2)

    def scatter_item(g, axis):
        m = g.shape[axis] // N_DEV
        shard = tuple(m if a == axis else s for a, s in enumerate(g.shape))
        return g, (N_DEV,) + shard, ('scatter', axis, m)

    dmod = [None] * DEPTH
    dn1g, dn2g = [None] * DEPTH, [None] * DEPTH
    d_conv_w, d_conv_b = [None] * n_conv, [None] * n_conv
    ff_sent, mix_sent = [None] * DEPTH, [None] * DEPTH
    small = {}
    for i in reversed(range(DEPTH)):
        kind, j = i % 3, i // 3
        S = saved[i]
        w_mix = S['w_mix']
        dg2 = dg2_next
        da = _mm(dfb, S['w2'], tb=True, name=f"ff_down_bwd{i}", out_dtypes=(BF16,), bm=2048,
                 epi=lambda acc, rav: (acc * (2.0 * rav.astype(F32)),), extras=[(S['ra'], 'mn')])
        dw2 = _wgrad(S['ra'], dfb, name=f"ff_w2_grad{i}", a_fn=_square, bm=256, bn=1024)
        dh2 = _mm(da, S['w1'], tb=True, name=f"ff_up_bwd{i}", out_dtypes=(BF16,), bm=512, bk=da.shape[1])
        dw1 = _wgrad(S['h2'], da, name=f"ff_w1_grad{i}")
        (ff_sent[i],), token = _exchange_start([[scatter_item(dw1, 1), scatter_item(dw2, 0)]], dx,
                                               name=f"ff_grads_start{i}")
        dx_mid, st2, dyb = _normmod_bwd(dh2, S['x_mid'], S['wn2'] + token[0:1, 0:1], dx,
                                        (S['y_mix'], S['g1']), name=f"norm2_bwd{i}")
        dsc2 = _stat_row(st2, 0) * norm2_g[i:i + 1]
        dn2g[i] = _stat_row(st2, 0) * (1.0 + S['sc2'])
        dsh2 = _stat_row(st2, 1)
        dg1 = _stat_row(st2, 2)
        if kind == 0:
            dp = _mm(dyb, w_mix[1], tb=True, name=f"conv_out_bwd{i}", out_dtypes=(BF16,))
            d_cwo = _wgrad(S['pb'], dyb, name=f"conv_w_out_grad{i}")
            dbcx, stc = _conv_bwd(dp, S['bcx'], S['wb'], name=f"conv_mix_bwd{i}")
            d_conv_w[j] = stc[0:3]
            d_conv_b[j] = stc[3:4]
            dh1 = _mm(dbcx, w_mix[0], tb=True, name=f"conv_in_bwd{i}", out_dtypes=(BF16,), bm=512)
            d_cwi = _wgrad(S['h1'], dbcx, name=f"conv_w_in_grad{i}")
            mix_grads = [scatter_item(d_cwi, 1), scatter_item(d_cwo, 0)]
        elif kind == 1:
            dy2 = _mm(dyb, w_mix[2], tb=True, name=f"ssm_out_bwd{i}")
            d_ssm_out = _wgrad(S['pb'], dyb, name=f"ssm_w_out_grad{i}")
            dtb, dya, stg = _glu_bwd(dy2, S['yg'], S['tt'], name=f"ssm_glu_bwd{i}")
            dypre = _mm(dtb, w_mix[1], tb=True, name=f"ssm_glu_in_bwd{i}",
                        epi=lambda acc, a, yp: ((a + acc) * _gelu_grad(yp),),
                        extras=[(dya, 'mn'), (S['ypre'], 'mn')])
            d_glu = _wgrad(S['yg'], dtb, name=f"ssm_glu_w_grad{i}", bm=512)
            dub, dbre, dbim, dcre, dcim, ga, dd = _s5_bwd(dypre, S['u'], S['sre'], S['sim'], *s5_w, pw_rev, ssm_d,
                                                           name=f"s5_scan_bwd{i}")
            dh1 = _mm(dub, w_mix[0], tb=True, name=f"ssm_in_bwd{i}", out_dtypes=(BF16,))
            d_ssm_in = _wgrad(S['h1'], dub, name=f"ssm_w_in_grad{i}")
            da_re, da_im, dlog_dt, db_re, db_im, dc_re, dc_im = s5_vjp((ga[0:1], ga[1:2], dbre, dbim, dcre, dcim))
            s5_small = _pack_rows([da_re, da_im, dlog_dt, db_re, db_im, dc_re, dc_im, dd[0], stg[0]])
            mix_grads = [scatter_item(d_ssm_in, 0), scatter_item(d_glu, 0), scatter_item(d_ssm_out, 0),
                         gather_item(s5_small, 0)]
        else:
            dp = _mm(dyb, w_mix[1], tb=True, name=f"sg_out_bwd{i}")
            d_sgo = _wgrad(S['pb'], dyb, name=f"sg_w_out_grad{i}")
            duv, dws, dbs, stv = _sg_bwd(dp, S['uv'], sg_vg_full, ws_b, wst_b, bsb, name=f"sg_mix_bwd{i}")
            dh1 = _mm(duv, w_mix[0], tb=True, name=f"sg_in_bwd{i}", out_dtypes=(BF16,), bm=512, bk=duv.shape[1])
            d_sgi = _wgrad(S['h1'], duv, name=f"sg_w_in_grad{i}")
            sg_small = _pack_rows([jnp.where(causal[None], dws, 0.0), jnp.sum(dbs, axis=-1)])
            d_sg_vg = stv[0:1]
            mix_grads = [scatter_item(d_sgi, 1), scatter_item(d_sgo, 0), gather_item(sg_small, 0)]
        (mix_sent[i],), token = _exchange_start([mix_grads], dx_mid, name=f"mix_grads_start{i}")
        wn1 = S['wn1'] + token[0:1, 0:1]
        if i > 0:
            prev = saved[i - 1]
            dx, st1, dfb = _normmod_bwd(dh1, S['x_in'], wn1, dx_mid, (prev['f_out'], prev['g2']),
                                        name=f"norm1_bwd{i}")
            dg2_next = _stat_row(st1, 2)
        else:
            dx, st1 = _normmod_bwd(dh1, S['x_in'], wn1, dx_mid, None, name=f"norm1_bwd{i}")
        dsc1 = _stat_row(st1, 0) * norm1_g[i:i + 1]
        dn1g[i] = _stat_row(st1, 0) * (1.0 + S['sc1'])
        dsh1 = _stat_row(st1, 1)
        dmod[i] = jnp.concatenate([dsh1, dsc1, dg1, dsh2, dsc2, dg2], 1)
    grad_x = dx[None]

    out = {}

    def small_group(names, parts, label):
        shapes = [P[n].shape for n in names]
        w, m, v = (_pack_rows([P[pre + n] for n in names])[None] for pre in ('', 'm_', 'v_'))
        res = [_unpack_rows(t[0], shapes) for t in _adamw(w, [parts], m, v, name=label)]
        for q, n in enumerate(names):
            out[n] = tuple(r[q] for r in res)

    small.update(ada_b=jnp.concatenate(dmod, 0), norm1_g=jnp.concatenate(dn1g, 0), norm2_g=jnp.concatenate(dn2g, 0),
                 final_g=d_final_g, conv_w=jnp.stack(d_conv_w), conv_b=jnp.concatenate(d_conv_b, 0), sg_v_g=d_sg_vg)
    last_pack = _pack_rows([small[n] for n in LAST_SMALL + SMALL_SHARD])
    n_last = _pack_rows([P[n] for n in LAST_SMALL]).shape[0]
    pack_all = _all_gather(last_pack[None], 0, name="gather_small_grads")
    small_group(LAST_SMALL, pack_all[:, :n_last], "adamw_small")
    sh_rows = (last_pack.shape[0] - n_last) // N_DEV
    sh_parts = pack_all[:, n_last:].reshape(N_DEV, sh_rows, N_DEV, LANE)
    sh_parts = lax.dynamic_index_in_dim(sh_parts, me, 2, keepdims=False)
    sh_parts = jnp.pad(sh_parts, ((0, 0), (0, 16 - sh_rows), (0, 0)))

    def pack_shard(prefix):
        return _pad_rows(jnp.concatenate([P[prefix + n].reshape(-1, LANE) for n in SMALL_SHARD], 0), 16)[None]

    sg_, sd_, sm_, sv_ = _adamw(pack_shard(''), [sh_parts], pack_shard('m_'), pack_shard('v_'), name="adamw_channel")
    off = 0
    for n in SMALL_SHARD:
        rows = math.prod(P[n].shape) // LANE
        out[n] = tuple(t[0, off:off + rows].reshape(P[n].shape) for t in (sg_, sd_, sm_, sv_))
        off += rows

    dmod_all = pack_all[:, :DEPTH * 6 * D // LANE].reshape(N_DEV, DEPTH, 6 * D)
    dmod_cols = lax.dynamic_slice_in_dim(dmod_all, me * ncol, ncol, 2)
    g_ada = [_mm(c_pad, _pad_rows(dmod_cols[:, i], LANE), ta=True, name=f"ada_w_grad{i}")[None] for i in range(DEPTH)]

    def big(name, parts):
        res = _adamw(P[name], parts, P['m_' + name], P['v_' + name], name="adamw_" + name)
        out[name] = res
        return res[1]

    ff_parts = [_exchange_wait(ff_sent[i], dx, name=f"ff_grads_wait{i}") for i in range(DEPTH)]
    mix_parts = [None] + [_exchange_wait(mix_sent[i], dx, name=f"mix_grads_wait{i}") for i in range(1, DEPTH)]
    big('ada_w', g_ada)
    big('ff_w1', [p[0] for p in ff_parts])
    big('ff_w2', [p[1] for p in ff_parts])
    done = big('sg_w_in', [mix_parts[2][0]])
    mix_parts[0] = _exchange_wait(mix_sent[0], done, name="mix_grads_wait0")
    big('conv_w_in', [mix_parts[i][0] for i in range(DEPTH) if i % 3 == 0])
    row_names = ['conv_w_out', 'ssm_w_in', 'ssm_glu_w', 'ssm_w_out', 'sg_w_out']
    row_parts = ([mix_parts[i][1] for i in range(DEPTH) if i % 3 == 0] + mix_parts[1][:3] + [mix_parts[2][1]])
    small_group(S5_SMALL, mix_parts[1][3].reshape(N_DEV, -1, LANE), "adamw_s5")
    small_group(SG_SMALL, mix_parts[2][2].reshape(N_DEV, -1, LANE), "adamw_sg")
    row_w, row_m, row_v = (jnp.concatenate([P[pre + n] for n in row_names], 0) for pre in ('', 'm_', 'v_'))
    rw = _adamw(row_w, row_parts, row_m, row_v, name="adamw_row_sharded")
    off = 0
    for n in row_names:
        cnt = P[n].shape[0]
        out[n] = tuple(t[off:off + cnt] for t in rw)
        off += cnt

    return (loss, grad_x, *[out[n][0] for n in WEIGHTS], *[out[n][1] for n in WEIGHTS],
            *[out[n][2] for n in WEIGHTS], *[out[n][3] for n in WEIGHTS])
```

```python
import math

import jax
import jax.numpy as jnp
from jax import lax
from jax.experimental import pallas as pl
from jax.experimental.pallas import tpu as pltpu

F32 = jnp.float32
BF16 = jnp.bfloat16

N_DEV = 8
MESH_ID = pl.DeviceIdType.MESH
DEPTH = 4
EPS = 1e-6
S5_GROUPS, S5_GROUP, S5_STATE = 64, 16, 64
S5_LANES = S5_GROUPS * S5_STATE
S5_BLOCKS = 8
S5_CHUNK = 512
SG_HEADS, SG_CHUNK = 8, 128
LANE = 128
SUBLANE = 8
VMEM_LIMIT = 48 * 1024 * 1024
ADAM_LR, ADAM_B1, ADAM_B2, ADAM_EPS, ADAM_WD, ADAM_STEP = 0.001, 0.9, 0.999, 1e-08, 0.01, 10
GELU_C = math.sqrt(2.0 / math.pi)
GELU_A = 0.044715

WEIGHTS = ['ada_w', 'ada_b', 'norm1_g', 'norm2_g', 'ff_w1', 'ff_w2', 'final_g', 'conv_w_in', 'conv_w', 'conv_b',
           'conv_w_out', 'ssm_w_in', 'ssm_a_re', 'ssm_a_im', 'ssm_log_dt', 'ssm_b_re', 'ssm_b_im', 'ssm_c_re',
           'ssm_c_im', 'ssm_d', 'ssm_glu_w', 'ssm_glu_b', 'ssm_w_out', 'sg_w_in', 'sg_v_g', 'sg_w_s', 'sg_b_s',
           'sg_w_out']
INPUTS = ['x', 'c'] + WEIGHTS + ['loss_target'] + ['m_' + n for n in WEIGHTS] + ['v_' + n for n in WEIGHTS]
S5_SMALL = ['ssm_a_re', 'ssm_a_im', 'ssm_log_dt', 'ssm_b_re', 'ssm_b_im', 'ssm_c_re', 'ssm_c_im', 'ssm_d', 'ssm_glu_b']
SG_SMALL = ['sg_w_s', 'sg_b_s']
LAST_SMALL = ['ada_b', 'norm1_g', 'norm2_g', 'final_g']
SMALL_SHARD = ['conv_w', 'conv_b', 'sg_v_g']


def _params(*sem):
    return pltpu.CompilerParams(dimension_semantics=sem or None, vmem_limit_bytes=VMEM_LIMIT)


def _my_pos():
    return lax.axis_index("x"), lax.axis_index("y"), lax.axis_index("c")


def _my_index():
    x, y, c = _my_pos()
    return 4 * x + 2 * y + c


def _mm(a, b, *, name, ta=False, tb=False, out_dtypes=(F32,), epi=None, extras=(), a_fn=None, bm=1024, bn=1024,
        bk=1024):
    a_chunks = a.shape[0] if a.ndim == 3 else 0
    b_chunks = b.shape[0] if b.ndim == 3 else 0
    assert not (a_chunks and ta) and not (b_chunks and tb)
    if a_chunks:
        m, k = a.shape[1], a_chunks * a.shape[2]
        bk = k
    else:
        m, k = (a.shape[1], a.shape[0]) if ta else a.shape
    if b_chunks:
        k2, n = b.shape[1], b_chunks * b.shape[2]
        bn = min(bn, b.shape[2])
    else:
        k2, n = (b.shape[1], b.shape[0]) if tb else b.shape
    assert k == k2, (a.shape, b.shape, ta, tb)
    bm, bn, bk = min(bm, m), min(bn, n), min(bk, k)
    assert m % bm == 0 and n % bn == 0 and k % bk == 0, (m, n, k, bm, bn, bk)
    nk = k // bk
    n_ex, n_out = len(extras), len(out_dtypes)
    dims = (((0 if ta else 1,), (1 if tb else 0,)), ((), ()))

    def body(*refs):
        a_ref, b_ref = refs[0], refs[1]
        ex_refs = refs[2:2 + n_ex]
        out_refs = refs[2 + n_ex:2 + n_ex + n_out]

        def finish(acc):
            outs = epi(acc, *[r[...] for r in ex_refs]) if epi is not None else (acc,)
            for r, o in zip(out_refs, outs):
                r[...] = o.astype(r.dtype)

        av = jnp.concatenate([a_ref[t] for t in range(a_chunks)], axis=1) if a_chunks else a_ref[...]
        if a_fn is not None:
            av = a_fn(av)
        part = lax.dot_general(av.astype(BF16), b_ref[...].astype(BF16), dims, preferred_element_type=F32)
        if nk == 1:
            finish(part)
            return
        acc_ref = refs[-1]
        kk = pl.program_id(2)

        @pl.when(kk == 0)
        def _():
            acc_ref[...] = part

        @pl.when(kk > 0)
        def _():
            acc_ref[...] += part

        @pl.when(kk == nk - 1)
        def _():
            finish(acc_ref[...])

    if a_chunks:
        a_spec = pl.BlockSpec((a_chunks, bm, a.shape[2]), lambda i, j, q: (0, i, 0))
    elif ta:
        a_spec = pl.BlockSpec((bk, bm), lambda i, j, q: (q, i))
    else:
        a_spec = pl.BlockSpec((bm, bk), lambda i, j, q: (i, q))
    if b_chunks:
        per = b.shape[2] // bn
        b_spec = pl.BlockSpec((None, bk, bn), lambda i, j, q: (j // per, q, j % per))
    elif tb:
        b_spec = pl.BlockSpec((bn, bk), lambda i, j, q: (j, q))
    else:
        b_spec = pl.BlockSpec((bk, bn), lambda i, j, q: (q, j))
    ex_specs = []
    for arr, kind in extras:
        if kind == 'mn':
            assert arr.shape == (m, n), (arr.shape, m, n)
            ex_specs.append(pl.BlockSpec((bm, bn), lambda i, j, q: (i, j)))
        else:
            assert arr.shape == (1, n), (arr.shape, n)
            ex_specs.append(pl.BlockSpec((1, bn), lambda i, j, q: (0, j)))
    outs = pl.pallas_call(
        body, name=name,
        out_shape=tuple(jax.ShapeDtypeStruct((m, n), d) for d in out_dtypes),
        grid=(m // bm, n // bn, nk),
        in_specs=[a_spec, b_spec] + ex_specs,
        out_specs=tuple(pl.BlockSpec((bm, bn), lambda i, j, q: (i, j)) for _ in out_dtypes),
        scratch_shapes=[pltpu.VMEM((bm, bn), F32)] if nk > 1 else [],
        compiler_params=_params("parallel", "parallel", "arbitrary"),
    )(a, b, *[arr for arr, _ in extras])
    return outs if n_out > 1 else outs[0]


def _epi_residual(acc, res, gate):
    return res + gate * acc, acc


def _wgrad(acts, cots, *, name, a_fn=None, bm=1024, bn=512):
    return _mm(acts, cots, ta=True, name=name, out_dtypes=(BF16,), a_fn=a_fn, bm=bm, bn=bn, bk=acts.shape[0])


def _square(a):
    af = a.astype(F32)
    return af * af


def _rstd(xv):
    return lax.rsqrt(jnp.mean(xv * xv, axis=-1, keepdims=True) + EPS)


def _normmod_fwd(x, w, sh, *, name, tm=512):
    L, D = x.shape

    def body(x_ref, w_ref, s_ref, h_ref):
        xv = x_ref[...]
        h_ref[...] = (xv * _rstd(xv) * w_ref[...] + s_ref[...]).astype(h_ref.dtype)

    row = pl.BlockSpec((tm, D), lambda i: (i, 0))
    vec = pl.BlockSpec((1, D), lambda i: (0, 0))
    return pl.pallas_call(body, name=name, out_shape=jax.ShapeDtypeStruct((L, D), BF16), grid=(L // tm,),
                          in_specs=[row, vec, vec], out_specs=row, compiler_params=_params("parallel"))(x, w, sh)


def _normmod_bwd(dh, x, w, dres, gate, *, name, tm=256):
    L, D = x.shape
    has_gate = gate is not None

    def body(*refs):
        if has_gate:
            dh_ref, x_ref, w_ref, r_ref, y_ref, g_ref, dx_ref, st_ref, dy_ref = refs
        else:
            dh_ref, x_ref, w_ref, r_ref, dx_ref, st_ref = refs
        i = pl.program_id(0)

        @pl.when(i == 0)
        def _():
            st_ref[...] = jnp.zeros_like(st_ref)

        xv = x_ref[...]
        dhv = dh_ref[...].astype(F32)
        rstd = _rstd(xv)
        xn = xv * rstd
        dxn = dhv * w_ref[...]
        dx = rstd * (dxn - xn * jnp.mean(dxn * xn, axis=-1, keepdims=True)) + r_ref[...]
        dx_ref[...] = dx
        st_ref[0:1, :] += jnp.sum(dhv * xn, axis=0, keepdims=True)
        st_ref[1:2, :] += jnp.sum(dhv, axis=0, keepdims=True)
        if has_gate:
            dy_ref[...] = (dx * g_ref[...]).astype(dy_ref.dtype)
            st_ref[2:3, :] += jnp.sum(dx * y_ref[...].astype(F32), axis=0, keepdims=True)

    row = pl.BlockSpec((tm, D), lambda i: (i, 0))
    vec = pl.BlockSpec((1, D), lambda i: (0, 0))
    st = pl.BlockSpec((SUBLANE, D), lambda i: (0, 0))
    in_specs = [row, row, vec, row] + ([row, vec] if has_gate else [])
    out_shape = [jax.ShapeDtypeStruct((L, D), F32), jax.ShapeDtypeStruct((SUBLANE, D), F32)]
    out_specs = [row, st]
    if has_gate:
        out_shape.append(jax.ShapeDtypeStruct((L, D), BF16))
        out_specs.append(row)
    args = (dh, x, w, dres) + (tuple(gate) if has_gate else ())
    return pl.pallas_call(body, name=name, out_shape=tuple(out_shape), grid=(L // tm,), in_specs=in_specs,
                          out_specs=tuple(out_specs), compiler_params=_params("arbitrary"))(*args)


def _loss_head(x, tgt, fg, y, g, *, name, tm=256):
    L, D = x.shape

    def body(x_ref, t_ref, fg_ref, y_ref, g_ref, dx_ref, st_ref, dy_ref, loss_ref):
        i = pl.program_id(0)

        @pl.when(i == 0)
        def _():
            st_ref[...] = jnp.zeros_like(st_ref)
            loss_ref[...] = jnp.zeros_like(loss_ref)

        xv = x_ref[...]
        rstd = _rstd(xv)
        xn = xv * rstd
        err = xn * fg_ref[...] - t_ref[...]
        loss_ref[...] += 0.5 * jnp.sum(jnp.mean(err * err, axis=-1, keepdims=True))
        dout = err * (1.0 / D)
        dxn = dout * fg_ref[...]
        dx = rstd * (dxn - xn * jnp.mean(dxn * xn, axis=-1, keepdims=True))
        dx_ref[...] = dx
        dy_ref[...] = (dx * g_ref[...]).astype(dy_ref.dtype)
        st_ref[0:1, :] += jnp.sum(dout * xn, axis=0, keepdims=True)
        st_ref[2:3, :] += jnp.sum(dx * y_ref[...].astype(F32), axis=0, keepdims=True)

    row = pl.BlockSpec((tm, D), lambda i: (i, 0))
    vec = pl.BlockSpec((1, D), lambda i: (0, 0))
    return pl.pallas_call(
        body, name=name,
        out_shape=(jax.ShapeDtypeStruct((L, D), F32), jax.ShapeDtypeStruct((SUBLANE, D), F32),
                   jax.ShapeDtypeStruct((L, D), BF16), jax.ShapeDtypeStruct((SUBLANE, LANE), F32)),
        grid=(L // tm,), in_specs=[row, row, vec, row, vec],
        out_specs=(row, pl.BlockSpec((SUBLANE, D), lambda i: (0, 0)), row,
                   pl.BlockSpec((SUBLANE, LANE), lambda i: (0, 0))),
        compiler_params=_params("arbitrary"))(x, tgt, fg, y, g)


def _shift_down(v, k):
    row = lax.broadcasted_iota(jnp.int32, v.shape, 0)
    return jnp.where(row >= k, pltpu.roll(v, k, 0), 0.0)


def _shift_up(v, k):
    n = v.shape[0]
    row = lax.broadcasted_iota(jnp.int32, v.shape, 0)
    return jnp.where(row < n - k, pltpu.roll(v, n - k, 0), 0.0)


def _conv_views(L, D):
    return [pl.BlockSpec((L, LANE), lambda j, s=s: (0, s * (D // LANE) + j)) for s in range(3)]


def _conv_fwd(bcx, wb, *, name):
    L, D = bcx.shape[0], bcx.shape[1] // 3

    def body(b_ref, c_ref, x_ref, wb_ref, p_ref):
        z = c_ref[...].astype(F32) * x_ref[...].astype(F32)
        conv = (wb_ref[0:1, :] * _shift_down(z, 2) + wb_ref[1:2, :] * _shift_down(z, 1)
                + wb_ref[2:3, :] * z + wb_ref[3:4, :])
        p_ref[...] = (b_ref[...].astype(F32) * conv).astype(p_ref.dtype)

    col = pl.BlockSpec((L, LANE), lambda j: (0, j))
    return pl.pallas_call(body, name=name, out_shape=jax.ShapeDtypeStruct((L, D), BF16), grid=(D // LANE,),
                          in_specs=_conv_views(L, D) + [pl.BlockSpec((SUBLANE, LANE), lambda j: (0, j))],
                          out_specs=col, compiler_params=_params("parallel"))(bcx, bcx, bcx, wb)


def _conv_bwd(dp, bcx, wb, *, name):
    L, D = dp.shape

    def body(dp_ref, b_ref, c_ref, x_ref, wb_ref, d3_ref, st_ref):
        cv, xv = c_ref[...].astype(F32), x_ref[...].astype(F32)
        z = cv * xv
        z1, z2 = _shift_down(z, 1), _shift_down(z, 2)
        w0, w1, w2 = wb_ref[0:1, :], wb_ref[1:2, :], wb_ref[2:3, :]
        conv = w0 * z2 + w1 * z1 + w2 * z + wb_ref[3:4, :]
        dpv = dp_ref[...].astype(F32)
        d3_ref[0] = (dpv * conv).astype(d3_ref.dtype)
        dconv = dpv * b_ref[...].astype(F32)
        dz = w2 * dconv + w1 * _shift_up(dconv, 1) + w0 * _shift_up(dconv, 2)
        d3_ref[1] = (dz * xv).astype(d3_ref.dtype)
        d3_ref[2] = (dz * cv).astype(d3_ref.dtype)
        st_ref[...] = jnp.zeros_like(st_ref)
        st_ref[0:1, :] = jnp.sum(dconv * z2, axis=0, keepdims=True)
        st_ref[1:2, :] = jnp.sum(dconv * z1, axis=0, keepdims=True)
        st_ref[2:3, :] = jnp.sum(dconv * z, axis=0, keepdims=True)
        st_ref[3:4, :] = jnp.sum(dconv, axis=0, keepdims=True)

    col = pl.BlockSpec((L, LANE), lambda j: (0, j))
    vec = pl.BlockSpec((SUBLANE, LANE), lambda j: (0, j))
    return pl.pallas_call(body, name=name,
                          out_shape=(jax.ShapeDtypeStruct((3, L, D), BF16), jax.ShapeDtypeStruct((SUBLANE, D), F32)),
                          grid=(D // LANE,), in_specs=[col] + _conv_views(L, D) + [vec],
                          out_specs=(pl.BlockSpec((3, L, LANE), lambda j: (0, 0, j)), vec),
                          compiler_params=_params("parallel"))(dp, bcx, bcx, bcx, wb)


def _sg_fwd(uv, vg, ws, bsb, *, name, tr=512):
    L, D = uv.shape[0], uv.shape[1] // 2

    def body(uv_ref, vg_ref, ws_ref, bsb_ref, p_ref):
        for ci in range(tr // SG_CHUNK):
            rows = slice(ci * SG_CHUNK, (ci + 1) * SG_CHUNK)
            v = uv_ref[rows, D:2 * D]
            vn = (v * _rstd(v) * vg_ref[...]).astype(BF16)
            for h in range(SG_HEADS):
                cols = slice(h * LANE, (h + 1) * LANE)
                vm = jnp.dot(ws_ref[h], vn[:, cols], preferred_element_type=F32) + bsb_ref[h]
                p_ref[rows, cols] = (uv_ref[rows, cols] * vm).astype(p_ref.dtype)

    full3 = pl.BlockSpec((SG_HEADS, SG_CHUNK, LANE), lambda i: (0, 0, 0))
    return pl.pallas_call(body, name=name, out_shape=jax.ShapeDtypeStruct((L, D), BF16), grid=(L // tr,),
                          in_specs=[pl.BlockSpec((tr, 2 * D), lambda i: (i, 0)), pl.BlockSpec((1, D), lambda i: (0, 0)),
                                    full3, full3],
                          out_specs=pl.BlockSpec((tr, D), lambda i: (i, 0)),
                          compiler_params=_params("parallel"))(uv, vg, ws, bsb)


def _sg_bwd(dp, uv, vg, ws, wst, bsb, *, name, tr=512):
    L, D = dp.shape

    def body(dp_ref, uv_ref, vg_ref, ws_ref, wst_ref, bsb_ref, duv_ref, dws_ref, dbs_ref, st_ref, dvn_ref):
        i = pl.program_id(0)

        @pl.when(i == 0)
        def _():
            dws_ref[...] = jnp.zeros_like(dws_ref)
            dbs_ref[...] = jnp.zeros_like(dbs_ref)
            st_ref[...] = jnp.zeros_like(st_ref)

        for ci in range(tr // SG_CHUNK):
            rows = slice(ci * SG_CHUNK, (ci + 1) * SG_CHUNK)
            v = uv_ref[rows, D:2 * D]
            rstd = _rstd(v)
            vhat = v * rstd
            vn = (vhat * vg_ref[...]).astype(BF16)
            for h in range(SG_HEADS):
                cols = slice(h * LANE, (h + 1) * LANE)
                vm = jnp.dot(ws_ref[h], vn[:, cols], preferred_element_type=F32) + bsb_ref[h]
                dph = dp_ref[rows, cols]
                duv_ref[rows, cols] = (dph * vm).astype(duv_ref.dtype)
                dvm = dph * uv_ref[rows, cols]
                dbs_ref[h] += dvm
                dvmb = dvm.astype(BF16)
                dws_ref[h] += lax.dot_general(dvmb, vn[:, cols], (((1,), (1,)), ((), ())),
                                              preferred_element_type=F32)
                dvn_ref[rows, cols] = jnp.dot(wst_ref[h], dvmb, preferred_element_type=F32)
            dvn = dvn_ref[rows, :]
            gv = dvn * vg_ref[...]
            dv = rstd * (gv - vhat * jnp.mean(gv * vhat, axis=-1, keepdims=True))
            duv_ref[rows, D:2 * D] = dv.astype(duv_ref.dtype)
            st_ref[0:1, :] += jnp.sum(dvn * vhat, axis=0, keepdims=True)

    full3 = pl.BlockSpec((SG_HEADS, SG_CHUNK, LANE), lambda i: (0, 0, 0))
    acc3 = jax.ShapeDtypeStruct((SG_HEADS, SG_CHUNK, LANE), F32)
    return pl.pallas_call(
        body, name=name,
        out_shape=(jax.ShapeDtypeStruct((L, 2 * D), BF16), acc3, acc3, jax.ShapeDtypeStruct((SUBLANE, D), F32)),
        grid=(L // tr,),
        in_specs=[pl.BlockSpec((tr, D), lambda i: (i, 0)), pl.BlockSpec((tr, 2 * D), lambda i: (i, 0)),
                  pl.BlockSpec((1, D), lambda i: (0, 0)), full3, full3, full3],
        out_specs=(pl.BlockSpec((tr, 2 * D), lambda i: (i, 0)), full3, full3,
                   pl.BlockSpec((SUBLANE, D), lambda i: (0, 0))),
        scratch_shapes=[pltpu.VMEM((tr, D), F32)],
        compiler_params=_params("arbitrary"))(dp, uv, vg, ws, wst, bsb)


def _gelu(x):
    return 0.5 * x * (1.0 + jnp.tanh(GELU_C * (x + GELU_A * x * x * x)))


def _gelu_grad(x):
    th = jnp.tanh(GELU_C * (x + GELU_A * x * x * x))
    return 0.5 * (1.0 + th) + 0.5 * x * (1.0 - th * th) * GELU_C * (1.0 + 3.0 * GELU_A * x * x)


def _cmul_add(xr, xi, ar, ai, br, bi):
    return xr + ar * br - ai * bi, xi + ar * bi + ai * br


def _slab_rows(ref, rows):
    return jnp.concatenate([ref[c, rows, :] for c in range(ref.shape[0])], axis=1)


def _slab_set_rows(ref, rows, v):
    for c in range(ref.shape[0]):
        ref[c, rows, :] = v[:, c * LANE:(c + 1) * LANE]


def _slab_all(ref):
    return jnp.concatenate([ref[c] for c in range(ref.shape[0])], axis=1)


def _slab_set(ref, v):
    for c in range(ref.shape[0]):
        ref[c] = v[:, c * LANE:(c + 1) * LANE]


def _s5_fwd(u, bre, bim, cre, cim, pw, pos, dsk, *, name, tc=S5_CHUNK):
    L, D = u.shape
    W = S5_LANES // S5_BLOCKS
    nt = L // tc
    n = tc // SUBLANE

    def sub(k):
        return pl.ds(k, SUBLANE, stride=n)

    def body(u_ref, bre_ref, bim_ref, cre_ref, cim_ref, pw_ref, pos_ref, d_ref, sre_ref, sim_ref, ypre_ref, yg_ref,
             carry):
        t = pl.program_id(1)

        @pl.when(t == 0)
        def _():
            carry[...] = jnp.zeros_like(carry)

        uv = u_ref[...]
        ub = uv.astype(BF16)
        _slab_set(sre_ref, jnp.dot(ub, bre_ref[...], preferred_element_type=F32))
        _slab_set(sim_ref, jnp.dot(ub, bim_ref[...], preferred_element_type=F32))

        ar, ai = pw_ref[8], pw_ref[9]
        xr = jnp.zeros((SUBLANE, W), F32)
        xi = jnp.zeros((SUBLANE, W), F32)
        for k in range(n):
            xr, xi = _cmul_add(_slab_rows(sre_ref, sub(k)), _slab_rows(sim_ref, sub(k)), ar, ai, xr, xi)
            _slab_set_rows(sre_ref, sub(k), xr)
            _slab_set_rows(sim_ref, sub(k), xi)
        for q, d in enumerate((1, 2, 4)):
            xr, xi = _cmul_add(xr, xi, pw_ref[2 * q], pw_ref[2 * q + 1], pltpu.roll(xr, d, 0), pltpu.roll(xi, d, 0))
        cr, ci = carry[0], carry[1]
        xr, xi = _cmul_add(xr, xi, pw_ref[6], pw_ref[7], cr, ci)
        first = lax.broadcasted_iota(jnp.int32, (SUBLANE, W), 0) == 0
        er = jnp.where(first, cr, pltpu.roll(xr, 1, 0))
        ei = jnp.where(first, ci, pltpu.roll(xi, 1, 0))
        last = slice(SUBLANE - 1, SUBLANE)
        carry[0] = jnp.broadcast_to(xr[last, :], (SUBLANE, W))
        carry[1] = jnp.broadcast_to(xi[last, :], (SUBLANE, W))
        for k in range(n):
            sr, si = _cmul_add(_slab_rows(sre_ref, sub(k)), _slab_rows(sim_ref, sub(k)), pos_ref[0, k:k + 1, :],
                               pos_ref[1, k:k + 1, :], er, ei)
            _slab_set_rows(sre_ref, sub(k), sr)
            _slab_set_rows(sim_ref, sub(k), si)
        y = (jnp.dot(_slab_all(sre_ref).astype(BF16), cre_ref[...], preferred_element_type=F32)
             - jnp.dot(_slab_all(sim_ref).astype(BF16), cim_ref[...], preferred_element_type=F32) + d_ref[...] * uv)
        ypre_ref[...] = y
        yg_ref[...] = _gelu(y)

    ch = pl.BlockSpec((tc, LANE), lambda j, t: (t, j))
    st = pl.BlockSpec((W // LANE, tc, LANE), lambda j, t: (j, t, 0))
    bsp = pl.BlockSpec((None, LANE, W), lambda j, t: (j, 0, 0))
    csp = pl.BlockSpec((None, W, LANE), lambda j, t: (j, 0, 0))
    return pl.pallas_call(
        body, name=name,
        out_shape=(jax.ShapeDtypeStruct((S5_LANES // LANE, L, LANE), F32),
                   jax.ShapeDtypeStruct((S5_LANES // LANE, L, LANE), F32),
                   jax.ShapeDtypeStruct((L, D), F32), jax.ShapeDtypeStruct((L, D), F32)),
        grid=(S5_BLOCKS, nt),
        in_specs=[ch, bsp, bsp, csp, csp, pl.BlockSpec((10, SUBLANE, W), lambda j, t: (0, 0, j)),
                  pl.BlockSpec((2, n, W), lambda j, t: (0, 0, j)), pl.BlockSpec((1, LANE), lambda j, t: (0, j))],
        out_specs=(st, st, ch, ch),
        scratch_shapes=[pltpu.VMEM((2, SUBLANE, W), F32)],
        compiler_params=_params("parallel", "arbitrary"))(u, bre, bim, cre, cim, pw, pos, dsk)


def _s5_bwd(dy, u, sre, sim, bre, bim, cre, cim, pwr, posr, dsk, *, name, tc=S5_CHUNK):
    L, D = u.shape
    W = S5_LANES // S5_BLOCKS
    nt = L // tc
    n = tc // SUBLANE
    nt_dims = (((1,), (1,)), ((), ()))
    tn_dims = (((0,), (0,)), ((), ()))

    def sub(k):
        return pl.ds(k, SUBLANE, stride=n)

    def body(dy_ref, u_ref, sre_ref, sim_ref, bre_ref, bim_ref, cre_ref, cim_ref, pw_ref, pos_ref, d_ref,
             du_ref, dbre_ref, dbim_ref, dcre_ref, dcim_ref, ga_ref, dd_ref, gre, gim, carry, gacc):
        t = pl.program_id(1)

        @pl.when(t == 0)
        def _():
            for r in (carry, gacc, dbre_ref, dbim_ref, dcre_ref, dcim_ref, ga_ref, dd_ref):
                r[...] = jnp.zeros_like(r)

        dyv, uv = dy_ref[...], u_ref[...]
        dyb, ub = dyv.astype(BF16), uv.astype(BF16)
        _slab_set(gre, lax.dot_general(dyb, cre_ref[...], nt_dims, preferred_element_type=F32))
        _slab_set(gim, -lax.dot_general(dyb, cim_ref[...], nt_dims, preferred_element_type=F32))
        br, bi = pw_ref[8], pw_ref[9]
        xr = jnp.zeros((SUBLANE, W), F32)
        xi = jnp.zeros((SUBLANE, W), F32)
        for k in reversed(range(n)):
            xr, xi = _cmul_add(_slab_rows(gre, sub(k)), _slab_rows(gim, sub(k)), br, bi, xr, xi)
            _slab_set_rows(gre, sub(k), xr)
            _slab_set_rows(gim, sub(k), xi)
        for q, d in enumerate((1, 2, 4)):
            xr, xi = _cmul_add(xr, xi, pw_ref[2 * q], pw_ref[2 * q + 1], pltpu.roll(xr, SUBLANE - d, 0),
                               pltpu.roll(xi, SUBLANE - d, 0))
        cr, ci = carry[0], carry[1]
        xr, xi = _cmul_add(xr, xi, pw_ref[6], pw_ref[7], cr, ci)
        top = lax.broadcasted_iota(jnp.int32, (SUBLANE, W), 0) == SUBLANE - 1
        er = jnp.where(top, cr, pltpu.roll(xr, SUBLANE - 1, 0))
        ei = jnp.where(top, ci, pltpu.roll(xi, SUBLANE - 1, 0))
        carry[0] = jnp.broadcast_to(xr[0:1, :], (SUBLANE, W))
        carry[1] = jnp.broadcast_to(xi[0:1, :], (SUBLANE, W))
        nr, ni = er, ei
        acc_r = jnp.zeros((SUBLANE, W), F32)
        acc_i = jnp.zeros((SUBLANE, W), F32)
        for k in reversed(range(n)):
            gr, gi = _cmul_add(_slab_rows(gre, sub(k)), _slab_rows(gim, sub(k)), pos_ref[0, k:k + 1, :],
                                pos_ref[1, k:k + 1, :], er, ei)
            _slab_set_rows(gre, sub(k), gr)
            _slab_set_rows(gim, sub(k), gi)
            sr, si = _slab_rows(sre_ref, sub(k)), _slab_rows(sim_ref, sub(k))
            acc_r = acc_r + sr * nr + si * ni
            acc_i = acc_i + sr * ni - si * nr
            nr, ni = gr, gi
        gacc[0] += acc_r
        gacc[1] += acc_i
        grb, gib = _slab_all(gre).astype(BF16), _slab_all(gim).astype(BF16)
        du = (lax.dot_general(grb, bre_ref[...], nt_dims, preferred_element_type=F32)
              + lax.dot_general(gib, bim_ref[...], nt_dims, preferred_element_type=F32) + d_ref[...] * dyv)
        du_ref[...] = du.astype(du_ref.dtype)
        dbre_ref[...] += lax.dot_general(ub, grb, tn_dims, preferred_element_type=F32)
        dbim_ref[...] += lax.dot_general(ub, gib, tn_dims, preferred_element_type=F32)
        dcre_ref[...] += lax.dot_general(_slab_all(sre_ref).astype(BF16), dyb, tn_dims, preferred_element_type=F32)
        dcim_ref[...] -= lax.dot_general(_slab_all(sim_ref).astype(BF16), dyb, tn_dims, preferred_element_type=F32)
        dd_ref[0:1, :] += jnp.sum(dyv * uv, axis=0, keepdims=True)

        @pl.when(t == nt - 1)
        def _():
            ga_ref[0:1, :] = jnp.sum(gacc[0], axis=0, keepdims=True)
            ga_ref[1:2, :] = jnp.sum(gacc[1], axis=0, keepdims=True)

    ch = pl.BlockSpec((tc, LANE), lambda j, t: (nt - 1 - t, j))
    st = pl.BlockSpec((W // LANE, tc, LANE), lambda j, t: (j, nt - 1 - t, 0))
    bsp = pl.BlockSpec((None, LANE, W), lambda j, t: (j, 0, 0))
    csp = pl.BlockSpec((None, W, LANE), lambda j, t: (j, 0, 0))
    return pl.pallas_call(
        body, name=name,
        out_shape=(jax.ShapeDtypeStruct((L, D), BF16),
                   jax.ShapeDtypeStruct((S5_BLOCKS, LANE, W), F32), jax.ShapeDtypeStruct((S5_BLOCKS, LANE, W), F32),
                   jax.ShapeDtypeStruct((S5_BLOCKS, W, LANE), F32), jax.ShapeDtypeStruct((S5_BLOCKS, W, LANE), F32),
                   jax.ShapeDtypeStruct((SUBLANE, S5_LANES), F32), jax.ShapeDtypeStruct((SUBLANE, D), F32)),
        grid=(S5_BLOCKS, nt),
        in_specs=[ch, ch, st, st, bsp, bsp, csp, csp, pl.BlockSpec((10, SUBLANE, W), lambda j, t: (0, 0, j)),
                  pl.BlockSpec((2, n, W), lambda j, t: (0, 0, j)), pl.BlockSpec((1, LANE), lambda j, t: (0, j))],
        out_specs=(ch, bsp, bsp, csp, csp, pl.BlockSpec((SUBLANE, W), lambda j, t: (0, j)),
                   pl.BlockSpec((SUBLANE, LANE), lambda j, t: (0, j))),
        scratch_shapes=[pltpu.VMEM((W // LANE, tc, LANE), F32), pltpu.VMEM((W // LANE, tc, LANE), F32),
                        pltpu.VMEM((2, SUBLANE, W), F32),
                        pltpu.VMEM((2, SUBLANE, W), F32)],
        compiler_params=_params("parallel", "arbitrary"))(dy, u, sre, sim, bre, bim, cre, cim, pwr, posr, dsk)


def _glu_bwd(dy2, y, t, *, name, tm=256):
    L, D = y.shape

    def body(dy2_ref, y_ref, t_ref, dt_ref, dya_ref, st_ref):
        i = pl.program_id(0)

        @pl.when(i == 0)
        def _():
            st_ref[...] = jnp.zeros_like(st_ref)

        sig = 1.0 / (1.0 + jnp.exp(-t_ref[...]))
        dy2v = dy2_ref[...]
        dt = dy2v * y_ref[...] * sig * (1.0 - sig)
        dt_ref[...] = dt.astype(dt_ref.dtype)
        dya_ref[...] = dy2v * sig
        st_ref[0:1, :] += jnp.sum(dt, axis=0, keepdims=True)

    row = pl.BlockSpec((tm, D), lambda i: (i, 0))
    return pl.pallas_call(
        body, name=name,
        out_shape=(jax.ShapeDtypeStruct((L, D), BF16), jax.ShapeDtypeStruct((L, D), F32),
                   jax.ShapeDtypeStruct((SUBLANE, D), F32)),
        grid=(L // tm,), in_specs=[row, row, row],
        out_specs=(row, row, pl.BlockSpec((SUBLANE, D), lambda i: (0, 0))),
        compiler_params=_params("arbitrary"))(dy2, y, t)


def _s5_prep(a_re, a_im, log_dt, b_re, b_im, c_re, c_im):
    dt = jnp.exp(log_dt)[:, None]
    mag = jnp.exp(a_re * dt)
    abar_re = mag * jnp.cos(a_im * dt)
    abar_im = mag * jnp.sin(a_im * dt)
    den = a_re * a_re + a_im * a_im
    nr = abar_re - 1.0
    ni = abar_im
    f_re = ((nr * a_re + ni * a_im) / den)[..., None]
    f_im = ((ni * a_re - nr * a_im) / den)[..., None]
    bbar_re = f_re * b_re - f_im * b_im
    bbar_im = f_re * b_im + f_im * b_re
    eye = jnp.eye(S5_GROUPS // S5_BLOCKS, dtype=F32)
    gb = S5_GROUPS // S5_BLOCKS

    def blk_b(bb):
        t = bb.reshape(S5_BLOCKS, gb, S5_STATE, S5_GROUP)
        return jnp.einsum('jgph,gk->jghkp', t, eye).reshape(S5_BLOCKS, gb * S5_GROUP, gb * S5_STATE)

    def blk_c(cc):
        t = cc.reshape(S5_BLOCKS, gb, S5_GROUP, S5_STATE)
        return jnp.einsum('jghp,gk->jgpkh', t, eye).reshape(S5_BLOCKS, gb * S5_STATE, gb * S5_GROUP)

    return (abar_re.reshape(1, S5_LANES), abar_im.reshape(1, S5_LANES), blk_b(bbar_re), blk_b(bbar_im),
            blk_c(c_re), blk_c(c_im))


def _cpowers(ar, ai, count):
    pr, pi = [jnp.ones_like(ar)], [jnp.zeros_like(ai)]
    for _ in range(count):
        pr, pi = pr + [pr[-1] * ar - pi[-1] * ai], pi + [pr[-1] * ai + pi[-1] * ar]
    return pr, pi


def _s5_power_tables(ar, ai, n):
    pr, pi = _cpowers(ar, ai, n)
    qr, qi = _cpowers(pr[n], pi[n], SUBLANE)
    row = jnp.arange(SUBLANE)[:, None]
    lanes = ar.shape[1]

    def tables(sign, keep, carry_pow, places):
        out = []
        for d in (1, 2, 4):
            out += [jnp.where(keep(d), qr[d], 0.0), jnp.where(keep(d), sign * qi[d], 0.0)]
        out += [jnp.concatenate([qr[p] for p in carry_pow], 0), sign * jnp.concatenate([qi[p] for p in carry_pow], 0)]
        out += [ar, sign * ai]
        pw = jnp.stack([jnp.broadcast_to(o, (SUBLANE, lanes)) for o in out])
        pos = jnp.stack([jnp.concatenate([pr[p] for p in places], 0), sign * jnp.concatenate([pi[p] for p in places], 0)])
        return pw, pos

    fwd = tables(1.0, lambda d: row >= d, [r + 1 for r in range(SUBLANE)], [k + 1 for k in range(n)])
    rev = tables(-1.0, lambda d: row + d <= SUBLANE - 1, [SUBLANE - r for r in range(SUBLANE)], [n - k for k in range(n)])
    return fwd, rev


ADAMW_PART_BLOCK_BYTES = 2 * 1024 * 1024


def _adamw(w, parts, m, v, *, name):
    n, R, C = w.shape
    assert len(parts) == n
    P = parts[0].shape[0]
    tr = R
    while P * tr * C * parts[0].dtype.itemsize > ADAMW_PART_BLOCK_BYTES and tr % 16 == 0:
        tr //= 2
    c1 = 1.0 / (1.0 - ADAM_B1 ** ADAM_STEP)
    c2 = 1.0 / (1.0 - ADAM_B2 ** ADAM_STEP)

    def body(*refs):
        w_ref, m_ref, v_ref = refs[:3]
        p_refs = refs[3:3 + n]
        g_ref, d_ref, nm_ref, nv_ref = refs[3 + n:]
        layer = pl.program_id(0)
        for q, p_ref in enumerate(p_refs):
            @pl.when(layer == q)
            def _(p_ref=p_ref):
                g = p_ref[0].astype(F32)
                for s in range(1, P):
                    g = g + p_ref[s].astype(F32)
                nm = ADAM_B1 * m_ref[...] + (1.0 - ADAM_B1) * g
                nv = ADAM_B2 * v_ref[...] + (1.0 - ADAM_B2) * (g * g)
                g_ref[...] = g
                nm_ref[...] = nm
                nv_ref[...] = nv
                d_ref[...] = -ADAM_LR * ((nm * c1) / (jnp.sqrt(nv * c2) + ADAM_EPS) + ADAM_WD * w_ref[...])

    row = pl.BlockSpec((None, tr, C), lambda l, i: (l, i, 0))
    part_specs = [pl.BlockSpec((P, tr, C), lambda l, i, q=q: (0, jnp.where(l == q, i, 0), 0)) for q in range(n)]
    out = jax.ShapeDtypeStruct((n, R, C), F32)
    return pl.pallas_call(body, name=name, out_shape=(out, out, out, out), grid=(n, R // tr),
                          in_specs=[row, row, row] + part_specs, out_specs=(row, row, row, row),
                          compiler_params=_params("arbitrary", "arbitrary"))(w, m, v, *parts)


def _all_gather(xs, axis, *, name):
    m = xs.shape[axis]
    out_shape = list(xs.shape)
    out_shape[axis] = N_DEV * m

    def body(x_ref, out_ref, send_sems, recv_sems, local_sem):
        x, y, c = _my_pos()
        me, sibling = (x, y, c), (x, y, 1 - c)
        chips = [(1 - x, y), (x, 1 - y), (1 - x, 1 - y)]

        def blk(px, py, pc):
            idx = [slice(None)] * 3
            idx[axis] = pl.ds((4 * px + 2 * py + pc) * m, m)
            return out_ref.at[tuple(idx)]

        def copy(k, block, to, src=None):
            return pltpu.make_async_remote_copy(src_ref=blk(*block) if src is None else src, dst_ref=blk(*block),
                                                send_sem=send_sems.at[k], recv_sem=recv_sems.at[k],
                                                device_id=to, device_id_type=MESH_ID)

        mine = pltpu.make_async_copy(x_ref, blk(*me), local_sem)
        mine.start()
        first = [copy(0, me, sibling, src=x_ref)]
        first += [copy(1 + j, me, (*chip, c), src=x_ref) for j, chip in enumerate(chips)]
        for cp in first:
            cp.start()
        passed = [copy(4 + j, (*chip, c), sibling) for j, chip in enumerate(chips)]
        for j, chip in enumerate(chips):
            copy(1 + j, (*chip, c), me).wait_recv()
            passed[j].start()
        copy(0, sibling, me).wait_recv()
        for j, chip in enumerate(chips):
            copy(4 + j, (*chip, 1 - c), me).wait_recv()
        for cp in first + passed:
            cp.wait_send()
        mine.wait()

    hbm = pl.BlockSpec(memory_space=pl.ANY)
    return pl.pallas_call(body, name=name, out_shape=jax.ShapeDtypeStruct(tuple(out_shape), xs.dtype),
                          in_specs=[hbm], out_specs=hbm,
                          scratch_shapes=[pltpu.SemaphoreType.DMA((N_DEV - 1,)), pltpu.SemaphoreType.DMA((N_DEV - 1,)),
                                          pltpu.SemaphoreType.DMA],
                          compiler_params=pltpu.CompilerParams(has_side_effects=True))(xs)


def _block(ref, axis, idx, m):
    return ref.at[pl.ds(idx * m, m), :] if axis == 0 else ref.at[:, pl.ds(idx * m, m)]


def _exchange_copies(metas, src_refs, zone_refs, send_sems, recv_sems, group):
    x, y, c = _my_pos()
    me = 4 * x + 2 * y + c
    base = group * N_DEV
    pairs = []
    for r in range(1, N_DEV):
        pos = (1 - x if r & 4 else x, 1 - y if r & 2 else y, 1 - c if r & 1 else c)
        peer = 4 * pos[0] + 2 * pos[1] + pos[2]
        for (kind, axis, m), s_ref, z_ref in zip(metas, src_refs, zone_refs):
            if kind == 'gather':
                src, dst, arrival = s_ref, _block(z_ref, axis, me, m), _block(z_ref, axis, peer, m)
            else:
                src, dst, arrival = _block(s_ref, axis, peer, m), z_ref.at[me], z_ref.at[peer]
            pairs.append(tuple(
                pltpu.make_async_remote_copy(src_ref=src, dst_ref=d, send_sem=send_sems.at[base + r - 1],
                                             recv_sem=recv_sems.at[base + r - 1], device_id=pos,
                                             device_id_type=MESH_ID)
                for d in (dst, arrival)))
    own = []
    for (kind, axis, m), s_ref, z_ref in zip(metas, src_refs, zone_refs):
        src, dst = (s_ref, _block(z_ref, axis, me, m)) if kind == 'gather' else (_block(s_ref, axis, me, m), z_ref.at[me])
        own.append(pltpu.make_async_copy(src, dst, recv_sems.at[base + N_DEV - 1]))
    return pairs, own


def _exchange_start(groups, after, *, name):
    flat = [it for g in groups for it in g]
    n, ng = len(flat), len(groups)
    metas = [it[2] for it in flat]
    bounds = [(sum(len(g) for g in groups[:q]), sum(len(g) for g in groups[:q + 1])) for q in range(ng)]

    def body(*refs):
        src_refs = refs[:n]
        send_sems, recv_sems = refs[n + 1], refs[n + 2]
        zone_refs = refs[2 * n + 3:3 * n + 3]
        token = refs[-1]
        for q, (lo, hi) in enumerate(bounds):
            pairs, own = _exchange_copies(metas[lo:hi], src_refs[lo:hi], zone_refs[lo:hi], send_sems, recv_sems, q)
            for outgoing, _ in pairs:
                outgoing.start()
            for cp in own:
                cp.start()
        token[...] = jnp.zeros_like(token)

    hbm = pl.BlockSpec(memory_space=pltpu.HBM)
    sem = pl.BlockSpec(memory_space=pltpu.SEMAPHORE)
    srcs = [it[0] for it in flat]
    res = pl.pallas_call(
        body, name=name,
        out_shape=(pltpu.SemaphoreType.DMA((ng * N_DEV,)), pltpu.SemaphoreType.DMA((ng * N_DEV,)),
                   *[pltpu.HBM(a.shape, a.dtype) for a in srcs], *[pltpu.HBM(it[1], it[0].dtype) for it in flat],
                   jax.ShapeDtypeStruct((SUBLANE, LANE), F32)),
        in_specs=[hbm] * n + [pl.BlockSpec(memory_space=pl.ANY)],
        out_specs=(sem, sem, *[hbm] * (2 * n), pl.BlockSpec(memory_space=pltpu.VMEM)),
        input_output_aliases={q: 2 + q for q in range(n)},
        compiler_params=pltpu.CompilerParams(has_side_effects=pltpu.SideEffectType.DATAFLOW_SIDE_EFFECTING),
    )(*[pltpu.with_memory_space_constraint(a, pltpu.HBM) for a in srcs], after)
    handles = [(res[0], res[1], q, list(res[2 + lo:2 + hi]), list(res[2 + n + lo:2 + n + hi]), metas[lo:hi])
               for q, (lo, hi) in enumerate(bounds)]
    return handles, res[-1]


def _exchange_wait(handle, after, *, name):
    send_sems, recv_sems, group, srcs, zones, metas = handle
    n = len(srcs)

    def body(*refs):
        src_refs, zone_refs = refs[:n], refs[n:2 * n]
        s_sems, r_sems = refs[2 * n], refs[2 * n + 1]
        pairs, own = _exchange_copies(metas, src_refs, zone_refs, s_sems, r_sems, group)
        for outgoing, incoming in pairs:
            outgoing.wait_send()
            incoming.wait_recv()
        for cp in own:
            cp.wait()

    hbm = pl.BlockSpec(memory_space=pltpu.HBM)
    sem = pl.BlockSpec(memory_space=pltpu.SEMAPHORE)
    arrays = srcs + zones
    res = pl.pallas_call(
        body, name=name,
        out_shape=tuple(pltpu.HBM(a.shape, a.dtype) for a in arrays),
        in_specs=[hbm] * (2 * n) + [sem, sem, pl.BlockSpec(memory_space=pl.ANY)],
        out_specs=tuple([hbm] * (2 * n)),
        input_output_aliases={q: q for q in range(2 * n)},
        compiler_params=pltpu.CompilerParams(has_side_effects=pltpu.SideEffectType.DATAFLOW_SIDE_EFFECTING),
    )(*arrays, send_sems, recv_sems, after)
    return list(res[n:])


def _pad_rows(a, rows):
    return jnp.pad(a, ((0, rows - a.shape[0]), (0, 0)))


def _rows(a):
    flat = a.reshape(-1).astype(F32)
    pad = -flat.shape[0] % (SUBLANE * LANE)
    return (jnp.pad(flat, (0, pad)) if pad else flat).reshape(-1, LANE)


def _pack_rows(arrays):
    return jnp.concatenate([_rows(a) for a in arrays], 0)


def _unpack_rows(t, shapes):
    out, off = [], 0
    for shp in shapes:
        size = math.prod(shp)
        rows = -(-size // (SUBLANE * LANE)) * SUBLANE
        out.append(t[off:off + rows].reshape(-1)[:size].reshape(shp))
        off += rows
    return out


def _stat_row(st, r):
    return st[r:r + 1, :]


def kernel(x, c, ada_w, ada_b, norm1_g, norm2_g, ff_w1, ff_w2, final_g, conv_w_in, conv_w, conv_b, conv_w_out, ssm_w_in, ssm_a_re, ssm_a_im, ssm_log_dt, ssm_b_re, ssm_b_im, ssm_c_re, ssm_c_im, ssm_d, ssm_glu_w, ssm_glu_b, ssm_w_out, sg_w_in, sg_v_g, sg_w_s, sg_b_s, sg_w_out, loss_target, m_ada_w, m_ada_b, m_norm1_g, m_norm2_g, m_ff_w1, m_ff_w2, m_final_g, m_conv_w_in, m_conv_w, m_conv_b, m_conv_w_out, m_ssm_w_in, m_ssm_a_re, m_ssm_a_im, m_ssm_log_dt, m_ssm_b_re, m_ssm_b_im, m_ssm_c_re, m_ssm_c_im, m_ssm_d, m_ssm_glu_w, m_ssm_glu_b, m_ssm_w_out, m_sg_w_in, m_sg_v_g, m_sg_w_s, m_sg_b_s, m_sg_w_out, v_ada_w, v_ada_b, v_norm1_g, v_norm2_g, v_ff_w1, v_ff_w2, v_final_g, v_conv_w_in, v_conv_w, v_conv_b, v_conv_w_out, v_ssm_w_in, v_ssm_a_re, v_ssm_a_im, v_ssm_log_dt, v_ssm_b_re, v_ssm_b_im, v_ssm_c_re, v_ssm_c_im, v_ssm_d, v_ssm_glu_w, v_ssm_glu_b, v_ssm_w_out, v_sg_w_in, v_sg_v_g, v_sg_w_s, v_sg_b_s, v_sg_w_out):
    P = dict(zip(INPUTS, (x, c, ada_w, ada_b, norm1_g, norm2_g, ff_w1, ff_w2, final_g, conv_w_in, conv_w, conv_b, conv_w_out, ssm_w_in, ssm_a_re, ssm_a_im, ssm_log_dt, ssm_b_re, ssm_b_im, ssm_c_re, ssm_c_im, ssm_d, ssm_glu_w, ssm_glu_b, ssm_w_out, sg_w_in, sg_v_g, sg_w_s, sg_b_s, sg_w_out, loss_target, m_ada_w, m_ada_b, m_norm1_g, m_norm2_g, m_ff_w1, m_ff_w2, m_final_g, m_conv_w_in, m_conv_w, m_conv_b, m_conv_w_out, m_ssm_w_in, m_ssm_a_re, m_ssm_a_im, m_ssm_log_dt, m_ssm_b_re, m_ssm_b_im, m_ssm_c_re, m_ssm_c_im, m_ssm_d, m_ssm_glu_w, m_ssm_glu_b, m_ssm_w_out, m_sg_w_in, m_sg_v_g, m_sg_w_s, m_sg_b_s, m_sg_w_out, v_ada_w, v_ada_b, v_norm1_g, v_norm2_g, v_ff_w1, v_ff_w2, v_final_g, v_conv_w_in, v_conv_w, v_conv_b, v_conv_w_out, v_ssm_w_in, v_ssm_a_re, v_ssm_a_im, v_ssm_log_dt, v_ssm_b_re, v_ssm_b_im, v_ssm_c_re, v_ssm_c_im, v_ssm_d, v_ssm_glu_w, v_ssm_glu_b, v_ssm_w_out, v_sg_w_in, v_sg_v_g, v_sg_w_s, v_sg_b_s, v_sg_w_out)))
    L, D = x.shape[1], x.shape[2]
    me = _my_index()
    xs = x[0]
    tgt = loss_target[0]
    n_conv = conv_w_in.shape[0]

    def gather_item(shard, axis):
        full = tuple(N_DEV * s if a == axis else s for a, s in enumerate(shard.shape))
        return shard, full, ('gather', axis, shard.shape[axis])

    def mixer_shards(i):
        kind, j = i % 3, i // 3
        if kind == 0:
            return [(conv_w_in[j], 1), (conv_w_out[j], 0)]
        if kind == 1:
            return [(ssm_w_in[j], 0), (ssm_glu_w[j], 0), (ssm_w_out[j], 0)]
        return [(sg_w_in[j], 1), (sg_w_out[j], 0)]

    c_act = c * (1.0 / (1.0 + jnp.exp(-c)))
    vec_rows = jnp.concatenate([c_act.reshape(D // LANE, LANE), conv_w.reshape(-1, LANE), conv_b.reshape(-1, LANE),
                                sg_v_g.reshape(-1, LANE)], 0)
    n_vec = vec_rows.shape[0]
    vec_all = _all_gather(_pad_rows(vec_rows, 24)[None], 0, name="gather_vectors")
    c_all = vec_all[:, :D // LANE, :].reshape(N_DEV, D)
    sharded_full = vec_all[:, D // LANE:n_vec, :].transpose(1, 0, 2).reshape(n_vec - D // LANE, D)
    conv_w_full = sharded_full[:3 * n_conv].reshape(n_conv, 3, D)
    conv_b_full = sharded_full[3 * n_conv:4 * n_conv]
    sg_vg_full = sharded_full[4 * n_conv:4 * n_conv + 1]

    c_pad = _pad_rows(c_all, LANE)
    ncol = ada_w.shape[2]
    mod_part = jnp.stack([_mm(c_pad, ada_w[i], name=f"ada_fwd{i}")[:N_DEV] for i in range(DEPTH)])
    mod_all = _all_gather(mod_part.reshape(1, DEPTH * N_DEV, ncol), 0, name="gather_mod")
    mod_all = mod_all.reshape(N_DEV, DEPTH, N_DEV, ncol)
    mod_me = lax.dynamic_index_in_dim(mod_all, me, 2, keepdims=False)
    mod = mod_me.transpose(1, 0, 2).reshape(DEPTH, N_DEV * ncol) + ada_b
    gathers, gather_token = _exchange_start(
        [[gather_item(w.astype(BF16), ax) for w, ax in shards]
         for i in range(DEPTH) for shards in (mixer_shards(i), [(ff_w1[i], 1), (ff_w2[i], 0)])],
        mod, name="gather_start")
    mod = mod + gather_token[0:1, 0:1]

    s5_args = (ssm_a_re[0], ssm_a_im[0], ssm_log_dt[0], ssm_b_re[0], ssm_b_im[0], ssm_c_re[0], ssm_c_im[0])
    (abar_re, abar_im, bblk_re, bblk_im, cblk_re, cblk_im), s5_vjp = jax.vjp(_s5_prep, *s5_args)
    (pw_fwd, pos_fwd), (pw_rev, pos_rev) = _s5_power_tables(abar_re, abar_im, S5_CHUNK // SUBLANE)
    s5_w = tuple(t.astype(BF16) for t in (bblk_re, bblk_im, cblk_re, cblk_im))
    causal = jnp.tril(jnp.ones((SG_CHUNK, SG_CHUNK), dtype=bool))
    ws_m = jnp.where(causal[None], sg_w_s[0], 0.0)
    ws_b = ws_m.astype(BF16)
    wst_b = ws_m.transpose(0, 2, 1).astype(BF16)
    bsb = jnp.broadcast_to(sg_b_s[0][:, :, None], (SG_HEADS, SG_CHUNK, LANE))

    saved = []
    xa = xs
    for i in range(DEPTH):
        kind, j = i % 3, i // 3
        sh1, sc1, g1, sh2, sc2, g2 = (mod[i:i + 1, q * D:(q + 1) * D] for q in range(6))
        wn1 = norm1_g[i:i + 1] * (1.0 + sc1)
        wn2 = norm2_g[i:i + 1] * (1.0 + sc2)
        S = dict(x_in=xa, g1=g1, g2=g2, sc1=sc1, sc2=sc2, wn1=wn1, wn2=wn2)
        h1 = _normmod_fwd(xa, wn1, sh1, name=f"norm1_fwd{i}")
        w_mix = _exchange_wait(gathers[2 * i], h1, name=f"gather_mix_wait{i}")
        S['h1'] = h1
        if kind == 0:
            bcx = _mm(h1, w_mix[0], name=f"conv_in{i}", out_dtypes=(BF16,), bm=2048)
            wb = _pad_rows(jnp.concatenate([conv_w_full[j], conv_b_full[j:j + 1]], 0), SUBLANE)
            pb = _conv_fwd(bcx, wb, name=f"conv_mix{i}")
            S.update(bcx=bcx, wb=wb, pb=pb)
        elif kind == 1:
            u = _mm(h1, w_mix[0], name=f"ssm_in{i}")
            sre, sim, ypre, yg = _s5_fwd(u, *s5_w, pw_fwd, pos_fwd, ssm_d, name=f"s5_scan{i}")

            def glu_epi(acc, yv, bias):
                t = acc + bias
                return yv * (1.0 / (1.0 + jnp.exp(-t))), t

            pb, tt = _mm(yg, w_mix[1], name=f"ssm_glu{i}", out_dtypes=(BF16, F32), epi=glu_epi,
                         extras=[(yg, 'mn'), (ssm_glu_b, 'n')])
            S.update(u=u, sre=sre, sim=sim, ypre=ypre, yg=yg, pb=pb, tt=tt)
        else:
            uv = _mm(h1, w_mix[0], name=f"sg_in{i}", bm=2048)
            pb = _sg_fwd(uv, sg_vg_full, ws_b, bsb, name=f"sg_mix{i}")
            S.update(uv=uv, pb=pb)
        x_mid, y_mix = _mm(pb, w_mix[-1], name=f"mix_out{i}", out_dtypes=(F32, BF16), epi=_epi_residual,
                           extras=[(xa, 'mn'), (g1, 'n')])
        h2 = _normmod_fwd(x_mid, wn2, sh2, name=f"norm2_fwd{i}")
        w1_full, w2_full = _exchange_wait(gathers[2 * i + 1], h2, name=f"gather_ff_wait{i}")
        S.update(w_mix=w_mix, w1=w1_full, w2=w2_full)
        ra = _mm(h2, w1_full, name=f"ff_up{i}", out_dtypes=(BF16,), epi=lambda acc: (jnp.maximum(acc, 0.0),), bm=2048)
        xa, f_out = _mm(ra, w2_full, name=f"ff_down{i}", out_dtypes=(F32, BF16), epi=_epi_residual, a_fn=_square,
                        extras=[(x_mid, 'mn'), (g2, 'n')], bm=256, bk=w2_full.shape[0])
        S.update(x_mid=x_mid, y_mix=y_mix, h2=h2, ra=ra, f_out=f_out)
        saved.append(S)

    S = saved[-1]
    dx, st, dfb, loss_tile = _loss_head(xa, tgt, final_g[None], S['f_out'], S['g2'], name="loss_head")
    d_final_g = _stat_row(st, 0)
    dg2_next = _stat_row(st, 2)

    def scatter_item(g, axis):
        m = g.shape[axis] // N_DEV
        shard = tuple(m if a == axis else s for a, s in enumerate(g.shape))
        return g, (N_DEV,) + shard, ('scatter', axis, m)

    dmod = [None] * DEPTH
    dn1g, dn2g = [None] * DEPTH, [None] * DEPTH
    d_conv_w, d_conv_b = [None] * n_conv, [None] * n_conv
    ff_sent, mix_sent = [None] * DEPTH, [None] * DEPTH
    small = {}
    for i in reversed(range(DEPTH)):
        kind, j = i % 3, i // 3
        S = saved[i]
        w_mix = S['w_mix']
        dg2 = dg2_next
        da = _mm(dfb, S['w2'], tb=True, name=f"ff_down_bwd{i}", out_dtypes=(BF16,), bm=2048,
                 epi=lambda acc, rav: (acc * (2.0 * rav.astype(F32)),), extras=[(S['ra'], 'mn')])
        dw2 = _wgrad(S['ra'], dfb, name=f"ff_w2_grad{i}", a_fn=_square, bm=256, bn=1024)
        dh2 = _mm(da, S['w1'], tb=True, name=f"ff_up_bwd{i}", out_dtypes=(BF16,), bm=512, bk=da.shape[1])
        dw1 = _wgrad(S['h2'], da, name=f"ff_w1_grad{i}")
        (ff_sent[i],), token = _exchange_start([[scatter_item(dw1, 1), scatter_item(dw2, 0)]], dx,
                                               name=f"ff_grads_start{i}")
        dx_mid, st2, dyb = _normmod_bwd(dh2, S['x_mid'], S['wn2'] + token[0:1, 0:1], dx,
                                        (S['y_mix'], S['g1']), name=f"norm2_bwd{i}")
        dsc2 = _stat_row(st2, 0) * norm2_g[i:i + 1]
        dn2g[i] = _stat_row(st2, 0) * (1.0 + S['sc2'])
        dsh2 = _stat_row(st2, 1)
        dg1 = _stat_row(st2, 2)
        if kind == 0:
            dp = _mm(dyb, w_mix[1], tb=True, name=f"conv_out_bwd{i}", out_dtypes=(BF16,))
            d_cwo = _wgrad(S['pb'], dyb, name=f"conv_w_out_grad{i}")
            dbcx, stc = _conv_bwd(dp, S['bcx'], S['wb'], name=f"conv_mix_bwd{i}")
            d_conv_w[j] = stc[0:3]
            d_conv_b[j] = stc[3:4]
            dh1 = _mm(dbcx, w_mix[0], tb=True, name=f"conv_in_bwd{i}", out_dtypes=(BF16,), bm=512)
            d_cwi = _wgrad(S['h1'], dbcx, name=f"conv_w_in_grad{i}")
            mix_grads = [scatter_item(d_cwi, 1), scatter_item(d_cwo, 0)]
        elif kind == 1:
            dy2 = _mm(dyb, w_mix[2], tb=True, name=f"ssm_out_bwd{i}")
            d_ssm_out = _wgrad(S['pb'], dyb, name=f"ssm_w_out_grad{i}")
            dtb, dya, stg = _glu_bwd(dy2, S['yg'], S['tt'], name=f"ssm_glu_bwd{i}")
            dypre = _mm(dtb, w_mix[1], tb=True, name=f"ssm_glu_in_bwd{i}",
                        epi=lambda acc, a, yp: ((a + acc) * _gelu_grad(yp),),
                        extras=[(dya, 'mn'), (S['ypre'], 'mn')])
            d_glu = _wgrad(S['yg'], dtb, name=f"ssm_glu_w_grad{i}", bm=512)
            dub, dbre, dbim, dcre, dcim, ga, dd = _s5_bwd(dypre, S['u'], S['sre'], S['sim'], *s5_w, pw_rev, pos_rev, ssm_d,
                                                           name=f"s5_scan_bwd{i}")
            dh1 = _mm(dub, w_mix[0], tb=True, name=f"ssm_in_bwd{i}", out_dtypes=(BF16,))
            d_ssm_in = _wgrad(S['h1'], dub, name=f"ssm_w_in_grad{i}")
            da_re, da_im, dlog_dt, db_re, db_im, dc_re, dc_im = s5_vjp((ga[0:1], ga[1:2], dbre, dbim, dcre, dcim))
            s5_small = _pack_rows([da_re, da_im, dlog_dt, db_re, db_im, dc_re, dc_im, dd[0], stg[0]])
            mix_grads = [scatter_item(d_ssm_in, 0), scatter_item(d_glu, 0), scatter_item(d_ssm_out, 0),
                         gather_item(s5_small, 0)]
        else:
            dp = _mm(dyb, w_mix[1], tb=True, name=f"sg_out_bwd{i}")
            d_sgo = _wgrad(S['pb'], dyb, name=f"sg_w_out_grad{i}")
            duv, dws, dbs, stv = _sg_bwd(dp, S['uv'], sg_vg_full, ws_b, wst_b, bsb, name=f"sg_mix_bwd{i}")
            dh1 = _mm(duv, w_mix[0], tb=True, name=f"sg_in_bwd{i}", out_dtypes=(BF16,), bm=512, bk=duv.shape[1])
            d_sgi = _wgrad(S['h1'], duv, name=f"sg_w_in_grad{i}")
            sg_small = _pack_rows([jnp.where(causal[None], dws, 0.0), jnp.sum(dbs, axis=-1)])
            d_sg_vg = stv[0:1]
            mix_grads = [scatter_item(d_sgi, 1), scatter_item(d_sgo, 0), gather_item(sg_small, 0)]
        wn1 = S['wn1']
        if i > 0:
            (mix_sent[i],), token = _exchange_start([mix_grads], dx_mid, name=f"mix_grads_start{i}")
            wn1 = wn1 + token[0:1, 0:1]
            prev = saved[i - 1]
            dx, st1, dfb = _normmod_bwd(dh1, S['x_in'], wn1, dx_mid, (prev['f_out'], prev['g2']),
                                        name=f"norm1_bwd{i}")
            dg2_next = _stat_row(st1, 2)
        else:
            dx, st1 = _normmod_bwd(dh1, S['x_in'], wn1, dx_mid, None, name=f"norm1_bwd{i}")
        dsc1 = _stat_row(st1, 0) * norm1_g[i:i + 1]
        dn1g[i] = _stat_row(st1, 0) * (1.0 + S['sc1'])
        dsh1 = _stat_row(st1, 1)
        dmod[i] = jnp.concatenate([dsh1, dsc1, dg1, dsh2, dsc2, dg2], 1)
    grad_x = dx[None]

    out = {}

    def small_group(names, parts, label):
        shapes = [P[n].shape for n in names]
        w, m, v = (_pack_rows([P[pre + n] for n in names])[None] for pre in ('', 'm_', 'v_'))
        res = [_unpack_rows(t[0], shapes) for t in _adamw(w, [parts], m, v, name=label)]
        for q, n in enumerate(names):
            out[n] = tuple(r[q] for r in res)

    small.update(ada_b=jnp.concatenate(dmod, 0), norm1_g=jnp.concatenate(dn1g, 0), norm2_g=jnp.concatenate(dn2g, 0),
                 final_g=d_final_g, conv_w=jnp.stack(d_conv_w), conv_b=jnp.concatenate(d_conv_b, 0), sg_v_g=d_sg_vg)
    last_pack = _pack_rows([small[n] for n in LAST_SMALL + SMALL_SHARD])
    n_last = _pack_rows([P[n] for n in LAST_SMALL]).shape[0]
    n_pack = last_pack.shape[0]
    pack_all = _all_gather(jnp.concatenate([last_pack, loss_tile], 0)[None], 0, name="gather_small_grads")
    loss = jnp.sum(pack_all[:, n_pack, 0])
    (mix_sent[0],), _ = _exchange_start([mix_grads], pack_all, name="mix_grads_start0")
    small_group(LAST_SMALL, pack_all[:, :n_last], "adamw_small")
    sh_rows = (n_pack - n_last) // N_DEV
    sh_parts = pack_all[:, n_last:n_pack].reshape(N_DEV, sh_rows, N_DEV, LANE)
    sh_parts = lax.dynamic_index_in_dim(sh_parts, me, 2, keepdims=False)
    sh_parts = jnp.pad(sh_parts, ((0, 0), (0, 16 - sh_rows), (0, 0)))

    def pack_shard(prefix):
        return _pad_rows(jnp.concatenate([P[prefix + n].reshape(-1, LANE) for n in SMALL_SHARD], 0), 16)[None]

    sg_, sd_, sm_, sv_ = _adamw(pack_shard(''), [sh_parts], pack_shard('m_'), pack_shard('v_'), name="adamw_channel")
    off = 0
    for n in SMALL_SHARD:
        rows = math.prod(P[n].shape) // LANE
        out[n] = tuple(t[0, off:off + rows].reshape(P[n].shape) for t in (sg_, sd_, sm_, sv_))
        off += rows

    dmod_all = pack_all[:, :DEPTH * 6 * D // LANE].reshape(N_DEV, DEPTH, 6 * D)
    dmod_cols = lax.dynamic_slice_in_dim(dmod_all, me * ncol, ncol, 2)
    g_ada = [_mm(c_pad, _pad_rows(dmod_cols[:, i], LANE), ta=True, name=f"ada_w_grad{i}")[None] for i in range(DEPTH)]

    def big(name, parts):
        res = _adamw(P[name], parts, P['m_' + name], P['v_' + name], name="adamw_" + name)
        out[name] = res
        return res[1]

    ff_parts = [_exchange_wait(ff_sent[i], dx, name=f"ff_grads_wait{i}") for i in range(DEPTH)]
    mix_parts = [None] + [_exchange_wait(mix_sent[i], dx, name=f"mix_grads_wait{i}") for i in range(1, DEPTH)]
    big('ada_w', g_ada)
    big('ff_w1', [p[0] for p in ff_parts])
    big('ff_w2', [p[1] for p in ff_parts])
    done = big('sg_w_in', [mix_parts[2][0]])
    mix_parts[0] = _exchange_wait(mix_sent[0], done, name="mix_grads_wait0")
    big('conv_w_in', [mix_parts[i][0] for i in range(DEPTH) if i % 3 == 0])
    row_names = ['conv_w_out', 'ssm_w_in', 'ssm_glu_w', 'ssm_w_out', 'sg_w_out']
    row_parts = ([mix_parts[i][1] for i in range(DEPTH) if i % 3 == 0] + mix_parts[1][:3] + [mix_parts[2][1]])
    small_group(S5_SMALL, mix_parts[1][3].reshape(N_DEV, -1, LANE), "adamw_s5")
    small_group(SG_SMALL, mix_parts[2][2].reshape(N_DEV, -1, LANE), "adamw_sg")
    row_w, row_m, row_v = (jnp.concatenate([P[pre + n] for n in row_names], 0) for pre in ('', 'm_', 'v_'))
    rw = _adamw(row_w, row_parts, row_m, row_v, name="adamw_row_sharded")
    off = 0
    for n in row_names:
        cnt = P[n].shape[0]
        out[n] = tuple(t[off:off + cnt] for t in rw)
        off += cnt

    return (loss, grad_x, *[out[n][0] for n in WEIGHTS], *[out[n][1] for n in WEIGHTS],
            *[out[n][2] for n in WEIGHTS], *[out[n][3] for n in WEIGHTS])
```

```python
import math

import jax
import jax.numpy as jnp
from jax import lax
from jax.experimental import pallas as pl
from jax.experimental.pallas import tpu as pltpu

F32 = jnp.float32
BF16 = jnp.bfloat16

N_DEV = 8
MESH_ID = pl.DeviceIdType.MESH
DEPTH = 4
EPS = 1e-6
S5_GROUPS, S5_GROUP, S5_STATE = 64, 16, 64
S5_LANES = S5_GROUPS * S5_STATE
S5_BLOCKS = 8
S5_CHUNK = 512
SG_HEADS, SG_CHUNK = 8, 128
LANE = 128
SUBLANE = 8
VMEM_LIMIT = 48 * 1024 * 1024
ADAM_LR, ADAM_B1, ADAM_B2, ADAM_EPS, ADAM_WD, ADAM_STEP = 0.001, 0.9, 0.999, 1e-08, 0.01, 10
GELU_C = math.sqrt(2.0 / math.pi)
GELU_A = 0.044715

WEIGHTS = ['ada_w', 'ada_b', 'norm1_g', 'norm2_g', 'ff_w1', 'ff_w2', 'final_g', 'conv_w_in', 'conv_w', 'conv_b',
           'conv_w_out', 'ssm_w_in', 'ssm_a_re', 'ssm_a_im', 'ssm_log_dt', 'ssm_b_re', 'ssm_b_im', 'ssm_c_re',
           'ssm_c_im', 'ssm_d', 'ssm_glu_w', 'ssm_glu_b', 'ssm_w_out', 'sg_w_in', 'sg_v_g', 'sg_w_s', 'sg_b_s',
           'sg_w_out']
INPUTS = ['x', 'c'] + WEIGHTS + ['loss_target'] + ['m_' + n for n in WEIGHTS] + ['v_' + n for n in WEIGHTS]
S5_SMALL = ['ssm_a_re', 'ssm_a_im', 'ssm_log_dt', 'ssm_b_re', 'ssm_b_im', 'ssm_c_re', 'ssm_c_im', 'ssm_d', 'ssm_glu_b']
SG_SMALL = ['sg_w_s', 'sg_b_s']
LAST_SMALL = ['ada_b', 'norm1_g', 'norm2_g', 'final_g']
SMALL_SHARD = ['conv_w', 'conv_b', 'sg_v_g']


def _params(*sem):
    return pltpu.CompilerParams(dimension_semantics=sem or None, vmem_limit_bytes=VMEM_LIMIT)


def _my_pos():
    return lax.axis_index("x"), lax.axis_index("y"), lax.axis_index("c")


def _my_index():
    x, y, c = _my_pos()
    return 4 * x + 2 * y + c


def _mm(a, b, *, name, ta=False, tb=False, out_dtypes=(F32,), epi=None, extras=(), a_fn=None, bm=1024, bn=1024,
        bk=1024):
    a_chunks = a.shape[0] if a.ndim == 3 else 0
    b_chunks = b.shape[0] if b.ndim == 3 else 0
    assert not (a_chunks and ta) and not (b_chunks and tb)
    if a_chunks:
        m, k = a.shape[1], a_chunks * a.shape[2]
        bk = k
    else:
        m, k = (a.shape[1], a.shape[0]) if ta else a.shape
    if b_chunks:
        k2, n = b.shape[1], b_chunks * b.shape[2]
        bn = min(bn, b.shape[2])
    else:
        k2, n = (b.shape[1], b.shape[0]) if tb else b.shape
    assert k == k2, (a.shape, b.shape, ta, tb)
    bm, bn, bk = min(bm, m), min(bn, n), min(bk, k)
    assert m % bm == 0 and n % bn == 0 and k % bk == 0, (m, n, k, bm, bn, bk)
    nk = k // bk
    n_ex, n_out = len(extras), len(out_dtypes)
    dims = (((0 if ta else 1,), (1 if tb else 0,)), ((), ()))

    def body(*refs):
        a_ref, b_ref = refs[0], refs[1]
        ex_refs = refs[2:2 + n_ex]
        out_refs = refs[2 + n_ex:2 + n_ex + n_out]

        def finish(acc):
            outs = epi(acc, *[r[...] for r in ex_refs]) if epi is not None else (acc,)
            for r, o in zip(out_refs, outs):
                r[...] = o.astype(r.dtype)

        av = jnp.concatenate([a_ref[t] for t in range(a_chunks)], axis=1) if a_chunks else a_ref[...]
        if a_fn is not None:
            av = a_fn(av)
        part = lax.dot_general(av.astype(BF16), b_ref[...].astype(BF16), dims, preferred_element_type=F32)
        if nk == 1:
            finish(part)
            return
        acc_ref = refs[-1]
        kk = pl.program_id(2)

        @pl.when(kk == 0)
        def _():
            acc_ref[...] = part

        @pl.when(kk > 0)
        def _():
            acc_ref[...] += part

        @pl.when(kk == nk - 1)
        def _():
            finish(acc_ref[...])

    if a_chunks:
        a_spec = pl.BlockSpec((a_chunks, bm, a.shape[2]), lambda i, j, q: (0, i, 0))
    elif ta:
        a_spec = pl.BlockSpec((bk, bm), lambda i, j, q: (q, i))
    else:
        a_spec = pl.BlockSpec((bm, bk), lambda i, j, q: (i, q))
    if b_chunks:
        per = b.shape[2] // bn
        b_spec = pl.BlockSpec((None, bk, bn), lambda i, j, q: (j // per, q, j % per))
    elif tb:
        b_spec = pl.BlockSpec((bn, bk), lambda i, j, q: (j, q))
    else:
        b_spec = pl.BlockSpec((bk, bn), lambda i, j, q: (q, j))
    ex_specs = []
    for arr, kind in extras:
        if kind == 'mn':
            assert arr.shape == (m, n), (arr.shape, m, n)
            ex_specs.append(pl.BlockSpec((bm, bn), lambda i, j, q: (i, j)))
        else:
            assert arr.shape == (1, n), (arr.shape, n)
            ex_specs.append(pl.BlockSpec((1, bn), lambda i, j, q: (0, j)))
    outs = pl.pallas_call(
        body, name=name,
        out_shape=tuple(jax.ShapeDtypeStruct((m, n), d) for d in out_dtypes),
        grid=(m // bm, n // bn, nk),
        in_specs=[a_spec, b_spec] + ex_specs,
        out_specs=tuple(pl.BlockSpec((bm, bn), lambda i, j, q: (i, j)) for _ in out_dtypes),
        scratch_shapes=[pltpu.VMEM((bm, bn), F32)] if nk > 1 else [],
        compiler_params=_params("parallel", "parallel", "arbitrary"),
    )(a, b, *[arr for arr, _ in extras])
    return outs if n_out > 1 else outs[0]


def _epi_residual(acc, res, gate):
    return res + gate * acc, acc


def _wgrad(acts, cots, *, name, a_fn=None, bm=1024, bn=512):
    return _mm(acts, cots, ta=True, name=name, out_dtypes=(BF16,), a_fn=a_fn, bm=bm, bn=bn, bk=acts.shape[0])


def _square(a):
    af = a.astype(F32)
    return af * af


def _rstd(xv):
    return lax.rsqrt(jnp.mean(xv * xv, axis=-1, keepdims=True) + EPS)


def _normmod_fwd(x, w, sh, *, name, tm=512):
    L, D = x.shape

    def body(x_ref, w_ref, s_ref, h_ref):
        xv = x_ref[...]
        h_ref[...] = (xv * _rstd(xv) * w_ref[...] + s_ref[...]).astype(h_ref.dtype)

    row = pl.BlockSpec((tm, D), lambda i: (i, 0))
    vec = pl.BlockSpec((1, D), lambda i: (0, 0))
    return pl.pallas_call(body, name=name, out_shape=jax.ShapeDtypeStruct((L, D), BF16), grid=(L // tm,),
                          in_specs=[row, vec, vec], out_specs=row, compiler_params=_params("parallel"))(x, w, sh)


def _normmod_bwd(dh, x, w, dres, gate, *, name, tm=256):
    L, D = x.shape
    has_gate = gate is not None

    def body(*refs):
        if has_gate:
            dh_ref, x_ref, w_ref, r_ref, y_ref, g_ref, dx_ref, st_ref, dy_ref = refs
        else:
            dh_ref, x_ref, w_ref, r_ref, dx_ref, st_ref = refs
        i = pl.program_id(0)

        @pl.when(i == 0)
        def _():
            st_ref[...] = jnp.zeros_like(st_ref)

        xv = x_ref[...]
        dhv = dh_ref[...].astype(F32)
        rstd = _rstd(xv)
        xn = xv * rstd
        dxn = dhv * w_ref[...]
        dx = rstd * (dxn - xn * jnp.mean(dxn * xn, axis=-1, keepdims=True)) + r_ref[...]
        dx_ref[...] = dx
        st_ref[0:1, :] += jnp.sum(dhv * xn, axis=0, keepdims=True)
        st_ref[1:2, :] += jnp.sum(dhv, axis=0, keepdims=True)
        if has_gate:
            dy_ref[...] = (dx * g_ref[...]).astype(dy_ref.dtype)
            st_ref[2:3, :] += jnp.sum(dx * y_ref[...].astype(F32), axis=0, keepdims=True)

    row = pl.BlockSpec((tm, D), lambda i: (i, 0))
    vec = pl.BlockSpec((1, D), lambda i: (0, 0))
    st = pl.BlockSpec((SUBLANE, D), lambda i: (0, 0))
    in_specs = [row, row, vec, row] + ([row, vec] if has_gate else [])
    out_shape = [jax.ShapeDtypeStruct((L, D), F32), jax.ShapeDtypeStruct((SUBLANE, D), F32)]
    out_specs = [row, st]
    if has_gate:
        out_shape.append(jax.ShapeDtypeStruct((L, D), BF16))
        out_specs.append(row)
    args = (dh, x, w, dres) + (tuple(gate) if has_gate else ())
    return pl.pallas_call(body, name=name, out_shape=tuple(out_shape), grid=(L // tm,), in_specs=in_specs,
                          out_specs=tuple(out_specs), compiler_params=_params("arbitrary"))(*args)


def _loss_head(x, tgt, fg, y, g, *, name, tm=256):
    L, D = x.shape

    def body(x_ref, t_ref, fg_ref, y_ref, g_ref, dx_ref, st_ref, dy_ref, loss_ref):
        i = pl.program_id(0)

        @pl.when(i == 0)
        def _():
            st_ref[...] = jnp.zeros_like(st_ref)
            loss_ref[...] = jnp.zeros_like(loss_ref)

        xv = x_ref[...]
        rstd = _rstd(xv)
        xn = xv * rstd
        err = xn * fg_ref[...] - t_ref[...]
        loss_ref[...] += 0.5 * jnp.sum(jnp.mean(err * err, axis=-1, keepdims=True))
        dout = err * (1.0 / D)
        dxn = dout * fg_ref[...]
        dx = rstd * (dxn - xn * jnp.mean(dxn * xn, axis=-1, keepdims=True))
        dx_ref[...] = dx
        dy_ref[...] = (dx * g_ref[...]).astype(dy_ref.dtype)
        st_ref[0:1, :] += jnp.sum(dout * xn, axis=0, keepdims=True)
        st_ref[2:3, :] += jnp.sum(dx * y_ref[...].astype(F32), axis=0, keepdims=True)

    row = pl.BlockSpec((tm, D), lambda i: (i, 0))
    vec = pl.BlockSpec((1, D), lambda i: (0, 0))
    return pl.pallas_call(
        body, name=name,
        out_shape=(jax.ShapeDtypeStruct((L, D), F32), jax.ShapeDtypeStruct((SUBLANE, D), F32),
                   jax.ShapeDtypeStruct((L, D), BF16), jax.ShapeDtypeStruct((SUBLANE, LANE), F32)),
        grid=(L // tm,), in_specs=[row, row, vec, row, vec],
        out_specs=(row, pl.BlockSpec((SUBLANE, D), lambda i: (0, 0)), row,
                   pl.BlockSpec((SUBLANE, LANE), lambda i: (0, 0))),
        compiler_params=_params("arbitrary"))(x, tgt, fg, y, g)


def _shift_down(v, k):
    row = lax.broadcasted_iota(jnp.int32, v.shape, 0)
    return jnp.where(row >= k, pltpu.roll(v, k, 0), 0.0)


def _shift_up(v, k):
    n = v.shape[0]
    row = lax.broadcasted_iota(jnp.int32, v.shape, 0)
    return jnp.where(row < n - k, pltpu.roll(v, n - k, 0), 0.0)


def _conv_views(L, D):
    return [pl.BlockSpec((L, LANE), lambda j, s=s: (0, s * (D // LANE) + j)) for s in range(3)]


def _conv_fwd(bcx, wb, *, name):
    L, D = bcx.shape[0], bcx.shape[1] // 3

    def body(b_ref, c_ref, x_ref, wb_ref, p_ref):
        z = c_ref[...].astype(F32) * x_ref[...].astype(F32)
        conv = (wb_ref[0:1, :] * _shift_down(z, 2) + wb_ref[1:2, :] * _shift_down(z, 1)
                + wb_ref[2:3, :] * z + wb_ref[3:4, :])
        p_ref[...] = (b_ref[...].astype(F32) * conv).astype(p_ref.dtype)

    col = pl.BlockSpec((L, LANE), lambda j: (0, j))
    return pl.pallas_call(body, name=name, out_shape=jax.ShapeDtypeStruct((L, D), BF16), grid=(D // LANE,),
                          in_specs=_conv_views(L, D) + [pl.BlockSpec((SUBLANE, LANE), lambda j: (0, j))],
                          out_specs=col, compiler_params=_params("parallel"))(bcx, bcx, bcx, wb)


def _conv_bwd(dp, bcx, wb, *, name):
    L, D = dp.shape

    def body(dp_ref, b_ref, c_ref, x_ref, wb_ref, d3_ref, st_ref):
        cv, xv = c_ref[...].astype(F32), x_ref[...].astype(F32)
        z = cv * xv
        z1, z2 = _shift_down(z, 1), _shift_down(z, 2)
        w0, w1, w2 = wb_ref[0:1, :], wb_ref[1:2, :], wb_ref[2:3, :]
        conv = w0 * z2 + w1 * z1 + w2 * z + wb_ref[3:4, :]
        dpv = dp_ref[...].astype(F32)
        d3_ref[0] = (dpv * conv).astype(d3_ref.dtype)
        dconv = dpv * b_ref[...].astype(F32)
        dz = w2 * dconv + w1 * _shift_up(dconv, 1) + w0 * _shift_up(dconv, 2)
        d3_ref[1] = (dz * xv).astype(d3_ref.dtype)
        d3_ref[2] = (dz * cv).astype(d3_ref.dtype)
        st_ref[...] = jnp.zeros_like(st_ref)
        st_ref[0:1, :] = jnp.sum(dconv * z2, axis=0, keepdims=True)
        st_ref[1:2, :] = jnp.sum(dconv * z1, axis=0, keepdims=True)
        st_ref[2:3, :] = jnp.sum(dconv * z, axis=0, keepdims=True)
        st_ref[3:4, :] = jnp.sum(dconv, axis=0, keepdims=True)

    col = pl.BlockSpec((L, LANE), lambda j: (0, j))
    vec = pl.BlockSpec((SUBLANE, LANE), lambda j: (0, j))
    return pl.pallas_call(body, name=name,
                          out_shape=(jax.ShapeDtypeStruct((3, L, D), BF16), jax.ShapeDtypeStruct((SUBLANE, D), F32)),
                          grid=(D // LANE,), in_specs=[col] + _conv_views(L, D) + [vec],
                          out_specs=(pl.BlockSpec((3, L, LANE), lambda j: (0, 0, j)), vec),
                          compiler_params=_params("parallel"))(dp, bcx, bcx, bcx, wb)


def _sg_fwd(uv, vg, ws, bsb, *, name, tr=512):
    L, D = uv.shape[0], uv.shape[1] // 2

    def body(uv_ref, vg_ref, ws_ref, bsb_ref, p_ref):
        for ci in range(tr // SG_CHUNK):
            rows = slice(ci * SG_CHUNK, (ci + 1) * SG_CHUNK)
            v = uv_ref[rows, D:2 * D]
            vn = (v * _rstd(v) * vg_ref[...]).astype(BF16)
            for h in range(SG_HEADS):
                cols = slice(h * LANE, (h + 1) * LANE)
                vm = jnp.dot(ws_ref[h], vn[:, cols], preferred_element_type=F32) + bsb_ref[h]
                p_ref[rows, cols] = (uv_ref[rows, cols] * vm).astype(p_ref.dtype)

    full3 = pl.BlockSpec((SG_HEADS, SG_CHUNK, LANE), lambda i: (0, 0, 0))
    return pl.pallas_call(body, name=name, out_shape=jax.ShapeDtypeStruct((L, D), BF16), grid=(L // tr,),
                          in_specs=[pl.BlockSpec((tr, 2 * D), lambda i: (i, 0)), pl.BlockSpec((1, D), lambda i: (0, 0)),
                                    full3, full3],
                          out_specs=pl.BlockSpec((tr, D), lambda i: (i, 0)),
                          compiler_params=_params("parallel"))(uv, vg, ws, bsb)


def _sg_bwd(dp, uv, vg, ws, wst, bsb, *, name, tr=512):
    L, D = dp.shape

    def body(dp_ref, uv_ref, vg_ref, ws_ref, wst_ref, bsb_ref, duv_ref, dws_ref, dbs_ref, st_ref, dvn_ref):
        i = pl.program_id(0)

        @pl.when(i == 0)
        def _():
            dws_ref[...] = jnp.zeros_like(dws_ref)
            dbs_ref[...] = jnp.zeros_like(dbs_ref)
            st_ref[...] = jnp.zeros_like(st_ref)

        for ci in range(tr // SG_CHUNK):
            rows = slice(ci * SG_CHUNK, (ci + 1) * SG_CHUNK)
            v = uv_ref[rows, D:2 * D]
            rstd = _rstd(v)
            vhat = v * rstd
            vn = (vhat * vg_ref[...]).astype(BF16)
            for h in range(SG_HEADS):
                cols = slice(h * LANE, (h + 1) * LANE)
                vm = jnp.dot(ws_ref[h], vn[:, cols], preferred_element_type=F32) + bsb_ref[h]
                dph = dp_ref[rows, cols]
                duv_ref[rows, cols] = (dph * vm).astype(duv_ref.dtype)
                dvm = dph * uv_ref[rows, cols]
                dbs_ref[h] += dvm
                dvmb = dvm.astype(BF16)
                dws_ref[h] += lax.dot_general(dvmb, vn[:, cols], (((1,), (1,)), ((), ())),
                                              preferred_element_type=F32)
                dvn_ref[rows, cols] = jnp.dot(wst_ref[h], dvmb, preferred_element_type=F32)
            dvn = dvn_ref[rows, :]
            gv = dvn * vg_ref[...]
            dv = rstd * (gv - vhat * jnp.mean(gv * vhat, axis=-1, keepdims=True))
            duv_ref[rows, D:2 * D] = dv.astype(duv_ref.dtype)
            st_ref[0:1, :] += jnp.sum(dvn * vhat, axis=0, keepdims=True)

    full3 = pl.BlockSpec((SG_HEADS, SG_CHUNK, LANE), lambda i: (0, 0, 0))
    acc3 = jax.ShapeDtypeStruct((SG_HEADS, SG_CHUNK, LANE), F32)
    return pl.pallas_call(
        body, name=name,
        out_shape=(jax.ShapeDtypeStruct((L, 2 * D), BF16), acc3, acc3, jax.ShapeDtypeStruct((SUBLANE, D), F32)),
        grid=(L // tr,),
        in_specs=[pl.BlockSpec((tr, D), lambda i: (i, 0)), pl.BlockSpec((tr, 2 * D), lambda i: (i, 0)),
                  pl.BlockSpec((1, D), lambda i: (0, 0)), full3, full3, full3],
        out_specs=(pl.BlockSpec((tr, 2 * D), lambda i: (i, 0)), full3, full3,
                   pl.BlockSpec((SUBLANE, D), lambda i: (0, 0))),
        scratch_shapes=[pltpu.VMEM((tr, D), F32)],
        compiler_params=_params("arbitrary"))(dp, uv, vg, ws, wst, bsb)


def _gelu(x):
    return 0.5 * x * (1.0 + jnp.tanh(GELU_C * (x + GELU_A * x * x * x)))


def _gelu_grad(x):
    th = jnp.tanh(GELU_C * (x + GELU_A * x * x * x))
    return 0.5 * (1.0 + th) + 0.5 * x * (1.0 - th * th) * GELU_C * (1.0 + 3.0 * GELU_A * x * x)


def _cmul_add(xr, xi, ar, ai, br, bi):
    return xr + ar * br - ai * bi, xi + ar * bi + ai * br


def _to_subchunk_order(src_ref, dst_ref, n):
    for k in range(n):
        dst_ref[pl.ds(SUBLANE * k, SUBLANE), :] = src_ref[pl.ds(k, SUBLANE, stride=n), :].astype(dst_ref.dtype)


def _to_time_order(src_ref, dst_ref, n):
    for m in range(n):
        r, k = divmod(SUBLANE * m, n)
        dst_ref[pl.ds(SUBLANE * m, SUBLANE), :] = src_ref[pl.ds(SUBLANE * k + r, SUBLANE, stride=SUBLANE), :]


def _s5_fwd(u, bre, bim, cre, cim, pw, pos, dsk, *, name, tc=S5_CHUNK):
    L, D = u.shape
    W = S5_LANES // S5_BLOCKS
    nt = L // tc
    n = tc // SUBLANE

    def sub(k):
        return pl.ds(SUBLANE * k, SUBLANE)

    def body(u_ref, bre_ref, bim_ref, cre_ref, cim_ref, pw_ref, pos_ref, d_ref, sre_ref, sim_ref, ypre_ref, yg_ref,
             carry, up, yp):
        t = pl.program_id(1)

        @pl.when(t == 0)
        def _():
            carry[...] = jnp.zeros_like(carry)

        _to_subchunk_order(u_ref, up, n)
        uv = up[...]
        ub = uv.astype(BF16)
        sre_ref[...] = jnp.dot(ub, bre_ref[...], preferred_element_type=F32)
        sim_ref[...] = jnp.dot(ub, bim_ref[...], preferred_element_type=F32)

        ar, ai = pw_ref[8], pw_ref[9]
        xr = jnp.zeros((SUBLANE, W), F32)
        xi = jnp.zeros((SUBLANE, W), F32)
        for k in range(n):
            xr, xi = _cmul_add(sre_ref[sub(k), :], sim_ref[sub(k), :], ar, ai, xr, xi)
            sre_ref[sub(k), :] = xr
            sim_ref[sub(k), :] = xi
        for q, d in enumerate((1, 2, 4)):
            xr, xi = _cmul_add(xr, xi, pw_ref[2 * q], pw_ref[2 * q + 1], pltpu.roll(xr, d, 0), pltpu.roll(xi, d, 0))
        cr, ci = carry[0], carry[1]
        xr, xi = _cmul_add(xr, xi, pw_ref[6], pw_ref[7], cr, ci)
        first = lax.broadcasted_iota(jnp.int32, (SUBLANE, W), 0) == 0
        er = jnp.where(first, cr, pltpu.roll(xr, 1, 0))
        ei = jnp.where(first, ci, pltpu.roll(xi, 1, 0))
        last = slice(SUBLANE - 1, SUBLANE)
        carry[0] = jnp.broadcast_to(xr[last, :], (SUBLANE, W))
        carry[1] = jnp.broadcast_to(xi[last, :], (SUBLANE, W))
        for k in range(n):
            sr, si = _cmul_add(sre_ref[sub(k), :], sim_ref[sub(k), :], pos_ref[0, k:k + 1, :], pos_ref[1, k:k + 1, :],
                               er, ei)
            sre_ref[sub(k), :] = sr
            sim_ref[sub(k), :] = si
        yp[...] = (jnp.dot(sre_ref[...].astype(BF16), cre_ref[...], preferred_element_type=F32)
                   - jnp.dot(sim_ref[...].astype(BF16), cim_ref[...], preferred_element_type=F32) + d_ref[...] * uv)
        _to_time_order(yp, ypre_ref, n)
        yg_ref[...] = _gelu(ypre_ref[...])

    ch = pl.BlockSpec((tc, LANE), lambda j, t: (t, j))
    st = pl.BlockSpec((tc, W), lambda j, t: (t, j))
    bsp = pl.BlockSpec((None, LANE, W), lambda j, t: (j, 0, 0))
    csp = pl.BlockSpec((None, W, LANE), lambda j, t: (j, 0, 0))
    return pl.pallas_call(
        body, name=name,
        out_shape=(jax.ShapeDtypeStruct((L, S5_LANES), F32), jax.ShapeDtypeStruct((L, S5_LANES), F32),
                   jax.ShapeDtypeStruct((L, D), F32), jax.ShapeDtypeStruct((L, D), F32)),
        grid=(S5_BLOCKS, nt),
        in_specs=[ch, bsp, bsp, csp, csp, pl.BlockSpec((10, SUBLANE, W), lambda j, t: (0, 0, j)),
                  pl.BlockSpec((2, n, W), lambda j, t: (0, 0, j)), pl.BlockSpec((1, LANE), lambda j, t: (0, j))],
        out_specs=(st, st, ch, ch),
        scratch_shapes=[pltpu.VMEM((2, SUBLANE, W), F32), pltpu.VMEM((tc, LANE), F32), pltpu.VMEM((tc, LANE), F32)],
        compiler_params=_params("parallel", "arbitrary"))(u, bre, bim, cre, cim, pw, pos, dsk)


def _s5_bwd(dy, u, sre, sim, bre, bim, cre, cim, pwr, posr, dsk, *, name, tc=S5_CHUNK):
    L, D = u.shape
    W = S5_LANES // S5_BLOCKS
    nt = L // tc
    n = tc // SUBLANE
    nt_dims = (((1,), (1,)), ((), ()))
    tn_dims = (((0,), (0,)), ((), ()))

    def sub(k):
        return pl.ds(SUBLANE * k, SUBLANE)

    def body(dy_ref, u_ref, sre_ref, sim_ref, bre_ref, bim_ref, cre_ref, cim_ref, pw_ref, pos_ref, d_ref,
             du_ref, dbre_ref, dbim_ref, dcre_ref, dcim_ref, ga_ref, dd_ref, gre, gim, carry, gacc, up, dyp):
        t = pl.program_id(1)

        @pl.when(t == 0)
        def _():
            for r in (carry, gacc, dbre_ref, dbim_ref, dcre_ref, dcim_ref, ga_ref, dd_ref):
                r[...] = jnp.zeros_like(r)

        _to_subchunk_order(dy_ref, dyp, n)
        _to_subchunk_order(u_ref, up, n)
        dyv, uv = dyp[...], up[...]
        dyb, ub = dyv.astype(BF16), uv.astype(BF16)
        gre[...] = lax.dot_general(dyb, cre_ref[...], nt_dims, preferred_element_type=F32)
        gim[...] = -lax.dot_general(dyb, cim_ref[...], nt_dims, preferred_element_type=F32)
        br, bi = pw_ref[8], pw_ref[9]
        xr = jnp.zeros((SUBLANE, W), F32)
        xi = jnp.zeros((SUBLANE, W), F32)
        for k in reversed(range(n)):
            xr, xi = _cmul_add(gre[sub(k), :], gim[sub(k), :], br, bi, xr, xi)
            gre[sub(k), :] = xr
            gim[sub(k), :] = xi
        for q, d in enumerate((1, 2, 4)):
            xr, xi = _cmul_add(xr, xi, pw_ref[2 * q], pw_ref[2 * q + 1], pltpu.roll(xr, SUBLANE - d, 0),
                               pltpu.roll(xi, SUBLANE - d, 0))
        cr, ci = carry[0], carry[1]
        xr, xi = _cmul_add(xr, xi, pw_ref[6], pw_ref[7], cr, ci)
        top = lax.broadcasted_iota(jnp.int32, (SUBLANE, W), 0) == SUBLANE - 1
        er = jnp.where(top, cr, pltpu.roll(xr, SUBLANE - 1, 0))
        ei = jnp.where(top, ci, pltpu.roll(xi, SUBLANE - 1, 0))
        carry[0] = jnp.broadcast_to(xr[0:1, :], (SUBLANE, W))
        carry[1] = jnp.broadcast_to(xi[0:1, :], (SUBLANE, W))
        nr, ni = er, ei
        acc_r = jnp.zeros((SUBLANE, W), F32)
        acc_i = jnp.zeros((SUBLANE, W), F32)
        for k in reversed(range(n)):
            gr, gi = _cmul_add(gre[sub(k), :], gim[sub(k), :], pos_ref[0, k:k + 1, :], pos_ref[1, k:k + 1, :], er, ei)
            gre[sub(k), :] = gr
            gim[sub(k), :] = gi
            sr, si = sre_ref[sub(k), :], sim_ref[sub(k), :]
            acc_r = acc_r + sr * nr + si * ni
            acc_i = acc_i + sr * ni - si * nr
            nr, ni = gr, gi
        gacc[0] += acc_r
        gacc[1] += acc_i
        grb, gib = gre[...].astype(BF16), gim[...].astype(BF16)
        dyp[...] = (lax.dot_general(grb, bre_ref[...], nt_dims, preferred_element_type=F32)
                    + lax.dot_general(gib, bim_ref[...], nt_dims, preferred_element_type=F32) + d_ref[...] * dyv)
        _to_time_order(dyp, up, n)
        du_ref[...] = up[...].astype(du_ref.dtype)
        dbre_ref[...] += lax.dot_general(ub, grb, tn_dims, preferred_element_type=F32)
        dbim_ref[...] += lax.dot_general(ub, gib, tn_dims, preferred_element_type=F32)
        dcre_ref[...] += lax.dot_general(sre_ref[...].astype(BF16), dyb, tn_dims, preferred_element_type=F32)
        dcim_ref[...] -= lax.dot_general(sim_ref[...].astype(BF16), dyb, tn_dims, preferred_element_type=F32)
        dd_ref[0:1, :] += jnp.sum(dyv * uv, axis=0, keepdims=True)

        @pl.when(t == nt - 1)
        def _():
            ga_ref[0:1, :] = jnp.sum(gacc[0], axis=0, keepdims=True)
            ga_ref[1:2, :] = jnp.sum(gacc[1], axis=0, keepdims=True)

    ch = pl.BlockSpec((tc, LANE), lambda j, t: (nt - 1 - t, j))
    st = pl.BlockSpec((tc, W), lambda j, t: (nt - 1 - t, j))
    bsp = pl.BlockSpec((None, LANE, W), lambda j, t: (j, 0, 0))
    csp = pl.BlockSpec((None, W, LANE), lambda j, t: (j, 0, 0))
    return pl.pallas_call(
        body, name=name,
        out_shape=(jax.ShapeDtypeStruct((L, D), BF16),
                   jax.ShapeDtypeStruct((S5_BLOCKS, LANE, W), F32), jax.ShapeDtypeStruct((S5_BLOCKS, LANE, W), F32),
                   jax.ShapeDtypeStruct((S5_BLOCKS, W, LANE), F32), jax.ShapeDtypeStruct((S5_BLOCKS, W, LANE), F32),
                   jax.ShapeDtypeStruct((SUBLANE, S5_LANES), F32), jax.ShapeDtypeStruct((SUBLANE, D), F32)),
        grid=(S5_BLOCKS, nt),
        in_specs=[ch, ch, st, st, bsp, bsp, csp, csp, pl.BlockSpec((10, SUBLANE, W), lambda j, t: (0, 0, j)),
                  pl.BlockSpec((2, n, W), lambda j, t: (0, 0, j)), pl.BlockSpec((1, LANE), lambda j, t: (0, j))],
        out_specs=(ch, bsp, bsp, csp, csp, pl.BlockSpec((SUBLANE, W), lambda j, t: (0, j)),
                   pl.BlockSpec((SUBLANE, LANE), lambda j, t: (0, j))),
        scratch_shapes=[pltpu.VMEM((tc, W), F32), pltpu.VMEM((tc, W), F32), pltpu.VMEM((2, SUBLANE, W), F32),
                        pltpu.VMEM((2, SUBLANE, W), F32), pltpu.VMEM((tc, LANE), F32), pltpu.VMEM((tc, LANE), F32)],
        compiler_params=_params("parallel", "arbitrary"))(dy, u, sre, sim, bre, bim, cre, cim, pwr, posr, dsk)


def _glu_bwd(dy2, y, t, *, name, tm=256):
    L, D = y.shape

    def body(dy2_ref, y_ref, t_ref, dt_ref, dya_ref, st_ref):
        i = pl.program_id(0)

        @pl.when(i == 0)
        def _():
            st_ref[...] = jnp.zeros_like(st_ref)

        sig = 1.0 / (1.0 + jnp.exp(-t_ref[...]))
        dy2v = dy2_ref[...]
        dt = dy2v * y_ref[...] * sig * (1.0 - sig)
        dt_ref[...] = dt.astype(dt_ref.dtype)
        dya_ref[...] = dy2v * sig
        st_ref[0:1, :] += jnp.sum(dt, axis=0, keepdims=True)

    row = pl.BlockSpec((tm, D), lambda i: (i, 0))
    return pl.pallas_call(
        body, name=name,
        out_shape=(jax.ShapeDtypeStruct((L, D), BF16), jax.ShapeDtypeStruct((L, D), F32),
                   jax.ShapeDtypeStruct((SUBLANE, D), F32)),
        grid=(L // tm,), in_specs=[row, row, row],
        out_specs=(row, row, pl.BlockSpec((SUBLANE, D), lambda i: (0, 0))),
        compiler_params=_params("arbitrary"))(dy2, y, t)


def _s5_prep(a_re, a_im, log_dt, b_re, b_im, c_re, c_im):
    dt = jnp.exp(log_dt)[:, None]
    mag = jnp.exp(a_re * dt)
    abar_re = mag * jnp.cos(a_im * dt)
    abar_im = mag * jnp.sin(a_im * dt)
    den = a_re * a_re + a_im * a_im
    nr = abar_re - 1.0
    ni = abar_im
    f_re = ((nr * a_re + ni * a_im) / den)[..., None]
    f_im = ((ni * a_re - nr * a_im) / den)[..., None]
    bbar_re = f_re * b_re - f_im * b_im
    bbar_im = f_re * b_im + f_im * b_re
    eye = jnp.eye(S5_GROUPS // S5_BLOCKS, dtype=F32)
    gb = S5_GROUPS // S5_BLOCKS

    def blk_b(bb):
        t = bb.reshape(S5_BLOCKS, gb, S5_STATE, S5_GROUP)
        return jnp.einsum('jgph,gk->jghkp', t, eye).reshape(S5_BLOCKS, gb * S5_GROUP, gb * S5_STATE)

    def blk_c(cc):
        t = cc.reshape(S5_BLOCKS, gb, S5_GROUP, S5_STATE)
        return jnp.einsum('jghp,gk->jgpkh', t, eye).reshape(S5_BLOCKS, gb * S5_STATE, gb * S5_GROUP)

    return (abar_re.reshape(1, S5_LANES), abar_im.reshape(1, S5_LANES), blk_b(bbar_re), blk_b(bbar_im),
            blk_c(c_re), blk_c(c_im))


def _cpowers(ar, ai, count):
    pr, pi = [jnp.ones_like(ar)], [jnp.zeros_like(ai)]
    for _ in range(count):
        pr, pi = pr + [pr[-1] * ar - pi[-1] * ai], pi + [pr[-1] * ai + pi[-1] * ar]
    return pr, pi


def _s5_power_tables(ar, ai, n):
    pr, pi = _cpowers(ar, ai, n)
    qr, qi = _cpowers(pr[n], pi[n], SUBLANE)
    row = jnp.arange(SUBLANE)[:, None]
    lanes = ar.shape[1]

    def tables(sign, keep, carry_pow, places):
        out = []
        for d in (1, 2, 4):
            out += [jnp.where(keep(d), qr[d], 0.0), jnp.where(keep(d), sign * qi[d], 0.0)]
        out += [jnp.concatenate([qr[p] for p in carry_pow], 0), sign * jnp.concatenate([qi[p] for p in carry_pow], 0)]
        out += [ar, sign * ai]
        pw = jnp.stack([jnp.broadcast_to(o, (SUBLANE, lanes)) for o in out])
        pos = jnp.stack([jnp.concatenate([pr[p] for p in places], 0), sign * jnp.concatenate([pi[p] for p in places], 0)])
        return pw, pos

    fwd = tables(1.0, lambda d: row >= d, [r + 1 for r in range(SUBLANE)], [k + 1 for k in range(n)])
    rev = tables(-1.0, lambda d: row + d <= SUBLANE - 1, [SUBLANE - r for r in range(SUBLANE)], [n - k for k in range(n)])
    return fwd, rev


ADAMW_PART_BLOCK_BYTES = 2 * 1024 * 1024


def _adamw(w, parts, m, v, *, name):
    n, R, C = w.shape
    assert len(parts) == n
    P = parts[0].shape[0]
    tr = R
    while P * tr * C * parts[0].dtype.itemsize > ADAMW_PART_BLOCK_BYTES and tr % 16 == 0:
        tr //= 2
    c1 = 1.0 / (1.0 - ADAM_B1 ** ADAM_STEP)
    c2 = 1.0 / (1.0 - ADAM_B2 ** ADAM_STEP)

    def body(*refs):
        w_ref, m_ref, v_ref = refs[:3]
        p_refs = refs[3:3 + n]
        g_ref, d_ref, nm_ref, nv_ref = refs[3 + n:]
        layer = pl.program_id(0)
        for q, p_ref in enumerate(p_refs):
            @pl.when(layer == q)
            def _(p_ref=p_ref):
                g = p_ref[0].astype(F32)
                for s in range(1, P):
                    g = g + p_ref[s].astype(F32)
                nm = ADAM_B1 * m_ref[...] + (1.0 - ADAM_B1) * g
                nv = ADAM_B2 * v_ref[...] + (1.0 - ADAM_B2) * (g * g)
                g_ref[...] = g
                nm_ref[...] = nm
                nv_ref[...] = nv
                d_ref[...] = -ADAM_LR * ((nm * c1) / (jnp.sqrt(nv * c2) + ADAM_EPS) + ADAM_WD * w_ref[...])

    row = pl.BlockSpec((None, tr, C), lambda l, i: (l, i, 0))
    part_specs = [pl.BlockSpec((P, tr, C), lambda l, i, q=q: (0, jnp.where(l == q, i, 0), 0)) for q in range(n)]
    out = jax.ShapeDtypeStruct((n, R, C), F32)
    return pl.pallas_call(body, name=name, out_shape=(out, out, out, out), grid=(n, R // tr),
                          in_specs=[row, row, row] + part_specs, out_specs=(row, row, row, row),
                          compiler_params=_params("arbitrary", "arbitrary"))(w, m, v, *parts)


def _all_gather(xs, axis, *, name):
    m = xs.shape[axis]
    out_shape = list(xs.shape)
    out_shape[axis] = N_DEV * m

    def body(x_ref, out_ref, send_sems, recv_sems, local_sem):
        x, y, c = _my_pos()
        me, sibling = (x, y, c), (x, y, 1 - c)
        chips = [(1 - x, y), (x, 1 - y), (1 - x, 1 - y)]

        def blk(px, py, pc):
            idx = [slice(None)] * 3
            idx[axis] = pl.ds((4 * px + 2 * py + pc) * m, m)
            return out_ref.at[tuple(idx)]

        def copy(k, block, to, src=None):
            return pltpu.make_async_remote_copy(src_ref=blk(*block) if src is None else src, dst_ref=blk(*block),
                                                send_sem=send_sems.at[k], recv_sem=recv_sems.at[k],
                                                device_id=to, device_id_type=MESH_ID)

        mine = pltpu.make_async_copy(x_ref, blk(*me), local_sem)
        mine.start()
        first = [copy(0, me, sibling, src=x_ref)]
        first += [copy(1 + j, me, (*chip, c), src=x_ref) for j, chip in enumerate(chips)]
        for cp in first:
            cp.start()
        passed = [copy(4 + j, (*chip, c), sibling) for j, chip in enumerate(chips)]
        for j, chip in enumerate(chips):
            copy(1 + j, (*chip, c), me).wait_recv()
            passed[j].start()
        copy(0, sibling, me).wait_recv()
        for j, chip in enumerate(chips):
            copy(4 + j, (*chip, 1 - c), me).wait_recv()
        for cp in first + passed:
            cp.wait_send()
        mine.wait()

    hbm = pl.BlockSpec(memory_space=pl.ANY)
    return pl.pallas_call(body, name=name, out_shape=jax.ShapeDtypeStruct(tuple(out_shape), xs.dtype),
                          in_specs=[hbm], out_specs=hbm,
                          scratch_shapes=[pltpu.SemaphoreType.DMA((N_DEV - 1,)), pltpu.SemaphoreType.DMA((N_DEV - 1,)),
                                          pltpu.SemaphoreType.DMA],
                          compiler_params=pltpu.CompilerParams(has_side_effects=True))(xs)


def _block(ref, axis, idx, m):
    return ref.at[pl.ds(idx * m, m), :] if axis == 0 else ref.at[:, pl.ds(idx * m, m)]


def _exchange_copies(metas, src_refs, zone_refs, send_sems, recv_sems, group):
    x, y, c = _my_pos()
    me = 4 * x + 2 * y + c
    base = group * N_DEV
    pairs = []
    for r in range(1, N_DEV):
        pos = (1 - x if r & 4 else x, 1 - y if r & 2 else y, 1 - c if r & 1 else c)
        peer = 4 * pos[0] + 2 * pos[1] + pos[2]
        for (kind, axis, m), s_ref, z_ref in zip(metas, src_refs, zone_refs):
            if kind == 'gather':
                src, dst, arrival = s_ref, _block(z_ref, axis, me, m), _block(z_ref, axis, peer, m)
            else:
                src, dst, arrival = _block(s_ref, axis, peer, m), z_ref.at[me], z_ref.at[peer]
            pairs.append(tuple(
                pltpu.make_async_remote_copy(src_ref=src, dst_ref=d, send_sem=send_sems.at[base + r - 1],
                                             recv_sem=recv_sems.at[base + r - 1], device_id=pos,
                                             device_id_type=MESH_ID)
                for d in (dst, arrival)))
    own = []
    for (kind, axis, m), s_ref, z_ref in zip(metas, src_refs, zone_refs):
        src, dst = (s_ref, _block(z_ref, axis, me, m)) if kind == 'gather' else (_block(s_ref, axis, me, m), z_ref.at[me])
        own.append(pltpu.make_async_copy(src, dst, recv_sems.at[base + N_DEV - 1]))
    return pairs, own


def _exchange_start(groups, after, *, name):
    flat = [it for g in groups for it in g]
    n, ng = len(flat), len(groups)
    metas = [it[2] for it in flat]
    bounds = [(sum(len(g) for g in groups[:q]), sum(len(g) for g in groups[:q + 1])) for q in range(ng)]

    def body(*refs):
        src_refs = refs[:n]
        send_sems, recv_sems = refs[n + 1], refs[n + 2]
        zone_refs = refs[2 * n + 3:3 * n + 3]
        token = refs[-1]
        for q, (lo, hi) in enumerate(bounds):
            pairs, own = _exchange_copies(metas[lo:hi], src_refs[lo:hi], zone_refs[lo:hi], send_sems, recv_sems, q)
            for outgoing, _ in pairs:
                outgoing.start()
            for cp in own:
                cp.start()
        token[...] = jnp.zeros_like(token)

    hbm = pl.BlockSpec(memory_space=pltpu.HBM)
    sem = pl.BlockSpec(memory_space=pltpu.SEMAPHORE)
    srcs = [it[0] for it in flat]
    res = pl.pallas_call(
        body, name=name,
        out_shape=(pltpu.SemaphoreType.DMA((ng * N_DEV,)), pltpu.SemaphoreType.DMA((ng * N_DEV,)),
                   *[pltpu.HBM(a.shape, a.dtype) for a in srcs], *[pltpu.HBM(it[1], it[0].dtype) for it in flat],
                   jax.ShapeDtypeStruct((SUBLANE, LANE), F32)),
        in_specs=[hbm] * n + [pl.BlockSpec(memory_space=pl.ANY)],
        out_specs=(sem, sem, *[hbm] * (2 * n), pl.BlockSpec(memory_space=pltpu.VMEM)),
        input_output_aliases={q: 2 + q for q in range(n)},
        compiler_params=pltpu.CompilerParams(has_side_effects=pltpu.SideEffectType.DATAFLOW_SIDE_EFFECTING),
    )(*[pltpu.with_memory_space_constraint(a, pltpu.HBM) for a in srcs], after)
    handles = [(res[0], res[1], q, list(res[2 + lo:2 + hi]), list(res[2 + n + lo:2 + n + hi]), metas[lo:hi])
               for q, (lo, hi) in enumerate(bounds)]
    return handles, res[-1]


def _exchange_wait(handle, after, *, name):
    send_sems, recv_sems, group, srcs, zones, metas = handle
    n = len(srcs)

    def body(*refs):
        src_refs, zone_refs = refs[:n], refs[n:2 * n]
        s_sems, r_sems = refs[2 * n], refs[2 * n + 1]
        pairs, own = _exchange_copies(metas, src_refs, zone_refs, s_sems, r_sems, group)
        for outgoing, incoming in pairs:
            outgoing.wait_send()
            incoming.wait_recv()
        for cp in own:
            cp.wait()

    hbm = pl.BlockSpec(memory_space=pltpu.HBM)
    sem = pl.BlockSpec(memory_space=pltpu.SEMAPHORE)
    arrays = srcs + zones
    res = pl.pallas_call(
        body, name=name,
        out_shape=tuple(pltpu.HBM(a.shape, a.dtype) for a in arrays),
        in_specs=[hbm] * (2 * n) + [sem, sem, pl.BlockSpec(memory_space=pl.ANY)],
        out_specs=tuple([hbm] * (2 * n)),
        input_output_aliases={q: q for q in range(2 * n)},
        compiler_params=pltpu.CompilerParams(has_side_effects=pltpu.SideEffectType.DATAFLOW_SIDE_EFFECTING),
    )(*arrays, send_sems, recv_sems, after)
    return list(res[n:])


def _pad_rows(a, rows):
    return jnp.pad(a, ((0, rows - a.shape[0]), (0, 0)))


def _rows(a):
    flat = a.reshape(-1).astype(F32)
    pad = -flat.shape[0] % (SUBLANE * LANE)
    return (jnp.pad(flat, (0, pad)) if pad else flat).reshape(-1, LANE)


def _pack_rows(arrays):
    return jnp.concatenate([_rows(a) for a in arrays], 0)


def _unpack_rows(t, shapes):
    out, off = [], 0
    for shp in shapes:
        size = math.prod(shp)
        rows = -(-size // (SUBLANE * LANE)) * SUBLANE
        out.append(t[off:off + rows].reshape(-1)[:size].reshape(shp))
        off += rows
    return out


def _stat_row(st, r):
    return st[r:r + 1, :]


def kernel(x, c, ada_w, ada_b, norm1_g, norm2_g, ff_w1, ff_w2, final_g, conv_w_in, conv_w, conv_b, conv_w_out, ssm_w_in, ssm_a_re, ssm_a_im, ssm_log_dt, ssm_b_re, ssm_b_im, ssm_c_re, ssm_c_im, ssm_d, ssm_glu_w, ssm_glu_b, ssm_w_out, sg_w_in, sg_v_g, sg_w_s, sg_b_s, sg_w_out, loss_target, m_ada_w, m_ada_b, m_norm1_g, m_norm2_g, m_ff_w1, m_ff_w2, m_final_g, m_conv_w_in, m_conv_w, m_conv_b, m_conv_w_out, m_ssm_w_in, m_ssm_a_re, m_ssm_a_im, m_ssm_log_dt, m_ssm_b_re, m_ssm_b_im, m_ssm_c_re, m_ssm_c_im, m_ssm_d, m_ssm_glu_w, m_ssm_glu_b, m_ssm_w_out, m_sg_w_in, m_sg_v_g, m_sg_w_s, m_sg_b_s, m_sg_w_out, v_ada_w, v_ada_b, v_norm1_g, v_norm2_g, v_ff_w1, v_ff_w2, v_final_g, v_conv_w_in, v_conv_w, v_conv_b, v_conv_w_out, v_ssm_w_in, v_ssm_a_re, v_ssm_a_im, v_ssm_log_dt, v_ssm_b_re, v_ssm_b_im, v_ssm_c_re, v_ssm_c_im, v_ssm_d, v_ssm_glu_w, v_ssm_glu_b, v_ssm_w_out, v_sg_w_in, v_sg_v_g, v_sg_w_s, v_sg_b_s, v_sg_w_out):
    P = dict(zip(INPUTS, (x, c, ada_w, ada_b, norm1_g, norm2_g, ff_w1, ff_w2, final_g, conv_w_in, conv_w, conv_b, conv_w_out, ssm_w_in, ssm_a_re, ssm_a_im, ssm_log_dt, ssm_b_re, ssm_b_im, ssm_c_re, ssm_c_im, ssm_d, ssm_glu_w, ssm_glu_b, ssm_w_out, sg_w_in, sg_v_g, sg_w_s, sg_b_s, sg_w_out, loss_target, m_ada_w, m_ada_b, m_norm1_g, m_norm2_g, m_ff_w1, m_ff_w2, m_final_g, m_conv_w_in, m_conv_w, m_conv_b, m_conv_w_out, m_ssm_w_in, m_ssm_a_re, m_ssm_a_im, m_ssm_log_dt, m_ssm_b_re, m_ssm_b_im, m_ssm_c_re, m_ssm_c_im, m_ssm_d, m_ssm_glu_w, m_ssm_glu_b, m_ssm_w_out, m_sg_w_in, m_sg_v_g, m_sg_w_s, m_sg_b_s, m_sg_w_out, v_ada_w, v_ada_b, v_norm1_g, v_norm2_g, v_ff_w1, v_ff_w2, v_final_g, v_conv_w_in, v_conv_w, v_conv_b, v_conv_w_out, v_ssm_w_in, v_ssm_a_re, v_ssm_a_im, v_ssm_log_dt, v_ssm_b_re, v_ssm_b_im, v_ssm_c_re, v_ssm_c_im, v_ssm_d, v_ssm_glu_w, v_ssm_glu_b, v_ssm_w_out, v_sg_w_in, v_sg_v_g, v_sg_w_s, v_sg_b_s, v_sg_w_out)))
    L, D = x.shape[1], x.shape[2]
    me = _my_index()
    xs = x[0]
    tgt = loss_target[0]
    n_conv = conv_w_in.shape[0]

    def gather_item(shard, axis):
        full = tuple(N_DEV * s if a == axis else s for a, s in enumerate(shard.shape))
        return shard, full, ('gather', axis, shard.shape[axis])

    def mixer_shards(i):
        kind, j = i % 3, i // 3
        if kind == 0:
            return [(conv_w_in[j], 1), (conv_w_out[j], 0)]
        if kind == 1:
            return [(ssm_w_in[j], 0), (ssm_glu_w[j], 0), (ssm_w_out[j], 0)]
        return [(sg_w_in[j], 1), (sg_w_out[j], 0)]

    c_act = c * (1.0 / (1.0 + jnp.exp(-c)))
    vec_rows = jnp.concatenate([c_act.reshape(D // LANE, LANE), conv_w.reshape(-1, LANE), conv_b.reshape(-1, LANE),
                                sg_v_g.reshape(-1, LANE)], 0)
    n_vec = vec_rows.shape[0]
    vec_all = _all_gather(_pad_rows(vec_rows, 24)[None], 0, name="gather_vectors")
    c_all = vec_all[:, :D // LANE, :].reshape(N_DEV, D)
    sharded_full = vec_all[:, D // LANE:n_vec, :].transpose(1, 0, 2).reshape(n_vec - D // LANE, D)
    conv_w_full = sharded_full[:3 * n_conv].reshape(n_conv, 3, D)
    conv_b_full = sharded_full[3 * n_conv:4 * n_conv]
    sg_vg_full = sharded_full[4 * n_conv:4 * n_conv + 1]

    c_pad = _pad_rows(c_all, LANE)
    ncol = ada_w.shape[2]
    mod_part = jnp.stack([_mm(c_pad, ada_w[i], name=f"ada_fwd{i}")[:N_DEV] for i in range(DEPTH)])
    mod_all = _all_gather(mod_part.reshape(1, DEPTH * N_DEV, ncol), 0, name="gather_mod")
    mod_all = mod_all.reshape(N_DEV, DEPTH, N_DEV, ncol)
    mod_me = lax.dynamic_index_in_dim(mod_all, me, 2, keepdims=False)
    mod = mod_me.transpose(1, 0, 2).reshape(DEPTH, N_DEV * ncol) + ada_b
    gathers, gather_token = _exchange_start(
        [[gather_item(w.astype(BF16), ax) for w, ax in shards]
         for i in range(DEPTH) for shards in (mixer_shards(i), [(ff_w1[i], 1), (ff_w2[i], 0)])],
        mod, name="gather_start")
    mod = mod + gather_token[0:1, 0:1]

    s5_args = (ssm_a_re[0], ssm_a_im[0], ssm_log_dt[0], ssm_b_re[0], ssm_b_im[0], ssm_c_re[0], ssm_c_im[0])
    (abar_re, abar_im, bblk_re, bblk_im, cblk_re, cblk_im), s5_vjp = jax.vjp(_s5_prep, *s5_args)
    (pw_fwd, pos_fwd), (pw_rev, pos_rev) = _s5_power_tables(abar_re, abar_im, S5_CHUNK // SUBLANE)
    s5_w = tuple(t.astype(BF16) for t in (bblk_re, bblk_im, cblk_re, cblk_im))
    causal = jnp.tril(jnp.ones((SG_CHUNK, SG_CHUNK), dtype=bool))
    ws_m = jnp.where(causal[None], sg_w_s[0], 0.0)
    ws_b = ws_m.astype(BF16)
    wst_b = ws_m.transpose(0, 2, 1).astype(BF16)
    bsb = jnp.broadcast_to(sg_b_s[0][:, :, None], (SG_HEADS, SG_CHUNK, LANE))

    saved = []
    xa = xs
    for i in range(DEPTH):
        kind, j = i % 3, i // 3
        sh1, sc1, g1, sh2, sc2, g2 = (mod[i:i + 1, q * D:(q + 1) * D] for q in range(6))
        wn1 = norm1_g[i:i + 1] * (1.0 + sc1)
        wn2 = norm2_g[i:i + 1] * (1.0 + sc2)
        S = dict(x_in=xa, g1=g1, g2=g2, sc1=sc1, sc2=sc2, wn1=wn1, wn2=wn2)
        h1 = _normmod_fwd(xa, wn1, sh1, name=f"norm1_fwd{i}")
        w_mix = _exchange_wait(gathers[2 * i], h1, name=f"gather_mix_wait{i}")
        S['h1'] = h1
        if kind == 0:
            bcx = _mm(h1, w_mix[0], name=f"conv_in{i}", out_dtypes=(BF16,), bm=2048)
            wb = _pad_rows(jnp.concatenate([conv_w_full[j], conv_b_full[j:j + 1]], 0), SUBLANE)
            pb = _conv_fwd(bcx, wb, name=f"conv_mix{i}")
            S.update(bcx=bcx, wb=wb, pb=pb)
        elif kind == 1:
            u = _mm(h1, w_mix[0], name=f"ssm_in{i}")
            sre, sim, ypre, yg = _s5_fwd(u, *s5_w, pw_fwd, pos_fwd, ssm_d, name=f"s5_scan{i}")

            def glu_epi(acc, yv, bias):
                t = acc + bias
                return yv * (1.0 / (1.0 + jnp.exp(-t))), t

            pb, tt = _mm(yg, w_mix[1], name=f"ssm_glu{i}", out_dtypes=(BF16, F32), epi=glu_epi,
                         extras=[(yg, 'mn'), (ssm_glu_b, 'n')])
            S.update(u=u, sre=sre, sim=sim, ypre=ypre, yg=yg, pb=pb, tt=tt)
        else:
            uv = _mm(h1, w_mix[0], name=f"sg_in{i}", bm=2048)
            pb = _sg_fwd(uv, sg_vg_full, ws_b, bsb, name=f"sg_mix{i}")
            S.update(uv=uv, pb=pb)
        x_mid, y_mix = _mm(pb, w_mix[-1], name=f"mix_out{i}", out_dtypes=(F32, BF16), epi=_epi_residual,
                           extras=[(xa, 'mn'), (g1, 'n')])
        h2 = _normmod_fwd(x_mid, wn2, sh2, name=f"norm2_fwd{i}")
        w1_full, w2_full = _exchange_wait(gathers[2 * i + 1], h2, name=f"gather_ff_wait{i}")
        S.update(w_mix=w_mix, w1=w1_full, w2=w2_full)
        ra = _mm(h2, w1_full, name=f"ff_up{i}", out_dtypes=(BF16,), epi=lambda acc: (jnp.maximum(acc, 0.0),), bm=2048)
        xa, f_out = _mm(ra, w2_full, name=f"ff_down{i}", out_dtypes=(F32, BF16), epi=_epi_residual, a_fn=_square,
                        extras=[(x_mid, 'mn'), (g2, 'n')], bm=256, bk=w2_full.shape[0])
        S.update(x_mid=x_mid, y_mix=y_mix, h2=h2, ra=ra, f_out=f_out)
        saved.append(S)

    S = saved[-1]
    dx, st, dfb, loss_tile = _loss_head(xa, tgt, final_g[None], S['f_out'], S['g2'], name="loss_head")
    d_final_g = _stat_row(st, 0)
    dg2_next = _stat_row(st, 2)

    def scatter_item(g, axis):
        m = g.shape[axis] // N_DEV
        shard = tuple(m if a == axis else s for a, s in enumerate(g.shape))
        return g, (N_DEV,) + shard, ('scatter', axis, m)

    dmod = [None] * DEPTH
    dn1g, dn2g = [None] * DEPTH, [None] * DEPTH
    d_conv_w, d_conv_b = [None] * n_conv, [None] * n_conv
    ff_sent, mix_sent = [None] * DEPTH, [None] * DEPTH
    small = {}
    for i in reversed(range(DEPTH)):
        kind, j = i % 3, i // 3
        S = saved[i]
        w_mix = S['w_mix']
        dg2 = dg2_next
        da = _mm(dfb, S['w2'], tb=True, name=f"ff_down_bwd{i}", out_dtypes=(BF16,), bm=2048,
                 epi=lambda acc, rav: (acc * (2.0 * rav.astype(F32)),), extras=[(S['ra'], 'mn')])
        dw2 = _wgrad(S['ra'], dfb, name=f"ff_w2_grad{i}", a_fn=_square, bm=256, bn=1024)
        dh2 = _mm(da, S['w1'], tb=True, name=f"ff_up_bwd{i}", out_dtypes=(BF16,), bm=512, bk=da.shape[1])
        dw1 = _wgrad(S['h2'], da, name=f"ff_w1_grad{i}")
        (ff_sent[i],), token = _exchange_start([[scatter_item(dw1, 1), scatter_item(dw2, 0)]], dx,
                                               name=f"ff_grads_start{i}")
        dx_mid, st2, dyb = _normmod_bwd(dh2, S['x_mid'], S['wn2'] + token[0:1, 0:1], dx,
                                        (S['y_mix'], S['g1']), name=f"norm2_bwd{i}")
        dsc2 = _stat_row(st2, 0) * norm2_g[i:i + 1]
        dn2g[i] = _stat_row(st2, 0) * (1.0 + S['sc2'])
        dsh2 = _stat_row(st2, 1)
        dg1 = _stat_row(st2, 2)
        if kind == 0:
            dp = _mm(dyb, w_mix[1], tb=True, name=f"conv_out_bwd{i}", out_dtypes=(BF16,))
            d_cwo = _wgrad(S['pb'], dyb, name=f"conv_w_out_grad{i}")
            dbcx, stc = _conv_bwd(dp, S['bcx'], S['wb'], name=f"conv_mix_bwd{i}")
            d_conv_w[j] = stc[0:3]
            d_conv_b[j] = stc[3:4]
            dh1 = _mm(dbcx, w_mix[0], tb=True, name=f"conv_in_bwd{i}", out_dtypes=(BF16,), bm=512)
            d_cwi = _wgrad(S['h1'], dbcx, name=f"conv_w_in_grad{i}")
            mix_grads = [scatter_item(d_cwi, 1), scatter_item(d_cwo, 0)]
        elif kind == 1:
            dy2 = _mm(dyb, w_mix[2], tb=True, name=f"ssm_out_bwd{i}")
            d_ssm_out = _wgrad(S['pb'], dyb, name=f"ssm_w_out_grad{i}")
            dtb, dya, stg = _glu_bwd(dy2, S['yg'], S['tt'], name=f"ssm_glu_bwd{i}")
            dypre = _mm(dtb, w_mix[1], tb=True, name=f"ssm_glu_in_bwd{i}",
                        epi=lambda acc, a, yp: ((a + acc) * _gelu_grad(yp),),
                        extras=[(dya, 'mn'), (S['ypre'], 'mn')])
            d_glu = _wgrad(S['yg'], dtb, name=f"ssm_glu_w_grad{i}", bm=512)
            dub, dbre, dbim, dcre, dcim, ga, dd = _s5_bwd(dypre, S['u'], S['sre'], S['sim'], *s5_w, pw_rev, pos_rev, ssm_d,
                                                           name=f"s5_scan_bwd{i}")
            dh1 = _mm(dub, w_mix[0], tb=True, name=f"ssm_in_bwd{i}", out_dtypes=(BF16,))
            d_ssm_in = _wgrad(S['h1'], dub, name=f"ssm_w_in_grad{i}")
            da_re, da_im, dlog_dt, db_re, db_im, dc_re, dc_im = s5_vjp((ga[0:1], ga[1:2], dbre, dbim, dcre, dcim))
            s5_small = _pack_rows([da_re, da_im, dlog_dt, db_re, db_im, dc_re, dc_im, dd[0], stg[0]])
            mix_grads = [scatter_item(d_ssm_in, 0), scatter_item(d_glu, 0), scatter_item(d_ssm_out, 0),
                         gather_item(s5_small, 0)]
        else:
            dp = _mm(dyb, w_mix[1], tb=True, name=f"sg_out_bwd{i}")
            d_sgo = _wgrad(S['pb'], dyb, name=f"sg_w_out_grad{i}")
            duv, dws, dbs, stv = _sg_bwd(dp, S['uv'], sg_vg_full, ws_b, wst_b, bsb, name=f"sg_mix_bwd{i}")
            dh1 = _mm(duv, w_mix[0], tb=True, name=f"sg_in_bwd{i}", out_dtypes=(BF16,), bm=512, bk=duv.shape[1])
            d_sgi = _wgrad(S['h1'], duv, name=f"sg_w_in_grad{i}")
            sg_small = _pack_rows([jnp.where(causal[None], dws, 0.0), jnp.sum(dbs, axis=-1)])
            d_sg_vg = stv[0:1]
            mix_grads = [scatter_item(d_sgi, 1), scatter_item(d_sgo, 0), gather_item(sg_small, 0)]
        wn1 = S['wn1']
        if i > 0:
            (mix_sent[i],), token = _exchange_start([mix_grads], dx_mid, name=f"mix_grads_start{i}")
            wn1 = wn1 + token[0:1, 0:1]
            prev = saved[i - 1]
            dx, st1, dfb = _normmod_bwd(dh1, S['x_in'], wn1, dx_mid, (prev['f_out'], prev['g2']),
                                        name=f"norm1_bwd{i}")
            dg2_next = _stat_row(st1, 2)
        else:
            dx, st1 = _normmod_bwd(dh1, S['x_in'], wn1, dx_mid, None, name=f"norm1_bwd{i}")
        dsc1 = _stat_row(st1, 0) * norm1_g[i:i + 1]
        dn1g[i] = _stat_row(st1, 0) * (1.0 + S['sc1'])
        dsh1 = _stat_row(st1, 1)
        dmod[i] = jnp.concatenate([dsh1, dsc1, dg1, dsh2, dsc2, dg2], 1)
    grad_x = dx[None]

    out = {}

    def small_group(names, parts, label):
        shapes = [P[n].shape for n in names]
        w, m, v = (_pack_rows([P[pre + n] for n in names])[None] for pre in ('', 'm_', 'v_'))
        res = [_unpack_rows(t[0], shapes) for t in _adamw(w, [parts], m, v, name=label)]
        for q, n in enumerate(names):
            out[n] = tuple(r[q] for r in res)

    small.update(ada_b=jnp.concatenate(dmod, 0), norm1_g=jnp.concatenate(dn1g, 0), norm2_g=jnp.concatenate(dn2g, 0),
                 final_g=d_final_g, conv_w=jnp.stack(d_conv_w), conv_b=jnp.concatenate(d_conv_b, 0), sg_v_g=d_sg_vg)
    last_pack = _pack_rows([small[n] for n in LAST_SMALL + SMALL_SHARD])
    n_last = _pack_rows([P[n] for n in LAST_SMALL]).shape[0]
    n_pack = last_pack.shape[0]
    pack_all = _all_gather(jnp.concatenate([last_pack, loss_tile], 0)[None], 0, name="gather_small_grads")
    loss = jnp.sum(pack_all[:, n_pack, 0])
    (mix_sent[0],), _ = _exchange_start([mix_grads], pack_all, name="mix_grads_start0")
    small_group(LAST_SMALL, pack_all[:, :n_last], "adamw_small")
    sh_rows = (n_pack - n_last) // N_DEV
    sh_parts = pack_all[:, n_last:n_pack].reshape(N_DEV, sh_rows, N_DEV, LANE)
    sh_parts = lax.dynamic_index_in_dim(sh_parts, me, 2, keepdims=False)
    sh_parts = jnp.pad(sh_parts, ((0, 0), (0, 16 - sh_rows), (0, 0)))

    def pack_shard(prefix):
        return _pad_rows(jnp.concatenate([P[prefix + n].reshape(-1, LANE) for n in SMALL_SHARD], 0), 16)[None]

    sg_, sd_, sm_, sv_ = _adamw(pack_shard(''), [sh_parts], pack_shard('m_'), pack_shard('v_'), name="adamw_channel")
    off = 0
    for n in SMALL_SHARD:
        rows = math.prod(P[n].shape) // LANE
        out[n] = tuple(t[0, off:off + rows].reshape(P[n].shape) for t in (sg_, sd_, sm_, sv_))
        off += rows

    dmod_all = pack_all[:, :DEPTH * 6 * D // LANE].reshape(N_DEV, DEPTH, 6 * D)
    dmod_cols = lax.dynamic_slice_in_dim(dmod_all, me * ncol, ncol, 2)
    g_ada = [_mm(c_pad, _pad_rows(dmod_cols[:, i], LANE), ta=True, name=f"ada_w_grad{i}")[None] for i in range(DEPTH)]

    def big(name, parts):
        res = _adamw(P[name], parts, P['m_' + name], P['v_' + name], name="adamw_" + name)
        out[name] = res
        return res[1]

    ff_parts = [_exchange_wait(ff_sent[i], dx, name=f"ff_grads_wait{i}") for i in range(DEPTH)]
    mix_parts = [None] + [_exchange_wait(mix_sent[i], dx, name=f"mix_grads_wait{i}") for i in range(1, DEPTH)]
    big('ada_w', g_ada)
    big('ff_w1', [p[0] for p in ff_parts])
    big('ff_w2', [p[1] for p in ff_parts])
    done = big('sg_w_in', [mix_parts[2][0]])
    mix_parts[0] = _exchange_wait(mix_sent[0], done, name="mix_grads_wait0")
    big('conv_w_in', [mix_parts[i][0] for i in range(DEPTH) if i % 3 == 0])
    row_names = ['conv_w_out', 'ssm_w_in', 'ssm_glu_w', 'ssm_w_out', 'sg_w_out']
    row_parts = ([mix_parts[i][1] for i in range(DEPTH) if i % 3 == 0] + mix_parts[1][:3] + [mix_parts[2][1]])
    small_group(S5_SMALL, mix_parts[1][3].reshape(N_DEV, -1, LANE), "adamw_s5")
    small_group(SG_SMALL, mix_parts[2][2].reshape(N_DEV, -1, LANE), "adamw_sg")
    row_w, row_m, row_v = (jnp.concatenate([P[pre + n] for n in row_names], 0) for pre in ('', 'm_', 'v_'))
    rw = _adamw(row_w, row_parts, row_m, row_v, name="adamw_row_sharded")
    off = 0
    for n in row_names:
        cnt = P[n].shape[0]
        out[n] = tuple(t[off:off + cnt] for t in rw)
        off += cnt

    return (loss, grad_x, *[out[n][0] for n in WEIGHTS], *[out[n][1] for n in WEIGHTS],
            *[out[n][2] for n in WEIGHTS], *[out[n][3] for n in WEIGHTS])
```

```python
import math

import jax
import jax.numpy as jnp
from jax import lax
from jax.experimental import pallas as pl
from jax.experimental.pallas import tpu as pltpu

F32 = jnp.float32
BF16 = jnp.bfloat16

N_DEV = 8
MESH_ID = pl.DeviceIdType.MESH
DEPTH = 4
EPS = 1e-6
S5_GROUPS, S5_GROUP, S5_STATE = 64, 16, 64
S5_LANES = S5_GROUPS * S5_STATE
S5_BLOCKS = 8
S5_CHUNK = 512
SG_HEADS, SG_CHUNK = 8, 128
LANE = 128
SUBLANE = 8
VMEM_LIMIT = 48 * 1024 * 1024
ADAM_LR, ADAM_B1, ADAM_B2, ADAM_EPS, ADAM_WD, ADAM_STEP = 0.001, 0.9, 0.999, 1e-08, 0.01, 10
GELU_C = math.sqrt(2.0 / math.pi)
GELU_A = 0.044715

WEIGHTS = ['ada_w', 'ada_b', 'norm1_g', 'norm2_g', 'ff_w1', 'ff_w2', 'final_g', 'conv_w_in', 'conv_w', 'conv_b',
           'conv_w_out', 'ssm_w_in', 'ssm_a_re', 'ssm_a_im', 'ssm_log_dt', 'ssm_b_re', 'ssm_b_im', 'ssm_c_re',
           'ssm_c_im', 'ssm_d', 'ssm_glu_w', 'ssm_glu_b', 'ssm_w_out', 'sg_w_in', 'sg_v_g', 'sg_w_s', 'sg_b_s',
           'sg_w_out']
INPUTS = ['x', 'c'] + WEIGHTS + ['loss_target'] + ['m_' + n for n in WEIGHTS] + ['v_' + n for n in WEIGHTS]
S5_SMALL = ['ssm_a_re', 'ssm_a_im', 'ssm_log_dt', 'ssm_b_re', 'ssm_b_im', 'ssm_c_re', 'ssm_c_im', 'ssm_d', 'ssm_glu_b']
SG_SMALL = ['sg_w_s', 'sg_b_s']
LAST_SMALL = ['ada_b', 'norm1_g', 'norm2_g', 'final_g']
SMALL_SHARD = ['conv_w', 'conv_b', 'sg_v_g']


def _params(*sem):
    return pltpu.CompilerParams(dimension_semantics=sem or None, vmem_limit_bytes=VMEM_LIMIT)


def _my_pos():
    return lax.axis_index("x"), lax.axis_index("y"), lax.axis_index("c")


def _my_index():
    x, y, c = _my_pos()
    return 4 * x + 2 * y + c


def _mm(a, b, *, name, ta=False, tb=False, out_dtypes=(F32,), epi=None, extras=(), a_fn=None, bm=1024, bn=1024,
        bk=1024):
    a_chunks = a.shape[0] if a.ndim == 3 else 0
    b_chunks = b.shape[0] if b.ndim == 3 else 0
    assert not (a_chunks and ta) and not (b_chunks and tb)
    if a_chunks:
        m, k = a.shape[1], a_chunks * a.shape[2]
        bk = k
    else:
        m, k = (a.shape[1], a.shape[0]) if ta else a.shape
    if b_chunks:
        k2, n = b.shape[1], b_chunks * b.shape[2]
        bn = min(bn, b.shape[2])
    else:
        k2, n = (b.shape[1], b.shape[0]) if tb else b.shape
    assert k == k2, (a.shape, b.shape, ta, tb)
    bm, bn, bk = min(bm, m), min(bn, n), min(bk, k)
    assert m % bm == 0 and n % bn == 0 and k % bk == 0, (m, n, k, bm, bn, bk)
    nk = k // bk
    n_ex, n_out = len(extras), len(out_dtypes)
    dims = (((0 if ta else 1,), (1 if tb else 0,)), ((), ()))

    def body(*refs):
        a_ref, b_ref = refs[0], refs[1]
        ex_refs = refs[2:2 + n_ex]
        out_refs = refs[2 + n_ex:2 + n_ex + n_out]

        def finish(acc):
            outs = epi(acc, *[r[...] for r in ex_refs]) if epi is not None else (acc,)
            for r, o in zip(out_refs, outs):
                r[...] = o.astype(r.dtype)

        av = jnp.concatenate([a_ref[t] for t in range(a_chunks)], axis=1) if a_chunks else a_ref[...]
        if a_fn is not None:
            av = a_fn(av)
        part = lax.dot_general(av.astype(BF16), b_ref[...].astype(BF16), dims, preferred_element_type=F32)
        if nk == 1:
            finish(part)
            return
        acc_ref = refs[-1]
        kk = pl.program_id(2)

        @pl.when(kk == 0)
        def _():
            acc_ref[...] = part

        @pl.when(kk > 0)
        def _():
            acc_ref[...] += part

        @pl.when(kk == nk - 1)
        def _():
            finish(acc_ref[...])

    if a_chunks:
        a_spec = pl.BlockSpec((a_chunks, bm, a.shape[2]), lambda i, j, q: (0, i, 0))
    elif ta:
        a_spec = pl.BlockSpec((bk, bm), lambda i, j, q: (q, i))
    else:
        a_spec = pl.BlockSpec((bm, bk), lambda i, j, q: (i, q))
    if b_chunks:
        per = b.shape[2] // bn
        b_spec = pl.BlockSpec((None, bk, bn), lambda i, j, q: (j // per, q, j % per))
    elif tb:
        b_spec = pl.BlockSpec((bn, bk), lambda i, j, q: (j, q))
    else:
        b_spec = pl.BlockSpec((bk, bn), lambda i, j, q: (q, j))
    ex_specs = []
    for arr, kind in extras:
        if kind == 'mn':
            assert arr.shape == (m, n), (arr.shape, m, n)
            ex_specs.append(pl.BlockSpec((bm, bn), lambda i, j, q: (i, j)))
        else:
            assert arr.shape == (1, n), (arr.shape, n)
            ex_specs.append(pl.BlockSpec((1, bn), lambda i, j, q: (0, j)))
    outs = pl.pallas_call(
        body, name=name,
        out_shape=tuple(jax.ShapeDtypeStruct((m, n), d) for d in out_dtypes),
        grid=(m // bm, n // bn, nk),
        in_specs=[a_spec, b_spec] + ex_specs,
        out_specs=tuple(pl.BlockSpec((bm, bn), lambda i, j, q: (i, j)) for _ in out_dtypes),
        scratch_shapes=[pltpu.VMEM((bm, bn), F32)] if nk > 1 else [],
        compiler_params=_params("parallel", "parallel", "arbitrary"),
    )(a, b, *[arr for arr, _ in extras])
    return outs if n_out > 1 else outs[0]


def _epi_residual(acc, res, gate):
    return res + gate * acc, acc


def _epi_residual_norm(acc, res, gate, w, sh):
    xn = res + gate * acc
    return xn, acc, xn * _rstd(xn) * w + sh


def _wgrad(acts, cots, *, name, a_fn=None, bm=1024, bn=512):
    return _mm(acts, cots, ta=True, name=name, out_dtypes=(BF16,), a_fn=a_fn, bm=bm, bn=bn, bk=acts.shape[0])


def _square(a):
    af = a.astype(F32)
    return af * af


def _rstd(xv):
    return lax.rsqrt(jnp.mean(xv * xv, axis=-1, keepdims=True) + EPS)


def _normmod_fwd(x, w, sh, *, name, tm=512):
    L, D = x.shape

    def body(x_ref, w_ref, s_ref, h_ref):
        xv = x_ref[...]
        h_ref[...] = (xv * _rstd(xv) * w_ref[...] + s_ref[...]).astype(h_ref.dtype)

    row = pl.BlockSpec((tm, D), lambda i: (i, 0))
    vec = pl.BlockSpec((1, D), lambda i: (0, 0))
    return pl.pallas_call(body, name=name, out_shape=jax.ShapeDtypeStruct((L, D), BF16), grid=(L // tm,),
                          in_specs=[row, vec, vec], out_specs=row, compiler_params=_params("parallel"))(x, w, sh)


def _normmod_bwd(dh, x, w, dres, gate, *, name, tm=256):
    L, D = x.shape
    has_gate = gate is not None

    def body(*refs):
        if has_gate:
            dh_ref, x_ref, w_ref, r_ref, y_ref, g_ref, dx_ref, st_ref, dy_ref = refs
        else:
            dh_ref, x_ref, w_ref, r_ref, dx_ref, st_ref = refs
        i = pl.program_id(0)

        @pl.when(i == 0)
        def _():
            st_ref[...] = jnp.zeros_like(st_ref)

        xv = x_ref[...]
        dhv = dh_ref[...].astype(F32)
        rstd = _rstd(xv)
        xn = xv * rstd
        dxn = dhv * w_ref[...]
        dx = rstd * (dxn - xn * jnp.mean(dxn * xn, axis=-1, keepdims=True)) + r_ref[...]
        dx_ref[...] = dx
        st_ref[0:1, :] += jnp.sum(dhv * xn, axis=0, keepdims=True)
        st_ref[1:2, :] += jnp.sum(dhv, axis=0, keepdims=True)
        if has_gate:
            dy_ref[...] = (dx * g_ref[...]).astype(dy_ref.dtype)
            st_ref[2:3, :] += jnp.sum(dx * y_ref[...].astype(F32), axis=0, keepdims=True)

    row = pl.BlockSpec((tm, D), lambda i: (i, 0))
    vec = pl.BlockSpec((1, D), lambda i: (0, 0))
    st = pl.BlockSpec((SUBLANE, D), lambda i: (0, 0))
    in_specs = [row, row, vec, row] + ([row, vec] if has_gate else [])
    out_shape = [jax.ShapeDtypeStruct((L, D), F32), jax.ShapeDtypeStruct((SUBLANE, D), F32)]
    out_specs = [row, st]
    if has_gate:
        out_shape.append(jax.ShapeDtypeStruct((L, D), BF16))
        out_specs.append(row)
    args = (dh, x, w, dres) + (tuple(gate) if has_gate else ())
    return pl.pallas_call(body, name=name, out_shape=tuple(out_shape), grid=(L // tm,), in_specs=in_specs,
                          out_specs=tuple(out_specs), compiler_params=_params("arbitrary"))(*args)


def _loss_head(x, tgt, fg, y, g, *, name, tm=256):
    L, D = x.shape

    def body(x_ref, t_ref, fg_ref, y_ref, g_ref, dx_ref, st_ref, dy_ref, loss_ref):
        i = pl.program_id(0)

        @pl.when(i == 0)
        def _():
            st_ref[...] = jnp.zeros_like(st_ref)
            loss_ref[...] = jnp.zeros_like(loss_ref)

        xv = x_ref[...]
        rstd = _rstd(xv)
        xn = xv * rstd
        err = xn * fg_ref[...] - t_ref[...]
        loss_ref[...] += 0.5 * jnp.sum(jnp.mean(err * err, axis=-1, keepdims=True))
        dout = err * (1.0 / D)
        dxn = dout * fg_ref[...]
        dx = rstd * (dxn - xn * jnp.mean(dxn * xn, axis=-1, keepdims=True))
        dx_ref[...] = dx
        dy_ref[...] = (dx * g_ref[...]).astype(dy_ref.dtype)
        st_ref[0:1, :] += jnp.sum(dout * xn, axis=0, keepdims=True)
        st_ref[2:3, :] += jnp.sum(dx * y_ref[...].astype(F32), axis=0, keepdims=True)

    row = pl.BlockSpec((tm, D), lambda i: (i, 0))
    vec = pl.BlockSpec((1, D), lambda i: (0, 0))
    return pl.pallas_call(
        body, name=name,
        out_shape=(jax.ShapeDtypeStruct((L, D), F32), jax.ShapeDtypeStruct((SUBLANE, D), F32),
                   jax.ShapeDtypeStruct((L, D), BF16), jax.ShapeDtypeStruct((SUBLANE, LANE), F32)),
        grid=(L // tm,), in_specs=[row, row, vec, row, vec],
        out_specs=(row, pl.BlockSpec((SUBLANE, D), lambda i: (0, 0)), row,
                   pl.BlockSpec((SUBLANE, LANE), lambda i: (0, 0))),
        compiler_params=_params("arbitrary"))(x, tgt, fg, y, g)


def _shift_down(v, k):
    row = lax.broadcasted_iota(jnp.int32, v.shape, 0)
    return jnp.where(row >= k, pltpu.roll(v, k, 0), 0.0)


def _shift_up(v, k):
    n = v.shape[0]
    row = lax.broadcasted_iota(jnp.int32, v.shape, 0)
    return jnp.where(row < n - k, pltpu.roll(v, n - k, 0), 0.0)


def _conv_views(L, D):
    return [pl.BlockSpec((L, LANE), lambda j, s=s: (0, s * (D // LANE) + j)) for s in range(3)]


def _conv_fwd(bcx, wb, *, name):
    L, D = bcx.shape[0], bcx.shape[1] // 3

    def body(b_ref, c_ref, x_ref, wb_ref, p_ref):
        z = c_ref[...].astype(F32) * x_ref[...].astype(F32)
        conv = (wb_ref[0:1, :] * _shift_down(z, 2) + wb_ref[1:2, :] * _shift_down(z, 1)
                + wb_ref[2:3, :] * z + wb_ref[3:4, :])
        p_ref[...] = (b_ref[...].astype(F32) * conv).astype(p_ref.dtype)

    col = pl.BlockSpec((L, LANE), lambda j: (0, j))
    return pl.pallas_call(body, name=name, out_shape=jax.ShapeDtypeStruct((L, D), BF16), grid=(D // LANE,),
                          in_specs=_conv_views(L, D) + [pl.BlockSpec((SUBLANE, LANE), lambda j: (0, j))],
                          out_specs=col, compiler_params=_params("parallel"))(bcx, bcx, bcx, wb)


def _conv_bwd(dp, bcx, wb, *, name):
    L, D = dp.shape

    def body(dp_ref, b_ref, c_ref, x_ref, wb_ref, d3_ref, st_ref):
        cv, xv = c_ref[...].astype(F32), x_ref[...].astype(F32)
        z = cv * xv
        z1, z2 = _shift_down(z, 1), _shift_down(z, 2)
        w0, w1, w2 = wb_ref[0:1, :], wb_ref[1:2, :], wb_ref[2:3, :]
        conv = w0 * z2 + w1 * z1 + w2 * z + wb_ref[3:4, :]
        dpv = dp_ref[...].astype(F32)
        d3_ref[0] = (dpv * conv).astype(d3_ref.dtype)
        dconv = dpv * b_ref[...].astype(F32)
        dz = w2 * dconv + w1 * _shift_up(dconv, 1) + w0 * _shift_up(dconv, 2)
        d3_ref[1] = (dz * xv).astype(d3_ref.dtype)
        d3_ref[2] = (dz * cv).astype(d3_ref.dtype)
        st_ref[...] = jnp.zeros_like(st_ref)
        st_ref[0:1, :] = jnp.sum(dconv * z2, axis=0, keepdims=True)
        st_ref[1:2, :] = jnp.sum(dconv * z1, axis=0, keepdims=True)
        st_ref[2:3, :] = jnp.sum(dconv * z, axis=0, keepdims=True)
        st_ref[3:4, :] = jnp.sum(dconv, axis=0, keepdims=True)

    col = pl.BlockSpec((L, LANE), lambda j: (0, j))
    vec = pl.BlockSpec((SUBLANE, LANE), lambda j: (0, j))
    return pl.pallas_call(body, name=name,
                          out_shape=(jax.ShapeDtypeStruct((3, L, D), BF16), jax.ShapeDtypeStruct((SUBLANE, D), F32)),
                          grid=(D // LANE,), in_specs=[col] + _conv_views(L, D) + [vec],
                          out_specs=(pl.BlockSpec((3, L, LANE), lambda j: (0, 0, j)), vec),
                          compiler_params=_params("parallel"))(dp, bcx, bcx, bcx, wb)


def _sg_fwd(uv, vg, ws, bsb, *, name, tr=512):
    L, D = uv.shape[0], uv.shape[1] // 2

    def body(uv_ref, vg_ref, ws_ref, bsb_ref, p_ref):
        for ci in range(tr // SG_CHUNK):
            rows = slice(ci * SG_CHUNK, (ci + 1) * SG_CHUNK)
            v = uv_ref[rows, D:2 * D]
            vn = (v * _rstd(v) * vg_ref[...]).astype(BF16)
            for h in range(SG_HEADS):
                cols = slice(h * LANE, (h + 1) * LANE)
                vm = jnp.dot(ws_ref[h], vn[:, cols], preferred_element_type=F32) + bsb_ref[h]
                p_ref[rows, cols] = (uv_ref[rows, cols] * vm).astype(p_ref.dtype)

    full3 = pl.BlockSpec((SG_HEADS, SG_CHUNK, LANE), lambda i: (0, 0, 0))
    return pl.pallas_call(body, name=name, out_shape=jax.ShapeDtypeStruct((L, D), BF16), grid=(L // tr,),
                          in_specs=[pl.BlockSpec((tr, 2 * D), lambda i: (i, 0)), pl.BlockSpec((1, D), lambda i: (0, 0)),
                                    full3, full3],
                          out_specs=pl.BlockSpec((tr, D), lambda i: (i, 0)),
                          compiler_params=_params("parallel"))(uv, vg, ws, bsb)


def _sg_bwd(dp, uv, vg, ws, wst, bsb, *, name, tr=512):
    L, D = dp.shape

    def body(dp_ref, uv_ref, vg_ref, ws_ref, wst_ref, bsb_ref, duv_ref, dws_ref, dbs_ref, st_ref, dvn_ref):
        i = pl.program_id(0)

        @pl.when(i == 0)
        def _():
            dws_ref[...] = jnp.zeros_like(dws_ref)
            dbs_ref[...] = jnp.zeros_like(dbs_ref)
            st_ref[...] = jnp.zeros_like(st_ref)

        for ci in range(tr // SG_CHUNK):
            rows = slice(ci * SG_CHUNK, (ci + 1) * SG_CHUNK)
            v = uv_ref[rows, D:2 * D]
            rstd = _rstd(v)
            vhat = v * rstd
            vn = (vhat * vg_ref[...]).astype(BF16)
            for h in range(SG_HEADS):
                cols = slice(h * LANE, (h + 1) * LANE)
                vm = jnp.dot(ws_ref[h], vn[:, cols], preferred_element_type=F32) + bsb_ref[h]
                dph = dp_ref[rows, cols]
                duv_ref[rows, cols] = (dph * vm).astype(duv_ref.dtype)
                dvm = dph * uv_ref[rows, cols]
                dbs_ref[h] += dvm
                dvmb = dvm.astype(BF16)
                dws_ref[h] += lax.dot_general(dvmb, vn[:, cols], (((1,), (1,)), ((), ())),
                                              preferred_element_type=F32)
                dvn_ref[rows, cols] = jnp.dot(wst_ref[h], dvmb, preferred_element_type=F32)
            dvn = dvn_ref[rows, :]
            gv = dvn * vg_ref[...]
            dv = rstd * (gv - vhat * jnp.mean(gv * vhat, axis=-1, keepdims=True))
            duv_ref[rows, D:2 * D] = dv.astype(duv_ref.dtype)
            st_ref[0:1, :] += jnp.sum(dvn * vhat, axis=0, keepdims=True)

    full3 = pl.BlockSpec((SG_HEADS, SG_CHUNK, LANE), lambda i: (0, 0, 0))
    acc3 = jax.ShapeDtypeStruct((SG_HEADS, SG_CHUNK, LANE), F32)
    return pl.pallas_call(
        body, name=name,
        out_shape=(jax.ShapeDtypeStruct((L, 2 * D), BF16), acc3, acc3, jax.ShapeDtypeStruct((SUBLANE, D), F32)),
        grid=(L // tr,),
        in_specs=[pl.BlockSpec((tr, D), lambda i: (i, 0)), pl.BlockSpec((tr, 2 * D), lambda i: (i, 0)),
                  pl.BlockSpec((1, D), lambda i: (0, 0)), full3, full3, full3],
        out_specs=(pl.BlockSpec((tr, 2 * D), lambda i: (i, 0)), full3, full3,
                   pl.BlockSpec((SUBLANE, D), lambda i: (0, 0))),
        scratch_shapes=[pltpu.VMEM((tr, D), F32)],
        compiler_params=_params("arbitrary"))(dp, uv, vg, ws, wst, bsb)


def _gelu(x):
    return 0.5 * x * (1.0 + jnp.tanh(GELU_C * (x + GELU_A * x * x * x)))


def _gelu_grad(x):
    th = jnp.tanh(GELU_C * (x + GELU_A * x * x * x))
    return 0.5 * (1.0 + th) + 0.5 * x * (1.0 - th * th) * GELU_C * (1.0 + 3.0 * GELU_A * x * x)


def _cmul_add(xr, xi, ar, ai, br, bi):
    return xr + ar * br - ai * bi, xi + ar * bi + ai * br


def _to_subchunk_order(src_ref, dst_ref, n):
    for k in range(n):
        dst_ref[pl.ds(SUBLANE * k, SUBLANE), :] = src_ref[pl.ds(k, SUBLANE, stride=n), :].astype(dst_ref.dtype)


def _to_time_order(src_ref, dst_ref, n):
    for m in range(n):
        r, k = divmod(SUBLANE * m, n)
        dst_ref[pl.ds(SUBLANE * m, SUBLANE), :] = src_ref[pl.ds(SUBLANE * k + r, SUBLANE, stride=SUBLANE), :]


def _s5_fwd(u, bre, bim, cre, cim, pw, pos, dsk, *, name, tc=S5_CHUNK):
    L, D = u.shape
    W = S5_LANES // S5_BLOCKS
    nt = L // tc
    n = tc // SUBLANE

    def sub(k):
        return pl.ds(SUBLANE * k, SUBLANE)

    def body(u_ref, bre_ref, bim_ref, cre_ref, cim_ref, pw_ref, pos_ref, d_ref, sre_ref, sim_ref, ypre_ref, yg_ref,
             carry, up, yp):
        t = pl.program_id(1)

        @pl.when(t == 0)
        def _():
            carry[...] = jnp.zeros_like(carry)

        _to_subchunk_order(u_ref, up, n)
        uv = up[...]
        ub = uv.astype(BF16)
        sre_ref[...] = jnp.dot(ub, bre_ref[...], preferred_element_type=F32)
        sim_ref[...] = jnp.dot(ub, bim_ref[...], preferred_element_type=F32)

        ar, ai = pw_ref[8], pw_ref[9]
        xr = jnp.zeros((SUBLANE, W), F32)
        xi = jnp.zeros((SUBLANE, W), F32)
        for k in range(n):
            xr, xi = _cmul_add(sre_ref[sub(k), :], sim_ref[sub(k), :], ar, ai, xr, xi)
            sre_ref[sub(k), :] = xr
            sim_ref[sub(k), :] = xi
        for q, d in enumerate((1, 2, 4)):
            xr, xi = _cmul_add(xr, xi, pw_ref[2 * q], pw_ref[2 * q + 1], pltpu.roll(xr, d, 0), pltpu.roll(xi, d, 0))
        cr, ci = carry[0], carry[1]
        xr, xi = _cmul_add(xr, xi, pw_ref[6], pw_ref[7], cr, ci)
        first = lax.broadcasted_iota(jnp.int32, (SUBLANE, W), 0) == 0
        er = jnp.where(first, cr, pltpu.roll(xr, 1, 0))
        ei = jnp.where(first, ci, pltpu.roll(xi, 1, 0))
        last = slice(SUBLANE - 1, SUBLANE)
        carry[0] = jnp.broadcast_to(xr[last, :], (SUBLANE, W))
        carry[1] = jnp.broadcast_to(xi[last, :], (SUBLANE, W))
        for k in range(n):
            sr, si = _cmul_add(sre_ref[sub(k), :], sim_ref[sub(k), :], pos_ref[0, k:k + 1, :], pos_ref[1, k:k + 1, :],
                               er, ei)
            sre_ref[sub(k), :] = sr
            sim_ref[sub(k), :] = si
        yp[...] = (jnp.dot(sre_ref[...].astype(BF16), cre_ref[...], preferred_element_type=F32)
                   - jnp.dot(sim_ref[...].astype(BF16), cim_ref[...], preferred_element_type=F32) + d_ref[...] * uv)
        _to_time_order(yp, ypre_ref, n)
        yg_ref[...] = _gelu(ypre_ref[...])

    ch = pl.BlockSpec((tc, LANE), lambda j, t: (t, j))
    st = pl.BlockSpec((tc, W), lambda j, t: (t, j))
    bsp = pl.BlockSpec((None, LANE, W), lambda j, t: (j, 0, 0))
    csp = pl.BlockSpec((None, W, LANE), lambda j, t: (j, 0, 0))
    return pl.pallas_call(
        body, name=name,
        out_shape=(jax.ShapeDtypeStruct((L, S5_LANES), F32), jax.ShapeDtypeStruct((L, S5_LANES), F32),
                   jax.ShapeDtypeStruct((L, D), F32), jax.ShapeDtypeStruct((L, D), F32)),
        grid=(S5_BLOCKS, nt),
        in_specs=[ch, bsp, bsp, csp, csp, pl.BlockSpec((10, SUBLANE, W), lambda j, t: (0, 0, j)),
                  pl.BlockSpec((2, n, W), lambda j, t: (0, 0, j)), pl.BlockSpec((1, LANE), lambda j, t: (0, j))],
        out_specs=(st, st, ch, ch),
        scratch_shapes=[pltpu.VMEM((2, SUBLANE, W), F32), pltpu.VMEM((tc, LANE), F32), pltpu.VMEM((tc, LANE), F32)],
        compiler_params=_params("parallel", "arbitrary"))(u, bre, bim, cre, cim, pw, pos, dsk)


def _s5_bwd(dy, u, sre, sim, bre, bim, cre, cim, pwr, posr, dsk, *, name, tc=S5_CHUNK):
    L, D = u.shape
    W = S5_LANES // S5_BLOCKS
    nt = L // tc
    n = tc // SUBLANE
    nt_dims = (((1,), (1,)), ((), ()))
    tn_dims = (((0,), (0,)), ((), ()))

    def sub(k):
        return pl.ds(SUBLANE * k, SUBLANE)

    def body(dy_ref, u_ref, sre_ref, sim_ref, bre_ref, bim_ref, cre_ref, cim_ref, pw_ref, pos_ref, d_ref,
             du_ref, dbre_ref, dbim_ref, dcre_ref, dcim_ref, ga_ref, dd_ref, gre, gim, carry, gacc, up, dyp):
        t = pl.program_id(1)

        @pl.when(t == 0)
        def _():
            for r in (carry, gacc, dbre_ref, dbim_ref, dcre_ref, dcim_ref, ga_ref, dd_ref):
                r[...] = jnp.zeros_like(r)

        _to_subchunk_order(dy_ref, dyp, n)
        _to_subchunk_order(u_ref, up, n)
        dyv, uv = dyp[...], up[...]
        dyb, ub = dyv.astype(BF16), uv.astype(BF16)
        gre[...] = lax.dot_general(dyb, cre_ref[...], nt_dims, preferred_element_type=F32)
        gim[...] = -lax.dot_general(dyb, cim_ref[...], nt_dims, preferred_element_type=F32)
        br, bi = pw_ref[8], pw_ref[9]
        xr = jnp.zeros((SUBLANE, W), F32)
        xi = jnp.zeros((SUBLANE, W), F32)
        for k in reversed(range(n)):
            xr, xi = _cmul_add(gre[sub(k), :], gim[sub(k), :], br, bi, xr, xi)
            gre[sub(k), :] = xr
            gim[sub(k), :] = xi
        for q, d in enumerate((1, 2, 4)):
            xr, xi = _cmul_add(xr, xi, pw_ref[2 * q], pw_ref[2 * q + 1], pltpu.roll(xr, SUBLANE - d, 0),
                               pltpu.roll(xi, SUBLANE - d, 0))
        cr, ci = carry[0], carry[1]
        xr, xi = _cmul_add(xr, xi, pw_ref[6], pw_ref[7], cr, ci)
        top = lax.broadcasted_iota(jnp.int32, (SUBLANE, W), 0) == SUBLANE - 1
        er = jnp.where(top, cr, pltpu.roll(xr, SUBLANE - 1, 0))
        ei = jnp.where(top, ci, pltpu.roll(xi, SUBLANE - 1, 0))
        carry[0] = jnp.broadcast_to(xr[0:1, :], (SUBLANE, W))
        carry[1] = jnp.broadcast_to(xi[0:1, :], (SUBLANE, W))
        nr, ni = er, ei
        acc_r = jnp.zeros((SUBLANE, W), F32)
        acc_i = jnp.zeros((SUBLANE, W), F32)
        for k in reversed(range(n)):
            gr, gi = _cmul_add(gre[sub(k), :], gim[sub(k), :], pos_ref[0, k:k + 1, :], pos_ref[1, k:k + 1, :], er, ei)
            gre[sub(k), :] = gr
            gim[sub(k), :] = gi
            sr, si = sre_ref[sub(k), :], sim_ref[sub(k), :]
            acc_r = acc_r + sr * nr + si * ni
            acc_i = acc_i + sr * ni - si * nr
            nr, ni = gr, gi
        gacc[0] += acc_r
        gacc[1] += acc_i
        grb, gib = gre[...].astype(BF16), gim[...].astype(BF16)
        dyp[...] = (lax.dot_general(grb, bre_ref[...], nt_dims, preferred_element_type=F32)
                    + lax.dot_general(gib, bim_ref[...], nt_dims, preferred_element_type=F32) + d_ref[...] * dyv)
        _to_time_order(dyp, up, n)
        du_ref[...] = up[...].astype(du_ref.dtype)
        dbre_ref[...] += lax.dot_general(ub, grb, tn_dims, preferred_element_type=F32)
        dbim_ref[...] += lax.dot_general(ub, gib, tn_dims, preferred_element_type=F32)
        dcre_ref[...] += lax.dot_general(sre_ref[...].astype(BF16), dyb, tn_dims, preferred_element_type=F32)
        dcim_ref[...] -= lax.dot_general(sim_ref[...].astype(BF16), dyb, tn_dims, preferred_element_type=F32)
        dd_ref[0:1, :] += jnp.sum(dyv * uv, axis=0, keepdims=True)

        @pl.when(t == nt - 1)
        def _():
            ga_ref[0:1, :] = jnp.sum(gacc[0], axis=0, keepdims=True)
            ga_ref[1:2, :] = jnp.sum(gacc[1], axis=0, keepdims=True)

    ch = pl.BlockSpec((tc, LANE), lambda j, t: (nt - 1 - t, j))
    st = pl.BlockSpec((tc, W), lambda j, t: (nt - 1 - t, j))
    bsp = pl.BlockSpec((None, LANE, W), lambda j, t: (j, 0, 0))
    csp = pl.BlockSpec((None, W, LANE), lambda j, t: (j, 0, 0))
    return pl.pallas_call(
        body, name=name,
        out_shape=(jax.ShapeDtypeStruct((L, D), BF16),
                   jax.ShapeDtypeStruct((S5_BLOCKS, LANE, W), F32), jax.ShapeDtypeStruct((S5_BLOCKS, LANE, W), F32),
                   jax.ShapeDtypeStruct((S5_BLOCKS, W, LANE), F32), jax.ShapeDtypeStruct((S5_BLOCKS, W, LANE), F32),
                   jax.ShapeDtypeStruct((SUBLANE, S5_LANES), F32), jax.ShapeDtypeStruct((SUBLANE, D), F32)),
        grid=(S5_BLOCKS, nt),
        in_specs=[ch, ch, st, st, bsp, bsp, csp, csp, pl.BlockSpec((10, SUBLANE, W), lambda j, t: (0, 0, j)),
                  pl.BlockSpec((2, n, W), lambda j, t: (0, 0, j)), pl.BlockSpec((1, LANE), lambda j, t: (0, j))],
        out_specs=(ch, bsp, bsp, csp, csp, pl.BlockSpec((SUBLANE, W), lambda j, t: (0, j)),
                   pl.BlockSpec((SUBLANE, LANE), lambda j, t: (0, j))),
        scratch_shapes=[pltpu.VMEM((tc, W), F32), pltpu.VMEM((tc, W), F32), pltpu.VMEM((2, SUBLANE, W), F32),
                        pltpu.VMEM((2, SUBLANE, W), F32), pltpu.VMEM((tc, LANE), F32), pltpu.VMEM((tc, LANE), F32)],
        compiler_params=_params("parallel", "arbitrary"))(dy, u, sre, sim, bre, bim, cre, cim, pwr, posr, dsk)


def _glu_bwd(dy2, y, t, *, name, tm=256):
    L, D = y.shape

    def body(dy2_ref, y_ref, t_ref, dt_ref, dya_ref, st_ref):
        i = pl.program_id(0)

        @pl.when(i == 0)
        def _():
            st_ref[...] = jnp.zeros_like(st_ref)

        sig = 1.0 / (1.0 + jnp.exp(-t_ref[...]))
        dy2v = dy2_ref[...]
        dt = dy2v * y_ref[...] * sig * (1.0 - sig)
        dt_ref[...] = dt.astype(dt_ref.dtype)
        dya_ref[...] = dy2v * sig
        st_ref[0:1, :] += jnp.sum(dt, axis=0, keepdims=True)

    row = pl.BlockSpec((tm, D), lambda i: (i, 0))
    return pl.pallas_call(
        body, name=name,
        out_shape=(jax.ShapeDtypeStruct((L, D), BF16), jax.ShapeDtypeStruct((L, D), F32),
                   jax.ShapeDtypeStruct((SUBLANE, D), F32)),
        grid=(L // tm,), in_specs=[row, row, row],
        out_specs=(row, row, pl.BlockSpec((SUBLANE, D), lambda i: (0, 0))),
        compiler_params=_params("arbitrary"))(dy2, y, t)


def _s5_prep(a_re, a_im, log_dt, b_re, b_im, c_re, c_im):
    dt = jnp.exp(log_dt)[:, None]
    mag = jnp.exp(a_re * dt)
    abar_re = mag * jnp.cos(a_im * dt)
    abar_im = mag * jnp.sin(a_im * dt)
    den = a_re * a_re + a_im * a_im
    nr = abar_re - 1.0
    ni = abar_im
    f_re = ((nr * a_re + ni * a_im) / den)[..., None]
    f_im = ((ni * a_re - nr * a_im) / den)[..., None]
    bbar_re = f_re * b_re - f_im * b_im
    bbar_im = f_re * b_im + f_im * b_re
    eye = jnp.eye(S5_GROUPS // S5_BLOCKS, dtype=F32)
    gb = S5_GROUPS // S5_BLOCKS

    def blk_b(bb):
        t = bb.reshape(S5_BLOCKS, gb, S5_STATE, S5_GROUP)
        return jnp.einsum('jgph,gk->jghkp', t, eye).reshape(S5_BLOCKS, gb * S5_GROUP, gb * S5_STATE)

    def blk_c(cc):
        t = cc.reshape(S5_BLOCKS, gb, S5_GROUP, S5_STATE)
        return jnp.einsum('jghp,gk->jgpkh', t, eye).reshape(S5_BLOCKS, gb * S5_STATE, gb * S5_GROUP)

    return (abar_re.reshape(1, S5_LANES), abar_im.reshape(1, S5_LANES), blk_b(bbar_re), blk_b(bbar_im),
            blk_c(c_re), blk_c(c_im))


def _cpowers(ar, ai, count):
    pr, pi, m = ar, ai, 1
    while m < count:
        tr, ti = pr[m - 1:m], pi[m - 1:m]
        pr, pi = jnp.concatenate([pr, pr * tr - pi * ti], 0), jnp.concatenate([pi, pr * ti + pi * tr], 0)
        m *= 2
    return pr, pi


def _s5_power_tables(ar, ai, n):
    pr, pi = _cpowers(ar, ai, n)
    qr, qi = _cpowers(pr[n - 1:n], pi[n - 1:n], SUBLANE)
    row = jnp.arange(SUBLANE)[:, None]
    lanes = ar.shape[1]

    def tables(sign, keep, flip):
        out = []
        for d in (1, 2, 4):
            out += [jnp.where(keep(d), qr[d - 1:d], 0.0), jnp.where(keep(d), sign * qi[d - 1:d], 0.0)]
        out += [flip(qr), sign * flip(qi), ar, sign * ai]
        pw = jnp.stack([jnp.broadcast_to(o, (SUBLANE, lanes)) for o in out])
        return pw, jnp.stack([flip(pr), sign * flip(pi)])

    fwd = tables(1.0, lambda d: row >= d, lambda t: t)
    rev = tables(-1.0, lambda d: row + d <= SUBLANE - 1, lambda t: t[::-1])
    return fwd, rev


ADAMW_PART_BLOCK_BYTES = 2 * 1024 * 1024


def _adamw(w, parts, m, v, *, name):
    n, R, C = w.shape
    assert len(parts) == n
    P = parts[0].shape[0]
    tr = R
    while P * tr * C * parts[0].dtype.itemsize > ADAMW_PART_BLOCK_BYTES and tr % 16 == 0:
        tr //= 2
    c1 = 1.0 / (1.0 - ADAM_B1 ** ADAM_STEP)
    c2 = 1.0 / (1.0 - ADAM_B2 ** ADAM_STEP)

    def body(*refs):
        w_ref, m_ref, v_ref = refs[:3]
        p_refs = refs[3:3 + n]
        g_ref, d_ref, nm_ref, nv_ref = refs[3 + n:]
        layer = pl.program_id(0)
        for q, p_ref in enumerate(p_refs):
            @pl.when(layer == q)
            def _(p_ref=p_ref):
                g = p_ref[0].astype(F32)
                for s in range(1, P):
                    g = g + p_ref[s].astype(F32)
                nm = ADAM_B1 * m_ref[...] + (1.0 - ADAM_B1) * g
                nv = ADAM_B2 * v_ref[...] + (1.0 - ADAM_B2) * (g * g)
                g_ref[...] = g
                nm_ref[...] = nm
                nv_ref[...] = nv
                d_ref[...] = -ADAM_LR * ((nm * c1) / (jnp.sqrt(nv * c2) + ADAM_EPS) + ADAM_WD * w_ref[...])

    row = pl.BlockSpec((None, tr, C), lambda l, i: (l, i, 0))
    part_specs = [pl.BlockSpec((P, tr, C), lambda l, i, q=q: (0, jnp.where(l == q, i, 0), 0)) for q in range(n)]
    out = jax.ShapeDtypeStruct((n, R, C), F32)
    return pl.pallas_call(body, name=name, out_shape=(out, out, out, out), grid=(n, R // tr),
                          in_specs=[row, row, row] + part_specs, out_specs=(row, row, row, row),
                          compiler_params=_params("arbitrary", "arbitrary"))(w, m, v, *parts)


def _all_gather(xs, axis, *, name):
    m = xs.shape[axis]
    out_shape = list(xs.shape)
    out_shape[axis] = N_DEV * m

    def body(x_ref, out_ref, send_sems, recv_sems, local_sem):
        x, y, c = _my_pos()
        me, sibling = (x, y, c), (x, y, 1 - c)
        chips = [(1 - x, y), (x, 1 - y), (1 - x, 1 - y)]

        def blk(px, py, pc):
            idx = [slice(None)] * 3
            idx[axis] = pl.ds((4 * px + 2 * py + pc) * m, m)
            return out_ref.at[tuple(idx)]

        def copy(k, block, to, src=None):
            return pltpu.make_async_remote_copy(src_ref=blk(*block) if src is None else src, dst_ref=blk(*block),
                                                send_sem=send_sems.at[k], recv_sem=recv_sems.at[k],
                                                device_id=to, device_id_type=MESH_ID)

        mine = pltpu.make_async_copy(x_ref, blk(*me), local_sem)
        mine.start()
        first = [copy(0, me, sibling, src=x_ref)]
        first += [copy(1 + j, me, (*chip, c), src=x_ref) for j, chip in enumerate(chips)]
        for cp in first:
            cp.start()
        passed = [copy(4 + j, (*chip, c), sibling) for j, chip in enumerate(chips)]
        for j, chip in enumerate(chips):
            copy(1 + j, (*chip, c), me).wait_recv()
            passed[j].start()
        copy(0, sibling, me).wait_recv()
        for j, chip in enumerate(chips):
            copy(4 + j, (*chip, 1 - c), me).wait_recv()
        for cp in first + passed:
            cp.wait_send()
        mine.wait()

    hbm = pl.BlockSpec(memory_space=pl.ANY)
    return pl.pallas_call(body, name=name, out_shape=jax.ShapeDtypeStruct(tuple(out_shape), xs.dtype),
                          in_specs=[hbm], out_specs=hbm,
                          scratch_shapes=[pltpu.SemaphoreType.DMA((N_DEV - 1,)), pltpu.SemaphoreType.DMA((N_DEV - 1,)),
                                          pltpu.SemaphoreType.DMA],
                          compiler_params=pltpu.CompilerParams(has_side_effects=True))(xs)


def _block(ref, axis, idx, m):
    return ref.at[pl.ds(idx * m, m), :] if axis == 0 else ref.at[:, pl.ds(idx * m, m)]


def _exchange_copies(metas, src_refs, zone_refs, send_sems, recv_sems, group):
    x, y, c = _my_pos()
    me = 4 * x + 2 * y + c
    base = group * N_DEV
    pairs = []
    for r in range(1, N_DEV):
        pos = (1 - x if r & 4 else x, 1 - y if r & 2 else y, 1 - c if r & 1 else c)
        peer = 4 * pos[0] + 2 * pos[1] + pos[2]
        for (kind, axis, m), s_ref, z_ref in zip(metas, src_refs, zone_refs):
            if kind == 'gather':
                src, dst, arrival = s_ref, _block(z_ref, axis, me, m), _block(z_ref, axis, peer, m)
            else:
                src, dst, arrival = _block(s_ref, axis, peer, m), z_ref.at[me], z_ref.at[peer]
            pairs.append(tuple(
                pltpu.make_async_remote_copy(src_ref=src, dst_ref=d, send_sem=send_sems.at[base + r - 1],
                                             recv_sem=recv_sems.at[base + r - 1], device_id=pos,
                                             device_id_type=MESH_ID)
                for d in (dst, arrival)))
    own = []
    for (kind, axis, m), s_ref, z_ref in zip(metas, src_refs, zone_refs):
        src, dst = (s_ref, _block(z_ref, axis, me, m)) if kind == 'gather' else (_block(s_ref, axis, me, m), z_ref.at[me])
        own.append(pltpu.make_async_copy(src, dst, recv_sems.at[base + N_DEV - 1]))
    return pairs, own


def _exchange_start(groups, after, *, name):
    flat = [it for g in groups for it in g]
    n, ng = len(flat), len(groups)
    metas = [it[2] for it in flat]
    bounds = [(sum(len(g) for g in groups[:q]), sum(len(g) for g in groups[:q + 1])) for q in range(ng)]

    def body(*refs):
        src_refs = refs[:n]
        send_sems, recv_sems = refs[n + 1], refs[n + 2]
        zone_refs = refs[2 * n + 3:3 * n + 3]
        token = refs[-1]
        for q, (lo, hi) in enumerate(bounds):
            pairs, own = _exchange_copies(metas[lo:hi], src_refs[lo:hi], zone_refs[lo:hi], send_sems, recv_sems, q)
            for outgoing, _ in pairs:
                outgoing.start()
            for cp in own:
                cp.start()
        token[...] = jnp.zeros_like(token)

    hbm = pl.BlockSpec(memory_space=pltpu.HBM)
    sem = pl.BlockSpec(memory_space=pltpu.SEMAPHORE)
    srcs = [it[0] for it in flat]
    res = pl.pallas_call(
        body, name=name,
        out_shape=(pltpu.SemaphoreType.DMA((ng * N_DEV,)), pltpu.SemaphoreType.DMA((ng * N_DEV,)),
                   *[pltpu.HBM(a.shape, a.dtype) for a in srcs], *[pltpu.HBM(it[1], it[0].dtype) for it in flat],
                   jax.ShapeDtypeStruct((SUBLANE, LANE), F32)),
        in_specs=[hbm] * n + [pl.BlockSpec(memory_space=pl.ANY)],
        out_specs=(sem, sem, *[hbm] * (2 * n), pl.BlockSpec(memory_space=pltpu.VMEM)),
        input_output_aliases={q: 2 + q for q in range(n)},
        compiler_params=pltpu.CompilerParams(has_side_effects=pltpu.SideEffectType.DATAFLOW_SIDE_EFFECTING),
    )(*[pltpu.with_memory_space_constraint(a, pltpu.HBM) for a in srcs], after)
    handles = [(res[0], res[1], q, list(res[2 + lo:2 + hi]), list(res[2 + n + lo:2 + n + hi]), metas[lo:hi])
               for q, (lo, hi) in enumerate(bounds)]
    return handles, res[-1]


def _exchange_wait(handle, after, *, name):
    send_sems, recv_sems, group, srcs, zones, metas = handle
    n = len(srcs)

    def body(*refs):
        src_refs, zone_refs = refs[:n], refs[n:2 * n]
        s_sems, r_sems = refs[2 * n], refs[2 * n + 1]
        pairs, own = _exchange_copies(metas, src_refs, zone_refs, s_sems, r_sems, group)
        for outgoing, incoming in pairs:
            outgoing.wait_send()
            incoming.wait_recv()
        for cp in own:
            cp.wait()

    hbm = pl.BlockSpec(memory_space=pltpu.HBM)
    sem = pl.BlockSpec(memory_space=pltpu.SEMAPHORE)
    arrays = srcs + zones
    res = pl.pallas_call(
        body, name=name,
        out_shape=tuple(pltpu.HBM(a.shape, a.dtype) for a in arrays),
        in_specs=[hbm] * (2 * n) + [sem, sem, pl.BlockSpec(memory_space=pl.ANY)],
        out_specs=tuple([hbm] * (2 * n)),
        input_output_aliases={q: q for q in range(2 * n)},
        compiler_params=pltpu.CompilerParams(has_side_effects=pltpu.SideEffectType.DATAFLOW_SIDE_EFFECTING),
    )(*arrays, send_sems, recv_sems, after)
    return list(res[n:])


def _pad_rows(a, rows):
    return jnp.pad(a, ((0, rows - a.shape[0]), (0, 0)))


def _rows(a):
    flat = a.reshape(-1).astype(F32)
    pad = -flat.shape[0] % (SUBLANE * LANE)
    return (jnp.pad(flat, (0, pad)) if pad else flat).reshape(-1, LANE)


def _pack_rows(arrays):
    return jnp.concatenate([_rows(a) for a in arrays], 0)


def _unpack_rows(t, shapes):
    out, off = [], 0
    for shp in shapes:
        size = math.prod(shp)
        rows = -(-size // (SUBLANE * LANE)) * SUBLANE
        out.append(t[off:off + rows].reshape(-1)[:size].reshape(shp))
        off += rows
    return out


def _stat_row(st, r):
    return st[r:r + 1, :]


def kernel(x, c, ada_w, ada_b, norm1_g, norm2_g, ff_w1, ff_w2, final_g, conv_w_in, conv_w, conv_b, conv_w_out, ssm_w_in, ssm_a_re, ssm_a_im, ssm_log_dt, ssm_b_re, ssm_b_im, ssm_c_re, ssm_c_im, ssm_d, ssm_glu_w, ssm_glu_b, ssm_w_out, sg_w_in, sg_v_g, sg_w_s, sg_b_s, sg_w_out, loss_target, m_ada_w, m_ada_b, m_norm1_g, m_norm2_g, m_ff_w1, m_ff_w2, m_final_g, m_conv_w_in, m_conv_w, m_conv_b, m_conv_w_out, m_ssm_w_in, m_ssm_a_re, m_ssm_a_im, m_ssm_log_dt, m_ssm_b_re, m_ssm_b_im, m_ssm_c_re, m_ssm_c_im, m_ssm_d, m_ssm_glu_w, m_ssm_glu_b, m_ssm_w_out, m_sg_w_in, m_sg_v_g, m_sg_w_s, m_sg_b_s, m_sg_w_out, v_ada_w, v_ada_b, v_norm1_g, v_norm2_g, v_ff_w1, v_ff_w2, v_final_g, v_conv_w_in, v_conv_w, v_conv_b, v_conv_w_out, v_ssm_w_in, v_ssm_a_re, v_ssm_a_im, v_ssm_log_dt, v_ssm_b_re, v_ssm_b_im, v_ssm_c_re, v_ssm_c_im, v_ssm_d, v_ssm_glu_w, v_ssm_glu_b, v_ssm_w_out, v_sg_w_in, v_sg_v_g, v_sg_w_s, v_sg_b_s, v_sg_w_out):
    P = dict(zip(INPUTS, (x, c, ada_w, ada_b, norm1_g, norm2_g, ff_w1, ff_w2, final_g, conv_w_in, conv_w, conv_b, conv_w_out, ssm_w_in, ssm_a_re, ssm_a_im, ssm_log_dt, ssm_b_re, ssm_b_im, ssm_c_re, ssm_c_im, ssm_d, ssm_glu_w, ssm_glu_b, ssm_w_out, sg_w_in, sg_v_g, sg_w_s, sg_b_s, sg_w_out, loss_target, m_ada_w, m_ada_b, m_norm1_g, m_norm2_g, m_ff_w1, m_ff_w2, m_final_g, m_conv_w_in, m_conv_w, m_conv_b, m_conv_w_out, m_ssm_w_in, m_ssm_a_re, m_ssm_a_im, m_ssm_log_dt, m_ssm_b_re, m_ssm_b_im, m_ssm_c_re, m_ssm_c_im, m_ssm_d, m_ssm_glu_w, m_ssm_glu_b, m_ssm_w_out, m_sg_w_in, m_sg_v_g, m_sg_w_s, m_sg_b_s, m_sg_w_out, v_ada_w, v_ada_b, v_norm1_g, v_norm2_g, v_ff_w1, v_ff_w2, v_final_g, v_conv_w_in, v_conv_w, v_conv_b, v_conv_w_out, v_ssm_w_in, v_ssm_a_re, v_ssm_a_im, v_ssm_log_dt, v_ssm_b_re, v_ssm_b_im, v_ssm_c_re, v_ssm_c_im, v_ssm_d, v_ssm_glu_w, v_ssm_glu_b, v_ssm_w_out, v_sg_w_in, v_sg_v_g, v_sg_w_s, v_sg_b_s, v_sg_w_out)))
    L, D = x.shape[1], x.shape[2]
    me = _my_index()
    xs = x[0]
    tgt = loss_target[0]
    n_conv = conv_w_in.shape[0]

    def gather_item(shard, axis):
        full = tuple(N_DEV * s if a == axis else s for a, s in enumerate(shard.shape))
        return shard, full, ('gather', axis, shard.shape[axis])

    def mixer_shards(i):
        kind, j = i % 3, i // 3
        if kind == 0:
            return [(conv_w_in[j], 1), (conv_w_out[j], 0)]
        if kind == 1:
            return [(ssm_w_in[j], 0), (ssm_glu_w[j], 0), (ssm_w_out[j], 0)]
        return [(sg_w_in[j], 1), (sg_w_out[j], 0)]

    c_act = c * (1.0 / (1.0 + jnp.exp(-c)))
    vec_rows = jnp.concatenate([c_act.reshape(D // LANE, LANE), conv_w.reshape(-1, LANE), conv_b.reshape(-1, LANE),
                                sg_v_g.reshape(-1, LANE)], 0)
    n_vec = vec_rows.shape[0]
    vec_all = _all_gather(_pad_rows(vec_rows, 24)[None], 0, name="gather_vectors")
    c_all = vec_all[:, :D // LANE, :].reshape(N_DEV, D)
    sharded_full = vec_all[:, D // LANE:n_vec, :].transpose(1, 0, 2).reshape(n_vec - D // LANE, D)
    conv_w_full = sharded_full[:3 * n_conv].reshape(n_conv, 3, D)
    conv_b_full = sharded_full[3 * n_conv:4 * n_conv]
    sg_vg_full = sharded_full[4 * n_conv:4 * n_conv + 1]

    c_pad = _pad_rows(c_all, LANE)
    ncol = ada_w.shape[2]
    mod_part = jnp.stack([_mm(c_pad, ada_w[i], name=f"ada_fwd{i}")[:N_DEV] for i in range(DEPTH)])
    mod_all = _all_gather(mod_part.reshape(1, DEPTH * N_DEV, ncol), 0, name="gather_mod")
    mod_all = mod_all.reshape(N_DEV, DEPTH, N_DEV, ncol)
    mod_me = lax.dynamic_index_in_dim(mod_all, me, 2, keepdims=False)
    mod = mod_me.transpose(1, 0, 2).reshape(DEPTH, N_DEV * ncol) + ada_b
    gathers, gather_token = _exchange_start(
        [[gather_item(w.astype(BF16), ax) for w, ax in shards]
         for i in range(DEPTH) for shards in (mixer_shards(i), [(ff_w1[i], 1), (ff_w2[i], 0)])],
        mod, name="gather_start")
    mod = mod + gather_token[0:1, 0:1]

    s5_args = (ssm_a_re[0], ssm_a_im[0], ssm_log_dt[0], ssm_b_re[0], ssm_b_im[0], ssm_c_re[0], ssm_c_im[0])
    (abar_re, abar_im, bblk_re, bblk_im, cblk_re, cblk_im), s5_vjp = jax.vjp(_s5_prep, *s5_args)
    (pw_fwd, pos_fwd), (pw_rev, pos_rev) = _s5_power_tables(abar_re, abar_im, S5_CHUNK // SUBLANE)
    s5_w = tuple(t.astype(BF16) for t in (bblk_re, bblk_im, cblk_re, cblk_im))
    causal = jnp.tril(jnp.ones((SG_CHUNK, SG_CHUNK), dtype=bool))
    ws_m = jnp.where(causal[None], sg_w_s[0], 0.0)
    ws_b = ws_m.astype(BF16)
    wst_b = ws_m.transpose(0, 2, 1).astype(BF16)
    bsb = jnp.broadcast_to(sg_b_s[0][:, :, None], (SG_HEADS, SG_CHUNK, LANE))

    saved = []
    xa = xs
    mods = [[mod[i:i + 1, q * D:(q + 1) * D] for q in range(6)] for i in range(DEPTH)]
    wn1s = [norm1_g[i:i + 1] * (1.0 + mods[i][1]) for i in range(DEPTH)]
    h1 = _normmod_fwd(xa, wn1s[0], mods[0][0], name="norm1_fwd0")
    for i in range(DEPTH):
        kind, j = i % 3, i // 3
        sh1, sc1, g1, sh2, sc2, g2 = mods[i]
        wn1 = wn1s[i]
        wn2 = norm2_g[i:i + 1] * (1.0 + sc2)
        S = dict(x_in=xa, g1=g1, g2=g2, sc1=sc1, sc2=sc2, wn1=wn1, wn2=wn2)
        w_mix = _exchange_wait(gathers[2 * i], h1, name=f"gather_mix_wait{i}")
        S['h1'] = h1
        if kind == 0:
            bcx = _mm(h1, w_mix[0], name=f"conv_in{i}", out_dtypes=(BF16,), bm=2048)
            wb = _pad_rows(jnp.concatenate([conv_w_full[j], conv_b_full[j:j + 1]], 0), SUBLANE)
            pb = _conv_fwd(bcx, wb, name=f"conv_mix{i}")
            S.update(bcx=bcx, wb=wb, pb=pb)
        elif kind == 1:
            u = _mm(h1, w_mix[0], name=f"ssm_in{i}")
            sre, sim, ypre, yg = _s5_fwd(u, *s5_w, pw_fwd, pos_fwd, ssm_d, name=f"s5_scan{i}")

            def glu_epi(acc, yv, bias):
                t = acc + bias
                return yv * (1.0 / (1.0 + jnp.exp(-t))), t

            pb, tt = _mm(yg, w_mix[1], name=f"ssm_glu{i}", out_dtypes=(BF16, F32), epi=glu_epi,
                         extras=[(yg, 'mn'), (ssm_glu_b, 'n')])
            S.update(u=u, sre=sre, sim=sim, ypre=ypre, yg=yg, pb=pb, tt=tt)
        else:
            uv = _mm(h1, w_mix[0], name=f"sg_in{i}", bm=2048)
            pb = _sg_fwd(uv, sg_vg_full, ws_b, bsb, name=f"sg_mix{i}")
            S.update(uv=uv, pb=pb)
        x_mid, y_mix, h2 = _mm(pb, w_mix[-1], name=f"mix_out{i}", out_dtypes=(F32, BF16, BF16), epi=_epi_residual_norm,
                               extras=[(xa, 'mn'), (g1, 'n'), (wn2, 'n'), (sh2, 'n')])
        w1_full, w2_full = _exchange_wait(gathers[2 * i + 1], h2, name=f"gather_ff_wait{i}")
        S.update(w_mix=w_mix, w1=w1_full, w2=w2_full)
        ra = _mm(h2, w1_full, name=f"ff_up{i}", out_dtypes=(BF16,), epi=lambda acc: (jnp.maximum(acc, 0.0),), bm=2048)
        if i + 1 < DEPTH:
            xa, f_out, h1 = _mm(ra, w2_full, name=f"ff_down{i}", out_dtypes=(F32, BF16, BF16), a_fn=_square,
                                epi=_epi_residual_norm, bm=256, bk=w2_full.shape[0],
                                extras=[(x_mid, 'mn'), (g2, 'n'), (wn1s[i + 1], 'n'), (mods[i + 1][0], 'n')])
        else:
            xa, f_out = _mm(ra, w2_full, name=f"ff_down{i}", out_dtypes=(F32, BF16), epi=_epi_residual, a_fn=_square,
                            extras=[(x_mid, 'mn'), (g2, 'n')], bm=256, bk=w2_full.shape[0])
        S.update(x_mid=x_mid, y_mix=y_mix, h2=h2, ra=ra, f_out=f_out)
        saved.append(S)

    S = saved[-1]
    dx, st, dfb, loss_tile = _loss_head(xa, tgt, final_g[None], S['f_out'], S['g2'], name="loss_head")
    d_final_g = _stat_row(st, 0)
    dg2_next = _stat_row(st, 2)

    def scatter_item(g, axis):
        m = g.shape[axis] // N_DEV
        shard = tuple(m if a == axis else s for a, s in enumerate(g.shape))
        return g, (N_DEV,) + shard, ('scatter', axis, m)

    dmod = [None] * DEPTH
    dn1g, dn2g = [None] * DEPTH, [None] * DEPTH
    d_conv_w, d_conv_b = [None] * n_conv, [None] * n_conv
    ff_sent, mix_sent = [None] * DEPTH, [None] * DEPTH
    small = {}
    for i in reversed(range(DEPTH)):
        kind, j = i % 3, i // 3
        S = saved[i]
        w_mix = S['w_mix']
        dg2 = dg2_next
        da = _mm(dfb, S['w2'], tb=True, name=f"ff_down_bwd{i}", out_dtypes=(BF16,), bm=2048,
                 epi=lambda acc, rav: (acc * (2.0 * rav.astype(F32)),), extras=[(S['ra'], 'mn')])
        dw2 = _wgrad(S['ra'], dfb, name=f"ff_w2_grad{i}", a_fn=_square, bm=256, bn=1024)
        dh2 = _mm(da, S['w1'], tb=True, name=f"ff_up_bwd{i}", out_dtypes=(BF16,), bm=512, bk=da.shape[1])
        dw1 = _wgrad(S['h2'], da, name=f"ff_w1_grad{i}")
        (ff_sent[i],), token = _exchange_start([[scatter_item(dw1, 1), scatter_item(dw2, 0)]], dx,
                                               name=f"ff_grads_start{i}")
        dx_mid, st2, dyb = _normmod_bwd(dh2, S['x_mid'], S['wn2'] + token[0:1, 0:1], dx,
                                        (S['y_mix'], S['g1']), name=f"norm2_bwd{i}")
        dsc2 = _stat_row(st2, 0) * norm2_g[i:i + 1]
        dn2g[i] = _stat_row(st2, 0) * (1.0 + S['sc2'])
        dsh2 = _stat_row(st2, 1)
        dg1 = _stat_row(st2, 2)
        if kind == 0:
            dp = _mm(dyb, w_mix[1], tb=True, name=f"conv_out_bwd{i}", out_dtypes=(BF16,))
            d_cwo = _wgrad(S['pb'], dyb, name=f"conv_w_out_grad{i}")
            dbcx, stc = _conv_bwd(dp, S['bcx'], S['wb'], name=f"conv_mix_bwd{i}")
            d_conv_w[j] = stc[0:3]
            d_conv_b[j] = stc[3:4]
            dh1 = _mm(dbcx, w_mix[0], tb=True, name=f"conv_in_bwd{i}", out_dtypes=(BF16,), bm=512)
            d_cwi = _wgrad(S['h1'], dbcx, name=f"conv_w_in_grad{i}")
            mix_grads = [scatter_item(d_cwi, 1), scatter_item(d_cwo, 0)]
        elif kind == 1:
            dy2 = _mm(dyb, w_mix[2], tb=True, name=f"ssm_out_bwd{i}")
            d_ssm_out = _wgrad(S['pb'], dyb, name=f"ssm_w_out_grad{i}")
            dtb, dya, stg = _glu_bwd(dy2, S['yg'], S['tt'], name=f"ssm_glu_bwd{i}")
            dypre = _mm(dtb, w_mix[1], tb=True, name=f"ssm_glu_in_bwd{i}",
                        epi=lambda acc, a, yp: ((a + acc) * _gelu_grad(yp),),
                        extras=[(dya, 'mn'), (S['ypre'], 'mn')])
            d_glu = _wgrad(S['yg'], dtb, name=f"ssm_glu_w_grad{i}", bm=512)
            dub, dbre, dbim, dcre, dcim, ga, dd = _s5_bwd(dypre, S['u'], S['sre'], S['sim'], *s5_w, pw_rev, pos_rev, ssm_d,
                                                           name=f"s5_scan_bwd{i}")
            dh1 = _mm(dub, w_mix[0], tb=True, name=f"ssm_in_bwd{i}", out_dtypes=(BF16,))
            d_ssm_in = _wgrad(S['h1'], dub, name=f"ssm_w_in_grad{i}")
            da_re, da_im, dlog_dt, db_re, db_im, dc_re, dc_im = s5_vjp((ga[0:1], ga[1:2], dbre, dbim, dcre, dcim))
            s5_small = _pack_rows([da_re, da_im, dlog_dt, db_re, db_im, dc_re, dc_im, dd[0], stg[0]])
            mix_grads = [scatter_item(d_ssm_in, 0), scatter_item(d_glu, 0), scatter_item(d_ssm_out, 0),
                         gather_item(s5_small, 0)]
        else:
            dp = _mm(dyb, w_mix[1], tb=True, name=f"sg_out_bwd{i}")
            d_sgo = _wgrad(S['pb'], dyb, name=f"sg_w_out_grad{i}")
            duv, dws, dbs, stv = _sg_bwd(dp, S['uv'], sg_vg_full, ws_b, wst_b, bsb, name=f"sg_mix_bwd{i}")
            dh1 = _mm(duv, w_mix[0], tb=True, name=f"sg_in_bwd{i}", out_dtypes=(BF16,), bm=512, bk=duv.shape[1])
            d_sgi = _wgrad(S['h1'], duv, name=f"sg_w_in_grad{i}")
            sg_small = _pack_rows([jnp.where(causal[None], dws, 0.0), jnp.sum(dbs, axis=-1)])
            d_sg_vg = stv[0:1]
            mix_grads = [scatter_item(d_sgi, 1), scatter_item(d_sgo, 0), gather_item(sg_small, 0)]
        wn1 = S['wn1']
        if i > 0:
            (mix_sent[i],), token = _exchange_start([mix_grads], dx_mid, name=f"mix_grads_start{i}")
            wn1 = wn1 + token[0:1, 0:1]
            prev = saved[i - 1]
            dx, st1, dfb = _normmod_bwd(dh1, S['x_in'], wn1, dx_mid, (prev['f_out'], prev['g2']),
                                        name=f"norm1_bwd{i}")
            dg2_next = _stat_row(st1, 2)
        else:
            dx, st1 = _normmod_bwd(dh1, S['x_in'], wn1, dx_mid, None, name=f"norm1_bwd{i}")
        dsc1 = _stat_row(st1, 0) * norm1_g[i:i + 1]
        dn1g[i] = _stat_row(st1, 0) * (1.0 + S['sc1'])
        dsh1 = _stat_row(st1, 1)
        dmod[i] = jnp.concatenate([dsh1, dsc1, dg1, dsh2, dsc2, dg2], 1)
    grad_x = dx[None]

    out = {}

    def small_group(names, parts, label):
        shapes = [P[n].shape for n in names]
        w, m, v = (_pack_rows([P[pre + n] for n in names])[None] for pre in ('', 'm_', 'v_'))
        res = [_unpack_rows(t[0], shapes) for t in _adamw(w, [parts], m, v, name=label)]
        for q, n in enumerate(names):
            out[n] = tuple(r[q] for r in res)

    small.update(ada_b=jnp.concatenate(dmod, 0), norm1_g=jnp.concatenate(dn1g, 0), norm2_g=jnp.concatenate(dn2g, 0),
                 final_g=d_final_g, conv_w=jnp.stack(d_conv_w), conv_b=jnp.concatenate(d_conv_b, 0), sg_v_g=d_sg_vg)
    last_pack = _pack_rows([small[n] for n in LAST_SMALL + SMALL_SHARD])
    n_last = _pack_rows([P[n] for n in LAST_SMALL]).shape[0]
    n_pack = last_pack.shape[0]
    pack_all = _all_gather(jnp.concatenate([last_pack, loss_tile], 0)[None], 0, name="gather_small_grads")
    loss = jnp.sum(pack_all[:, n_pack, 0])
    (mix_sent[0],), last_token = _exchange_start([mix_grads], pack_all, name="mix_grads_start0")
    small_group(LAST_SMALL, pack_all[:, :n_last], "adamw_small")
    sh_rows = (n_pack - n_last) // N_DEV
    sh_parts = pack_all[:, n_last:n_pack].reshape(N_DEV, sh_rows, N_DEV, LANE)
    sh_parts = lax.dynamic_index_in_dim(sh_parts, me, 2, keepdims=False)
    sh_parts = jnp.pad(sh_parts, ((0, 0), (0, 16 - sh_rows), (0, 0)))

    def pack_shard(prefix):
        return _pad_rows(jnp.concatenate([P[prefix + n].reshape(-1, LANE) for n in SMALL_SHARD], 0), 16)[None]

    sg_, sd_, sm_, sv_ = _adamw(pack_shard(''), [sh_parts], pack_shard('m_'), pack_shard('v_'), name="adamw_channel")
    off = 0
    for n in SMALL_SHARD:
        rows = math.prod(P[n].shape) // LANE
        out[n] = tuple(t[0, off:off + rows].reshape(P[n].shape) for t in (sg_, sd_, sm_, sv_))
        off += rows

    dmod_all = pack_all[:, :DEPTH * 6 * D // LANE].reshape(N_DEV, DEPTH, 6 * D)
    dmod_cols = lax.dynamic_slice_in_dim(dmod_all, me * ncol, ncol, 2)
    g_ada = [_mm(c_pad, _pad_rows(dmod_cols[:, i], LANE), ta=True, name=f"ada_w_grad{i}")[None] for i in range(DEPTH)]

    def big(name, parts):
        res = _adamw(P[name], parts, P['m_' + name], P['v_' + name], name="adamw_" + name)
        out[name] = res
        return res[1]

    ff_parts = [_exchange_wait(ff_sent[i], last_token, name=f"ff_grads_wait{i}") for i in range(DEPTH)]
    mix_parts = [None] + [_exchange_wait(mix_sent[i], last_token, name=f"mix_grads_wait{i}") for i in range(1, DEPTH)]
    big('ada_w', g_ada)
    big('ff_w1', [p[0] for p in ff_parts])
    big('ff_w2', [p[1] for p in ff_parts])
    done = big('sg_w_in', [mix_parts[2][0]])
    mix_parts[0] = _exchange_wait(mix_sent[0], done, name="mix_grads_wait0")
    big('conv_w_in', [mix_parts[i][0] for i in range(DEPTH) if i % 3 == 0])
    row_names = ['conv_w_out', 'ssm_w_in', 'ssm_glu_w', 'ssm_w_out', 'sg_w_out']
    row_parts = ([mix_parts[i][1] for i in range(DEPTH) if i % 3 == 0] + mix_parts[1][:3] + [mix_parts[2][1]])
    small_group(S5_SMALL, mix_parts[1][3].reshape(N_DEV, -1, LANE), "adamw_s5")
    small_group(SG_SMALL, mix_parts[2][2].reshape(N_DEV, -1, LANE), "adamw_sg")
    row_w, row_m, row_v = (jnp.concatenate([P[pre + n] for n in row_names], 0) for pre in ('', 'm_', 'v_'))
    rw = _adamw(row_w, row_parts, row_m, row_v, name="adamw_row_sharded")
    off = 0
    for n in row_names:
        cnt = P[n].shape[0]
        out[n] = tuple(t[off:off + cnt] for t in rw)
        off += cnt

    return (loss, grad_x, *[out[n][0] for n in WEIGHTS], *[out[n][1] for n in WEIGHTS],
            *[out[n][2] for n in WEIGHTS], *[out[n][3] for n in WEIGHTS])
```

```python
import math

import jax
import jax.numpy as jnp
from jax import lax
from jax.experimental import pallas as pl
from jax.experimental.pallas import tpu as pltpu

F32 = jnp.float32
BF16 = jnp.bfloat16

N_DEV = 8
MESH_ID = pl.DeviceIdType.MESH
DEPTH = 4
EPS = 1e-6
S5_GROUPS, S5_GROUP, S5_STATE = 64, 16, 64
S5_LANES = S5_GROUPS * S5_STATE
S5_BLOCKS = 8
S5_CHUNK = 512
SG_HEADS, SG_CHUNK = 8, 128
LANE = 128
SUBLANE = 8
VMEM_LIMIT = 48 * 1024 * 1024
ADAM_LR, ADAM_B1, ADAM_B2, ADAM_EPS, ADAM_WD, ADAM_STEP = 0.001, 0.9, 0.999, 1e-08, 0.01, 10
GELU_C = math.sqrt(2.0 / math.pi)
GELU_A = 0.044715

WEIGHTS = ['ada_w', 'ada_b', 'norm1_g', 'norm2_g', 'ff_w1', 'ff_w2', 'final_g', 'conv_w_in', 'conv_w', 'conv_b',
           'conv_w_out', 'ssm_w_in', 'ssm_a_re', 'ssm_a_im', 'ssm_log_dt', 'ssm_b_re', 'ssm_b_im', 'ssm_c_re',
           'ssm_c_im', 'ssm_d', 'ssm_glu_w', 'ssm_glu_b', 'ssm_w_out', 'sg_w_in', 'sg_v_g', 'sg_w_s', 'sg_b_s',
           'sg_w_out']
INPUTS = ['x', 'c'] + WEIGHTS + ['loss_target'] + ['m_' + n for n in WEIGHTS] + ['v_' + n for n in WEIGHTS]
S5_SMALL = ['ssm_a_re', 'ssm_a_im', 'ssm_log_dt', 'ssm_b_re', 'ssm_b_im', 'ssm_c_re', 'ssm_c_im', 'ssm_d', 'ssm_glu_b']
SG_SMALL = ['sg_w_s', 'sg_b_s']
LAST_SMALL = ['ada_b', 'norm1_g', 'norm2_g', 'final_g']
SMALL_SHARD = ['conv_w', 'conv_b', 'sg_v_g']


def _params(*sem):
    return pltpu.CompilerParams(dimension_semantics=sem or None, vmem_limit_bytes=VMEM_LIMIT)


def _my_pos():
    return lax.axis_index("x"), lax.axis_index("y"), lax.axis_index("c")


def _my_index():
    x, y, c = _my_pos()
    return 4 * x + 2 * y + c


def _mm(a, b, *, name, ta=False, tb=False, out_dtypes=(F32,), epi=None, extras=(), a_fn=None, bm=1024, bn=1024,
        bk=1024):
    a_chunks = a.shape[0] if a.ndim == 3 else 0
    b_chunks = b.shape[0] if b.ndim == 3 else 0
    assert not (a_chunks and ta) and not (b_chunks and tb)
    if a_chunks:
        m, k = a.shape[1], a_chunks * a.shape[2]
        bk = k
    else:
        m, k = (a.shape[1], a.shape[0]) if ta else a.shape
    if b_chunks:
        k2, n = b.shape[1], b_chunks * b.shape[2]
        bn = min(bn, b.shape[2])
    else:
        k2, n = (b.shape[1], b.shape[0]) if tb else b.shape
    assert k == k2, (a.shape, b.shape, ta, tb)
    bm, bn, bk = min(bm, m), min(bn, n), min(bk, k)
    assert m % bm == 0 and n % bn == 0 and k % bk == 0, (m, n, k, bm, bn, bk)
    nk = k // bk
    n_ex, n_out = len(extras), len(out_dtypes)
    dims = (((0 if ta else 1,), (1 if tb else 0,)), ((), ()))

    def body(*refs):
        a_ref, b_ref = refs[0], refs[1]
        ex_refs = refs[2:2 + n_ex]
        out_refs = refs[2 + n_ex:2 + n_ex + n_out]

        def finish(acc):
            outs = epi(acc, *[r[...] for r in ex_refs]) if epi is not None else (acc,)
            for r, o in zip(out_refs, outs):
                r[...] = o.astype(r.dtype)

        av = jnp.concatenate([a_ref[t] for t in range(a_chunks)], axis=1) if a_chunks else a_ref[...]
        if a_fn is not None:
            av = a_fn(av)
        part = lax.dot_general(av.astype(BF16), b_ref[...].astype(BF16), dims, preferred_element_type=F32)
        if nk == 1:
            finish(part)
            return
        acc_ref = refs[-1]
        kk = pl.program_id(2)

        @pl.when(kk == 0)
        def _():
            acc_ref[...] = part

        @pl.when(kk > 0)
        def _():
            acc_ref[...] += part

        @pl.when(kk == nk - 1)
        def _():
            finish(acc_ref[...])

    if a_chunks:
        a_spec = pl.BlockSpec((a_chunks, bm, a.shape[2]), lambda i, j, q: (0, i, 0))
    elif ta:
        a_spec = pl.BlockSpec((bk, bm), lambda i, j, q: (q, i))
    else:
        a_spec = pl.BlockSpec((bm, bk), lambda i, j, q: (i, q))
    if b_chunks:
        per = b.shape[2] // bn
        b_spec = pl.BlockSpec((None, bk, bn), lambda i, j, q: (j // per, q, j % per))
    elif tb:
        b_spec = pl.BlockSpec((bn, bk), lambda i, j, q: (j, q))
    else:
        b_spec = pl.BlockSpec((bk, bn), lambda i, j, q: (q, j))
    ex_specs = []
    for arr, kind in extras:
        if kind == 'mn':
            assert arr.shape == (m, n), (arr.shape, m, n)
            ex_specs.append(pl.BlockSpec((bm, bn), lambda i, j, q: (i, j)))
        else:
            assert arr.shape == (1, n), (arr.shape, n)
            ex_specs.append(pl.BlockSpec((1, bn), lambda i, j, q: (0, j)))
    outs = pl.pallas_call(
        body, name=name,
        out_shape=tuple(jax.ShapeDtypeStruct((m, n), d) for d in out_dtypes),
        grid=(m // bm, n // bn, nk),
        in_specs=[a_spec, b_spec] + ex_specs,
        out_specs=tuple(pl.BlockSpec((bm, bn), lambda i, j, q: (i, j)) for _ in out_dtypes),
        scratch_shapes=[pltpu.VMEM((bm, bn), F32)] if nk > 1 else [],
        compiler_params=_params("parallel", "parallel", "arbitrary"),
    )(a, b, *[arr for arr, _ in extras])
    return outs if n_out > 1 else outs[0]


def _epi_residual(acc, res, gate):
    return res + gate * acc, acc


def _epi_residual_norm(acc, res, gate, w, sh):
    xn = res + gate * acc
    return xn, acc, xn * _rstd(xn) * w + sh


def _wgrad(acts, cots, *, name, a_fn=None, bm=1024, bn=512):
    return _mm(acts, cots, ta=True, name=name, out_dtypes=(BF16,), a_fn=a_fn, bm=bm, bn=bn, bk=acts.shape[0])


def _square(a):
    af = a.astype(F32)
    return af * af


def _rstd(xv):
    return lax.rsqrt(jnp.mean(xv * xv, axis=-1, keepdims=True) + EPS)


def _normmod_fwd(x, w, sh, *, name, tm=512):
    L, D = x.shape

    def body(x_ref, w_ref, s_ref, h_ref):
        xv = x_ref[...]
        h_ref[...] = (xv * _rstd(xv) * w_ref[...] + s_ref[...]).astype(h_ref.dtype)

    row = pl.BlockSpec((tm, D), lambda i: (i, 0))
    vec = pl.BlockSpec((1, D), lambda i: (0, 0))
    return pl.pallas_call(body, name=name, out_shape=jax.ShapeDtypeStruct((L, D), BF16), grid=(L // tm,),
                          in_specs=[row, vec, vec], out_specs=row, compiler_params=_params("parallel"))(x, w, sh)


def _normmod_bwd(dh, x, w, dres, gate, *, name, tm=256):
    L, D = x.shape
    has_gate = gate is not None

    def body(*refs):
        if has_gate:
            dh_ref, x_ref, w_ref, r_ref, y_ref, g_ref, dx_ref, st_ref, dy_ref = refs
        else:
            dh_ref, x_ref, w_ref, r_ref, dx_ref, st_ref = refs
        i = pl.program_id(0)

        @pl.when(i == 0)
        def _():
            st_ref[...] = jnp.zeros_like(st_ref)

        xv = x_ref[...]
        dhv = dh_ref[...].astype(F32)
        rstd = _rstd(xv)
        xn = xv * rstd
        dxn = dhv * w_ref[...]
        dx = rstd * (dxn - xn * jnp.mean(dxn * xn, axis=-1, keepdims=True)) + r_ref[...]
        dx_ref[...] = dx
        st_ref[0:1, :] += jnp.sum(dhv * xn, axis=0, keepdims=True)
        st_ref[1:2, :] += jnp.sum(dhv, axis=0, keepdims=True)
        if has_gate:
            dy_ref[...] = (dx * g_ref[...]).astype(dy_ref.dtype)
            st_ref[2:3, :] += jnp.sum(dx * y_ref[...].astype(F32), axis=0, keepdims=True)

    row = pl.BlockSpec((tm, D), lambda i: (i, 0))
    vec = pl.BlockSpec((1, D), lambda i: (0, 0))
    st = pl.BlockSpec((SUBLANE, D), lambda i: (0, 0))
    in_specs = [row, row, vec, row] + ([row, vec] if has_gate else [])
    out_shape = [jax.ShapeDtypeStruct((L, D), F32), jax.ShapeDtypeStruct((SUBLANE, D), F32)]
    out_specs = [row, st]
    if has_gate:
        out_shape.append(jax.ShapeDtypeStruct((L, D), BF16))
        out_specs.append(row)
    args = (dh, x, w, dres) + (tuple(gate) if has_gate else ())
    return pl.pallas_call(body, name=name, out_shape=tuple(out_shape), grid=(L // tm,), in_specs=in_specs,
                          out_specs=tuple(out_specs), compiler_params=_params("arbitrary"))(*args)


def _loss_head(x, tgt, fg, y, g, *, name, tm=256):
    L, D = x.shape

    def body(x_ref, t_ref, fg_ref, y_ref, g_ref, dx_ref, st_ref, dy_ref, loss_ref):
        i = pl.program_id(0)

        @pl.when(i == 0)
        def _():
            st_ref[...] = jnp.zeros_like(st_ref)
            loss_ref[...] = jnp.zeros_like(loss_ref)

        xv = x_ref[...]
        rstd = _rstd(xv)
        xn = xv * rstd
        err = xn * fg_ref[...] - t_ref[...]
        loss_ref[...] += 0.5 * jnp.sum(jnp.mean(err * err, axis=-1, keepdims=True))
        dout = err * (1.0 / D)
        dxn = dout * fg_ref[...]
        dx = rstd * (dxn - xn * jnp.mean(dxn * xn, axis=-1, keepdims=True))
        dx_ref[...] = dx
        dy_ref[...] = (dx * g_ref[...]).astype(dy_ref.dtype)
        st_ref[0:1, :] += jnp.sum(dout * xn, axis=0, keepdims=True)
        st_ref[2:3, :] += jnp.sum(dx * y_ref[...].astype(F32), axis=0, keepdims=True)

    row = pl.BlockSpec((tm, D), lambda i: (i, 0))
    vec = pl.BlockSpec((1, D), lambda i: (0, 0))
    return pl.pallas_call(
        body, name=name,
        out_shape=(jax.ShapeDtypeStruct((L, D), F32), jax.ShapeDtypeStruct((SUBLANE, D), F32),
                   jax.ShapeDtypeStruct((L, D), BF16), jax.ShapeDtypeStruct((SUBLANE, LANE), F32)),
        grid=(L // tm,), in_specs=[row, row, vec, row, vec],
        out_specs=(row, pl.BlockSpec((SUBLANE, D), lambda i: (0, 0)), row,
                   pl.BlockSpec((SUBLANE, LANE), lambda i: (0, 0))),
        compiler_params=_params("arbitrary"))(x, tgt, fg, y, g)


def _shift_down(v, k):
    row = lax.broadcasted_iota(jnp.int32, v.shape, 0)
    return jnp.where(row >= k, pltpu.roll(v, k, 0), 0.0)


def _shift_up(v, k):
    n = v.shape[0]
    row = lax.broadcasted_iota(jnp.int32, v.shape, 0)
    return jnp.where(row < n - k, pltpu.roll(v, n - k, 0), 0.0)


def _conv_views(L, D):
    return [pl.BlockSpec((L, LANE), lambda j, s=s: (0, s * (D // LANE) + j)) for s in range(3)]


def _conv_fwd(bcx, wb, *, name):
    L, D = bcx.shape[0], bcx.shape[1] // 3

    def body(b_ref, c_ref, x_ref, wb_ref, p_ref):
        z = c_ref[...].astype(F32) * x_ref[...].astype(F32)
        conv = (wb_ref[0:1, :] * _shift_down(z, 2) + wb_ref[1:2, :] * _shift_down(z, 1)
                + wb_ref[2:3, :] * z + wb_ref[3:4, :])
        p_ref[...] = (b_ref[...].astype(F32) * conv).astype(p_ref.dtype)

    col = pl.BlockSpec((L, LANE), lambda j: (0, j))
    return pl.pallas_call(body, name=name, out_shape=jax.ShapeDtypeStruct((L, D), BF16), grid=(D // LANE,),
                          in_specs=_conv_views(L, D) + [pl.BlockSpec((SUBLANE, LANE), lambda j: (0, j))],
                          out_specs=col, compiler_params=_params("parallel"))(bcx, bcx, bcx, wb)


def _conv_bwd(dp, bcx, wb, *, name):
    L, D = dp.shape

    def body(dp_ref, b_ref, c_ref, x_ref, wb_ref, d3_ref, st_ref):
        cv, xv = c_ref[...].astype(F32), x_ref[...].astype(F32)
        z = cv * xv
        z1, z2 = _shift_down(z, 1), _shift_down(z, 2)
        w0, w1, w2 = wb_ref[0:1, :], wb_ref[1:2, :], wb_ref[2:3, :]
        conv = w0 * z2 + w1 * z1 + w2 * z + wb_ref[3:4, :]
        dpv = dp_ref[...].astype(F32)
        d3_ref[0] = (dpv * conv).astype(d3_ref.dtype)
        dconv = dpv * b_ref[...].astype(F32)
        dz = w2 * dconv + w1 * _shift_up(dconv, 1) + w0 * _shift_up(dconv, 2)
        d3_ref[1] = (dz * xv).astype(d3_ref.dtype)
        d3_ref[2] = (dz * cv).astype(d3_ref.dtype)
        st_ref[...] = jnp.zeros_like(st_ref)
        st_ref[0:1, :] = jnp.sum(dconv * z2, axis=0, keepdims=True)
        st_ref[1:2, :] = jnp.sum(dconv * z1, axis=0, keepdims=True)
        st_ref[2:3, :] = jnp.sum(dconv * z, axis=0, keepdims=True)
        st_ref[3:4, :] = jnp.sum(dconv, axis=0, keepdims=True)

    col = pl.BlockSpec((L, LANE), lambda j: (0, j))
    vec = pl.BlockSpec((SUBLANE, LANE), lambda j: (0, j))
    return pl.pallas_call(body, name=name,
                          out_shape=(jax.ShapeDtypeStruct((3, L, D), BF16), jax.ShapeDtypeStruct((SUBLANE, D), F32)),
                          grid=(D // LANE,), in_specs=[col] + _conv_views(L, D) + [vec],
                          out_specs=(pl.BlockSpec((3, L, LANE), lambda j: (0, 0, j)), vec),
                          compiler_params=_params("parallel"))(dp, bcx, bcx, bcx, wb)


def _sg_fwd(uv, vg, ws, bsb, *, name, tr=512):
    L, D = uv.shape[0], uv.shape[1] // 2

    def body(uv_ref, vg_ref, ws_ref, bsb_ref, p_ref):
        for ci in range(tr // SG_CHUNK):
            rows = slice(ci * SG_CHUNK, (ci + 1) * SG_CHUNK)
            v = uv_ref[rows, D:2 * D]
            vn = (v * _rstd(v) * vg_ref[...]).astype(BF16)
            for h in range(SG_HEADS):
                cols = slice(h * LANE, (h + 1) * LANE)
                vm = jnp.dot(ws_ref[h], vn[:, cols], preferred_element_type=F32) + bsb_ref[h]
                p_ref[rows, cols] = (uv_ref[rows, cols] * vm).astype(p_ref.dtype)

    full3 = pl.BlockSpec((SG_HEADS, SG_CHUNK, LANE), lambda i: (0, 0, 0))
    return pl.pallas_call(body, name=name, out_shape=jax.ShapeDtypeStruct((L, D), BF16), grid=(L // tr,),
                          in_specs=[pl.BlockSpec((tr, 2 * D), lambda i: (i, 0)), pl.BlockSpec((1, D), lambda i: (0, 0)),
                                    full3, full3],
                          out_specs=pl.BlockSpec((tr, D), lambda i: (i, 0)),
                          compiler_params=_params("parallel"))(uv, vg, ws, bsb)


def _sg_bwd(dp, uv, vg, ws, wst, bsb, *, name, tr=512):
    L, D = dp.shape

    def body(dp_ref, uv_ref, vg_ref, ws_ref, wst_ref, bsb_ref, duv_ref, dws_ref, dbs_ref, st_ref, dvn_ref):
        i = pl.program_id(0)

        @pl.when(i == 0)
        def _():
            dws_ref[...] = jnp.zeros_like(dws_ref)
            dbs_ref[...] = jnp.zeros_like(dbs_ref)
            st_ref[...] = jnp.zeros_like(st_ref)

        for ci in range(tr // SG_CHUNK):
            rows = slice(ci * SG_CHUNK, (ci + 1) * SG_CHUNK)
            v = uv_ref[rows, D:2 * D]
            rstd = _rstd(v)
            vhat = v * rstd
            vn = (vhat * vg_ref[...]).astype(BF16)
            for h in range(SG_HEADS):
                cols = slice(h * LANE, (h + 1) * LANE)
                vm = jnp.dot(ws_ref[h], vn[:, cols], preferred_element_type=F32) + bsb_ref[h]
                dph = dp_ref[rows, cols]
                duv_ref[rows, cols] = (dph * vm).astype(duv_ref.dtype)
                dvm = dph * uv_ref[rows, cols]
                dbs_ref[h] += dvm
                dvmb = dvm.astype(BF16)
                dws_ref[h] += lax.dot_general(dvmb, vn[:, cols], (((1,), (1,)), ((), ())),
                                              preferred_element_type=F32)
                dvn_ref[rows, cols] = jnp.dot(wst_ref[h], dvmb, preferred_element_type=F32)
            dvn = dvn_ref[rows, :]
            gv = dvn * vg_ref[...]
            dv = rstd * (gv - vhat * jnp.mean(gv * vhat, axis=-1, keepdims=True))
            duv_ref[rows, D:2 * D] = dv.astype(duv_ref.dtype)
            st_ref[0:1, :] += jnp.sum(dvn * vhat, axis=0, keepdims=True)

    full3 = pl.BlockSpec((SG_HEADS, SG_CHUNK, LANE), lambda i: (0, 0, 0))
    acc3 = jax.ShapeDtypeStruct((SG_HEADS, SG_CHUNK, LANE), F32)
    return pl.pallas_call(
        body, name=name,
        out_shape=(jax.ShapeDtypeStruct((L, 2 * D), BF16), acc3, acc3, jax.ShapeDtypeStruct((SUBLANE, D), F32)),
        grid=(L // tr,),
        in_specs=[pl.BlockSpec((tr, D), lambda i: (i, 0)), pl.BlockSpec((tr, 2 * D), lambda i: (i, 0)),
                  pl.BlockSpec((1, D), lambda i: (0, 0)), full3, full3, full3],
        out_specs=(pl.BlockSpec((tr, 2 * D), lambda i: (i, 0)), full3, full3,
                   pl.BlockSpec((SUBLANE, D), lambda i: (0, 0))),
        scratch_shapes=[pltpu.VMEM((tr, D), F32)],
        compiler_params=_params("arbitrary"))(dp, uv, vg, ws, wst, bsb)


def _gelu(x):
    return 0.5 * x * (1.0 + jnp.tanh(GELU_C * (x + GELU_A * x * x * x)))


def _gelu_grad(x):
    th = jnp.tanh(GELU_C * (x + GELU_A * x * x * x))
    return 0.5 * (1.0 + th) + 0.5 * x * (1.0 - th * th) * GELU_C * (1.0 + 3.0 * GELU_A * x * x)


def _cmul_add(xr, xi, ar, ai, br, bi):
    return xr + ar * br - ai * bi, xi + ar * bi + ai * br


def _cmul_conj_add(xr, xi, ar, ai, br, bi):
    return xr + ar * br + ai * bi, xi + ar * bi - ai * br


def _to_subchunk_order(src_ref, dst_ref, n):
    for k in range(n):
        dst_ref[pl.ds(SUBLANE * k, SUBLANE), :] = src_ref[pl.ds(k, SUBLANE, stride=n), :].astype(dst_ref.dtype)


def _to_time_order(src_ref, dst_ref, n):
    for m in range(n):
        r, k = divmod(SUBLANE * m, n)
        dst_ref[pl.ds(SUBLANE * m, SUBLANE), :] = src_ref[pl.ds(SUBLANE * k + r, SUBLANE, stride=SUBLANE), :]


def _s5_fwd(u, bre, bim, cre, cim, pw, pos, dsk, *, name, tc=S5_CHUNK):
    L, D = u.shape
    W = S5_LANES // S5_BLOCKS
    nt = L // tc
    n = tc // SUBLANE

    def sub(k):
        return pl.ds(SUBLANE * k, SUBLANE)

    def body(u_ref, bre_ref, bim_ref, cre_ref, cim_ref, pw_ref, pos_ref, d_ref, sre_ref, sim_ref, ypre_ref, yg_ref,
             carry, up, yp):
        t = pl.program_id(1)

        @pl.when(t == 0)
        def _():
            carry[...] = jnp.zeros_like(carry)

        _to_subchunk_order(u_ref, up, n)
        uv = up[...]
        ub = uv.astype(BF16)
        sre_ref[...] = jnp.dot(ub, bre_ref[...], preferred_element_type=F32)
        sim_ref[...] = jnp.dot(ub, bim_ref[...], preferred_element_type=F32)

        ar, ai = pw_ref[8], pw_ref[9]
        xr = jnp.zeros((SUBLANE, W), F32)
        xi = jnp.zeros((SUBLANE, W), F32)
        for k in range(n):
            xr, xi = _cmul_add(sre_ref[sub(k), :], sim_ref[sub(k), :], ar, ai, xr, xi)
            sre_ref[sub(k), :] = xr
            sim_ref[sub(k), :] = xi
        for q, d in enumerate((1, 2, 4)):
            xr, xi = _cmul_add(xr, xi, pw_ref[2 * q], pw_ref[2 * q + 1], pltpu.roll(xr, d, 0), pltpu.roll(xi, d, 0))
        cr, ci = carry[0], carry[1]
        xr, xi = _cmul_add(xr, xi, pw_ref[6], pw_ref[7], cr, ci)
        first = lax.broadcasted_iota(jnp.int32, (SUBLANE, W), 0) == 0
        er = jnp.where(first, cr, pltpu.roll(xr, 1, 0))
        ei = jnp.where(first, ci, pltpu.roll(xi, 1, 0))
        last = slice(SUBLANE - 1, SUBLANE)
        carry[0] = jnp.broadcast_to(xr[last, :], (SUBLANE, W))
        carry[1] = jnp.broadcast_to(xi[last, :], (SUBLANE, W))
        for k in range(n):
            sr, si = _cmul_add(sre_ref[sub(k), :], sim_ref[sub(k), :], pos_ref[0, k:k + 1, :], pos_ref[1, k:k + 1, :],
                               er, ei)
            sre_ref[sub(k), :] = sr
            sim_ref[sub(k), :] = si
        yp[...] = (jnp.dot(sre_ref[...].astype(BF16), cre_ref[...], preferred_element_type=F32)
                   - jnp.dot(sim_ref[...].astype(BF16), cim_ref[...], preferred_element_type=F32) + d_ref[...] * uv)
        _to_time_order(yp, ypre_ref, n)
        yg_ref[...] = _gelu(ypre_ref[...])

    ch = pl.BlockSpec((tc, LANE), lambda j, t: (t, j))
    st = pl.BlockSpec((tc, W), lambda j, t: (t, j))
    bsp = pl.BlockSpec((None, LANE, W), lambda j, t: (j, 0, 0))
    csp = pl.BlockSpec((None, W, LANE), lambda j, t: (j, 0, 0))
    return pl.pallas_call(
        body, name=name,
        out_shape=(jax.ShapeDtypeStruct((L, S5_LANES), F32), jax.ShapeDtypeStruct((L, S5_LANES), F32),
                   jax.ShapeDtypeStruct((L, D), F32), jax.ShapeDtypeStruct((L, D), F32)),
        grid=(S5_BLOCKS, nt),
        in_specs=[ch, bsp, bsp, csp, csp, pl.BlockSpec((10, SUBLANE, W), lambda j, t: (0, 0, j)),
                  pl.BlockSpec((2, n, W), lambda j, t: (0, 0, j)), pl.BlockSpec((1, LANE), lambda j, t: (0, j))],
        out_specs=(st, st, ch, ch),
        scratch_shapes=[pltpu.VMEM((2, SUBLANE, W), F32), pltpu.VMEM((tc, LANE), F32), pltpu.VMEM((tc, LANE), F32)],
        compiler_params=_params("parallel", "arbitrary"))(u, bre, bim, cre, cim, pw, pos, dsk)


def _s5_bwd(dy, u, sre, sim, bre, bim, cre, cim, pwr, posr, dsk, *, name, tc=S5_CHUNK):
    L, D = u.shape
    W = S5_LANES // S5_BLOCKS
    nt = L // tc
    n = tc // SUBLANE
    nt_dims = (((1,), (1,)), ((), ()))
    tn_dims = (((0,), (0,)), ((), ()))

    def sub(k):
        return pl.ds(SUBLANE * k, SUBLANE)

    def body(dy_ref, u_ref, sre_ref, sim_ref, bre_ref, bim_ref, cre_ref, cim_ref, pw_ref, pos_ref, d_ref,
             du_ref, dbre_ref, dbim_ref, dcre_ref, dcim_ref, ga_ref, dd_ref, gre, gim, carry, gacc, up, dyp):
        t = pl.program_id(1)

        @pl.when(t == 0)
        def _():
            for r in (carry, gacc, dbre_ref, dbim_ref, dcre_ref, dcim_ref, ga_ref, dd_ref):
                r[...] = jnp.zeros_like(r)

        _to_subchunk_order(dy_ref, dyp, n)
        _to_subchunk_order(u_ref, up, n)
        dyv, uv = dyp[...], up[...]
        dyb, ub = dyv.astype(BF16), uv.astype(BF16)
        gre[...] = lax.dot_general(dyb, cre_ref[...], nt_dims, preferred_element_type=F32)
        gim[...] = -lax.dot_general(dyb, cim_ref[...], nt_dims, preferred_element_type=F32)
        br, bi = pw_ref[8], pw_ref[9]
        xr = jnp.zeros((SUBLANE, W), F32)
        xi = jnp.zeros((SUBLANE, W), F32)
        for k in reversed(range(n)):
            xr, xi = _cmul_add(gre[sub(k), :], gim[sub(k), :], br, bi, xr, xi)
            gre[sub(k), :] = xr
            gim[sub(k), :] = xi
        for q, d in enumerate((1, 2, 4)):
            xr, xi = _cmul_add(xr, xi, pw_ref[2 * q], pw_ref[2 * q + 1], pltpu.roll(xr, SUBLANE - d, 0),
                               pltpu.roll(xi, SUBLANE - d, 0))
        cr, ci = carry[0], carry[1]
        xr, xi = _cmul_add(xr, xi, pw_ref[6], pw_ref[7], cr, ci)
        top = lax.broadcasted_iota(jnp.int32, (SUBLANE, W), 0) == SUBLANE - 1
        er = jnp.where(top, cr, pltpu.roll(xr, SUBLANE - 1, 0))
        ei = jnp.where(top, ci, pltpu.roll(xi, SUBLANE - 1, 0))
        carry[0] = jnp.broadcast_to(xr[0:1, :], (SUBLANE, W))
        carry[1] = jnp.broadcast_to(xi[0:1, :], (SUBLANE, W))
        nr, ni = er, ei
        acc_r = jnp.zeros((SUBLANE, W), F32)
        acc_i = jnp.zeros((SUBLANE, W), F32)
        for k in reversed(range(n)):
            place = slice(n - 1 - k, n - k)
            gr, gi = _cmul_conj_add(gre[sub(k), :], gim[sub(k), :], pos_ref[0, place, :], pos_ref[1, place, :], er, ei)
            gre[sub(k), :] = gr
            gim[sub(k), :] = gi
            sr, si = sre_ref[sub(k), :], sim_ref[sub(k), :]
            acc_r = acc_r + sr * nr + si * ni
            acc_i = acc_i + sr * ni - si * nr
            nr, ni = gr, gi
        gacc[0] += acc_r
        gacc[1] += acc_i
        grb, gib = gre[...].astype(BF16), gim[...].astype(BF16)
        dyp[...] = (lax.dot_general(grb, bre_ref[...], nt_dims, preferred_element_type=F32)
                    + lax.dot_general(gib, bim_ref[...], nt_dims, preferred_element_type=F32) + d_ref[...] * dyv)
        _to_time_order(dyp, up, n)
        du_ref[...] = up[...].astype(du_ref.dtype)
        dbre_ref[...] += lax.dot_general(ub, grb, tn_dims, preferred_element_type=F32)
        dbim_ref[...] += lax.dot_general(ub, gib, tn_dims, preferred_element_type=F32)
        dcre_ref[...] += lax.dot_general(sre_ref[...].astype(BF16), dyb, tn_dims, preferred_element_type=F32)
        dcim_ref[...] -= lax.dot_general(sim_ref[...].astype(BF16), dyb, tn_dims, preferred_element_type=F32)
        dd_ref[0:1, :] += jnp.sum(dyv * uv, axis=0, keepdims=True)

        @pl.when(t == nt - 1)
        def _():
            ga_ref[0:1, :] = jnp.sum(gacc[0], axis=0, keepdims=True)
            ga_ref[1:2, :] = jnp.sum(gacc[1], axis=0, keepdims=True)

    ch = pl.BlockSpec((tc, LANE), lambda j, t: (nt - 1 - t, j))
    st = pl.BlockSpec((tc, W), lambda j, t: (nt - 1 - t, j))
    bsp = pl.BlockSpec((None, LANE, W), lambda j, t: (j, 0, 0))
    csp = pl.BlockSpec((None, W, LANE), lambda j, t: (j, 0, 0))
    return pl.pallas_call(
        body, name=name,
        out_shape=(jax.ShapeDtypeStruct((L, D), BF16),
                   jax.ShapeDtypeStruct((S5_BLOCKS, LANE, W), F32), jax.ShapeDtypeStruct((S5_BLOCKS, LANE, W), F32),
                   jax.ShapeDtypeStruct((S5_BLOCKS, W, LANE), F32), jax.ShapeDtypeStruct((S5_BLOCKS, W, LANE), F32),
                   jax.ShapeDtypeStruct((SUBLANE, S5_LANES), F32), jax.ShapeDtypeStruct((SUBLANE, D), F32)),
        grid=(S5_BLOCKS, nt),
        in_specs=[ch, ch, st, st, bsp, bsp, csp, csp, pl.BlockSpec((10, SUBLANE, W), lambda j, t: (0, 0, j)),
                  pl.BlockSpec((2, n, W), lambda j, t: (0, 0, j)), pl.BlockSpec((1, LANE), lambda j, t: (0, j))],
        out_specs=(ch, bsp, bsp, csp, csp, pl.BlockSpec((SUBLANE, W), lambda j, t: (0, j)),
                   pl.BlockSpec((SUBLANE, LANE), lambda j, t: (0, j))),
        scratch_shapes=[pltpu.VMEM((tc, W), F32), pltpu.VMEM((tc, W), F32), pltpu.VMEM((2, SUBLANE, W), F32),
                        pltpu.VMEM((2, SUBLANE, W), F32), pltpu.VMEM((tc, LANE), F32), pltpu.VMEM((tc, LANE), F32)],
        compiler_params=_params("parallel", "arbitrary"))(dy, u, sre, sim, bre, bim, cre, cim, pwr, posr, dsk)


def _glu_bwd(dy2, y, t, *, name, tm=256):
    L, D = y.shape

    def body(dy2_ref, y_ref, t_ref, dt_ref, dya_ref, st_ref):
        i = pl.program_id(0)

        @pl.when(i == 0)
        def _():
            st_ref[...] = jnp.zeros_like(st_ref)

        sig = 1.0 / (1.0 + jnp.exp(-t_ref[...]))
        dy2v = dy2_ref[...]
        dt = dy2v * y_ref[...] * sig * (1.0 - sig)
        dt_ref[...] = dt.astype(dt_ref.dtype)
        dya_ref[...] = dy2v * sig
        st_ref[0:1, :] += jnp.sum(dt, axis=0, keepdims=True)

    row = pl.BlockSpec((tm, D), lambda i: (i, 0))
    return pl.pallas_call(
        body, name=name,
        out_shape=(jax.ShapeDtypeStruct((L, D), BF16), jax.ShapeDtypeStruct((L, D), F32),
                   jax.ShapeDtypeStruct((SUBLANE, D), F32)),
        grid=(L // tm,), in_specs=[row, row, row],
        out_specs=(row, row, pl.BlockSpec((SUBLANE, D), lambda i: (0, 0))),
        compiler_params=_params("arbitrary"))(dy2, y, t)


def _s5_prep(a_re, a_im, log_dt, b_re, b_im, c_re, c_im):
    dt = jnp.exp(log_dt)[:, None]
    mag = jnp.exp(a_re * dt)
    abar_re = mag * jnp.cos(a_im * dt)
    abar_im = mag * jnp.sin(a_im * dt)
    den = a_re * a_re + a_im * a_im
    nr = abar_re - 1.0
    ni = abar_im
    f_re = ((nr * a_re + ni * a_im) / den)[..., None]
    f_im = ((ni * a_re - nr * a_im) / den)[..., None]
    bbar_re = f_re * b_re - f_im * b_im
    bbar_im = f_re * b_im + f_im * b_re
    eye = jnp.eye(S5_GROUPS // S5_BLOCKS, dtype=F32)
    gb = S5_GROUPS // S5_BLOCKS

    def blk_b(bb):
        t = bb.reshape(S5_BLOCKS, gb, S5_STATE, S5_GROUP)
        return jnp.einsum('jgph,gk->jghkp', t, eye).reshape(S5_BLOCKS, gb * S5_GROUP, gb * S5_STATE)

    def blk_c(cc):
        t = cc.reshape(S5_BLOCKS, gb, S5_GROUP, S5_STATE)
        return jnp.einsum('jghp,gk->jgpkh', t, eye).reshape(S5_BLOCKS, gb * S5_STATE, gb * S5_GROUP)

    return (abar_re.reshape(1, S5_LANES), abar_im.reshape(1, S5_LANES), blk_b(bbar_re), blk_b(bbar_im),
            blk_c(c_re), blk_c(c_im))


def _cpowers(ar, ai, count):
    pr, pi, m = ar, ai, 1
    while m < count:
        tr, ti = pr[m - 1:m], pi[m - 1:m]
        pr, pi = jnp.concatenate([pr, pr * tr - pi * ti], 0), jnp.concatenate([pi, pr * ti + pi * tr], 0)
        m *= 2
    return pr, pi


def _s5_power_tables(ar, ai, n):
    pr, pi = _cpowers(ar, ai, n)
    qr, qi = _cpowers(pr[n - 1:n], pi[n - 1:n], SUBLANE)
    row = jnp.arange(SUBLANE)[:, None]
    lanes = ar.shape[1]

    def tables(sign, keep, order):
        out = []
        for d in (1, 2, 4):
            out += [jnp.where(keep(d), qr[d - 1:d], 0.0), jnp.where(keep(d), sign * qi[d - 1:d], 0.0)]
        out += [jnp.concatenate([qr[r:r + 1] for r in order], 0), sign * jnp.concatenate([qi[r:r + 1] for r in order], 0),
                ar, sign * ai]
        return jnp.stack([jnp.broadcast_to(o, (SUBLANE, lanes)) for o in out])

    fwd = tables(1.0, lambda d: row >= d, list(range(SUBLANE)))
    rev = tables(-1.0, lambda d: row + d <= SUBLANE - 1, list(reversed(range(SUBLANE))))
    return fwd, rev, jnp.stack([pr, pi])


ADAMW_PART_BLOCK_BYTES = 2 * 1024 * 1024


def _adamw(w, parts, m, v, *, name):
    n, R, C = w.shape
    assert len(parts) == n
    P = parts[0].shape[0]
    tr = R
    while P * tr * C * parts[0].dtype.itemsize > ADAMW_PART_BLOCK_BYTES and tr % 16 == 0:
        tr //= 2
    c1 = 1.0 / (1.0 - ADAM_B1 ** ADAM_STEP)
    c2 = 1.0 / (1.0 - ADAM_B2 ** ADAM_STEP)

    def body(*refs):
        w_ref, m_ref, v_ref = refs[:3]
        p_refs = refs[3:3 + n]
        g_ref, d_ref, nm_ref, nv_ref = refs[3 + n:]
        layer = pl.program_id(0)
        for q, p_ref in enumerate(p_refs):
            @pl.when(layer == q)
            def _(p_ref=p_ref):
                g = p_ref[0].astype(F32)
                for s in range(1, P):
                    g = g + p_ref[s].astype(F32)
                nm = ADAM_B1 * m_ref[...] + (1.0 - ADAM_B1) * g
                nv = ADAM_B2 * v_ref[...] + (1.0 - ADAM_B2) * (g * g)
                g_ref[...] = g
                nm_ref[...] = nm
                nv_ref[...] = nv
                d_ref[...] = -ADAM_LR * ((nm * c1) / (jnp.sqrt(nv * c2) + ADAM_EPS) + ADAM_WD * w_ref[...])

    row = pl.BlockSpec((None, tr, C), lambda l, i: (l, i, 0))
    part_specs = [pl.BlockSpec((P, tr, C), lambda l, i, q=q: (0, jnp.where(l == q, i, 0), 0)) for q in range(n)]
    out = jax.ShapeDtypeStruct((n, R, C), F32)
    return pl.pallas_call(body, name=name, out_shape=(out, out, out, out), grid=(n, R // tr),
                          in_specs=[row, row, row] + part_specs, out_specs=(row, row, row, row),
                          compiler_params=_params("arbitrary", "arbitrary"))(w, m, v, *parts)


def _all_gather(xs, axis, *, name):
    m = xs.shape[axis]
    out_shape = list(xs.shape)
    out_shape[axis] = N_DEV * m

    def body(x_ref, out_ref, send_sems, recv_sems, local_sem):
        x, y, c = _my_pos()
        me, sibling = (x, y, c), (x, y, 1 - c)
        chips = [(1 - x, y), (x, 1 - y), (1 - x, 1 - y)]

        def blk(px, py, pc):
            idx = [slice(None)] * 3
            idx[axis] = pl.ds((4 * px + 2 * py + pc) * m, m)
            return out_ref.at[tuple(idx)]

        def copy(k, block, to, src=None):
            return pltpu.make_async_remote_copy(src_ref=blk(*block) if src is None else src, dst_ref=blk(*block),
                                                send_sem=send_sems.at[k], recv_sem=recv_sems.at[k],
                                                device_id=to, device_id_type=MESH_ID)

        mine = pltpu.make_async_copy(x_ref, blk(*me), local_sem)
        mine.start()
        first = [copy(0, me, sibling, src=x_ref)]
        first += [copy(1 + j, me, (*chip, c), src=x_ref) for j, chip in enumerate(chips)]
        for cp in first:
            cp.start()
        passed = [copy(4 + j, (*chip, c), sibling) for j, chip in enumerate(chips)]
        for j, chip in enumerate(chips):
            copy(1 + j, (*chip, c), me).wait_recv()
            passed[j].start()
        copy(0, sibling, me).wait_recv()
        for j, chip in enumerate(chips):
            copy(4 + j, (*chip, 1 - c), me).wait_recv()
        for cp in first + passed:
            cp.wait_send()
        mine.wait()

    hbm = pl.BlockSpec(memory_space=pl.ANY)
    return pl.pallas_call(body, name=name, out_shape=jax.ShapeDtypeStruct(tuple(out_shape), xs.dtype),
                          in_specs=[hbm], out_specs=hbm,
                          scratch_shapes=[pltpu.SemaphoreType.DMA((N_DEV - 1,)), pltpu.SemaphoreType.DMA((N_DEV - 1,)),
                                          pltpu.SemaphoreType.DMA],
                          compiler_params=pltpu.CompilerParams(has_side_effects=True))(xs)


NEAR_PEERS = (1, 2, 4, 6)
RELAY_PEERS = (2, 4, 6)


def _block(ref, axis, idx, m):
    return ref.at[pl.ds(idx * m, m), :] if axis == 0 else ref.at[:, pl.ds(idx * m, m)]


def _exchange_copies(metas, src_refs, zone_refs, send_sems, recv_sems, base, phase):
    x, y, c = _my_pos()
    me = 4 * x + 2 * y + c

    def place(r):
        pos = (1 - x if r & 4 else x, 1 - y if r & 2 else y, 1 - c if r & 1 else c)
        return pos, 4 * pos[0] + 2 * pos[1] + pos[2]

    def copies(r, to, src, dst, arrival):
        return tuple(pltpu.make_async_remote_copy(src_ref=src, dst_ref=d, send_sem=send_sems.at[base + r - 1],
                                                  recv_sem=recv_sems.at[base + r - 1], device_id=to,
                                                  device_id_type=MESH_ID) for d in (dst, arrival))

    pairs, own = [], []
    if phase == 'relay':
        sibling, _ = place(1)
        for r in RELAY_PEERS:
            held, comes = place(r)[1], place(r | 1)[1]
            for (kind, axis, m), z_ref in zip(metas, zone_refs):
                pairs.append(copies(r, sibling, _block(z_ref, axis, held, m), _block(z_ref, axis, held, m),
                                    _block(z_ref, axis, comes, m)))
        return pairs, own
    for r in (NEAR_PEERS if phase == 'near' else range(1, N_DEV)):
        pos, peer = place(r)
        for (kind, axis, m), s_ref, z_ref in zip(metas, src_refs, zone_refs):
            if kind == 'gather':
                pairs.append(copies(r, pos, s_ref, _block(z_ref, axis, me, m), _block(z_ref, axis, peer, m)))
            else:
                pairs.append(copies(r, pos, _block(s_ref, axis, peer, m), z_ref.at[me], z_ref.at[peer]))
    for (kind, axis, m), s_ref, z_ref in zip(metas, src_refs, zone_refs):
        src, dst = (s_ref, _block(z_ref, axis, me, m)) if kind == 'gather' else (_block(s_ref, axis, me, m), z_ref.at[me])
        own.append(pltpu.make_async_copy(src, dst, recv_sems.at[base + N_DEV - 1]))
    return pairs, own


def _exchange_start(groups, after, *, name, relayed=False):
    flat = [it for g in groups for it in g]
    n, ng = len(flat), len(groups)
    metas = [it[2] for it in flat]
    bounds = [(sum(len(g) for g in groups[:q]), sum(len(g) for g in groups[:q + 1])) for q in range(ng)]
    phase = 'near' if relayed else 'all'

    def body(*refs):
        src_refs = refs[:n]
        send_sems, recv_sems = refs[n + 1], refs[n + 2]
        zone_refs = refs[2 * n + 3:3 * n + 3]
        token = refs[-1]
        for q, (lo, hi) in enumerate(bounds):
            pairs, own = _exchange_copies(metas[lo:hi], src_refs[lo:hi], zone_refs[lo:hi], send_sems, recv_sems,
                                          q * N_DEV, phase)
            for outgoing, _ in pairs:
                outgoing.start()
            for cp in own:
                cp.start()
        token[...] = jnp.zeros_like(token)

    hbm = pl.BlockSpec(memory_space=pltpu.HBM)
    sem = pl.BlockSpec(memory_space=pltpu.SEMAPHORE)
    srcs = [it[0] for it in flat]
    res = pl.pallas_call(
        body, name=name,
        out_shape=(pltpu.SemaphoreType.DMA((ng * N_DEV,)), pltpu.SemaphoreType.DMA((ng * N_DEV,)),
                   *[pltpu.HBM(a.shape, a.dtype) for a in srcs], *[pltpu.HBM(it[1], it[0].dtype) for it in flat],
                   jax.ShapeDtypeStruct((SUBLANE, LANE), F32)),
        in_specs=[hbm] * n + [pl.BlockSpec(memory_space=pl.ANY)],
        out_specs=(sem, sem, *[hbm] * (2 * n), pl.BlockSpec(memory_space=pltpu.VMEM)),
        input_output_aliases={q: 2 + q for q in range(n)},
        compiler_params=pltpu.CompilerParams(has_side_effects=pltpu.SideEffectType.DATAFLOW_SIDE_EFFECTING),
    )(*[pltpu.with_memory_space_constraint(a, pltpu.HBM) for a in srcs], after)
    handles = [(res[0], res[1], q * N_DEV, phase, list(res[2 + lo:2 + hi]), list(res[2 + n + lo:2 + n + hi]),
                metas[lo:hi]) for q, (lo, hi) in enumerate(bounds)]
    return handles, res[-1]


def _exchange_wait(handle, after, *, name):
    send_sems, recv_sems, base, phase, srcs, zones, metas = handle
    ns, nz = len(srcs), len(zones)

    def body(*refs):
        src_refs, zone_refs = refs[:ns], refs[ns:ns + nz]
        s_sems, r_sems = refs[ns + nz], refs[ns + nz + 1]
        pairs, own = _exchange_copies(metas, src_refs, zone_refs, s_sems, r_sems, base, phase)
        for outgoing, incoming in pairs:
            outgoing.wait_send()
            incoming.wait_recv()
        for cp in own:
            cp.wait()

    hbm = pl.BlockSpec(memory_space=pltpu.HBM)
    sem = pl.BlockSpec(memory_space=pltpu.SEMAPHORE)
    arrays = srcs + zones
    res = pl.pallas_call(
        body, name=name,
        out_shape=tuple(pltpu.HBM(a.shape, a.dtype) for a in arrays),
        in_specs=[hbm] * (ns + nz) + [sem, sem, pl.BlockSpec(memory_space=pl.ANY)],
        out_specs=tuple([hbm] * (ns + nz)),
        input_output_aliases={q: q for q in range(ns + nz)},
        compiler_params=pltpu.CompilerParams(has_side_effects=pltpu.SideEffectType.DATAFLOW_SIDE_EFFECTING),
    )(*arrays, send_sems, recv_sems, after)
    return list(res[ns:])


def _exchange_relay(handle, after, *, name):
    metas = handle[6]
    zones = _exchange_wait(handle, after, name=name + "_in")
    nz = len(zones)

    def body(*refs):
        zone_refs = refs[:nz]
        send_sems, recv_sems = refs[nz], refs[nz + 1]
        pairs, _ = _exchange_copies(metas, (), zone_refs, send_sems, recv_sems, 0, 'relay')
        for outgoing, _ in pairs:
            outgoing.start()

    hbm = pl.BlockSpec(memory_space=pltpu.HBM)
    sem = pl.BlockSpec(memory_space=pltpu.SEMAPHORE)
    res = pl.pallas_call(
        body, name=name + "_out",
        out_shape=(pltpu.SemaphoreType.DMA((N_DEV,)), pltpu.SemaphoreType.DMA((N_DEV,)),
                   *[pltpu.HBM(z.shape, z.dtype) for z in zones]),
        in_specs=[hbm] * nz, out_specs=(sem, sem, *[hbm] * nz),
        input_output_aliases={q: 2 + q for q in range(nz)},
        compiler_params=pltpu.CompilerParams(has_side_effects=pltpu.SideEffectType.DATAFLOW_SIDE_EFFECTING),
    )(*zones)
    return res[0], res[1], 0, 'relay', [], list(res[2:]), metas


def _pad_rows(a, rows):
    return jnp.pad(a, ((0, rows - a.shape[0]), (0, 0)))


def _rows(a):
    flat = a.reshape(-1).astype(F32)
    pad = -flat.shape[0] % (SUBLANE * LANE)
    return (jnp.pad(flat, (0, pad)) if pad else flat).reshape(-1, LANE)


def _pack_rows(arrays):
    return jnp.concatenate([_rows(a) for a in arrays], 0)


def _unpack_rows(t, shapes):
    out, off = [], 0
    for shp in shapes:
        size = math.prod(shp)
        rows = -(-size // (SUBLANE * LANE)) * SUBLANE
        out.append(t[off:off + rows].reshape(-1)[:size].reshape(shp))
        off += rows
    return out


def _stat_row(st, r):
    return st[r:r + 1, :]


def kernel(x, c, ada_w, ada_b, norm1_g, norm2_g, ff_w1, ff_w2, final_g, conv_w_in, conv_w, conv_b, conv_w_out, ssm_w_in, ssm_a_re, ssm_a_im, ssm_log_dt, ssm_b_re, ssm_b_im, ssm_c_re, ssm_c_im, ssm_d, ssm_glu_w, ssm_glu_b, ssm_w_out, sg_w_in, sg_v_g, sg_w_s, sg_b_s, sg_w_out, loss_target, m_ada_w, m_ada_b, m_norm1_g, m_norm2_g, m_ff_w1, m_ff_w2, m_final_g, m_conv_w_in, m_conv_w, m_conv_b, m_conv_w_out, m_ssm_w_in, m_ssm_a_re, m_ssm_a_im, m_ssm_log_dt, m_ssm_b_re, m_ssm_b_im, m_ssm_c_re, m_ssm_c_im, m_ssm_d, m_ssm_glu_w, m_ssm_glu_b, m_ssm_w_out, m_sg_w_in, m_sg_v_g, m_sg_w_s, m_sg_b_s, m_sg_w_out, v_ada_w, v_ada_b, v_norm1_g, v_norm2_g, v_ff_w1, v_ff_w2, v_final_g, v_conv_w_in, v_conv_w, v_conv_b, v_conv_w_out, v_ssm_w_in, v_ssm_a_re, v_ssm_a_im, v_ssm_log_dt, v_ssm_b_re, v_ssm_b_im, v_ssm_c_re, v_ssm_c_im, v_ssm_d, v_ssm_glu_w, v_ssm_glu_b, v_ssm_w_out, v_sg_w_in, v_sg_v_g, v_sg_w_s, v_sg_b_s, v_sg_w_out):
    P = dict(zip(INPUTS, (x, c, ada_w, ada_b, norm1_g, norm2_g, ff_w1, ff_w2, final_g, conv_w_in, conv_w, conv_b, conv_w_out, ssm_w_in, ssm_a_re, ssm_a_im, ssm_log_dt, ssm_b_re, ssm_b_im, ssm_c_re, ssm_c_im, ssm_d, ssm_glu_w, ssm_glu_b, ssm_w_out, sg_w_in, sg_v_g, sg_w_s, sg_b_s, sg_w_out, loss_target, m_ada_w, m_ada_b, m_norm1_g, m_norm2_g, m_ff_w1, m_ff_w2, m_final_g, m_conv_w_in, m_conv_w, m_conv_b, m_conv_w_out, m_ssm_w_in, m_ssm_a_re, m_ssm_a_im, m_ssm_log_dt, m_ssm_b_re, m_ssm_b_im, m_ssm_c_re, m_ssm_c_im, m_ssm_d, m_ssm_glu_w, m_ssm_glu_b, m_ssm_w_out, m_sg_w_in, m_sg_v_g, m_sg_w_s, m_sg_b_s, m_sg_w_out, v_ada_w, v_ada_b, v_norm1_g, v_norm2_g, v_ff_w1, v_ff_w2, v_final_g, v_conv_w_in, v_conv_w, v_conv_b, v_conv_w_out, v_ssm_w_in, v_ssm_a_re, v_ssm_a_im, v_ssm_log_dt, v_ssm_b_re, v_ssm_b_im, v_ssm_c_re, v_ssm_c_im, v_ssm_d, v_ssm_glu_w, v_ssm_glu_b, v_ssm_w_out, v_sg_w_in, v_sg_v_g, v_sg_w_s, v_sg_b_s, v_sg_w_out)))
    L, D = x.shape[1], x.shape[2]
    me = _my_index()
    xs = x[0]
    tgt = loss_target[0]
    n_conv = conv_w_in.shape[0]

    def gather_item(shard, axis):
        full = tuple(N_DEV * s if a == axis else s for a, s in enumerate(shard.shape))
        return shard, full, ('gather', axis, shard.shape[axis])

    def mixer_shards(i):
        kind, j = i % 3, i // 3
        if kind == 0:
            return [(conv_w_in[j], 1), (conv_w_out[j], 0)]
        if kind == 1:
            return [(ssm_w_in[j], 0), (ssm_glu_w[j], 0), (ssm_w_out[j], 0)]
        return [(sg_w_in[j], 1), (sg_w_out[j], 0)]

    c_act = c * (1.0 / (1.0 + jnp.exp(-c)))
    vec_rows = jnp.concatenate([c_act.reshape(D // LANE, LANE), conv_w.reshape(-1, LANE), conv_b.reshape(-1, LANE),
                                sg_v_g.reshape(-1, LANE)], 0)
    n_vec = vec_rows.shape[0]
    vec_all = _all_gather(_pad_rows(vec_rows, 24)[None], 0, name="gather_vectors")
    c_all = vec_all[:, :D // LANE, :].reshape(N_DEV, D)
    sharded_full = vec_all[:, D // LANE:n_vec, :].transpose(1, 0, 2).reshape(n_vec - D // LANE, D)
    conv_w_full = sharded_full[:3 * n_conv].reshape(n_conv, 3, D)
    conv_b_full = sharded_full[3 * n_conv:4 * n_conv]
    sg_vg_full = sharded_full[4 * n_conv:4 * n_conv + 1]

    c_pad = _pad_rows(c_all, LANE)
    ncol = ada_w.shape[2]
    mod_part = jnp.stack([_mm(c_pad, ada_w[i], name=f"ada_fwd{i}")[:N_DEV] for i in range(DEPTH)])
    mod_all = _all_gather(mod_part.reshape(1, DEPTH * N_DEV, ncol), 0, name="gather_mod")
    mod_all = mod_all.reshape(N_DEV, DEPTH, N_DEV, ncol)
    mod_me = lax.dynamic_index_in_dim(mod_all, me, 2, keepdims=False)
    mod = mod_me.transpose(1, 0, 2).reshape(DEPTH, N_DEV * ncol) + ada_b
    gathers, gather_token = _exchange_start(
        [[gather_item(w.astype(BF16), ax) for w, ax in shards]
         for i in range(DEPTH) for shards in (mixer_shards(i), [(ff_w1[i], 1), (ff_w2[i], 0)])],
        mod, name="gather_start", relayed=True)
    mod = mod + gather_token[0:1, 0:1]
    relayed = [None] * len(gathers)
    relayed[0] = _exchange_relay(gathers[0], mod, name="gather_mix_relay0")

    s5_args = (ssm_a_re[0], ssm_a_im[0], ssm_log_dt[0], ssm_b_re[0], ssm_b_im[0], ssm_c_re[0], ssm_c_im[0])
    (abar_re, abar_im, bblk_re, bblk_im, cblk_re, cblk_im), s5_vjp = jax.vjp(_s5_prep, *s5_args)
    pw_fwd, pw_rev, pos_fwd = _s5_power_tables(abar_re, abar_im, S5_CHUNK // SUBLANE)
    s5_w = tuple(t.astype(BF16) for t in (bblk_re, bblk_im, cblk_re, cblk_im))
    causal = jnp.tril(jnp.ones((SG_CHUNK, SG_CHUNK), dtype=bool))
    ws_m = jnp.where(causal[None], sg_w_s[0], 0.0)
    ws_b = ws_m.astype(BF16)
    wst_b = ws_m.transpose(0, 2, 1).astype(BF16)
    bsb = jnp.broadcast_to(sg_b_s[0][:, :, None], (SG_HEADS, SG_CHUNK, LANE))

    saved = []
    xa = xs
    mods = [[mod[i:i + 1, q * D:(q + 1) * D] for q in range(6)] for i in range(DEPTH)]
    wn1s = [norm1_g[i:i + 1] * (1.0 + mods[i][1]) for i in range(DEPTH)]
    h1 = _normmod_fwd(xa, wn1s[0], mods[0][0], name="norm1_fwd0")
    for i in range(DEPTH):
        kind, j = i % 3, i // 3
        sh1, sc1, g1, sh2, sc2, g2 = mods[i]
        wn1 = wn1s[i]
        wn2 = norm2_g[i:i + 1] * (1.0 + sc2)
        S = dict(x_in=xa, g1=g1, g2=g2, sc1=sc1, sc2=sc2, wn1=wn1, wn2=wn2)
        w_mix = _exchange_wait(relayed[2 * i], h1, name=f"gather_mix_wait{i}")
        S['h1'] = h1
        if kind == 0:
            bcx = _mm(h1, w_mix[0], name=f"conv_in{i}", out_dtypes=(BF16,), bm=2048)
            wb = _pad_rows(jnp.concatenate([conv_w_full[j], conv_b_full[j:j + 1]], 0), SUBLANE)
            pb = _conv_fwd(bcx, wb, name=f"conv_mix{i}")
            S.update(bcx=bcx, wb=wb, pb=pb)
        elif kind == 1:
            u = _mm(h1, w_mix[0], name=f"ssm_in{i}")
            sre, sim, ypre, yg = _s5_fwd(u, *s5_w, pw_fwd, pos_fwd, ssm_d, name=f"s5_scan{i}")

            def glu_epi(acc, yv, bias):
                t = acc + bias
                return yv * (1.0 / (1.0 + jnp.exp(-t))), t

            pb, tt = _mm(yg, w_mix[1], name=f"ssm_glu{i}", out_dtypes=(BF16, F32), epi=glu_epi,
                         extras=[(yg, 'mn'), (ssm_glu_b, 'n')])
            S.update(u=u, sre=sre, sim=sim, ypre=ypre, yg=yg, pb=pb, tt=tt)
        else:
            uv = _mm(h1, w_mix[0], name=f"sg_in{i}", bm=2048)
            pb = _sg_fwd(uv, sg_vg_full, ws_b, bsb, name=f"sg_mix{i}")
            S.update(uv=uv, pb=pb)
        relayed[2 * i + 1] = _exchange_relay(gathers[2 * i + 1], pb, name=f"gather_ff_relay{i}")
        x_mid, y_mix, h2 = _mm(pb, w_mix[-1], name=f"mix_out{i}", out_dtypes=(F32, BF16, BF16), epi=_epi_residual_norm,
                               extras=[(xa, 'mn'), (g1, 'n'), (wn2, 'n'), (sh2, 'n')])
        w1_full, w2_full = _exchange_wait(relayed[2 * i + 1], h2, name=f"gather_ff_wait{i}")
        S.update(w_mix=w_mix, w1=w1_full, w2=w2_full)
        ra = _mm(h2, w1_full, name=f"ff_up{i}", out_dtypes=(BF16,), epi=lambda acc: (jnp.maximum(acc, 0.0),), bm=2048)
        if i + 1 < DEPTH:
            relayed[2 * i + 2] = _exchange_relay(gathers[2 * i + 2], ra, name=f"gather_mix_relay{i + 1}")
            xa, f_out, h1 = _mm(ra, w2_full, name=f"ff_down{i}", out_dtypes=(F32, BF16, BF16), a_fn=_square,
                                epi=_epi_residual_norm, bm=256, bk=w2_full.shape[0],
                                extras=[(x_mid, 'mn'), (g2, 'n'), (wn1s[i + 1], 'n'), (mods[i + 1][0], 'n')])
        else:
            xa, f_out = _mm(ra, w2_full, name=f"ff_down{i}", out_dtypes=(F32, BF16), epi=_epi_residual, a_fn=_square,
                            extras=[(x_mid, 'mn'), (g2, 'n')], bm=256, bk=w2_full.shape[0])
        S.update(x_mid=x_mid, y_mix=y_mix, h2=h2, ra=ra, f_out=f_out)
        saved.append(S)

    S = saved[-1]
    dx, st, dfb, loss_tile = _loss_head(xa, tgt, final_g[None], S['f_out'], S['g2'], name="loss_head")
    d_final_g = _stat_row(st, 0)
    dg2_next = _stat_row(st, 2)

    def scatter_item(g, axis):
        m = g.shape[axis] // N_DEV
        shard = tuple(m if a == axis else s for a, s in enumerate(g.shape))
        return g, (N_DEV,) + shard, ('scatter', axis, m)

    dmod = [None] * DEPTH
    dn1g, dn2g = [None] * DEPTH, [None] * DEPTH
    d_conv_w, d_conv_b = [None] * n_conv, [None] * n_conv
    ff_sent, mix_sent = [None] * DEPTH, [None] * DEPTH
    small = {}
    for i in reversed(range(DEPTH)):
        kind, j = i % 3, i // 3
        S = saved[i]
        w_mix = S['w_mix']
        dg2 = dg2_next
        da = _mm(dfb, S['w2'], tb=True, name=f"ff_down_bwd{i}", out_dtypes=(BF16,), bm=2048,
                 epi=lambda acc, rav: (acc * (2.0 * rav.astype(F32)),), extras=[(S['ra'], 'mn')])
        dw2 = _wgrad(S['ra'], dfb, name=f"ff_w2_grad{i}", a_fn=_square, bm=256, bn=1024)
        dh2 = _mm(da, S['w1'], tb=True, name=f"ff_up_bwd{i}", out_dtypes=(BF16,), bm=512, bk=da.shape[1])
        dw1 = _wgrad(S['h2'], da, name=f"ff_w1_grad{i}")
        (ff_sent[i],), token = _exchange_start([[scatter_item(dw1, 1), scatter_item(dw2, 0)]], dx,
                                               name=f"ff_grads_start{i}")
        dx_mid, st2, dyb = _normmod_bwd(dh2, S['x_mid'], S['wn2'] + token[0:1, 0:1], dx,
                                        (S['y_mix'], S['g1']), name=f"norm2_bwd{i}")
        dsc2 = _stat_row(st2, 0) * norm2_g[i:i + 1]
        dn2g[i] = _stat_row(st2, 0) * (1.0 + S['sc2'])
        dsh2 = _stat_row(st2, 1)
        dg1 = _stat_row(st2, 2)
        if kind == 0:
            dp = _mm(dyb, w_mix[1], tb=True, name=f"conv_out_bwd{i}", out_dtypes=(BF16,))
            d_cwo = _wgrad(S['pb'], dyb, name=f"conv_w_out_grad{i}")
            dbcx, stc = _conv_bwd(dp, S['bcx'], S['wb'], name=f"conv_mix_bwd{i}")
            d_conv_w[j] = stc[0:3]
            d_conv_b[j] = stc[3:4]
            dh1 = _mm(dbcx, w_mix[0], tb=True, name=f"conv_in_bwd{i}", out_dtypes=(BF16,), bm=512)
            d_cwi = _wgrad(S['h1'], dbcx, name=f"conv_w_in_grad{i}")
            mix_grads = [scatter_item(d_cwi, 1), scatter_item(d_cwo, 0)]
        elif kind == 1:
            dy2 = _mm(dyb, w_mix[2], tb=True, name=f"ssm_out_bwd{i}")
            d_ssm_out = _wgrad(S['pb'], dyb, name=f"ssm_w_out_grad{i}")
            dtb, dya, stg = _glu_bwd(dy2, S['yg'], S['tt'], name=f"ssm_glu_bwd{i}")
            dypre = _mm(dtb, w_mix[1], tb=True, name=f"ssm_glu_in_bwd{i}",
                        epi=lambda acc, a, yp: ((a + acc) * _gelu_grad(yp),),
                        extras=[(dya, 'mn'), (S['ypre'], 'mn')])
            d_glu = _wgrad(S['yg'], dtb, name=f"ssm_glu_w_grad{i}", bm=512)
            dub, dbre, dbim, dcre, dcim, ga, dd = _s5_bwd(dypre, S['u'], S['sre'], S['sim'], *s5_w, pw_rev, pos_fwd, ssm_d,
                                                           name=f"s5_scan_bwd{i}")
            dh1 = _mm(dub, w_mix[0], tb=True, name=f"ssm_in_bwd{i}", out_dtypes=(BF16,))
            d_ssm_in = _wgrad(S['h1'], dub, name=f"ssm_w_in_grad{i}")
            da_re, da_im, dlog_dt, db_re, db_im, dc_re, dc_im = s5_vjp((ga[0:1], ga[1:2], dbre, dbim, dcre, dcim))
            s5_small = _pack_rows([da_re, da_im, dlog_dt, db_re, db_im, dc_re, dc_im, dd[0], stg[0]])
            mix_grads = [scatter_item(d_ssm_in, 0), scatter_item(d_glu, 0), scatter_item(d_ssm_out, 0),
                         gather_item(s5_small, 0)]
        else:
            dp = _mm(dyb, w_mix[1], tb=True, name=f"sg_out_bwd{i}")
            d_sgo = _wgrad(S['pb'], dyb, name=f"sg_w_out_grad{i}")
            duv, dws, dbs, stv = _sg_bwd(dp, S['uv'], sg_vg_full, ws_b, wst_b, bsb, name=f"sg_mix_bwd{i}")
            dh1 = _mm(duv, w_mix[0], tb=True, name=f"sg_in_bwd{i}", out_dtypes=(BF16,), bm=512, bk=duv.shape[1])
            d_sgi = _wgrad(S['h1'], duv, name=f"sg_w_in_grad{i}")
            sg_small = _pack_rows([jnp.where(causal[None], dws, 0.0), jnp.sum(dbs, axis=-1)])
            d_sg_vg = stv[0:1]
            mix_grads = [scatter_item(d_sgi, 1), scatter_item(d_sgo, 0), gather_item(sg_small, 0)]
        wn1 = S['wn1']
        if i > 0:
            (mix_sent[i],), token = _exchange_start([mix_grads], dx_mid, name=f"mix_grads_start{i}")
            wn1 = wn1 + token[0:1, 0:1]
            prev = saved[i - 1]
            dx, st1, dfb = _normmod_bwd(dh1, S['x_in'], wn1, dx_mid, (prev['f_out'], prev['g2']),
                                        name=f"norm1_bwd{i}")
            dg2_next = _stat_row(st1, 2)
        else:
            dx, st1 = _normmod_bwd(dh1, S['x_in'], wn1, dx_mid, None, name=f"norm1_bwd{i}")
        dsc1 = _stat_row(st1, 0) * norm1_g[i:i + 1]
        dn1g[i] = _stat_row(st1, 0) * (1.0 + S['sc1'])
        dsh1 = _stat_row(st1, 1)
        dmod[i] = jnp.concatenate([dsh1, dsc1, dg1, dsh2, dsc2, dg2], 1)
    grad_x = dx[None]

    out = {}

    def small_group(names, parts, label):
        shapes = [P[n].shape for n in names]
        w, m, v = (_pack_rows([P[pre + n] for n in names])[None] for pre in ('', 'm_', 'v_'))
        res = [_unpack_rows(t[0], shapes) for t in _adamw(w, [parts], m, v, name=label)]
        for q, n in enumerate(names):
            out[n] = tuple(r[q] for r in res)

    small.update(ada_b=jnp.concatenate(dmod, 0), norm1_g=jnp.concatenate(dn1g, 0), norm2_g=jnp.concatenate(dn2g, 0),
                 final_g=d_final_g, conv_w=jnp.stack(d_conv_w), conv_b=jnp.concatenate(d_conv_b, 0), sg_v_g=d_sg_vg)
    last_pack = _pack_rows([small[n] for n in LAST_SMALL + SMALL_SHARD])
    n_last = _pack_rows([P[n] for n in LAST_SMALL]).shape[0]
    n_pack = last_pack.shape[0]
    pack_all = _all_gather(jnp.concatenate([last_pack, loss_tile], 0)[None], 0, name="gather_small_grads")
    loss = jnp.sum(pack_all[:, n_pack, 0])
    (mix_sent[0],), last_token = _exchange_start([mix_grads], pack_all, name="mix_grads_start0")
    small_group(LAST_SMALL, pack_all[:, :n_last], "adamw_small")
    sh_rows = (n_pack - n_last) // N_DEV
    sh_parts = pack_all[:, n_last:n_pack].reshape(N_DEV, sh_rows, N_DEV, LANE)
    sh_parts = lax.dynamic_index_in_dim(sh_parts, me, 2, keepdims=False)
    sh_parts = jnp.pad(sh_parts, ((0, 0), (0, 16 - sh_rows), (0, 0)))

    def pack_shard(prefix):
        return _pad_rows(jnp.concatenate([P[prefix + n].reshape(-1, LANE) for n in SMALL_SHARD], 0), 16)[None]

    sg_, sd_, sm_, sv_ = _adamw(pack_shard(''), [sh_parts], pack_shard('m_'), pack_shard('v_'), name="adamw_channel")
    off = 0
    for n in SMALL_SHARD:
        rows = math.prod(P[n].shape) // LANE
        out[n] = tuple(t[0, off:off + rows].reshape(P[n].shape) for t in (sg_, sd_, sm_, sv_))
        off += rows

    dmod_all = pack_all[:, :DEPTH * 6 * D // LANE].reshape(N_DEV, DEPTH, 6 * D)
    dmod_cols = lax.dynamic_slice_in_dim(dmod_all, me * ncol, ncol, 2)
    g_ada = [_mm(c_pad, _pad_rows(dmod_cols[:, i], LANE), ta=True, name=f"ada_w_grad{i}")[None] for i in range(DEPTH)]

    def big(name, parts):
        res = _adamw(P[name], parts, P['m_' + name], P['v_' + name], name="adamw_" + name)
        out[name] = res
        return res[1]

    ff_parts = [_exchange_wait(ff_sent[i], last_token, name=f"ff_grads_wait{i}") for i in range(DEPTH)]
    mix_parts = [None] + [_exchange_wait(mix_sent[i], last_token, name=f"mix_grads_wait{i}") for i in range(1, DEPTH)]
    big('ada_w', g_ada)
    big('ff_w1', [p[0] for p in ff_parts])
    big('ff_w2', [p[1] for p in ff_parts])
    done = big('sg_w_in', [mix_parts[2][0]])
    mix_parts[0] = _exchange_wait(mix_sent[0], done, name="mix_grads_wait0")
    big('conv_w_in', [mix_parts[i][0] for i in range(DEPTH) if i % 3 == 0])
    row_names = ['conv_w_out', 'ssm_w_in', 'ssm_glu_w', 'ssm_w_out', 'sg_w_out']
    row_parts = ([mix_parts[i][1] for i in range(DEPTH) if i % 3 == 0] + mix_parts[1][:3] + [mix_parts[2][1]])
    small_group(S5_SMALL, mix_parts[1][3].reshape(N_DEV, -1, LANE), "adamw_s5")
    small_group(SG_SMALL, mix_parts[2][2].reshape(N_DEV, -1, LANE), "adamw_sg")
    row_w, row_m, row_v = (jnp.concatenate([P[pre + n] for n in row_names], 0) for pre in ('', 'm_', 'v_'))
    rw = _adamw(row_w, row_parts, row_m, row_v, name="adamw_row_sharded")
    off = 0
    for n in row_names:
        cnt = P[n].shape[0]
        out[n] = tuple(t[off:off + cnt] for t in rw)
        off += cnt

    return (loss, grad_x, *[out[n][0] for n in WEIGHTS], *[out[n][1] for n in WEIGHTS],
            *[out[n][2] for n in WEIGHTS], *[out[n][3] for n in WEIGHTS])
```

```python
import math

import jax
import jax.numpy as jnp
from jax import lax
from jax.experimental import pallas as pl
from jax.experimental.pallas import tpu as pltpu

F32 = jnp.float32
BF16 = jnp.bfloat16

N_DEV = 8
MESH_ID = pl.DeviceIdType.MESH
DEPTH = 4
EPS = 1e-6
S5_GROUPS, S5_GROUP, S5_STATE = 64, 16, 64
S5_LANES = S5_GROUPS * S5_STATE
S5_BLOCKS = 8
S5_CHUNK = 512
SG_HEADS, SG_CHUNK = 8, 128
LANE = 128
SUBLANE = 8
VMEM_LIMIT = 48 * 1024 * 1024
ADAM_LR, ADAM_B1, ADAM_B2, ADAM_EPS, ADAM_WD, ADAM_STEP = 0.001, 0.9, 0.999, 1e-08, 0.01, 10
GELU_C = math.sqrt(2.0 / math.pi)
GELU_A = 0.044715

WEIGHTS = ['ada_w', 'ada_b', 'norm1_g', 'norm2_g', 'ff_w1', 'ff_w2', 'final_g', 'conv_w_in', 'conv_w', 'conv_b',
           'conv_w_out', 'ssm_w_in', 'ssm_a_re', 'ssm_a_im', 'ssm_log_dt', 'ssm_b_re', 'ssm_b_im', 'ssm_c_re',
           'ssm_c_im', 'ssm_d', 'ssm_glu_w', 'ssm_glu_b', 'ssm_w_out', 'sg_w_in', 'sg_v_g', 'sg_w_s', 'sg_b_s',
           'sg_w_out']
INPUTS = ['x', 'c'] + WEIGHTS + ['loss_target'] + ['m_' + n for n in WEIGHTS] + ['v_' + n for n in WEIGHTS]
S5_SMALL = ['ssm_a_re', 'ssm_a_im', 'ssm_log_dt', 'ssm_b_re', 'ssm_b_im', 'ssm_c_re', 'ssm_c_im', 'ssm_d', 'ssm_glu_b']
SG_SMALL = ['sg_w_s', 'sg_b_s']
LAST_SMALL = ['ada_b', 'norm1_g', 'norm2_g', 'final_g']
SMALL_SHARD = ['conv_w', 'conv_b', 'sg_v_g']


def _params(*sem):
    return pltpu.CompilerParams(dimension_semantics=sem or None, vmem_limit_bytes=VMEM_LIMIT)


def _my_pos():
    return lax.axis_index("x"), lax.axis_index("y"), lax.axis_index("c")


def _my_index():
    x, y, c = _my_pos()
    return 4 * x + 2 * y + c


def _mm(a, b, *, name, ta=False, tb=False, out_dtypes=(F32,), epi=None, extras=(), a_fn=None, bm=1024, bn=1024,
        bk=1024):
    a_chunks = a.shape[0] if a.ndim == 3 else 0
    b_chunks = b.shape[0] if b.ndim == 3 else 0
    assert not (a_chunks and ta) and not (b_chunks and tb)
    if a_chunks:
        m, k = a.shape[1], a_chunks * a.shape[2]
        bk = k
    else:
        m, k = (a.shape[1], a.shape[0]) if ta else a.shape
    if b_chunks:
        k2, n = b.shape[1], b_chunks * b.shape[2]
        bn = min(bn, b.shape[2])
    else:
        k2, n = (b.shape[1], b.shape[0]) if tb else b.shape
    assert k == k2, (a.shape, b.shape, ta, tb)
    bm, bn, bk = min(bm, m), min(bn, n), min(bk, k)
    assert m % bm == 0 and n % bn == 0 and k % bk == 0, (m, n, k, bm, bn, bk)
    nk = k // bk
    n_ex, n_out = len(extras), len(out_dtypes)
    dims = (((0 if ta else 1,), (1 if tb else 0,)), ((), ()))

    def body(*refs):
        a_ref, b_ref = refs[0], refs[1]
        ex_refs = refs[2:2 + n_ex]
        out_refs = refs[2 + n_ex:2 + n_ex + n_out]

        def finish(acc):
            outs = epi(acc, *[r[...] for r in ex_refs]) if epi is not None else (acc,)
            for r, o in zip(out_refs, outs):
                r[...] = o.astype(r.dtype)

        av = jnp.concatenate([a_ref[t] for t in range(a_chunks)], axis=1) if a_chunks else a_ref[...]
        if a_fn is not None:
            av = a_fn(av)
        part = lax.dot_general(av.astype(BF16), b_ref[...].astype(BF16), dims, preferred_element_type=F32)
        if nk == 1:
            finish(part)
            return
        acc_ref = refs[-1]
        kk = pl.program_id(2)

        @pl.when(kk == 0)
        def _():
            acc_ref[...] = part

        @pl.when(kk > 0)
        def _():
            acc_ref[...] += part

        @pl.when(kk == nk - 1)
        def _():
            finish(acc_ref[...])

    if a_chunks:
        a_spec = pl.BlockSpec((a_chunks, bm, a.shape[2]), lambda i, j, q: (0, i, 0))
    elif ta:
        a_spec = pl.BlockSpec((bk, bm), lambda i, j, q: (q, i))
    else:
        a_spec = pl.BlockSpec((bm, bk), lambda i, j, q: (i, q))
    if b_chunks:
        per = b.shape[2] // bn
        b_spec = pl.BlockSpec((None, bk, bn), lambda i, j, q: (j // per, q, j % per))
    elif tb:
        b_spec = pl.BlockSpec((bn, bk), lambda i, j, q: (j, q))
    else:
        b_spec = pl.BlockSpec((bk, bn), lambda i, j, q: (q, j))
    ex_specs = []
    for arr, kind in extras:
        if kind == 'mn':
            assert arr.shape == (m, n), (arr.shape, m, n)
            ex_specs.append(pl.BlockSpec((bm, bn), lambda i, j, q: (i, j)))
        else:
            assert arr.shape == (1, n), (arr.shape, n)
            ex_specs.append(pl.BlockSpec((1, bn), lambda i, j, q: (0, j)))
    outs = pl.pallas_call(
        body, name=name,
        out_shape=tuple(jax.ShapeDtypeStruct((m, n), d) for d in out_dtypes),
        grid=(m // bm, n // bn, nk),
        in_specs=[a_spec, b_spec] + ex_specs,
        out_specs=tuple(pl.BlockSpec((bm, bn), lambda i, j, q: (i, j)) for _ in out_dtypes),
        scratch_shapes=[pltpu.VMEM((bm, bn), F32)] if nk > 1 else [],
        compiler_params=_params("parallel", "parallel", "arbitrary"),
    )(a, b, *[arr for arr, _ in extras])
    return outs if n_out > 1 else outs[0]


def _epi_residual(acc, res, gate):
    return res + gate * acc, acc


def _epi_residual_norm(acc, res, gate, w, sh):
    xn = res + gate * acc
    return xn, acc, xn * _rstd(xn) * w + sh


def _wgrad(acts, cots, *, name, a_fn=None, bm=1024, bn=512):
    return _mm(acts, cots, ta=True, name=name, out_dtypes=(BF16,), a_fn=a_fn, bm=bm, bn=bn, bk=acts.shape[0])


def _square(a):
    af = a.astype(F32)
    return af * af


def _rstd(xv):
    return lax.rsqrt(jnp.mean(xv * xv, axis=-1, keepdims=True) + EPS)


def _normmod_fwd(x, w, sh, *, name, tm=512):
    L, D = x.shape

    def body(x_ref, w_ref, s_ref, h_ref):
        xv = x_ref[...]
        h_ref[...] = (xv * _rstd(xv) * w_ref[...] + s_ref[...]).astype(h_ref.dtype)

    row = pl.BlockSpec((tm, D), lambda i: (i, 0))
    vec = pl.BlockSpec((1, D), lambda i: (0, 0))
    return pl.pallas_call(body, name=name, out_shape=jax.ShapeDtypeStruct((L, D), BF16), grid=(L // tm,),
                          in_specs=[row, vec, vec], out_specs=row, compiler_params=_params("parallel"))(x, w, sh)


def _normmod_bwd(dh, x, w, dres, gate, *, name, tm=256):
    L, D = x.shape
    has_gate = gate is not None

    def body(*refs):
        if has_gate:
            dh_ref, x_ref, w_ref, r_ref, y_ref, g_ref, dx_ref, st_ref, dy_ref = refs
        else:
            dh_ref, x_ref, w_ref, r_ref, dx_ref, st_ref = refs
        i = pl.program_id(0)

        @pl.when(i == 0)
        def _():
            st_ref[...] = jnp.zeros_like(st_ref)

        xv = x_ref[...]
        dhv = dh_ref[...].astype(F32)
        rstd = _rstd(xv)
        xn = xv * rstd
        dxn = dhv * w_ref[...]
        dx = rstd * (dxn - xn * jnp.mean(dxn * xn, axis=-1, keepdims=True)) + r_ref[...]
        dx_ref[...] = dx
        st_ref[0:1, :] += jnp.sum(dhv * xn, axis=0, keepdims=True)
        st_ref[1:2, :] += jnp.sum(dhv, axis=0, keepdims=True)
        if has_gate:
            dy_ref[...] = (dx * g_ref[...]).astype(dy_ref.dtype)
            st_ref[2:3, :] += jnp.sum(dx * y_ref[...].astype(F32), axis=0, keepdims=True)

    row = pl.BlockSpec((tm, D), lambda i: (i, 0))
    vec = pl.BlockSpec((1, D), lambda i: (0, 0))
    st = pl.BlockSpec((SUBLANE, D), lambda i: (0, 0))
    in_specs = [row, row, vec, row] + ([row, vec] if has_gate else [])
    out_shape = [jax.ShapeDtypeStruct((L, D), F32), jax.ShapeDtypeStruct((SUBLANE, D), F32)]
    out_specs = [row, st]
    if has_gate:
        out_shape.append(jax.ShapeDtypeStruct((L, D), BF16))
        out_specs.append(row)
    args = (dh, x, w, dres) + (tuple(gate) if has_gate else ())
    return pl.pallas_call(body, name=name, out_shape=tuple(out_shape), grid=(L // tm,), in_specs=in_specs,
                          out_specs=tuple(out_specs), compiler_params=_params("arbitrary"))(*args)


def _loss_head(x, tgt, fg, y, g, *, name, tm=256):
    L, D = x.shape

    def body(x_ref, t_ref, fg_ref, y_ref, g_ref, dx_ref, st_ref, dy_ref, loss_ref):
        i = pl.program_id(0)

        @pl.when(i == 0)
        def _():
            st_ref[...] = jnp.zeros_like(st_ref)
            loss_ref[...] = jnp.zeros_like(loss_ref)

        xv = x_ref[...]
        rstd = _rstd(xv)
        xn = xv * rstd
        err = xn * fg_ref[...] - t_ref[...]
        loss_ref[...] += 0.5 * jnp.sum(jnp.mean(err * err, axis=-1, keepdims=True))
        dout = err * (1.0 / D)
        dxn = dout * fg_ref[...]
        dx = rstd * (dxn - xn * jnp.mean(dxn * xn, axis=-1, keepdims=True))
        dx_ref[...] = dx
        dy_ref[...] = (dx * g_ref[...]).astype(dy_ref.dtype)
        st_ref[0:1, :] += jnp.sum(dout * xn, axis=0, keepdims=True)
        st_ref[2:3, :] += jnp.sum(dx * y_ref[...].astype(F32), axis=0, keepdims=True)

    row = pl.BlockSpec((tm, D), lambda i: (i, 0))
    vec = pl.BlockSpec((1, D), lambda i: (0, 0))
    return pl.pallas_call(
        body, name=name,
        out_shape=(jax.ShapeDtypeStruct((L, D), F32), jax.ShapeDtypeStruct((SUBLANE, D), F32),
                   jax.ShapeDtypeStruct((L, D), BF16), jax.ShapeDtypeStruct((SUBLANE, LANE), F32)),
        grid=(L // tm,), in_specs=[row, row, vec, row, vec],
        out_specs=(row, pl.BlockSpec((SUBLANE, D), lambda i: (0, 0)), row,
                   pl.BlockSpec((SUBLANE, LANE), lambda i: (0, 0))),
        compiler_params=_params("arbitrary"))(x, tgt, fg, y, g)


def _shift_down(v, k):
    row = lax.broadcasted_iota(jnp.int32, v.shape, 0)
    return jnp.where(row >= k, pltpu.roll(v, k, 0), 0.0)


def _shift_up(v, k):
    n = v.shape[0]
    row = lax.broadcasted_iota(jnp.int32, v.shape, 0)
    return jnp.where(row < n - k, pltpu.roll(v, n - k, 0), 0.0)


def _conv_views(L, D):
    return [pl.BlockSpec((L, LANE), lambda j, s=s: (0, s * (D // LANE) + j)) for s in range(3)]


def _conv_fwd(bcx, wb, *, name):
    L, D = bcx.shape[0], bcx.shape[1] // 3

    def body(b_ref, c_ref, x_ref, wb_ref, p_ref):
        z = c_ref[...].astype(F32) * x_ref[...].astype(F32)
        conv = (wb_ref[0:1, :] * _shift_down(z, 2) + wb_ref[1:2, :] * _shift_down(z, 1)
                + wb_ref[2:3, :] * z + wb_ref[3:4, :])
        p_ref[...] = (b_ref[...].astype(F32) * conv).astype(p_ref.dtype)

    col = pl.BlockSpec((L, LANE), lambda j: (0, j))
    return pl.pallas_call(body, name=name, out_shape=jax.ShapeDtypeStruct((L, D), BF16), grid=(D // LANE,),
                          in_specs=_conv_views(L, D) + [pl.BlockSpec((SUBLANE, LANE), lambda j: (0, j))],
                          out_specs=col, compiler_params=_params("parallel"))(bcx, bcx, bcx, wb)


def _conv_bwd(dp, bcx, wb, *, name):
    L, D = dp.shape

    def body(dp_ref, b_ref, c_ref, x_ref, wb_ref, d3_ref, st_ref):
        cv, xv = c_ref[...].astype(F32), x_ref[...].astype(F32)
        z = cv * xv
        z1, z2 = _shift_down(z, 1), _shift_down(z, 2)
        w0, w1, w2 = wb_ref[0:1, :], wb_ref[1:2, :], wb_ref[2:3, :]
        conv = w0 * z2 + w1 * z1 + w2 * z + wb_ref[3:4, :]
        dpv = dp_ref[...].astype(F32)
        d3_ref[0] = (dpv * conv).astype(d3_ref.dtype)
        dconv = dpv * b_ref[...].astype(F32)
        dz = w2 * dconv + w1 * _shift_up(dconv, 1) + w0 * _shift_up(dconv, 2)
        d3_ref[1] = (dz * xv).astype(d3_ref.dtype)
        d3_ref[2] = (dz * cv).astype(d3_ref.dtype)
        st_ref[...] = jnp.zeros_like(st_ref)
        st_ref[0:1, :] = jnp.sum(dconv * z2, axis=0, keepdims=True)
        st_ref[1:2, :] = jnp.sum(dconv * z1, axis=0, keepdims=True)
        st_ref[2:3, :] = jnp.sum(dconv * z, axis=0, keepdims=True)
        st_ref[3:4, :] = jnp.sum(dconv, axis=0, keepdims=True)

    col = pl.BlockSpec((L, LANE), lambda j: (0, j))
    vec = pl.BlockSpec((SUBLANE, LANE), lambda j: (0, j))
    return pl.pallas_call(body, name=name,
                          out_shape=(jax.ShapeDtypeStruct((3, L, D), BF16), jax.ShapeDtypeStruct((SUBLANE, D), F32)),
                          grid=(D // LANE,), in_specs=[col] + _conv_views(L, D) + [vec],
                          out_specs=(pl.BlockSpec((3, L, LANE), lambda j: (0, 0, j)), vec),
                          compiler_params=_params("parallel"))(dp, bcx, bcx, bcx, wb)


def _sg_fwd(uv, vg, ws, bsb, *, name, tr=512):
    L, D = uv.shape[0], uv.shape[1] // 2

    def body(uv_ref, vg_ref, ws_ref, bsb_ref, p_ref):
        for ci in range(tr // SG_CHUNK):
            rows = slice(ci * SG_CHUNK, (ci + 1) * SG_CHUNK)
            v = uv_ref[rows, D:2 * D]
            vn = (v * _rstd(v) * vg_ref[...]).astype(BF16)
            for h in range(SG_HEADS):
                cols = slice(h * LANE, (h + 1) * LANE)
                vm = jnp.dot(ws_ref[h], vn[:, cols], preferred_element_type=F32) + bsb_ref[h]
                p_ref[rows, cols] = (uv_ref[rows, cols] * vm).astype(p_ref.dtype)

    full3 = pl.BlockSpec((SG_HEADS, SG_CHUNK, LANE), lambda i: (0, 0, 0))
    return pl.pallas_call(body, name=name, out_shape=jax.ShapeDtypeStruct((L, D), BF16), grid=(L // tr,),
                          in_specs=[pl.BlockSpec((tr, 2 * D), lambda i: (i, 0)), pl.BlockSpec((1, D), lambda i: (0, 0)),
                                    full3, full3],
                          out_specs=pl.BlockSpec((tr, D), lambda i: (i, 0)),
                          compiler_params=_params("parallel"))(uv, vg, ws, bsb)


def _sg_bwd(dp, uv, vg, ws, wst, bsb, *, name, tr=512):
    L, D = dp.shape

    def body(dp_ref, uv_ref, vg_ref, ws_ref, wst_ref, bsb_ref, duv_ref, dws_ref, dbs_ref, st_ref, dvn_ref):
        i = pl.program_id(0)

        @pl.when(i == 0)
        def _():
            dws_ref[...] = jnp.zeros_like(dws_ref)
            dbs_ref[...] = jnp.zeros_like(dbs_ref)
            st_ref[...] = jnp.zeros_like(st_ref)

        for ci in range(tr // SG_CHUNK):
            rows = slice(ci * SG_CHUNK, (ci + 1) * SG_CHUNK)
            v = uv_ref[rows, D:2 * D]
            rstd = _rstd(v)
            vhat = v * rstd
            vn = (vhat * vg_ref[...]).astype(BF16)
            for h in range(SG_HEADS):
                cols = slice(h * LANE, (h + 1) * LANE)
                vm = jnp.dot(ws_ref[h], vn[:, cols], preferred_element_type=F32) + bsb_ref[h]
                dph = dp_ref[rows, cols]
                duv_ref[rows, cols] = (dph * vm).astype(duv_ref.dtype)
                dvm = dph * uv_ref[rows, cols]
                dbs_ref[h] += dvm
                dvmb = dvm.astype(BF16)
                dws_ref[h] += lax.dot_general(dvmb, vn[:, cols], (((1,), (1,)), ((), ())),
                                              preferred_element_type=F32)
                dvn_ref[rows, cols] = jnp.dot(wst_ref[h], dvmb, preferred_element_type=F32)
            dvn = dvn_ref[rows, :]
            gv = dvn * vg_ref[...]
            dv = rstd * (gv - vhat * jnp.mean(gv * vhat, axis=-1, keepdims=True))
            duv_ref[rows, D:2 * D] = dv.astype(duv_ref.dtype)
            st_ref[0:1, :] += jnp.sum(dvn * vhat, axis=0, keepdims=True)

    full3 = pl.BlockSpec((SG_HEADS, SG_CHUNK, LANE), lambda i: (0, 0, 0))
    acc3 = jax.ShapeDtypeStruct((SG_HEADS, SG_CHUNK, LANE), F32)
    return pl.pallas_call(
        body, name=name,
        out_shape=(jax.ShapeDtypeStruct((L, 2 * D), BF16), acc3, acc3, jax.ShapeDtypeStruct((SUBLANE, D), F32)),
        grid=(L // tr,),
        in_specs=[pl.BlockSpec((tr, D), lambda i: (i, 0)), pl.BlockSpec((tr, 2 * D), lambda i: (i, 0)),
                  pl.BlockSpec((1, D), lambda i: (0, 0)), full3, full3, full3],
        out_specs=(pl.BlockSpec((tr, 2 * D), lambda i: (i, 0)), full3, full3,
                   pl.BlockSpec((SUBLANE, D), lambda i: (0, 0))),
        scratch_shapes=[pltpu.VMEM((tr, D), F32)],
        compiler_params=_params("arbitrary"))(dp, uv, vg, ws, wst, bsb)


def _gelu(x):
    return 0.5 * x * (1.0 + jnp.tanh(GELU_C * (x + GELU_A * x * x * x)))


def _gelu_grad(x):
    th = jnp.tanh(GELU_C * (x + GELU_A * x * x * x))
    return 0.5 * (1.0 + th) + 0.5 * x * (1.0 - th * th) * GELU_C * (1.0 + 3.0 * GELU_A * x * x)


def _cmul_add(xr, xi, ar, ai, br, bi):
    return xr + ar * br - ai * bi, xi + ar * bi + ai * br


def _cmul_conj_add(xr, xi, ar, ai, br, bi):
    return xr + ar * br + ai * bi, xi + ar * bi - ai * br


def _to_subchunk_order(src_ref, dst_ref, n):
    for k in range(n):
        dst_ref[pl.ds(SUBLANE * k, SUBLANE), :] = src_ref[pl.ds(k, SUBLANE, stride=n), :].astype(dst_ref.dtype)


def _to_time_order(src_ref, dst_ref, n):
    for m in range(n):
        r, k = divmod(SUBLANE * m, n)
        dst_ref[pl.ds(SUBLANE * m, SUBLANE), :] = src_ref[pl.ds(SUBLANE * k + r, SUBLANE, stride=SUBLANE), :]


def _s5_fwd(u, bre, bim, cre, cim, pw, pos, dsk, *, name, tc=S5_CHUNK):
    L, D = u.shape
    W = S5_LANES // S5_BLOCKS
    nt = L // tc
    n = tc // SUBLANE

    def sub(k):
        return pl.ds(SUBLANE * k, SUBLANE)

    def body(u_ref, bre_ref, bim_ref, cre_ref, cim_ref, pw_ref, pos_ref, d_ref, sre_ref, sim_ref, ypre_ref, yg_ref,
             carry, up, yp):
        t = pl.program_id(1)

        @pl.when(t == 0)
        def _():
            carry[...] = jnp.zeros_like(carry)

        _to_subchunk_order(u_ref, up, n)
        uv = up[...]
        ub = uv.astype(BF16)
        sre_ref[...] = jnp.dot(ub, bre_ref[...], preferred_element_type=F32)
        sim_ref[...] = jnp.dot(ub, bim_ref[...], preferred_element_type=F32)

        ar, ai = pw_ref[8], pw_ref[9]
        xr = jnp.zeros((SUBLANE, W), F32)
        xi = jnp.zeros((SUBLANE, W), F32)
        for k in range(n):
            xr, xi = _cmul_add(sre_ref[sub(k), :], sim_ref[sub(k), :], ar, ai, xr, xi)
            sre_ref[sub(k), :] = xr
            sim_ref[sub(k), :] = xi
        for q, d in enumerate((1, 2, 4)):
            xr, xi = _cmul_add(xr, xi, pw_ref[2 * q], pw_ref[2 * q + 1], pltpu.roll(xr, d, 0), pltpu.roll(xi, d, 0))
        cr, ci = carry[0], carry[1]
        xr, xi = _cmul_add(xr, xi, pw_ref[6], pw_ref[7], cr, ci)
        first = lax.broadcasted_iota(jnp.int32, (SUBLANE, W), 0) == 0
        er = jnp.where(first, cr, pltpu.roll(xr, 1, 0))
        ei = jnp.where(first, ci, pltpu.roll(xi, 1, 0))
        last = slice(SUBLANE - 1, SUBLANE)
        carry[0] = jnp.broadcast_to(xr[last, :], (SUBLANE, W))
        carry[1] = jnp.broadcast_to(xi[last, :], (SUBLANE, W))
        for k in range(n):
            sr, si = _cmul_add(sre_ref[sub(k), :], sim_ref[sub(k), :], pos_ref[0, k:k + 1, :], pos_ref[1, k:k + 1, :],
                               er, ei)
            sre_ref[sub(k), :] = sr
            sim_ref[sub(k), :] = si
        yp[...] = (jnp.dot(sre_ref[...].astype(BF16), cre_ref[...], preferred_element_type=F32)
                   - jnp.dot(sim_ref[...].astype(BF16), cim_ref[...], preferred_element_type=F32) + d_ref[...] * uv)
        _to_time_order(yp, ypre_ref, n)
        yg_ref[...] = _gelu(ypre_ref[...])

    ch = pl.BlockSpec((tc, LANE), lambda j, t: (t, j))
    st = pl.BlockSpec((tc, W), lambda j, t: (t, j))
    bsp = pl.BlockSpec((None, LANE, W), lambda j, t: (j, 0, 0))
    csp = pl.BlockSpec((None, W, LANE), lambda j, t: (j, 0, 0))
    return pl.pallas_call(
        body, name=name,
        out_shape=(jax.ShapeDtypeStruct((L, S5_LANES), F32), jax.ShapeDtypeStruct((L, S5_LANES), F32),
                   jax.ShapeDtypeStruct((L, D), F32), jax.ShapeDtypeStruct((L, D), F32)),
        grid=(S5_BLOCKS, nt),
        in_specs=[ch, bsp, bsp, csp, csp, pl.BlockSpec((10, SUBLANE, W), lambda j, t: (0, 0, j)),
                  pl.BlockSpec((2, n, W), lambda j, t: (0, 0, j)), pl.BlockSpec((1, LANE), lambda j, t: (0, j))],
        out_specs=(st, st, ch, ch),
        scratch_shapes=[pltpu.VMEM((2, SUBLANE, W), F32), pltpu.VMEM((tc, LANE), F32), pltpu.VMEM((tc, LANE), F32)],
        compiler_params=_params("parallel", "arbitrary"))(u, bre, bim, cre, cim, pw, pos, dsk)


def _s5_bwd(dy, u, sre, sim, bre, bim, cre, cim, pwr, posr, dsk, *, name, tc=S5_CHUNK):
    L, D = u.shape
    W = S5_LANES // S5_BLOCKS
    nt = L // tc
    n = tc // SUBLANE
    nt_dims = (((1,), (1,)), ((), ()))
    tn_dims = (((0,), (0,)), ((), ()))

    def sub(k):
        return pl.ds(SUBLANE * k, SUBLANE)

    def body(dy_ref, u_ref, sre_ref, sim_ref, bre_ref, bim_ref, cre_ref, cim_ref, pw_ref, pos_ref, d_ref,
             du_ref, dbre_ref, dbim_ref, dcre_ref, dcim_ref, ga_ref, dd_ref, gre, gim, carry, gacc, up, dyp):
        t = pl.program_id(1)

        @pl.when(t == 0)
        def _():
            for r in (carry, gacc, dbre_ref, dbim_ref, dcre_ref, dcim_ref, ga_ref, dd_ref):
                r[...] = jnp.zeros_like(r)

        _to_subchunk_order(dy_ref, dyp, n)
        _to_subchunk_order(u_ref, up, n)
        dyv, uv = dyp[...], up[...]
        dyb, ub = dyv.astype(BF16), uv.astype(BF16)
        gre[...] = lax.dot_general(dyb, cre_ref[...], nt_dims, preferred_element_type=F32)
        gim[...] = -lax.dot_general(dyb, cim_ref[...], nt_dims, preferred_element_type=F32)
        br, bi = pw_ref[8], pw_ref[9]
        xr = jnp.zeros((SUBLANE, W), F32)
        xi = jnp.zeros((SUBLANE, W), F32)
        for k in reversed(range(n)):
            xr, xi = _cmul_add(gre[sub(k), :], gim[sub(k), :], br, bi, xr, xi)
            gre[sub(k), :] = xr
            gim[sub(k), :] = xi
        for q, d in enumerate((1, 2, 4)):
            xr, xi = _cmul_add(xr, xi, pw_ref[2 * q], pw_ref[2 * q + 1], pltpu.roll(xr, SUBLANE - d, 0),
                               pltpu.roll(xi, SUBLANE - d, 0))
        cr, ci = carry[0], carry[1]
        xr, xi = _cmul_add(xr, xi, pw_ref[6], pw_ref[7], cr, ci)
        top = lax.broadcasted_iota(jnp.int32, (SUBLANE, W), 0) == SUBLANE - 1
        er = jnp.where(top, cr, pltpu.roll(xr, SUBLANE - 1, 0))
        ei = jnp.where(top, ci, pltpu.roll(xi, SUBLANE - 1, 0))
        carry[0] = jnp.broadcast_to(xr[0:1, :], (SUBLANE, W))
        carry[1] = jnp.broadcast_to(xi[0:1, :], (SUBLANE, W))
        nr, ni = er, ei
        acc_r = jnp.zeros((SUBLANE, W), F32)
        acc_i = jnp.zeros((SUBLANE, W), F32)
        for k in reversed(range(n)):
            place = slice(n - 1 - k, n - k)
            gr, gi = _cmul_conj_add(gre[sub(k), :], gim[sub(k), :], pos_ref[0, place, :], pos_ref[1, place, :], er, ei)
            gre[sub(k), :] = gr
            gim[sub(k), :] = gi
            sr, si = sre_ref[sub(k), :], sim_ref[sub(k), :]
            acc_r = acc_r + sr * nr + si * ni
            acc_i = acc_i + sr * ni - si * nr
            nr, ni = gr, gi
        gacc[0] += acc_r
        gacc[1] += acc_i
        grb, gib = gre[...].astype(BF16), gim[...].astype(BF16)
        dyp[...] = (lax.dot_general(grb, bre_ref[...], nt_dims, preferred_element_type=F32)
                    + lax.dot_general(gib, bim_ref[...], nt_dims, preferred_element_type=F32) + d_ref[...] * dyv)
        _to_time_order(dyp, up, n)
        du_ref[...] = up[...].astype(du_ref.dtype)
        dbre_ref[...] += lax.dot_general(ub, grb, tn_dims, preferred_element_type=F32)
        dbim_ref[...] += lax.dot_general(ub, gib, tn_dims, preferred_element_type=F32)
        dcre_ref[...] += lax.dot_general(sre_ref[...].astype(BF16), dyb, tn_dims, preferred_element_type=F32)
        dcim_ref[...] -= lax.dot_general(sim_ref[...].astype(BF16), dyb, tn_dims, preferred_element_type=F32)
        dd_ref[0:1, :] += jnp.sum(dyv * uv, axis=0, keepdims=True)

        @pl.when(t == nt - 1)
        def _():
            ga_ref[0:1, :] = jnp.sum(gacc[0], axis=0, keepdims=True)
            ga_ref[1:2, :] = jnp.sum(gacc[1], axis=0, keepdims=True)

    ch = pl.BlockSpec((tc, LANE), lambda j, t: (nt - 1 - t, j))
    st = pl.BlockSpec((tc, W), lambda j, t: (nt - 1 - t, j))
    bsp = pl.BlockSpec((None, LANE, W), lambda j, t: (j, 0, 0))
    csp = pl.BlockSpec((None, W, LANE), lambda j, t: (j, 0, 0))
    return pl.pallas_call(
        body, name=name,
        out_shape=(jax.ShapeDtypeStruct((L, D), BF16),
                   jax.ShapeDtypeStruct((S5_BLOCKS, LANE, W), F32), jax.ShapeDtypeStruct((S5_BLOCKS, LANE, W), F32),
                   jax.ShapeDtypeStruct((S5_BLOCKS, W, LANE), F32), jax.ShapeDtypeStruct((S5_BLOCKS, W, LANE), F32),
                   jax.ShapeDtypeStruct((SUBLANE, S5_LANES), F32), jax.ShapeDtypeStruct((SUBLANE, D), F32)),
        grid=(S5_BLOCKS, nt),
        in_specs=[ch, ch, st, st, bsp, bsp, csp, csp, pl.BlockSpec((10, SUBLANE, W), lambda j, t: (0, 0, j)),
                  pl.BlockSpec((2, n, W), lambda j, t: (0, 0, j)), pl.BlockSpec((1, LANE), lambda j, t: (0, j))],
        out_specs=(ch, bsp, bsp, csp, csp, pl.BlockSpec((SUBLANE, W), lambda j, t: (0, j)),
                   pl.BlockSpec((SUBLANE, LANE), lambda j, t: (0, j))),
        scratch_shapes=[pltpu.VMEM((tc, W), F32), pltpu.VMEM((tc, W), F32), pltpu.VMEM((2, SUBLANE, W), F32),
                        pltpu.VMEM((2, SUBLANE, W), F32), pltpu.VMEM((tc, LANE), F32), pltpu.VMEM((tc, LANE), F32)],
        compiler_params=_params("parallel", "arbitrary"))(dy, u, sre, sim, bre, bim, cre, cim, pwr, posr, dsk)


def _glu_bwd(dy2, y, t, *, name, tm=256):
    L, D = y.shape

    def body(dy2_ref, y_ref, t_ref, dt_ref, dya_ref, st_ref):
        i = pl.program_id(0)

        @pl.when(i == 0)
        def _():
            st_ref[...] = jnp.zeros_like(st_ref)

        sig = 1.0 / (1.0 + jnp.exp(-t_ref[...]))
        dy2v = dy2_ref[...]
        dt = dy2v * y_ref[...] * sig * (1.0 - sig)
        dt_ref[...] = dt.astype(dt_ref.dtype)
        dya_ref[...] = dy2v * sig
        st_ref[0:1, :] += jnp.sum(dt, axis=0, keepdims=True)

    row = pl.BlockSpec((tm, D), lambda i: (i, 0))
    return pl.pallas_call(
        body, name=name,
        out_shape=(jax.ShapeDtypeStruct((L, D), BF16), jax.ShapeDtypeStruct((L, D), F32),
                   jax.ShapeDtypeStruct((SUBLANE, D), F32)),
        grid=(L // tm,), in_specs=[row, row, row],
        out_specs=(row, row, pl.BlockSpec((SUBLANE, D), lambda i: (0, 0))),
        compiler_params=_params("arbitrary"))(dy2, y, t)


def _s5_prep(a_re, a_im, log_dt, b_re, b_im, c_re, c_im):
    dt = jnp.exp(log_dt)[:, None]
    mag = jnp.exp(a_re * dt)
    abar_re = mag * jnp.cos(a_im * dt)
    abar_im = mag * jnp.sin(a_im * dt)
    den = a_re * a_re + a_im * a_im
    nr = abar_re - 1.0
    ni = abar_im
    f_re = ((nr * a_re + ni * a_im) / den)[..., None]
    f_im = ((ni * a_re - nr * a_im) / den)[..., None]
    bbar_re = f_re * b_re - f_im * b_im
    bbar_im = f_re * b_im + f_im * b_re
    eye = jnp.eye(S5_GROUPS // S5_BLOCKS, dtype=F32)
    gb = S5_GROUPS // S5_BLOCKS

    def blk_b(bb):
        t = bb.reshape(S5_BLOCKS, gb, S5_STATE, S5_GROUP)
        return jnp.einsum('jgph,gk->jghkp', t, eye).reshape(S5_BLOCKS, gb * S5_GROUP, gb * S5_STATE)

    def blk_c(cc):
        t = cc.reshape(S5_BLOCKS, gb, S5_GROUP, S5_STATE)
        return jnp.einsum('jghp,gk->jgpkh', t, eye).reshape(S5_BLOCKS, gb * S5_STATE, gb * S5_GROUP)

    return (abar_re.reshape(1, S5_LANES), abar_im.reshape(1, S5_LANES), blk_b(bbar_re), blk_b(bbar_im),
            blk_c(c_re), blk_c(c_im))


def _cpowers(ar, ai, count):
    pr, pi, m = ar, ai, 1
    while m < count:
        tr, ti = pr[m - 1:m], pi[m - 1:m]
        pr, pi = jnp.concatenate([pr, pr * tr - pi * ti], 0), jnp.concatenate([pi, pr * ti + pi * tr], 0)
        m *= 2
    return pr, pi


def _s5_power_tables(ar, ai, n):
    pr, pi = _cpowers(ar, ai, n)
    qr, qi = _cpowers(pr[n - 1:n], pi[n - 1:n], SUBLANE)
    row = jnp.arange(SUBLANE)[:, None]
    lanes = ar.shape[1]

    def tables(sign, keep, order):
        out = []
        for d in (1, 2, 4):
            out += [jnp.where(keep(d), qr[d - 1:d], 0.0), jnp.where(keep(d), sign * qi[d - 1:d], 0.0)]
        out += [jnp.concatenate([qr[r:r + 1] for r in order], 0), sign * jnp.concatenate([qi[r:r + 1] for r in order], 0),
                ar, sign * ai]
        return jnp.stack([jnp.broadcast_to(o, (SUBLANE, lanes)) for o in out])

    fwd = tables(1.0, lambda d: row >= d, list(range(SUBLANE)))
    rev = tables(-1.0, lambda d: row + d <= SUBLANE - 1, list(reversed(range(SUBLANE))))
    return fwd, rev, jnp.stack([pr, pi])


ADAMW_PART_BLOCK_BYTES = 2 * 1024 * 1024


def _adamw(w, parts, m, v, *, name):
    n, R, C = w.shape
    assert len(parts) == n
    P = parts[0].shape[0]
    tr = R
    while P * tr * C * parts[0].dtype.itemsize > ADAMW_PART_BLOCK_BYTES and tr % 16 == 0:
        tr //= 2
    c1 = 1.0 / (1.0 - ADAM_B1 ** ADAM_STEP)
    c2 = 1.0 / (1.0 - ADAM_B2 ** ADAM_STEP)

    def body(*refs):
        w_ref, m_ref, v_ref = refs[:3]
        p_refs = refs[3:3 + n]
        g_ref, d_ref, nm_ref, nv_ref = refs[3 + n:]
        layer = pl.program_id(0)
        for q, p_ref in enumerate(p_refs):
            @pl.when(layer == q)
            def _(p_ref=p_ref):
                g = p_ref[0].astype(F32)
                for s in range(1, P):
                    g = g + p_ref[s].astype(F32)
                nm = ADAM_B1 * m_ref[...] + (1.0 - ADAM_B1) * g
                nv = ADAM_B2 * v_ref[...] + (1.0 - ADAM_B2) * (g * g)
                g_ref[...] = g
                nm_ref[...] = nm
                nv_ref[...] = nv
                d_ref[...] = -ADAM_LR * ((nm * c1) / (jnp.sqrt(nv * c2) + ADAM_EPS) + ADAM_WD * w_ref[...])

    row = pl.BlockSpec((None, tr, C), lambda l, i: (l, i, 0))
    part_specs = [pl.BlockSpec((P, tr, C), lambda l, i, q=q: (0, jnp.where(l == q, i, 0), 0)) for q in range(n)]
    out = jax.ShapeDtypeStruct((n, R, C), F32)
    return pl.pallas_call(body, name=name, out_shape=(out, out, out, out), grid=(n, R // tr),
                          in_specs=[row, row, row] + part_specs, out_specs=(row, row, row, row),
                          compiler_params=_params("arbitrary", "arbitrary"))(w, m, v, *parts)


def _all_gather(xs, axis, *, name):
    m = xs.shape[axis]
    out_shape = list(xs.shape)
    out_shape[axis] = N_DEV * m

    def body(x_ref, out_ref, send_sems, recv_sems, local_sem):
        x, y, c = _my_pos()
        me, sibling = (x, y, c), (x, y, 1 - c)
        chips = [(1 - x, y), (x, 1 - y), (1 - x, 1 - y)]

        def blk(px, py, pc):
            idx = [slice(None)] * 3
            idx[axis] = pl.ds((4 * px + 2 * py + pc) * m, m)
            return out_ref.at[tuple(idx)]

        def copy(k, block, to, src=None):
            return pltpu.make_async_remote_copy(src_ref=blk(*block) if src is None else src, dst_ref=blk(*block),
                                                send_sem=send_sems.at[k], recv_sem=recv_sems.at[k],
                                                device_id=to, device_id_type=MESH_ID)

        mine = pltpu.make_async_copy(x_ref, blk(*me), local_sem)
        mine.start()
        first = [copy(0, me, sibling, src=x_ref)]
        first += [copy(1 + j, me, (*chip, c), src=x_ref) for j, chip in enumerate(chips)]
        for cp in first:
            cp.start()
        passed = [copy(4 + j, (*chip, c), sibling) for j, chip in enumerate(chips)]
        for j, chip in enumerate(chips):
            copy(1 + j, (*chip, c), me).wait_recv()
            passed[j].start()
        copy(0, sibling, me).wait_recv()
        for j, chip in enumerate(chips):
            copy(4 + j, (*chip, 1 - c), me).wait_recv()
        for cp in first + passed:
            cp.wait_send()
        mine.wait()

    hbm = pl.BlockSpec(memory_space=pl.ANY)
    return pl.pallas_call(body, name=name, out_shape=jax.ShapeDtypeStruct(tuple(out_shape), xs.dtype),
                          in_specs=[hbm], out_specs=hbm,
                          scratch_shapes=[pltpu.SemaphoreType.DMA((N_DEV - 1,)), pltpu.SemaphoreType.DMA((N_DEV - 1,)),
                                          pltpu.SemaphoreType.DMA],
                          compiler_params=pltpu.CompilerParams(has_side_effects=True))(xs)


NEAR_PEERS = (1, 2, 4, 6)
RELAY_PEERS = (2, 4, 6)


def _block(ref, axis, idx, m):
    return ref.at[pl.ds(idx * m, m), :] if axis == 0 else ref.at[:, pl.ds(idx * m, m)]


def _exchange_copies(metas, src_refs, zone_refs, send_sems, recv_sems, base, phase):
    x, y, c = _my_pos()
    me = 4 * x + 2 * y + c

    def place(r):
        pos = (1 - x if r & 4 else x, 1 - y if r & 2 else y, 1 - c if r & 1 else c)
        return pos, 4 * pos[0] + 2 * pos[1] + pos[2]

    def copies(r, to, src, dst, arrival):
        return tuple(pltpu.make_async_remote_copy(src_ref=src, dst_ref=d, send_sem=send_sems.at[base + r - 1],
                                                  recv_sem=recv_sems.at[base + r - 1], device_id=to,
                                                  device_id_type=MESH_ID) for d in (dst, arrival))

    pairs, own = [], []
    if phase == 'relay':
        sibling, _ = place(1)
        for r in RELAY_PEERS:
            held, comes = place(r)[1], place(r | 1)[1]
            for (kind, axis, m), z_ref in zip(metas, zone_refs):
                pairs.append(copies(r, sibling, _block(z_ref, axis, held, m), _block(z_ref, axis, held, m),
                                    _block(z_ref, axis, comes, m)))
        return pairs, own
    for r in (NEAR_PEERS if phase == 'near' else range(1, N_DEV)):
        pos, peer = place(r)
        for (kind, axis, m), s_ref, z_ref in zip(metas, src_refs, zone_refs):
            if kind == 'gather':
                pairs.append(copies(r, pos, s_ref, _block(z_ref, axis, me, m), _block(z_ref, axis, peer, m)))
            else:
                pairs.append(copies(r, pos, _block(s_ref, axis, peer, m), z_ref.at[me], z_ref.at[peer]))
    for (kind, axis, m), s_ref, z_ref in zip(metas, src_refs, zone_refs):
        src, dst = (s_ref, _block(z_ref, axis, me, m)) if kind == 'gather' else (_block(s_ref, axis, me, m), z_ref.at[me])
        own.append(pltpu.make_async_copy(src, dst, recv_sems.at[base + N_DEV - 1]))
    return pairs, own


def _exchange_start(groups, after, *, name, relayed=False):
    flat = [it for g in groups for it in g]
    n, ng = len(flat), len(groups)
    metas = [it[2] for it in flat]
    bounds = [(sum(len(g) for g in groups[:q]), sum(len(g) for g in groups[:q + 1])) for q in range(ng)]
    phase = 'near' if relayed else 'all'

    def body(*refs):
        src_refs = refs[:n]
        send_sems, recv_sems = refs[n + 1], refs[n + 2]
        zone_refs = refs[2 * n + 3:3 * n + 3]
        token = refs[-1]
        for q, (lo, hi) in enumerate(bounds):
            pairs, own = _exchange_copies(metas[lo:hi], src_refs[lo:hi], zone_refs[lo:hi], send_sems, recv_sems,
                                          q * N_DEV, phase)
            for outgoing, _ in pairs:
                outgoing.start()
            for cp in own:
                cp.start()
        token[...] = jnp.zeros_like(token)

    hbm = pl.BlockSpec(memory_space=pltpu.HBM)
    sem = pl.BlockSpec(memory_space=pltpu.SEMAPHORE)
    srcs = [it[0] for it in flat]
    res = pl.pallas_call(
        body, name=name,
        out_shape=(pltpu.SemaphoreType.DMA((ng * N_DEV,)), pltpu.SemaphoreType.DMA((ng * N_DEV,)),
                   *[pltpu.HBM(a.shape, a.dtype) for a in srcs], *[pltpu.HBM(it[1], it[0].dtype) for it in flat],
                   jax.ShapeDtypeStruct((SUBLANE, LANE), F32)),
        in_specs=[hbm] * n + [pl.BlockSpec(memory_space=pl.ANY)],
        out_specs=(sem, sem, *[hbm] * (2 * n), pl.BlockSpec(memory_space=pltpu.VMEM)),
        input_output_aliases={q: 2 + q for q in range(n)},
        compiler_params=pltpu.CompilerParams(has_side_effects=pltpu.SideEffectType.DATAFLOW_SIDE_EFFECTING),
    )(*[pltpu.with_memory_space_constraint(a, pltpu.HBM) for a in srcs], after)
    handles = [(res[0], res[1], q * N_DEV, phase, list(res[2 + lo:2 + hi]), list(res[2 + n + lo:2 + n + hi]),
                metas[lo:hi]) for q, (lo, hi) in enumerate(bounds)]
    return handles, res[-1]


def _exchange_wait(handle, after, *, name):
    send_sems, recv_sems, base, phase, srcs, zones, metas = handle
    ns, nz = len(srcs), len(zones)

    def body(*refs):
        src_refs, zone_refs = refs[:ns], refs[ns:ns + nz]
        s_sems, r_sems = refs[ns + nz], refs[ns + nz + 1]
        pairs, own = _exchange_copies(metas, src_refs, zone_refs, s_sems, r_sems, base, phase)
        for outgoing, incoming in pairs:
            outgoing.wait_send()
            incoming.wait_recv()
        for cp in own:
            cp.wait()

    hbm = pl.BlockSpec(memory_space=pltpu.HBM)
    sem = pl.BlockSpec(memory_space=pltpu.SEMAPHORE)
    arrays = srcs + zones
    res = pl.pallas_call(
        body, name=name,
        out_shape=tuple(pltpu.HBM(a.shape, a.dtype) for a in arrays),
        in_specs=[hbm] * (ns + nz) + [sem, sem, pl.BlockSpec(memory_space=pl.ANY)],
        out_specs=tuple([hbm] * (ns + nz)),
        input_output_aliases={q: q for q in range(ns + nz)},
        compiler_params=pltpu.CompilerParams(has_side_effects=pltpu.SideEffectType.DATAFLOW_SIDE_EFFECTING),
    )(*arrays, send_sems, recv_sems, after)
    return list(res[ns:])


def _exchange_relay(handle, after, *, name):
    metas = handle[6]
    zones = _exchange_wait(handle, after, name=name + "_in")
    nz = len(zones)

    def body(*refs):
        zone_refs = refs[:nz]
        send_sems, recv_sems = refs[nz], refs[nz + 1]
        pairs, _ = _exchange_copies(metas, (), zone_refs, send_sems, recv_sems, 0, 'relay')
        for outgoing, _ in pairs:
            outgoing.start()
        refs[-1][...] = jnp.zeros_like(refs[-1])

    hbm = pl.BlockSpec(memory_space=pltpu.HBM)
    sem = pl.BlockSpec(memory_space=pltpu.SEMAPHORE)
    res = pl.pallas_call(
        body, name=name + "_out",
        out_shape=(pltpu.SemaphoreType.DMA((N_DEV,)), pltpu.SemaphoreType.DMA((N_DEV,)),
                   *[pltpu.HBM(z.shape, z.dtype) for z in zones], jax.ShapeDtypeStruct((SUBLANE, LANE), F32)),
        in_specs=[hbm] * nz, out_specs=(sem, sem, *[hbm] * nz, pl.BlockSpec(memory_space=pltpu.VMEM)),
        input_output_aliases={q: 2 + q for q in range(nz)},
        compiler_params=pltpu.CompilerParams(has_side_effects=pltpu.SideEffectType.DATAFLOW_SIDE_EFFECTING),
    )(*zones)
    return (res[0], res[1], 0, 'relay', [], list(res[2:2 + nz]), metas), res[-1][0:1, 0:1]


def _pad_rows(a, rows):
    return jnp.pad(a, ((0, rows - a.shape[0]), (0, 0)))


PACK_ROWS = 2 * SUBLANE


def _rows(a):
    flat = a.reshape(-1).astype(F32)
    pad = -flat.shape[0] % (PACK_ROWS * LANE)
    return (jnp.pad(flat, (0, pad)) if pad else flat).reshape(-1, LANE)


def _pack_rows(arrays):
    return jnp.concatenate([_rows(a) for a in arrays], 0)


def _unpack_rows(t, shapes):
    out, off = [], 0
    for shp in shapes:
        size = math.prod(shp)
        rows = -(-size // (PACK_ROWS * LANE)) * PACK_ROWS
        out.append(t[off:off + rows].reshape(-1)[:size].reshape(shp))
        off += rows
    return out


def _stat_row(st, r):
    return st[r:r + 1, :]


def kernel(x, c, ada_w, ada_b, norm1_g, norm2_g, ff_w1, ff_w2, final_g, conv_w_in, conv_w, conv_b, conv_w_out, ssm_w_in, ssm_a_re, ssm_a_im, ssm_log_dt, ssm_b_re, ssm_b_im, ssm_c_re, ssm_c_im, ssm_d, ssm_glu_w, ssm_glu_b, ssm_w_out, sg_w_in, sg_v_g, sg_w_s, sg_b_s, sg_w_out, loss_target, m_ada_w, m_ada_b, m_norm1_g, m_norm2_g, m_ff_w1, m_ff_w2, m_final_g, m_conv_w_in, m_conv_w, m_conv_b, m_conv_w_out, m_ssm_w_in, m_ssm_a_re, m_ssm_a_im, m_ssm_log_dt, m_ssm_b_re, m_ssm_b_im, m_ssm_c_re, m_ssm_c_im, m_ssm_d, m_ssm_glu_w, m_ssm_glu_b, m_ssm_w_out, m_sg_w_in, m_sg_v_g, m_sg_w_s, m_sg_b_s, m_sg_w_out, v_ada_w, v_ada_b, v_norm1_g, v_norm2_g, v_ff_w1, v_ff_w2, v_final_g, v_conv_w_in, v_conv_w, v_conv_b, v_conv_w_out, v_ssm_w_in, v_ssm_a_re, v_ssm_a_im, v_ssm_log_dt, v_ssm_b_re, v_ssm_b_im, v_ssm_c_re, v_ssm_c_im, v_ssm_d, v_ssm_glu_w, v_ssm_glu_b, v_ssm_w_out, v_sg_w_in, v_sg_v_g, v_sg_w_s, v_sg_b_s, v_sg_w_out):
    P = dict(zip(INPUTS, (x, c, ada_w, ada_b, norm1_g, norm2_g, ff_w1, ff_w2, final_g, conv_w_in, conv_w, conv_b, conv_w_out, ssm_w_in, ssm_a_re, ssm_a_im, ssm_log_dt, ssm_b_re, ssm_b_im, ssm_c_re, ssm_c_im, ssm_d, ssm_glu_w, ssm_glu_b, ssm_w_out, sg_w_in, sg_v_g, sg_w_s, sg_b_s, sg_w_out, loss_target, m_ada_w, m_ada_b, m_norm1_g, m_norm2_g, m_ff_w1, m_ff_w2, m_final_g, m_conv_w_in, m_conv_w, m_conv_b, m_conv_w_out, m_ssm_w_in, m_ssm_a_re, m_ssm_a_im, m_ssm_log_dt, m_ssm_b_re, m_ssm_b_im, m_ssm_c_re, m_ssm_c_im, m_ssm_d, m_ssm_glu_w, m_ssm_glu_b, m_ssm_w_out, m_sg_w_in, m_sg_v_g, m_sg_w_s, m_sg_b_s, m_sg_w_out, v_ada_w, v_ada_b, v_norm1_g, v_norm2_g, v_ff_w1, v_ff_w2, v_final_g, v_conv_w_in, v_conv_w, v_conv_b, v_conv_w_out, v_ssm_w_in, v_ssm_a_re, v_ssm_a_im, v_ssm_log_dt, v_ssm_b_re, v_ssm_b_im, v_ssm_c_re, v_ssm_c_im, v_ssm_d, v_ssm_glu_w, v_ssm_glu_b, v_ssm_w_out, v_sg_w_in, v_sg_v_g, v_sg_w_s, v_sg_b_s, v_sg_w_out)))
    L, D = x.shape[1], x.shape[2]
    me = _my_index()
    xs = x[0]
    tgt = loss_target[0]
    n_conv = conv_w_in.shape[0]

    def gather_item(shard, axis):
        full = tuple(N_DEV * s if a == axis else s for a, s in enumerate(shard.shape))
        return shard, full, ('gather', axis, shard.shape[axis])

    def mixer_shards(i):
        kind, j = i % 3, i // 3
        if kind == 0:
            return [(conv_w_in[j], 1), (conv_w_out[j], 0)]
        if kind == 1:
            return [(ssm_w_in[j], 0), (ssm_glu_w[j], 0), (ssm_w_out[j], 0)]
        return [(sg_w_in[j], 1), (sg_w_out[j], 0)]

    c_act = c * (1.0 / (1.0 + jnp.exp(-c)))
    vec_rows = jnp.concatenate([c_act.reshape(D // LANE, LANE), conv_w.reshape(-1, LANE), conv_b.reshape(-1, LANE),
                                sg_v_g.reshape(-1, LANE)], 0)
    n_vec = vec_rows.shape[0]
    vec_all = _all_gather(_pad_rows(vec_rows, 24)[None], 0, name="gather_vectors")
    c_all = vec_all[:, :D // LANE, :].reshape(N_DEV, D)
    sharded_full = vec_all[:, D // LANE:n_vec, :].transpose(1, 0, 2).reshape(n_vec - D // LANE, D)
    conv_w_full = sharded_full[:3 * n_conv].reshape(n_conv, 3, D)
    conv_b_full = sharded_full[3 * n_conv:4 * n_conv]
    sg_vg_full = sharded_full[4 * n_conv:4 * n_conv + 1]

    c_pad = _pad_rows(c_all, LANE)
    ncol = ada_w.shape[2]
    mod_part = jnp.stack([_mm(c_pad, ada_w[i], name=f"ada_fwd{i}")[:N_DEV] for i in range(DEPTH)])
    mod_all = _all_gather(mod_part.reshape(1, DEPTH * N_DEV, ncol), 0, name="gather_mod")
    mod_all = mod_all.reshape(N_DEV, DEPTH, N_DEV, ncol)
    mod_me = lax.dynamic_index_in_dim(mod_all, me, 2, keepdims=False)
    mod = mod_me.transpose(1, 0, 2).reshape(DEPTH, N_DEV * ncol) + ada_b
    gathers, gather_token = _exchange_start(
        [[gather_item(w.astype(BF16), ax) for w, ax in shards]
         for i in range(DEPTH) for shards in (mixer_shards(i), [(ff_w1[i], 1), (ff_w2[i], 0)])],
        mod, name="gather_start", relayed=True)
    mod = mod + gather_token[0:1, 0:1]
    relayed = [None] * len(gathers)
    relayed[0], sent = _exchange_relay(gathers[0], mod, name="gather_mix_relay0")

    s5_args = (ssm_a_re[0], ssm_a_im[0], ssm_log_dt[0], ssm_b_re[0], ssm_b_im[0], ssm_c_re[0], ssm_c_im[0])
    (abar_re, abar_im, bblk_re, bblk_im, cblk_re, cblk_im), s5_vjp = jax.vjp(_s5_prep, *s5_args)
    pw_fwd, pw_rev, pos_fwd = _s5_power_tables(abar_re, abar_im, S5_CHUNK // SUBLANE)
    s5_w = tuple(t.astype(BF16) for t in (bblk_re, bblk_im, cblk_re, cblk_im))
    causal = jnp.tril(jnp.ones((SG_CHUNK, SG_CHUNK), dtype=bool))
    ws_m = jnp.where(causal[None], sg_w_s[0], 0.0)
    ws_b = ws_m.astype(BF16)
    wst_b = ws_m.transpose(0, 2, 1).astype(BF16)
    bsb = jnp.broadcast_to(sg_b_s[0][:, :, None], (SG_HEADS, SG_CHUNK, LANE))

    saved = []
    xa = xs
    mods = [[mod[i:i + 1, q * D:(q + 1) * D] for q in range(6)] for i in range(DEPTH)]
    wn1s = [norm1_g[i:i + 1] * (1.0 + mods[i][1]) for i in range(DEPTH)]
    h1 = _normmod_fwd(xa, wn1s[0], mods[0][0] + sent, name="norm1_fwd0")
    for i in range(DEPTH):
        kind, j = i % 3, i // 3
        sh1, sc1, g1, sh2, sc2, g2 = mods[i]
        wn1 = wn1s[i]
        wn2 = norm2_g[i:i + 1] * (1.0 + sc2)
        S = dict(x_in=xa, g1=g1, g2=g2, sc1=sc1, sc2=sc2, wn1=wn1, wn2=wn2)
        w_mix = _exchange_wait(relayed[2 * i], h1, name=f"gather_mix_wait{i}")
        S['h1'] = h1
        if kind == 0:
            bcx = _mm(h1, w_mix[0], name=f"conv_in{i}", out_dtypes=(BF16,), bm=2048)
            wb = _pad_rows(jnp.concatenate([conv_w_full[j], conv_b_full[j:j + 1]], 0), SUBLANE)
            pb = _conv_fwd(bcx, wb, name=f"conv_mix{i}")
            S.update(bcx=bcx, wb=wb, pb=pb)
        elif kind == 1:
            u = _mm(h1, w_mix[0], name=f"ssm_in{i}")
            sre, sim, ypre, yg = _s5_fwd(u, *s5_w, pw_fwd, pos_fwd, ssm_d, name=f"s5_scan{i}")

            def glu_epi(acc, yv, bias):
                t = acc + bias
                return yv * (1.0 / (1.0 + jnp.exp(-t))), t

            pb, tt = _mm(yg, w_mix[1], name=f"ssm_glu{i}", out_dtypes=(BF16, F32), epi=glu_epi,
                         extras=[(yg, 'mn'), (ssm_glu_b, 'n')])
            S.update(u=u, sre=sre, sim=sim, ypre=ypre, yg=yg, pb=pb, tt=tt)
        else:
            uv = _mm(h1, w_mix[0], name=f"sg_in{i}", bm=2048)
            pb = _sg_fwd(uv, sg_vg_full, ws_b, bsb, name=f"sg_mix{i}")
            S.update(uv=uv, pb=pb)
        relayed[2 * i + 1], sent = _exchange_relay(gathers[2 * i + 1], pb, name=f"gather_ff_relay{i}")
        x_mid, y_mix, h2 = _mm(pb, w_mix[-1], name=f"mix_out{i}", out_dtypes=(F32, BF16, BF16), epi=_epi_residual_norm,
                               extras=[(xa, 'mn'), (g1 + sent, 'n'), (wn2, 'n'), (sh2, 'n')])
        w1_full, w2_full = _exchange_wait(relayed[2 * i + 1], h2, name=f"gather_ff_wait{i}")
        S.update(w_mix=w_mix, w1=w1_full, w2=w2_full)
        ra = _mm(h2, w1_full, name=f"ff_up{i}", out_dtypes=(BF16,), epi=lambda acc: (jnp.maximum(acc, 0.0),), bm=2048)
        if i + 1 < DEPTH:
            relayed[2 * i + 2], sent = _exchange_relay(gathers[2 * i + 2], ra, name=f"gather_mix_relay{i + 1}")
            xa, f_out, h1 = _mm(ra, w2_full, name=f"ff_down{i}", out_dtypes=(F32, BF16, BF16), a_fn=_square,
                                epi=_epi_residual_norm, bm=256, bk=w2_full.shape[0],
                                extras=[(x_mid, 'mn'), (g2 + sent, 'n'), (wn1s[i + 1], 'n'), (mods[i + 1][0], 'n')])
        else:
            xa, f_out = _mm(ra, w2_full, name=f"ff_down{i}", out_dtypes=(F32, BF16), epi=_epi_residual, a_fn=_square,
                            extras=[(x_mid, 'mn'), (g2, 'n')], bm=256, bk=w2_full.shape[0])
        S.update(x_mid=x_mid, y_mix=y_mix, h2=h2, ra=ra, f_out=f_out)
        saved.append(S)

    S = saved[-1]
    dx, st, dfb, loss_tile = _loss_head(xa, tgt, final_g[None], S['f_out'], S['g2'], name="loss_head")
    d_final_g = _stat_row(st, 0)
    dg2_next = _stat_row(st, 2)

    def scatter_item(g, axis):
        m = g.shape[axis] // N_DEV
        shard = tuple(m if a == axis else s for a, s in enumerate(g.shape))
        return g, (N_DEV,) + shard, ('scatter', axis, m)

    dmod = [None] * DEPTH
    dn1g, dn2g = [None] * DEPTH, [None] * DEPTH
    d_conv_w, d_conv_b = [None] * n_conv, [None] * n_conv
    ff_sent, mix_sent = [None] * DEPTH, [None] * DEPTH
    small = {}
    for i in reversed(range(DEPTH)):
        kind, j = i % 3, i // 3
        S = saved[i]
        w_mix = S['w_mix']
        dg2 = dg2_next
        da = _mm(dfb, S['w2'], tb=True, name=f"ff_down_bwd{i}", out_dtypes=(BF16,), bm=2048,
                 epi=lambda acc, rav: (acc * (2.0 * rav.astype(F32)),), extras=[(S['ra'], 'mn')])
        dw2 = _wgrad(S['ra'], dfb, name=f"ff_w2_grad{i}", a_fn=_square, bm=256, bn=1024)
        dh2 = _mm(da, S['w1'], tb=True, name=f"ff_up_bwd{i}", out_dtypes=(BF16,), bm=512, bk=da.shape[1])
        dw1 = _wgrad(S['h2'], da, name=f"ff_w1_grad{i}")
        (ff_sent[i],), token = _exchange_start([[scatter_item(dw1, 1), scatter_item(dw2, 0)]], dx,
                                               name=f"ff_grads_start{i}")
        dx_mid, st2, dyb = _normmod_bwd(dh2, S['x_mid'], S['wn2'] + token[0:1, 0:1], dx,
                                        (S['y_mix'], S['g1']), name=f"norm2_bwd{i}")
        dsc2 = _stat_row(st2, 0) * norm2_g[i:i + 1]
        dn2g[i] = _stat_row(st2, 0) * (1.0 + S['sc2'])
        dsh2 = _stat_row(st2, 1)
        dg1 = _stat_row(st2, 2)
        if kind == 0:
            dp = _mm(dyb, w_mix[1], tb=True, name=f"conv_out_bwd{i}", out_dtypes=(BF16,))
            d_cwo = _wgrad(S['pb'], dyb, name=f"conv_w_out_grad{i}")
            dbcx, stc = _conv_bwd(dp, S['bcx'], S['wb'], name=f"conv_mix_bwd{i}")
            d_conv_w[j] = stc[0:3]
            d_conv_b[j] = stc[3:4]
            dh1 = _mm(dbcx, w_mix[0], tb=True, name=f"conv_in_bwd{i}", out_dtypes=(BF16,), bm=512)
            d_cwi = _wgrad(S['h1'], dbcx, name=f"conv_w_in_grad{i}")
            mix_grads = [scatter_item(d_cwi, 1), scatter_item(d_cwo, 0)]
        elif kind == 1:
            dy2 = _mm(dyb, w_mix[2], tb=True, name=f"ssm_out_bwd{i}")
            d_ssm_out = _wgrad(S['pb'], dyb, name=f"ssm_w_out_grad{i}")
            dtb, dya, stg = _glu_bwd(dy2, S['yg'], S['tt'], name=f"ssm_glu_bwd{i}")
            dypre = _mm(dtb, w_mix[1], tb=True, name=f"ssm_glu_in_bwd{i}",
                        epi=lambda acc, a, yp: ((a + acc) * _gelu_grad(yp),),
                        extras=[(dya, 'mn'), (S['ypre'], 'mn')])
            d_glu = _wgrad(S['yg'], dtb, name=f"ssm_glu_w_grad{i}", bm=512)
            dub, dbre, dbim, dcre, dcim, ga, dd = _s5_bwd(dypre, S['u'], S['sre'], S['sim'], *s5_w, pw_rev, pos_fwd, ssm_d,
                                                           name=f"s5_scan_bwd{i}")
            dh1 = _mm(dub, w_mix[0], tb=True, name=f"ssm_in_bwd{i}", out_dtypes=(BF16,))
            d_ssm_in = _wgrad(S['h1'], dub, name=f"ssm_w_in_grad{i}")
            da_re, da_im, dlog_dt, db_re, db_im, dc_re, dc_im = s5_vjp((ga[0:1], ga[1:2], dbre, dbim, dcre, dcim))
            s5_small = _pack_rows([da_re, da_im, dlog_dt, db_re, db_im, dc_re, dc_im, dd[0], stg[0]])
            mix_grads = [scatter_item(d_ssm_in, 0), scatter_item(d_glu, 0), scatter_item(d_ssm_out, 0),
                         gather_item(s5_small.astype(BF16), 0)]
        else:
            dp = _mm(dyb, w_mix[1], tb=True, name=f"sg_out_bwd{i}")
            d_sgo = _wgrad(S['pb'], dyb, name=f"sg_w_out_grad{i}")
            duv, dws, dbs, stv = _sg_bwd(dp, S['uv'], sg_vg_full, ws_b, wst_b, bsb, name=f"sg_mix_bwd{i}")
            dh1 = _mm(duv, w_mix[0], tb=True, name=f"sg_in_bwd{i}", out_dtypes=(BF16,), bm=512, bk=duv.shape[1])
            d_sgi = _wgrad(S['h1'], duv, name=f"sg_w_in_grad{i}")
            sg_small = _pack_rows([jnp.where(causal[None], dws, 0.0), jnp.sum(dbs, axis=-1)])
            d_sg_vg = stv[0:1]
            mix_grads = [scatter_item(d_sgi, 1), scatter_item(d_sgo, 0), gather_item(sg_small.astype(BF16), 0)]
        wn1 = S['wn1']
        if i > 0:
            (mix_sent[i],), token = _exchange_start([mix_grads], dx_mid, name=f"mix_grads_start{i}")
            wn1 = wn1 + token[0:1, 0:1]
            prev = saved[i - 1]
            dx, st1, dfb = _normmod_bwd(dh1, S['x_in'], wn1, dx_mid, (prev['f_out'], prev['g2']),
                                        name=f"norm1_bwd{i}")
            dg2_next = _stat_row(st1, 2)
        else:
            dx, st1 = _normmod_bwd(dh1, S['x_in'], wn1, dx_mid, None, name=f"norm1_bwd{i}")
        dsc1 = _stat_row(st1, 0) * norm1_g[i:i + 1]
        dn1g[i] = _stat_row(st1, 0) * (1.0 + S['sc1'])
        dsh1 = _stat_row(st1, 1)
        dmod[i] = jnp.concatenate([dsh1, dsc1, dg1, dsh2, dsc2, dg2], 1)
    grad_x = dx[None]

    out = {}

    def small_group(names, parts, label):
        shapes = [P[n].shape for n in names]
        w, m, v = (_pack_rows([P[pre + n] for n in names])[None] for pre in ('', 'm_', 'v_'))
        res = [_unpack_rows(t[0], shapes) for t in _adamw(w, [parts], m, v, name=label)]
        for q, n in enumerate(names):
            out[n] = tuple(r[q] for r in res)

    small.update(ada_b=jnp.concatenate(dmod, 0), norm1_g=jnp.concatenate(dn1g, 0), norm2_g=jnp.concatenate(dn2g, 0),
                 final_g=d_final_g, conv_w=jnp.stack(d_conv_w), conv_b=jnp.concatenate(d_conv_b, 0), sg_v_g=d_sg_vg)
    last_pack = _pack_rows([small[n] for n in LAST_SMALL + SMALL_SHARD])
    n_last = _pack_rows([P[n] for n in LAST_SMALL]).shape[0]
    n_pack = last_pack.shape[0]
    pack_all = _all_gather(jnp.concatenate([last_pack, loss_tile], 0)[None], 0, name="gather_small_grads")
    loss = jnp.sum(pack_all[:, n_pack, 0])
    (mix_sent[0],), last_token = _exchange_start([mix_grads], pack_all, name="mix_grads_start0")
    small_group(LAST_SMALL, pack_all[:, :n_last], "adamw_small")
    sh_rows = (n_pack - n_last) // N_DEV
    sh_parts = pack_all[:, n_last:n_pack].reshape(N_DEV, sh_rows, N_DEV, LANE)
    sh_parts = lax.dynamic_index_in_dim(sh_parts, me, 2, keepdims=False)
    sh_parts = jnp.pad(sh_parts, ((0, 0), (0, 16 - sh_rows), (0, 0)))

    def pack_shard(prefix):
        return _pad_rows(jnp.concatenate([P[prefix + n].reshape(-1, LANE) for n in SMALL_SHARD], 0), 16)[None]

    sg_, sd_, sm_, sv_ = _adamw(pack_shard(''), [sh_parts], pack_shard('m_'), pack_shard('v_'), name="adamw_channel")
    off = 0
    for n in SMALL_SHARD:
        rows = math.prod(P[n].shape) // LANE
        out[n] = tuple(t[0, off:off + rows].reshape(P[n].shape) for t in (sg_, sd_, sm_, sv_))
        off += rows

    dmod_all = pack_all[:, :DEPTH * 6 * D // LANE].reshape(N_DEV, DEPTH, 6 * D)
    dmod_cols = lax.dynamic_slice_in_dim(dmod_all, me * ncol, ncol, 2)
    g_ada = [_mm(c_pad, _pad_rows(dmod_cols[:, i], LANE), ta=True, name=f"ada_w_grad{i}")[None] for i in range(DEPTH)]

    def big(name, parts):
        res = _adamw(P[name], parts, P['m_' + name], P['v_' + name], name="adamw_" + name)
        out[name] = res
        return res[1]

    ff_parts = [_exchange_wait(ff_sent[i], last_token, name=f"ff_grads_wait{i}") for i in range(DEPTH)]
    mix_parts = [None] + [_exchange_wait(mix_sent[i], last_token, name=f"mix_grads_wait{i}") for i in range(1, DEPTH)]
    big('ada_w', g_ada)
    big('ff_w1', [p[0] for p in ff_parts])
    big('ff_w2', [p[1] for p in ff_parts])
    done = big('sg_w_in', [mix_parts[2][0]])
    mix_parts[0] = _exchange_wait(mix_sent[0], done, name="mix_grads_wait0")
    big('conv_w_in', [mix_parts[i][0] for i in range(DEPTH) if i % 3 == 0])
    row_names = ['conv_w_out', 'ssm_w_in', 'ssm_glu_w', 'ssm_w_out', 'sg_w_out']
    row_parts = ([mix_parts[i][1] for i in range(DEPTH) if i % 3 == 0] + mix_parts[1][:3] + [mix_parts[2][1]])
    small_group(S5_SMALL, mix_parts[1][3].reshape(N_DEV, -1, LANE), "adamw_s5")
    small_group(SG_SMALL, mix_parts[2][2].reshape(N_DEV, -1, LANE), "adamw_sg")
    row_w, row_m, row_v = (jnp.concatenate([P[pre + n] for n in row_names], 0) for pre in ('', 'm_', 'v_'))
    rw = _adamw(row_w, row_parts, row_m, row_v, name="adamw_row_sharded")
    off = 0
    for n in row_names:
        cnt = P[n].shape[0]
        out[n] = tuple(t[off:off + cnt] for t in rw)
        off += cnt

    return (loss, grad_x, *[out[n][0] for n in WEIGHTS], *[out[n][1] for n in WEIGHTS],
            *[out[n][2] for n in WEIGHTS], *[out[n][3] for n in WEIGHTS])
```

```python
import math

import jax
import jax.numpy as jnp
from jax import lax
from jax.experimental import pallas as pl
from jax.experimental.pallas import tpu as pltpu

F32 = jnp.float32
BF16 = jnp.bfloat16

N_DEV = 8
MESH_ID = pl.DeviceIdType.MESH
DEPTH = 4
EPS = 1e-6
S5_GROUPS, S5_GROUP, S5_STATE = 64, 16, 64
S5_LANES = S5_GROUPS * S5_STATE
S5_BLOCKS = 8
S5_CHUNK = 512
SG_HEADS, SG_CHUNK = 8, 128
LANE = 128
SUBLANE = 8
VMEM_LIMIT = 48 * 1024 * 1024
ADAM_LR, ADAM_B1, ADAM_B2, ADAM_EPS, ADAM_WD, ADAM_STEP = 0.001, 0.9, 0.999, 1e-08, 0.01, 10
GELU_C = math.sqrt(2.0 / math.pi)
GELU_A = 0.044715

WEIGHTS = ['ada_w', 'ada_b', 'norm1_g', 'norm2_g', 'ff_w1', 'ff_w2', 'final_g', 'conv_w_in', 'conv_w', 'conv_b',
           'conv_w_out', 'ssm_w_in', 'ssm_a_re', 'ssm_a_im', 'ssm_log_dt', 'ssm_b_re', 'ssm_b_im', 'ssm_c_re',
           'ssm_c_im', 'ssm_d', 'ssm_glu_w', 'ssm_glu_b', 'ssm_w_out', 'sg_w_in', 'sg_v_g', 'sg_w_s', 'sg_b_s',
           'sg_w_out']
INPUTS = ['x', 'c'] + WEIGHTS + ['loss_target'] + ['m_' + n for n in WEIGHTS] + ['v_' + n for n in WEIGHTS]
S5_SMALL = ['ssm_a_re', 'ssm_a_im', 'ssm_log_dt', 'ssm_b_re', 'ssm_b_im', 'ssm_c_re', 'ssm_c_im', 'ssm_d', 'ssm_glu_b']
SG_SMALL = ['sg_w_s', 'sg_b_s']
LAST_SMALL = ['ada_b', 'norm1_g', 'norm2_g', 'final_g']
SMALL_SHARD = ['conv_w', 'conv_b', 'sg_v_g']


def _params(*sem):
    return pltpu.CompilerParams(dimension_semantics=sem or None, vmem_limit_bytes=VMEM_LIMIT)


def _my_pos():
    return lax.axis_index("x"), lax.axis_index("y"), lax.axis_index("c")


def _my_index():
    x, y, c = _my_pos()
    return 4 * x + 2 * y + c


def _mm(a, b, *, name, ta=False, tb=False, out_dtypes=(F32,), epi=None, extras=(), a_fn=None, n_stats=0, bm=1024,
        bn=1024, bk=1024):
    a_chunks = a.shape[0] if a.ndim == 3 else 0
    b_chunks = b.shape[0] if b.ndim == 3 else 0
    assert not (a_chunks and ta) and not (b_chunks and tb)
    if a_chunks:
        m, k = a.shape[1], a_chunks * a.shape[2]
        bk = k
    else:
        m, k = (a.shape[1], a.shape[0]) if ta else a.shape
    if b_chunks:
        k2, n = b.shape[1], b_chunks * b.shape[2]
        bn = min(bn, b.shape[2])
    else:
        k2, n = (b.shape[1], b.shape[0]) if tb else b.shape
    assert k == k2, (a.shape, b.shape, ta, tb)
    bm, bn, bk = min(bm, m), min(bn, n), min(bk, k)
    assert m % bm == 0 and n % bn == 0 and k % bk == 0, (m, n, k, bm, bn, bk)
    nk = k // bk
    n_ex, n_out = len(extras), len(out_dtypes)
    dims = (((0 if ta else 1,), (1 if tb else 0,)), ((), ()))

    def body(*refs):
        a_ref, b_ref = refs[0], refs[1]
        ex_refs = refs[2:2 + n_ex]
        out_refs = refs[2 + n_ex:2 + n_ex + n_out]

        def finish(acc):
            outs = epi(acc, *[r[...] for r in ex_refs]) if epi is not None else (acc,)
            for r, o in zip(out_refs, outs[:n_out]):
                r[...] = o.astype(r.dtype)
            if n_stats:
                st_ref = refs[2 + n_ex + n_out]

                @pl.when(pl.program_id(0) == 0)
                def _():
                    st_ref[...] = jnp.zeros_like(st_ref)

                for q, row in enumerate(outs[n_out:]):
                    st_ref[q:q + 1, :] += row

        av = jnp.concatenate([a_ref[t] for t in range(a_chunks)], axis=1) if a_chunks else a_ref[...]
        if a_fn is not None:
            av = a_fn(av)
        part = lax.dot_general(av.astype(BF16), b_ref[...].astype(BF16), dims, preferred_element_type=F32)
        if nk == 1:
            finish(part)
            return
        acc_ref = refs[-1]
        kk = pl.program_id(2)

        @pl.when(kk == 0)
        def _():
            acc_ref[...] = part

        @pl.when(kk > 0)
        def _():
            acc_ref[...] += part

        @pl.when(kk == nk - 1)
        def _():
            finish(acc_ref[...])

    if a_chunks:
        a_spec = pl.BlockSpec((a_chunks, bm, a.shape[2]), lambda i, j, q: (0, i, 0))
    elif ta:
        a_spec = pl.BlockSpec((bk, bm), lambda i, j, q: (q, i))
    else:
        a_spec = pl.BlockSpec((bm, bk), lambda i, j, q: (i, q))
    if b_chunks:
        per = b.shape[2] // bn
        b_spec = pl.BlockSpec((None, bk, bn), lambda i, j, q: (j // per, q, j % per))
    elif tb:
        b_spec = pl.BlockSpec((bn, bk), lambda i, j, q: (j, q))
    else:
        b_spec = pl.BlockSpec((bk, bn), lambda i, j, q: (q, j))
    ex_specs = []
    for arr, kind in extras:
        if kind == 'mn':
            assert arr.shape == (m, n), (arr.shape, m, n)
            ex_specs.append(pl.BlockSpec((bm, bn), lambda i, j, q: (i, j)))
        else:
            assert arr.shape == (1, n), (arr.shape, n)
            ex_specs.append(pl.BlockSpec((1, bn), lambda i, j, q: (0, j)))
    out_shape = [jax.ShapeDtypeStruct((m, n), d) for d in out_dtypes]
    out_specs = [pl.BlockSpec((bm, bn), lambda i, j, q: (i, j)) for _ in out_dtypes]
    if n_stats:
        assert n_stats <= SUBLANE
        out_shape.append(jax.ShapeDtypeStruct((SUBLANE, n), F32))
        out_specs.append(pl.BlockSpec((SUBLANE, bn), lambda i, j, q: (0, j)))
    outs = pl.pallas_call(
        body, name=name, out_shape=tuple(out_shape), grid=(m // bm, n // bn, nk),
        in_specs=[a_spec, b_spec] + ex_specs, out_specs=tuple(out_specs),
        scratch_shapes=[pltpu.VMEM((bm, bn), F32)] if nk > 1 else [],
        compiler_params=_params(*(["arbitrary"] * 3 if n_stats else ["parallel", "parallel", "arbitrary"])),
    )(a, b, *[arr for arr, _ in extras])
    return outs if len(outs) > 1 else outs[0]


def _epi_residual(acc, res, gate):
    return res + gate * acc, acc


def _epi_residual_norm(acc, res, gate, w, sh):
    xn = res + gate * acc
    return xn, acc, xn * _rstd(xn) * w + sh


def _epi_norm_bwd(gated):
    def epi(dh, xv, w, dres, *gate):
        rstd = _rstd(xv)
        xn = xv * rstd
        dxn = dh * w
        dx = rstd * (dxn - xn * jnp.mean(dxn * xn, axis=-1, keepdims=True)) + dres
        stats = [jnp.sum(dh * xn, axis=0, keepdims=True), jnp.sum(dh, axis=0, keepdims=True)]
        if not gated:
            return (dx, *stats)
        yv, g = gate
        return (dx, dx * g, *stats, jnp.sum(dx * yv.astype(F32), axis=0, keepdims=True))
    return epi


def _wgrad(acts, cots, *, name, a_fn=None, bm=1024, bn=512):
    return _mm(acts, cots, ta=True, name=name, out_dtypes=(BF16,), a_fn=a_fn, bm=bm, bn=bn, bk=acts.shape[0])


def _square(a):
    af = a.astype(F32)
    return af * af


def _rstd(xv):
    return lax.rsqrt(jnp.mean(xv * xv, axis=-1, keepdims=True) + EPS)


def _normmod_fwd(x, w, sh, *, name, tm=512):
    L, D = x.shape

    def body(x_ref, w_ref, s_ref, h_ref):
        xv = x_ref[...]
        h_ref[...] = (xv * _rstd(xv) * w_ref[...] + s_ref[...]).astype(h_ref.dtype)

    row = pl.BlockSpec((tm, D), lambda i: (i, 0))
    vec = pl.BlockSpec((1, D), lambda i: (0, 0))
    return pl.pallas_call(body, name=name, out_shape=jax.ShapeDtypeStruct((L, D), BF16), grid=(L // tm,),
                          in_specs=[row, vec, vec], out_specs=row, compiler_params=_params("parallel"))(x, w, sh)


def _normmod_bwd(dh, x, w, dres, gate, *, name, tm=256):
    L, D = x.shape
    has_gate = gate is not None

    def body(*refs):
        if has_gate:
            dh_ref, x_ref, w_ref, r_ref, y_ref, g_ref, dx_ref, st_ref, dy_ref = refs
        else:
            dh_ref, x_ref, w_ref, r_ref, dx_ref, st_ref = refs
        i = pl.program_id(0)

        @pl.when(i == 0)
        def _():
            st_ref[...] = jnp.zeros_like(st_ref)

        xv = x_ref[...]
        dhv = dh_ref[...].astype(F32)
        rstd = _rstd(xv)
        xn = xv * rstd
        dxn = dhv * w_ref[...]
        dx = rstd * (dxn - xn * jnp.mean(dxn * xn, axis=-1, keepdims=True)) + r_ref[...]
        dx_ref[...] = dx
        st_ref[0:1, :] += jnp.sum(dhv * xn, axis=0, keepdims=True)
        st_ref[1:2, :] += jnp.sum(dhv, axis=0, keepdims=True)
        if has_gate:
            dy_ref[...] = (dx * g_ref[...]).astype(dy_ref.dtype)
            st_ref[2:3, :] += jnp.sum(dx * y_ref[...].astype(F32), axis=0, keepdims=True)

    row = pl.BlockSpec((tm, D), lambda i: (i, 0))
    vec = pl.BlockSpec((1, D), lambda i: (0, 0))
    st = pl.BlockSpec((SUBLANE, D), lambda i: (0, 0))
    in_specs = [row, row, vec, row] + ([row, vec] if has_gate else [])
    out_shape = [jax.ShapeDtypeStruct((L, D), F32), jax.ShapeDtypeStruct((SUBLANE, D), F32)]
    out_specs = [row, st]
    if has_gate:
        out_shape.append(jax.ShapeDtypeStruct((L, D), BF16))
        out_specs.append(row)
    args = (dh, x, w, dres) + (tuple(gate) if has_gate else ())
    return pl.pallas_call(body, name=name, out_shape=tuple(out_shape), grid=(L // tm,), in_specs=in_specs,
                          out_specs=tuple(out_specs), compiler_params=_params("arbitrary"))(*args)


def _loss_head(x, tgt, fg, y, g, *, name, tm=256):
    L, D = x.shape

    def body(x_ref, t_ref, fg_ref, y_ref, g_ref, dx_ref, st_ref, dy_ref, loss_ref):
        i = pl.program_id(0)

        @pl.when(i == 0)
        def _():
            st_ref[...] = jnp.zeros_like(st_ref)
            loss_ref[...] = jnp.zeros_like(loss_ref)

        xv = x_ref[...]
        rstd = _rstd(xv)
        xn = xv * rstd
        err = xn * fg_ref[...] - t_ref[...]
        loss_ref[...] += 0.5 * jnp.sum(jnp.mean(err * err, axis=-1, keepdims=True))
        dout = err * (1.0 / D)
        dxn = dout * fg_ref[...]
        dx = rstd * (dxn - xn * jnp.mean(dxn * xn, axis=-1, keepdims=True))
        dx_ref[...] = dx
        dy_ref[...] = (dx * g_ref[...]).astype(dy_ref.dtype)
        st_ref[0:1, :] += jnp.sum(dout * xn, axis=0, keepdims=True)
        st_ref[2:3, :] += jnp.sum(dx * y_ref[...].astype(F32), axis=0, keepdims=True)

    row = pl.BlockSpec((tm, D), lambda i: (i, 0))
    vec = pl.BlockSpec((1, D), lambda i: (0, 0))
    return pl.pallas_call(
        body, name=name,
        out_shape=(jax.ShapeDtypeStruct((L, D), F32), jax.ShapeDtypeStruct((SUBLANE, D), F32),
                   jax.ShapeDtypeStruct((L, D), BF16), jax.ShapeDtypeStruct((SUBLANE, LANE), F32)),
        grid=(L // tm,), in_specs=[row, row, vec, row, vec],
        out_specs=(row, pl.BlockSpec((SUBLANE, D), lambda i: (0, 0)), row,
                   pl.BlockSpec((SUBLANE, LANE), lambda i: (0, 0))),
        compiler_params=_params("arbitrary"))(x, tgt, fg, y, g)


def _shift_down(v, k):
    row = lax.broadcasted_iota(jnp.int32, v.shape, 0)
    return jnp.where(row >= k, pltpu.roll(v, k, 0), 0.0)


def _shift_up(v, k):
    n = v.shape[0]
    row = lax.broadcasted_iota(jnp.int32, v.shape, 0)
    return jnp.where(row < n - k, pltpu.roll(v, n - k, 0), 0.0)


def _conv_views(L, D):
    return [pl.BlockSpec((L, LANE), lambda j, s=s: (0, s * (D // LANE) + j)) for s in range(3)]


def _conv_fwd(bcx, wb, *, name):
    L, D = bcx.shape[0], bcx.shape[1] // 3

    def body(b_ref, c_ref, x_ref, wb_ref, p_ref):
        z = c_ref[...].astype(F32) * x_ref[...].astype(F32)
        conv = (wb_ref[0:1, :] * _shift_down(z, 2) + wb_ref[1:2, :] * _shift_down(z, 1)
                + wb_ref[2:3, :] * z + wb_ref[3:4, :])
        p_ref[...] = (b_ref[...].astype(F32) * conv).astype(p_ref.dtype)

    col = pl.BlockSpec((L, LANE), lambda j: (0, j))
    return pl.pallas_call(body, name=name, out_shape=jax.ShapeDtypeStruct((L, D), BF16), grid=(D // LANE,),
                          in_specs=_conv_views(L, D) + [pl.BlockSpec((SUBLANE, LANE), lambda j: (0, j))],
                          out_specs=col, compiler_params=_params("parallel"))(bcx, bcx, bcx, wb)


def _conv_bwd(dp, bcx, wb, *, name):
    L, D = dp.shape

    def body(dp_ref, b_ref, c_ref, x_ref, wb_ref, d3_ref, st_ref):
        cv, xv = c_ref[...].astype(F32), x_ref[...].astype(F32)
        z = cv * xv
        z1, z2 = _shift_down(z, 1), _shift_down(z, 2)
        w0, w1, w2 = wb_ref[0:1, :], wb_ref[1:2, :], wb_ref[2:3, :]
        conv = w0 * z2 + w1 * z1 + w2 * z + wb_ref[3:4, :]
        dpv = dp_ref[...].astype(F32)
        d3_ref[0] = (dpv * conv).astype(d3_ref.dtype)
        dconv = dpv * b_ref[...].astype(F32)
        dz = w2 * dconv + w1 * _shift_up(dconv, 1) + w0 * _shift_up(dconv, 2)
        d3_ref[1] = (dz * xv).astype(d3_ref.dtype)
        d3_ref[2] = (dz * cv).astype(d3_ref.dtype)
        st_ref[...] = jnp.zeros_like(st_ref)
        st_ref[0:1, :] = jnp.sum(dconv * z2, axis=0, keepdims=True)
        st_ref[1:2, :] = jnp.sum(dconv * z1, axis=0, keepdims=True)
        st_ref[2:3, :] = jnp.sum(dconv * z, axis=0, keepdims=True)
        st_ref[3:4, :] = jnp.sum(dconv, axis=0, keepdims=True)

    col = pl.BlockSpec((L, LANE), lambda j: (0, j))
    vec = pl.BlockSpec((SUBLANE, LANE), lambda j: (0, j))
    return pl.pallas_call(body, name=name,
                          out_shape=(jax.ShapeDtypeStruct((3, L, D), BF16), jax.ShapeDtypeStruct((SUBLANE, D), F32)),
                          grid=(D // LANE,), in_specs=[col] + _conv_views(L, D) + [vec],
                          out_specs=(pl.BlockSpec((3, L, LANE), lambda j: (0, 0, j)), vec),
                          compiler_params=_params("parallel"))(dp, bcx, bcx, bcx, wb)


def _sg_fwd(uv, vg, ws, bsb, *, name, tr=512):
    L, D = uv.shape[0], uv.shape[1] // 2

    def body(uv_ref, vg_ref, ws_ref, bsb_ref, p_ref):
        for ci in range(tr // SG_CHUNK):
            rows = slice(ci * SG_CHUNK, (ci + 1) * SG_CHUNK)
            v = uv_ref[rows, D:2 * D]
            vn = (v * _rstd(v) * vg_ref[...]).astype(BF16)
            for h in range(SG_HEADS):
                cols = slice(h * LANE, (h + 1) * LANE)
                vm = jnp.dot(ws_ref[h], vn[:, cols], preferred_element_type=F32) + bsb_ref[h]
                p_ref[rows, cols] = (uv_ref[rows, cols] * vm).astype(p_ref.dtype)

    full3 = pl.BlockSpec((SG_HEADS, SG_CHUNK, LANE), lambda i: (0, 0, 0))
    return pl.pallas_call(body, name=name, out_shape=jax.ShapeDtypeStruct((L, D), BF16), grid=(L // tr,),
                          in_specs=[pl.BlockSpec((tr, 2 * D), lambda i: (i, 0)), pl.BlockSpec((1, D), lambda i: (0, 0)),
                                    full3, full3],
                          out_specs=pl.BlockSpec((tr, D), lambda i: (i, 0)),
                          compiler_params=_params("parallel"))(uv, vg, ws, bsb)


def _sg_bwd(dp, uv, vg, ws, wst, bsb, *, name, tr=512):
    L, D = dp.shape

    def body(dp_ref, uv_ref, vg_ref, ws_ref, wst_ref, bsb_ref, duv_ref, dws_ref, dbs_ref, st_ref, dvn_ref):
        i = pl.program_id(0)

        @pl.when(i == 0)
        def _():
            dws_ref[...] = jnp.zeros_like(dws_ref)
            dbs_ref[...] = jnp.zeros_like(dbs_ref)
            st_ref[...] = jnp.zeros_like(st_ref)

        for ci in range(tr // SG_CHUNK):
            rows = slice(ci * SG_CHUNK, (ci + 1) * SG_CHUNK)
            v = uv_ref[rows, D:2 * D]
            rstd = _rstd(v)
            vhat = v * rstd
            vn = (vhat * vg_ref[...]).astype(BF16)
            for h in range(SG_HEADS):
                cols = slice(h * LANE, (h + 1) * LANE)
                vm = jnp.dot(ws_ref[h], vn[:, cols], preferred_element_type=F32) + bsb_ref[h]
                dph = dp_ref[rows, cols]
                duv_ref[rows, cols] = (dph * vm).astype(duv_ref.dtype)
                dvm = dph * uv_ref[rows, cols]
                dbs_ref[h] += dvm
                dvmb = dvm.astype(BF16)
                dws_ref[h] += lax.dot_general(dvmb, vn[:, cols], (((1,), (1,)), ((), ())),
                                              preferred_element_type=F32)
                dvn_ref[rows, cols] = jnp.dot(wst_ref[h], dvmb, preferred_element_type=F32)
            dvn = dvn_ref[rows, :]
            gv = dvn * vg_ref[...]
            dv = rstd * (gv - vhat * jnp.mean(gv * vhat, axis=-1, keepdims=True))
            duv_ref[rows, D:2 * D] = dv.astype(duv_ref.dtype)
            st_ref[0:1, :] += jnp.sum(dvn * vhat, axis=0, keepdims=True)

    full3 = pl.BlockSpec((SG_HEADS, SG_CHUNK, LANE), lambda i: (0, 0, 0))
    acc3 = jax.ShapeDtypeStruct((SG_HEADS, SG_CHUNK, LANE), F32)
    return pl.pallas_call(
        body, name=name,
        out_shape=(jax.ShapeDtypeStruct((L, 2 * D), BF16), acc3, acc3, jax.ShapeDtypeStruct((SUBLANE, D), F32)),
        grid=(L // tr,),
        in_specs=[pl.BlockSpec((tr, D), lambda i: (i, 0)), pl.BlockSpec((tr, 2 * D), lambda i: (i, 0)),
                  pl.BlockSpec((1, D), lambda i: (0, 0)), full3, full3, full3],
        out_specs=(pl.BlockSpec((tr, 2 * D), lambda i: (i, 0)), full3, full3,
                   pl.BlockSpec((SUBLANE, D), lambda i: (0, 0))),
        scratch_shapes=[pltpu.VMEM((tr, D), F32)],
        compiler_params=_params("arbitrary"))(dp, uv, vg, ws, wst, bsb)


def _gelu(x):
    return 0.5 * x * (1.0 + jnp.tanh(GELU_C * (x + GELU_A * x * x * x)))


def _gelu_grad(x):
    th = jnp.tanh(GELU_C * (x + GELU_A * x * x * x))
    return 0.5 * (1.0 + th) + 0.5 * x * (1.0 - th * th) * GELU_C * (1.0 + 3.0 * GELU_A * x * x)


def _cmul_add(xr, xi, ar, ai, br, bi):
    return xr + ar * br - ai * bi, xi + ar * bi + ai * br


def _cmul_conj_add(xr, xi, ar, ai, br, bi):
    return xr + ar * br + ai * bi, xi + ar * bi - ai * br


def _to_subchunk_order(src_ref, dst_ref, n):
    for k in range(n):
        dst_ref[pl.ds(SUBLANE * k, SUBLANE), :] = src_ref[pl.ds(k, SUBLANE, stride=n), :].astype(dst_ref.dtype)


def _to_time_order(src_ref, dst_ref, n):
    for m in range(n):
        r, k = divmod(SUBLANE * m, n)
        dst_ref[pl.ds(SUBLANE * m, SUBLANE), :] = src_ref[pl.ds(SUBLANE * k + r, SUBLANE, stride=SUBLANE), :]


def _s5_fwd(u, bre, bim, cre, cim, pw, pos, dsk, *, name, tc=S5_CHUNK):
    L, D = u.shape
    W = S5_LANES // S5_BLOCKS
    nt = L // tc
    n = tc // SUBLANE

    def sub(k):
        return pl.ds(SUBLANE * k, SUBLANE)

    def body(u_ref, bre_ref, bim_ref, cre_ref, cim_ref, pw_ref, pos_ref, d_ref, sre_ref, sim_ref, ypre_ref, yg_ref,
             carry, up, yp):
        t = pl.program_id(1)

        @pl.when(t == 0)
        def _():
            carry[...] = jnp.zeros_like(carry)

        _to_subchunk_order(u_ref, up, n)
        uv = up[...]
        ub = uv.astype(BF16)
        sre_ref[...] = jnp.dot(ub, bre_ref[...], preferred_element_type=F32)
        sim_ref[...] = jnp.dot(ub, bim_ref[...], preferred_element_type=F32)

        ar, ai = pw_ref[8], pw_ref[9]
        xr = jnp.zeros((SUBLANE, W), F32)
        xi = jnp.zeros((SUBLANE, W), F32)
        for k in range(n):
            xr, xi = _cmul_add(sre_ref[sub(k), :], sim_ref[sub(k), :], ar, ai, xr, xi)
            sre_ref[sub(k), :] = xr
            sim_ref[sub(k), :] = xi
        for q, d in enumerate((1, 2, 4)):
            xr, xi = _cmul_add(xr, xi, pw_ref[2 * q], pw_ref[2 * q + 1], pltpu.roll(xr, d, 0), pltpu.roll(xi, d, 0))
        cr, ci = carry[0], carry[1]
        xr, xi = _cmul_add(xr, xi, pw_ref[6], pw_ref[7], cr, ci)
        first = lax.broadcasted_iota(jnp.int32, (SUBLANE, W), 0) == 0
        er = jnp.where(first, cr, pltpu.roll(xr, 1, 0))
        ei = jnp.where(first, ci, pltpu.roll(xi, 1, 0))
        last = slice(SUBLANE - 1, SUBLANE)
        carry[0] = jnp.broadcast_to(xr[last, :], (SUBLANE, W))
        carry[1] = jnp.broadcast_to(xi[last, :], (SUBLANE, W))
        for k in range(n):
            sr, si = _cmul_add(sre_ref[sub(k), :], sim_ref[sub(k), :], pos_ref[0, k:k + 1, :], pos_ref[1, k:k + 1, :],
                               er, ei)
            sre_ref[sub(k), :] = sr
            sim_ref[sub(k), :] = si
        yp[...] = (jnp.dot(sre_ref[...].astype(BF16), cre_ref[...], preferred_element_type=F32)
                   - jnp.dot(sim_ref[...].astype(BF16), cim_ref[...], preferred_element_type=F32) + d_ref[...] * uv)
        _to_time_order(yp, ypre_ref, n)
        yg_ref[...] = _gelu(ypre_ref[...])

    ch = pl.BlockSpec((tc, LANE), lambda j, t: (t, j))
    st = pl.BlockSpec((tc, W), lambda j, t: (t, j))
    bsp = pl.BlockSpec((None, LANE, W), lambda j, t: (j, 0, 0))
    csp = pl.BlockSpec((None, W, LANE), lambda j, t: (j, 0, 0))
    return pl.pallas_call(
        body, name=name,
        out_shape=(jax.ShapeDtypeStruct((L, S5_LANES), F32), jax.ShapeDtypeStruct((L, S5_LANES), F32),
                   jax.ShapeDtypeStruct((L, D), F32), jax.ShapeDtypeStruct((L, D), F32)),
        grid=(S5_BLOCKS, nt),
        in_specs=[ch, bsp, bsp, csp, csp, pl.BlockSpec((10, SUBLANE, W), lambda j, t: (0, 0, j)),
                  pl.BlockSpec((2, n, W), lambda j, t: (0, 0, j)), pl.BlockSpec((1, LANE), lambda j, t: (0, j))],
        out_specs=(st, st, ch, ch),
        scratch_shapes=[pltpu.VMEM((2, SUBLANE, W), F32), pltpu.VMEM((tc, LANE), F32), pltpu.VMEM((tc, LANE), F32)],
        compiler_params=_params("parallel", "arbitrary"))(u, bre, bim, cre, cim, pw, pos, dsk)


def _s5_bwd(dy, u, sre, sim, bre, bim, cre, cim, pwr, posr, dsk, *, name, tc=S5_CHUNK):
    L, D = u.shape
    W = S5_LANES // S5_BLOCKS
    nt = L // tc
    n = tc // SUBLANE
    nt_dims = (((1,), (1,)), ((), ()))
    tn_dims = (((0,), (0,)), ((), ()))

    def sub(k):
        return pl.ds(SUBLANE * k, SUBLANE)

    def body(dy_ref, u_ref, sre_ref, sim_ref, bre_ref, bim_ref, cre_ref, cim_ref, pw_ref, pos_ref, d_ref,
             du_ref, dbre_ref, dbim_ref, dcre_ref, dcim_ref, ga_ref, dd_ref, gre, gim, carry, gacc, up, dyp):
        t = pl.program_id(1)

        @pl.when(t == 0)
        def _():
            for r in (carry, gacc, dbre_ref, dbim_ref, dcre_ref, dcim_ref, ga_ref, dd_ref):
                r[...] = jnp.zeros_like(r)

        _to_subchunk_order(dy_ref, dyp, n)
        _to_subchunk_order(u_ref, up, n)
        dyv, uv = dyp[...], up[...]
        dyb, ub = dyv.astype(BF16), uv.astype(BF16)
        gre[...] = lax.dot_general(dyb, cre_ref[...], nt_dims, preferred_element_type=F32)
        gim[...] = -lax.dot_general(dyb, cim_ref[...], nt_dims, preferred_element_type=F32)
        br, bi = pw_ref[8], pw_ref[9]
        xr = jnp.zeros((SUBLANE, W), F32)
        xi = jnp.zeros((SUBLANE, W), F32)
        for k in reversed(range(n)):
            xr, xi = _cmul_add(gre[sub(k), :], gim[sub(k), :], br, bi, xr, xi)
            gre[sub(k), :] = xr
            gim[sub(k), :] = xi
        for q, d in enumerate((1, 2, 4)):
            xr, xi = _cmul_add(xr, xi, pw_ref[2 * q], pw_ref[2 * q + 1], pltpu.roll(xr, SUBLANE - d, 0),
                               pltpu.roll(xi, SUBLANE - d, 0))
        cr, ci = carry[0], carry[1]
        xr, xi = _cmul_add(xr, xi, pw_ref[6], pw_ref[7], cr, ci)
        top = lax.broadcasted_iota(jnp.int32, (SUBLANE, W), 0) == SUBLANE - 1
        er = jnp.where(top, cr, pltpu.roll(xr, SUBLANE - 1, 0))
        ei = jnp.where(top, ci, pltpu.roll(xi, SUBLANE - 1, 0))
        carry[0] = jnp.broadcast_to(xr[0:1, :], (SUBLANE, W))
        carry[1] = jnp.broadcast_to(xi[0:1, :], (SUBLANE, W))
        nr, ni = er, ei
        acc_r = jnp.zeros((SUBLANE, W), F32)
        acc_i = jnp.zeros((SUBLANE, W), F32)
        for k in reversed(range(n)):
            place = slice(n - 1 - k, n - k)
            gr, gi = _cmul_conj_add(gre[sub(k), :], gim[sub(k), :], pos_ref[0, place, :], pos_ref[1, place, :], er, ei)
            gre[sub(k), :] = gr
            gim[sub(k), :] = gi
            sr, si = sre_ref[sub(k), :], sim_ref[sub(k), :]
            acc_r = acc_r + sr * nr + si * ni
            acc_i = acc_i + sr * ni - si * nr
            nr, ni = gr, gi
        gacc[0] += acc_r
        gacc[1] += acc_i
        grb, gib = gre[...].astype(BF16), gim[...].astype(BF16)
        dyp[...] = (lax.dot_general(grb, bre_ref[...], nt_dims, preferred_element_type=F32)
                    + lax.dot_general(gib, bim_ref[...], nt_dims, preferred_element_type=F32) + d_ref[...] * dyv)
        _to_time_order(dyp, up, n)
        du_ref[...] = up[...].astype(du_ref.dtype)
        dbre_ref[...] += lax.dot_general(ub, grb, tn_dims, preferred_element_type=F32)
        dbim_ref[...] += lax.dot_general(ub, gib, tn_dims, preferred_element_type=F32)
        dcre_ref[...] += lax.dot_general(sre_ref[...].astype(BF16), dyb, tn_dims, preferred_element_type=F32)
        dcim_ref[...] -= lax.dot_general(sim_ref[...].astype(BF16), dyb, tn_dims, preferred_element_type=F32)
        dd_ref[0:1, :] += jnp.sum(dyv * uv, axis=0, keepdims=True)

        @pl.when(t == nt - 1)
        def _():
            ga_ref[0:1, :] = jnp.sum(gacc[0], axis=0, keepdims=True)
            ga_ref[1:2, :] = jnp.sum(gacc[1], axis=0, keepdims=True)

    ch = pl.BlockSpec((tc, LANE), lambda j, t: (nt - 1 - t, j))
    st = pl.BlockSpec((tc, W), lambda j, t: (nt - 1 - t, j))
    bsp = pl.BlockSpec((None, LANE, W), lambda j, t: (j, 0, 0))
    csp = pl.BlockSpec((None, W, LANE), lambda j, t: (j, 0, 0))
    return pl.pallas_call(
        body, name=name,
        out_shape=(jax.ShapeDtypeStruct((L, D), BF16),
                   jax.ShapeDtypeStruct((S5_BLOCKS, LANE, W), F32), jax.ShapeDtypeStruct((S5_BLOCKS, LANE, W), F32),
                   jax.ShapeDtypeStruct((S5_BLOCKS, W, LANE), F32), jax.ShapeDtypeStruct((S5_BLOCKS, W, LANE), F32),
                   jax.ShapeDtypeStruct((SUBLANE, S5_LANES), F32), jax.ShapeDtypeStruct((SUBLANE, D), F32)),
        grid=(S5_BLOCKS, nt),
        in_specs=[ch, ch, st, st, bsp, bsp, csp, csp, pl.BlockSpec((10, SUBLANE, W), lambda j, t: (0, 0, j)),
                  pl.BlockSpec((2, n, W), lambda j, t: (0, 0, j)), pl.BlockSpec((1, LANE), lambda j, t: (0, j))],
        out_specs=(ch, bsp, bsp, csp, csp, pl.BlockSpec((SUBLANE, W), lambda j, t: (0, j)),
                   pl.BlockSpec((SUBLANE, LANE), lambda j, t: (0, j))),
        scratch_shapes=[pltpu.VMEM((tc, W), F32), pltpu.VMEM((tc, W), F32), pltpu.VMEM((2, SUBLANE, W), F32),
                        pltpu.VMEM((2, SUBLANE, W), F32), pltpu.VMEM((tc, LANE), F32), pltpu.VMEM((tc, LANE), F32)],
        compiler_params=_params("parallel", "arbitrary"))(dy, u, sre, sim, bre, bim, cre, cim, pwr, posr, dsk)


def _glu_bwd(dy2, y, t, *, name, tm=256):
    L, D = y.shape

    def body(dy2_ref, y_ref, t_ref, dt_ref, dya_ref, st_ref):
        i = pl.program_id(0)

        @pl.when(i == 0)
        def _():
            st_ref[...] = jnp.zeros_like(st_ref)

        sig = 1.0 / (1.0 + jnp.exp(-t_ref[...]))
        dy2v = dy2_ref[...]
        dt = dy2v * y_ref[...] * sig * (1.0 - sig)
        dt_ref[...] = dt.astype(dt_ref.dtype)
        dya_ref[...] = dy2v * sig
        st_ref[0:1, :] += jnp.sum(dt, axis=0, keepdims=True)

    row = pl.BlockSpec((tm, D), lambda i: (i, 0))
    return pl.pallas_call(
        body, name=name,
        out_shape=(jax.ShapeDtypeStruct((L, D), BF16), jax.ShapeDtypeStruct((L, D), F32),
                   jax.ShapeDtypeStruct((SUBLANE, D), F32)),
        grid=(L // tm,), in_specs=[row, row, row],
        out_specs=(row, row, pl.BlockSpec((SUBLANE, D), lambda i: (0, 0))),
        compiler_params=_params("arbitrary"))(dy2, y, t)


def _s5_prep(a_re, a_im, log_dt, b_re, b_im, c_re, c_im):
    dt = jnp.exp(log_dt)[:, None]
    mag = jnp.exp(a_re * dt)
    abar_re = mag * jnp.cos(a_im * dt)
    abar_im = mag * jnp.sin(a_im * dt)
    den = a_re * a_re + a_im * a_im
    nr = abar_re - 1.0
    ni = abar_im
    f_re = ((nr * a_re + ni * a_im) / den)[..., None]
    f_im = ((ni * a_re - nr * a_im) / den)[..., None]
    bbar_re = f_re * b_re - f_im * b_im
    bbar_im = f_re * b_im + f_im * b_re
    eye = jnp.eye(S5_GROUPS // S5_BLOCKS, dtype=F32)
    gb = S5_GROUPS // S5_BLOCKS

    def blk_b(bb):
        t = bb.reshape(S5_BLOCKS, gb, S5_STATE, S5_GROUP)
        return jnp.einsum('jgph,gk->jghkp', t, eye).reshape(S5_BLOCKS, gb * S5_GROUP, gb * S5_STATE)

    def blk_c(cc):
        t = cc.reshape(S5_BLOCKS, gb, S5_GROUP, S5_STATE)
        return jnp.einsum('jghp,gk->jgpkh', t, eye).reshape(S5_BLOCKS, gb * S5_STATE, gb * S5_GROUP)

    return (abar_re.reshape(1, S5_LANES), abar_im.reshape(1, S5_LANES), blk_b(bbar_re), blk_b(bbar_im),
            blk_c(c_re), blk_c(c_im))


def _cpowers(ar, ai, count):
    pr, pi, m = ar, ai, 1
    while m < count:
        tr, ti = pr[m - 1:m], pi[m - 1:m]
        pr, pi = jnp.concatenate([pr, pr * tr - pi * ti], 0), jnp.concatenate([pi, pr * ti + pi * tr], 0)
        m *= 2
    return pr, pi


def _s5_power_tables(ar, ai, n):
    pr, pi = _cpowers(ar, ai, n)
    qr, qi = _cpowers(pr[n - 1:n], pi[n - 1:n], SUBLANE)
    row = jnp.arange(SUBLANE)[:, None]
    lanes = ar.shape[1]

    def tables(sign, keep, order):
        out = []
        for d in (1, 2, 4):
            out += [jnp.where(keep(d), qr[d - 1:d], 0.0), jnp.where(keep(d), sign * qi[d - 1:d], 0.0)]
        out += [jnp.concatenate([qr[r:r + 1] for r in order], 0), sign * jnp.concatenate([qi[r:r + 1] for r in order], 0),
                ar, sign * ai]
        return jnp.stack([jnp.broadcast_to(o, (SUBLANE, lanes)) for o in out])

    fwd = tables(1.0, lambda d: row >= d, list(range(SUBLANE)))
    rev = tables(-1.0, lambda d: row + d <= SUBLANE - 1, list(reversed(range(SUBLANE))))
    return fwd, rev, jnp.stack([pr, pi])


ADAMW_PART_BLOCK_BYTES = 2 * 1024 * 1024


def _adamw(w, parts, m, v, *, name):
    n, R, C = w.shape
    assert len(parts) == n
    P = parts[0].shape[0]
    tr = R
    while P * tr * C * parts[0].dtype.itemsize > ADAMW_PART_BLOCK_BYTES and tr % 16 == 0:
        tr //= 2
    c1 = 1.0 / (1.0 - ADAM_B1 ** ADAM_STEP)
    c2 = 1.0 / (1.0 - ADAM_B2 ** ADAM_STEP)

    def body(*refs):
        w_ref, m_ref, v_ref = refs[:3]
        p_refs = refs[3:3 + n]
        g_ref, d_ref, nm_ref, nv_ref = refs[3 + n:]
        layer = pl.program_id(0)
        for q, p_ref in enumerate(p_refs):
            @pl.when(layer == q)
            def _(p_ref=p_ref):
                g = p_ref[0].astype(F32)
                for s in range(1, P):
                    g = g + p_ref[s].astype(F32)
                nm = ADAM_B1 * m_ref[...] + (1.0 - ADAM_B1) * g
                nv = ADAM_B2 * v_ref[...] + (1.0 - ADAM_B2) * (g * g)
                g_ref[...] = g
                nm_ref[...] = nm
                nv_ref[...] = nv
                d_ref[...] = -ADAM_LR * ((nm * c1) / (jnp.sqrt(nv * c2) + ADAM_EPS) + ADAM_WD * w_ref[...])

    row = pl.BlockSpec((None, tr, C), lambda l, i: (l, i, 0))
    part_specs = [pl.BlockSpec((P, tr, C), lambda l, i, q=q: (0, jnp.where(l == q, i, 0), 0)) for q in range(n)]
    out = jax.ShapeDtypeStruct((n, R, C), F32)
    return pl.pallas_call(body, name=name, out_shape=(out, out, out, out), grid=(n, R // tr),
                          in_specs=[row, row, row] + part_specs, out_specs=(row, row, row, row),
                          compiler_params=_params("arbitrary", "arbitrary"))(w, m, v, *parts)


def _all_gather(xs, axis, *, name):
    m = xs.shape[axis]
    out_shape = list(xs.shape)
    out_shape[axis] = N_DEV * m

    def body(x_ref, out_ref, send_sems, recv_sems, local_sem):
        x, y, c = _my_pos()
        me, sibling = (x, y, c), (x, y, 1 - c)
        chips = [(1 - x, y), (x, 1 - y), (1 - x, 1 - y)]

        def blk(px, py, pc):
            idx = [slice(None)] * 3
            idx[axis] = pl.ds((4 * px + 2 * py + pc) * m, m)
            return out_ref.at[tuple(idx)]

        def copy(k, block, to, src=None):
            return pltpu.make_async_remote_copy(src_ref=blk(*block) if src is None else src, dst_ref=blk(*block),
                                                send_sem=send_sems.at[k], recv_sem=recv_sems.at[k],
                                                device_id=to, device_id_type=MESH_ID)

        mine = pltpu.make_async_copy(x_ref, blk(*me), local_sem)
        mine.start()
        first = [copy(0, me, sibling, src=x_ref)]
        first += [copy(1 + j, me, (*chip, c), src=x_ref) for j, chip in enumerate(chips)]
        for cp in first:
            cp.start()
        passed = [copy(4 + j, (*chip, c), sibling) for j, chip in enumerate(chips)]
        for j, chip in enumerate(chips):
            copy(1 + j, (*chip, c), me).wait_recv()
            passed[j].start()
        copy(0, sibling, me).wait_recv()
        for j, chip in enumerate(chips):
            copy(4 + j, (*chip, 1 - c), me).wait_recv()
        for cp in first + passed:
            cp.wait_send()
        mine.wait()

    hbm = pl.BlockSpec(memory_space=pl.ANY)
    return pl.pallas_call(body, name=name, out_shape=jax.ShapeDtypeStruct(tuple(out_shape), xs.dtype),
                          in_specs=[hbm], out_specs=hbm,
                          scratch_shapes=[pltpu.SemaphoreType.DMA((N_DEV - 1,)), pltpu.SemaphoreType.DMA((N_DEV - 1,)),
                                          pltpu.SemaphoreType.DMA],
                          compiler_params=pltpu.CompilerParams(has_side_effects=True))(xs)


NEAR_PEERS = (1, 2, 4, 6)
RELAY_PEERS = (2, 4, 6)


def _block(ref, axis, idx, m):
    return ref.at[pl.ds(idx * m, m), :] if axis == 0 else ref.at[:, pl.ds(idx * m, m)]


def _exchange_copies(metas, src_refs, zone_refs, send_sems, recv_sems, base, phase):
    x, y, c = _my_pos()
    me = 4 * x + 2 * y + c

    def place(r):
        pos = (1 - x if r & 4 else x, 1 - y if r & 2 else y, 1 - c if r & 1 else c)
        return pos, 4 * pos[0] + 2 * pos[1] + pos[2]

    def copies(r, to, src, dst, arrival):
        return tuple(pltpu.make_async_remote_copy(src_ref=src, dst_ref=d, send_sem=send_sems.at[base + r - 1],
                                                  recv_sem=recv_sems.at[base + r - 1], device_id=to,
                                                  device_id_type=MESH_ID) for d in (dst, arrival))

    pairs, own = [], []
    if phase == 'relay':
        sibling, _ = place(1)
        for r in RELAY_PEERS:
            held, comes = place(r)[1], place(r | 1)[1]
            for (kind, axis, m), z_ref in zip(metas, zone_refs):
                pairs.append(copies(r, sibling, _block(z_ref, axis, held, m), _block(z_ref, axis, held, m),
                                    _block(z_ref, axis, comes, m)))
        return pairs, own
    for r in (NEAR_PEERS if phase == 'near' else range(1, N_DEV)):
        pos, peer = place(r)
        for (kind, axis, m), s_ref, z_ref in zip(metas, src_refs, zone_refs):
            if kind == 'gather':
                pairs.append(copies(r, pos, s_ref, _block(z_ref, axis, me, m), _block(z_ref, axis, peer, m)))
            else:
                pairs.append(copies(r, pos, _block(s_ref, axis, peer, m), z_ref.at[me], z_ref.at[peer]))
    for (kind, axis, m), s_ref, z_ref in zip(metas, src_refs, zone_refs):
        src, dst = (s_ref, _block(z_ref, axis, me, m)) if kind == 'gather' else (_block(s_ref, axis, me, m), z_ref.at[me])
        own.append(pltpu.make_async_copy(src, dst, recv_sems.at[base + N_DEV - 1]))
    return pairs, own


def _exchange_start(groups, after, *, name, relayed=False):
    flat = [it for g in groups for it in g]
    n, ng = len(flat), len(groups)
    metas = [it[2] for it in flat]
    bounds = [(sum(len(g) for g in groups[:q]), sum(len(g) for g in groups[:q + 1])) for q in range(ng)]
    phase = 'near' if relayed else 'all'

    def body(*refs):
        src_refs = refs[:n]
        send_sems, recv_sems = refs[n + 1], refs[n + 2]
        zone_refs = refs[2 * n + 3:3 * n + 3]
        token = refs[-1]
        for q, (lo, hi) in enumerate(bounds):
            pairs, own = _exchange_copies(metas[lo:hi], src_refs[lo:hi], zone_refs[lo:hi], send_sems, recv_sems,
                                          q * N_DEV, phase)
            for outgoing, _ in pairs:
                outgoing.start()
            for cp in own:
                cp.start()
        token[...] = jnp.zeros_like(token)

    hbm = pl.BlockSpec(memory_space=pltpu.HBM)
    sem = pl.BlockSpec(memory_space=pltpu.SEMAPHORE)
    srcs = [it[0] for it in flat]
    res = pl.pallas_call(
        body, name=name,
        out_shape=(pltpu.SemaphoreType.DMA((ng * N_DEV,)), pltpu.SemaphoreType.DMA((ng * N_DEV,)),
                   *[pltpu.HBM(a.shape, a.dtype) for a in srcs], *[pltpu.HBM(it[1], it[0].dtype) for it in flat],
                   jax.ShapeDtypeStruct((SUBLANE, LANE), F32)),
        in_specs=[hbm] * n + [pl.BlockSpec(memory_space=pl.ANY)],
        out_specs=(sem, sem, *[hbm] * (2 * n), pl.BlockSpec(memory_space=pltpu.VMEM)),
        input_output_aliases={q: 2 + q for q in range(n)},
        compiler_params=pltpu.CompilerParams(has_side_effects=pltpu.SideEffectType.DATAFLOW_SIDE_EFFECTING),
    )(*[pltpu.with_memory_space_constraint(a, pltpu.HBM) for a in srcs], after)
    handles = [(res[0], res[1], q * N_DEV, phase, list(res[2 + lo:2 + hi]), list(res[2 + n + lo:2 + n + hi]),
                metas[lo:hi]) for q, (lo, hi) in enumerate(bounds)]
    return handles, res[-1]


def _exchange_wait(handle, after, *, name):
    send_sems, recv_sems, base, phase, srcs, zones, metas = handle
    ns, nz = len(srcs), len(zones)

    def body(*refs):
        src_refs, zone_refs = refs[:ns], refs[ns:ns + nz]
        s_sems, r_sems = refs[ns + nz], refs[ns + nz + 1]
        pairs, own = _exchange_copies(metas, src_refs, zone_refs, s_sems, r_sems, base, phase)
        for outgoing, incoming in pairs:
            outgoing.wait_send()
            incoming.wait_recv()
        for cp in own:
            cp.wait()

    hbm = pl.BlockSpec(memory_space=pltpu.HBM)
    sem = pl.BlockSpec(memory_space=pltpu.SEMAPHORE)
    arrays = srcs + zones
    res = pl.pallas_call(
        body, name=name,
        out_shape=tuple(pltpu.HBM(a.shape, a.dtype) for a in arrays),
        in_specs=[hbm] * (ns + nz) + [sem, sem, pl.BlockSpec(memory_space=pl.ANY)],
        out_specs=tuple([hbm] * (ns + nz)),
        input_output_aliases={q: q for q in range(ns + nz)},
        compiler_params=pltpu.CompilerParams(has_side_effects=pltpu.SideEffectType.DATAFLOW_SIDE_EFFECTING),
    )(*arrays, send_sems, recv_sems, after)
    return list(res[ns:])


def _exchange_relay(handle, after, *, name):
    metas = handle[6]
    zones = _exchange_wait(handle, after, name=name + "_in")
    nz = len(zones)

    def body(*refs):
        zone_refs = refs[:nz]
        send_sems, recv_sems = refs[nz], refs[nz + 1]
        pairs, _ = _exchange_copies(metas, (), zone_refs, send_sems, recv_sems, 0, 'relay')
        for outgoing, _ in pairs:
            outgoing.start()
        refs[-1][...] = jnp.zeros_like(refs[-1])

    hbm = pl.BlockSpec(memory_space=pltpu.HBM)
    sem = pl.BlockSpec(memory_space=pltpu.SEMAPHORE)
    res = pl.pallas_call(
        body, name=name + "_out",
        out_shape=(pltpu.SemaphoreType.DMA((N_DEV,)), pltpu.SemaphoreType.DMA((N_DEV,)),
                   *[pltpu.HBM(z.shape, z.dtype) for z in zones], jax.ShapeDtypeStruct((SUBLANE, LANE), F32)),
        in_specs=[hbm] * nz, out_specs=(sem, sem, *[hbm] * nz, pl.BlockSpec(memory_space=pltpu.VMEM)),
        input_output_aliases={q: 2 + q for q in range(nz)},
        compiler_params=pltpu.CompilerParams(has_side_effects=pltpu.SideEffectType.DATAFLOW_SIDE_EFFECTING),
    )(*zones)
    return (res[0], res[1], 0, 'relay', [], list(res[2:2 + nz]), metas), res[-1][0:1, 0:1]


def _pad_rows(a, rows):
    return jnp.pad(a, ((0, rows - a.shape[0]), (0, 0)))


PACK_ROWS = 2 * SUBLANE


def _rows(a):
    flat = a.reshape(-1).astype(F32)
    pad = -flat.shape[0] % (PACK_ROWS * LANE)
    return (jnp.pad(flat, (0, pad)) if pad else flat).reshape(-1, LANE)


def _pack_rows(arrays):
    return jnp.concatenate([_rows(a) for a in arrays], 0)


def _unpack_rows(t, shapes):
    out, off = [], 0
    for shp in shapes:
        size = math.prod(shp)
        rows = -(-size // (PACK_ROWS * LANE)) * PACK_ROWS
        out.append(t[off:off + rows].reshape(-1)[:size].reshape(shp))
        off += rows
    return out


def _stat_row(st, r):
    return st[r:r + 1, :]


def kernel(x, c, ada_w, ada_b, norm1_g, norm2_g, ff_w1, ff_w2, final_g, conv_w_in, conv_w, conv_b, conv_w_out, ssm_w_in, ssm_a_re, ssm_a_im, ssm_log_dt, ssm_b_re, ssm_b_im, ssm_c_re, ssm_c_im, ssm_d, ssm_glu_w, ssm_glu_b, ssm_w_out, sg_w_in, sg_v_g, sg_w_s, sg_b_s, sg_w_out, loss_target, m_ada_w, m_ada_b, m_norm1_g, m_norm2_g, m_ff_w1, m_ff_w2, m_final_g, m_conv_w_in, m_conv_w, m_conv_b, m_conv_w_out, m_ssm_w_in, m_ssm_a_re, m_ssm_a_im, m_ssm_log_dt, m_ssm_b_re, m_ssm_b_im, m_ssm_c_re, m_ssm_c_im, m_ssm_d, m_ssm_glu_w, m_ssm_glu_b, m_ssm_w_out, m_sg_w_in, m_sg_v_g, m_sg_w_s, m_sg_b_s, m_sg_w_out, v_ada_w, v_ada_b, v_norm1_g, v_norm2_g, v_ff_w1, v_ff_w2, v_final_g, v_conv_w_in, v_conv_w, v_conv_b, v_conv_w_out, v_ssm_w_in, v_ssm_a_re, v_ssm_a_im, v_ssm_log_dt, v_ssm_b_re, v_ssm_b_im, v_ssm_c_re, v_ssm_c_im, v_ssm_d, v_ssm_glu_w, v_ssm_glu_b, v_ssm_w_out, v_sg_w_in, v_sg_v_g, v_sg_w_s, v_sg_b_s, v_sg_w_out):
    P = dict(zip(INPUTS, (x, c, ada_w, ada_b, norm1_g, norm2_g, ff_w1, ff_w2, final_g, conv_w_in, conv_w, conv_b, conv_w_out, ssm_w_in, ssm_a_re, ssm_a_im, ssm_log_dt, ssm_b_re, ssm_b_im, ssm_c_re, ssm_c_im, ssm_d, ssm_glu_w, ssm_glu_b, ssm_w_out, sg_w_in, sg_v_g, sg_w_s, sg_b_s, sg_w_out, loss_target, m_ada_w, m_ada_b, m_norm1_g, m_norm2_g, m_ff_w1, m_ff_w2, m_final_g, m_conv_w_in, m_conv_w, m_conv_b, m_conv_w_out, m_ssm_w_in, m_ssm_a_re, m_ssm_a_im, m_ssm_log_dt, m_ssm_b_re, m_ssm_b_im, m_ssm_c_re, m_ssm_c_im, m_ssm_d, m_ssm_glu_w, m_ssm_glu_b, m_ssm_w_out, m_sg_w_in, m_sg_v_g, m_sg_w_s, m_sg_b_s, m_sg_w_out, v_ada_w, v_ada_b, v_norm1_g, v_norm2_g, v_ff_w1, v_ff_w2, v_final_g, v_conv_w_in, v_conv_w, v_conv_b, v_conv_w_out, v_ssm_w_in, v_ssm_a_re, v_ssm_a_im, v_ssm_log_dt, v_ssm_b_re, v_ssm_b_im, v_ssm_c_re, v_ssm_c_im, v_ssm_d, v_ssm_glu_w, v_ssm_glu_b, v_ssm_w_out, v_sg_w_in, v_sg_v_g, v_sg_w_s, v_sg_b_s, v_sg_w_out)))
    L, D = x.shape[1], x.shape[2]
    me = _my_index()
    xs = x[0]
    tgt = loss_target[0]
    n_conv = conv_w_in.shape[0]

    def gather_item(shard, axis):
        full = tuple(N_DEV * s if a == axis else s for a, s in enumerate(shard.shape))
        return shard, full, ('gather', axis, shard.shape[axis])

    def mixer_shards(i):
        kind, j = i % 3, i // 3
        if kind == 0:
            return [(conv_w_in[j], 1), (conv_w_out[j], 0)]
        if kind == 1:
            return [(ssm_w_in[j], 0), (ssm_glu_w[j], 0), (ssm_w_out[j], 0)]
        return [(sg_w_in[j], 1), (sg_w_out[j], 0)]

    c_act = c * (1.0 / (1.0 + jnp.exp(-c)))
    vec_rows = jnp.concatenate([c_act.reshape(D // LANE, LANE), conv_w.reshape(-1, LANE), conv_b.reshape(-1, LANE),
                                sg_v_g.reshape(-1, LANE)], 0)
    n_vec = vec_rows.shape[0]
    vec_all = _all_gather(_pad_rows(vec_rows, 24)[None], 0, name="gather_vectors")
    c_all = vec_all[:, :D // LANE, :].reshape(N_DEV, D)
    sharded_full = vec_all[:, D // LANE:n_vec, :].transpose(1, 0, 2).reshape(n_vec - D // LANE, D)
    conv_w_full = sharded_full[:3 * n_conv].reshape(n_conv, 3, D)
    conv_b_full = sharded_full[3 * n_conv:4 * n_conv]
    sg_vg_full = sharded_full[4 * n_conv:4 * n_conv + 1]

    c_pad = _pad_rows(c_all, LANE)
    ncol = ada_w.shape[2]
    mod_part = jnp.stack([_mm(c_pad, ada_w[i], name=f"ada_fwd{i}")[:N_DEV] for i in range(DEPTH)])
    mod_all = _all_gather(mod_part.reshape(1, DEPTH * N_DEV, ncol), 0, name="gather_mod")
    mod_all = mod_all.reshape(N_DEV, DEPTH, N_DEV, ncol)
    mod_me = lax.dynamic_index_in_dim(mod_all, me, 2, keepdims=False)
    mod = mod_me.transpose(1, 0, 2).reshape(DEPTH, N_DEV * ncol) + ada_b
    gathers, gather_token = _exchange_start(
        [[gather_item(w.astype(BF16), ax) for w, ax in shards]
         for i in range(DEPTH) for shards in (mixer_shards(i), [(ff_w1[i], 1), (ff_w2[i], 0)])],
        mod, name="gather_start", relayed=True)
    mod = mod + gather_token[0:1, 0:1]
    relayed = [None] * len(gathers)
    relayed[0], sent = _exchange_relay(gathers[0], mod, name="gather_mix_relay0")

    s5_args = (ssm_a_re[0], ssm_a_im[0], ssm_log_dt[0], ssm_b_re[0], ssm_b_im[0], ssm_c_re[0], ssm_c_im[0])
    (abar_re, abar_im, bblk_re, bblk_im, cblk_re, cblk_im), s5_vjp = jax.vjp(_s5_prep, *s5_args)
    pw_fwd, pw_rev, pos_fwd = _s5_power_tables(abar_re, abar_im, S5_CHUNK // SUBLANE)
    s5_w = tuple(t.astype(BF16) for t in (bblk_re, bblk_im, cblk_re, cblk_im))
    causal = jnp.tril(jnp.ones((SG_CHUNK, SG_CHUNK), dtype=bool))
    ws_m = jnp.where(causal[None], sg_w_s[0], 0.0)
    ws_b = ws_m.astype(BF16)
    wst_b = ws_m.transpose(0, 2, 1).astype(BF16)
    bsb = jnp.broadcast_to(sg_b_s[0][:, :, None], (SG_HEADS, SG_CHUNK, LANE))

    saved = []
    xa = xs
    mods = [[mod[i:i + 1, q * D:(q + 1) * D] for q in range(6)] for i in range(DEPTH)]
    wn1s = [norm1_g[i:i + 1] * (1.0 + mods[i][1]) for i in range(DEPTH)]
    h1 = _normmod_fwd(xa, wn1s[0], mods[0][0] + sent, name="norm1_fwd0")
    for i in range(DEPTH):
        kind, j = i % 3, i // 3
        sh1, sc1, g1, sh2, sc2, g2 = mods[i]
        wn1 = wn1s[i]
        wn2 = norm2_g[i:i + 1] * (1.0 + sc2)
        S = dict(x_in=xa, g1=g1, g2=g2, sc1=sc1, sc2=sc2, wn1=wn1, wn2=wn2)
        w_mix = _exchange_wait(relayed[2 * i], h1, name=f"gather_mix_wait{i}")
        S['h1'] = h1
        if kind == 0:
            bcx = _mm(h1, w_mix[0], name=f"conv_in{i}", out_dtypes=(BF16,), bm=2048)
            wb = _pad_rows(jnp.concatenate([conv_w_full[j], conv_b_full[j:j + 1]], 0), SUBLANE)
            pb = _conv_fwd(bcx, wb, name=f"conv_mix{i}")
            S.update(bcx=bcx, wb=wb, pb=pb)
        elif kind == 1:
            u = _mm(h1, w_mix[0], name=f"ssm_in{i}")
            sre, sim, ypre, yg = _s5_fwd(u, *s5_w, pw_fwd, pos_fwd, ssm_d, name=f"s5_scan{i}")

            def glu_epi(acc, yv, bias):
                t = acc + bias
                return yv * (1.0 / (1.0 + jnp.exp(-t))), t

            pb, tt = _mm(yg, w_mix[1], name=f"ssm_glu{i}", out_dtypes=(BF16, F32), epi=glu_epi,
                         extras=[(yg, 'mn'), (ssm_glu_b, 'n')])
            S.update(u=u, sre=sre, sim=sim, ypre=ypre, yg=yg, pb=pb, tt=tt)
        else:
            uv = _mm(h1, w_mix[0], name=f"sg_in{i}", bm=2048)
            pb = _sg_fwd(uv, sg_vg_full, ws_b, bsb, name=f"sg_mix{i}")
            S.update(uv=uv, pb=pb)
        relayed[2 * i + 1], sent = _exchange_relay(gathers[2 * i + 1], pb, name=f"gather_ff_relay{i}")
        x_mid, y_mix, h2 = _mm(pb, w_mix[-1], name=f"mix_out{i}", out_dtypes=(F32, BF16, BF16), epi=_epi_residual_norm,
                               extras=[(xa, 'mn'), (g1 + sent, 'n'), (wn2, 'n'), (sh2, 'n')])
        w1_full, w2_full = _exchange_wait(relayed[2 * i + 1], h2, name=f"gather_ff_wait{i}")
        S.update(w_mix=w_mix, w1=w1_full, w2=w2_full)
        ra = _mm(h2, w1_full, name=f"ff_up{i}", out_dtypes=(BF16,), epi=lambda acc: (jnp.maximum(acc, 0.0),), bm=2048)
        if i + 1 < DEPTH:
            relayed[2 * i + 2], sent = _exchange_relay(gathers[2 * i + 2], ra, name=f"gather_mix_relay{i + 1}")
            xa, f_out, h1 = _mm(ra, w2_full, name=f"ff_down{i}", out_dtypes=(F32, BF16, BF16), a_fn=_square,
                                epi=_epi_residual_norm, bm=256, bk=w2_full.shape[0],
                                extras=[(x_mid, 'mn'), (g2 + sent, 'n'), (wn1s[i + 1], 'n'), (mods[i + 1][0], 'n')])
        else:
            xa, f_out = _mm(ra, w2_full, name=f"ff_down{i}", out_dtypes=(F32, BF16), epi=_epi_residual, a_fn=_square,
                            extras=[(x_mid, 'mn'), (g2, 'n')], bm=256, bk=w2_full.shape[0])
        S.update(x_mid=x_mid, y_mix=y_mix, h2=h2, ra=ra, f_out=f_out)
        saved.append(S)

    S = saved[-1]
    dx, st, dfb, loss_tile = _loss_head(xa, tgt, final_g[None], S['f_out'], S['g2'], name="loss_head")
    d_final_g = _stat_row(st, 0)
    dg2_next = _stat_row(st, 2)

    def scatter_item(g, axis):
        m = g.shape[axis] // N_DEV
        shard = tuple(m if a == axis else s for a, s in enumerate(g.shape))
        return g, (N_DEV,) + shard, ('scatter', axis, m)

    dmod = [None] * DEPTH
    dn1g, dn2g = [None] * DEPTH, [None] * DEPTH
    d_conv_w, d_conv_b = [None] * n_conv, [None] * n_conv
    ff_sent, mix_sent = [None] * DEPTH, [None] * DEPTH
    small = {}
    for i in reversed(range(DEPTH)):
        kind, j = i % 3, i // 3
        S = saved[i]
        w_mix = S['w_mix']
        dg2 = dg2_next
        da = _mm(dfb, S['w2'], tb=True, name=f"ff_down_bwd{i}", out_dtypes=(BF16,), bm=2048,
                 epi=lambda acc, rav: (acc * (2.0 * rav.astype(F32)),), extras=[(S['ra'], 'mn')])
        dw2 = _wgrad(S['ra'], dfb, name=f"ff_w2_grad{i}", a_fn=_square, bm=256, bn=1024)
        dw1 = _wgrad(S['h2'], da, name=f"ff_w1_grad{i}")
        (ff_sent[i],), token = _exchange_start([[scatter_item(dw1, 1), scatter_item(dw2, 0)]], dx,
                                               name=f"ff_grads_start{i}")
        dx_mid, dyb, st2 = _mm(da, S['w1'], tb=True, name=f"ff_up_bwd{i}", out_dtypes=(F32, BF16), bm=256,
                               bk=da.shape[1], epi=_epi_norm_bwd(True), n_stats=3,
                               extras=[(S['x_mid'], 'mn'), (S['wn2'] + token[0:1, 0:1], 'n'), (dx, 'mn'),
                                       (S['y_mix'], 'mn'), (S['g1'], 'n')])
        dsc2 = _stat_row(st2, 0) * norm2_g[i:i + 1]
        dn2g[i] = _stat_row(st2, 0) * (1.0 + S['sc2'])
        dsh2 = _stat_row(st2, 1)
        dg1 = _stat_row(st2, 2)
        if kind == 0:
            dp = _mm(dyb, w_mix[1], tb=True, name=f"conv_out_bwd{i}", out_dtypes=(BF16,))
            d_cwo = _wgrad(S['pb'], dyb, name=f"conv_w_out_grad{i}")
            dbcx, stc = _conv_bwd(dp, S['bcx'], S['wb'], name=f"conv_mix_bwd{i}")
            d_conv_w[j] = stc[0:3]
            d_conv_b[j] = stc[3:4]
            dh_operand, dh_name = dbcx, "conv_in_bwd"
            d_cwi = _wgrad(S['h1'], dbcx, name=f"conv_w_in_grad{i}")
            mix_grads = [scatter_item(d_cwi, 1), scatter_item(d_cwo, 0)]
        elif kind == 1:
            dy2 = _mm(dyb, w_mix[2], tb=True, name=f"ssm_out_bwd{i}")
            d_ssm_out = _wgrad(S['pb'], dyb, name=f"ssm_w_out_grad{i}")
            dtb, dya, stg = _glu_bwd(dy2, S['yg'], S['tt'], name=f"ssm_glu_bwd{i}")
            dypre = _mm(dtb, w_mix[1], tb=True, name=f"ssm_glu_in_bwd{i}",
                        epi=lambda acc, a, yp: ((a + acc) * _gelu_grad(yp),),
                        extras=[(dya, 'mn'), (S['ypre'], 'mn')])
            d_glu = _wgrad(S['yg'], dtb, name=f"ssm_glu_w_grad{i}", bm=512)
            dub, dbre, dbim, dcre, dcim, ga, dd = _s5_bwd(dypre, S['u'], S['sre'], S['sim'], *s5_w, pw_rev, pos_fwd, ssm_d,
                                                           name=f"s5_scan_bwd{i}")
            dh_operand, dh_name = dub, "ssm_in_bwd"
            d_ssm_in = _wgrad(S['h1'], dub, name=f"ssm_w_in_grad{i}")
            da_re, da_im, dlog_dt, db_re, db_im, dc_re, dc_im = s5_vjp((ga[0:1], ga[1:2], dbre, dbim, dcre, dcim))
            s5_small = _pack_rows([da_re, da_im, dlog_dt, db_re, db_im, dc_re, dc_im, dd[0], stg[0]])
            mix_grads = [scatter_item(d_ssm_in, 0), scatter_item(d_glu, 0), scatter_item(d_ssm_out, 0),
                         gather_item(s5_small.astype(BF16), 0)]
        else:
            dp = _mm(dyb, w_mix[1], tb=True, name=f"sg_out_bwd{i}")
            d_sgo = _wgrad(S['pb'], dyb, name=f"sg_w_out_grad{i}")
            duv, dws, dbs, stv = _sg_bwd(dp, S['uv'], sg_vg_full, ws_b, wst_b, bsb, name=f"sg_mix_bwd{i}")
            dh_operand, dh_name = duv, "sg_in_bwd"
            d_sgi = _wgrad(S['h1'], duv, name=f"sg_w_in_grad{i}")
            sg_small = _pack_rows([jnp.where(causal[None], dws, 0.0), jnp.sum(dbs, axis=-1)])
            d_sg_vg = stv[0:1]
            mix_grads = [scatter_item(d_sgi, 1), scatter_item(d_sgo, 0), gather_item(sg_small.astype(BF16), 0)]
        wn1 = S['wn1']
        gate = []
        if i > 0:
            (mix_sent[i],), token = _exchange_start([mix_grads], dx_mid, name=f"mix_grads_start{i}")
            wn1 = wn1 + token[0:1, 0:1]
            gate = [(saved[i - 1]['f_out'], 'mn'), (saved[i - 1]['g2'], 'n')]
        res = _mm(dh_operand, w_mix[0], tb=True, name=f"{dh_name}{i}", out_dtypes=(F32, BF16) if i > 0 else (F32,),
                  bm=256, bk=w_mix[0].shape[1], epi=_epi_norm_bwd(i > 0), n_stats=3 if i > 0 else 2,
                  extras=[(S['x_in'], 'mn'), (wn1, 'n'), (dx_mid, 'mn')] + gate)
        if i > 0:
            dx, dfb, st1 = res
            dg2_next = _stat_row(st1, 2)
        else:
            dx, st1 = res
        dsc1 = _stat_row(st1, 0) * norm1_g[i:i + 1]
        dn1g[i] = _stat_row(st1, 0) * (1.0 + S['sc1'])
        dsh1 = _stat_row(st1, 1)
        dmod[i] = jnp.concatenate([dsh1, dsc1, dg1, dsh2, dsc2, dg2], 1)
    grad_x = dx[None]

    out = {}

    def small_group(names, parts, label):
        shapes = [P[n].shape for n in names]
        w, m, v = (_pack_rows([P[pre + n] for n in names])[None] for pre in ('', 'm_', 'v_'))
        res = [_unpack_rows(t[0], shapes) for t in _adamw(w, [parts], m, v, name=label)]
        for q, n in enumerate(names):
            out[n] = tuple(r[q] for r in res)

    small.update(ada_b=jnp.concatenate(dmod, 0), norm1_g=jnp.concatenate(dn1g, 0), norm2_g=jnp.concatenate(dn2g, 0),
                 final_g=d_final_g, conv_w=jnp.stack(d_conv_w), conv_b=jnp.concatenate(d_conv_b, 0), sg_v_g=d_sg_vg)
    last_pack = _pack_rows([small[n] for n in LAST_SMALL + SMALL_SHARD])
    n_last = _pack_rows([P[n] for n in LAST_SMALL]).shape[0]
    n_pack = last_pack.shape[0]
    pack_all = _all_gather(jnp.concatenate([last_pack, loss_tile], 0)[None], 0, name="gather_small_grads")
    loss = jnp.sum(pack_all[:, n_pack, 0])
    (mix_sent[0],), last_token = _exchange_start([mix_grads], pack_all, name="mix_grads_start0")
    small_group(LAST_SMALL, pack_all[:, :n_last], "adamw_small")
    sh_rows = (n_pack - n_last) // N_DEV
    sh_parts = pack_all[:, n_last:n_pack].reshape(N_DEV, sh_rows, N_DEV, LANE)
    sh_parts = lax.dynamic_index_in_dim(sh_parts, me, 2, keepdims=False)
    sh_parts = jnp.pad(sh_parts, ((0, 0), (0, 16 - sh_rows), (0, 0)))

    def pack_shard(prefix):
        return _pad_rows(jnp.concatenate([P[prefix + n].reshape(-1, LANE) for n in SMALL_SHARD], 0), 16)[None]

    sg_, sd_, sm_, sv_ = _adamw(pack_shard(''), [sh_parts], pack_shard('m_'), pack_shard('v_'), name="adamw_channel")
    off = 0
    for n in SMALL_SHARD:
        rows = math.prod(P[n].shape) // LANE
        out[n] = tuple(t[0, off:off + rows].reshape(P[n].shape) for t in (sg_, sd_, sm_, sv_))
        off += rows

    dmod_all = pack_all[:, :DEPTH * 6 * D // LANE].reshape(N_DEV, DEPTH, 6 * D)
    dmod_cols = lax.dynamic_slice_in_dim(dmod_all, me * ncol, ncol, 2)
    g_ada = [_mm(c_pad, _pad_rows(dmod_cols[:, i], LANE), ta=True, name=f"ada_w_grad{i}")[None] for i in range(DEPTH)]

    def big(name, parts):
        res = _adamw(P[name], parts, P['m_' + name], P['v_' + name], name="adamw_" + name)
        out[name] = res
        return res[1]

    ff_parts = [_exchange_wait(ff_sent[i], last_token, name=f"ff_grads_wait{i}") for i in range(DEPTH)]
    mix_parts = [None] + [_exchange_wait(mix_sent[i], last_token, name=f"mix_grads_wait{i}") for i in range(1, DEPTH)]
    big('ada_w', g_ada)
    big('ff_w1', [p[0] for p in ff_parts])
    big('ff_w2', [p[1] for p in ff_parts])
    done = big('sg_w_in', [mix_parts[2][0]])
    mix_parts[0] = _exchange_wait(mix_sent[0], done, name="mix_grads_wait0")
    big('conv_w_in', [mix_parts[i][0] for i in range(DEPTH) if i % 3 == 0])
    row_names = ['conv_w_out', 'ssm_w_in', 'ssm_glu_w', 'ssm_w_out', 'sg_w_out']
    row_parts = ([mix_parts[i][1] for i in range(DEPTH) if i % 3 == 0] + mix_parts[1][:3] + [mix_parts[2][1]])
    small_group(S5_SMALL, mix_parts[1][3].reshape(N_DEV, -1, LANE), "adamw_s5")
    small_group(SG_SMALL, mix_parts[2][2].reshape(N_DEV, -1, LANE), "adamw_sg")
    row_w, row_m, row_v = (jnp.concatenate([P[pre + n] for n in row_names], 0) for pre in ('', 'm_', 'v_'))
    rw = _adamw(row_w, row_parts, row_m, row_v, name="adamw_row_sharded")
    off = 0
    for n in row_names:
        cnt = P[n].shape[0]
        out[n] = tuple(t[off:off + cnt] for t in rw)
        off += cnt

    return (loss, grad_x, *[out[n][0] for n in WEIGHTS], *[out[n][1] for n in WEIGHTS],
            *[out[n][2] for n in WEIGHTS], *[out[n][3] for n in WEIGHTS])
```

```python
import math

import jax
import jax.numpy as jnp
from jax import lax
from jax.experimental import pallas as pl
from jax.experimental.pallas import tpu as pltpu

F32 = jnp.float32
BF16 = jnp.bfloat16

N_DEV = 8
MESH_ID = pl.DeviceIdType.MESH
DEPTH = 4
EPS = 1e-6
S5_GROUPS, S5_GROUP, S5_STATE = 64, 16, 64
S5_LANES = S5_GROUPS * S5_STATE
S5_BLOCKS = 8
S5_CHUNK = 512
SG_HEADS, SG_CHUNK = 8, 128
LANE = 128
SUBLANE = 8
VMEM_LIMIT = 48 * 1024 * 1024
ADAM_LR, ADAM_B1, ADAM_B2, ADAM_EPS, ADAM_WD, ADAM_STEP = 0.001, 0.9, 0.999, 1e-08, 0.01, 10
GELU_C = math.sqrt(2.0 / math.pi)
GELU_A = 0.044715

WEIGHTS = ['ada_w', 'ada_b', 'norm1_g', 'norm2_g', 'ff_w1', 'ff_w2', 'final_g', 'conv_w_in', 'conv_w', 'conv_b',
           'conv_w_out', 'ssm_w_in', 'ssm_a_re', 'ssm_a_im', 'ssm_log_dt', 'ssm_b_re', 'ssm_b_im', 'ssm_c_re',
           'ssm_c_im', 'ssm_d', 'ssm_glu_w', 'ssm_glu_b', 'ssm_w_out', 'sg_w_in', 'sg_v_g', 'sg_w_s', 'sg_b_s',
           'sg_w_out']
INPUTS = ['x', 'c'] + WEIGHTS + ['loss_target'] + ['m_' + n for n in WEIGHTS] + ['v_' + n for n in WEIGHTS]
S5_SMALL = ['ssm_a_re', 'ssm_a_im', 'ssm_log_dt', 'ssm_b_re', 'ssm_b_im', 'ssm_c_re', 'ssm_c_im', 'ssm_d', 'ssm_glu_b']
SG_SMALL = ['sg_w_s', 'sg_b_s']
LAST_SMALL = ['ada_b', 'norm1_g', 'norm2_g', 'final_g']
SMALL_SHARD = ['conv_w', 'conv_b', 'sg_v_g']


def _params(*sem):
    return pltpu.CompilerParams(dimension_semantics=sem or None, vmem_limit_bytes=VMEM_LIMIT)


def _my_pos():
    return lax.axis_index("x"), lax.axis_index("y"), lax.axis_index("c")


def _my_index():
    x, y, c = _my_pos()
    return 4 * x + 2 * y + c


def _mm(a, b, *, name, ta=False, tb=False, out_dtypes=(F32,), epi=None, extras=(), a_fn=None, n_stats=0, bm=1024,
        bn=1024, bk=1024):
    a_chunks = a.shape[0] if a.ndim == 3 else 0
    b_chunks = b.shape[0] if b.ndim == 3 else 0
    assert not (a_chunks and ta) and not (b_chunks and tb)
    if a_chunks:
        m, k = a.shape[1], a_chunks * a.shape[2]
        bk = k
    else:
        m, k = (a.shape[1], a.shape[0]) if ta else a.shape
    if b_chunks:
        k2, n = b.shape[1], b_chunks * b.shape[2]
        bn = min(bn, b.shape[2])
    else:
        k2, n = (b.shape[1], b.shape[0]) if tb else b.shape
    assert k == k2, (a.shape, b.shape, ta, tb)
    bm, bn, bk = min(bm, m), min(bn, n), min(bk, k)
    assert m % bm == 0 and n % bn == 0 and k % bk == 0, (m, n, k, bm, bn, bk)
    nk = k // bk
    n_ex, n_out = len(extras), len(out_dtypes)
    dims = (((0 if ta else 1,), (1 if tb else 0,)), ((), ()))

    def body(*refs):
        a_ref, b_ref = refs[0], refs[1]
        ex_refs = refs[2:2 + n_ex]
        out_refs = refs[2 + n_ex:2 + n_ex + n_out]

        def finish(acc):
            outs = epi(acc, *[r[...] for r in ex_refs]) if epi is not None else (acc,)
            for r, o in zip(out_refs, outs[:n_out]):
                r[...] = o.astype(r.dtype)
            if n_stats:
                st_ref = refs[2 + n_ex + n_out]

                @pl.when(pl.program_id(0) == 0)
                def _():
                    st_ref[...] = jnp.zeros_like(st_ref)

                for q, row in enumerate(outs[n_out:]):
                    st_ref[q:q + 1, :] += row

        av = jnp.concatenate([a_ref[t] for t in range(a_chunks)], axis=1) if a_chunks else a_ref[...]
        if a_fn is not None:
            av = a_fn(av)
        part = lax.dot_general(av.astype(BF16), b_ref[...].astype(BF16), dims, preferred_element_type=F32)
        if nk == 1:
            finish(part)
            return
        acc_ref = refs[-1]
        kk = pl.program_id(2)

        @pl.when(kk == 0)
        def _():
            acc_ref[...] = part

        @pl.when(kk > 0)
        def _():
            acc_ref[...] += part

        @pl.when(kk == nk - 1)
        def _():
            finish(acc_ref[...])

    if a_chunks:
        a_spec = pl.BlockSpec((a_chunks, bm, a.shape[2]), lambda i, j, q: (0, i, 0))
    elif ta:
        a_spec = pl.BlockSpec((bk, bm), lambda i, j, q: (q, i))
    else:
        a_spec = pl.BlockSpec((bm, bk), lambda i, j, q: (i, q))
    if b_chunks:
        per = b.shape[2] // bn
        b_spec = pl.BlockSpec((None, bk, bn), lambda i, j, q: (j // per, q, j % per))
    elif tb:
        b_spec = pl.BlockSpec((bn, bk), lambda i, j, q: (j, q))
    else:
        b_spec = pl.BlockSpec((bk, bn), lambda i, j, q: (q, j))
    ex_specs = []
    for arr, kind in extras:
        if kind == 'mn':
            assert arr.shape == (m, n), (arr.shape, m, n)
            ex_specs.append(pl.BlockSpec((bm, bn), lambda i, j, q: (i, j)))
        else:
            assert arr.shape == (1, n), (arr.shape, n)
            ex_specs.append(pl.BlockSpec((1, bn), lambda i, j, q: (0, j)))
    out_shape = [jax.ShapeDtypeStruct((m, n), d) for d in out_dtypes]
    out_specs = [pl.BlockSpec((bm, bn), lambda i, j, q: (i, j)) for _ in out_dtypes]
    if n_stats:
        assert n_stats <= SUBLANE
        out_shape.append(jax.ShapeDtypeStruct((SUBLANE, n), F32))
        out_specs.append(pl.BlockSpec((SUBLANE, bn), lambda i, j, q: (0, j)))
    outs = pl.pallas_call(
        body, name=name, out_shape=tuple(out_shape), grid=(m // bm, n // bn, nk),
        in_specs=[a_spec, b_spec] + ex_specs, out_specs=tuple(out_specs),
        scratch_shapes=[pltpu.VMEM((bm, bn), F32)] if nk > 1 else [],
        compiler_params=_params(*(["arbitrary"] * 3 if n_stats else ["parallel", "parallel", "arbitrary"])),
    )(a, b, *[arr for arr, _ in extras])
    return outs if len(outs) > 1 else outs[0]


def _epi_residual(acc, res, gate):
    return res + gate * acc, acc


def _epi_residual_norm(acc, res, gate, w, sh):
    xn = res + gate * acc
    return xn, acc, xn * _rstd(xn) * w + sh


def _epi_norm_bwd(gated):
    def epi(dh, xv, w, dres, *gate):
        rstd = _rstd(xv)
        xn = xv * rstd
        dxn = dh * w
        dx = rstd * (dxn - xn * jnp.mean(dxn * xn, axis=-1, keepdims=True)) + dres
        stats = [jnp.sum(dh * xn, axis=0, keepdims=True), jnp.sum(dh, axis=0, keepdims=True)]
        if not gated:
            return (dx, *stats)
        yv, g = gate
        return (dx, dx * g, *stats, jnp.sum(dx * yv.astype(F32), axis=0, keepdims=True))
    return epi


def _epi_loss_head(f, x_mid, g, tgt, fg):
    xv = x_mid + g * f
    rstd = _rstd(xv)
    xn = xv * rstd
    err = xn * fg - tgt
    loss = 0.5 * jnp.sum(jnp.mean(err * err, axis=-1, keepdims=True))
    dout = err * (1.0 / xv.shape[-1])
    dxn = dout * fg
    dx = rstd * (dxn - xn * jnp.mean(dxn * xn, axis=-1, keepdims=True))
    return (dx, dx * g, jnp.full((1, xv.shape[-1]), loss, F32), jnp.sum(dout * xn, axis=0, keepdims=True),
            jnp.sum(dx * f, axis=0, keepdims=True))


def _epi_glu_bwd(dy2, yv, t):
    sig = 1.0 / (1.0 + jnp.exp(-t))
    dt = dy2 * yv * sig * (1.0 - sig)
    return dt, dy2 * sig, jnp.sum(dt, axis=0, keepdims=True)


def _wgrad(acts, cots, *, name, a_fn=None, bm=1024, bn=512):
    return _mm(acts, cots, ta=True, name=name, out_dtypes=(BF16,), a_fn=a_fn, bm=bm, bn=bn, bk=acts.shape[0])


def _square(a):
    af = a.astype(F32)
    return af * af


def _rstd(xv):
    return lax.rsqrt(jnp.mean(xv * xv, axis=-1, keepdims=True) + EPS)


def _normmod_fwd(x, w, sh, *, name, tm=512):
    L, D = x.shape

    def body(x_ref, w_ref, s_ref, h_ref):
        xv = x_ref[...]
        h_ref[...] = (xv * _rstd(xv) * w_ref[...] + s_ref[...]).astype(h_ref.dtype)

    row = pl.BlockSpec((tm, D), lambda i: (i, 0))
    vec = pl.BlockSpec((1, D), lambda i: (0, 0))
    return pl.pallas_call(body, name=name, out_shape=jax.ShapeDtypeStruct((L, D), BF16), grid=(L // tm,),
                          in_specs=[row, vec, vec], out_specs=row, compiler_params=_params("parallel"))(x, w, sh)


def _normmod_bwd(dh, x, w, dres, gate, *, name, tm=256):
    L, D = x.shape
    has_gate = gate is not None

    def body(*refs):
        if has_gate:
            dh_ref, x_ref, w_ref, r_ref, y_ref, g_ref, dx_ref, st_ref, dy_ref = refs
        else:
            dh_ref, x_ref, w_ref, r_ref, dx_ref, st_ref = refs
        i = pl.program_id(0)

        @pl.when(i == 0)
        def _():
            st_ref[...] = jnp.zeros_like(st_ref)

        xv = x_ref[...]
        dhv = dh_ref[...].astype(F32)
        rstd = _rstd(xv)
        xn = xv * rstd
        dxn = dhv * w_ref[...]
        dx = rstd * (dxn - xn * jnp.mean(dxn * xn, axis=-1, keepdims=True)) + r_ref[...]
        dx_ref[...] = dx
        st_ref[0:1, :] += jnp.sum(dhv * xn, axis=0, keepdims=True)
        st_ref[1:2, :] += jnp.sum(dhv, axis=0, keepdims=True)
        if has_gate:
            dy_ref[...] = (dx * g_ref[...]).astype(dy_ref.dtype)
            st_ref[2:3, :] += jnp.sum(dx * y_ref[...].astype(F32), axis=0, keepdims=True)

    row = pl.BlockSpec((tm, D), lambda i: (i, 0))
    vec = pl.BlockSpec((1, D), lambda i: (0, 0))
    st = pl.BlockSpec((SUBLANE, D), lambda i: (0, 0))
    in_specs = [row, row, vec, row] + ([row, vec] if has_gate else [])
    out_shape = [jax.ShapeDtypeStruct((L, D), F32), jax.ShapeDtypeStruct((SUBLANE, D), F32)]
    out_specs = [row, st]
    if has_gate:
        out_shape.append(jax.ShapeDtypeStruct((L, D), BF16))
        out_specs.append(row)
    args = (dh, x, w, dres) + (tuple(gate) if has_gate else ())
    return pl.pallas_call(body, name=name, out_shape=tuple(out_shape), grid=(L // tm,), in_specs=in_specs,
                          out_specs=tuple(out_specs), compiler_params=_params("arbitrary"))(*args)


def _loss_head(x, tgt, fg, y, g, *, name, tm=256):
    L, D = x.shape

    def body(x_ref, t_ref, fg_ref, y_ref, g_ref, dx_ref, st_ref, dy_ref, loss_ref):
        i = pl.program_id(0)

        @pl.when(i == 0)
        def _():
            st_ref[...] = jnp.zeros_like(st_ref)
            loss_ref[...] = jnp.zeros_like(loss_ref)

        xv = x_ref[...]
        rstd = _rstd(xv)
        xn = xv * rstd
        err = xn * fg_ref[...] - t_ref[...]
        loss_ref[...] += 0.5 * jnp.sum(jnp.mean(err * err, axis=-1, keepdims=True))
        dout = err * (1.0 / D)
        dxn = dout * fg_ref[...]
        dx = rstd * (dxn - xn * jnp.mean(dxn * xn, axis=-1, keepdims=True))
        dx_ref[...] = dx
        dy_ref[...] = (dx * g_ref[...]).astype(dy_ref.dtype)
        st_ref[0:1, :] += jnp.sum(dout * xn, axis=0, keepdims=True)
        st_ref[2:3, :] += jnp.sum(dx * y_ref[...].astype(F32), axis=0, keepdims=True)

    row = pl.BlockSpec((tm, D), lambda i: (i, 0))
    vec = pl.BlockSpec((1, D), lambda i: (0, 0))
    return pl.pallas_call(
        body, name=name,
        out_shape=(jax.ShapeDtypeStruct((L, D), F32), jax.ShapeDtypeStruct((SUBLANE, D), F32),
                   jax.ShapeDtypeStruct((L, D), BF16), jax.ShapeDtypeStruct((SUBLANE, LANE), F32)),
        grid=(L // tm,), in_specs=[row, row, vec, row, vec],
        out_specs=(row, pl.BlockSpec((SUBLANE, D), lambda i: (0, 0)), row,
                   pl.BlockSpec((SUBLANE, LANE), lambda i: (0, 0))),
        compiler_params=_params("arbitrary"))(x, tgt, fg, y, g)


def _shift_down(v, k):
    row = lax.broadcasted_iota(jnp.int32, v.shape, 0)
    return jnp.where(row >= k, pltpu.roll(v, k, 0), 0.0)


def _shift_up(v, k):
    n = v.shape[0]
    row = lax.broadcasted_iota(jnp.int32, v.shape, 0)
    return jnp.where(row < n - k, pltpu.roll(v, n - k, 0), 0.0)


def _conv_views(L, D):
    return [pl.BlockSpec((L, LANE), lambda j, s=s: (0, s * (D // LANE) + j)) for s in range(3)]


def _conv_fwd(bcx, wb, *, name):
    L, D = bcx.shape[0], bcx.shape[1] // 3

    def body(b_ref, c_ref, x_ref, wb_ref, p_ref):
        z = c_ref[...].astype(F32) * x_ref[...].astype(F32)
        conv = (wb_ref[0:1, :] * _shift_down(z, 2) + wb_ref[1:2, :] * _shift_down(z, 1)
                + wb_ref[2:3, :] * z + wb_ref[3:4, :])
        p_ref[...] = (b_ref[...].astype(F32) * conv).astype(p_ref.dtype)

    col = pl.BlockSpec((L, LANE), lambda j: (0, j))
    return pl.pallas_call(body, name=name, out_shape=jax.ShapeDtypeStruct((L, D), BF16), grid=(D // LANE,),
                          in_specs=_conv_views(L, D) + [pl.BlockSpec((SUBLANE, LANE), lambda j: (0, j))],
                          out_specs=col, compiler_params=_params("parallel"))(bcx, bcx, bcx, wb)


def _conv_bwd(dp, bcx, wb, *, name):
    L, D = dp.shape

    def body(dp_ref, b_ref, c_ref, x_ref, wb_ref, d3_ref, st_ref):
        cv, xv = c_ref[...].astype(F32), x_ref[...].astype(F32)
        z = cv * xv
        z1, z2 = _shift_down(z, 1), _shift_down(z, 2)
        w0, w1, w2 = wb_ref[0:1, :], wb_ref[1:2, :], wb_ref[2:3, :]
        conv = w0 * z2 + w1 * z1 + w2 * z + wb_ref[3:4, :]
        dpv = dp_ref[...].astype(F32)
        d3_ref[0] = (dpv * conv).astype(d3_ref.dtype)
        dconv = dpv * b_ref[...].astype(F32)
        dz = w2 * dconv + w1 * _shift_up(dconv, 1) + w0 * _shift_up(dconv, 2)
        d3_ref[1] = (dz * xv).astype(d3_ref.dtype)
        d3_ref[2] = (dz * cv).astype(d3_ref.dtype)
        st_ref[...] = jnp.zeros_like(st_ref)
        st_ref[0:1, :] = jnp.sum(dconv * z2, axis=0, keepdims=True)
        st_ref[1:2, :] = jnp.sum(dconv * z1, axis=0, keepdims=True)
        st_ref[2:3, :] = jnp.sum(dconv * z, axis=0, keepdims=True)
        st_ref[3:4, :] = jnp.sum(dconv, axis=0, keepdims=True)

    col = pl.BlockSpec((L, LANE), lambda j: (0, j))
    vec = pl.BlockSpec((SUBLANE, LANE), lambda j: (0, j))
    return pl.pallas_call(body, name=name,
                          out_shape=(jax.ShapeDtypeStruct((3, L, D), BF16), jax.ShapeDtypeStruct((SUBLANE, D), F32)),
                          grid=(D // LANE,), in_specs=[col] + _conv_views(L, D) + [vec],
                          out_specs=(pl.BlockSpec((3, L, LANE), lambda j: (0, 0, j)), vec),
                          compiler_params=_params("parallel"))(dp, bcx, bcx, bcx, wb)


def _sg_fwd(uv, vg, ws, bsb, *, name, tr=512):
    L, D = uv.shape[0], uv.shape[1] // 2

    def body(uv_ref, vg_ref, ws_ref, bsb_ref, p_ref):
        for ci in range(tr // SG_CHUNK):
            rows = slice(ci * SG_CHUNK, (ci + 1) * SG_CHUNK)
            v = uv_ref[rows, D:2 * D]
            vn = (v * _rstd(v) * vg_ref[...]).astype(BF16)
            for h in range(SG_HEADS):
                cols = slice(h * LANE, (h + 1) * LANE)
                vm = jnp.dot(ws_ref[h], vn[:, cols], preferred_element_type=F32) + bsb_ref[h]
                p_ref[rows, cols] = (uv_ref[rows, cols] * vm).astype(p_ref.dtype)

    full3 = pl.BlockSpec((SG_HEADS, SG_CHUNK, LANE), lambda i: (0, 0, 0))
    return pl.pallas_call(body, name=name, out_shape=jax.ShapeDtypeStruct((L, D), BF16), grid=(L // tr,),
                          in_specs=[pl.BlockSpec((tr, 2 * D), lambda i: (i, 0)), pl.BlockSpec((1, D), lambda i: (0, 0)),
                                    full3, full3],
                          out_specs=pl.BlockSpec((tr, D), lambda i: (i, 0)),
                          compiler_params=_params("parallel"))(uv, vg, ws, bsb)


def _sg_bwd(dp, uv, vg, ws, wst, bsb, *, name, tr=512):
    L, D = dp.shape

    def body(dp_ref, uv_ref, vg_ref, ws_ref, wst_ref, bsb_ref, duv_ref, dws_ref, dbs_ref, st_ref, dvn_ref):
        i = pl.program_id(0)

        @pl.when(i == 0)
        def _():
            dws_ref[...] = jnp.zeros_like(dws_ref)
            dbs_ref[...] = jnp.zeros_like(dbs_ref)
            st_ref[...] = jnp.zeros_like(st_ref)

        for ci in range(tr // SG_CHUNK):
            rows = slice(ci * SG_CHUNK, (ci + 1) * SG_CHUNK)
            v = uv_ref[rows, D:2 * D]
            rstd = _rstd(v)
            vhat = v * rstd
            vn = (vhat * vg_ref[...]).astype(BF16)
            for h in range(SG_HEADS):
                cols = slice(h * LANE, (h + 1) * LANE)
                vm = jnp.dot(ws_ref[h], vn[:, cols], preferred_element_type=F32) + bsb_ref[h]
                dph = dp_ref[rows, cols]
                duv_ref[rows, cols] = (dph * vm).astype(duv_ref.dtype)
                dvm = dph * uv_ref[rows, cols]
                dbs_ref[h] += dvm
                dvmb = dvm.astype(BF16)
                dws_ref[h] += lax.dot_general(dvmb, vn[:, cols], (((1,), (1,)), ((), ())),
                                              preferred_element_type=F32)
                dvn_ref[rows, cols] = jnp.dot(wst_ref[h], dvmb, preferred_element_type=F32)
            dvn = dvn_ref[rows, :]
            gv = dvn * vg_ref[...]
            dv = rstd * (gv - vhat * jnp.mean(gv * vhat, axis=-1, keepdims=True))
            duv_ref[rows, D:2 * D] = dv.astype(duv_ref.dtype)
            st_ref[0:1, :] += jnp.sum(dvn * vhat, axis=0, keepdims=True)

    full3 = pl.BlockSpec((SG_HEADS, SG_CHUNK, LANE), lambda i: (0, 0, 0))
    acc3 = jax.ShapeDtypeStruct((SG_HEADS, SG_CHUNK, LANE), F32)
    return pl.pallas_call(
        body, name=name,
        out_shape=(jax.ShapeDtypeStruct((L, 2 * D), BF16), acc3, acc3, jax.ShapeDtypeStruct((SUBLANE, D), F32)),
        grid=(L // tr,),
        in_specs=[pl.BlockSpec((tr, D), lambda i: (i, 0)), pl.BlockSpec((tr, 2 * D), lambda i: (i, 0)),
                  pl.BlockSpec((1, D), lambda i: (0, 0)), full3, full3, full3],
        out_specs=(pl.BlockSpec((tr, 2 * D), lambda i: (i, 0)), full3, full3,
                   pl.BlockSpec((SUBLANE, D), lambda i: (0, 0))),
        scratch_shapes=[pltpu.VMEM((tr, D), F32)],
        compiler_params=_params("arbitrary"))(dp, uv, vg, ws, wst, bsb)


def _gelu(x):
    return 0.5 * x * (1.0 + jnp.tanh(GELU_C * (x + GELU_A * x * x * x)))


def _gelu_grad(x):
    th = jnp.tanh(GELU_C * (x + GELU_A * x * x * x))
    return 0.5 * (1.0 + th) + 0.5 * x * (1.0 - th * th) * GELU_C * (1.0 + 3.0 * GELU_A * x * x)


def _cmul_add(xr, xi, ar, ai, br, bi):
    return xr + ar * br - ai * bi, xi + ar * bi + ai * br


def _cmul_conj_add(xr, xi, ar, ai, br, bi):
    return xr + ar * br + ai * bi, xi + ar * bi - ai * br


def _to_subchunk_order(src_ref, dst_ref, n):
    for k in range(n):
        dst_ref[pl.ds(SUBLANE * k, SUBLANE), :] = src_ref[pl.ds(k, SUBLANE, stride=n), :].astype(dst_ref.dtype)


def _to_time_order(src_ref, dst_ref, n):
    for m in range(n):
        r, k = divmod(SUBLANE * m, n)
        dst_ref[pl.ds(SUBLANE * m, SUBLANE), :] = src_ref[pl.ds(SUBLANE * k + r, SUBLANE, stride=SUBLANE), :]


def _s5_fwd(u, bre, bim, cre, cim, pw, pos, dsk, *, name, tc=S5_CHUNK):
    L, D = u.shape
    W = S5_LANES // S5_BLOCKS
    nt = L // tc
    n = tc // SUBLANE

    def sub(k):
        return pl.ds(SUBLANE * k, SUBLANE)

    def body(u_ref, bre_ref, bim_ref, cre_ref, cim_ref, pw_ref, pos_ref, d_ref, sre_ref, sim_ref, ypre_ref, yg_ref,
             carry, up, yp):
        t = pl.program_id(1)

        @pl.when(t == 0)
        def _():
            carry[...] = jnp.zeros_like(carry)

        _to_subchunk_order(u_ref, up, n)
        uv = up[...]
        ub = uv.astype(BF16)
        sre_ref[...] = jnp.dot(ub, bre_ref[...], preferred_element_type=F32)
        sim_ref[...] = jnp.dot(ub, bim_ref[...], preferred_element_type=F32)

        ar, ai = pw_ref[8], pw_ref[9]
        xr = jnp.zeros((SUBLANE, W), F32)
        xi = jnp.zeros((SUBLANE, W), F32)
        for k in range(n):
            xr, xi = _cmul_add(sre_ref[sub(k), :], sim_ref[sub(k), :], ar, ai, xr, xi)
            sre_ref[sub(k), :] = xr
            sim_ref[sub(k), :] = xi
        for q, d in enumerate((1, 2, 4)):
            xr, xi = _cmul_add(xr, xi, pw_ref[2 * q], pw_ref[2 * q + 1], pltpu.roll(xr, d, 0), pltpu.roll(xi, d, 0))
        cr, ci = carry[0], carry[1]
        xr, xi = _cmul_add(xr, xi, pw_ref[6], pw_ref[7], cr, ci)
        first = lax.broadcasted_iota(jnp.int32, (SUBLANE, W), 0) == 0
        er = jnp.where(first, cr, pltpu.roll(xr, 1, 0))
        ei = jnp.where(first, ci, pltpu.roll(xi, 1, 0))
        last = slice(SUBLANE - 1, SUBLANE)
        carry[0] = jnp.broadcast_to(xr[last, :], (SUBLANE, W))
        carry[1] = jnp.broadcast_to(xi[last, :], (SUBLANE, W))
        for k in range(n):
            sr, si = _cmul_add(sre_ref[sub(k), :], sim_ref[sub(k), :], pos_ref[0, k:k + 1, :], pos_ref[1, k:k + 1, :],
                               er, ei)
            sre_ref[sub(k), :] = sr
            sim_ref[sub(k), :] = si
        yp[...] = (jnp.dot(sre_ref[...].astype(BF16), cre_ref[...], preferred_element_type=F32)
                   - jnp.dot(sim_ref[...].astype(BF16), cim_ref[...], preferred_element_type=F32) + d_ref[...] * uv)
        _to_time_order(yp, ypre_ref, n)
        yg_ref[...] = _gelu(ypre_ref[...])

    ch = pl.BlockSpec((tc, LANE), lambda j, t: (t, j))
    st = pl.BlockSpec((tc, W), lambda j, t: (t, j))
    bsp = pl.BlockSpec((None, LANE, W), lambda j, t: (j, 0, 0))
    csp = pl.BlockSpec((None, W, LANE), lambda j, t: (j, 0, 0))
    return pl.pallas_call(
        body, name=name,
        out_shape=(jax.ShapeDtypeStruct((L, S5_LANES), F32), jax.ShapeDtypeStruct((L, S5_LANES), F32),
                   jax.ShapeDtypeStruct((L, D), F32), jax.ShapeDtypeStruct((L, D), F32)),
        grid=(S5_BLOCKS, nt),
        in_specs=[ch, bsp, bsp, csp, csp, pl.BlockSpec((10, SUBLANE, W), lambda j, t: (0, 0, j)),
                  pl.BlockSpec((2, n, W), lambda j, t: (0, 0, j)), pl.BlockSpec((1, LANE), lambda j, t: (0, j))],
        out_specs=(st, st, ch, ch),
        scratch_shapes=[pltpu.VMEM((2, SUBLANE, W), F32), pltpu.VMEM((tc, LANE), F32), pltpu.VMEM((tc, LANE), F32)],
        compiler_params=_params("parallel", "arbitrary"))(u, bre, bim, cre, cim, pw, pos, dsk)


def _s5_bwd(dy, u, sre, sim, bre, bim, cre, cim, pwr, posr, dsk, *, name, tc=S5_CHUNK):
    L, D = u.shape
    W = S5_LANES // S5_BLOCKS
    nt = L // tc
    n = tc // SUBLANE
    nt_dims = (((1,), (1,)), ((), ()))
    tn_dims = (((0,), (0,)), ((), ()))

    def sub(k):
        return pl.ds(SUBLANE * k, SUBLANE)

    def body(dy_ref, u_ref, sre_ref, sim_ref, bre_ref, bim_ref, cre_ref, cim_ref, pw_ref, pos_ref, d_ref,
             du_ref, dbre_ref, dbim_ref, dcre_ref, dcim_ref, ga_ref, dd_ref, gre, gim, carry, gacc, up, dyp):
        t = pl.program_id(1)

        @pl.when(t == 0)
        def _():
            for r in (carry, gacc, dbre_ref, dbim_ref, dcre_ref, dcim_ref, ga_ref, dd_ref):
                r[...] = jnp.zeros_like(r)

        _to_subchunk_order(dy_ref, dyp, n)
        _to_subchunk_order(u_ref, up, n)
        dyv, uv = dyp[...], up[...]
        dyb, ub = dyv.astype(BF16), uv.astype(BF16)
        gre[...] = lax.dot_general(dyb, cre_ref[...], nt_dims, preferred_element_type=F32)
        gim[...] = -lax.dot_general(dyb, cim_ref[...], nt_dims, preferred_element_type=F32)
        br, bi = pw_ref[8], pw_ref[9]
        xr = jnp.zeros((SUBLANE, W), F32)
        xi = jnp.zeros((SUBLANE, W), F32)
        for k in reversed(range(n)):
            xr, xi = _cmul_add(gre[sub(k), :], gim[sub(k), :], br, bi, xr, xi)
            gre[sub(k), :] = xr
            gim[sub(k), :] = xi
        for q, d in enumerate((1, 2, 4)):
            xr, xi = _cmul_add(xr, xi, pw_ref[2 * q], pw_ref[2 * q + 1], pltpu.roll(xr, SUBLANE - d, 0),
                               pltpu.roll(xi, SUBLANE - d, 0))
        cr, ci = carry[0], carry[1]
        xr, xi = _cmul_add(xr, xi, pw_ref[6], pw_ref[7], cr, ci)
        top = lax.broadcasted_iota(jnp.int32, (SUBLANE, W), 0) == SUBLANE - 1
        er = jnp.where(top, cr, pltpu.roll(xr, SUBLANE - 1, 0))
        ei = jnp.where(top, ci, pltpu.roll(xi, SUBLANE - 1, 0))
        carry[0] = jnp.broadcast_to(xr[0:1, :], (SUBLANE, W))
        carry[1] = jnp.broadcast_to(xi[0:1, :], (SUBLANE, W))
        nr, ni = er, ei
        acc_r = jnp.zeros((SUBLANE, W), F32)
        acc_i = jnp.zeros((SUBLANE, W), F32)
        for k in reversed(range(n)):
            place = slice(n - 1 - k, n - k)
            gr, gi = _cmul_conj_add(gre[sub(k), :], gim[sub(k), :], pos_ref[0, place, :], pos_ref[1, place, :], er, ei)
            gre[sub(k), :] = gr
            gim[sub(k), :] = gi
            sr, si = sre_ref[sub(k), :], sim_ref[sub(k), :]
            acc_r = acc_r + sr * nr + si * ni
            acc_i = acc_i + sr * ni - si * nr
            nr, ni = gr, gi
        gacc[0] += acc_r
        gacc[1] += acc_i
        grb, gib = gre[...].astype(BF16), gim[...].astype(BF16)
        dyp[...] = (lax.dot_general(grb, bre_ref[...], nt_dims, preferred_element_type=F32)
                    + lax.dot_general(gib, bim_ref[...], nt_dims, preferred_element_type=F32) + d_ref[...] * dyv)
        _to_time_order(dyp, up, n)
        du_ref[...] = up[...].astype(du_ref.dtype)
        dbre_ref[...] += lax.dot_general(ub, grb, tn_dims, preferred_element_type=F32)
        dbim_ref[...] += lax.dot_general(ub, gib, tn_dims, preferred_element_type=F32)
        dcre_ref[...] += lax.dot_general(sre_ref[...].astype(BF16), dyb, tn_dims, preferred_element_type=F32)
        dcim_ref[...] -= lax.dot_general(sim_ref[...].astype(BF16), dyb, tn_dims, preferred_element_type=F32)
        dd_ref[0:1, :] += jnp.sum(dyv * uv, axis=0, keepdims=True)

        @pl.when(t == nt - 1)
        def _():
            ga_ref[0:1, :] = jnp.sum(gacc[0], axis=0, keepdims=True)
            ga_ref[1:2, :] = jnp.sum(gacc[1], axis=0, keepdims=True)

    ch = pl.BlockSpec((tc, LANE), lambda j, t: (nt - 1 - t, j))
    st = pl.BlockSpec((tc, W), lambda j, t: (nt - 1 - t, j))
    bsp = pl.BlockSpec((None, LANE, W), lambda j, t: (j, 0, 0))
    csp = pl.BlockSpec((None, W, LANE), lambda j, t: (j, 0, 0))
    return pl.pallas_call(
        body, name=name,
        out_shape=(jax.ShapeDtypeStruct((L, D), BF16),
                   jax.ShapeDtypeStruct((S5_BLOCKS, LANE, W), F32), jax.ShapeDtypeStruct((S5_BLOCKS, LANE, W), F32),
                   jax.ShapeDtypeStruct((S5_BLOCKS, W, LANE), F32), jax.ShapeDtypeStruct((S5_BLOCKS, W, LANE), F32),
                   jax.ShapeDtypeStruct((SUBLANE, S5_LANES), F32), jax.ShapeDtypeStruct((SUBLANE, D), F32)),
        grid=(S5_BLOCKS, nt),
        in_specs=[ch, ch, st, st, bsp, bsp, csp, csp, pl.BlockSpec((10, SUBLANE, W), lambda j, t: (0, 0, j)),
                  pl.BlockSpec((2, n, W), lambda j, t: (0, 0, j)), pl.BlockSpec((1, LANE), lambda j, t: (0, j))],
        out_specs=(ch, bsp, bsp, csp, csp, pl.BlockSpec((SUBLANE, W), lambda j, t: (0, j)),
                   pl.BlockSpec((SUBLANE, LANE), lambda j, t: (0, j))),
        scratch_shapes=[pltpu.VMEM((tc, W), F32), pltpu.VMEM((tc, W), F32), pltpu.VMEM((2, SUBLANE, W), F32),
                        pltpu.VMEM((2, SUBLANE, W), F32), pltpu.VMEM((tc, LANE), F32), pltpu.VMEM((tc, LANE), F32)],
        compiler_params=_params("parallel", "arbitrary"))(dy, u, sre, sim, bre, bim, cre, cim, pwr, posr, dsk)


def _glu_bwd(dy2, y, t, *, name, tm=256):
    L, D = y.shape

    def body(dy2_ref, y_ref, t_ref, dt_ref, dya_ref, st_ref):
        i = pl.program_id(0)

        @pl.when(i == 0)
        def _():
            st_ref[...] = jnp.zeros_like(st_ref)

        sig = 1.0 / (1.0 + jnp.exp(-t_ref[...]))
        dy2v = dy2_ref[...]
        dt = dy2v * y_ref[...] * sig * (1.0 - sig)
        dt_ref[...] = dt.astype(dt_ref.dtype)
        dya_ref[...] = dy2v * sig
        st_ref[0:1, :] += jnp.sum(dt, axis=0, keepdims=True)

    row = pl.BlockSpec((tm, D), lambda i: (i, 0))
    return pl.pallas_call(
        body, name=name,
        out_shape=(jax.ShapeDtypeStruct((L, D), BF16), jax.ShapeDtypeStruct((L, D), F32),
                   jax.ShapeDtypeStruct((SUBLANE, D), F32)),
        grid=(L // tm,), in_specs=[row, row, row],
        out_specs=(row, row, pl.BlockSpec((SUBLANE, D), lambda i: (0, 0))),
        compiler_params=_params("arbitrary"))(dy2, y, t)


def _s5_prep(a_re, a_im, log_dt, b_re, b_im, c_re, c_im):
    dt = jnp.exp(log_dt)[:, None]
    mag = jnp.exp(a_re * dt)
    abar_re = mag * jnp.cos(a_im * dt)
    abar_im = mag * jnp.sin(a_im * dt)
    den = a_re * a_re + a_im * a_im
    nr = abar_re - 1.0
    ni = abar_im
    f_re = ((nr * a_re + ni * a_im) / den)[..., None]
    f_im = ((ni * a_re - nr * a_im) / den)[..., None]
    bbar_re = f_re * b_re - f_im * b_im
    bbar_im = f_re * b_im + f_im * b_re
    eye = jnp.eye(S5_GROUPS // S5_BLOCKS, dtype=F32)
    gb = S5_GROUPS // S5_BLOCKS

    def blk_b(bb):
        t = bb.reshape(S5_BLOCKS, gb, S5_STATE, S5_GROUP)
        return jnp.einsum('jgph,gk->jghkp', t, eye).reshape(S5_BLOCKS, gb * S5_GROUP, gb * S5_STATE)

    def blk_c(cc):
        t = cc.reshape(S5_BLOCKS, gb, S5_GROUP, S5_STATE)
        return jnp.einsum('jghp,gk->jgpkh', t, eye).reshape(S5_BLOCKS, gb * S5_STATE, gb * S5_GROUP)

    return (abar_re.reshape(1, S5_LANES), abar_im.reshape(1, S5_LANES), blk_b(bbar_re), blk_b(bbar_im),
            blk_c(c_re), blk_c(c_im))


def _cpowers(ar, ai, count):
    pr, pi, m = ar, ai, 1
    while m < count:
        tr, ti = pr[m - 1:m], pi[m - 1:m]
        pr, pi = jnp.concatenate([pr, pr * tr - pi * ti], 0), jnp.concatenate([pi, pr * ti + pi * tr], 0)
        m *= 2
    return pr, pi


def _s5_power_tables(ar, ai, n):
    pr, pi = _cpowers(ar, ai, n)
    qr, qi = _cpowers(pr[n - 1:n], pi[n - 1:n], SUBLANE)
    row = jnp.arange(SUBLANE)[:, None]
    lanes = ar.shape[1]

    def tables(sign, keep, order):
        out = []
        for d in (1, 2, 4):
            out += [jnp.where(keep(d), qr[d - 1:d], 0.0), jnp.where(keep(d), sign * qi[d - 1:d], 0.0)]
        out += [jnp.concatenate([qr[r:r + 1] for r in order], 0), sign * jnp.concatenate([qi[r:r + 1] for r in order], 0),
                ar, sign * ai]
        return jnp.stack([jnp.broadcast_to(o, (SUBLANE, lanes)) for o in out])

    fwd = tables(1.0, lambda d: row >= d, list(range(SUBLANE)))
    rev = tables(-1.0, lambda d: row + d <= SUBLANE - 1, list(reversed(range(SUBLANE))))
    return fwd, rev, jnp.stack([pr, pi])


ADAMW_PART_BLOCK_BYTES = 2 * 1024 * 1024


def _adamw(w, parts, m, v, *, name):
    n, R, C = w.shape
    assert len(parts) == n
    P = parts[0].shape[0]
    tr = R
    while P * tr * C * parts[0].dtype.itemsize > ADAMW_PART_BLOCK_BYTES and tr % 16 == 0:
        tr //= 2
    c1 = 1.0 / (1.0 - ADAM_B1 ** ADAM_STEP)
    c2 = 1.0 / (1.0 - ADAM_B2 ** ADAM_STEP)

    def body(*refs):
        w_ref, m_ref, v_ref = refs[:3]
        p_refs = refs[3:3 + n]
        g_ref, d_ref, nm_ref, nv_ref = refs[3 + n:]
        layer = pl.program_id(0)
        for q, p_ref in enumerate(p_refs):
            @pl.when(layer == q)
            def _(p_ref=p_ref):
                g = p_ref[0].astype(F32)
                for s in range(1, P):
                    g = g + p_ref[s].astype(F32)
                nm = ADAM_B1 * m_ref[...] + (1.0 - ADAM_B1) * g
                nv = ADAM_B2 * v_ref[...] + (1.0 - ADAM_B2) * (g * g)
                g_ref[...] = g
                nm_ref[...] = nm
                nv_ref[...] = nv
                d_ref[...] = -ADAM_LR * ((nm * c1) / (jnp.sqrt(nv * c2) + ADAM_EPS) + ADAM_WD * w_ref[...])

    row = pl.BlockSpec((None, tr, C), lambda l, i: (l, i, 0))
    part_specs = [pl.BlockSpec((P, tr, C), lambda l, i, q=q: (0, jnp.where(l == q, i, 0), 0)) for q in range(n)]
    out = jax.ShapeDtypeStruct((n, R, C), F32)
    return pl.pallas_call(body, name=name, out_shape=(out, out, out, out), grid=(n, R // tr),
                          in_specs=[row, row, row] + part_specs, out_specs=(row, row, row, row),
                          compiler_params=_params("arbitrary", "arbitrary"))(w, m, v, *parts)


def _all_gather(xs, axis, *, name):
    m = xs.shape[axis]
    out_shape = list(xs.shape)
    out_shape[axis] = N_DEV * m

    def body(x_ref, out_ref, send_sems, recv_sems, local_sem):
        x, y, c = _my_pos()
        me, sibling = (x, y, c), (x, y, 1 - c)
        chips = [(1 - x, y), (x, 1 - y), (1 - x, 1 - y)]

        def blk(px, py, pc):
            idx = [slice(None)] * 3
            idx[axis] = pl.ds((4 * px + 2 * py + pc) * m, m)
            return out_ref.at[tuple(idx)]

        def copy(k, block, to, src=None):
            return pltpu.make_async_remote_copy(src_ref=blk(*block) if src is None else src, dst_ref=blk(*block),
                                                send_sem=send_sems.at[k], recv_sem=recv_sems.at[k],
                                                device_id=to, device_id_type=MESH_ID)

        mine = pltpu.make_async_copy(x_ref, blk(*me), local_sem)
        mine.start()
        first = [copy(0, me, sibling, src=x_ref)]
        first += [copy(1 + j, me, (*chip, c), src=x_ref) for j, chip in enumerate(chips)]
        for cp in first:
            cp.start()
        passed = [copy(4 + j, (*chip, c), sibling) for j, chip in enumerate(chips)]
        for j, chip in enumerate(chips):
            copy(1 + j, (*chip, c), me).wait_recv()
            passed[j].start()
        copy(0, sibling, me).wait_recv()
        for j, chip in enumerate(chips):
            copy(4 + j, (*chip, 1 - c), me).wait_recv()
        for cp in first + passed:
            cp.wait_send()
        mine.wait()

    hbm = pl.BlockSpec(memory_space=pl.ANY)
    return pl.pallas_call(body, name=name, out_shape=jax.ShapeDtypeStruct(tuple(out_shape), xs.dtype),
                          in_specs=[hbm], out_specs=hbm,
                          scratch_shapes=[pltpu.SemaphoreType.DMA((N_DEV - 1,)), pltpu.SemaphoreType.DMA((N_DEV - 1,)),
                                          pltpu.SemaphoreType.DMA],
                          compiler_params=pltpu.CompilerParams(has_side_effects=True))(xs)


NEAR_PEERS = (1, 2, 4, 6)
RELAY_PEERS = (2, 4, 6)


def _block(ref, axis, idx, m):
    return ref.at[pl.ds(idx * m, m), :] if axis == 0 else ref.at[:, pl.ds(idx * m, m)]


def _exchange_copies(metas, src_refs, zone_refs, send_sems, recv_sems, base, phase):
    x, y, c = _my_pos()
    me = 4 * x + 2 * y + c

    def place(r):
        pos = (1 - x if r & 4 else x, 1 - y if r & 2 else y, 1 - c if r & 1 else c)
        return pos, 4 * pos[0] + 2 * pos[1] + pos[2]

    def copies(r, to, src, dst, arrival):
        return tuple(pltpu.make_async_remote_copy(src_ref=src, dst_ref=d, send_sem=send_sems.at[base + r - 1],
                                                  recv_sem=recv_sems.at[base + r - 1], device_id=to,
                                                  device_id_type=MESH_ID) for d in (dst, arrival))

    pairs, own = [], []
    if phase == 'relay':
        sibling, _ = place(1)
        for r in RELAY_PEERS:
            held, comes = place(r)[1], place(r | 1)[1]
            for (kind, axis, m), z_ref in zip(metas, zone_refs):
                pairs.append(copies(r, sibling, _block(z_ref, axis, held, m), _block(z_ref, axis, held, m),
                                    _block(z_ref, axis, comes, m)))
        return pairs, own
    for r in (NEAR_PEERS if phase == 'near' else range(1, N_DEV)):
        pos, peer = place(r)
        for (kind, axis, m), s_ref, z_ref in zip(metas, src_refs, zone_refs):
            if kind == 'gather':
                pairs.append(copies(r, pos, s_ref, _block(z_ref, axis, me, m), _block(z_ref, axis, peer, m)))
            else:
                pairs.append(copies(r, pos, _block(s_ref, axis, peer, m), z_ref.at[me], z_ref.at[peer]))
    for (kind, axis, m), s_ref, z_ref in zip(metas, src_refs, zone_refs):
        src, dst = (s_ref, _block(z_ref, axis, me, m)) if kind == 'gather' else (_block(s_ref, axis, me, m), z_ref.at[me])
        own.append(pltpu.make_async_copy(src, dst, recv_sems.at[base + N_DEV - 1]))
    return pairs, own


def _exchange_start(groups, after, *, name, relayed=False):
    flat = [it for g in groups for it in g]
    n, ng = len(flat), len(groups)
    metas = [it[2] for it in flat]
    bounds = [(sum(len(g) for g in groups[:q]), sum(len(g) for g in groups[:q + 1])) for q in range(ng)]
    phase = 'near' if relayed else 'all'

    def body(*refs):
        src_refs = refs[:n]
        send_sems, recv_sems = refs[n + 1], refs[n + 2]
        zone_refs = refs[2 * n + 3:3 * n + 3]
        token = refs[-1]
        for q, (lo, hi) in enumerate(bounds):
            pairs, own = _exchange_copies(metas[lo:hi], src_refs[lo:hi], zone_refs[lo:hi], send_sems, recv_sems,
                                          q * N_DEV, phase)
            for outgoing, _ in pairs:
                outgoing.start()
            for cp in own:
                cp.start()
        token[...] = jnp.zeros_like(token)

    hbm = pl.BlockSpec(memory_space=pltpu.HBM)
    sem = pl.BlockSpec(memory_space=pltpu.SEMAPHORE)
    srcs = [it[0] for it in flat]
    res = pl.pallas_call(
        body, name=name,
        out_shape=(pltpu.SemaphoreType.DMA((ng * N_DEV,)), pltpu.SemaphoreType.DMA((ng * N_DEV,)),
                   *[pltpu.HBM(a.shape, a.dtype) for a in srcs], *[pltpu.HBM(it[1], it[0].dtype) for it in flat],
                   jax.ShapeDtypeStruct((SUBLANE, LANE), F32)),
        in_specs=[hbm] * n + [pl.BlockSpec(memory_space=pl.ANY)],
        out_specs=(sem, sem, *[hbm] * (2 * n), pl.BlockSpec(memory_space=pltpu.VMEM)),
        input_output_aliases={q: 2 + q for q in range(n)},
        compiler_params=pltpu.CompilerParams(has_side_effects=pltpu.SideEffectType.DATAFLOW_SIDE_EFFECTING),
    )(*[pltpu.with_memory_space_constraint(a, pltpu.HBM) for a in srcs], after)
    handles = [(res[0], res[1], q * N_DEV, phase, list(res[2 + lo:2 + hi]), list(res[2 + n + lo:2 + n + hi]),
                metas[lo:hi]) for q, (lo, hi) in enumerate(bounds)]
    return handles, res[-1]


def _exchange_wait(handle, after, *, name):
    send_sems, recv_sems, base, phase, srcs, zones, metas = handle
    ns, nz = len(srcs), len(zones)

    def body(*refs):
        src_refs, zone_refs = refs[:ns], refs[ns:ns + nz]
        s_sems, r_sems = refs[ns + nz], refs[ns + nz + 1]
        pairs, own = _exchange_copies(metas, src_refs, zone_refs, s_sems, r_sems, base, phase)
        for outgoing, incoming in pairs:
            outgoing.wait_send()
            incoming.wait_recv()
        for cp in own:
            cp.wait()

    hbm = pl.BlockSpec(memory_space=pltpu.HBM)
    sem = pl.BlockSpec(memory_space=pltpu.SEMAPHORE)
    arrays = srcs + zones
    res = pl.pallas_call(
        body, name=name,
        out_shape=tuple(pltpu.HBM(a.shape, a.dtype) for a in arrays),
        in_specs=[hbm] * (ns + nz) + [sem, sem, pl.BlockSpec(memory_space=pl.ANY)],
        out_specs=tuple([hbm] * (ns + nz)),
        input_output_aliases={q: q for q in range(ns + nz)},
        compiler_params=pltpu.CompilerParams(has_side_effects=pltpu.SideEffectType.DATAFLOW_SIDE_EFFECTING),
    )(*arrays, send_sems, recv_sems, after)
    return list(res[ns:])


def _exchange_relay(handle, after, *, name):
    metas = handle[6]
    zones = _exchange_wait(handle, after, name=name + "_in")
    nz = len(zones)

    def body(*refs):
        zone_refs = refs[:nz]
        send_sems, recv_sems = refs[nz], refs[nz + 1]
        pairs, _ = _exchange_copies(metas, (), zone_refs, send_sems, recv_sems, 0, 'relay')
        for outgoing, _ in pairs:
            outgoing.start()
        refs[-1][...] = jnp.zeros_like(refs[-1])

    hbm = pl.BlockSpec(memory_space=pltpu.HBM)
    sem = pl.BlockSpec(memory_space=pltpu.SEMAPHORE)
    res = pl.pallas_call(
        body, name=name + "_out",
        out_shape=(pltpu.SemaphoreType.DMA((N_DEV,)), pltpu.SemaphoreType.DMA((N_DEV,)),
                   *[pltpu.HBM(z.shape, z.dtype) for z in zones], jax.ShapeDtypeStruct((SUBLANE, LANE), F32)),
        in_specs=[hbm] * nz, out_specs=(sem, sem, *[hbm] * nz, pl.BlockSpec(memory_space=pltpu.VMEM)),
        input_output_aliases={q: 2 + q for q in range(nz)},
        compiler_params=pltpu.CompilerParams(has_side_effects=pltpu.SideEffectType.DATAFLOW_SIDE_EFFECTING),
    )(*zones)
    return (res[0], res[1], 0, 'relay', [], list(res[2:2 + nz]), metas), res[-1][0:1, 0:1]


def _pad_rows(a, rows):
    return jnp.pad(a, ((0, rows - a.shape[0]), (0, 0)))


PACK_ROWS = 2 * SUBLANE


def _rows(a):
    flat = a.reshape(-1).astype(F32)
    pad = -flat.shape[0] % (PACK_ROWS * LANE)
    return (jnp.pad(flat, (0, pad)) if pad else flat).reshape(-1, LANE)


def _pack_rows(arrays):
    return jnp.concatenate([_rows(a) for a in arrays], 0)


def _unpack_rows(t, shapes):
    out, off = [], 0
    for shp in shapes:
        size = math.prod(shp)
        rows = -(-size // (PACK_ROWS * LANE)) * PACK_ROWS
        out.append(t[off:off + rows].reshape(-1)[:size].reshape(shp))
        off += rows
    return out


def _stat_row(st, r):
    return st[r:r + 1, :]


def kernel(x, c, ada_w, ada_b, norm1_g, norm2_g, ff_w1, ff_w2, final_g, conv_w_in, conv_w, conv_b, conv_w_out, ssm_w_in, ssm_a_re, ssm_a_im, ssm_log_dt, ssm_b_re, ssm_b_im, ssm_c_re, ssm_c_im, ssm_d, ssm_glu_w, ssm_glu_b, ssm_w_out, sg_w_in, sg_v_g, sg_w_s, sg_b_s, sg_w_out, loss_target, m_ada_w, m_ada_b, m_norm1_g, m_norm2_g, m_ff_w1, m_ff_w2, m_final_g, m_conv_w_in, m_conv_w, m_conv_b, m_conv_w_out, m_ssm_w_in, m_ssm_a_re, m_ssm_a_im, m_ssm_log_dt, m_ssm_b_re, m_ssm_b_im, m_ssm_c_re, m_ssm_c_im, m_ssm_d, m_ssm_glu_w, m_ssm_glu_b, m_ssm_w_out, m_sg_w_in, m_sg_v_g, m_sg_w_s, m_sg_b_s, m_sg_w_out, v_ada_w, v_ada_b, v_norm1_g, v_norm2_g, v_ff_w1, v_ff_w2, v_final_g, v_conv_w_in, v_conv_w, v_conv_b, v_conv_w_out, v_ssm_w_in, v_ssm_a_re, v_ssm_a_im, v_ssm_log_dt, v_ssm_b_re, v_ssm_b_im, v_ssm_c_re, v_ssm_c_im, v_ssm_d, v_ssm_glu_w, v_ssm_glu_b, v_ssm_w_out, v_sg_w_in, v_sg_v_g, v_sg_w_s, v_sg_b_s, v_sg_w_out):
    P = dict(zip(INPUTS, (x, c, ada_w, ada_b, norm1_g, norm2_g, ff_w1, ff_w2, final_g, conv_w_in, conv_w, conv_b, conv_w_out, ssm_w_in, ssm_a_re, ssm_a_im, ssm_log_dt, ssm_b_re, ssm_b_im, ssm_c_re, ssm_c_im, ssm_d, ssm_glu_w, ssm_glu_b, ssm_w_out, sg_w_in, sg_v_g, sg_w_s, sg_b_s, sg_w_out, loss_target, m_ada_w, m_ada_b, m_norm1_g, m_norm2_g, m_ff_w1, m_ff_w2, m_final_g, m_conv_w_in, m_conv_w, m_conv_b, m_conv_w_out, m_ssm_w_in, m_ssm_a_re, m_ssm_a_im, m_ssm_log_dt, m_ssm_b_re, m_ssm_b_im, m_ssm_c_re, m_ssm_c_im, m_ssm_d, m_ssm_glu_w, m_ssm_glu_b, m_ssm_w_out, m_sg_w_in, m_sg_v_g, m_sg_w_s, m_sg_b_s, m_sg_w_out, v_ada_w, v_ada_b, v_norm1_g, v_norm2_g, v_ff_w1, v_ff_w2, v_final_g, v_conv_w_in, v_conv_w, v_conv_b, v_conv_w_out, v_ssm_w_in, v_ssm_a_re, v_ssm_a_im, v_ssm_log_dt, v_ssm_b_re, v_ssm_b_im, v_ssm_c_re, v_ssm_c_im, v_ssm_d, v_ssm_glu_w, v_ssm_glu_b, v_ssm_w_out, v_sg_w_in, v_sg_v_g, v_sg_w_s, v_sg_b_s, v_sg_w_out)))
    L, D = x.shape[1], x.shape[2]
    me = _my_index()
    xs = x[0]
    tgt = loss_target[0]
    n_conv = conv_w_in.shape[0]

    def gather_item(shard, axis):
        full = tuple(N_DEV * s if a == axis else s for a, s in enumerate(shard.shape))
        return shard, full, ('gather', axis, shard.shape[axis])

    def mixer_shards(i):
        kind, j = i % 3, i // 3
        if kind == 0:
            return [(conv_w_in[j], 1), (conv_w_out[j], 0)]
        if kind == 1:
            return [(ssm_w_in[j], 0), (ssm_glu_w[j], 0), (ssm_w_out[j], 0)]
        return [(sg_w_in[j], 1), (sg_w_out[j], 0)]

    c_act = c * (1.0 / (1.0 + jnp.exp(-c)))
    vec_rows = jnp.concatenate([c_act.reshape(D // LANE, LANE), conv_w.reshape(-1, LANE), conv_b.reshape(-1, LANE),
                                sg_v_g.reshape(-1, LANE)], 0)
    n_vec = vec_rows.shape[0]
    vec_all = _all_gather(_pad_rows(vec_rows, 24)[None], 0, name="gather_vectors")
    c_all = vec_all[:, :D // LANE, :].reshape(N_DEV, D)
    sharded_full = vec_all[:, D // LANE:n_vec, :].transpose(1, 0, 2).reshape(n_vec - D // LANE, D)
    conv_w_full = sharded_full[:3 * n_conv].reshape(n_conv, 3, D)
    conv_b_full = sharded_full[3 * n_conv:4 * n_conv]
    sg_vg_full = sharded_full[4 * n_conv:4 * n_conv + 1]

    c_pad = _pad_rows(c_all, LANE)
    ncol = ada_w.shape[2]
    mod_part = jnp.stack([_mm(c_pad, ada_w[i], name=f"ada_fwd{i}")[:N_DEV] for i in range(DEPTH)])
    mod_all = _all_gather(mod_part.reshape(1, DEPTH * N_DEV, ncol), 0, name="gather_mod")
    mod_all = mod_all.reshape(N_DEV, DEPTH, N_DEV, ncol)
    mod_me = lax.dynamic_index_in_dim(mod_all, me, 2, keepdims=False)
    mod = mod_me.transpose(1, 0, 2).reshape(DEPTH, N_DEV * ncol) + ada_b
    gathers, gather_token = _exchange_start(
        [[gather_item(w.astype(BF16), ax) for w, ax in shards]
         for i in range(DEPTH) for shards in (mixer_shards(i), [(ff_w1[i], 1), (ff_w2[i], 0)])],
        mod, name="gather_start", relayed=True)
    mod = mod + gather_token[0:1, 0:1]
    relayed = [None] * len(gathers)
    relayed[0], sent = _exchange_relay(gathers[0], mod, name="gather_mix_relay0")

    s5_args = (ssm_a_re[0], ssm_a_im[0], ssm_log_dt[0], ssm_b_re[0], ssm_b_im[0], ssm_c_re[0], ssm_c_im[0])
    (abar_re, abar_im, bblk_re, bblk_im, cblk_re, cblk_im), s5_vjp = jax.vjp(_s5_prep, *s5_args)
    pw_fwd, pw_rev, pos_fwd = _s5_power_tables(abar_re, abar_im, S5_CHUNK // SUBLANE)
    s5_w = tuple(t.astype(BF16) for t in (bblk_re, bblk_im, cblk_re, cblk_im))
    causal = jnp.tril(jnp.ones((SG_CHUNK, SG_CHUNK), dtype=bool))
    ws_m = jnp.where(causal[None], sg_w_s[0], 0.0)
    ws_b = ws_m.astype(BF16)
    wst_b = ws_m.transpose(0, 2, 1).astype(BF16)
    bsb = jnp.broadcast_to(sg_b_s[0][:, :, None], (SG_HEADS, SG_CHUNK, LANE))

    saved = []
    xa = xs
    mods = [[mod[i:i + 1, q * D:(q + 1) * D] for q in range(6)] for i in range(DEPTH)]
    wn1s = [norm1_g[i:i + 1] * (1.0 + mods[i][1]) for i in range(DEPTH)]
    h1 = _normmod_fwd(xa, wn1s[0], mods[0][0] + sent, name="norm1_fwd0")
    for i in range(DEPTH):
        kind, j = i % 3, i // 3
        sh1, sc1, g1, sh2, sc2, g2 = mods[i]
        wn1 = wn1s[i]
        wn2 = norm2_g[i:i + 1] * (1.0 + sc2)
        S = dict(x_in=xa, g1=g1, g2=g2, sc1=sc1, sc2=sc2, wn1=wn1, wn2=wn2)
        w_mix = _exchange_wait(relayed[2 * i], h1, name=f"gather_mix_wait{i}")
        S['h1'] = h1
        if kind == 0:
            bcx = _mm(h1, w_mix[0], name=f"conv_in{i}", out_dtypes=(BF16,), bm=2048)
            wb = _pad_rows(jnp.concatenate([conv_w_full[j], conv_b_full[j:j + 1]], 0), SUBLANE)
            pb = _conv_fwd(bcx, wb, name=f"conv_mix{i}")
            S.update(bcx=bcx, wb=wb, pb=pb)
        elif kind == 1:
            u = _mm(h1, w_mix[0], name=f"ssm_in{i}")
            sre, sim, ypre, yg = _s5_fwd(u, *s5_w, pw_fwd, pos_fwd, ssm_d, name=f"s5_scan{i}")

            def glu_epi(acc, yv, bias):
                t = acc + bias
                return yv * (1.0 / (1.0 + jnp.exp(-t))), t

            pb, tt = _mm(yg, w_mix[1], name=f"ssm_glu{i}", out_dtypes=(BF16, F32), epi=glu_epi,
                         extras=[(yg, 'mn'), (ssm_glu_b, 'n')])
            S.update(u=u, sre=sre, sim=sim, ypre=ypre, yg=yg, pb=pb, tt=tt)
        else:
            uv = _mm(h1, w_mix[0], name=f"sg_in{i}", bm=2048)
            pb = _sg_fwd(uv, sg_vg_full, ws_b, bsb, name=f"sg_mix{i}")
            S.update(uv=uv, pb=pb)
        relayed[2 * i + 1], sent = _exchange_relay(gathers[2 * i + 1], pb, name=f"gather_ff_relay{i}")
        x_mid, y_mix, h2 = _mm(pb, w_mix[-1], name=f"mix_out{i}", out_dtypes=(F32, BF16, BF16), epi=_epi_residual_norm,
                               extras=[(xa, 'mn'), (g1 + sent, 'n'), (wn2, 'n'), (sh2, 'n')])
        w1_full, w2_full = _exchange_wait(relayed[2 * i + 1], h2, name=f"gather_ff_wait{i}")
        S.update(w_mix=w_mix, w1=w1_full, w2=w2_full)
        ra = _mm(h2, w1_full, name=f"ff_up{i}", out_dtypes=(BF16,), epi=lambda acc: (jnp.maximum(acc, 0.0),), bm=2048)
        if i + 1 < DEPTH:
            relayed[2 * i + 2], sent = _exchange_relay(gathers[2 * i + 2], ra, name=f"gather_mix_relay{i + 1}")
            xa, f_out, h1 = _mm(ra, w2_full, name=f"ff_down{i}", out_dtypes=(F32, BF16, BF16), a_fn=_square,
                                epi=_epi_residual_norm, bm=256, bk=w2_full.shape[0],
                                extras=[(x_mid, 'mn'), (g2 + sent, 'n'), (wn1s[i + 1], 'n'), (mods[i + 1][0], 'n')])
        else:
            f_out = None
            dx, dfb, st = _mm(ra, w2_full, name=f"ff_down{i}", out_dtypes=(F32, BF16), epi=_epi_loss_head, a_fn=_square,
                              n_stats=3, bm=256, bk=w2_full.shape[0],
                              extras=[(x_mid, 'mn'), (g2, 'n'), (tgt, 'mn'), (final_g[None], 'n')])
        S.update(x_mid=x_mid, y_mix=y_mix, h2=h2, ra=ra, f_out=f_out)
        saved.append(S)

    loss_tile = st[:, :LANE]
    d_final_g = _stat_row(st, 1)
    dg2_next = _stat_row(st, 2)

    def scatter_item(g, axis):
        m = g.shape[axis] // N_DEV
        shard = tuple(m if a == axis else s for a, s in enumerate(g.shape))
        return g, (N_DEV,) + shard, ('scatter', axis, m)

    dmod = [None] * DEPTH
    dn1g, dn2g = [None] * DEPTH, [None] * DEPTH
    d_conv_w, d_conv_b = [None] * n_conv, [None] * n_conv
    ff_sent, mix_sent = [None] * DEPTH, [None] * DEPTH
    small = {}
    for i in reversed(range(DEPTH)):
        kind, j = i % 3, i // 3
        S = saved[i]
        w_mix = S['w_mix']
        dg2 = dg2_next
        da = _mm(dfb, S['w2'], tb=True, name=f"ff_down_bwd{i}", out_dtypes=(BF16,), bm=2048,
                 epi=lambda acc, rav: (acc * (2.0 * rav.astype(F32)),), extras=[(S['ra'], 'mn')])
        dw2 = _wgrad(S['ra'], dfb, name=f"ff_w2_grad{i}", a_fn=_square, bm=256, bn=1024)
        dw1 = _wgrad(S['h2'], da, name=f"ff_w1_grad{i}")
        (ff_sent[i],), token = _exchange_start([[scatter_item(dw1, 1), scatter_item(dw2, 0)]], dx,
                                               name=f"ff_grads_start{i}")
        dx_mid, dyb, st2 = _mm(da, S['w1'], tb=True, name=f"ff_up_bwd{i}", out_dtypes=(F32, BF16), bm=256,
                               bk=da.shape[1], epi=_epi_norm_bwd(True), n_stats=3,
                               extras=[(S['x_mid'], 'mn'), (S['wn2'] + token[0:1, 0:1], 'n'), (dx, 'mn'),
                                       (S['y_mix'], 'mn'), (S['g1'], 'n')])
        dsc2 = _stat_row(st2, 0) * norm2_g[i:i + 1]
        dn2g[i] = _stat_row(st2, 0) * (1.0 + S['sc2'])
        dsh2 = _stat_row(st2, 1)
        dg1 = _stat_row(st2, 2)
        if kind == 0:
            dp = _mm(dyb, w_mix[1], tb=True, name=f"conv_out_bwd{i}", out_dtypes=(BF16,))
            d_cwo = _wgrad(S['pb'], dyb, name=f"conv_w_out_grad{i}")
            dbcx, stc = _conv_bwd(dp, S['bcx'], S['wb'], name=f"conv_mix_bwd{i}")
            d_conv_w[j] = stc[0:3]
            d_conv_b[j] = stc[3:4]
            dh_operand, dh_name = dbcx, "conv_in_bwd"
            d_cwi = _wgrad(S['h1'], dbcx, name=f"conv_w_in_grad{i}")
            mix_grads = [scatter_item(d_cwi, 1), scatter_item(d_cwo, 0)]
        elif kind == 1:
            dtb, dya, stg = _mm(dyb, w_mix[2], tb=True, name=f"ssm_out_bwd{i}", out_dtypes=(BF16, F32), bm=512,
                                epi=_epi_glu_bwd, n_stats=1, extras=[(S['yg'], 'mn'), (S['tt'], 'mn')])
            d_ssm_out = _wgrad(S['pb'], dyb, name=f"ssm_w_out_grad{i}")
            dypre = _mm(dtb, w_mix[1], tb=True, name=f"ssm_glu_in_bwd{i}",
                        epi=lambda acc, a, yp: ((a + acc) * _gelu_grad(yp),),
                        extras=[(dya, 'mn'), (S['ypre'], 'mn')])
            d_glu = _wgrad(S['yg'], dtb, name=f"ssm_glu_w_grad{i}", bm=512)
            dub, dbre, dbim, dcre, dcim, ga, dd = _s5_bwd(dypre, S['u'], S['sre'], S['sim'], *s5_w, pw_rev, pos_fwd, ssm_d,
                                                           name=f"s5_scan_bwd{i}")
            dh_operand, dh_name = dub, "ssm_in_bwd"
            d_ssm_in = _wgrad(S['h1'], dub, name=f"ssm_w_in_grad{i}")
            da_re, da_im, dlog_dt, db_re, db_im, dc_re, dc_im = s5_vjp((ga[0:1], ga[1:2], dbre, dbim, dcre, dcim))
            s5_small = _pack_rows([da_re, da_im, dlog_dt, db_re, db_im, dc_re, dc_im, dd[0], stg[0]])
            mix_grads = [scatter_item(d_ssm_in, 0), scatter_item(d_glu, 0), scatter_item(d_ssm_out, 0),
                         gather_item(s5_small.astype(BF16), 0)]
        else:
            dp = _mm(dyb, w_mix[1], tb=True, name=f"sg_out_bwd{i}")
            d_sgo = _wgrad(S['pb'], dyb, name=f"sg_w_out_grad{i}")
            duv, dws, dbs, stv = _sg_bwd(dp, S['uv'], sg_vg_full, ws_b, wst_b, bsb, name=f"sg_mix_bwd{i}")
            dh_operand, dh_name = duv, "sg_in_bwd"
            d_sgi = _wgrad(S['h1'], duv, name=f"sg_w_in_grad{i}")
            sg_small = _pack_rows([jnp.where(causal[None], dws, 0.0), jnp.sum(dbs, axis=-1)])
            d_sg_vg = stv[0:1]
            mix_grads = [scatter_item(d_sgi, 1), scatter_item(d_sgo, 0), gather_item(sg_small.astype(BF16), 0)]
        wn1 = S['wn1']
        gate = []
        if i > 0:
            (mix_sent[i],), token = _exchange_start([mix_grads], dx_mid, name=f"mix_grads_start{i}")
            wn1 = wn1 + token[0:1, 0:1]
            gate = [(saved[i - 1]['f_out'], 'mn'), (saved[i - 1]['g2'], 'n')]
        res = _mm(dh_operand, w_mix[0], tb=True, name=f"{dh_name}{i}", out_dtypes=(F32, BF16) if i > 0 else (F32,),
                  bm=256, bk=w_mix[0].shape[1], epi=_epi_norm_bwd(i > 0), n_stats=3 if i > 0 else 2,
                  extras=[(S['x_in'], 'mn'), (wn1, 'n'), (dx_mid, 'mn')] + gate)
        if i > 0:
            dx, dfb, st1 = res
            dg2_next = _stat_row(st1, 2)
        else:
            dx, st1 = res
        dsc1 = _stat_row(st1, 0) * norm1_g[i:i + 1]
        dn1g[i] = _stat_row(st1, 0) * (1.0 + S['sc1'])
        dsh1 = _stat_row(st1, 1)
        dmod[i] = jnp.concatenate([dsh1, dsc1, dg1, dsh2, dsc2, dg2], 1)
    grad_x = dx[None]

    out = {}

    def small_group(names, parts, label):
        shapes = [P[n].shape for n in names]
        w, m, v = (_pack_rows([P[pre + n] for n in names])[None] for pre in ('', 'm_', 'v_'))
        res = [_unpack_rows(t[0], shapes) for t in _adamw(w, [parts], m, v, name=label)]
        for q, n in enumerate(names):
            out[n] = tuple(r[q] for r in res)

    small.update(ada_b=jnp.concatenate(dmod, 0), norm1_g=jnp.concatenate(dn1g, 0), norm2_g=jnp.concatenate(dn2g, 0),
                 final_g=d_final_g, conv_w=jnp.stack(d_conv_w), conv_b=jnp.concatenate(d_conv_b, 0), sg_v_g=d_sg_vg)
    last_pack = _pack_rows([small[n] for n in LAST_SMALL + SMALL_SHARD])
    n_last = _pack_rows([P[n] for n in LAST_SMALL]).shape[0]
    n_pack = last_pack.shape[0]
    pack_all = _all_gather(jnp.concatenate([last_pack, loss_tile], 0)[None], 0, name="gather_small_grads")
    loss = jnp.sum(pack_all[:, n_pack, 0])
    (mix_sent[0],), last_token = _exchange_start([mix_grads], pack_all, name="mix_grads_start0")
    small_group(LAST_SMALL, pack_all[:, :n_last], "adamw_small")
    sh_rows = (n_pack - n_last) // N_DEV
    sh_parts = pack_all[:, n_last:n_pack].reshape(N_DEV, sh_rows, N_DEV, LANE)
    sh_parts = lax.dynamic_index_in_dim(sh_parts, me, 2, keepdims=False)
    sh_parts = jnp.pad(sh_parts, ((0, 0), (0, 16 - sh_rows), (0, 0)))

    def pack_shard(prefix):
        return _pad_rows(jnp.concatenate([P[prefix + n].reshape(-1, LANE) for n in SMALL_SHARD], 0), 16)[None]

    sg_, sd_, sm_, sv_ = _adamw(pack_shard(''), [sh_parts], pack_shard('m_'), pack_shard('v_'), name="adamw_channel")
    off = 0
    for n in SMALL_SHARD:
        rows = math.prod(P[n].shape) // LANE
        out[n] = tuple(t[0, off:off + rows].reshape(P[n].shape) for t in (sg_, sd_, sm_, sv_))
        off += rows

    dmod_all = pack_all[:, :DEPTH * 6 * D // LANE].reshape(N_DEV, DEPTH, 6 * D)
    dmod_cols = lax.dynamic_slice_in_dim(dmod_all, me * ncol, ncol, 2)
    g_ada = [_mm(c_pad, _pad_rows(dmod_cols[:, i], LANE), ta=True, name=f"ada_w_grad{i}")[None] for i in range(DEPTH)]

    def big(name, parts):
        res = _adamw(P[name], parts, P['m_' + name], P['v_' + name], name="adamw_" + name)
        out[name] = res
        return res[1]

    ff_parts = [_exchange_wait(ff_sent[i], last_token, name=f"ff_grads_wait{i}") for i in range(DEPTH)]
    mix_parts = [None] + [_exchange_wait(mix_sent[i], last_token, name=f"mix_grads_wait{i}") for i in range(1, DEPTH)]
    big('ada_w', g_ada)
    big('ff_w1', [p[0] for p in ff_parts])
    big('ff_w2', [p[1] for p in ff_parts])
    done = big('sg_w_in', [mix_parts[2][0]])
    mix_parts[0] = _exchange_wait(mix_sent[0], done, name="mix_grads_wait0")
    big('conv_w_in', [mix_parts[i][0] for i in range(DEPTH) if i % 3 == 0])
    row_names = ['conv_w_out', 'ssm_w_in', 'ssm_glu_w', 'ssm_w_out', 'sg_w_out']
    row_parts = ([mix_parts[i][1] for i in range(DEPTH) if i % 3 == 0] + mix_parts[1][:3] + [mix_parts[2][1]])
    small_group(S5_SMALL, mix_parts[1][3].reshape(N_DEV, -1, LANE), "adamw_s5")
    small_group(SG_SMALL, mix_parts[2][2].reshape(N_DEV, -1, LANE), "adamw_sg")
    row_w, row_m, row_v = (jnp.concatenate([P[pre + n] for n in row_names], 0) for pre in ('', 'm_', 'v_'))
    rw = _adamw(row_w, row_parts, row_m, row_v, name="adamw_row_sharded")
    off = 0
    for n in row_names:
        cnt = P[n].shape[0]
        out[n] = tuple(t[off:off + cnt] for t in rw)
        off += cnt

    return (loss, grad_x, *[out[n][0] for n in WEIGHTS], *[out[n][1] for n in WEIGHTS],
            *[out[n][2] for n in WEIGHTS], *[out[n][3] for n in WEIGHTS])
```

```python
import math

import jax
import jax.numpy as jnp
from jax import lax
from jax.experimental import pallas as pl
from jax.experimental.pallas import tpu as pltpu

F32 = jnp.float32
BF16 = jnp.bfloat16

N_DEV = 8
MESH_ID = pl.DeviceIdType.MESH
DEPTH = 4
EPS = 1e-6
S5_GROUPS, S5_GROUP, S5_STATE = 64, 16, 64
S5_LANES = S5_GROUPS * S5_STATE
S5_BLOCKS = 8
S5_CHUNK = 512
SG_HEADS, SG_CHUNK = 8, 128
LANE = 128
SUBLANE = 8
VMEM_LIMIT = 48 * 1024 * 1024
ADAM_LR, ADAM_B1, ADAM_B2, ADAM_EPS, ADAM_WD, ADAM_STEP = 0.001, 0.9, 0.999, 1e-08, 0.01, 10
GELU_C = math.sqrt(2.0 / math.pi)
GELU_A = 0.044715

WEIGHTS = ['ada_w', 'ada_b', 'norm1_g', 'norm2_g', 'ff_w1', 'ff_w2', 'final_g', 'conv_w_in', 'conv_w', 'conv_b',
           'conv_w_out', 'ssm_w_in', 'ssm_a_re', 'ssm_a_im', 'ssm_log_dt', 'ssm_b_re', 'ssm_b_im', 'ssm_c_re',
           'ssm_c_im', 'ssm_d', 'ssm_glu_w', 'ssm_glu_b', 'ssm_w_out', 'sg_w_in', 'sg_v_g', 'sg_w_s', 'sg_b_s',
           'sg_w_out']
INPUTS = ['x', 'c'] + WEIGHTS + ['loss_target'] + ['m_' + n for n in WEIGHTS] + ['v_' + n for n in WEIGHTS]
S5_SMALL = ['ssm_a_re', 'ssm_a_im', 'ssm_log_dt', 'ssm_b_re', 'ssm_b_im', 'ssm_c_re', 'ssm_c_im', 'ssm_d', 'ssm_glu_b']
SG_SMALL = ['sg_w_s', 'sg_b_s']
LAST_SMALL = ['ada_b', 'norm1_g', 'norm2_g', 'final_g']
SMALL_SHARD = ['conv_w', 'conv_b', 'sg_v_g']


def _params(*sem):
    return pltpu.CompilerParams(dimension_semantics=sem or None, vmem_limit_bytes=VMEM_LIMIT)


def _my_pos():
    return lax.axis_index("x"), lax.axis_index("y"), lax.axis_index("c")


def _my_index():
    x, y, c = _my_pos()
    return 4 * x + 2 * y + c


def _mm(a, b, *, name, ta=False, tb=False, out_dtypes=(F32,), epi=None, extras=(), a_fn=None, n_stats=0, bm=512,
        bn=1024, bk=1024):
    a_chunks = a.shape[0] if a.ndim == 3 else 0
    b_chunks = b.shape[0] if b.ndim == 3 else 0
    assert not (a_chunks and ta) and not (b_chunks and tb)
    if a_chunks:
        m, k = a.shape[1], a_chunks * a.shape[2]
        bk = k
    else:
        m, k = (a.shape[1], a.shape[0]) if ta else a.shape
    if b_chunks:
        k2, n = b.shape[1], b_chunks * b.shape[2]
        bn = min(bn, b.shape[2])
    else:
        k2, n = (b.shape[1], b.shape[0]) if tb else b.shape
    assert k == k2, (a.shape, b.shape, ta, tb)
    bm, bn, bk = min(bm, m), min(bn, n), min(bk, k)
    assert m % bm == 0 and n % bn == 0 and k % bk == 0, (m, n, k, bm, bn, bk)
    nk = k // bk
    assert nk == 1 or n_stats == 0
    n_ex, n_out = len(extras), len(out_dtypes)
    dims = (((0 if ta else 1,), (1 if tb else 0,)), ((), ()))

    def body(*refs):
        a_ref, b_ref = refs[0], refs[1]
        ex_refs = refs[2:2 + n_ex]
        out_refs = refs[2 + n_ex:2 + n_ex + n_out]

        def finish(acc):
            outs = epi(acc, *[r[...] for r in ex_refs]) if epi is not None else (acc,)
            for r, o in zip(out_refs, outs[:n_out]):
                r[...] = o.astype(r.dtype)
            if n_stats:
                st_ref = refs[2 + n_ex + n_out]

                @pl.when(pl.program_id(0) == 0)
                def _():
                    st_ref[...] = jnp.zeros_like(st_ref)

                for q, row in enumerate(outs[n_out:]):
                    st_ref[q:q + 1, :] += row

        av = jnp.concatenate([a_ref[t] for t in range(a_chunks)], axis=1) if a_chunks else a_ref[...]
        if a_fn is not None:
            av = a_fn(av)
        part = lax.dot_general(av.astype(BF16), b_ref[...].astype(BF16), dims, preferred_element_type=F32)
        if nk == 1:
            finish(part)
            return
        acc_ref = refs[-1]
        kk = pl.program_id(2)

        @pl.when(kk == 0)
        def _():
            acc_ref[...] = part

        @pl.when(kk > 0)
        def _():
            acc_ref[...] += part

        @pl.when(kk == nk - 1)
        def _():
            finish(acc_ref[...])

    if a_chunks:
        a_spec = pl.BlockSpec((a_chunks, bm, a.shape[2]), lambda i, j, q: (0, i, 0))
    elif ta:
        a_spec = pl.BlockSpec((bk, bm), lambda i, j, q: (q, i))
    else:
        a_spec = pl.BlockSpec((bm, bk), lambda i, j, q: (i, q))
    if b_chunks:
        per = b.shape[2] // bn
        b_spec = pl.BlockSpec((None, bk, bn), lambda i, j, q: (j // per, q, j % per))
    elif tb:
        b_spec = pl.BlockSpec((bn, bk), lambda i, j, q: (j, q))
    else:
        b_spec = pl.BlockSpec((bk, bn), lambda i, j, q: (q, j))
    ex_specs = []
    for arr, kind in extras:
        if kind == 'mn':
            assert arr.shape == (m, n), (arr.shape, m, n)
            ex_specs.append(pl.BlockSpec((bm, bn), lambda i, j, q: (i, j)))
        else:
            assert arr.shape == (1, n), (arr.shape, n)
            ex_specs.append(pl.BlockSpec((1, bn), lambda i, j, q: (0, j)))
    out_shape = [jax.ShapeDtypeStruct((m, n), d) for d in out_dtypes]
    out_specs = [pl.BlockSpec((bm, bn), lambda i, j, q: (i, j)) for _ in out_dtypes]
    if n_stats:
        assert n_stats <= SUBLANE
        out_shape.append(jax.ShapeDtypeStruct((SUBLANE, n), F32))
        out_specs.append(pl.BlockSpec((SUBLANE, bn), lambda i, j, q: (0, j)))
    outs = pl.pallas_call(
        body, name=name, out_shape=tuple(out_shape), grid=(m // bm, n // bn, nk),
        in_specs=[a_spec, b_spec] + ex_specs, out_specs=tuple(out_specs),
        scratch_shapes=[pltpu.VMEM((bm, bn), F32)] if nk > 1 else [],
        compiler_params=_params(*(["arbitrary"] * 3 if n_stats else ["parallel", "parallel", "arbitrary"])),
    )(a, b, *[arr for arr, _ in extras])
    return outs if len(outs) > 1 else outs[0]


def _epi_residual_norm(acc, res, gate, w, sh):
    xn = res + gate * acc
    return xn, acc, xn * _rstd(xn) * w + sh


def _epi_norm_bwd(gated):
    def epi(dh, xv, w, dres, *gate):
        rstd = _rstd(xv)
        xn = xv * rstd
        dxn = dh * w
        dx = rstd * (dxn - xn * jnp.mean(dxn * xn, axis=-1, keepdims=True)) + dres
        stats = [jnp.sum(dh * xn, axis=0, keepdims=True), jnp.sum(dh, axis=0, keepdims=True)]
        if not gated:
            return (dx, *stats)
        yv, g = gate
        return (dx, dx * g, *stats, jnp.sum(dx * yv.astype(F32), axis=0, keepdims=True))
    return epi


def _epi_loss_head(f, x_mid, g, tgt, fg):
    xv = x_mid + g * f
    rstd = _rstd(xv)
    xn = xv * rstd
    err = xn * fg - tgt
    loss = 0.5 * jnp.sum(jnp.mean(err * err, axis=-1, keepdims=True))
    dout = err * (1.0 / xv.shape[-1])
    dxn = dout * fg
    dx = rstd * (dxn - xn * jnp.mean(dxn * xn, axis=-1, keepdims=True))
    return (dx, dx * g, jnp.full((1, xv.shape[-1]), loss, F32), jnp.sum(dout * xn, axis=0, keepdims=True),
            jnp.sum(dx * f, axis=0, keepdims=True))


def _epi_glu_bwd(dy2, yv, t):
    sig = 1.0 / (1.0 + jnp.exp(-t))
    dt = dy2 * yv * sig * (1.0 - sig)
    return dt, dy2 * sig, jnp.sum(dt, axis=0, keepdims=True)


def _wgrad(acts, cots, *, name, a_fn=None, bm=1024, bn=512):
    return _mm(acts, cots, ta=True, name=name, out_dtypes=(BF16,), a_fn=a_fn, bm=bm, bn=bn, bk=acts.shape[0])


def _square(a):
    af = a.astype(F32)
    return af * af


def _rstd(xv):
    return lax.rsqrt(jnp.mean(xv * xv, axis=-1, keepdims=True) + EPS)


def _normmod_fwd(x, w, sh, *, name, tm=512):
    L, D = x.shape

    def body(x_ref, w_ref, s_ref, h_ref):
        xv = x_ref[...]
        h_ref[...] = (xv * _rstd(xv) * w_ref[...] + s_ref[...]).astype(h_ref.dtype)

    row = pl.BlockSpec((tm, D), lambda i: (i, 0))
    vec = pl.BlockSpec((1, D), lambda i: (0, 0))
    return pl.pallas_call(body, name=name, out_shape=jax.ShapeDtypeStruct((L, D), BF16), grid=(L // tm,),
                          in_specs=[row, vec, vec], out_specs=row, compiler_params=_params("parallel"))(x, w, sh)


def _shift_down(v, k):
    row = lax.broadcasted_iota(jnp.int32, v.shape, 0)
    return jnp.where(row >= k, pltpu.roll(v, k, 0), 0.0)


def _shift_up(v, k):
    n = v.shape[0]
    row = lax.broadcasted_iota(jnp.int32, v.shape, 0)
    return jnp.where(row < n - k, pltpu.roll(v, n - k, 0), 0.0)


def _conv_views(L, D):
    return [pl.BlockSpec((L, LANE), lambda j, s=s: (0, s * (D // LANE) + j)) for s in range(3)]


def _conv_fwd(bcx, wb, *, name):
    L, D = bcx.shape[0], bcx.shape[1] // 3

    def body(b_ref, c_ref, x_ref, wb_ref, p_ref):
        z = c_ref[...].astype(F32) * x_ref[...].astype(F32)
        conv = (wb_ref[0:1, :] * _shift_down(z, 2) + wb_ref[1:2, :] * _shift_down(z, 1)
                + wb_ref[2:3, :] * z + wb_ref[3:4, :])
        p_ref[...] = (b_ref[...].astype(F32) * conv).astype(p_ref.dtype)

    col = pl.BlockSpec((L, LANE), lambda j: (0, j))
    return pl.pallas_call(body, name=name, out_shape=jax.ShapeDtypeStruct((L, D), BF16), grid=(D // LANE,),
                          in_specs=_conv_views(L, D) + [pl.BlockSpec((SUBLANE, LANE), lambda j: (0, j))],
                          out_specs=col, compiler_params=_params("parallel"))(bcx, bcx, bcx, wb)


def _conv_bwd(dp, bcx, wb, *, name):
    L, D = dp.shape

    def body(dp_ref, b_ref, c_ref, x_ref, wb_ref, d3_ref, st_ref):
        cv, xv = c_ref[...].astype(F32), x_ref[...].astype(F32)
        z = cv * xv
        z1, z2 = _shift_down(z, 1), _shift_down(z, 2)
        w0, w1, w2 = wb_ref[0:1, :], wb_ref[1:2, :], wb_ref[2:3, :]
        conv = w0 * z2 + w1 * z1 + w2 * z + wb_ref[3:4, :]
        dpv = dp_ref[...].astype(F32)
        d3_ref[0] = (dpv * conv).astype(d3_ref.dtype)
        dconv = dpv * b_ref[...].astype(F32)
        dz = w2 * dconv + w1 * _shift_up(dconv, 1) + w0 * _shift_up(dconv, 2)
        d3_ref[1] = (dz * xv).astype(d3_ref.dtype)
        d3_ref[2] = (dz * cv).astype(d3_ref.dtype)
        st_ref[...] = jnp.zeros_like(st_ref)
        st_ref[0:1, :] = jnp.sum(dconv * z2, axis=0, keepdims=True)
        st_ref[1:2, :] = jnp.sum(dconv * z1, axis=0, keepdims=True)
        st_ref[2:3, :] = jnp.sum(dconv * z, axis=0, keepdims=True)
        st_ref[3:4, :] = jnp.sum(dconv, axis=0, keepdims=True)

    col = pl.BlockSpec((L, LANE), lambda j: (0, j))
    vec = pl.BlockSpec((SUBLANE, LANE), lambda j: (0, j))
    return pl.pallas_call(body, name=name,
                          out_shape=(jax.ShapeDtypeStruct((3, L, D), BF16), jax.ShapeDtypeStruct((SUBLANE, D), F32)),
                          grid=(D // LANE,), in_specs=[col] + _conv_views(L, D) + [vec],
                          out_specs=(pl.BlockSpec((3, L, LANE), lambda j: (0, 0, j)), vec),
                          compiler_params=_params("parallel"))(dp, bcx, bcx, bcx, wb)


def _sg_fwd(uv, vg, ws, bsb, *, name, tr=512):
    L, D = uv.shape[0], uv.shape[1] // 2

    def body(uv_ref, vg_ref, ws_ref, bsb_ref, p_ref):
        for ci in range(tr // SG_CHUNK):
            rows = slice(ci * SG_CHUNK, (ci + 1) * SG_CHUNK)
            v = uv_ref[rows, D:2 * D]
            vn = (v * _rstd(v) * vg_ref[...]).astype(BF16)
            for h in range(SG_HEADS):
                cols = slice(h * LANE, (h + 1) * LANE)
                vm = jnp.dot(ws_ref[h], vn[:, cols], preferred_element_type=F32) + bsb_ref[h]
                p_ref[rows, cols] = (uv_ref[rows, cols] * vm).astype(p_ref.dtype)

    full3 = pl.BlockSpec((SG_HEADS, SG_CHUNK, LANE), lambda i: (0, 0, 0))
    return pl.pallas_call(body, name=name, out_shape=jax.ShapeDtypeStruct((L, D), BF16), grid=(L // tr,),
                          in_specs=[pl.BlockSpec((tr, 2 * D), lambda i: (i, 0)), pl.BlockSpec((1, D), lambda i: (0, 0)),
                                    full3, full3],
                          out_specs=pl.BlockSpec((tr, D), lambda i: (i, 0)),
                          compiler_params=_params("parallel"))(uv, vg, ws, bsb)


def _sg_bwd(dp, uv, vg, ws, wst, bsb, *, name, tr=512):
    L, D = dp.shape

    def body(dp_ref, uv_ref, vg_ref, ws_ref, wst_ref, bsb_ref, duv_ref, dws_ref, dbs_ref, st_ref, dvn_ref):
        i = pl.program_id(0)

        @pl.when(i == 0)
        def _():
            dws_ref[...] = jnp.zeros_like(dws_ref)
            dbs_ref[...] = jnp.zeros_like(dbs_ref)
            st_ref[...] = jnp.zeros_like(st_ref)

        for ci in range(tr // SG_CHUNK):
            rows = slice(ci * SG_CHUNK, (ci + 1) * SG_CHUNK)
            v = uv_ref[rows, D:2 * D]
            rstd = _rstd(v)
            vhat = v * rstd
            vn = (vhat * vg_ref[...]).astype(BF16)
            for h in range(SG_HEADS):
                cols = slice(h * LANE, (h + 1) * LANE)
                vm = jnp.dot(ws_ref[h], vn[:, cols], preferred_element_type=F32) + bsb_ref[h]
                dph = dp_ref[rows, cols]
                duv_ref[rows, cols] = (dph * vm).astype(duv_ref.dtype)
                dvm = dph * uv_ref[rows, cols]
                dbs_ref[h] += dvm
                dvmb = dvm.astype(BF16)
                dws_ref[h] += lax.dot_general(dvmb, vn[:, cols], (((1,), (1,)), ((), ())),
                                              preferred_element_type=F32)
                dvn_ref[rows, cols] = jnp.dot(wst_ref[h], dvmb, preferred_element_type=F32)
            dvn = dvn_ref[rows, :]
            gv = dvn * vg_ref[...]
            dv = rstd * (gv - vhat * jnp.mean(gv * vhat, axis=-1, keepdims=True))
            duv_ref[rows, D:2 * D] = dv.astype(duv_ref.dtype)
            st_ref[0:1, :] += jnp.sum(dvn * vhat, axis=0, keepdims=True)

    full3 = pl.BlockSpec((SG_HEADS, SG_CHUNK, LANE), lambda i: (0, 0, 0))
    acc3 = jax.ShapeDtypeStruct((SG_HEADS, SG_CHUNK, LANE), F32)
    return pl.pallas_call(
        body, name=name,
        out_shape=(jax.ShapeDtypeStruct((L, 2 * D), BF16), acc3, acc3, jax.ShapeDtypeStruct((SUBLANE, D), F32)),
        grid=(L // tr,),
        in_specs=[pl.BlockSpec((tr, D), lambda i: (i, 0)), pl.BlockSpec((tr, 2 * D), lambda i: (i, 0)),
                  pl.BlockSpec((1, D), lambda i: (0, 0)), full3, full3, full3],
        out_specs=(pl.BlockSpec((tr, 2 * D), lambda i: (i, 0)), full3, full3,
                   pl.BlockSpec((SUBLANE, D), lambda i: (0, 0))),
        scratch_shapes=[pltpu.VMEM((tr, D), F32)],
        compiler_params=_params("arbitrary"))(dp, uv, vg, ws, wst, bsb)


def _gelu(x):
    return 0.5 * x * (1.0 + jnp.tanh(GELU_C * (x + GELU_A * x * x * x)))


def _gelu_grad(x):
    th = jnp.tanh(GELU_C * (x + GELU_A * x * x * x))
    return 0.5 * (1.0 + th) + 0.5 * x * (1.0 - th * th) * GELU_C * (1.0 + 3.0 * GELU_A * x * x)


def _cmul_add(xr, xi, ar, ai, br, bi):
    return xr + ar * br - ai * bi, xi + ar * bi + ai * br


def _cmul_conj_add(xr, xi, ar, ai, br, bi):
    return xr + ar * br + ai * bi, xi + ar * bi - ai * br


def _to_subchunk_order(src_ref, dst_ref, n):
    for k in range(n):
        dst_ref[pl.ds(SUBLANE * k, SUBLANE), :] = src_ref[pl.ds(k, SUBLANE, stride=n), :].astype(dst_ref.dtype)


def _to_time_order(src_ref, dst_ref, n):
    for m in range(n):
        r, k = divmod(SUBLANE * m, n)
        dst_ref[pl.ds(SUBLANE * m, SUBLANE), :] = src_ref[pl.ds(SUBLANE * k + r, SUBLANE, stride=SUBLANE), :]


def _s5_fwd(u, bre, bim, cre, cim, pw, pos, dsk, *, name, tc=S5_CHUNK):
    L, D = u.shape
    W = S5_LANES // S5_BLOCKS
    nt = L // tc
    n = tc // SUBLANE

    def sub(k):
        return pl.ds(SUBLANE * k, SUBLANE)

    def body(u_ref, bre_ref, bim_ref, cre_ref, cim_ref, pw_ref, pos_ref, d_ref, sre_ref, sim_ref, ypre_ref, yg_ref,
             carry, up, yp):
        t = pl.program_id(1)

        @pl.when(t == 0)
        def _():
            carry[...] = jnp.zeros_like(carry)

        _to_subchunk_order(u_ref, up, n)
        uv = up[...]
        ub = uv.astype(BF16)
        sre_ref[...] = jnp.dot(ub, bre_ref[...], preferred_element_type=F32)
        sim_ref[...] = jnp.dot(ub, bim_ref[...], preferred_element_type=F32)

        ar, ai = pw_ref[8], pw_ref[9]
        xr = jnp.zeros((SUBLANE, W), F32)
        xi = jnp.zeros((SUBLANE, W), F32)
        for k in range(n):
            xr, xi = _cmul_add(sre_ref[sub(k), :], sim_ref[sub(k), :], ar, ai, xr, xi)
            sre_ref[sub(k), :] = xr
            sim_ref[sub(k), :] = xi
        for q, d in enumerate((1, 2, 4)):
            xr, xi = _cmul_add(xr, xi, pw_ref[2 * q], pw_ref[2 * q + 1], pltpu.roll(xr, d, 0), pltpu.roll(xi, d, 0))
        cr, ci = carry[0], carry[1]
        xr, xi = _cmul_add(xr, xi, pw_ref[6], pw_ref[7], cr, ci)
        first = lax.broadcasted_iota(jnp.int32, (SUBLANE, W), 0) == 0
        er = jnp.where(first, cr, pltpu.roll(xr, 1, 0))
        ei = jnp.where(first, ci, pltpu.roll(xi, 1, 0))
        last = slice(SUBLANE - 1, SUBLANE)
        carry[0] = jnp.broadcast_to(xr[last, :], (SUBLANE, W))
        carry[1] = jnp.broadcast_to(xi[last, :], (SUBLANE, W))
        for k in range(n):
            sr, si = _cmul_add(sre_ref[sub(k), :], sim_ref[sub(k), :], pos_ref[0, k:k + 1, :], pos_ref[1, k:k + 1, :],
                               er, ei)
            sre_ref[sub(k), :] = sr
            sim_ref[sub(k), :] = si
        yp[...] = (jnp.dot(sre_ref[...].astype(BF16), cre_ref[...], preferred_element_type=F32)
                   - jnp.dot(sim_ref[...].astype(BF16), cim_ref[...], preferred_element_type=F32) + d_ref[...] * uv)
        _to_time_order(yp, ypre_ref, n)
        yg_ref[...] = _gelu(ypre_ref[...])

    ch = pl.BlockSpec((tc, LANE), lambda j, t: (t, j))
    st = pl.BlockSpec((tc, W), lambda j, t: (t, j))
    bsp = pl.BlockSpec((None, LANE, W), lambda j, t: (j, 0, 0))
    csp = pl.BlockSpec((None, W, LANE), lambda j, t: (j, 0, 0))
    return pl.pallas_call(
        body, name=name,
        out_shape=(jax.ShapeDtypeStruct((L, S5_LANES), F32), jax.ShapeDtypeStruct((L, S5_LANES), F32),
                   jax.ShapeDtypeStruct((L, D), F32), jax.ShapeDtypeStruct((L, D), F32)),
        grid=(S5_BLOCKS, nt),
        in_specs=[ch, bsp, bsp, csp, csp, pl.BlockSpec((10, SUBLANE, W), lambda j, t: (0, 0, j)),
                  pl.BlockSpec((2, n, W), lambda j, t: (0, 0, j)), pl.BlockSpec((1, LANE), lambda j, t: (0, j))],
        out_specs=(st, st, ch, ch),
        scratch_shapes=[pltpu.VMEM((2, SUBLANE, W), F32), pltpu.VMEM((tc, LANE), F32), pltpu.VMEM((tc, LANE), F32)],
        compiler_params=_params("parallel", "arbitrary"))(u, bre, bim, cre, cim, pw, pos, dsk)


def _s5_bwd(dy, u, sre, sim, bre, bim, cre, cim, pwr, posr, dsk, *, name, tc=S5_CHUNK):
    L, D = u.shape
    W = S5_LANES // S5_BLOCKS
    nt = L // tc
    n = tc // SUBLANE
    nt_dims = (((1,), (1,)), ((), ()))
    tn_dims = (((0,), (0,)), ((), ()))

    def sub(k):
        return pl.ds(SUBLANE * k, SUBLANE)

    def body(dy_ref, u_ref, sre_ref, sim_ref, bre_ref, bim_ref, cre_ref, cim_ref, pw_ref, pos_ref, d_ref,
             du_ref, dbre_ref, dbim_ref, dcre_ref, dcim_ref, ga_ref, dd_ref, gre, gim, carry, gacc, up, dyp):
        t = pl.program_id(1)

        @pl.when(t == 0)
        def _():
            for r in (carry, gacc, dbre_ref, dbim_ref, dcre_ref, dcim_ref, ga_ref, dd_ref):
                r[...] = jnp.zeros_like(r)

        _to_subchunk_order(dy_ref, dyp, n)
        _to_subchunk_order(u_ref, up, n)
        dyv, uv = dyp[...], up[...]
        dyb, ub = dyv.astype(BF16), uv.astype(BF16)
        gre[...] = lax.dot_general(dyb, cre_ref[...], nt_dims, preferred_element_type=F32)
        gim[...] = -lax.dot_general(dyb, cim_ref[...], nt_dims, preferred_element_type=F32)
        br, bi = pw_ref[8], pw_ref[9]
        xr = jnp.zeros((SUBLANE, W), F32)
        xi = jnp.zeros((SUBLANE, W), F32)
        for k in reversed(range(n)):
            xr, xi = _cmul_add(gre[sub(k), :], gim[sub(k), :], br, bi, xr, xi)
            gre[sub(k), :] = xr
            gim[sub(k), :] = xi
        for q, d in enumerate((1, 2, 4)):
            xr, xi = _cmul_add(xr, xi, pw_ref[2 * q], pw_ref[2 * q + 1], pltpu.roll(xr, SUBLANE - d, 0),
                               pltpu.roll(xi, SUBLANE - d, 0))
        cr, ci = carry[0], carry[1]
        xr, xi = _cmul_add(xr, xi, pw_ref[6], pw_ref[7], cr, ci)
        top = lax.broadcasted_iota(jnp.int32, (SUBLANE, W), 0) == SUBLANE - 1
        er = jnp.where(top, cr, pltpu.roll(xr, SUBLANE - 1, 0))
        ei = jnp.where(top, ci, pltpu.roll(xi, SUBLANE - 1, 0))
        carry[0] = jnp.broadcast_to(xr[0:1, :], (SUBLANE, W))
        carry[1] = jnp.broadcast_to(xi[0:1, :], (SUBLANE, W))
        nr, ni = er, ei
        acc_r = jnp.zeros((SUBLANE, W), F32)
        acc_i = jnp.zeros((SUBLANE, W), F32)
        for k in reversed(range(n)):
            place = slice(n - 1 - k, n - k)
            gr, gi = _cmul_conj_add(gre[sub(k), :], gim[sub(k), :], pos_ref[0, place, :], pos_ref[1, place, :], er, ei)
            gre[sub(k), :] = gr
            gim[sub(k), :] = gi
            sr, si = sre_ref[sub(k), :], sim_ref[sub(k), :]
            acc_r = acc_r + sr * nr + si * ni
            acc_i = acc_i + sr * ni - si * nr
            nr, ni = gr, gi
        gacc[0] += acc_r
        gacc[1] += acc_i
        grb, gib = gre[...].astype(BF16), gim[...].astype(BF16)
        dyp[...] = (lax.dot_general(grb, bre_ref[...], nt_dims, preferred_element_type=F32)
                    + lax.dot_general(gib, bim_ref[...], nt_dims, preferred_element_type=F32) + d_ref[...] * dyv)
        _to_time_order(dyp, up, n)
        du_ref[...] = up[...].astype(du_ref.dtype)
        dbre_ref[...] += lax.dot_general(ub, grb, tn_dims, preferred_element_type=F32)
        dbim_ref[...] += lax.dot_general(ub, gib, tn_dims, preferred_element_type=F32)
        dcre_ref[...] += lax.dot_general(sre_ref[...].astype(BF16), dyb, tn_dims, preferred_element_type=F32)
        dcim_ref[...] -= lax.dot_general(sim_ref[...].astype(BF16), dyb, tn_dims, preferred_element_type=F32)
        dd_ref[0:1, :] += jnp.sum(dyv * uv, axis=0, keepdims=True)

        @pl.when(t == nt - 1)
        def _():
            ga_ref[0:1, :] = jnp.sum(gacc[0], axis=0, keepdims=True)
            ga_ref[1:2, :] = jnp.sum(gacc[1], axis=0, keepdims=True)

    ch = pl.BlockSpec((tc, LANE), lambda j, t: (nt - 1 - t, j))
    st = pl.BlockSpec((tc, W), lambda j, t: (nt - 1 - t, j))
    bsp = pl.BlockSpec((None, LANE, W), lambda j, t: (j, 0, 0))
    csp = pl.BlockSpec((None, W, LANE), lambda j, t: (j, 0, 0))
    return pl.pallas_call(
        body, name=name,
        out_shape=(jax.ShapeDtypeStruct((L, D), BF16),
                   jax.ShapeDtypeStruct((S5_BLOCKS, LANE, W), F32), jax.ShapeDtypeStruct((S5_BLOCKS, LANE, W), F32),
                   jax.ShapeDtypeStruct((S5_BLOCKS, W, LANE), F32), jax.ShapeDtypeStruct((S5_BLOCKS, W, LANE), F32),
                   jax.ShapeDtypeStruct((SUBLANE, S5_LANES), F32), jax.ShapeDtypeStruct((SUBLANE, D), F32)),
        grid=(S5_BLOCKS, nt),
        in_specs=[ch, ch, st, st, bsp, bsp, csp, csp, pl.BlockSpec((10, SUBLANE, W), lambda j, t: (0, 0, j)),
                  pl.BlockSpec((2, n, W), lambda j, t: (0, 0, j)), pl.BlockSpec((1, LANE), lambda j, t: (0, j))],
        out_specs=(ch, bsp, bsp, csp, csp, pl.BlockSpec((SUBLANE, W), lambda j, t: (0, j)),
                   pl.BlockSpec((SUBLANE, LANE), lambda j, t: (0, j))),
        scratch_shapes=[pltpu.VMEM((tc, W), F32), pltpu.VMEM((tc, W), F32), pltpu.VMEM((2, SUBLANE, W), F32),
                        pltpu.VMEM((2, SUBLANE, W), F32), pltpu.VMEM((tc, LANE), F32), pltpu.VMEM((tc, LANE), F32)],
        compiler_params=_params("parallel", "arbitrary"))(dy, u, sre, sim, bre, bim, cre, cim, pwr, posr, dsk)


def _s5_prep(a_re, a_im, log_dt, b_re, b_im, c_re, c_im):
    dt = jnp.exp(log_dt)[:, None]
    mag = jnp.exp(a_re * dt)
    abar_re = mag * jnp.cos(a_im * dt)
    abar_im = mag * jnp.sin(a_im * dt)
    den = a_re * a_re + a_im * a_im
    nr = abar_re - 1.0
    ni = abar_im
    f_re = ((nr * a_re + ni * a_im) / den)[..., None]
    f_im = ((ni * a_re - nr * a_im) / den)[..., None]
    bbar_re = f_re * b_re - f_im * b_im
    bbar_im = f_re * b_im + f_im * b_re
    eye = jnp.eye(S5_GROUPS // S5_BLOCKS, dtype=F32)
    gb = S5_GROUPS // S5_BLOCKS

    def blk_b(bb):
        t = bb.reshape(S5_BLOCKS, gb, S5_STATE, S5_GROUP)
        return jnp.einsum('jgph,gk->jghkp', t, eye).reshape(S5_BLOCKS, gb * S5_GROUP, gb * S5_STATE)

    def blk_c(cc):
        t = cc.reshape(S5_BLOCKS, gb, S5_GROUP, S5_STATE)
        return jnp.einsum('jghp,gk->jgpkh', t, eye).reshape(S5_BLOCKS, gb * S5_STATE, gb * S5_GROUP)

    return (abar_re.reshape(1, S5_LANES), abar_im.reshape(1, S5_LANES), blk_b(bbar_re), blk_b(bbar_im),
            blk_c(c_re), blk_c(c_im))


def _cpowers(ar, ai, count):
    pr, pi, m = ar, ai, 1
    while m < count:
        tr, ti = pr[m - 1:m], pi[m - 1:m]
        pr, pi = jnp.concatenate([pr, pr * tr - pi * ti], 0), jnp.concatenate([pi, pr * ti + pi * tr], 0)
        m *= 2
    return pr, pi


def _s5_power_tables(ar, ai, n):
    pr, pi = _cpowers(ar, ai, n)
    qr, qi = _cpowers(pr[n - 1:n], pi[n - 1:n], SUBLANE)
    row = jnp.arange(SUBLANE)[:, None]
    lanes = ar.shape[1]

    def tables(sign, keep, order):
        out = []
        for d in (1, 2, 4):
            out += [jnp.where(keep(d), qr[d - 1:d], 0.0), jnp.where(keep(d), sign * qi[d - 1:d], 0.0)]
        out += [jnp.concatenate([qr[r:r + 1] for r in order], 0), sign * jnp.concatenate([qi[r:r + 1] for r in order], 0),
                ar, sign * ai]
        return jnp.stack([jnp.broadcast_to(o, (SUBLANE, lanes)) for o in out])

    fwd = tables(1.0, lambda d: row >= d, list(range(SUBLANE)))
    rev = tables(-1.0, lambda d: row + d <= SUBLANE - 1, list(reversed(range(SUBLANE))))
    return fwd, rev, jnp.stack([pr, pi])


ADAMW_PART_BLOCK_BYTES = 2 * 1024 * 1024


def _adamw(w, parts, m, v, *, name):
    n, R, C = w.shape
    assert len(parts) == n
    P = parts[0].shape[0]
    tr = R
    while P * tr * C * parts[0].dtype.itemsize > ADAMW_PART_BLOCK_BYTES and tr % 16 == 0:
        tr //= 2
    c1 = 1.0 / (1.0 - ADAM_B1 ** ADAM_STEP)
    c2 = 1.0 / (1.0 - ADAM_B2 ** ADAM_STEP)

    def body(*refs):
        w_ref, m_ref, v_ref = refs[:3]
        p_refs = refs[3:3 + n]
        g_ref, d_ref, nm_ref, nv_ref = refs[3 + n:]
        layer = pl.program_id(0)
        for q, p_ref in enumerate(p_refs):
            @pl.when(layer == q)
            def _(p_ref=p_ref):
                g = p_ref[0].astype(F32)
                for s in range(1, P):
                    g = g + p_ref[s].astype(F32)
                nm = ADAM_B1 * m_ref[...] + (1.0 - ADAM_B1) * g
                nv = ADAM_B2 * v_ref[...] + (1.0 - ADAM_B2) * (g * g)
                g_ref[...] = g
                nm_ref[...] = nm
                nv_ref[...] = nv
                d_ref[...] = -ADAM_LR * ((nm * c1) / (jnp.sqrt(nv * c2) + ADAM_EPS) + ADAM_WD * w_ref[...])

    row = pl.BlockSpec((None, tr, C), lambda l, i: (l, i, 0))
    part_specs = [pl.BlockSpec((P, tr, C), lambda l, i, q=q: (0, jnp.where(l == q, i, 0), 0)) for q in range(n)]
    out = jax.ShapeDtypeStruct((n, R, C), F32)
    return pl.pallas_call(body, name=name, out_shape=(out, out, out, out), grid=(n, R // tr),
                          in_specs=[row, row, row] + part_specs, out_specs=(row, row, row, row),
                          compiler_params=_params("arbitrary", "arbitrary"))(w, m, v, *parts)


def _all_gather(xs, axis, *, name):
    m = xs.shape[axis]
    out_shape = list(xs.shape)
    out_shape[axis] = N_DEV * m

    def body(x_ref, out_ref, send_sems, recv_sems, local_sem):
        x, y, c = _my_pos()
        me, sibling = (x, y, c), (x, y, 1 - c)
        chips = [(1 - x, y), (x, 1 - y), (1 - x, 1 - y)]

        def blk(px, py, pc):
            idx = [slice(None)] * 3
            idx[axis] = pl.ds((4 * px + 2 * py + pc) * m, m)
            return out_ref.at[tuple(idx)]

        def copy(k, block, to, src=None):
            return pltpu.make_async_remote_copy(src_ref=blk(*block) if src is None else src, dst_ref=blk(*block),
                                                send_sem=send_sems.at[k], recv_sem=recv_sems.at[k],
                                                device_id=to, device_id_type=MESH_ID)

        mine = pltpu.make_async_copy(x_ref, blk(*me), local_sem)
        mine.start()
        first = [copy(0, me, sibling, src=x_ref)]
        first += [copy(1 + j, me, (*chip, c), src=x_ref) for j, chip in enumerate(chips)]
        for cp in first:
            cp.start()
        passed = [copy(4 + j, (*chip, c), sibling) for j, chip in enumerate(chips)]
        for j, chip in enumerate(chips):
            copy(1 + j, (*chip, c), me).wait_recv()
            passed[j].start()
        copy(0, sibling, me).wait_recv()
        for j, chip in enumerate(chips):
            copy(4 + j, (*chip, 1 - c), me).wait_recv()
        for cp in first + passed:
            cp.wait_send()
        mine.wait()

    hbm = pl.BlockSpec(memory_space=pl.ANY)
    return pl.pallas_call(body, name=name, out_shape=jax.ShapeDtypeStruct(tuple(out_shape), xs.dtype),
                          in_specs=[hbm], out_specs=hbm,
                          scratch_shapes=[pltpu.SemaphoreType.DMA((N_DEV - 1,)), pltpu.SemaphoreType.DMA((N_DEV - 1,)),
                                          pltpu.SemaphoreType.DMA],
                          compiler_params=pltpu.CompilerParams(has_side_effects=True))(xs)


NEAR_PEERS = (1, 2, 4, 6)
RELAY_PEERS = (2, 4, 6)


def _block(ref, axis, idx, m):
    return ref.at[pl.ds(idx * m, m), :] if axis == 0 else ref.at[:, pl.ds(idx * m, m)]


def _exchange_copies(metas, src_refs, zone_refs, send_sems, recv_sems, base, phase):
    x, y, c = _my_pos()
    me = 4 * x + 2 * y + c

    def place(r):
        pos = (1 - x if r & 4 else x, 1 - y if r & 2 else y, 1 - c if r & 1 else c)
        return pos, 4 * pos[0] + 2 * pos[1] + pos[2]

    def copies(r, to, src, dst, arrival):
        return tuple(pltpu.make_async_remote_copy(src_ref=src, dst_ref=d, send_sem=send_sems.at[base + r - 1],
                                                  recv_sem=recv_sems.at[base + r - 1], device_id=to,
                                                  device_id_type=MESH_ID) for d in (dst, arrival))

    pairs, own = [], []
    if phase == 'relay':
        sibling, _ = place(1)
        for r in RELAY_PEERS:
            held, comes = place(r)[1], place(r | 1)[1]
            for (kind, axis, m), z_ref in zip(metas, zone_refs):
                pairs.append(copies(r, sibling, _block(z_ref, axis, held, m), _block(z_ref, axis, held, m),
                                    _block(z_ref, axis, comes, m)))
        return pairs, own
    for r in (NEAR_PEERS if phase == 'near' else range(1, N_DEV)):
        pos, peer = place(r)
        for (kind, axis, m), s_ref, z_ref in zip(metas, src_refs, zone_refs):
            if kind == 'gather':
                pairs.append(copies(r, pos, s_ref, _block(z_ref, axis, me, m), _block(z_ref, axis, peer, m)))
            else:
                pairs.append(copies(r, pos, _block(s_ref, axis, peer, m), z_ref.at[me], z_ref.at[peer]))
    for (kind, axis, m), s_ref, z_ref in zip(metas, src_refs, zone_refs):
        src, dst = (s_ref, _block(z_ref, axis, me, m)) if kind == 'gather' else (_block(s_ref, axis, me, m), z_ref.at[me])
        own.append(pltpu.make_async_copy(src, dst, recv_sems.at[base + N_DEV - 1]))
    return pairs, own


def _exchange_start(groups, after, *, name, relayed=False):
    flat = [it for g in groups for it in g]
    n, ng = len(flat), len(groups)
    metas = [it[2] for it in flat]
    bounds = [(sum(len(g) for g in groups[:q]), sum(len(g) for g in groups[:q + 1])) for q in range(ng)]
    phase = 'near' if relayed else 'all'

    def body(*refs):
        src_refs = refs[:n]
        send_sems, recv_sems = refs[n + 1], refs[n + 2]
        zone_refs = refs[2 * n + 3:3 * n + 3]
        token = refs[-1]
        for q, (lo, hi) in enumerate(bounds):
            pairs, own = _exchange_copies(metas[lo:hi], src_refs[lo:hi], zone_refs[lo:hi], send_sems, recv_sems,
                                          q * N_DEV, phase)
            for outgoing, _ in pairs:
                outgoing.start()
            for cp in own:
                cp.start()
        token[...] = jnp.zeros_like(token)

    hbm = pl.BlockSpec(memory_space=pltpu.HBM)
    sem = pl.BlockSpec(memory_space=pltpu.SEMAPHORE)
    srcs = [it[0] for it in flat]
    res = pl.pallas_call(
        body, name=name,
        out_shape=(pltpu.SemaphoreType.DMA((ng * N_DEV,)), pltpu.SemaphoreType.DMA((ng * N_DEV,)),
                   *[pltpu.HBM(a.shape, a.dtype) for a in srcs], *[pltpu.HBM(it[1], it[0].dtype) for it in flat],
                   jax.ShapeDtypeStruct((SUBLANE, LANE), F32)),
        in_specs=[hbm] * n + [pl.BlockSpec(memory_space=pl.ANY)],
        out_specs=(sem, sem, *[hbm] * (2 * n), pl.BlockSpec(memory_space=pltpu.VMEM)),
        input_output_aliases={q: 2 + q for q in range(n)},
        compiler_params=pltpu.CompilerParams(has_side_effects=pltpu.SideEffectType.DATAFLOW_SIDE_EFFECTING),
    )(*[pltpu.with_memory_space_constraint(a, pltpu.HBM) for a in srcs], after)
    handles = [(res[0], res[1], q * N_DEV, phase, list(res[2 + lo:2 + hi]), list(res[2 + n + lo:2 + n + hi]),
                metas[lo:hi]) for q, (lo, hi) in enumerate(bounds)]
    return handles, res[-1]


def _exchange_wait(handle, after, *, name):
    send_sems, recv_sems, base, phase, srcs, zones, metas = handle
    ns, nz = len(srcs), len(zones)

    def body(*refs):
        src_refs, zone_refs = refs[:ns], refs[ns:ns + nz]
        s_sems, r_sems = refs[ns + nz], refs[ns + nz + 1]
        pairs, own = _exchange_copies(metas, src_refs, zone_refs, s_sems, r_sems, base, phase)
        for outgoing, incoming in pairs:
            outgoing.wait_send()
            incoming.wait_recv()
        for cp in own:
            cp.wait()

    hbm = pl.BlockSpec(memory_space=pltpu.HBM)
    sem = pl.BlockSpec(memory_space=pltpu.SEMAPHORE)
    arrays = srcs + zones
    res = pl.pallas_call(
        body, name=name,
        out_shape=tuple(pltpu.HBM(a.shape, a.dtype) for a in arrays),
        in_specs=[hbm] * (ns + nz) + [sem, sem, pl.BlockSpec(memory_space=pl.ANY)],
        out_specs=tuple([hbm] * (ns + nz)),
        input_output_aliases={q: q for q in range(ns + nz)},
        compiler_params=pltpu.CompilerParams(has_side_effects=pltpu.SideEffectType.DATAFLOW_SIDE_EFFECTING),
    )(*arrays, send_sems, recv_sems, after)
    return list(res[ns:])


def _exchange_relay(handle, after, *, name):
    metas = handle[6]
    zones = _exchange_wait(handle, after, name=name + "_in")
    nz = len(zones)

    def body(*refs):
        zone_refs = refs[:nz]
        send_sems, recv_sems = refs[nz], refs[nz + 1]
        pairs, _ = _exchange_copies(metas, (), zone_refs, send_sems, recv_sems, 0, 'relay')
        for outgoing, _ in pairs:
            outgoing.start()
        refs[-1][...] = jnp.zeros_like(refs[-1])

    hbm = pl.BlockSpec(memory_space=pltpu.HBM)
    sem = pl.BlockSpec(memory_space=pltpu.SEMAPHORE)
    res = pl.pallas_call(
        body, name=name + "_out",
        out_shape=(pltpu.SemaphoreType.DMA((N_DEV,)), pltpu.SemaphoreType.DMA((N_DEV,)),
                   *[pltpu.HBM(z.shape, z.dtype) for z in zones], jax.ShapeDtypeStruct((SUBLANE, LANE), F32)),
        in_specs=[hbm] * nz, out_specs=(sem, sem, *[hbm] * nz, pl.BlockSpec(memory_space=pltpu.VMEM)),
        input_output_aliases={q: 2 + q for q in range(nz)},
        compiler_params=pltpu.CompilerParams(has_side_effects=pltpu.SideEffectType.DATAFLOW_SIDE_EFFECTING),
    )(*zones)
    return (res[0], res[1], 0, 'relay', [], list(res[2:2 + nz]), metas), res[-1][0:1, 0:1]


def _pad_rows(a, rows):
    return jnp.pad(a, ((0, rows - a.shape[0]), (0, 0)))


PACK_ROWS = 2 * SUBLANE


def _rows(a):
    flat = a.reshape(-1).astype(F32)
    pad = -flat.shape[0] % (PACK_ROWS * LANE)
    return (jnp.pad(flat, (0, pad)) if pad else flat).reshape(-1, LANE)


def _pack_rows(arrays):
    return jnp.concatenate([_rows(a) for a in arrays], 0)


def _unpack_rows(t, shapes):
    out, off = [], 0
    for shp in shapes:
        size = math.prod(shp)
        rows = -(-size // (PACK_ROWS * LANE)) * PACK_ROWS
        out.append(t[off:off + rows].reshape(-1)[:size].reshape(shp))
        off += rows
    return out


def _stat_row(st, r):
    return st[r:r + 1, :]


def kernel(x, c, ada_w, ada_b, norm1_g, norm2_g, ff_w1, ff_w2, final_g, conv_w_in, conv_w, conv_b, conv_w_out, ssm_w_in, ssm_a_re, ssm_a_im, ssm_log_dt, ssm_b_re, ssm_b_im, ssm_c_re, ssm_c_im, ssm_d, ssm_glu_w, ssm_glu_b, ssm_w_out, sg_w_in, sg_v_g, sg_w_s, sg_b_s, sg_w_out, loss_target, m_ada_w, m_ada_b, m_norm1_g, m_norm2_g, m_ff_w1, m_ff_w2, m_final_g, m_conv_w_in, m_conv_w, m_conv_b, m_conv_w_out, m_ssm_w_in, m_ssm_a_re, m_ssm_a_im, m_ssm_log_dt, m_ssm_b_re, m_ssm_b_im, m_ssm_c_re, m_ssm_c_im, m_ssm_d, m_ssm_glu_w, m_ssm_glu_b, m_ssm_w_out, m_sg_w_in, m_sg_v_g, m_sg_w_s, m_sg_b_s, m_sg_w_out, v_ada_w, v_ada_b, v_norm1_g, v_norm2_g, v_ff_w1, v_ff_w2, v_final_g, v_conv_w_in, v_conv_w, v_conv_b, v_conv_w_out, v_ssm_w_in, v_ssm_a_re, v_ssm_a_im, v_ssm_log_dt, v_ssm_b_re, v_ssm_b_im, v_ssm_c_re, v_ssm_c_im, v_ssm_d, v_ssm_glu_w, v_ssm_glu_b, v_ssm_w_out, v_sg_w_in, v_sg_v_g, v_sg_w_s, v_sg_b_s, v_sg_w_out):
    P = dict(zip(INPUTS, (x, c, ada_w, ada_b, norm1_g, norm2_g, ff_w1, ff_w2, final_g, conv_w_in, conv_w, conv_b, conv_w_out, ssm_w_in, ssm_a_re, ssm_a_im, ssm_log_dt, ssm_b_re, ssm_b_im, ssm_c_re, ssm_c_im, ssm_d, ssm_glu_w, ssm_glu_b, ssm_w_out, sg_w_in, sg_v_g, sg_w_s, sg_b_s, sg_w_out, loss_target, m_ada_w, m_ada_b, m_norm1_g, m_norm2_g, m_ff_w1, m_ff_w2, m_final_g, m_conv_w_in, m_conv_w, m_conv_b, m_conv_w_out, m_ssm_w_in, m_ssm_a_re, m_ssm_a_im, m_ssm_log_dt, m_ssm_b_re, m_ssm_b_im, m_ssm_c_re, m_ssm_c_im, m_ssm_d, m_ssm_glu_w, m_ssm_glu_b, m_ssm_w_out, m_sg_w_in, m_sg_v_g, m_sg_w_s, m_sg_b_s, m_sg_w_out, v_ada_w, v_ada_b, v_norm1_g, v_norm2_g, v_ff_w1, v_ff_w2, v_final_g, v_conv_w_in, v_conv_w, v_conv_b, v_conv_w_out, v_ssm_w_in, v_ssm_a_re, v_ssm_a_im, v_ssm_log_dt, v_ssm_b_re, v_ssm_b_im, v_ssm_c_re, v_ssm_c_im, v_ssm_d, v_ssm_glu_w, v_ssm_glu_b, v_ssm_w_out, v_sg_w_in, v_sg_v_g, v_sg_w_s, v_sg_b_s, v_sg_w_out)))
    L, D = x.shape[1], x.shape[2]
    me = _my_index()
    xs = x[0]
    tgt = loss_target[0]
    n_conv = conv_w_in.shape[0]

    def gather_item(shard, axis):
        full = tuple(N_DEV * s if a == axis else s for a, s in enumerate(shard.shape))
        return shard, full, ('gather', axis, shard.shape[axis])

    def mixer_shards(i):
        kind, j = i % 3, i // 3
        if kind == 0:
            return [(conv_w_in[j], 1), (conv_w_out[j], 0)]
        if kind == 1:
            return [(ssm_w_in[j], 0), (ssm_glu_w[j], 0), (ssm_w_out[j], 0)]
        return [(sg_w_in[j], 1), (sg_w_out[j], 0)]

    c_act = c * (1.0 / (1.0 + jnp.exp(-c)))
    vec_rows = jnp.concatenate([c_act.reshape(D // LANE, LANE), conv_w.reshape(-1, LANE), conv_b.reshape(-1, LANE),
                                sg_v_g.reshape(-1, LANE)], 0)
    n_vec = vec_rows.shape[0]
    vec_all = _all_gather(_pad_rows(vec_rows, 24)[None], 0, name="gather_vectors")
    c_all = vec_all[:, :D // LANE, :].reshape(N_DEV, D)
    sharded_full = vec_all[:, D // LANE:n_vec, :].transpose(1, 0, 2).reshape(n_vec - D // LANE, D)
    conv_w_full = sharded_full[:3 * n_conv].reshape(n_conv, 3, D)
    conv_b_full = sharded_full[3 * n_conv:4 * n_conv]
    sg_vg_full = sharded_full[4 * n_conv:4 * n_conv + 1]

    c_pad = _pad_rows(c_all, LANE)
    ncol = ada_w.shape[2]
    mod_part = jnp.stack([_mm(c_pad, ada_w[i], name=f"ada_fwd{i}")[:N_DEV] for i in range(DEPTH)])
    mod_all = _all_gather(mod_part.reshape(1, DEPTH * N_DEV, ncol), 0, name="gather_mod")
    mod_all = mod_all.reshape(N_DEV, DEPTH, N_DEV, ncol)
    mod_me = lax.dynamic_index_in_dim(mod_all, me, 2, keepdims=False)
    mod = mod_me.transpose(1, 0, 2).reshape(DEPTH, N_DEV * ncol) + ada_b
    gathers, gather_token = _exchange_start(
        [[gather_item(w.astype(BF16), ax) for w, ax in shards]
         for i in range(DEPTH) for shards in (mixer_shards(i), [(ff_w1[i], 1), (ff_w2[i], 0)])],
        mod, name="gather_start", relayed=True)
    mod = mod + gather_token[0:1, 0:1]
    relayed = [None] * len(gathers)
    relayed[0], sent = _exchange_relay(gathers[0], mod, name="gather_mix_relay0")

    s5_args = (ssm_a_re[0], ssm_a_im[0], ssm_log_dt[0], ssm_b_re[0], ssm_b_im[0], ssm_c_re[0], ssm_c_im[0])
    (abar_re, abar_im, bblk_re, bblk_im, cblk_re, cblk_im), s5_vjp = jax.vjp(_s5_prep, *s5_args)
    pw_fwd, pw_rev, pos_fwd = _s5_power_tables(abar_re, abar_im, S5_CHUNK // SUBLANE)
    s5_w = tuple(t.astype(BF16) for t in (bblk_re, bblk_im, cblk_re, cblk_im))
    causal = jnp.tril(jnp.ones((SG_CHUNK, SG_CHUNK), dtype=bool))
    ws_m = jnp.where(causal[None], sg_w_s[0], 0.0)
    ws_b = ws_m.astype(BF16)
    wst_b = ws_m.transpose(0, 2, 1).astype(BF16)
    bsb = jnp.broadcast_to(sg_b_s[0][:, :, None], (SG_HEADS, SG_CHUNK, LANE))

    saved = []
    xa = xs
    mods = [[mod[i:i + 1, q * D:(q + 1) * D] for q in range(6)] for i in range(DEPTH)]
    wn1s = [norm1_g[i:i + 1] * (1.0 + mods[i][1]) for i in range(DEPTH)]
    h1 = _normmod_fwd(xa, wn1s[0], mods[0][0] + sent, name="norm1_fwd0")
    for i in range(DEPTH):
        kind, j = i % 3, i // 3
        sh1, sc1, g1, sh2, sc2, g2 = mods[i]
        wn1 = wn1s[i]
        wn2 = norm2_g[i:i + 1] * (1.0 + sc2)
        S = dict(x_in=xa, g1=g1, g2=g2, sc1=sc1, sc2=sc2, wn1=wn1, wn2=wn2)
        w_mix = _exchange_wait(relayed[2 * i], h1, name=f"gather_mix_wait{i}")
        S['h1'] = h1
        if kind == 0:
            bcx = _mm(h1, w_mix[0], name=f"conv_in{i}", out_dtypes=(BF16,), bm=2048)
            wb = _pad_rows(jnp.concatenate([conv_w_full[j], conv_b_full[j:j + 1]], 0), SUBLANE)
            pb = _conv_fwd(bcx, wb, name=f"conv_mix{i}")
            S.update(bcx=bcx, wb=wb, pb=pb)
        elif kind == 1:
            u = _mm(h1, w_mix[0], name=f"ssm_in{i}")
            sre, sim, ypre, yg = _s5_fwd(u, *s5_w, pw_fwd, pos_fwd, ssm_d, name=f"s5_scan{i}")

            def glu_epi(acc, yv, bias):
                t = acc + bias
                return yv * (1.0 / (1.0 + jnp.exp(-t))), t

            pb, tt = _mm(yg, w_mix[1], name=f"ssm_glu{i}", out_dtypes=(BF16, F32), epi=glu_epi,
                         extras=[(yg, 'mn'), (ssm_glu_b, 'n')])
            S.update(u=u, sre=sre, sim=sim, ypre=ypre, yg=yg, pb=pb, tt=tt)
        else:
            uv = _mm(h1, w_mix[0], name=f"sg_in{i}", bm=2048)
            pb = _sg_fwd(uv, sg_vg_full, ws_b, bsb, name=f"sg_mix{i}")
            S.update(uv=uv, pb=pb)
        relayed[2 * i + 1], sent = _exchange_relay(gathers[2 * i + 1], pb, name=f"gather_ff_relay{i}")
        x_mid, y_mix, h2 = _mm(pb, w_mix[-1], name=f"mix_out{i}", out_dtypes=(F32, BF16, BF16), epi=_epi_residual_norm,
                               extras=[(xa, 'mn'), (g1 + sent, 'n'), (wn2, 'n'), (sh2, 'n')])
        w1_full, w2_full = _exchange_wait(relayed[2 * i + 1], h2, name=f"gather_ff_wait{i}")
        S.update(w_mix=w_mix, w1=w1_full, w2=w2_full)
        ra = _mm(h2, w1_full, name=f"ff_up{i}", out_dtypes=(BF16,), epi=lambda acc: (jnp.maximum(acc, 0.0),), bm=2048)
        if i + 1 < DEPTH:
            relayed[2 * i + 2], sent = _exchange_relay(gathers[2 * i + 2], ra, name=f"gather_mix_relay{i + 1}")
            xa, f_out, h1 = _mm(ra, w2_full, name=f"ff_down{i}", out_dtypes=(F32, BF16, BF16), a_fn=_square,
                                epi=_epi_residual_norm, bm=256, bk=w2_full.shape[0],
                                extras=[(x_mid, 'mn'), (g2 + sent, 'n'), (wn1s[i + 1], 'n'), (mods[i + 1][0], 'n')])
        else:
            f_out = None
            dx, dfb, st = _mm(ra, w2_full, name=f"ff_down{i}", out_dtypes=(F32, BF16), epi=_epi_loss_head, a_fn=_square,
                              n_stats=3, bm=256, bk=w2_full.shape[0],
                              extras=[(x_mid, 'mn'), (g2, 'n'), (tgt, 'mn'), (final_g[None], 'n')])
        S.update(x_mid=x_mid, y_mix=y_mix, h2=h2, ra=ra, f_out=f_out)
        saved.append(S)

    loss_tile = st[:, :LANE]
    d_final_g = _stat_row(st, 1)
    dg2_next = _stat_row(st, 2)

    def scatter_item(g, axis):
        m = g.shape[axis] // N_DEV
        shard = tuple(m if a == axis else s for a, s in enumerate(g.shape))
        return g, (N_DEV,) + shard, ('scatter', axis, m)

    dmod = [None] * DEPTH
    dn1g, dn2g = [None] * DEPTH, [None] * DEPTH
    d_conv_w, d_conv_b = [None] * n_conv, [None] * n_conv
    ff_sent, mix_sent = [None] * DEPTH, [None] * DEPTH
    small = {}
    for i in reversed(range(DEPTH)):
        kind, j = i % 3, i // 3
        S = saved[i]
        w_mix = S['w_mix']
        dg2 = dg2_next
        da = _mm(dfb, S['w2'], tb=True, name=f"ff_down_bwd{i}", out_dtypes=(BF16,), bm=2048,
                 epi=lambda acc, rav: (acc * (2.0 * rav.astype(F32)),), extras=[(S['ra'], 'mn')])
        dw2 = _wgrad(S['ra'], dfb, name=f"ff_w2_grad{i}", a_fn=_square, bm=256, bn=1024)
        dw1 = _wgrad(S['h2'], da, name=f"ff_w1_grad{i}")
        (ff_sent[i],), token = _exchange_start([[scatter_item(dw1, 1), scatter_item(dw2, 0)]], dx,
                                               name=f"ff_grads_start{i}")
        dx_mid, dyb, st2 = _mm(da, S['w1'], tb=True, name=f"ff_up_bwd{i}", out_dtypes=(F32, BF16), bm=256,
                               bk=da.shape[1], epi=_epi_norm_bwd(True), n_stats=3,
                               extras=[(S['x_mid'], 'mn'), (S['wn2'] + token[0:1, 0:1], 'n'), (dx, 'mn'),
                                       (S['y_mix'], 'mn'), (S['g1'], 'n')])
        dsc2 = _stat_row(st2, 0) * norm2_g[i:i + 1]
        dn2g[i] = _stat_row(st2, 0) * (1.0 + S['sc2'])
        dsh2 = _stat_row(st2, 1)
        dg1 = _stat_row(st2, 2)
        if kind == 0:
            dp = _mm(dyb, w_mix[1], tb=True, name=f"conv_out_bwd{i}", out_dtypes=(BF16,))
            d_cwo = _wgrad(S['pb'], dyb, name=f"conv_w_out_grad{i}")
            dbcx, stc = _conv_bwd(dp, S['bcx'], S['wb'], name=f"conv_mix_bwd{i}")
            d_conv_w[j] = stc[0:3]
            d_conv_b[j] = stc[3:4]
            dh_operand, dh_name = dbcx, "conv_in_bwd"
            d_cwi = _wgrad(S['h1'], dbcx, name=f"conv_w_in_grad{i}")
            mix_grads = [scatter_item(d_cwi, 1), scatter_item(d_cwo, 0)]
        elif kind == 1:
            dtb, dya, stg = _mm(dyb, w_mix[2], tb=True, name=f"ssm_out_bwd{i}", out_dtypes=(BF16, F32), bm=512,
                                epi=_epi_glu_bwd, n_stats=1, extras=[(S['yg'], 'mn'), (S['tt'], 'mn')])
            d_ssm_out = _wgrad(S['pb'], dyb, name=f"ssm_w_out_grad{i}")
            dypre = _mm(dtb, w_mix[1], tb=True, name=f"ssm_glu_in_bwd{i}",
                        epi=lambda acc, a, yp: ((a + acc) * _gelu_grad(yp),),
                        extras=[(dya, 'mn'), (S['ypre'], 'mn')])
            d_glu = _wgrad(S['yg'], dtb, name=f"ssm_glu_w_grad{i}", bm=512)
            dub, dbre, dbim, dcre, dcim, ga, dd = _s5_bwd(dypre, S['u'], S['sre'], S['sim'], *s5_w, pw_rev, pos_fwd, ssm_d,
                                                           name=f"s5_scan_bwd{i}")
            dh_operand, dh_name = dub, "ssm_in_bwd"
            d_ssm_in = _wgrad(S['h1'], dub, name=f"ssm_w_in_grad{i}")
            da_re, da_im, dlog_dt, db_re, db_im, dc_re, dc_im = s5_vjp((ga[0:1], ga[1:2], dbre, dbim, dcre, dcim))
            s5_small = _pack_rows([da_re, da_im, dlog_dt, db_re, db_im, dc_re, dc_im, dd[0], stg[0]])
            mix_grads = [scatter_item(d_ssm_in, 0), scatter_item(d_glu, 0), scatter_item(d_ssm_out, 0),
                         gather_item(s5_small.astype(BF16), 0)]
        else:
            dp = _mm(dyb, w_mix[1], tb=True, name=f"sg_out_bwd{i}")
            d_sgo = _wgrad(S['pb'], dyb, name=f"sg_w_out_grad{i}")
            duv, dws, dbs, stv = _sg_bwd(dp, S['uv'], sg_vg_full, ws_b, wst_b, bsb, name=f"sg_mix_bwd{i}")
            dh_operand, dh_name = duv, "sg_in_bwd"
            d_sgi = _wgrad(S['h1'], duv, name=f"sg_w_in_grad{i}")
            sg_small = _pack_rows([jnp.where(causal[None], dws, 0.0), jnp.sum(dbs, axis=-1)])
            d_sg_vg = stv[0:1]
            mix_grads = [scatter_item(d_sgi, 1), scatter_item(d_sgo, 0), gather_item(sg_small.astype(BF16), 0)]
        wn1 = S['wn1']
        gate = []
        if i > 0:
            (mix_sent[i],), token = _exchange_start([mix_grads], dx_mid, name=f"mix_grads_start{i}")
            wn1 = wn1 + token[0:1, 0:1]
            gate = [(saved[i - 1]['f_out'], 'mn'), (saved[i - 1]['g2'], 'n')]
        res = _mm(dh_operand, w_mix[0], tb=True, name=f"{dh_name}{i}", out_dtypes=(F32, BF16) if i > 0 else (F32,),
                  bm=256, bk=w_mix[0].shape[1], epi=_epi_norm_bwd(i > 0), n_stats=3 if i > 0 else 2,
                  extras=[(S['x_in'], 'mn'), (wn1, 'n'), (dx_mid, 'mn')] + gate)
        if i > 0:
            dx, dfb, st1 = res
            dg2_next = _stat_row(st1, 2)
        else:
            dx, st1 = res
        dsc1 = _stat_row(st1, 0) * norm1_g[i:i + 1]
        dn1g[i] = _stat_row(st1, 0) * (1.0 + S['sc1'])
        dsh1 = _stat_row(st1, 1)
        dmod[i] = jnp.concatenate([dsh1, dsc1, dg1, dsh2, dsc2, dg2], 1)
    grad_x = dx[None]

    out = {}

    def small_group(names, parts, label):
        shapes = [P[n].shape for n in names]
        w, m, v = (_pack_rows([P[pre + n] for n in names])[None] for pre in ('', 'm_', 'v_'))
        res = [_unpack_rows(t[0], shapes) for t in _adamw(w, [parts], m, v, name=label)]
        for q, n in enumerate(names):
            out[n] = tuple(r[q] for r in res)

    small.update(ada_b=jnp.concatenate(dmod, 0), norm1_g=jnp.concatenate(dn1g, 0), norm2_g=jnp.concatenate(dn2g, 0),
                 final_g=d_final_g, conv_w=jnp.stack(d_conv_w), conv_b=jnp.concatenate(d_conv_b, 0), sg_v_g=d_sg_vg)
    last_pack = _pack_rows([small[n] for n in LAST_SMALL + SMALL_SHARD])
    n_last = _pack_rows([P[n] for n in LAST_SMALL]).shape[0]
    n_pack = last_pack.shape[0]
    pack_all = _all_gather(jnp.concatenate([last_pack, loss_tile], 0)[None], 0, name="gather_small_grads")
    loss = jnp.sum(pack_all[:, n_pack, 0])
    (mix_sent[0],), last_token = _exchange_start([mix_grads], pack_all, name="mix_grads_start0")
    small_group(LAST_SMALL, pack_all[:, :n_last], "adamw_small")
    sh_rows = (n_pack - n_last) // N_DEV
    sh_parts = pack_all[:, n_last:n_pack].reshape(N_DEV, sh_rows, N_DEV, LANE)
    sh_parts = lax.dynamic_index_in_dim(sh_parts, me, 2, keepdims=False)
    sh_parts = jnp.pad(sh_parts, ((0, 0), (0, 16 - sh_rows), (0, 0)))

    def pack_shard(prefix):
        return _pad_rows(jnp.concatenate([P[prefix + n].reshape(-1, LANE) for n in SMALL_SHARD], 0), 16)[None]

    sg_, sd_, sm_, sv_ = _adamw(pack_shard(''), [sh_parts], pack_shard('m_'), pack_shard('v_'), name="adamw_channel")
    off = 0
    for n in SMALL_SHARD:
        rows = math.prod(P[n].shape) // LANE
        out[n] = tuple(t[0, off:off + rows].reshape(P[n].shape) for t in (sg_, sd_, sm_, sv_))
        off += rows

    dmod_all = pack_all[:, :DEPTH * 6 * D // LANE].reshape(N_DEV, DEPTH, 6 * D)
    dmod_cols = lax.dynamic_slice_in_dim(dmod_all, me * ncol, ncol, 2)
    g_ada = [_mm(c_pad, _pad_rows(dmod_cols[:, i], LANE), ta=True, name=f"ada_w_grad{i}")[None] for i in range(DEPTH)]

    def big(name, parts):
        res = _adamw(P[name], parts, P['m_' + name], P['v_' + name], name="adamw_" + name)
        out[name] = res
        return res[1]

    ff_parts = [_exchange_wait(ff_sent[i], last_token, name=f"ff_grads_wait{i}") for i in range(DEPTH)]
    mix_parts = [None] + [_exchange_wait(mix_sent[i], last_token, name=f"mix_grads_wait{i}") for i in range(1, DEPTH)]
    big('ada_w', g_ada)
    big('ff_w1', [p[0] for p in ff_parts])
    big('ff_w2', [p[1] for p in ff_parts])
    done = big('sg_w_in', [mix_parts[2][0]])
    mix_parts[0] = _exchange_wait(mix_sent[0], done, name="mix_grads_wait0")
    big('conv_w_in', [mix_parts[i][0] for i in range(DEPTH) if i % 3 == 0])
    row_names = ['conv_w_out', 'ssm_w_in', 'ssm_glu_w', 'ssm_w_out', 'sg_w_out']
    row_parts = ([mix_parts[i][1] for i in range(DEPTH) if i % 3 == 0] + mix_parts[1][:3] + [mix_parts[2][1]])
    small_group(S5_SMALL, mix_parts[1][3].reshape(N_DEV, -1, LANE), "adamw_s5")
    small_group(SG_SMALL, mix_parts[2][2].reshape(N_DEV, -1, LANE), "adamw_sg")
    row_w, row_m, row_v = (jnp.concatenate([P[pre + n] for n in row_names], 0) for pre in ('', 'm_', 'v_'))
    rw = _adamw(row_w, row_parts, row_m, row_v, name="adamw_row_sharded")
    off = 0
    for n in row_names:
        cnt = P[n].shape[0]
        out[n] = tuple(t[off:off + cnt] for t in rw)
        off += cnt

    return (loss, grad_x, *[out[n][0] for n in WEIGHTS], *[out[n][1] for n in WEIGHTS],
            *[out[n][2] for n in WEIGHTS], *[out[n][3] for n in WEIGHTS])
```

```python
import math

import jax
import jax.numpy as jnp
from jax import lax
from jax.experimental import pallas as pl
from jax.experimental.pallas import tpu as pltpu

F32 = jnp.float32
BF16 = jnp.bfloat16

N_DEV = 8
MESH_ID = pl.DeviceIdType.MESH
DEPTH = 4
EPS = 1e-6
S5_GROUPS, S5_GROUP, S5_STATE = 64, 16, 64
S5_LANES = S5_GROUPS * S5_STATE
S5_BLOCKS = 8
S5_CHUNK = 512
SG_HEADS, SG_CHUNK = 8, 128
LANE = 128
SUBLANE = 8
VMEM_LIMIT = 48 * 1024 * 1024
ADAM_LR, ADAM_B1, ADAM_B2, ADAM_EPS, ADAM_WD, ADAM_STEP = 0.001, 0.9, 0.999, 1e-08, 0.01, 10
GELU_C = math.sqrt(2.0 / math.pi)
GELU_A = 0.044715

WEIGHTS = ['ada_w', 'ada_b', 'norm1_g', 'norm2_g', 'ff_w1', 'ff_w2', 'final_g', 'conv_w_in', 'conv_w', 'conv_b',
           'conv_w_out', 'ssm_w_in', 'ssm_a_re', 'ssm_a_im', 'ssm_log_dt', 'ssm_b_re', 'ssm_b_im', 'ssm_c_re',
           'ssm_c_im', 'ssm_d', 'ssm_glu_w', 'ssm_glu_b', 'ssm_w_out', 'sg_w_in', 'sg_v_g', 'sg_w_s', 'sg_b_s',
           'sg_w_out']
INPUTS = ['x', 'c'] + WEIGHTS + ['loss_target'] + ['m_' + n for n in WEIGHTS] + ['v_' + n for n in WEIGHTS]
S5_SMALL = ['ssm_a_re', 'ssm_a_im', 'ssm_log_dt', 'ssm_b_re', 'ssm_b_im', 'ssm_c_re', 'ssm_c_im', 'ssm_d', 'ssm_glu_b']
SG_SMALL = ['sg_w_s', 'sg_b_s']
LAST_SMALL = ['ada_b', 'norm1_g', 'norm2_g', 'final_g']
SMALL_SHARD = ['conv_w', 'conv_b', 'sg_v_g']


def _params(*sem):
    return pltpu.CompilerParams(dimension_semantics=sem or None, vmem_limit_bytes=VMEM_LIMIT)


def _my_pos():
    return lax.axis_index("x"), lax.axis_index("y"), lax.axis_index("c")


def _my_index():
    x, y, c = _my_pos()
    return 4 * x + 2 * y + c


def _mm(a, b, *, name, ta=False, tb=False, out_dtypes=(F32,), epi=None, extras=(), a_fn=None, n_stats=0, bm=1024,
        bn=1024, bk=1024):
    a_chunks = a.shape[0] if a.ndim == 3 else 0
    b_chunks = b.shape[0] if b.ndim == 3 else 0
    assert not (a_chunks and ta) and not (b_chunks and tb)
    if a_chunks:
        m, k = a.shape[1], a_chunks * a.shape[2]
        bk = k
    else:
        m, k = (a.shape[1], a.shape[0]) if ta else a.shape
    if b_chunks:
        k2, n = b.shape[1], b_chunks * b.shape[2]
        bn = min(bn, b.shape[2])
    else:
        k2, n = (b.shape[1], b.shape[0]) if tb else b.shape
    assert k == k2, (a.shape, b.shape, ta, tb)
    bm, bn, bk = min(bm, m), min(bn, n), min(bk, k)
    assert m % bm == 0 and n % bn == 0 and k % bk == 0, (m, n, k, bm, bn, bk)
    nk = k // bk
    assert nk == 1 or n_stats == 0
    n_ex, n_out = len(extras), len(out_dtypes)
    dims = (((0 if ta else 1,), (1 if tb else 0,)), ((), ()))

    def body(*refs):
        a_ref, b_ref = refs[0], refs[1]
        ex_refs = refs[2:2 + n_ex]
        out_refs = refs[2 + n_ex:2 + n_ex + n_out]

        def finish(acc):
            outs = epi(acc, *[r[...] for r in ex_refs]) if epi is not None else (acc,)
            for r, o in zip(out_refs, outs[:n_out]):
                r[...] = o.astype(r.dtype)
            if n_stats:
                st_ref = refs[2 + n_ex + n_out]

                @pl.when(pl.program_id(0) == 0)
                def _():
                    st_ref[...] = jnp.zeros_like(st_ref)

                for q, row in enumerate(outs[n_out:]):
                    st_ref[q:q + 1, :] += row

        av = jnp.concatenate([a_ref[t] for t in range(a_chunks)], axis=1) if a_chunks else a_ref[...]
        if a_fn is not None:
            av = a_fn(av)
        part = lax.dot_general(av.astype(BF16), b_ref[...].astype(BF16), dims, preferred_element_type=F32)
        if nk == 1:
            finish(part)
            return
        acc_ref = refs[-1]
        kk = pl.program_id(2)

        @pl.when(kk == 0)
        def _():
            acc_ref[...] = part

        @pl.when(kk > 0)
        def _():
            acc_ref[...] += part

        @pl.when(kk == nk - 1)
        def _():
            finish(acc_ref[...])

    if a_chunks:
        a_spec = pl.BlockSpec((a_chunks, bm, a.shape[2]), lambda i, j, q: (0, i, 0))
    elif ta:
        a_spec = pl.BlockSpec((bk, bm), lambda i, j, q: (q, i))
    else:
        a_spec = pl.BlockSpec((bm, bk), lambda i, j, q: (i, q))
    if b_chunks:
        per = b.shape[2] // bn
        b_spec = pl.BlockSpec((None, bk, bn), lambda i, j, q: (j // per, q, j % per))
    elif tb:
        b_spec = pl.BlockSpec((bn, bk), lambda i, j, q: (j, q))
    else:
        b_spec = pl.BlockSpec((bk, bn), lambda i, j, q: (q, j))
    ex_specs = []
    for arr, kind in extras:
        if kind == 'mn':
            assert arr.shape == (m, n), (arr.shape, m, n)
            ex_specs.append(pl.BlockSpec((bm, bn), lambda i, j, q: (i, j)))
        else:
            assert arr.shape == (1, n), (arr.shape, n)
            ex_specs.append(pl.BlockSpec((1, bn), lambda i, j, q: (0, j)))
    out_shape = [jax.ShapeDtypeStruct((m, n), d) for d in out_dtypes]
    out_specs = [pl.BlockSpec((bm, bn), lambda i, j, q: (i, j)) for _ in out_dtypes]
    if n_stats:
        assert n_stats <= SUBLANE
        out_shape.append(jax.ShapeDtypeStruct((SUBLANE, n), F32))
        out_specs.append(pl.BlockSpec((SUBLANE, bn), lambda i, j, q: (0, j)))
    outs = pl.pallas_call(
        body, name=name, out_shape=tuple(out_shape), grid=(m // bm, n // bn, nk),
        in_specs=[a_spec, b_spec] + ex_specs, out_specs=tuple(out_specs),
        scratch_shapes=[pltpu.VMEM((bm, bn), F32)] if nk > 1 else [],
        compiler_params=_params(*(["arbitrary"] * 3 if n_stats else ["parallel", "parallel", "arbitrary"])),
    )(a, b, *[arr for arr, _ in extras])
    return outs if len(outs) > 1 else outs[0]


def _epi_residual_norm(acc, res, gate, w, sh):
    xn = res + gate * acc
    return xn, acc, xn * _rstd(xn) * w + sh


def _epi_norm_bwd(gated):
    def epi(dh, xv, w, dres, *gate):
        rstd = _rstd(xv)
        xn = xv * rstd
        dxn = dh * w
        dx = rstd * (dxn - xn * jnp.mean(dxn * xn, axis=-1, keepdims=True)) + dres
        stats = [jnp.sum(dh * xn, axis=0, keepdims=True), jnp.sum(dh, axis=0, keepdims=True)]
        if not gated:
            return (dx, *stats)
        yv, g = gate
        return (dx, dx * g, *stats, jnp.sum(dx * yv.astype(F32), axis=0, keepdims=True))
    return epi


def _epi_loss_head(f, x_mid, g, tgt, fg):
    xv = x_mid + g * f
    rstd = _rstd(xv)
    xn = xv * rstd
    err = xn * fg - tgt
    loss = 0.5 * jnp.sum(jnp.mean(err * err, axis=-1, keepdims=True))
    dout = err * (1.0 / xv.shape[-1])
    dxn = dout * fg
    dx = rstd * (dxn - xn * jnp.mean(dxn * xn, axis=-1, keepdims=True))
    return (dx, dx * g, jnp.full((1, xv.shape[-1]), loss, F32), jnp.sum(dout * xn, axis=0, keepdims=True),
            jnp.sum(dx * f, axis=0, keepdims=True))


def _epi_glu_bwd(dy2, yv, t):
    sig = 1.0 / (1.0 + jnp.exp(-t))
    dt = dy2 * yv * sig * (1.0 - sig)
    return dt, dy2 * sig, jnp.sum(dt, axis=0, keepdims=True)


def _wgrad(acts, cots, *, name, a_fn=None, bm=1024, bn=512):
    return _mm(acts, cots, ta=True, name=name, out_dtypes=(BF16,), a_fn=a_fn, bm=bm, bn=bn, bk=acts.shape[0])


def _square(a):
    af = a.astype(F32)
    return af * af


def _rstd(xv):
    return lax.rsqrt(jnp.mean(xv * xv, axis=-1, keepdims=True) + EPS)


def _normmod_fwd(x, w, sh, *, name, tm=512):
    L, D = x.shape

    def body(x_ref, w_ref, s_ref, h_ref):
        xv = x_ref[...]
        h_ref[...] = (xv * _rstd(xv) * w_ref[...] + s_ref[...]).astype(h_ref.dtype)

    row = pl.BlockSpec((tm, D), lambda i: (i, 0))
    vec = pl.BlockSpec((1, D), lambda i: (0, 0))
    return pl.pallas_call(body, name=name, out_shape=jax.ShapeDtypeStruct((L, D), BF16), grid=(L // tm,),
                          in_specs=[row, vec, vec], out_specs=row, compiler_params=_params("parallel"))(x, w, sh)


def _shift_down(v, k):
    row = lax.broadcasted_iota(jnp.int32, v.shape, 0)
    return jnp.where(row >= k, pltpu.roll(v, k, 0), 0.0)


def _shift_up(v, k):
    n = v.shape[0]
    row = lax.broadcasted_iota(jnp.int32, v.shape, 0)
    return jnp.where(row < n - k, pltpu.roll(v, n - k, 0), 0.0)


def _conv_views(L, D):
    return [pl.BlockSpec((L, LANE), lambda j, s=s: (0, s * (D // LANE) + j)) for s in range(3)]


def _conv_fwd(bcx, wb, *, name):
    L, D = bcx.shape[0], bcx.shape[1] // 3

    def body(b_ref, c_ref, x_ref, wb_ref, p_ref):
        z = c_ref[...].astype(F32) * x_ref[...].astype(F32)
        conv = (wb_ref[0:1, :] * _shift_down(z, 2) + wb_ref[1:2, :] * _shift_down(z, 1)
                + wb_ref[2:3, :] * z + wb_ref[3:4, :])
        p_ref[...] = (b_ref[...].astype(F32) * conv).astype(p_ref.dtype)

    col = pl.BlockSpec((L, LANE), lambda j: (0, j))
    return pl.pallas_call(body, name=name, out_shape=jax.ShapeDtypeStruct((L, D), BF16), grid=(D // LANE,),
                          in_specs=_conv_views(L, D) + [pl.BlockSpec((SUBLANE, LANE), lambda j: (0, j))],
                          out_specs=col, compiler_params=_params("parallel"))(bcx, bcx, bcx, wb)


def _conv_bwd(dp, bcx, wb, *, name):
    L, D = dp.shape

    def body(dp_ref, b_ref, c_ref, x_ref, wb_ref, d3_ref, st_ref):
        cv, xv = c_ref[...].astype(F32), x_ref[...].astype(F32)
        z = cv * xv
        z1, z2 = _shift_down(z, 1), _shift_down(z, 2)
        w0, w1, w2 = wb_ref[0:1, :], wb_ref[1:2, :], wb_ref[2:3, :]
        conv = w0 * z2 + w1 * z1 + w2 * z + wb_ref[3:4, :]
        dpv = dp_ref[...].astype(F32)
        d3_ref[0] = (dpv * conv).astype(d3_ref.dtype)
        dconv = dpv * b_ref[...].astype(F32)
        dz = w2 * dconv + w1 * _shift_up(dconv, 1) + w0 * _shift_up(dconv, 2)
        d3_ref[1] = (dz * xv).astype(d3_ref.dtype)
        d3_ref[2] = (dz * cv).astype(d3_ref.dtype)
        st_ref[...] = jnp.zeros_like(st_ref)
        st_ref[0:1, :] = jnp.sum(dconv * z2, axis=0, keepdims=True)
        st_ref[1:2, :] = jnp.sum(dconv * z1, axis=0, keepdims=True)
        st_ref[2:3, :] = jnp.sum(dconv * z, axis=0, keepdims=True)
        st_ref[3:4, :] = jnp.sum(dconv, axis=0, keepdims=True)

    col = pl.BlockSpec((L, LANE), lambda j: (0, j))
    vec = pl.BlockSpec((SUBLANE, LANE), lambda j: (0, j))
    return pl.pallas_call(body, name=name,
                          out_shape=(jax.ShapeDtypeStruct((3, L, D), BF16), jax.ShapeDtypeStruct((SUBLANE, D), F32)),
                          grid=(D // LANE,), in_specs=[col] + _conv_views(L, D) + [vec],
                          out_specs=(pl.BlockSpec((3, L, LANE), lambda j: (0, 0, j)), vec),
                          compiler_params=_params("parallel"))(dp, bcx, bcx, bcx, wb)


def _sg_fwd(uv, vg, ws, bsb, *, name, tr=512):
    L, D = uv.shape[0], uv.shape[1] // 2

    def body(uv_ref, vg_ref, ws_ref, bsb_ref, p_ref):
        for ci in range(tr // SG_CHUNK):
            rows = slice(ci * SG_CHUNK, (ci + 1) * SG_CHUNK)
            v = uv_ref[rows, D:2 * D]
            vn = (v * _rstd(v) * vg_ref[...]).astype(BF16)
            for h in range(SG_HEADS):
                cols = slice(h * LANE, (h + 1) * LANE)
                vm = jnp.dot(ws_ref[h], vn[:, cols], preferred_element_type=F32) + bsb_ref[h]
                p_ref[rows, cols] = (uv_ref[rows, cols] * vm).astype(p_ref.dtype)

    full3 = pl.BlockSpec((SG_HEADS, SG_CHUNK, LANE), lambda i: (0, 0, 0))
    return pl.pallas_call(body, name=name, out_shape=jax.ShapeDtypeStruct((L, D), BF16), grid=(L // tr,),
                          in_specs=[pl.BlockSpec((tr, 2 * D), lambda i: (i, 0)), pl.BlockSpec((1, D), lambda i: (0, 0)),
                                    full3, full3],
                          out_specs=pl.BlockSpec((tr, D), lambda i: (i, 0)),
                          compiler_params=_params("parallel"))(uv, vg, ws, bsb)


def _sg_bwd(dp, uv, vg, ws, wst, bsb, *, name, tr=512):
    L, D = dp.shape

    def body(dp_ref, uv_ref, vg_ref, ws_ref, wst_ref, bsb_ref, duv_ref, dws_ref, dbs_ref, st_ref, dvn_ref):
        i = pl.program_id(0)

        @pl.when(i == 0)
        def _():
            dws_ref[...] = jnp.zeros_like(dws_ref)
            dbs_ref[...] = jnp.zeros_like(dbs_ref)
            st_ref[...] = jnp.zeros_like(st_ref)

        for ci in range(tr // SG_CHUNK):
            rows = slice(ci * SG_CHUNK, (ci + 1) * SG_CHUNK)
            v = uv_ref[rows, D:2 * D]
            rstd = _rstd(v)
            vhat = v * rstd
            vn = (vhat * vg_ref[...]).astype(BF16)
            for h in range(SG_HEADS):
                cols = slice(h * LANE, (h + 1) * LANE)
                vm = jnp.dot(ws_ref[h], vn[:, cols], preferred_element_type=F32) + bsb_ref[h]
                dph = dp_ref[rows, cols]
                duv_ref[rows, cols] = (dph * vm).astype(duv_ref.dtype)
                dvm = dph * uv_ref[rows, cols]
                dbs_ref[h] += dvm
                dvmb = dvm.astype(BF16)
                dws_ref[h] += lax.dot_general(dvmb, vn[:, cols], (((1,), (1,)), ((), ())),
                                              preferred_element_type=F32)
                dvn_ref[rows, cols] = jnp.dot(wst_ref[h], dvmb, preferred_element_type=F32)
            dvn = dvn_ref[rows, :]
            gv = dvn * vg_ref[...]
            dv = rstd * (gv - vhat * jnp.mean(gv * vhat, axis=-1, keepdims=True))
            duv_ref[rows, D:2 * D] = dv.astype(duv_ref.dtype)
            st_ref[0:1, :] += jnp.sum(dvn * vhat, axis=0, keepdims=True)

    full3 = pl.BlockSpec((SG_HEADS, SG_CHUNK, LANE), lambda i: (0, 0, 0))
    acc3 = jax.ShapeDtypeStruct((SG_HEADS, SG_CHUNK, LANE), F32)
    return pl.pallas_call(
        body, name=name,
        out_shape=(jax.ShapeDtypeStruct((L, 2 * D), BF16), acc3, acc3, jax.ShapeDtypeStruct((SUBLANE, D), F32)),
        grid=(L // tr,),
        in_specs=[pl.BlockSpec((tr, D), lambda i: (i, 0)), pl.BlockSpec((tr, 2 * D), lambda i: (i, 0)),
                  pl.BlockSpec((1, D), lambda i: (0, 0)), full3, full3, full3],
        out_specs=(pl.BlockSpec((tr, 2 * D), lambda i: (i, 0)), full3, full3,
                   pl.BlockSpec((SUBLANE, D), lambda i: (0, 0))),
        scratch_shapes=[pltpu.VMEM((tr, D), F32)],
        compiler_params=_params("arbitrary"))(dp, uv, vg, ws, wst, bsb)


def _gelu(x):
    return 0.5 * x * (1.0 + jnp.tanh(GELU_C * (x + GELU_A * x * x * x)))


def _gelu_grad(x):
    th = jnp.tanh(GELU_C * (x + GELU_A * x * x * x))
    return 0.5 * (1.0 + th) + 0.5 * x * (1.0 - th * th) * GELU_C * (1.0 + 3.0 * GELU_A * x * x)


def _cmul_add(xr, xi, ar, ai, br, bi):
    return xr + ar * br - ai * bi, xi + ar * bi + ai * br


def _cmul_conj_add(xr, xi, ar, ai, br, bi):
    return xr + ar * br + ai * bi, xi + ar * bi - ai * br


def _to_subchunk_order(src_ref, dst_ref, n):
    for k in range(n):
        dst_ref[pl.ds(SUBLANE * k, SUBLANE), :] = src_ref[pl.ds(k, SUBLANE, stride=n), :].astype(dst_ref.dtype)


def _to_time_order(src_ref, dst_ref, n):
    for m in range(n):
        r, k = divmod(SUBLANE * m, n)
        dst_ref[pl.ds(SUBLANE * m, SUBLANE), :] = src_ref[pl.ds(SUBLANE * k + r, SUBLANE, stride=SUBLANE), :]


def _s5_fwd(u, bre, bim, cre, cim, pw, pos, dsk, *, name, tc=S5_CHUNK):
    L, D = u.shape
    W = S5_LANES // S5_BLOCKS
    nt = L // tc
    n = tc // SUBLANE

    def sub(k):
        return pl.ds(SUBLANE * k, SUBLANE)

    def body(u_ref, bre_ref, bim_ref, cre_ref, cim_ref, pw_ref, pos_ref, d_ref, sre_ref, sim_ref, ypre_ref, yg_ref,
             carry, up, yp):
        t = pl.program_id(1)

        @pl.when(t == 0)
        def _():
            carry[...] = jnp.zeros_like(carry)

        _to_subchunk_order(u_ref, up, n)
        uv = up[...]
        ub = uv.astype(BF16)
        sre_ref[...] = jnp.dot(ub, bre_ref[...], preferred_element_type=F32)
        sim_ref[...] = jnp.dot(ub, bim_ref[...], preferred_element_type=F32)

        ar, ai = pw_ref[8], pw_ref[9]
        xr = jnp.zeros((SUBLANE, W), F32)
        xi = jnp.zeros((SUBLANE, W), F32)
        for k in range(n):
            xr, xi = _cmul_add(sre_ref[sub(k), :], sim_ref[sub(k), :], ar, ai, xr, xi)
            sre_ref[sub(k), :] = xr
            sim_ref[sub(k), :] = xi
        for q, d in enumerate((1, 2, 4)):
            xr, xi = _cmul_add(xr, xi, pw_ref[2 * q], pw_ref[2 * q + 1], pltpu.roll(xr, d, 0), pltpu.roll(xi, d, 0))
        cr, ci = carry[0], carry[1]
        xr, xi = _cmul_add(xr, xi, pw_ref[6], pw_ref[7], cr, ci)
        first = lax.broadcasted_iota(jnp.int32, (SUBLANE, W), 0) == 0
        er = jnp.where(first, cr, pltpu.roll(xr, 1, 0))
        ei = jnp.where(first, ci, pltpu.roll(xi, 1, 0))
        last = slice(SUBLANE - 1, SUBLANE)
        carry[0] = jnp.broadcast_to(xr[last, :], (SUBLANE, W))
        carry[1] = jnp.broadcast_to(xi[last, :], (SUBLANE, W))
        for k in range(n):
            sr, si = _cmul_add(sre_ref[sub(k), :], sim_ref[sub(k), :], pos_ref[0, k:k + 1, :], pos_ref[1, k:k + 1, :],
                               er, ei)
            sre_ref[sub(k), :] = sr
            sim_ref[sub(k), :] = si
        yp[...] = (jnp.dot(sre_ref[...].astype(BF16), cre_ref[...], preferred_element_type=F32)
                   - jnp.dot(sim_ref[...].astype(BF16), cim_ref[...], preferred_element_type=F32) + d_ref[...] * uv)
        _to_time_order(yp, ypre_ref, n)
        yg_ref[...] = _gelu(ypre_ref[...])

    ch = pl.BlockSpec((tc, LANE), lambda j, t: (t, j))
    st = pl.BlockSpec((tc, W), lambda j, t: (t, j))
    bsp = pl.BlockSpec((None, LANE, W), lambda j, t: (j, 0, 0))
    csp = pl.BlockSpec((None, W, LANE), lambda j, t: (j, 0, 0))
    return pl.pallas_call(
        body, name=name,
        out_shape=(jax.ShapeDtypeStruct((L, S5_LANES), F32), jax.ShapeDtypeStruct((L, S5_LANES), F32),
                   jax.ShapeDtypeStruct((L, D), F32), jax.ShapeDtypeStruct((L, D), F32)),
        grid=(S5_BLOCKS, nt),
        in_specs=[ch, bsp, bsp, csp, csp, pl.BlockSpec((10, SUBLANE, W), lambda j, t: (0, 0, j)),
                  pl.BlockSpec((2, n, W), lambda j, t: (0, 0, j)), pl.BlockSpec((1, LANE), lambda j, t: (0, j))],
        out_specs=(st, st, ch, ch),
        scratch_shapes=[pltpu.VMEM((2, SUBLANE, W), F32), pltpu.VMEM((tc, LANE), F32), pltpu.VMEM((tc, LANE), F32)],
        compiler_params=_params("parallel", "arbitrary"))(u, bre, bim, cre, cim, pw, pos, dsk)


def _s5_bwd(dy, u, sre, sim, bre, bim, cre, cim, pwr, posr, dsk, *, name, tc=S5_CHUNK):
    L, D = u.shape
    W = S5_LANES // S5_BLOCKS
    nt = L // tc
    n = tc // SUBLANE
    nt_dims = (((1,), (1,)), ((), ()))
    tn_dims = (((0,), (0,)), ((), ()))

    def sub(k):
        return pl.ds(SUBLANE * k, SUBLANE)

    def body(dy_ref, u_ref, sre_ref, sim_ref, bre_ref, bim_ref, cre_ref, cim_ref, pw_ref, pos_ref, d_ref,
             du_ref, dbre_ref, dbim_ref, dcre_ref, dcim_ref, ga_ref, dd_ref, gre, gim, carry, gacc, up, dyp):
        t = pl.program_id(1)

        @pl.when(t == 0)
        def _():
            for r in (carry, gacc, dbre_ref, dbim_ref, dcre_ref, dcim_ref, ga_ref, dd_ref):
                r[...] = jnp.zeros_like(r)

        _to_subchunk_order(dy_ref, dyp, n)
        _to_subchunk_order(u_ref, up, n)
        dyv, uv = dyp[...], up[...]
        dyb, ub = dyv.astype(BF16), uv.astype(BF16)
        gre[...] = lax.dot_general(dyb, cre_ref[...], nt_dims, preferred_element_type=F32)
        gim[...] = -lax.dot_general(dyb, cim_ref[...], nt_dims, preferred_element_type=F32)
        br, bi = pw_ref[8], pw_ref[9]
        xr = jnp.zeros((SUBLANE, W), F32)
        xi = jnp.zeros((SUBLANE, W), F32)
        for k in reversed(range(n)):
            xr, xi = _cmul_add(gre[sub(k), :], gim[sub(k), :], br, bi, xr, xi)
            gre[sub(k), :] = xr
            gim[sub(k), :] = xi
        for q, d in enumerate((1, 2, 4)):
            xr, xi = _cmul_add(xr, xi, pw_ref[2 * q], pw_ref[2 * q + 1], pltpu.roll(xr, SUBLANE - d, 0),
                               pltpu.roll(xi, SUBLANE - d, 0))
        cr, ci = carry[0], carry[1]
        xr, xi = _cmul_add(xr, xi, pw_ref[6], pw_ref[7], cr, ci)
        top = lax.broadcasted_iota(jnp.int32, (SUBLANE, W), 0) == SUBLANE - 1
        er = jnp.where(top, cr, pltpu.roll(xr, SUBLANE - 1, 0))
        ei = jnp.where(top, ci, pltpu.roll(xi, SUBLANE - 1, 0))
        carry[0] = jnp.broadcast_to(xr[0:1, :], (SUBLANE, W))
        carry[1] = jnp.broadcast_to(xi[0:1, :], (SUBLANE, W))
        nr, ni = er, ei
        acc_r = jnp.zeros((SUBLANE, W), F32)
        acc_i = jnp.zeros((SUBLANE, W), F32)
        for k in reversed(range(n)):
            place = slice(n - 1 - k, n - k)
            gr, gi = _cmul_conj_add(gre[sub(k), :], gim[sub(k), :], pos_ref[0, place, :], pos_ref[1, place, :], er, ei)
            gre[sub(k), :] = gr
            gim[sub(k), :] = gi
            sr, si = sre_ref[sub(k), :], sim_ref[sub(k), :]
            acc_r = acc_r + sr * nr + si * ni
            acc_i = acc_i + sr * ni - si * nr
            nr, ni = gr, gi
        gacc[0] += acc_r
        gacc[1] += acc_i
        grb, gib = gre[...].astype(BF16), gim[...].astype(BF16)
        dyp[...] = (lax.dot_general(grb, bre_ref[...], nt_dims, preferred_element_type=F32)
                    + lax.dot_general(gib, bim_ref[...], nt_dims, preferred_element_type=F32) + d_ref[...] * dyv)
        _to_time_order(dyp, up, n)
        du_ref[...] = up[...].astype(du_ref.dtype)
        dbre_ref[...] += lax.dot_general(ub, grb, tn_dims, preferred_element_type=F32)
        dbim_ref[...] += lax.dot_general(ub, gib, tn_dims, preferred_element_type=F32)
        dcre_ref[...] += lax.dot_general(sre_ref[...].astype(BF16), dyb, tn_dims, preferred_element_type=F32)
        dcim_ref[...] -= lax.dot_general(sim_ref[...].astype(BF16), dyb, tn_dims, preferred_element_type=F32)
        dd_ref[0:1, :] += jnp.sum(dyv * uv, axis=0, keepdims=True)

        @pl.when(t == nt - 1)
        def _():
            ga_ref[0:1, :] = jnp.sum(gacc[0], axis=0, keepdims=True)
            ga_ref[1:2, :] = jnp.sum(gacc[1], axis=0, keepdims=True)

    ch = pl.BlockSpec((tc, LANE), lambda j, t: (nt - 1 - t, j))
    st = pl.BlockSpec((tc, W), lambda j, t: (nt - 1 - t, j))
    bsp = pl.BlockSpec((None, LANE, W), lambda j, t: (j, 0, 0))
    csp = pl.BlockSpec((None, W, LANE), lambda j, t: (j, 0, 0))
    return pl.pallas_call(
        body, name=name,
        out_shape=(jax.ShapeDtypeStruct((L, D), BF16),
                   jax.ShapeDtypeStruct((S5_BLOCKS, LANE, W), F32), jax.ShapeDtypeStruct((S5_BLOCKS, LANE, W), F32),
                   jax.ShapeDtypeStruct((S5_BLOCKS, W, LANE), F32), jax.ShapeDtypeStruct((S5_BLOCKS, W, LANE), F32),
                   jax.ShapeDtypeStruct((SUBLANE, S5_LANES), F32), jax.ShapeDtypeStruct((SUBLANE, D), F32)),
        grid=(S5_BLOCKS, nt),
        in_specs=[ch, ch, st, st, bsp, bsp, csp, csp, pl.BlockSpec((10, SUBLANE, W), lambda j, t: (0, 0, j)),
                  pl.BlockSpec((2, n, W), lambda j, t: (0, 0, j)), pl.BlockSpec((1, LANE), lambda j, t: (0, j))],
        out_specs=(ch, bsp, bsp, csp, csp, pl.BlockSpec((SUBLANE, W), lambda j, t: (0, j)),
                   pl.BlockSpec((SUBLANE, LANE), lambda j, t: (0, j))),
        scratch_shapes=[pltpu.VMEM((tc, W), F32), pltpu.VMEM((tc, W), F32), pltpu.VMEM((2, SUBLANE, W), F32),
                        pltpu.VMEM((2, SUBLANE, W), F32), pltpu.VMEM((tc, LANE), F32), pltpu.VMEM((tc, LANE), F32)],
        compiler_params=_params("parallel", "arbitrary"))(dy, u, sre, sim, bre, bim, cre, cim, pwr, posr, dsk)


def _s5_prep(a_re, a_im, log_dt, b_re, b_im, c_re, c_im):
    dt = jnp.exp(log_dt)[:, None]
    mag = jnp.exp(a_re * dt)
    abar_re = mag * jnp.cos(a_im * dt)
    abar_im = mag * jnp.sin(a_im * dt)
    den = a_re * a_re + a_im * a_im
    nr = abar_re - 1.0
    ni = abar_im
    f_re = ((nr * a_re + ni * a_im) / den)[..., None]
    f_im = ((ni * a_re - nr * a_im) / den)[..., None]
    bbar_re = f_re * b_re - f_im * b_im
    bbar_im = f_re * b_im + f_im * b_re
    eye = jnp.eye(S5_GROUPS // S5_BLOCKS, dtype=F32)
    gb = S5_GROUPS // S5_BLOCKS

    def blk_b(bb):
        t = bb.reshape(S5_BLOCKS, gb, S5_STATE, S5_GROUP)
        return jnp.einsum('jgph,gk->jghkp', t, eye).reshape(S5_BLOCKS, gb * S5_GROUP, gb * S5_STATE)

    def blk_c(cc):
        t = cc.reshape(S5_BLOCKS, gb, S5_GROUP, S5_STATE)
        return jnp.einsum('jghp,gk->jgpkh', t, eye).reshape(S5_BLOCKS, gb * S5_STATE, gb * S5_GROUP)

    return (abar_re.reshape(1, S5_LANES), abar_im.reshape(1, S5_LANES), blk_b(bbar_re), blk_b(bbar_im),
            blk_c(c_re), blk_c(c_im))


def _cpowers(ar, ai, count):
    pr, pi, m = ar, ai, 1
    while m < count:
        tr, ti = pr[m - 1:m], pi[m - 1:m]
        pr, pi = jnp.concatenate([pr, pr * tr - pi * ti], 0), jnp.concatenate([pi, pr * ti + pi * tr], 0)
        m *= 2
    return pr, pi


def _s5_power_tables(ar, ai, n):
    pr, pi = _cpowers(ar, ai, n)
    qr, qi = _cpowers(pr[n - 1:n], pi[n - 1:n], SUBLANE)
    row = jnp.arange(SUBLANE)[:, None]
    lanes = ar.shape[1]

    def tables(sign, keep, order):
        out = []
        for d in (1, 2, 4):
            out += [jnp.where(keep(d), qr[d - 1:d], 0.0), jnp.where(keep(d), sign * qi[d - 1:d], 0.0)]
        out += [jnp.concatenate([qr[r:r + 1] for r in order], 0), sign * jnp.concatenate([qi[r:r + 1] for r in order], 0),
                ar, sign * ai]
        return jnp.stack([jnp.broadcast_to(o, (SUBLANE, lanes)) for o in out])

    fwd = tables(1.0, lambda d: row >= d, list(range(SUBLANE)))
    rev = tables(-1.0, lambda d: row + d <= SUBLANE - 1, list(reversed(range(SUBLANE))))
    return fwd, rev, jnp.stack([pr, pi])


ADAMW_PART_BLOCK_BYTES = 2 * 1024 * 1024


def _adamw(w, parts, m, v, *, name):
    n, R, C = w.shape
    assert len(parts) == n
    P = parts[0].shape[0]
    tr = R
    while P * tr * C * parts[0].dtype.itemsize > ADAMW_PART_BLOCK_BYTES and tr % 16 == 0:
        tr //= 2
    c1 = 1.0 / (1.0 - ADAM_B1 ** ADAM_STEP)
    c2 = 1.0 / (1.0 - ADAM_B2 ** ADAM_STEP)

    def body(*refs):
        w_ref, m_ref, v_ref = refs[:3]
        p_refs = refs[3:3 + n]
        g_ref, d_ref, nm_ref, nv_ref = refs[3 + n:]
        layer = pl.program_id(0)
        for q, p_ref in enumerate(p_refs):
            @pl.when(layer == q)
            def _(p_ref=p_ref):
                g = p_ref[0].astype(F32)
                for s in range(1, P):
                    g = g + p_ref[s].astype(F32)
                nm = ADAM_B1 * m_ref[...] + (1.0 - ADAM_B1) * g
                nv = ADAM_B2 * v_ref[...] + (1.0 - ADAM_B2) * (g * g)
                g_ref[...] = g
                nm_ref[...] = nm
                nv_ref[...] = nv
                d_ref[...] = -ADAM_LR * ((nm * c1) / (jnp.sqrt(nv * c2) + ADAM_EPS) + ADAM_WD * w_ref[...])

    row = pl.BlockSpec((None, tr, C), lambda l, i: (l, i, 0))
    part_specs = [pl.BlockSpec((P, tr, C), lambda l, i, q=q: (0, jnp.where(l == q, i, 0), 0)) for q in range(n)]
    out = jax.ShapeDtypeStruct((n, R, C), F32)
    return pl.pallas_call(body, name=name, out_shape=(out, out, out, out), grid=(n, R // tr),
                          in_specs=[row, row, row] + part_specs, out_specs=(row, row, row, row),
                          compiler_params=_params("arbitrary", "arbitrary"))(w, m, v, *parts)


def _all_gather(xs, axis, *, name):
    m = xs.shape[axis]
    out_shape = list(xs.shape)
    out_shape[axis] = N_DEV * m

    def body(x_ref, out_ref, send_sems, recv_sems, local_sem):
        x, y, c = _my_pos()
        me, sibling = (x, y, c), (x, y, 1 - c)
        chips = [(1 - x, y), (x, 1 - y), (1 - x, 1 - y)]

        def blk(px, py, pc):
            idx = [slice(None)] * 3
            idx[axis] = pl.ds((4 * px + 2 * py + pc) * m, m)
            return out_ref.at[tuple(idx)]

        def copy(k, block, to, src=None):
            return pltpu.make_async_remote_copy(src_ref=blk(*block) if src is None else src, dst_ref=blk(*block),
                                                send_sem=send_sems.at[k], recv_sem=recv_sems.at[k],
                                                device_id=to, device_id_type=MESH_ID)

        mine = pltpu.make_async_copy(x_ref, blk(*me), local_sem)
        mine.start()
        first = [copy(0, me, sibling, src=x_ref)]
        first += [copy(1 + j, me, (*chip, c), src=x_ref) for j, chip in enumerate(chips)]
        for cp in first:
            cp.start()
        passed = [copy(4 + j, (*chip, c), sibling) for j, chip in enumerate(chips)]
        for j, chip in enumerate(chips):
            copy(1 + j, (*chip, c), me).wait_recv()
            passed[j].start()
        copy(0, sibling, me).wait_recv()
        for j, chip in enumerate(chips):
            copy(4 + j, (*chip, 1 - c), me).wait_recv()
        for cp in first + passed:
            cp.wait_send()
        mine.wait()

    hbm = pl.BlockSpec(memory_space=pl.ANY)
    return pl.pallas_call(body, name=name, out_shape=jax.ShapeDtypeStruct(tuple(out_shape), xs.dtype),
                          in_specs=[hbm], out_specs=hbm,
                          scratch_shapes=[pltpu.SemaphoreType.DMA((N_DEV - 1,)), pltpu.SemaphoreType.DMA((N_DEV - 1,)),
                                          pltpu.SemaphoreType.DMA],
                          compiler_params=pltpu.CompilerParams(has_side_effects=True))(xs)


NEAR_PEERS = (1, 2, 4, 6)
RELAY_PEERS = (2, 4, 6)


def _block(ref, axis, idx, m):
    return ref.at[pl.ds(idx * m, m), :] if axis == 0 else ref.at[:, pl.ds(idx * m, m)]


def _exchange_copies(metas, src_refs, zone_refs, send_sems, recv_sems, base, phase):
    x, y, c = _my_pos()
    me = 4 * x + 2 * y + c

    def place(r):
        pos = (1 - x if r & 4 else x, 1 - y if r & 2 else y, 1 - c if r & 1 else c)
        return pos, 4 * pos[0] + 2 * pos[1] + pos[2]

    def copies(r, to, src, dst, arrival):
        return tuple(pltpu.make_async_remote_copy(src_ref=src, dst_ref=d, send_sem=send_sems.at[base + r - 1],
                                                  recv_sem=recv_sems.at[base + r - 1], device_id=to,
                                                  device_id_type=MESH_ID) for d in (dst, arrival))

    pairs, own = [], []
    if phase == 'relay':
        sibling, _ = place(1)
        for r in RELAY_PEERS:
            held, comes = place(r)[1], place(r | 1)[1]
            for (kind, axis, m), z_ref in zip(metas, zone_refs):
                pairs.append(copies(r, sibling, _block(z_ref, axis, held, m), _block(z_ref, axis, held, m),
                                    _block(z_ref, axis, comes, m)))
        return pairs, own
    for r in (NEAR_PEERS if phase == 'near' else range(1, N_DEV)):
        pos, peer = place(r)
        for (kind, axis, m), s_ref, z_ref in zip(metas, src_refs, zone_refs):
            if kind == 'gather':
                pairs.append(copies(r, pos, s_ref, _block(z_ref, axis, me, m), _block(z_ref, axis, peer, m)))
            else:
                pairs.append(copies(r, pos, _block(s_ref, axis, peer, m), z_ref.at[me], z_ref.at[peer]))
    for (kind, axis, m), s_ref, z_ref in zip(metas, src_refs, zone_refs):
        src, dst = (s_ref, _block(z_ref, axis, me, m)) if kind == 'gather' else (_block(s_ref, axis, me, m), z_ref.at[me])
        own.append(pltpu.make_async_copy(src, dst, recv_sems.at[base + N_DEV - 1]))
    return pairs, own


def _exchange_start(groups, after, *, name, relayed=False):
    flat = [it for g in groups for it in g]
    n, ng = len(flat), len(groups)
    metas = [it[2] for it in flat]
    bounds = [(sum(len(g) for g in groups[:q]), sum(len(g) for g in groups[:q + 1])) for q in range(ng)]
    phase = 'near' if relayed else 'all'

    def body(*refs):
        src_refs = refs[:n]
        send_sems, recv_sems = refs[n + 1], refs[n + 2]
        zone_refs = refs[2 * n + 3:3 * n + 3]
        token = refs[-1]
        for q, (lo, hi) in enumerate(bounds):
            pairs, own = _exchange_copies(metas[lo:hi], src_refs[lo:hi], zone_refs[lo:hi], send_sems, recv_sems,
                                          q * N_DEV, phase)
            for outgoing, _ in pairs:
                outgoing.start()
            for cp in own:
                cp.start()
        token[...] = jnp.zeros_like(token)

    hbm = pl.BlockSpec(memory_space=pltpu.HBM)
    sem = pl.BlockSpec(memory_space=pltpu.SEMAPHORE)
    srcs = [it[0] for it in flat]
    res = pl.pallas_call(
        body, name=name,
        out_shape=(pltpu.SemaphoreType.DMA((ng * N_DEV,)), pltpu.SemaphoreType.DMA((ng * N_DEV,)),
                   *[pltpu.HBM(a.shape, a.dtype) for a in srcs], *[pltpu.HBM(it[1], it[0].dtype) for it in flat],
                   jax.ShapeDtypeStruct((SUBLANE, LANE), F32)),
        in_specs=[hbm] * n + [pl.BlockSpec(memory_space=pl.ANY)],
        out_specs=(sem, sem, *[hbm] * (2 * n), pl.BlockSpec(memory_space=pltpu.VMEM)),
        input_output_aliases={q: 2 + q for q in range(n)},
        compiler_params=pltpu.CompilerParams(has_side_effects=pltpu.SideEffectType.DATAFLOW_SIDE_EFFECTING),
    )(*[pltpu.with_memory_space_constraint(a, pltpu.HBM) for a in srcs], after)
    handles = [(res[0], res[1], q * N_DEV, phase, list(res[2 + lo:2 + hi]), list(res[2 + n + lo:2 + n + hi]),
                metas[lo:hi]) for q, (lo, hi) in enumerate(bounds)]
    return handles, res[-1]


def _exchange_wait(handle, after, *, name):
    send_sems, recv_sems, base, phase, srcs, zones, metas = handle
    ns, nz = len(srcs), len(zones)

    def body(*refs):
        src_refs, zone_refs = refs[:ns], refs[ns:ns + nz]
        s_sems, r_sems = refs[ns + nz], refs[ns + nz + 1]
        pairs, own = _exchange_copies(metas, src_refs, zone_refs, s_sems, r_sems, base, phase)
        for outgoing, incoming in pairs:
            outgoing.wait_send()
            incoming.wait_recv()
        for cp in own:
            cp.wait()

    hbm = pl.BlockSpec(memory_space=pltpu.HBM)
    sem = pl.BlockSpec(memory_space=pltpu.SEMAPHORE)
    arrays = srcs + zones
    res = pl.pallas_call(
        body, name=name,
        out_shape=tuple(pltpu.HBM(a.shape, a.dtype) for a in arrays),
        in_specs=[hbm] * (ns + nz) + [sem, sem, pl.BlockSpec(memory_space=pl.ANY)],
        out_specs=tuple([hbm] * (ns + nz)),
        input_output_aliases={q: q for q in range(ns + nz)},
        compiler_params=pltpu.CompilerParams(has_side_effects=pltpu.SideEffectType.DATAFLOW_SIDE_EFFECTING),
    )(*arrays, send_sems, recv_sems, after)
    return list(res[ns:])


def _exchange_relay(handle, after, *, name):
    metas = handle[6]
    zones = _exchange_wait(handle, after, name=name + "_in")
    nz = len(zones)

    def body(*refs):
        zone_refs = refs[:nz]
        send_sems, recv_sems = refs[nz], refs[nz + 1]
        pairs, _ = _exchange_copies(metas, (), zone_refs, send_sems, recv_sems, 0, 'relay')
        for outgoing, _ in pairs:
            outgoing.start()
        refs[-1][...] = jnp.zeros_like(refs[-1])

    hbm = pl.BlockSpec(memory_space=pltpu.HBM)
    sem = pl.BlockSpec(memory_space=pltpu.SEMAPHORE)
    res = pl.pallas_call(
        body, name=name + "_out",
        out_shape=(pltpu.SemaphoreType.DMA((N_DEV,)), pltpu.SemaphoreType.DMA((N_DEV,)),
                   *[pltpu.HBM(z.shape, z.dtype) for z in zones], jax.ShapeDtypeStruct((SUBLANE, LANE), F32)),
        in_specs=[hbm] * nz, out_specs=(sem, sem, *[hbm] * nz, pl.BlockSpec(memory_space=pltpu.VMEM)),
        input_output_aliases={q: 2 + q for q in range(nz)},
        compiler_params=pltpu.CompilerParams(has_side_effects=pltpu.SideEffectType.DATAFLOW_SIDE_EFFECTING),
    )(*zones)
    return (res[0], res[1], 0, 'relay', [], list(res[2:2 + nz]), metas), res[-1][0:1, 0:1]


def _pad_rows(a, rows):
    return jnp.pad(a, ((0, rows - a.shape[0]), (0, 0)))


PACK_ROWS = 2 * SUBLANE


def _rows(a):
    flat = a.reshape(-1).astype(F32)
    pad = -flat.shape[0] % (PACK_ROWS * LANE)
    return (jnp.pad(flat, (0, pad)) if pad else flat).reshape(-1, LANE)


def _pack_rows(arrays):
    return jnp.concatenate([_rows(a) for a in arrays], 0)


def _unpack_rows(t, shapes):
    out, off = [], 0
    for shp in shapes:
        size = math.prod(shp)
        rows = -(-size // (PACK_ROWS * LANE)) * PACK_ROWS
        out.append(t[off:off + rows].reshape(-1)[:size].reshape(shp))
        off += rows
    return out


def _stat_row(st, r):
    return st[r:r + 1, :]


def kernel(x, c, ada_w, ada_b, norm1_g, norm2_g, ff_w1, ff_w2, final_g, conv_w_in, conv_w, conv_b, conv_w_out, ssm_w_in, ssm_a_re, ssm_a_im, ssm_log_dt, ssm_b_re, ssm_b_im, ssm_c_re, ssm_c_im, ssm_d, ssm_glu_w, ssm_glu_b, ssm_w_out, sg_w_in, sg_v_g, sg_w_s, sg_b_s, sg_w_out, loss_target, m_ada_w, m_ada_b, m_norm1_g, m_norm2_g, m_ff_w1, m_ff_w2, m_final_g, m_conv_w_in, m_conv_w, m_conv_b, m_conv_w_out, m_ssm_w_in, m_ssm_a_re, m_ssm_a_im, m_ssm_log_dt, m_ssm_b_re, m_ssm_b_im, m_ssm_c_re, m_ssm_c_im, m_ssm_d, m_ssm_glu_w, m_ssm_glu_b, m_ssm_w_out, m_sg_w_in, m_sg_v_g, m_sg_w_s, m_sg_b_s, m_sg_w_out, v_ada_w, v_ada_b, v_norm1_g, v_norm2_g, v_ff_w1, v_ff_w2, v_final_g, v_conv_w_in, v_conv_w, v_conv_b, v_conv_w_out, v_ssm_w_in, v_ssm_a_re, v_ssm_a_im, v_ssm_log_dt, v_ssm_b_re, v_ssm_b_im, v_ssm_c_re, v_ssm_c_im, v_ssm_d, v_ssm_glu_w, v_ssm_glu_b, v_ssm_w_out, v_sg_w_in, v_sg_v_g, v_sg_w_s, v_sg_b_s, v_sg_w_out):
    P = dict(zip(INPUTS, (x, c, ada_w, ada_b, norm1_g, norm2_g, ff_w1, ff_w2, final_g, conv_w_in, conv_w, conv_b, conv_w_out, ssm_w_in, ssm_a_re, ssm_a_im, ssm_log_dt, ssm_b_re, ssm_b_im, ssm_c_re, ssm_c_im, ssm_d, ssm_glu_w, ssm_glu_b, ssm_w_out, sg_w_in, sg_v_g, sg_w_s, sg_b_s, sg_w_out, loss_target, m_ada_w, m_ada_b, m_norm1_g, m_norm2_g, m_ff_w1, m_ff_w2, m_final_g, m_conv_w_in, m_conv_w, m_conv_b, m_conv_w_out, m_ssm_w_in, m_ssm_a_re, m_ssm_a_im, m_ssm_log_dt, m_ssm_b_re, m_ssm_b_im, m_ssm_c_re, m_ssm_c_im, m_ssm_d, m_ssm_glu_w, m_ssm_glu_b, m_ssm_w_out, m_sg_w_in, m_sg_v_g, m_sg_w_s, m_sg_b_s, m_sg_w_out, v_ada_w, v_ada_b, v_norm1_g, v_norm2_g, v_ff_w1, v_ff_w2, v_final_g, v_conv_w_in, v_conv_w, v_conv_b, v_conv_w_out, v_ssm_w_in, v_ssm_a_re, v_ssm_a_im, v_ssm_log_dt, v_ssm_b_re, v_ssm_b_im, v_ssm_c_re, v_ssm_c_im, v_ssm_d, v_ssm_glu_w, v_ssm_glu_b, v_ssm_w_out, v_sg_w_in, v_sg_v_g, v_sg_w_s, v_sg_b_s, v_sg_w_out)))
    L, D = x.shape[1], x.shape[2]
    me = _my_index()
    xs = x[0]
    tgt = loss_target[0]
    n_conv = conv_w_in.shape[0]

    def gather_item(shard, axis):
        full = tuple(N_DEV * s if a == axis else s for a, s in enumerate(shard.shape))
        return shard, full, ('gather', axis, shard.shape[axis])

    def mixer_shards(i):
        kind, j = i % 3, i // 3
        if kind == 0:
            return [(conv_w_in[j], 1), (conv_w_out[j], 0)]
        if kind == 1:
            return [(ssm_w_in[j], 0), (ssm_glu_w[j], 0), (ssm_w_out[j], 0)]
        return [(sg_w_in[j], 1), (sg_w_out[j], 0)]

    c_act = c * (1.0 / (1.0 + jnp.exp(-c)))
    vec_rows = jnp.concatenate([c_act.reshape(D // LANE, LANE), conv_w.reshape(-1, LANE), conv_b.reshape(-1, LANE),
                                sg_v_g.reshape(-1, LANE)], 0)
    n_vec = vec_rows.shape[0]
    vec_all = _all_gather(_pad_rows(vec_rows, 24)[None], 0, name="gather_vectors")
    c_all = vec_all[:, :D // LANE, :].reshape(N_DEV, D)
    sharded_full = vec_all[:, D // LANE:n_vec, :].transpose(1, 0, 2).reshape(n_vec - D // LANE, D)
    conv_w_full = sharded_full[:3 * n_conv].reshape(n_conv, 3, D)
    conv_b_full = sharded_full[3 * n_conv:4 * n_conv]
    sg_vg_full = sharded_full[4 * n_conv:4 * n_conv + 1]

    c_pad = _pad_rows(c_all, LANE)
    ncol = ada_w.shape[2]
    mod_part = jnp.stack([_mm(c_pad, ada_w[i], name=f"ada_fwd{i}")[:N_DEV] for i in range(DEPTH)])
    mod_all = _all_gather(mod_part.reshape(1, DEPTH * N_DEV, ncol), 0, name="gather_mod")
    mod_all = mod_all.reshape(N_DEV, DEPTH, N_DEV, ncol)
    mod_me = lax.dynamic_index_in_dim(mod_all, me, 2, keepdims=False)
    mod = mod_me.transpose(1, 0, 2).reshape(DEPTH, N_DEV * ncol) + ada_b
    gathers, gather_token = _exchange_start(
        [[gather_item(w.astype(BF16), ax) for w, ax in shards]
         for i in range(DEPTH) for shards in (mixer_shards(i), [(ff_w1[i], 1), (ff_w2[i], 0)])],
        mod, name="gather_start", relayed=True)
    mod = mod + gather_token[0:1, 0:1]
    relayed = [None] * len(gathers)
    relayed[0], sent = _exchange_relay(gathers[0], mod, name="gather_mix_relay0")

    s5_args = (ssm_a_re[0], ssm_a_im[0], ssm_log_dt[0], ssm_b_re[0], ssm_b_im[0], ssm_c_re[0], ssm_c_im[0])
    (abar_re, abar_im, bblk_re, bblk_im, cblk_re, cblk_im), s5_vjp = jax.vjp(_s5_prep, *s5_args)
    pw_fwd, pw_rev, pos_fwd = _s5_power_tables(abar_re, abar_im, S5_CHUNK // SUBLANE)
    s5_w = tuple(t.astype(BF16) for t in (bblk_re, bblk_im, cblk_re, cblk_im))
    causal = jnp.tril(jnp.ones((SG_CHUNK, SG_CHUNK), dtype=bool))
    ws_m = jnp.where(causal[None], sg_w_s[0], 0.0)
    ws_b = ws_m.astype(BF16)
    wst_b = ws_m.transpose(0, 2, 1).astype(BF16)
    bsb = jnp.broadcast_to(sg_b_s[0][:, :, None], (SG_HEADS, SG_CHUNK, LANE))

    saved = []
    xa = xs
    mods = [[mod[i:i + 1, q * D:(q + 1) * D] for q in range(6)] for i in range(DEPTH)]
    wn1s = [norm1_g[i:i + 1] * (1.0 + mods[i][1]) for i in range(DEPTH)]
    h1 = _normmod_fwd(xa, wn1s[0], mods[0][0] + sent, name="norm1_fwd0")
    for i in range(DEPTH):
        kind, j = i % 3, i // 3
        sh1, sc1, g1, sh2, sc2, g2 = mods[i]
        wn1 = wn1s[i]
        wn2 = norm2_g[i:i + 1] * (1.0 + sc2)
        S = dict(x_in=xa, g1=g1, g2=g2, sc1=sc1, sc2=sc2, wn1=wn1, wn2=wn2)
        w_mix = _exchange_wait(relayed[2 * i], h1, name=f"gather_mix_wait{i}")
        S['h1'] = h1
        if kind == 0:
            bcx = _mm(h1, w_mix[0], name=f"conv_in{i}", out_dtypes=(BF16,), bm=2048)
            wb = _pad_rows(jnp.concatenate([conv_w_full[j], conv_b_full[j:j + 1]], 0), SUBLANE)
            pb = _conv_fwd(bcx, wb, name=f"conv_mix{i}")
            S.update(bcx=bcx, wb=wb, pb=pb)
        elif kind == 1:
            u = _mm(h1, w_mix[0], name=f"ssm_in{i}")
            sre, sim, ypre, yg = _s5_fwd(u, *s5_w, pw_fwd, pos_fwd, ssm_d, name=f"s5_scan{i}")

            def glu_epi(acc, yv, bias):
                t = acc + bias
                return yv * (1.0 / (1.0 + jnp.exp(-t))), t

            pb, tt = _mm(yg, w_mix[1], name=f"ssm_glu{i}", out_dtypes=(BF16, F32), epi=glu_epi,
                         extras=[(yg, 'mn'), (ssm_glu_b, 'n')])
            S.update(u=u, sre=sre, sim=sim, ypre=ypre, yg=yg, pb=pb, tt=tt)
        else:
            uv = _mm(h1, w_mix[0], name=f"sg_in{i}", bm=2048)
            pb = _sg_fwd(uv, sg_vg_full, ws_b, bsb, name=f"sg_mix{i}")
            S.update(uv=uv, pb=pb)
        relayed[2 * i + 1], sent = _exchange_relay(gathers[2 * i + 1], pb, name=f"gather_ff_relay{i}")
        x_mid, y_mix, h2 = _mm(pb, w_mix[-1], name=f"mix_out{i}", out_dtypes=(F32, BF16, BF16), epi=_epi_residual_norm,
                               extras=[(xa, 'mn'), (g1 + sent, 'n'), (wn2, 'n'), (sh2, 'n')])
        w1_full, w2_full = _exchange_wait(relayed[2 * i + 1], h2, name=f"gather_ff_wait{i}")
        S.update(w_mix=w_mix, w1=w1_full, w2=w2_full)
        ra = _mm(h2, w1_full, name=f"ff_up{i}", out_dtypes=(BF16,), epi=lambda acc: (jnp.maximum(acc, 0.0),), bm=2048)
        if i + 1 < DEPTH:
            relayed[2 * i + 2], sent = _exchange_relay(gathers[2 * i + 2], ra, name=f"gather_mix_relay{i + 1}")
            xa, f_out, h1 = _mm(ra, w2_full, name=f"ff_down{i}", out_dtypes=(F32, BF16, BF16), a_fn=_square,
                                epi=_epi_residual_norm, bm=256, bk=w2_full.shape[0],
                                extras=[(x_mid, 'mn'), (g2 + sent, 'n'), (wn1s[i + 1], 'n'), (mods[i + 1][0], 'n')])
        else:
            f_out = None
            dx, dfb, st = _mm(ra, w2_full, name=f"ff_down{i}", out_dtypes=(F32, BF16), epi=_epi_loss_head, a_fn=_square,
                              n_stats=3, bm=256, bk=w2_full.shape[0],
                              extras=[(x_mid, 'mn'), (g2, 'n'), (tgt, 'mn'), (final_g[None], 'n')])
        S.update(x_mid=x_mid, y_mix=y_mix, h2=h2, ra=ra, f_out=f_out)
        saved.append(S)

    loss_tile = st[:, :LANE]
    d_final_g = _stat_row(st, 1)
    dg2_next = _stat_row(st, 2)

    def scatter_item(g, axis):
        m = g.shape[axis] // N_DEV
        shard = tuple(m if a == axis else s for a, s in enumerate(g.shape))
        return g, (N_DEV,) + shard, ('scatter', axis, m)

    dmod = [None] * DEPTH
    dn1g, dn2g = [None] * DEPTH, [None] * DEPTH
    d_conv_w, d_conv_b = [None] * n_conv, [None] * n_conv
    ff_sent, mix_sent = [None] * DEPTH, [None] * DEPTH
    small = {}
    for i in reversed(range(DEPTH)):
        kind, j = i % 3, i // 3
        S = saved[i]
        w_mix = S['w_mix']
        dg2 = dg2_next
        da = _mm(dfb, S['w2'], tb=True, name=f"ff_down_bwd{i}", out_dtypes=(BF16,), bm=2048,
                 epi=lambda acc, rav: (acc * (2.0 * rav.astype(F32)),), extras=[(S['ra'], 'mn')])
        dw2 = _wgrad(S['ra'], dfb, name=f"ff_w2_grad{i}", a_fn=_square, bm=256, bn=1024)
        dw1 = _wgrad(S['h2'], da, name=f"ff_w1_grad{i}")
        (ff_sent[i],), token = _exchange_start([[scatter_item(dw1, 1), scatter_item(dw2, 0)]], dx,
                                               name=f"ff_grads_start{i}")
        dx_mid, dyb, st2 = _mm(da, S['w1'], tb=True, name=f"ff_up_bwd{i}", out_dtypes=(F32, BF16), bm=256,
                               bk=da.shape[1], epi=_epi_norm_bwd(True), n_stats=3,
                               extras=[(S['x_mid'], 'mn'), (S['wn2'] + token[0:1, 0:1], 'n'), (dx, 'mn'),
                                       (S['y_mix'], 'mn'), (S['g1'], 'n')])
        dsc2 = _stat_row(st2, 0) * norm2_g[i:i + 1]
        dn2g[i] = _stat_row(st2, 0) * (1.0 + S['sc2'])
        dsh2 = _stat_row(st2, 1)
        dg1 = _stat_row(st2, 2)
        if kind == 0:
            dp = _mm(dyb, w_mix[1], tb=True, name=f"conv_out_bwd{i}", out_dtypes=(BF16,))
            d_cwo = _wgrad(S['pb'], dyb, name=f"conv_w_out_grad{i}")
            dbcx, stc = _conv_bwd(dp, S['bcx'], S['wb'], name=f"conv_mix_bwd{i}")
            d_conv_w[j] = stc[0:3]
            d_conv_b[j] = stc[3:4]
            dh_operand, dh_name = dbcx, "conv_in_bwd"
            d_cwi = _wgrad(S['h1'], dbcx, name=f"conv_w_in_grad{i}")
            mix_grads = [scatter_item(d_cwi, 1), scatter_item(d_cwo, 0)]
        elif kind == 1:
            dtb, dya, stg = _mm(dyb, w_mix[2], tb=True, name=f"ssm_out_bwd{i}", out_dtypes=(BF16, F32), bm=512,
                                epi=_epi_glu_bwd, n_stats=1, extras=[(S['yg'], 'mn'), (S['tt'], 'mn')])
            d_ssm_out = _wgrad(S['pb'], dyb, name=f"ssm_w_out_grad{i}")
            dypre = _mm(dtb, w_mix[1], tb=True, name=f"ssm_glu_in_bwd{i}",
                        epi=lambda acc, a, yp: ((a + acc) * _gelu_grad(yp),),
                        extras=[(dya, 'mn'), (S['ypre'], 'mn')])
            d_glu = _wgrad(S['yg'], dtb, name=f"ssm_glu_w_grad{i}", bm=512)
            dub, dbre, dbim, dcre, dcim, ga, dd = _s5_bwd(dypre, S['u'], S['sre'], S['sim'], *s5_w, pw_rev, pos_fwd, ssm_d,
                                                           name=f"s5_scan_bwd{i}")
            dh_operand, dh_name = dub, "ssm_in_bwd"
            d_ssm_in = _wgrad(S['h1'], dub, name=f"ssm_w_in_grad{i}")
            da_re, da_im, dlog_dt, db_re, db_im, dc_re, dc_im = s5_vjp((ga[0:1], ga[1:2], dbre, dbim, dcre, dcim))
            s5_small = _pack_rows([da_re, da_im, dlog_dt, db_re, db_im, dc_re, dc_im, dd[0], stg[0]])
            mix_grads = [scatter_item(d_ssm_in, 0), scatter_item(d_glu, 0), scatter_item(d_ssm_out, 0),
                         gather_item(s5_small.astype(BF16), 0)]
        else:
            dp = _mm(dyb, w_mix[1], tb=True, name=f"sg_out_bwd{i}")
            d_sgo = _wgrad(S['pb'], dyb, name=f"sg_w_out_grad{i}")
            duv, dws, dbs, stv = _sg_bwd(dp, S['uv'], sg_vg_full, ws_b, wst_b, bsb, name=f"sg_mix_bwd{i}")
            dh_operand, dh_name = duv, "sg_in_bwd"
            d_sgi = _wgrad(S['h1'], duv, name=f"sg_w_in_grad{i}")
            sg_small = _pack_rows([jnp.where(causal[None], dws, 0.0), jnp.sum(dbs, axis=-1)])
            d_sg_vg = stv[0:1]
            mix_grads = [scatter_item(d_sgi, 1), scatter_item(d_sgo, 0), gather_item(sg_small.astype(BF16), 0)]
        wn1 = S['wn1']
        gate = []
        if i > 0:
            (mix_sent[i],), token = _exchange_start([mix_grads], dx_mid, name=f"mix_grads_start{i}")
            wn1 = wn1 + token[0:1, 0:1]
            gate = [(saved[i - 1]['f_out'], 'mn'), (saved[i - 1]['g2'], 'n')]
        res = _mm(dh_operand, w_mix[0], tb=True, name=f"{dh_name}{i}", out_dtypes=(F32, BF16) if i > 0 else (F32,),
                  bm=512, bk=w_mix[0].shape[1], epi=_epi_norm_bwd(i > 0), n_stats=3 if i > 0 else 2,
                  extras=[(S['x_in'], 'mn'), (wn1, 'n'), (dx_mid, 'mn')] + gate)
        if i > 0:
            dx, dfb, st1 = res
            dg2_next = _stat_row(st1, 2)
        else:
            dx, st1 = res
        dsc1 = _stat_row(st1, 0) * norm1_g[i:i + 1]
        dn1g[i] = _stat_row(st1, 0) * (1.0 + S['sc1'])
        dsh1 = _stat_row(st1, 1)
        dmod[i] = jnp.concatenate([dsh1, dsc1, dg1, dsh2, dsc2, dg2], 1)
    grad_x = dx[None]

    out = {}

    def small_group(names, parts, label):
        shapes = [P[n].shape for n in names]
        w, m, v = (_pack_rows([P[pre + n] for n in names])[None] for pre in ('', 'm_', 'v_'))
        res = [_unpack_rows(t[0], shapes) for t in _adamw(w, [parts], m, v, name=label)]
        for q, n in enumerate(names):
            out[n] = tuple(r[q] for r in res)

    small.update(ada_b=jnp.concatenate(dmod, 0), norm1_g=jnp.concatenate(dn1g, 0), norm2_g=jnp.concatenate(dn2g, 0),
                 final_g=d_final_g, conv_w=jnp.stack(d_conv_w), conv_b=jnp.concatenate(d_conv_b, 0), sg_v_g=d_sg_vg)
    last_pack = _pack_rows([small[n] for n in LAST_SMALL + SMALL_SHARD])
    n_last = _pack_rows([P[n] for n in LAST_SMALL]).shape[0]
    n_pack = last_pack.shape[0]
    pack_all = _all_gather(jnp.concatenate([last_pack, loss_tile], 0)[None], 0, name="gather_small_grads")
    loss = jnp.sum(pack_all[:, n_pack, 0])
    (mix_sent[0],), last_token = _exchange_start([mix_grads], pack_all, name="mix_grads_start0")
    small_group(LAST_SMALL, pack_all[:, :n_last], "adamw_small")
    sh_rows = (n_pack - n_last) // N_DEV
    sh_parts = pack_all[:, n_last:n_pack].reshape(N_DEV, sh_rows, N_DEV, LANE)
    sh_parts = lax.dynamic_index_in_dim(sh_parts, me, 2, keepdims=False)
    sh_parts = jnp.pad(sh_parts, ((0, 0), (0, 16 - sh_rows), (0, 0)))

    def pack_shard(prefix):
        return _pad_rows(jnp.concatenate([P[prefix + n].reshape(-1, LANE) for n in SMALL_SHARD], 0), 16)[None]

    sg_, sd_, sm_, sv_ = _adamw(pack_shard(''), [sh_parts], pack_shard('m_'), pack_shard('v_'), name="adamw_channel")
    off = 0
    for n in SMALL_SHARD:
        rows = math.prod(P[n].shape) // LANE
        out[n] = tuple(t[0, off:off + rows].reshape(P[n].shape) for t in (sg_, sd_, sm_, sv_))
        off += rows

    dmod_all = pack_all[:, :DEPTH * 6 * D // LANE].reshape(N_DEV, DEPTH, 6 * D)
    dmod_cols = lax.dynamic_slice_in_dim(dmod_all, me * ncol, ncol, 2)
    g_ada = [_mm(c_pad, _pad_rows(dmod_cols[:, i], LANE), ta=True, name=f"ada_w_grad{i}")[None] for i in range(DEPTH)]

    def big(name, parts):
        res = _adamw(P[name], parts, P['m_' + name], P['v_' + name], name="adamw_" + name)
        out[name] = res
        return res[1]

    ff_parts = [_exchange_wait(ff_sent[i], last_token, name=f"ff_grads_wait{i}") for i in range(DEPTH)]
    mix_parts = [None] + [_exchange_wait(mix_sent[i], last_token, name=f"mix_grads_wait{i}") for i in range(1, DEPTH)]
    big('ada_w', g_ada)
    big('ff_w1', [p[0] for p in ff_parts])
    big('ff_w2', [p[1] for p in ff_parts])
    done = big('sg_w_in', [mix_parts[2][0]])
    mix_parts[0] = _exchange_wait(mix_sent[0], done, name="mix_grads_wait0")
    big('conv_w_in', [mix_parts[i][0] for i in range(DEPTH) if i % 3 == 0])
    row_names = ['conv_w_out', 'ssm_w_in', 'ssm_glu_w', 'ssm_w_out', 'sg_w_out']
    row_parts = ([mix_parts[i][1] for i in range(DEPTH) if i % 3 == 0] + mix_parts[1][:3] + [mix_parts[2][1]])
    small_group(S5_SMALL, mix_parts[1][3].reshape(N_DEV, -1, LANE), "adamw_s5")
    small_group(SG_SMALL, mix_parts[2][2].reshape(N_DEV, -1, LANE), "adamw_sg")
    row_w, row_m, row_v = (jnp.concatenate([P[pre + n] for n in row_names], 0) for pre in ('', 'm_', 'v_'))
    rw = _adamw(row_w, row_parts, row_m, row_v, name="adamw_row_sharded")
    off = 0
    for n in row_names:
        cnt = P[n].shape[0]
        out[n] = tuple(t[off:off + cnt] for t in rw)
        off += cnt

    return (loss, grad_x, *[out[n][0] for n in WEIGHTS], *[out[n][1] for n in WEIGHTS],
            *[out[n][2] for n in WEIGHTS], *[out[n][3] for n in WEIGHTS])
```

```python
import math

import jax
import jax.numpy as jnp
from jax import lax
from jax.experimental import pallas as pl
from jax.experimental.pallas import tpu as pltpu

F32 = jnp.float32
BF16 = jnp.bfloat16

N_DEV = 8
MESH_ID = pl.DeviceIdType.MESH
DEPTH = 4
EPS = 1e-6
S5_GROUPS, S5_GROUP, S5_STATE = 64, 16, 64
S5_LANES = S5_GROUPS * S5_STATE
S5_BLOCKS = 8
S5_CHUNK = 512
SG_HEADS, SG_CHUNK = 8, 128
LANE = 128
SUBLANE = 8
VMEM_LIMIT = 48 * 1024 * 1024
ADAM_LR, ADAM_B1, ADAM_B2, ADAM_EPS, ADAM_WD, ADAM_STEP = 0.001, 0.9, 0.999, 1e-08, 0.01, 10
GELU_C = math.sqrt(2.0 / math.pi)
GELU_A = 0.044715

WEIGHTS = ['ada_w', 'ada_b', 'norm1_g', 'norm2_g', 'ff_w1', 'ff_w2', 'final_g', 'conv_w_in', 'conv_w', 'conv_b',
           'conv_w_out', 'ssm_w_in', 'ssm_a_re', 'ssm_a_im', 'ssm_log_dt', 'ssm_b_re', 'ssm_b_im', 'ssm_c_re',
           'ssm_c_im', 'ssm_d', 'ssm_glu_w', 'ssm_glu_b', 'ssm_w_out', 'sg_w_in', 'sg_v_g', 'sg_w_s', 'sg_b_s',
           'sg_w_out']
INPUTS = ['x', 'c'] + WEIGHTS + ['loss_target'] + ['m_' + n for n in WEIGHTS] + ['v_' + n for n in WEIGHTS]
S5_SMALL = ['ssm_a_re', 'ssm_a_im', 'ssm_log_dt', 'ssm_b_re', 'ssm_b_im', 'ssm_c_re', 'ssm_c_im', 'ssm_d', 'ssm_glu_b']
SG_SMALL = ['sg_w_s', 'sg_b_s']
LAST_SMALL = ['ada_b', 'norm1_g', 'norm2_g', 'final_g']
SMALL_SHARD = ['conv_w', 'conv_b', 'sg_v_g']


def _params(*sem):
    return pltpu.CompilerParams(dimension_semantics=sem or None, vmem_limit_bytes=VMEM_LIMIT)


def _my_pos():
    return lax.axis_index("x"), lax.axis_index("y"), lax.axis_index("c")


def _my_index():
    x, y, c = _my_pos()
    return 4 * x + 2 * y + c


def _mm(a, b, *, name, ta=False, tb=False, out_dtypes=(F32,), epi=None, extras=(), a_fn=None, n_stats=0, bm=1024,
        bn=1024, bk=1024):
    a_chunks = a.shape[0] if a.ndim == 3 else 0
    b_chunks = b.shape[0] if b.ndim == 3 else 0
    assert not (a_chunks and ta) and not (b_chunks and tb)
    if a_chunks:
        m, k = a.shape[1], a_chunks * a.shape[2]
        bk = k
    else:
        m, k = (a.shape[1], a.shape[0]) if ta else a.shape
    if b_chunks:
        k2, n = b.shape[1], b_chunks * b.shape[2]
        bn = min(bn, b.shape[2])
    else:
        k2, n = (b.shape[1], b.shape[0]) if tb else b.shape
    assert k == k2, (a.shape, b.shape, ta, tb)
    bm, bn, bk = min(bm, m), min(bn, n), min(bk, k)
    assert m % bm == 0 and n % bn == 0 and k % bk == 0, (m, n, k, bm, bn, bk)
    nk = k // bk
    assert nk == 1 or n_stats == 0
    n_ex, n_out = len(extras), len(out_dtypes)
    dims = (((0 if ta else 1,), (1 if tb else 0,)), ((), ()))

    def body(*refs):
        a_ref, b_ref = refs[0], refs[1]
        ex_refs = refs[2:2 + n_ex]
        out_refs = refs[2 + n_ex:2 + n_ex + n_out]

        def finish(acc):
            outs = epi(acc, *[r[...] for r in ex_refs]) if epi is not None else (acc,)
            for r, o in zip(out_refs, outs[:n_out]):
                r[...] = o.astype(r.dtype)
            if n_stats:
                st_ref = refs[2 + n_ex + n_out]

                @pl.when(pl.program_id(0) == 0)
                def _():
                    st_ref[...] = jnp.zeros_like(st_ref)

                for q, row in enumerate(outs[n_out:]):
                    st_ref[q:q + 1, :] += row

        av = jnp.concatenate([a_ref[t] for t in range(a_chunks)], axis=1) if a_chunks else a_ref[...]
        if a_fn is not None:
            av = a_fn(av)
        part = lax.dot_general(av.astype(BF16), b_ref[...].astype(BF16), dims, preferred_element_type=F32)
        if nk == 1:
            finish(part)
            return
        acc_ref = refs[-1]
        kk = pl.program_id(2)

        @pl.when(kk == 0)
        def _():
            acc_ref[...] = part

        @pl.when(kk > 0)
        def _():
            acc_ref[...] += part

        @pl.when(kk == nk - 1)
        def _():
            finish(acc_ref[...])

    if a_chunks:
        a_spec = pl.BlockSpec((a_chunks, bm, a.shape[2]), lambda i, j, q: (0, i, 0))
    elif ta:
        a_spec = pl.BlockSpec((bk, bm), lambda i, j, q: (q, i))
    else:
        a_spec = pl.BlockSpec((bm, bk), lambda i, j, q: (i, q))
    if b_chunks:
        per = b.shape[2] // bn
        b_spec = pl.BlockSpec((None, bk, bn), lambda i, j, q: (j // per, q, j % per))
    elif tb:
        b_spec = pl.BlockSpec((bn, bk), lambda i, j, q: (j, q))
    else:
        b_spec = pl.BlockSpec((bk, bn), lambda i, j, q: (q, j))
    ex_specs = []
    for arr, kind in extras:
        if kind == 'mn':
            assert arr.shape == (m, n), (arr.shape, m, n)
            ex_specs.append(pl.BlockSpec((bm, bn), lambda i, j, q: (i, j)))
        else:
            assert arr.shape == (1, n), (arr.shape, n)
            ex_specs.append(pl.BlockSpec((1, bn), lambda i, j, q: (0, j)))
    out_shape = [jax.ShapeDtypeStruct((m, n), d) for d in out_dtypes]
    out_specs = [pl.BlockSpec((bm, bn), lambda i, j, q: (i, j)) for _ in out_dtypes]
    if n_stats:
        assert n_stats <= SUBLANE
        out_shape.append(jax.ShapeDtypeStruct((SUBLANE, n), F32))
        out_specs.append(pl.BlockSpec((SUBLANE, bn), lambda i, j, q: (0, j)))
    outs = pl.pallas_call(
        body, name=name, out_shape=tuple(out_shape), grid=(m // bm, n // bn, nk),
        in_specs=[a_spec, b_spec] + ex_specs, out_specs=tuple(out_specs),
        scratch_shapes=[pltpu.VMEM((bm, bn), F32)] if nk > 1 else [],
        compiler_params=_params(*(["arbitrary"] * 3 if n_stats else ["parallel", "parallel", "arbitrary"])),
    )(a, b, *[arr for arr, _ in extras])
    return outs if len(outs) > 1 else outs[0]


def _epi_residual_norm(acc, res, gate, w, sh):
    xn = res + gate * acc
    return xn, acc, xn * _rstd(xn) * w + sh


def _epi_norm_bwd(gated):
    def epi(dh, xv, w, dres, *gate):
        rstd = _rstd(xv)
        xn = xv * rstd
        dxn = dh * w
        dx = rstd * (dxn - xn * jnp.mean(dxn * xn, axis=-1, keepdims=True)) + dres
        stats = [jnp.sum(dh * xn, axis=0, keepdims=True), jnp.sum(dh, axis=0, keepdims=True)]
        if not gated:
            return (dx, *stats)
        yv, g = gate
        return (dx, dx * g, *stats, jnp.sum(dx * yv.astype(F32), axis=0, keepdims=True))
    return epi


def _epi_loss_head(f, x_mid, g, tgt, fg):
    xv = x_mid + g * f
    rstd = _rstd(xv)
    xn = xv * rstd
    err = xn * fg - tgt
    loss = 0.5 * jnp.sum(jnp.mean(err * err, axis=-1, keepdims=True))
    dout = err * (1.0 / xv.shape[-1])
    dxn = dout * fg
    dx = rstd * (dxn - xn * jnp.mean(dxn * xn, axis=-1, keepdims=True))
    return (dx, dx * g, jnp.full((1, xv.shape[-1]), loss, F32), jnp.sum(dout * xn, axis=0, keepdims=True),
            jnp.sum(dx * f, axis=0, keepdims=True))


def _epi_glu_bwd(dy2, yv, t):
    sig = 1.0 / (1.0 + jnp.exp(-t))
    dt = dy2 * yv * sig * (1.0 - sig)
    return dt, dy2 * sig, jnp.sum(dt, axis=0, keepdims=True)


def _wgrad(acts, cots, *, name, a_fn=None, bm=1024, bn=512):
    return _mm(acts, cots, ta=True, name=name, out_dtypes=(BF16,), a_fn=a_fn, bm=bm, bn=bn, bk=acts.shape[0])


def _square(a):
    af = a.astype(F32)
    return af * af


def _rstd(xv):
    return lax.rsqrt(jnp.mean(xv * xv, axis=-1, keepdims=True) + EPS)


def _normmod_fwd(x, w, sh, *, name, tm=512):
    L, D = x.shape

    def body(x_ref, w_ref, s_ref, h_ref):
        xv = x_ref[...]
        h_ref[...] = (xv * _rstd(xv) * w_ref[...] + s_ref[...]).astype(h_ref.dtype)

    row = pl.BlockSpec((tm, D), lambda i: (i, 0))
    vec = pl.BlockSpec((1, D), lambda i: (0, 0))
    return pl.pallas_call(body, name=name, out_shape=jax.ShapeDtypeStruct((L, D), BF16), grid=(L // tm,),
                          in_specs=[row, vec, vec], out_specs=row, compiler_params=_params("parallel"))(x, w, sh)


def _shift_down(v, k):
    row = lax.broadcasted_iota(jnp.int32, v.shape, 0)
    return jnp.where(row >= k, pltpu.roll(v, k, 0), 0.0)


def _shift_up(v, k):
    n = v.shape[0]
    row = lax.broadcasted_iota(jnp.int32, v.shape, 0)
    return jnp.where(row < n - k, pltpu.roll(v, n - k, 0), 0.0)


def _conv_views(L, D):
    return [pl.BlockSpec((L, LANE), lambda j, s=s: (0, s * (D // LANE) + j)) for s in range(3)]


def _conv_fwd(bcx, wb, *, name):
    L, D = bcx.shape[0], bcx.shape[1] // 3

    def body(b_ref, c_ref, x_ref, wb_ref, p_ref):
        z = c_ref[...].astype(F32) * x_ref[...].astype(F32)
        conv = (wb_ref[0:1, :] * _shift_down(z, 2) + wb_ref[1:2, :] * _shift_down(z, 1)
                + wb_ref[2:3, :] * z + wb_ref[3:4, :])
        p_ref[...] = (b_ref[...].astype(F32) * conv).astype(p_ref.dtype)

    col = pl.BlockSpec((L, LANE), lambda j: (0, j))
    return pl.pallas_call(body, name=name, out_shape=jax.ShapeDtypeStruct((L, D), BF16), grid=(D // LANE,),
                          in_specs=_conv_views(L, D) + [pl.BlockSpec((SUBLANE, LANE), lambda j: (0, j))],
                          out_specs=col, compiler_params=_params("parallel"))(bcx, bcx, bcx, wb)


def _conv_bwd(dp, bcx, wb, *, name):
    L, D = dp.shape

    def body(dp_ref, b_ref, c_ref, x_ref, wb_ref, d3_ref, st_ref):
        cv, xv = c_ref[...].astype(F32), x_ref[...].astype(F32)
        z = cv * xv
        z1, z2 = _shift_down(z, 1), _shift_down(z, 2)
        w0, w1, w2 = wb_ref[0:1, :], wb_ref[1:2, :], wb_ref[2:3, :]
        conv = w0 * z2 + w1 * z1 + w2 * z + wb_ref[3:4, :]
        dpv = dp_ref[...].astype(F32)
        d3_ref[0] = (dpv * conv).astype(d3_ref.dtype)
        dconv = dpv * b_ref[...].astype(F32)
        dz = w2 * dconv + w1 * _shift_up(dconv, 1) + w0 * _shift_up(dconv, 2)
        d3_ref[1] = (dz * xv).astype(d3_ref.dtype)
        d3_ref[2] = (dz * cv).astype(d3_ref.dtype)
        st_ref[...] = jnp.zeros_like(st_ref)
        st_ref[0:1, :] = jnp.sum(dconv * z2, axis=0, keepdims=True)
        st_ref[1:2, :] = jnp.sum(dconv * z1, axis=0, keepdims=True)
        st_ref[2:3, :] = jnp.sum(dconv * z, axis=0, keepdims=True)
        st_ref[3:4, :] = jnp.sum(dconv, axis=0, keepdims=True)

    col = pl.BlockSpec((L, LANE), lambda j: (0, j))
    vec = pl.BlockSpec((SUBLANE, LANE), lambda j: (0, j))
    return pl.pallas_call(body, name=name,
                          out_shape=(jax.ShapeDtypeStruct((3, L, D), BF16), jax.ShapeDtypeStruct((SUBLANE, D), F32)),
                          grid=(D // LANE,), in_specs=[col] + _conv_views(L, D) + [vec],
                          out_specs=(pl.BlockSpec((3, L, LANE), lambda j: (0, 0, j)), vec),
                          compiler_params=_params("parallel"))(dp, bcx, bcx, bcx, wb)


def _sg_fwd(uv, vg, ws, bsb, *, name, tr=512):
    L, D = uv.shape[0], uv.shape[1] // 2

    def body(uv_ref, vg_ref, ws_ref, bsb_ref, p_ref):
        for ci in range(tr // SG_CHUNK):
            rows = slice(ci * SG_CHUNK, (ci + 1) * SG_CHUNK)
            v = uv_ref[rows, D:2 * D]
            vn = (v * _rstd(v) * vg_ref[...]).astype(BF16)
            for h in range(SG_HEADS):
                cols = slice(h * LANE, (h + 1) * LANE)
                vm = jnp.dot(ws_ref[h], vn[:, cols], preferred_element_type=F32) + bsb_ref[h]
                p_ref[rows, cols] = (uv_ref[rows, cols] * vm).astype(p_ref.dtype)

    full3 = pl.BlockSpec((SG_HEADS, SG_CHUNK, LANE), lambda i: (0, 0, 0))
    return pl.pallas_call(body, name=name, out_shape=jax.ShapeDtypeStruct((L, D), BF16), grid=(L // tr,),
                          in_specs=[pl.BlockSpec((tr, 2 * D), lambda i: (i, 0)), pl.BlockSpec((1, D), lambda i: (0, 0)),
                                    full3, full3],
                          out_specs=pl.BlockSpec((tr, D), lambda i: (i, 0)),
                          compiler_params=_params("parallel"))(uv, vg, ws, bsb)


def _sg_bwd(dp, uv, vg, ws, wst, bsb, *, name, tr=512):
    L, D = dp.shape

    def body(dp_ref, uv_ref, vg_ref, ws_ref, wst_ref, bsb_ref, duv_ref, dws_ref, dbs_ref, st_ref, dvn_ref):
        i = pl.program_id(0)

        @pl.when(i == 0)
        def _():
            dws_ref[...] = jnp.zeros_like(dws_ref)
            dbs_ref[...] = jnp.zeros_like(dbs_ref)
            st_ref[...] = jnp.zeros_like(st_ref)

        for ci in range(tr // SG_CHUNK):
            rows = slice(ci * SG_CHUNK, (ci + 1) * SG_CHUNK)
            v = uv_ref[rows, D:2 * D]
            rstd = _rstd(v)
            vhat = v * rstd
            vn = (vhat * vg_ref[...]).astype(BF16)
            for h in range(SG_HEADS):
                cols = slice(h * LANE, (h + 1) * LANE)
                vm = jnp.dot(ws_ref[h], vn[:, cols], preferred_element_type=F32) + bsb_ref[h]
                dph = dp_ref[rows, cols]
                duv_ref[rows, cols] = (dph * vm).astype(duv_ref.dtype)
                dvm = dph * uv_ref[rows, cols]
                dbs_ref[h] += dvm
                dvmb = dvm.astype(BF16)
                dws_ref[h] += lax.dot_general(dvmb, vn[:, cols], (((1,), (1,)), ((), ())),
                                              preferred_element_type=F32)
                dvn_ref[rows, cols] = jnp.dot(wst_ref[h], dvmb, preferred_element_type=F32)
            dvn = dvn_ref[rows, :]
            gv = dvn * vg_ref[...]
            dv = rstd * (gv - vhat * jnp.mean(gv * vhat, axis=-1, keepdims=True))
            duv_ref[rows, D:2 * D] = dv.astype(duv_ref.dtype)
            st_ref[0:1, :] += jnp.sum(dvn * vhat, axis=0, keepdims=True)

    full3 = pl.BlockSpec((SG_HEADS, SG_CHUNK, LANE), lambda i: (0, 0, 0))
    acc3 = jax.ShapeDtypeStruct((SG_HEADS, SG_CHUNK, LANE), F32)
    return pl.pallas_call(
        body, name=name,
        out_shape=(jax.ShapeDtypeStruct((L, 2 * D), BF16), acc3, acc3, jax.ShapeDtypeStruct((SUBLANE, D), F32)),
        grid=(L // tr,),
        in_specs=[pl.BlockSpec((tr, D), lambda i: (i, 0)), pl.BlockSpec((tr, 2 * D), lambda i: (i, 0)),
                  pl.BlockSpec((1, D), lambda i: (0, 0)), full3, full3, full3],
        out_specs=(pl.BlockSpec((tr, 2 * D), lambda i: (i, 0)), full3, full3,
                   pl.BlockSpec((SUBLANE, D), lambda i: (0, 0))),
        scratch_shapes=[pltpu.VMEM((tr, D), F32)],
        compiler_params=_params("arbitrary"))(dp, uv, vg, ws, wst, bsb)


def _gelu(x):
    return 0.5 * x * (1.0 + jnp.tanh(GELU_C * (x + GELU_A * x * x * x)))


def _gelu_grad(x):
    th = jnp.tanh(GELU_C * (x + GELU_A * x * x * x))
    return 0.5 * (1.0 + th) + 0.5 * x * (1.0 - th * th) * GELU_C * (1.0 + 3.0 * GELU_A * x * x)


def _cmul_add(xr, xi, ar, ai, br, bi):
    return xr + ar * br - ai * bi, xi + ar * bi + ai * br


def _cmul_conj_add(xr, xi, ar, ai, br, bi):
    return xr + ar * br + ai * bi, xi + ar * bi - ai * br


def _to_subchunk_order(src_ref, dst_ref, n):
    for k in range(n):
        dst_ref[pl.ds(SUBLANE * k, SUBLANE), :] = src_ref[pl.ds(k, SUBLANE, stride=n), :].astype(dst_ref.dtype)


def _to_time_order(src_ref, dst_ref, n):
    for m in range(n):
        r, k = divmod(SUBLANE * m, n)
        dst_ref[pl.ds(SUBLANE * m, SUBLANE), :] = src_ref[pl.ds(SUBLANE * k + r, SUBLANE, stride=SUBLANE), :]


def _s5_fwd(u, bre, bim, cre, cim, pw, pos, dsk, *, name, tc=S5_CHUNK):
    L, D = u.shape
    W = S5_LANES // S5_BLOCKS
    nt = L // tc
    n = tc // SUBLANE

    def sub(k):
        return pl.ds(SUBLANE * k, SUBLANE)

    def body(u_ref, bre_ref, bim_ref, cre_ref, cim_ref, pw_ref, pos_ref, d_ref, sre_ref, sim_ref, ypre_ref, yg_ref,
             carry, up, yp):
        t = pl.program_id(1)

        @pl.when(t == 0)
        def _():
            carry[...] = jnp.zeros_like(carry)

        _to_subchunk_order(u_ref, up, n)
        uv = up[...]
        ub = uv.astype(BF16)
        sre_ref[...] = jnp.dot(ub, bre_ref[...], preferred_element_type=F32)
        sim_ref[...] = jnp.dot(ub, bim_ref[...], preferred_element_type=F32)

        ar, ai = pw_ref[8], pw_ref[9]
        xr = jnp.zeros((SUBLANE, W), F32)
        xi = jnp.zeros((SUBLANE, W), F32)
        for k in range(n):
            xr, xi = _cmul_add(sre_ref[sub(k), :], sim_ref[sub(k), :], ar, ai, xr, xi)
            sre_ref[sub(k), :] = xr
            sim_ref[sub(k), :] = xi
        for q, d in enumerate((1, 2, 4)):
            xr, xi = _cmul_add(xr, xi, pw_ref[2 * q], pw_ref[2 * q + 1], pltpu.roll(xr, d, 0), pltpu.roll(xi, d, 0))
        cr, ci = carry[0], carry[1]
        xr, xi = _cmul_add(xr, xi, pw_ref[6], pw_ref[7], cr, ci)
        first = lax.broadcasted_iota(jnp.int32, (SUBLANE, W), 0) == 0
        er = jnp.where(first, cr, pltpu.roll(xr, 1, 0))
        ei = jnp.where(first, ci, pltpu.roll(xi, 1, 0))
        last = slice(SUBLANE - 1, SUBLANE)
        carry[0] = jnp.broadcast_to(xr[last, :], (SUBLANE, W))
        carry[1] = jnp.broadcast_to(xi[last, :], (SUBLANE, W))
        for k in range(n):
            sr, si = _cmul_add(sre_ref[sub(k), :], sim_ref[sub(k), :], pos_ref[0, k:k + 1, :], pos_ref[1, k:k + 1, :],
                               er, ei)
            sre_ref[sub(k), :] = sr
            sim_ref[sub(k), :] = si
        yp[...] = (jnp.dot(sre_ref[...].astype(BF16), cre_ref[...], preferred_element_type=F32)
                   - jnp.dot(sim_ref[...].astype(BF16), cim_ref[...], preferred_element_type=F32) + d_ref[...] * uv)
        _to_time_order(yp, ypre_ref, n)
        yg_ref[...] = _gelu(ypre_ref[...])

    ch = pl.BlockSpec((tc, LANE), lambda j, t: (t, j))
    st = pl.BlockSpec((tc, W), lambda j, t: (t, j))
    bsp = pl.BlockSpec((None, LANE, W), lambda j, t: (j, 0, 0))
    csp = pl.BlockSpec((None, W, LANE), lambda j, t: (j, 0, 0))
    return pl.pallas_call(
        body, name=name,
        out_shape=(jax.ShapeDtypeStruct((L, S5_LANES), F32), jax.ShapeDtypeStruct((L, S5_LANES), F32),
                   jax.ShapeDtypeStruct((L, D), F32), jax.ShapeDtypeStruct((L, D), F32)),
        grid=(S5_BLOCKS, nt),
        in_specs=[ch, bsp, bsp, csp, csp, pl.BlockSpec((10, SUBLANE, W), lambda j, t: (0, 0, j)),
                  pl.BlockSpec((2, n, W), lambda j, t: (0, 0, j)), pl.BlockSpec((1, LANE), lambda j, t: (0, j))],
        out_specs=(st, st, ch, ch),
        scratch_shapes=[pltpu.VMEM((2, SUBLANE, W), F32), pltpu.VMEM((tc, LANE), F32), pltpu.VMEM((tc, LANE), F32)],
        compiler_params=_params("parallel", "arbitrary"))(u, bre, bim, cre, cim, pw, pos, dsk)


def _s5_bwd(dy, u, sre, sim, bre, bim, cre, cim, pwr, posr, dsk, *, name, tc=S5_CHUNK):
    L, D = u.shape
    W = S5_LANES // S5_BLOCKS
    nt = L // tc
    n = tc // SUBLANE
    nt_dims = (((1,), (1,)), ((), ()))
    tn_dims = (((0,), (0,)), ((), ()))

    def sub(k):
        return pl.ds(SUBLANE * k, SUBLANE)

    def body(dy_ref, u_ref, sre_ref, sim_ref, bre_ref, bim_ref, cre_ref, cim_ref, pw_ref, pos_ref, d_ref,
             du_ref, dbre_ref, dbim_ref, dcre_ref, dcim_ref, ga_ref, dd_ref, gre, gim, carry, gacc, up, dyp):
        t = pl.program_id(1)

        @pl.when(t == 0)
        def _():
            for r in (carry, gacc, dbre_ref, dbim_ref, dcre_ref, dcim_ref, ga_ref, dd_ref):
                r[...] = jnp.zeros_like(r)

        _to_subchunk_order(dy_ref, dyp, n)
        _to_subchunk_order(u_ref, up, n)
        dyv, uv = dyp[...], up[...]
        dyb, ub = dyv.astype(BF16), uv.astype(BF16)
        gre[...] = lax.dot_general(dyb, cre_ref[...], nt_dims, preferred_element_type=F32)
        gim[...] = -lax.dot_general(dyb, cim_ref[...], nt_dims, preferred_element_type=F32)
        br, bi = pw_ref[8], pw_ref[9]
        xr = jnp.zeros((SUBLANE, W), F32)
        xi = jnp.zeros((SUBLANE, W), F32)
        for k in reversed(range(n)):
            xr, xi = _cmul_add(gre[sub(k), :], gim[sub(k), :], br, bi, xr, xi)
            gre[sub(k), :] = xr
            gim[sub(k), :] = xi
        for q, d in enumerate((1, 2, 4)):
            xr, xi = _cmul_add(xr, xi, pw_ref[2 * q], pw_ref[2 * q + 1], pltpu.roll(xr, SUBLANE - d, 0),
                               pltpu.roll(xi, SUBLANE - d, 0))
        cr, ci = carry[0], carry[1]
        xr, xi = _cmul_add(xr, xi, pw_ref[6], pw_ref[7], cr, ci)
        top = lax.broadcasted_iota(jnp.int32, (SUBLANE, W), 0) == SUBLANE - 1
        er = jnp.where(top, cr, pltpu.roll(xr, SUBLANE - 1, 0))
        ei = jnp.where(top, ci, pltpu.roll(xi, SUBLANE - 1, 0))
        carry[0] = jnp.broadcast_to(xr[0:1, :], (SUBLANE, W))
        carry[1] = jnp.broadcast_to(xi[0:1, :], (SUBLANE, W))
        nr, ni = er, ei
        acc_r = jnp.zeros((SUBLANE, W), F32)
        acc_i = jnp.zeros((SUBLANE, W), F32)
        for k in reversed(range(n)):
            place = slice(n - 1 - k, n - k)
            gr, gi = _cmul_conj_add(gre[sub(k), :], gim[sub(k), :], pos_ref[0, place, :], pos_ref[1, place, :], er, ei)
            gre[sub(k), :] = gr
            gim[sub(k), :] = gi
            sr, si = sre_ref[sub(k), :], sim_ref[sub(k), :]
            acc_r = acc_r + sr * nr + si * ni
            acc_i = acc_i + sr * ni - si * nr
            nr, ni = gr, gi
        gacc[0] += acc_r
        gacc[1] += acc_i
        grb, gib = gre[...].astype(BF16), gim[...].astype(BF16)
        dyp[...] = (lax.dot_general(grb, bre_ref[...], nt_dims, preferred_element_type=F32)
                    + lax.dot_general(gib, bim_ref[...], nt_dims, preferred_element_type=F32) + d_ref[...] * dyv)
        _to_time_order(dyp, up, n)
        du_ref[...] = up[...].astype(du_ref.dtype)
        dbre_ref[...] += lax.dot_general(ub, grb, tn_dims, preferred_element_type=F32)
        dbim_ref[...] += lax.dot_general(ub, gib, tn_dims, preferred_element_type=F32)
        dcre_ref[...] += lax.dot_general(sre_ref[...].astype(BF16), dyb, tn_dims, preferred_element_type=F32)
        dcim_ref[...] -= lax.dot_general(sim_ref[...].astype(BF16), dyb, tn_dims, preferred_element_type=F32)
        dd_ref[0:1, :] += jnp.sum(dyv * uv, axis=0, keepdims=True)

        @pl.when(t == nt - 1)
        def _():
            ga_ref[0:1, :] = jnp.sum(gacc[0], axis=0, keepdims=True)
            ga_ref[1:2, :] = jnp.sum(gacc[1], axis=0, keepdims=True)

    ch = pl.BlockSpec((tc, LANE), lambda j, t: (nt - 1 - t, j))
    st = pl.BlockSpec((tc, W), lambda j, t: (nt - 1 - t, j))
    bsp = pl.BlockSpec((None, LANE, W), lambda j, t: (j, 0, 0))
    csp = pl.BlockSpec((None, W, LANE), lambda j, t: (j, 0, 0))
    return pl.pallas_call(
        body, name=name,
        out_shape=(jax.ShapeDtypeStruct((L, D), BF16),
                   jax.ShapeDtypeStruct((S5_BLOCKS, LANE, W), F32), jax.ShapeDtypeStruct((S5_BLOCKS, LANE, W), F32),
                   jax.ShapeDtypeStruct((S5_BLOCKS, W, LANE), F32), jax.ShapeDtypeStruct((S5_BLOCKS, W, LANE), F32),
                   jax.ShapeDtypeStruct((SUBLANE, S5_LANES), F32), jax.ShapeDtypeStruct((SUBLANE, D), F32)),
        grid=(S5_BLOCKS, nt),
        in_specs=[ch, ch, st, st, bsp, bsp, csp, csp, pl.BlockSpec((10, SUBLANE, W), lambda j, t: (0, 0, j)),
                  pl.BlockSpec((2, n, W), lambda j, t: (0, 0, j)), pl.BlockSpec((1, LANE), lambda j, t: (0, j))],
        out_specs=(ch, bsp, bsp, csp, csp, pl.BlockSpec((SUBLANE, W), lambda j, t: (0, j)),
                   pl.BlockSpec((SUBLANE, LANE), lambda j, t: (0, j))),
        scratch_shapes=[pltpu.VMEM((tc, W), F32), pltpu.VMEM((tc, W), F32), pltpu.VMEM((2, SUBLANE, W), F32),
                        pltpu.VMEM((2, SUBLANE, W), F32), pltpu.VMEM((tc, LANE), F32), pltpu.VMEM((tc, LANE), F32)],
        compiler_params=_params("parallel", "arbitrary"))(dy, u, sre, sim, bre, bim, cre, cim, pwr, posr, dsk)


def _s5_prep(a_re, a_im, log_dt, b_re, b_im, c_re, c_im):
    dt = jnp.exp(log_dt)[:, None]
    mag = jnp.exp(a_re * dt)
    abar_re = mag * jnp.cos(a_im * dt)
    abar_im = mag * jnp.sin(a_im * dt)
    den = a_re * a_re + a_im * a_im
    nr = abar_re - 1.0
    ni = abar_im
    f_re = ((nr * a_re + ni * a_im) / den)[..., None]
    f_im = ((ni * a_re - nr * a_im) / den)[..., None]
    bbar_re = f_re * b_re - f_im * b_im
    bbar_im = f_re * b_im + f_im * b_re
    eye = jnp.eye(S5_GROUPS // S5_BLOCKS, dtype=F32)
    gb = S5_GROUPS // S5_BLOCKS

    def blk_b(bb):
        t = bb.reshape(S5_BLOCKS, gb, S5_STATE, S5_GROUP)
        return jnp.einsum('jgph,gk->jghkp', t, eye).reshape(S5_BLOCKS, gb * S5_GROUP, gb * S5_STATE)

    def blk_c(cc):
        t = cc.reshape(S5_BLOCKS, gb, S5_GROUP, S5_STATE)
        return jnp.einsum('jghp,gk->jgpkh', t, eye).reshape(S5_BLOCKS, gb * S5_STATE, gb * S5_GROUP)

    return (abar_re.reshape(1, S5_LANES), abar_im.reshape(1, S5_LANES), blk_b(bbar_re), blk_b(bbar_im),
            blk_c(c_re), blk_c(c_im))


def _cpowers(ar, ai, count):
    pr, pi, m = ar, ai, 1
    while m < count:
        tr, ti = pr[m - 1:m], pi[m - 1:m]
        pr, pi = jnp.concatenate([pr, pr * tr - pi * ti], 0), jnp.concatenate([pi, pr * ti + pi * tr], 0)
        m *= 2
    return pr, pi


def _s5_power_tables(ar, ai, n):
    pr, pi = _cpowers(ar, ai, n)
    qr, qi = _cpowers(pr[n - 1:n], pi[n - 1:n], SUBLANE)
    row = jnp.arange(SUBLANE)[:, None]
    lanes = ar.shape[1]

    def tables(sign, keep, order):
        out = []
        for d in (1, 2, 4):
            out += [jnp.where(keep(d), qr[d - 1:d], 0.0), jnp.where(keep(d), sign * qi[d - 1:d], 0.0)]
        out += [jnp.concatenate([qr[r:r + 1] for r in order], 0), sign * jnp.concatenate([qi[r:r + 1] for r in order], 0),
                ar, sign * ai]
        return jnp.stack([jnp.broadcast_to(o, (SUBLANE, lanes)) for o in out])

    fwd = tables(1.0, lambda d: row >= d, list(range(SUBLANE)))
    rev = tables(-1.0, lambda d: row + d <= SUBLANE - 1, list(reversed(range(SUBLANE))))
    return fwd, rev, jnp.stack([pr, pi])


ADAMW_PART_BLOCK_BYTES = 2 * 1024 * 1024


def _adamw(w, parts, m, v, *, name):
    n, R, C = w.shape
    assert len(parts) == n
    P = parts[0].shape[0]
    tr = R
    while P * tr * C * parts[0].dtype.itemsize > ADAMW_PART_BLOCK_BYTES and tr % 16 == 0:
        tr //= 2
    c1 = 1.0 / (1.0 - ADAM_B1 ** ADAM_STEP)
    c2 = 1.0 / (1.0 - ADAM_B2 ** ADAM_STEP)

    def body(*refs):
        w_ref, m_ref, v_ref = refs[:3]
        p_refs = refs[3:3 + n]
        g_ref, d_ref, nm_ref, nv_ref = refs[3 + n:]
        layer = pl.program_id(0)
        for q, p_ref in enumerate(p_refs):
            @pl.when(layer == q)
            def _(p_ref=p_ref):
                g = p_ref[0].astype(F32)
                for s in range(1, P):
                    g = g + p_ref[s].astype(F32)
                nm = ADAM_B1 * m_ref[...] + (1.0 - ADAM_B1) * g
                nv = ADAM_B2 * v_ref[...] + (1.0 - ADAM_B2) * (g * g)
                g_ref[...] = g
                nm_ref[...] = nm
                nv_ref[...] = nv
                d_ref[...] = -ADAM_LR * ((nm * c1) / (jnp.sqrt(nv * c2) + ADAM_EPS) + ADAM_WD * w_ref[...])

    row = pl.BlockSpec((None, tr, C), lambda l, i: (l, i, 0))
    part_specs = [pl.BlockSpec((P, tr, C), lambda l, i, q=q: (0, jnp.where(l == q, i, 0), 0)) for q in range(n)]
    out = jax.ShapeDtypeStruct((n, R, C), F32)
    return pl.pallas_call(body, name=name, out_shape=(out, out, out, out), grid=(n, R // tr),
                          in_specs=[row, row, row] + part_specs, out_specs=(row, row, row, row),
                          compiler_params=_params("arbitrary", "arbitrary"))(w, m, v, *parts)


def _all_gather(xs, axis, *, name):
    m = xs.shape[axis]
    out_shape = list(xs.shape)
    out_shape[axis] = N_DEV * m

    def body(x_ref, out_ref, send_sems, recv_sems, local_sem):
        x, y, c = _my_pos()
        me, sibling = (x, y, c), (x, y, 1 - c)
        chips = [(1 - x, y), (x, 1 - y), (1 - x, 1 - y)]

        def blk(px, py, pc):
            idx = [slice(None)] * 3
            idx[axis] = pl.ds((4 * px + 2 * py + pc) * m, m)
            return out_ref.at[tuple(idx)]

        def copy(k, block, to, src=None):
            return pltpu.make_async_remote_copy(src_ref=blk(*block) if src is None else src, dst_ref=blk(*block),
                                                send_sem=send_sems.at[k], recv_sem=recv_sems.at[k],
                                                device_id=to, device_id_type=MESH_ID)

        mine = pltpu.make_async_copy(x_ref, blk(*me), local_sem)
        mine.start()
        first = [copy(0, me, sibling, src=x_ref)]
        first += [copy(1 + j, me, (*chip, c), src=x_ref) for j, chip in enumerate(chips)]
        for cp in first:
            cp.start()
        passed = [copy(4 + j, (*chip, c), sibling) for j, chip in enumerate(chips)]
        for j, chip in enumerate(chips):
            copy(1 + j, (*chip, c), me).wait_recv()
            passed[j].start()
        copy(0, sibling, me).wait_recv()
        for j, chip in enumerate(chips):
            copy(4 + j, (*chip, 1 - c), me).wait_recv()
        for cp in first + passed:
            cp.wait_send()
        mine.wait()

    hbm = pl.BlockSpec(memory_space=pl.ANY)
    return pl.pallas_call(body, name=name, out_shape=jax.ShapeDtypeStruct(tuple(out_shape), xs.dtype),
                          in_specs=[hbm], out_specs=hbm,
                          scratch_shapes=[pltpu.SemaphoreType.DMA((N_DEV - 1,)), pltpu.SemaphoreType.DMA((N_DEV - 1,)),
                                          pltpu.SemaphoreType.DMA],
                          compiler_params=pltpu.CompilerParams(has_side_effects=True))(xs)


NEAR_PEERS = (1, 2, 4, 6)
RELAY_PEERS = (2, 4, 6)


def _block(ref, axis, idx, m):
    return ref.at[pl.ds(idx * m, m), :] if axis == 0 else ref.at[:, pl.ds(idx * m, m)]


def _exchange_copies(metas, src_refs, zone_refs, send_sems, recv_sems, base, phase):
    x, y, c = _my_pos()
    me = 4 * x + 2 * y + c

    def place(r):
        pos = (1 - x if r & 4 else x, 1 - y if r & 2 else y, 1 - c if r & 1 else c)
        return pos, 4 * pos[0] + 2 * pos[1] + pos[2]

    def copies(r, to, src, dst, arrival):
        return tuple(pltpu.make_async_remote_copy(src_ref=src, dst_ref=d, send_sem=send_sems.at[base + r - 1],
                                                  recv_sem=recv_sems.at[base + r - 1], device_id=to,
                                                  device_id_type=MESH_ID) for d in (dst, arrival))

    pairs, own = [], []
    if phase == 'relay':
        sibling, _ = place(1)
        for r in RELAY_PEERS:
            held, comes = place(r)[1], place(r | 1)[1]
            for (kind, axis, m), z_ref in zip(metas, zone_refs):
                pairs.append(copies(r, sibling, _block(z_ref, axis, held, m), _block(z_ref, axis, held, m),
                                    _block(z_ref, axis, comes, m)))
        return pairs, own
    for r in (NEAR_PEERS if phase == 'near' else range(1, N_DEV)):
        pos, peer = place(r)
        for (kind, axis, m), s_ref, z_ref in zip(metas, src_refs, zone_refs):
            if kind == 'gather':
                pairs.append(copies(r, pos, s_ref, _block(z_ref, axis, me, m), _block(z_ref, axis, peer, m)))
            else:
                pairs.append(copies(r, pos, _block(s_ref, axis, peer, m), z_ref.at[me], z_ref.at[peer]))
    for (kind, axis, m), s_ref, z_ref in zip(metas, src_refs, zone_refs):
        src, dst = (s_ref, _block(z_ref, axis, me, m)) if kind == 'gather' else (_block(s_ref, axis, me, m), z_ref.at[me])
        own.append(pltpu.make_async_copy(src, dst, recv_sems.at[base + N_DEV - 1]))
    return pairs, own


def _exchange_start(groups, after, *, name, relayed=False):
    flat = [it for g in groups for it in g]
    n, ng = len(flat), len(groups)
    metas = [it[2] for it in flat]
    bounds = [(sum(len(g) for g in groups[:q]), sum(len(g) for g in groups[:q + 1])) for q in range(ng)]
    phase = 'near' if relayed else 'all'

    def body(*refs):
        src_refs = refs[:n]
        send_sems, recv_sems = refs[n + 1], refs[n + 2]
        zone_refs = refs[2 * n + 3:3 * n + 3]
        token = refs[-1]
        for q, (lo, hi) in enumerate(bounds):
            pairs, own = _exchange_copies(metas[lo:hi], src_refs[lo:hi], zone_refs[lo:hi], send_sems, recv_sems,
                                          q * N_DEV, phase)
            for outgoing, _ in pairs:
                outgoing.start()
            for cp in own:
                cp.start()
        token[...] = jnp.zeros_like(token)

    hbm = pl.BlockSpec(memory_space=pltpu.HBM)
    sem = pl.BlockSpec(memory_space=pltpu.SEMAPHORE)
    srcs = [it[0] for it in flat]
    res = pl.pallas_call(
        body, name=name,
        out_shape=(pltpu.SemaphoreType.DMA((ng * N_DEV,)), pltpu.SemaphoreType.DMA((ng * N_DEV,)),
                   *[pltpu.HBM(a.shape, a.dtype) for a in srcs], *[pltpu.HBM(it[1], it[0].dtype) for it in flat],
                   jax.ShapeDtypeStruct((SUBLANE, LANE), F32)),
        in_specs=[hbm] * n + [pl.BlockSpec(memory_space=pl.ANY)],
        out_specs=(sem, sem, *[hbm] * (2 * n), pl.BlockSpec(memory_space=pltpu.VMEM)),
        input_output_aliases={q: 2 + q for q in range(n)},
        compiler_params=pltpu.CompilerParams(has_side_effects=pltpu.SideEffectType.DATAFLOW_SIDE_EFFECTING),
    )(*[pltpu.with_memory_space_constraint(a, pltpu.HBM) for a in srcs], after)
    handles = [(res[0], res[1], q * N_DEV, phase, list(res[2 + lo:2 + hi]), list(res[2 + n + lo:2 + n + hi]),
                metas[lo:hi]) for q, (lo, hi) in enumerate(bounds)]
    return handles, res[-1]


def _exchange_wait(handle, after, *, name):
    send_sems, recv_sems, base, phase, srcs, zones, metas = handle
    ns, nz = len(srcs), len(zones)

    def body(*refs):
        src_refs, zone_refs = refs[:ns], refs[ns:ns + nz]
        s_sems, r_sems = refs[ns + nz], refs[ns + nz + 1]
        pairs, own = _exchange_copies(metas, src_refs, zone_refs, s_sems, r_sems, base, phase)
        for outgoing, incoming in pairs:
            outgoing.wait_send()
            incoming.wait_recv()
        for cp in own:
            cp.wait()

    hbm = pl.BlockSpec(memory_space=pltpu.HBM)
    sem = pl.BlockSpec(memory_space=pltpu.SEMAPHORE)
    arrays = srcs + zones
    res = pl.pallas_call(
        body, name=name,
        out_shape=tuple(pltpu.HBM(a.shape, a.dtype) for a in arrays),
        in_specs=[hbm] * (ns + nz) + [sem, sem, pl.BlockSpec(memory_space=pl.ANY)],
        out_specs=tuple([hbm] * (ns + nz)),
        input_output_aliases={q: q for q in range(ns + nz)},
        compiler_params=pltpu.CompilerParams(has_side_effects=pltpu.SideEffectType.DATAFLOW_SIDE_EFFECTING),
    )(*arrays, send_sems, recv_sems, after)
    return list(res[ns:])


def _exchange_relay(handle, after, *, name):
    metas = handle[6]
    zones = _exchange_wait(handle, after, name=name + "_in")
    nz = len(zones)

    def body(*refs):
        zone_refs = refs[:nz]
        send_sems, recv_sems = refs[nz], refs[nz + 1]
        pairs, _ = _exchange_copies(metas, (), zone_refs, send_sems, recv_sems, 0, 'relay')
        for outgoing, _ in pairs:
            outgoing.start()
        refs[-1][...] = jnp.zeros_like(refs[-1])

    hbm = pl.BlockSpec(memory_space=pltpu.HBM)
    sem = pl.BlockSpec(memory_space=pltpu.SEMAPHORE)
    res = pl.pallas_call(
        body, name=name + "_out",
        out_shape=(pltpu.SemaphoreType.DMA((N_DEV,)), pltpu.SemaphoreType.DMA((N_DEV,)),
                   *[pltpu.HBM(z.shape, z.dtype) for z in zones], jax.ShapeDtypeStruct((SUBLANE, LANE), F32)),
        in_specs=[hbm] * nz, out_specs=(sem, sem, *[hbm] * nz, pl.BlockSpec(memory_space=pltpu.VMEM)),
        input_output_aliases={q: 2 + q for q in range(nz)},
        compiler_params=pltpu.CompilerParams(has_side_effects=pltpu.SideEffectType.DATAFLOW_SIDE_EFFECTING),
    )(*zones)
    return (res[0], res[1], 0, 'relay', [], list(res[2:2 + nz]), metas), res[-1][0:1, 0:1]


def _pad_rows(a, rows):
    return jnp.pad(a, ((0, rows - a.shape[0]), (0, 0)))


PACK_ROWS = 2 * SUBLANE


def _rows(a):
    flat = a.reshape(-1).astype(F32)
    pad = -flat.shape[0] % (PACK_ROWS * LANE)
    return (jnp.pad(flat, (0, pad)) if pad else flat).reshape(-1, LANE)


def _pack_rows(arrays):
    return jnp.concatenate([_rows(a) for a in arrays], 0)


def _unpack_rows(t, shapes):
    out, off = [], 0
    for shp in shapes:
        size = math.prod(shp)
        rows = -(-size // (PACK_ROWS * LANE)) * PACK_ROWS
        out.append(t[off:off + rows].reshape(-1)[:size].reshape(shp))
        off += rows
    return out


def _stat_row(st, r):
    return st[r:r + 1, :]


def kernel(x, c, ada_w, ada_b, norm1_g, norm2_g, ff_w1, ff_w2, final_g, conv_w_in, conv_w, conv_b, conv_w_out, ssm_w_in, ssm_a_re, ssm_a_im, ssm_log_dt, ssm_b_re, ssm_b_im, ssm_c_re, ssm_c_im, ssm_d, ssm_glu_w, ssm_glu_b, ssm_w_out, sg_w_in, sg_v_g, sg_w_s, sg_b_s, sg_w_out, loss_target, m_ada_w, m_ada_b, m_norm1_g, m_norm2_g, m_ff_w1, m_ff_w2, m_final_g, m_conv_w_in, m_conv_w, m_conv_b, m_conv_w_out, m_ssm_w_in, m_ssm_a_re, m_ssm_a_im, m_ssm_log_dt, m_ssm_b_re, m_ssm_b_im, m_ssm_c_re, m_ssm_c_im, m_ssm_d, m_ssm_glu_w, m_ssm_glu_b, m_ssm_w_out, m_sg_w_in, m_sg_v_g, m_sg_w_s, m_sg_b_s, m_sg_w_out, v_ada_w, v_ada_b, v_norm1_g, v_norm2_g, v_ff_w1, v_ff_w2, v_final_g, v_conv_w_in, v_conv_w, v_conv_b, v_conv_w_out, v_ssm_w_in, v_ssm_a_re, v_ssm_a_im, v_ssm_log_dt, v_ssm_b_re, v_ssm_b_im, v_ssm_c_re, v_ssm_c_im, v_ssm_d, v_ssm_glu_w, v_ssm_glu_b, v_ssm_w_out, v_sg_w_in, v_sg_v_g, v_sg_w_s, v_sg_b_s, v_sg_w_out):
    P = dict(zip(INPUTS, (x, c, ada_w, ada_b, norm1_g, norm2_g, ff_w1, ff_w2, final_g, conv_w_in, conv_w, conv_b, conv_w_out, ssm_w_in, ssm_a_re, ssm_a_im, ssm_log_dt, ssm_b_re, ssm_b_im, ssm_c_re, ssm_c_im, ssm_d, ssm_glu_w, ssm_glu_b, ssm_w_out, sg_w_in, sg_v_g, sg_w_s, sg_b_s, sg_w_out, loss_target, m_ada_w, m_ada_b, m_norm1_g, m_norm2_g, m_ff_w1, m_ff_w2, m_final_g, m_conv_w_in, m_conv_w, m_conv_b, m_conv_w_out, m_ssm_w_in, m_ssm_a_re, m_ssm_a_im, m_ssm_log_dt, m_ssm_b_re, m_ssm_b_im, m_ssm_c_re, m_ssm_c_im, m_ssm_d, m_ssm_glu_w, m_ssm_glu_b, m_ssm_w_out, m_sg_w_in, m_sg_v_g, m_sg_w_s, m_sg_b_s, m_sg_w_out, v_ada_w, v_ada_b, v_norm1_g, v_norm2_g, v_ff_w1, v_ff_w2, v_final_g, v_conv_w_in, v_conv_w, v_conv_b, v_conv_w_out, v_ssm_w_in, v_ssm_a_re, v_ssm_a_im, v_ssm_log_dt, v_ssm_b_re, v_ssm_b_im, v_ssm_c_re, v_ssm_c_im, v_ssm_d, v_ssm_glu_w, v_ssm_glu_b, v_ssm_w_out, v_sg_w_in, v_sg_v_g, v_sg_w_s, v_sg_b_s, v_sg_w_out)))
    L, D = x.shape[1], x.shape[2]
    me = _my_index()
    xs = x[0]
    tgt = loss_target[0]
    n_conv = conv_w_in.shape[0]

    def gather_item(shard, axis):
        full = tuple(N_DEV * s if a == axis else s for a, s in enumerate(shard.shape))
        return shard, full, ('gather', axis, shard.shape[axis])

    def mixer_shards(i):
        kind, j = i % 3, i // 3
        if kind == 0:
            return [(conv_w_in[j], 1), (conv_w_out[j], 0)]
        if kind == 1:
            return [(ssm_w_in[j], 0), (ssm_glu_w[j], 0), (ssm_w_out[j], 0)]
        return [(sg_w_in[j], 1), (sg_w_out[j], 0)]

    gather_groups = [[gather_item(w.astype(BF16), ax) for w, ax in shards]
                     for i in range(DEPTH) for shards in (mixer_shards(i), [(ff_w1[i], 1), (ff_w2[i], 0)])]
    first_gather, first_token = _exchange_start(gather_groups[:1], c, name="gather_start_first", relayed=True)

    c_act = c * (1.0 / (1.0 + jnp.exp(-c))) + first_token[0:1, 0:1]
    vec_rows = jnp.concatenate([c_act.reshape(D // LANE, LANE), conv_w.reshape(-1, LANE), conv_b.reshape(-1, LANE),
                                sg_v_g.reshape(-1, LANE)], 0)
    n_vec = vec_rows.shape[0]
    vec_all = _all_gather(_pad_rows(vec_rows, 24)[None], 0, name="gather_vectors")
    c_all = vec_all[:, :D // LANE, :].reshape(N_DEV, D)
    sharded_full = vec_all[:, D // LANE:n_vec, :].transpose(1, 0, 2).reshape(n_vec - D // LANE, D)
    conv_w_full = sharded_full[:3 * n_conv].reshape(n_conv, 3, D)
    conv_b_full = sharded_full[3 * n_conv:4 * n_conv]
    sg_vg_full = sharded_full[4 * n_conv:4 * n_conv + 1]

    c_pad = _pad_rows(c_all, LANE)
    ncol = ada_w.shape[2]
    mod_part = jnp.stack([_mm(c_pad, ada_w[i], name=f"ada_fwd{i}")[:N_DEV] for i in range(DEPTH)])
    mod_all = _all_gather(mod_part.reshape(1, DEPTH * N_DEV, ncol), 0, name="gather_mod")
    mod_all = mod_all.reshape(N_DEV, DEPTH, N_DEV, ncol)
    mod_me = lax.dynamic_index_in_dim(mod_all, me, 2, keepdims=False)
    mod = mod_me.transpose(1, 0, 2).reshape(DEPTH, N_DEV * ncol) + ada_b
    gathers, gather_token = _exchange_start(gather_groups[1:], mod, name="gather_start", relayed=True)
    gathers = first_gather + gathers
    mod = mod + gather_token[0:1, 0:1]
    relayed = [None] * len(gathers)
    relayed[0], sent = _exchange_relay(gathers[0], mod, name="gather_mix_relay0")

    s5_args = (ssm_a_re[0], ssm_a_im[0], ssm_log_dt[0], ssm_b_re[0], ssm_b_im[0], ssm_c_re[0], ssm_c_im[0])
    (abar_re, abar_im, bblk_re, bblk_im, cblk_re, cblk_im), s5_vjp = jax.vjp(_s5_prep, *s5_args)
    pw_fwd, pw_rev, pos_fwd = _s5_power_tables(abar_re, abar_im, S5_CHUNK // SUBLANE)
    s5_w = tuple(t.astype(BF16) for t in (bblk_re, bblk_im, cblk_re, cblk_im))
    causal = jnp.tril(jnp.ones((SG_CHUNK, SG_CHUNK), dtype=bool))
    ws_m = jnp.where(causal[None], sg_w_s[0], 0.0)
    ws_b = ws_m.astype(BF16)
    wst_b = ws_m.transpose(0, 2, 1).astype(BF16)
    bsb = jnp.broadcast_to(sg_b_s[0][:, :, None], (SG_HEADS, SG_CHUNK, LANE))

    saved = []
    xa = xs
    mods = [[mod[i:i + 1, q * D:(q + 1) * D] for q in range(6)] for i in range(DEPTH)]
    wn1s = [norm1_g[i:i + 1] * (1.0 + mods[i][1]) for i in range(DEPTH)]
    h1 = _normmod_fwd(xa, wn1s[0], mods[0][0] + sent, name="norm1_fwd0")
    for i in range(DEPTH):
        kind, j = i % 3, i // 3
        sh1, sc1, g1, sh2, sc2, g2 = mods[i]
        wn1 = wn1s[i]
        wn2 = norm2_g[i:i + 1] * (1.0 + sc2)
        S = dict(x_in=xa, g1=g1, g2=g2, sc1=sc1, sc2=sc2, wn1=wn1, wn2=wn2)
        w_mix = _exchange_wait(relayed[2 * i], h1, name=f"gather_mix_wait{i}")
        S['h1'] = h1
        if kind == 0:
            bcx = _mm(h1, w_mix[0], name=f"conv_in{i}", out_dtypes=(BF16,), bm=2048)
            wb = _pad_rows(jnp.concatenate([conv_w_full[j], conv_b_full[j:j + 1]], 0), SUBLANE)
            pb = _conv_fwd(bcx, wb, name=f"conv_mix{i}")
            S.update(bcx=bcx, wb=wb, pb=pb)
        elif kind == 1:
            u = _mm(h1, w_mix[0], name=f"ssm_in{i}")
            sre, sim, ypre, yg = _s5_fwd(u, *s5_w, pw_fwd, pos_fwd, ssm_d, name=f"s5_scan{i}")

            def glu_epi(acc, yv, bias):
                t = acc + bias
                return yv * (1.0 / (1.0 + jnp.exp(-t))), t

            pb, tt = _mm(yg, w_mix[1], name=f"ssm_glu{i}", out_dtypes=(BF16, F32), epi=glu_epi,
                         extras=[(yg, 'mn'), (ssm_glu_b, 'n')])
            S.update(u=u, sre=sre, sim=sim, ypre=ypre, yg=yg, pb=pb, tt=tt)
        else:
            uv = _mm(h1, w_mix[0], name=f"sg_in{i}", bm=2048)
            pb = _sg_fwd(uv, sg_vg_full, ws_b, bsb, name=f"sg_mix{i}")
            S.update(uv=uv, pb=pb)
        relayed[2 * i + 1], sent = _exchange_relay(gathers[2 * i + 1], pb, name=f"gather_ff_relay{i}")
        x_mid, y_mix, h2 = _mm(pb, w_mix[-1], name=f"mix_out{i}", out_dtypes=(F32, BF16, BF16), epi=_epi_residual_norm,
                               extras=[(xa, 'mn'), (g1 + sent, 'n'), (wn2, 'n'), (sh2, 'n')])
        w1_full, w2_full = _exchange_wait(relayed[2 * i + 1], h2, name=f"gather_ff_wait{i}")
        S.update(w_mix=w_mix, w1=w1_full, w2=w2_full)
        ra = _mm(h2, w1_full, name=f"ff_up{i}", out_dtypes=(BF16,), epi=lambda acc: (jnp.maximum(acc, 0.0),), bm=2048)
        if i + 1 < DEPTH:
            relayed[2 * i + 2], sent = _exchange_relay(gathers[2 * i + 2], ra, name=f"gather_mix_relay{i + 1}")
            xa, f_out, h1 = _mm(ra, w2_full, name=f"ff_down{i}", out_dtypes=(F32, BF16, BF16), a_fn=_square,
                                epi=_epi_residual_norm, bm=256, bk=w2_full.shape[0],
                                extras=[(x_mid, 'mn'), (g2 + sent, 'n'), (wn1s[i + 1], 'n'), (mods[i + 1][0], 'n')])
        else:
            f_out = None
            dx, dfb, st = _mm(ra, w2_full, name=f"ff_down{i}", out_dtypes=(F32, BF16), epi=_epi_loss_head, a_fn=_square,
                              n_stats=3, bm=256, bk=w2_full.shape[0],
                              extras=[(x_mid, 'mn'), (g2, 'n'), (tgt, 'mn'), (final_g[None], 'n')])
        S.update(x_mid=x_mid, y_mix=y_mix, h2=h2, ra=ra, f_out=f_out)
        saved.append(S)

    loss_tile = st[:, :LANE]
    d_final_g = _stat_row(st, 1)
    dg2_next = _stat_row(st, 2)

    def scatter_item(g, axis):
        m = g.shape[axis] // N_DEV
        shard = tuple(m if a == axis else s for a, s in enumerate(g.shape))
        return g, (N_DEV,) + shard, ('scatter', axis, m)

    dmod = [None] * DEPTH
    dn1g, dn2g = [None] * DEPTH, [None] * DEPTH
    d_conv_w, d_conv_b = [None] * n_conv, [None] * n_conv
    ff_sent, mix_sent = [None] * DEPTH, [None] * DEPTH
    small = {}
    for i in reversed(range(DEPTH)):
        kind, j = i % 3, i // 3
        S = saved[i]
        w_mix = S['w_mix']
        dg2 = dg2_next
        da = _mm(dfb, S['w2'], tb=True, name=f"ff_down_bwd{i}", out_dtypes=(BF16,), bm=2048,
                 epi=lambda acc, rav: (acc * (2.0 * rav.astype(F32)),), extras=[(S['ra'], 'mn')])
        dw2 = _wgrad(S['ra'], dfb, name=f"ff_w2_grad{i}", a_fn=_square, bm=256, bn=1024)
        dw1 = _wgrad(S['h2'], da, name=f"ff_w1_grad{i}")
        (ff_sent[i],), token = _exchange_start([[scatter_item(dw1, 1), scatter_item(dw2, 0)]], dx,
                                               name=f"ff_grads_start{i}")
        dx_mid, dyb, st2 = _mm(da, S['w1'], tb=True, name=f"ff_up_bwd{i}", out_dtypes=(F32, BF16), bm=512,
                               bk=da.shape[1], epi=_epi_norm_bwd(True), n_stats=3,
                               extras=[(S['x_mid'], 'mn'), (S['wn2'] + token[0:1, 0:1], 'n'), (dx, 'mn'),
                                       (S['y_mix'], 'mn'), (S['g1'], 'n')])
        dsc2 = _stat_row(st2, 0) * norm2_g[i:i + 1]
        dn2g[i] = _stat_row(st2, 0) * (1.0 + S['sc2'])
        dsh2 = _stat_row(st2, 1)
        dg1 = _stat_row(st2, 2)
        if kind == 0:
            dp = _mm(dyb, w_mix[1], tb=True, name=f"conv_out_bwd{i}", out_dtypes=(BF16,))
            d_cwo = _wgrad(S['pb'], dyb, name=f"conv_w_out_grad{i}")
            dbcx, stc = _conv_bwd(dp, S['bcx'], S['wb'], name=f"conv_mix_bwd{i}")
            d_conv_w[j] = stc[0:3]
            d_conv_b[j] = stc[3:4]
            dh_operand, dh_name = dbcx, "conv_in_bwd"
            d_cwi = _wgrad(S['h1'], dbcx, name=f"conv_w_in_grad{i}")
            mix_grads = [scatter_item(d_cwi, 1), scatter_item(d_cwo, 0)]
        elif kind == 1:
            dtb, dya, stg = _mm(dyb, w_mix[2], tb=True, name=f"ssm_out_bwd{i}", out_dtypes=(BF16, F32), bm=512,
                                epi=_epi_glu_bwd, n_stats=1, extras=[(S['yg'], 'mn'), (S['tt'], 'mn')])
            d_ssm_out = _wgrad(S['pb'], dyb, name=f"ssm_w_out_grad{i}")
            dypre = _mm(dtb, w_mix[1], tb=True, name=f"ssm_glu_in_bwd{i}",
                        epi=lambda acc, a, yp: ((a + acc) * _gelu_grad(yp),),
                        extras=[(dya, 'mn'), (S['ypre'], 'mn')])
            d_glu = _wgrad(S['yg'], dtb, name=f"ssm_glu_w_grad{i}", bm=512)
            dub, dbre, dbim, dcre, dcim, ga, dd = _s5_bwd(dypre, S['u'], S['sre'], S['sim'], *s5_w, pw_rev, pos_fwd, ssm_d,
                                                           name=f"s5_scan_bwd{i}")
            dh_operand, dh_name = dub, "ssm_in_bwd"
            d_ssm_in = _wgrad(S['h1'], dub, name=f"ssm_w_in_grad{i}")
            da_re, da_im, dlog_dt, db_re, db_im, dc_re, dc_im = s5_vjp((ga[0:1], ga[1:2], dbre, dbim, dcre, dcim))
            s5_small = _pack_rows([da_re, da_im, dlog_dt, db_re, db_im, dc_re, dc_im, dd[0], stg[0]])
            mix_grads = [scatter_item(d_ssm_in, 0), scatter_item(d_glu, 0), scatter_item(d_ssm_out, 0),
                         gather_item(s5_small.astype(BF16), 0)]
        else:
            dp = _mm(dyb, w_mix[1], tb=True, name=f"sg_out_bwd{i}")
            d_sgo = _wgrad(S['pb'], dyb, name=f"sg_w_out_grad{i}")
            duv, dws, dbs, stv = _sg_bwd(dp, S['uv'], sg_vg_full, ws_b, wst_b, bsb, name=f"sg_mix_bwd{i}")
            dh_operand, dh_name = duv, "sg_in_bwd"
            d_sgi = _wgrad(S['h1'], duv, name=f"sg_w_in_grad{i}")
            sg_small = _pack_rows([jnp.where(causal[None], dws, 0.0), jnp.sum(dbs, axis=-1)])
            d_sg_vg = stv[0:1]
            mix_grads = [scatter_item(d_sgi, 1), scatter_item(d_sgo, 0), gather_item(sg_small.astype(BF16), 0)]
        wn1 = S['wn1']
        gate = []
        if i > 0:
            (mix_sent[i],), token = _exchange_start([mix_grads], dx_mid, name=f"mix_grads_start{i}")
            wn1 = wn1 + token[0:1, 0:1]
            gate = [(saved[i - 1]['f_out'], 'mn'), (saved[i - 1]['g2'], 'n')]
        res = _mm(dh_operand, w_mix[0], tb=True, name=f"{dh_name}{i}", out_dtypes=(F32, BF16) if i > 0 else (F32,),
                  bm=512, bk=w_mix[0].shape[1], epi=_epi_norm_bwd(i > 0), n_stats=3 if i > 0 else 2,
                  extras=[(S['x_in'], 'mn'), (wn1, 'n'), (dx_mid, 'mn')] + gate)
        if i > 0:
            dx, dfb, st1 = res
            dg2_next = _stat_row(st1, 2)
        else:
            dx, st1 = res
        dsc1 = _stat_row(st1, 0) * norm1_g[i:i + 1]
        dn1g[i] = _stat_row(st1, 0) * (1.0 + S['sc1'])
        dsh1 = _stat_row(st1, 1)
        dmod[i] = jnp.concatenate([dsh1, dsc1, dg1, dsh2, dsc2, dg2], 1)
    grad_x = dx[None]

    out = {}

    def small_group(names, parts, label):
        shapes = [P[n].shape for n in names]
        w, m, v = (_pack_rows([P[pre + n] for n in names])[None] for pre in ('', 'm_', 'v_'))
        res = [_unpack_rows(t[0], shapes) for t in _adamw(w, [parts], m, v, name=label)]
        for q, n in enumerate(names):
            out[n] = tuple(r[q] for r in res)

    small.update(ada_b=jnp.concatenate(dmod, 0), norm1_g=jnp.concatenate(dn1g, 0), norm2_g=jnp.concatenate(dn2g, 0),
                 final_g=d_final_g, conv_w=jnp.stack(d_conv_w), conv_b=jnp.concatenate(d_conv_b, 0), sg_v_g=d_sg_vg)
    last_pack = _pack_rows([small[n] for n in LAST_SMALL + SMALL_SHARD])
    n_last = _pack_rows([P[n] for n in LAST_SMALL]).shape[0]
    n_pack = last_pack.shape[0]
    pack_all = _all_gather(jnp.concatenate([last_pack, loss_tile], 0)[None], 0, name="gather_small_grads")
    loss = jnp.sum(pack_all[:, n_pack, 0])
    (mix_sent[0],), last_token = _exchange_start([mix_grads], pack_all, name="mix_grads_start0")
    small_group(LAST_SMALL, pack_all[:, :n_last], "adamw_small")
    sh_rows = (n_pack - n_last) // N_DEV
    sh_parts = pack_all[:, n_last:n_pack].reshape(N_DEV, sh_rows, N_DEV, LANE)
    sh_parts = lax.dynamic_index_in_dim(sh_parts, me, 2, keepdims=False)
    sh_parts = jnp.pad(sh_parts, ((0, 0), (0, 16 - sh_rows), (0, 0)))

    def pack_shard(prefix):
        return _pad_rows(jnp.concatenate([P[prefix + n].reshape(-1, LANE) for n in SMALL_SHARD], 0), 16)[None]

    sg_, sd_, sm_, sv_ = _adamw(pack_shard(''), [sh_parts], pack_shard('m_'), pack_shard('v_'), name="adamw_channel")
    off = 0
    for n in SMALL_SHARD:
        rows = math.prod(P[n].shape) // LANE
        out[n] = tuple(t[0, off:off + rows].reshape(P[n].shape) for t in (sg_, sd_, sm_, sv_))
        off += rows

    dmod_all = pack_all[:, :DEPTH * 6 * D // LANE].reshape(N_DEV, DEPTH, 6 * D)
    dmod_cols = lax.dynamic_slice_in_dim(dmod_all, me * ncol, ncol, 2)
    g_ada = [_mm(c_pad, _pad_rows(dmod_cols[:, i], LANE), ta=True, name=f"ada_w_grad{i}")[None] for i in range(DEPTH)]

    def big(name, parts):
        res = _adamw(P[name], parts, P['m_' + name], P['v_' + name], name="adamw_" + name)
        out[name] = res
        return res[1]

    ff_parts = [_exchange_wait(ff_sent[i], last_token, name=f"ff_grads_wait{i}") for i in range(DEPTH)]
    mix_parts = [None] + [_exchange_wait(mix_sent[i], last_token, name=f"mix_grads_wait{i}") for i in range(1, DEPTH)]
    big('ada_w', g_ada)
    big('ff_w1', [p[0] for p in ff_parts])
    big('ff_w2', [p[1] for p in ff_parts])
    done = big('sg_w_in', [mix_parts[2][0]])
    mix_parts[0] = _exchange_wait(mix_sent[0], done, name="mix_grads_wait0")
    big('conv_w_in', [mix_parts[i][0] for i in range(DEPTH) if i % 3 == 0])
    row_names = ['conv_w_out', 'ssm_w_in', 'ssm_glu_w', 'ssm_w_out', 'sg_w_out']
    row_parts = ([mix_parts[i][1] for i in range(DEPTH) if i % 3 == 0] + mix_parts[1][:3] + [mix_parts[2][1]])
    small_group(S5_SMALL, mix_parts[1][3].reshape(N_DEV, -1, LANE), "adamw_s5")
    small_group(SG_SMALL, mix_parts[2][2].reshape(N_DEV, -1, LANE), "adamw_sg")
    row_w, row_m, row_v = (jnp.concatenate([P[pre + n] for n in row_names], 0) for pre in ('', 'm_', 'v_'))
    rw = _adamw(row_w, row_parts, row_m, row_v, name="adamw_row_sharded")
    off = 0
    for n in row_names:
        cnt = P[n].shape[0]
        out[n] = tuple(t[off:off + cnt] for t in rw)
        off += cnt

    return (loss, grad_x, *[out[n][0] for n in WEIGHTS], *[out[n][1] for n in WEIGHTS],
            *[out[n][2] for n in WEIGHTS], *[out[n][3] for n in WEIGHTS])
```

```python
import math

import jax
import jax.numpy as jnp
from jax import lax
from jax.experimental import pallas as pl
from jax.experimental.pallas import tpu as pltpu

F32 = jnp.float32
BF16 = jnp.bfloat16

N_DEV = 8
MESH_ID = pl.DeviceIdType.MESH
DEPTH = 4
EPS = 1e-6
S5_GROUPS, S5_GROUP, S5_STATE = 64, 16, 64
S5_LANES = S5_GROUPS * S5_STATE
S5_BLOCKS = 8
S5_CHUNK = 512
SG_HEADS, SG_CHUNK = 8, 128
LANE = 128
SUBLANE = 8
VMEM_LIMIT = 48 * 1024 * 1024
ADAM_LR, ADAM_B1, ADAM_B2, ADAM_EPS, ADAM_WD, ADAM_STEP = 0.001, 0.9, 0.999, 1e-08, 0.01, 10
GELU_C = math.sqrt(2.0 / math.pi)
GELU_A = 0.044715

WEIGHTS = ['ada_w', 'ada_b', 'norm1_g', 'norm2_g', 'ff_w1', 'ff_w2', 'final_g', 'conv_w_in', 'conv_w', 'conv_b',
           'conv_w_out', 'ssm_w_in', 'ssm_a_re', 'ssm_a_im', 'ssm_log_dt', 'ssm_b_re', 'ssm_b_im', 'ssm_c_re',
           'ssm_c_im', 'ssm_d', 'ssm_glu_w', 'ssm_glu_b', 'ssm_w_out', 'sg_w_in', 'sg_v_g', 'sg_w_s', 'sg_b_s',
           'sg_w_out']
INPUTS = ['x', 'c'] + WEIGHTS + ['loss_target'] + ['m_' + n for n in WEIGHTS] + ['v_' + n for n in WEIGHTS]
S5_SMALL = ['ssm_a_re', 'ssm_a_im', 'ssm_log_dt', 'ssm_b_re', 'ssm_b_im', 'ssm_c_re', 'ssm_c_im', 'ssm_d', 'ssm_glu_b']
SG_SMALL = ['sg_w_s', 'sg_b_s']
LAST_SMALL = ['ada_b', 'norm1_g', 'norm2_g', 'final_g']
SMALL_SHARD = ['conv_w', 'conv_b', 'sg_v_g']


def _params(*sem):
    return pltpu.CompilerParams(dimension_semantics=sem or None, vmem_limit_bytes=VMEM_LIMIT)


def _my_pos():
    return lax.axis_index("x"), lax.axis_index("y"), lax.axis_index("c")


def _my_index():
    x, y, c = _my_pos()
    return 4 * x + 2 * y + c


def _mm(a, b, *, name, ta=False, tb=False, out_dtypes=(F32,), epi=None, extras=(), a_fn=None, n_stats=0, bm=1024,
        bn=1024, bk=1024):
    a_chunks = a.shape[0] if a.ndim == 3 else 0
    b_chunks = b.shape[0] if b.ndim == 3 else 0
    assert not (a_chunks and ta) and not (b_chunks and tb)
    if a_chunks:
        m, k = a.shape[1], a_chunks * a.shape[2]
        bk = k
    else:
        m, k = (a.shape[1], a.shape[0]) if ta else a.shape
    if b_chunks:
        k2, n = b.shape[1], b_chunks * b.shape[2]
        bn = min(bn, b.shape[2])
    else:
        k2, n = (b.shape[1], b.shape[0]) if tb else b.shape
    assert k == k2, (a.shape, b.shape, ta, tb)
    bm, bn, bk = min(bm, m), min(bn, n), min(bk, k)
    assert m % bm == 0 and n % bn == 0 and k % bk == 0, (m, n, k, bm, bn, bk)
    nk = k // bk
    assert nk == 1 or n_stats == 0
    n_ex, n_out = len(extras), len(out_dtypes)
    dims = (((0 if ta else 1,), (1 if tb else 0,)), ((), ()))

    def body(*refs):
        a_ref, b_ref = refs[0], refs[1]
        ex_refs = refs[2:2 + n_ex]
        out_refs = refs[2 + n_ex:2 + n_ex + n_out]

        def finish(acc):
            outs = epi(acc, *[r[...] for r in ex_refs]) if epi is not None else (acc,)
            for r, o in zip(out_refs, outs[:n_out]):
                r[...] = o.astype(r.dtype)
            if n_stats:
                st_ref = refs[2 + n_ex + n_out]

                @pl.when(pl.program_id(0) == 0)
                def _():
                    st_ref[...] = jnp.zeros_like(st_ref)

                for q, row in enumerate(outs[n_out:]):
                    st_ref[q:q + 1, :] += row

        av = jnp.concatenate([a_ref[t] for t in range(a_chunks)], axis=1) if a_chunks else a_ref[...]
        if a_fn is not None:
            av = a_fn(av)
        part = lax.dot_general(av.astype(BF16), b_ref[...].astype(BF16), dims, preferred_element_type=F32)
        if nk == 1:
            finish(part)
            return
        acc_ref = refs[-1]
        kk = pl.program_id(2)

        @pl.when(kk == 0)
        def _():
            acc_ref[...] = part

        @pl.when(kk > 0)
        def _():
            acc_ref[...] += part

        @pl.when(kk == nk - 1)
        def _():
            finish(acc_ref[...])

    if a_chunks:
        a_spec = pl.BlockSpec((a_chunks, bm, a.shape[2]), lambda i, j, q: (0, i, 0))
    elif ta:
        a_spec = pl.BlockSpec((bk, bm), lambda i, j, q: (q, i))
    else:
        a_spec = pl.BlockSpec((bm, bk), lambda i, j, q: (i, q))
    if b_chunks:
        per = b.shape[2] // bn
        b_spec = pl.BlockSpec((None, bk, bn), lambda i, j, q: (j // per, q, j % per))
    elif tb:
        b_spec = pl.BlockSpec((bn, bk), lambda i, j, q: (j, q))
    else:
        b_spec = pl.BlockSpec((bk, bn), lambda i, j, q: (q, j))
    ex_specs = []
    for arr, kind in extras:
        if kind == 'mn':
            assert arr.shape == (m, n), (arr.shape, m, n)
            ex_specs.append(pl.BlockSpec((bm, bn), lambda i, j, q: (i, j)))
        else:
            assert arr.shape == (1, n), (arr.shape, n)
            ex_specs.append(pl.BlockSpec((1, bn), lambda i, j, q: (0, j)))
    out_shape = [jax.ShapeDtypeStruct((m, n), d) for d in out_dtypes]
    out_specs = [pl.BlockSpec((bm, bn), lambda i, j, q: (i, j)) for _ in out_dtypes]
    if n_stats:
        assert n_stats <= SUBLANE
        out_shape.append(jax.ShapeDtypeStruct((SUBLANE, n), F32))
        out_specs.append(pl.BlockSpec((SUBLANE, bn), lambda i, j, q: (0, j)))
    outs = pl.pallas_call(
        body, name=name, out_shape=tuple(out_shape), grid=(m // bm, n // bn, nk),
        in_specs=[a_spec, b_spec] + ex_specs, out_specs=tuple(out_specs),
        scratch_shapes=[pltpu.VMEM((bm, bn), F32)] if nk > 1 else [],
        compiler_params=_params(*(["arbitrary"] * 3 if n_stats else ["parallel", "parallel", "arbitrary"])),
    )(a, b, *[arr for arr, _ in extras])
    return outs if len(outs) > 1 else outs[0]


def _epi_residual_norm(acc, res, gate, w, sh):
    xn = res + gate * acc
    return xn, acc, xn * _rstd(xn) * w + sh


def _epi_norm_bwd(gated):
    def epi(dh, xv, w, dres, *gate):
        rstd = _rstd(xv)
        xn = xv * rstd
        dxn = dh * w
        dx = rstd * (dxn - xn * jnp.mean(dxn * xn, axis=-1, keepdims=True)) + dres
        stats = [jnp.sum(dh * xn, axis=0, keepdims=True), jnp.sum(dh, axis=0, keepdims=True)]
        if not gated:
            return (dx, *stats)
        yv, g = gate
        return (dx, dx * g, *stats, jnp.sum(dx * yv.astype(F32), axis=0, keepdims=True))
    return epi


def _epi_loss_head(f, x_mid, g, tgt, fg):
    xv = x_mid + g * f
    rstd = _rstd(xv)
    xn = xv * rstd
    err = xn * fg - tgt
    loss = 0.5 * jnp.sum(jnp.mean(err * err, axis=-1, keepdims=True))
    dout = err * (1.0 / xv.shape[-1])
    dxn = dout * fg
    dx = rstd * (dxn - xn * jnp.mean(dxn * xn, axis=-1, keepdims=True))
    return (dx, dx * g, jnp.full((1, xv.shape[-1]), loss, F32), jnp.sum(dout * xn, axis=0, keepdims=True),
            jnp.sum(dx * f, axis=0, keepdims=True))


def _epi_glu_bwd(dy2, yv, t):
    sig = 1.0 / (1.0 + jnp.exp(-t))
    dt = dy2 * yv * sig * (1.0 - sig)
    return dt, dy2 * sig, jnp.sum(dt, axis=0, keepdims=True)


def _wgrad(acts, cots, *, name, a_fn=None, bm=1024, bn=512):
    return _mm(acts, cots, ta=True, name=name, out_dtypes=(BF16,), a_fn=a_fn, bm=bm, bn=bn, bk=acts.shape[0])


def _square(a):
    af = a.astype(F32)
    return af * af


def _rstd(xv):
    return lax.rsqrt(jnp.mean(xv * xv, axis=-1, keepdims=True) + EPS)


def _normmod_fwd(x, w, sh, *, name, tm=512):
    L, D = x.shape

    def body(x_ref, w_ref, s_ref, h_ref):
        xv = x_ref[...]
        h_ref[...] = (xv * _rstd(xv) * w_ref[...] + s_ref[...]).astype(h_ref.dtype)

    row = pl.BlockSpec((tm, D), lambda i: (i, 0))
    vec = pl.BlockSpec((1, D), lambda i: (0, 0))
    return pl.pallas_call(body, name=name, out_shape=jax.ShapeDtypeStruct((L, D), BF16), grid=(L // tm,),
                          in_specs=[row, vec, vec], out_specs=row, compiler_params=_params("parallel"))(x, w, sh)


def _shift_down(v, k):
    row = lax.broadcasted_iota(jnp.int32, v.shape, 0)
    return jnp.where(row >= k, pltpu.roll(v, k, 0), 0.0)


def _shift_up(v, k):
    n = v.shape[0]
    row = lax.broadcasted_iota(jnp.int32, v.shape, 0)
    return jnp.where(row < n - k, pltpu.roll(v, n - k, 0), 0.0)


def _conv_views(L, D):
    return [pl.BlockSpec((L, LANE), lambda j, s=s: (0, s * (D // LANE) + j)) for s in range(3)]


def _conv_fwd(bcx, wb, *, name):
    L, D = bcx.shape[0], bcx.shape[1] // 3

    def body(b_ref, c_ref, x_ref, wb_ref, p_ref):
        z = c_ref[...].astype(F32) * x_ref[...].astype(F32)
        conv = (wb_ref[0:1, :] * _shift_down(z, 2) + wb_ref[1:2, :] * _shift_down(z, 1)
                + wb_ref[2:3, :] * z + wb_ref[3:4, :])
        p_ref[...] = (b_ref[...].astype(F32) * conv).astype(p_ref.dtype)

    col = pl.BlockSpec((L, LANE), lambda j: (0, j))
    return pl.pallas_call(body, name=name, out_shape=jax.ShapeDtypeStruct((L, D), BF16), grid=(D // LANE,),
                          in_specs=_conv_views(L, D) + [pl.BlockSpec((SUBLANE, LANE), lambda j: (0, j))],
                          out_specs=col, compiler_params=_params("parallel"))(bcx, bcx, bcx, wb)


def _conv_bwd(dp, bcx, wb, *, name):
    L, D = dp.shape

    def body(dp_ref, b_ref, c_ref, x_ref, wb_ref, d3_ref, st_ref):
        cv, xv = c_ref[...].astype(F32), x_ref[...].astype(F32)
        z = cv * xv
        z1, z2 = _shift_down(z, 1), _shift_down(z, 2)
        w0, w1, w2 = wb_ref[0:1, :], wb_ref[1:2, :], wb_ref[2:3, :]
        conv = w0 * z2 + w1 * z1 + w2 * z + wb_ref[3:4, :]
        dpv = dp_ref[...].astype(F32)
        d3_ref[0] = (dpv * conv).astype(d3_ref.dtype)
        dconv = dpv * b_ref[...].astype(F32)
        dz = w2 * dconv + w1 * _shift_up(dconv, 1) + w0 * _shift_up(dconv, 2)
        d3_ref[1] = (dz * xv).astype(d3_ref.dtype)
        d3_ref[2] = (dz * cv).astype(d3_ref.dtype)
        st_ref[...] = jnp.zeros_like(st_ref)
        st_ref[0:1, :] = jnp.sum(dconv * z2, axis=0, keepdims=True)
        st_ref[1:2, :] = jnp.sum(dconv * z1, axis=0, keepdims=True)
        st_ref[2:3, :] = jnp.sum(dconv * z, axis=0, keepdims=True)
        st_ref[3:4, :] = jnp.sum(dconv, axis=0, keepdims=True)

    col = pl.BlockSpec((L, LANE), lambda j: (0, j))
    vec = pl.BlockSpec((SUBLANE, LANE), lambda j: (0, j))
    return pl.pallas_call(body, name=name,
                          out_shape=(jax.ShapeDtypeStruct((3, L, D), BF16), jax.ShapeDtypeStruct((SUBLANE, D), F32)),
                          grid=(D // LANE,), in_specs=[col] + _conv_views(L, D) + [vec],
                          out_specs=(pl.BlockSpec((3, L, LANE), lambda j: (0, 0, j)), vec),
                          compiler_params=_params("parallel"))(dp, bcx, bcx, bcx, wb)


def _sg_fwd(uv, vg, ws, bsb, *, name, tr=512):
    L, D = uv.shape[0], uv.shape[1] // 2

    def body(uv_ref, vg_ref, ws_ref, bsb_ref, p_ref):
        for ci in range(tr // SG_CHUNK):
            rows = slice(ci * SG_CHUNK, (ci + 1) * SG_CHUNK)
            v = uv_ref[rows, D:2 * D]
            vn = (v * _rstd(v) * vg_ref[...]).astype(BF16)
            for h in range(SG_HEADS):
                cols = slice(h * LANE, (h + 1) * LANE)
                vm = jnp.dot(ws_ref[h], vn[:, cols], preferred_element_type=F32) + bsb_ref[h]
                p_ref[rows, cols] = (uv_ref[rows, cols] * vm).astype(p_ref.dtype)

    full3 = pl.BlockSpec((SG_HEADS, SG_CHUNK, LANE), lambda i: (0, 0, 0))
    return pl.pallas_call(body, name=name, out_shape=jax.ShapeDtypeStruct((L, D), BF16), grid=(L // tr,),
                          in_specs=[pl.BlockSpec((tr, 2 * D), lambda i: (i, 0)), pl.BlockSpec((1, D), lambda i: (0, 0)),
                                    full3, full3],
                          out_specs=pl.BlockSpec((tr, D), lambda i: (i, 0)),
                          compiler_params=_params("parallel"))(uv, vg, ws, bsb)


def _sg_bwd(dp, uv, vg, ws, wst, bsb, *, name, tr=512):
    L, D = dp.shape

    def body(dp_ref, uv_ref, vg_ref, ws_ref, wst_ref, bsb_ref, duv_ref, dws_ref, dbs_ref, st_ref, dvn_ref):
        i = pl.program_id(0)

        @pl.when(i == 0)
        def _():
            dws_ref[...] = jnp.zeros_like(dws_ref)
            dbs_ref[...] = jnp.zeros_like(dbs_ref)
            st_ref[...] = jnp.zeros_like(st_ref)

        for ci in range(tr // SG_CHUNK):
            rows = slice(ci * SG_CHUNK, (ci + 1) * SG_CHUNK)
            v = uv_ref[rows, D:2 * D]
            rstd = _rstd(v)
            vhat = v * rstd
            vn = (vhat * vg_ref[...]).astype(BF16)
            for h in range(SG_HEADS):
                cols = slice(h * LANE, (h + 1) * LANE)
                vm = jnp.dot(ws_ref[h], vn[:, cols], preferred_element_type=F32) + bsb_ref[h]
                dph = dp_ref[rows, cols]
                duv_ref[rows, cols] = (dph * vm).astype(duv_ref.dtype)
                dvm = dph * uv_ref[rows, cols]
                dbs_ref[h] += dvm
                dvmb = dvm.astype(BF16)
                dws_ref[h] += lax.dot_general(dvmb, vn[:, cols], (((1,), (1,)), ((), ())),
                                              preferred_element_type=F32)
                dvn_ref[rows, cols] = jnp.dot(wst_ref[h], dvmb, preferred_element_type=F32)
            dvn = dvn_ref[rows, :]
            gv = dvn * vg_ref[...]
            dv = rstd * (gv - vhat * jnp.mean(gv * vhat, axis=-1, keepdims=True))
            duv_ref[rows, D:2 * D] = dv.astype(duv_ref.dtype)
            st_ref[0:1, :] += jnp.sum(dvn * vhat, axis=0, keepdims=True)

    full3 = pl.BlockSpec((SG_HEADS, SG_CHUNK, LANE), lambda i: (0, 0, 0))
    acc3 = jax.ShapeDtypeStruct((SG_HEADS, SG_CHUNK, LANE), F32)
    return pl.pallas_call(
        body, name=name,
        out_shape=(jax.ShapeDtypeStruct((L, 2 * D), BF16), acc3, acc3, jax.ShapeDtypeStruct((SUBLANE, D), F32)),
        grid=(L // tr,),
        in_specs=[pl.BlockSpec((tr, D), lambda i: (i, 0)), pl.BlockSpec((tr, 2 * D), lambda i: (i, 0)),
                  pl.BlockSpec((1, D), lambda i: (0, 0)), full3, full3, full3],
        out_specs=(pl.BlockSpec((tr, 2 * D), lambda i: (i, 0)), full3, full3,
                   pl.BlockSpec((SUBLANE, D), lambda i: (0, 0))),
        scratch_shapes=[pltpu.VMEM((tr, D), F32)],
        compiler_params=_params("arbitrary"))(dp, uv, vg, ws, wst, bsb)


def _gelu(x):
    return 0.5 * x * (1.0 + jnp.tanh(GELU_C * (x + GELU_A * x * x * x)))


def _gelu_grad(x):
    th = jnp.tanh(GELU_C * (x + GELU_A * x * x * x))
    return 0.5 * (1.0 + th) + 0.5 * x * (1.0 - th * th) * GELU_C * (1.0 + 3.0 * GELU_A * x * x)


def _cmul_add(xr, xi, ar, ai, br, bi):
    return xr + ar * br - ai * bi, xi + ar * bi + ai * br


def _cmul_conj_add(xr, xi, ar, ai, br, bi):
    return xr + ar * br + ai * bi, xi + ar * bi - ai * br


def _to_subchunk_order(src_ref, dst_ref, n):
    for k in range(n):
        dst_ref[pl.ds(SUBLANE * k, SUBLANE), :] = src_ref[pl.ds(k, SUBLANE, stride=n), :].astype(dst_ref.dtype)


def _to_time_order(src_ref, dst_ref, n):
    for m in range(n):
        r, k = divmod(SUBLANE * m, n)
        dst_ref[pl.ds(SUBLANE * m, SUBLANE), :] = src_ref[pl.ds(SUBLANE * k + r, SUBLANE, stride=SUBLANE), :]


def _s5_fwd(u, bre, bim, cre, cim, pw, pos, dsk, *, name, tc=S5_CHUNK):
    L, D = u.shape
    W = S5_LANES // S5_BLOCKS
    nt = L // tc
    n = tc // SUBLANE

    def sub(k):
        return pl.ds(SUBLANE * k, SUBLANE)

    def body(u_ref, bre_ref, bim_ref, cre_ref, cim_ref, pw_ref, pos_ref, d_ref, sre_ref, sim_ref, ypre_ref, yg_ref,
             carry, up, yp):
        t = pl.program_id(1)

        @pl.when(t == 0)
        def _():
            carry[...] = jnp.zeros_like(carry)

        _to_subchunk_order(u_ref, up, n)
        uv = up[...]
        ub = uv.astype(BF16)
        sre_ref[...] = jnp.dot(ub, bre_ref[...], preferred_element_type=F32)
        sim_ref[...] = jnp.dot(ub, bim_ref[...], preferred_element_type=F32)

        ar, ai = pw_ref[8], pw_ref[9]
        xr = jnp.zeros((SUBLANE, W), F32)
        xi = jnp.zeros((SUBLANE, W), F32)
        for k in range(n):
            xr, xi = _cmul_add(sre_ref[sub(k), :], sim_ref[sub(k), :], ar, ai, xr, xi)
            sre_ref[sub(k), :] = xr
            sim_ref[sub(k), :] = xi
        for q, d in enumerate((1, 2, 4)):
            xr, xi = _cmul_add(xr, xi, pw_ref[2 * q], pw_ref[2 * q + 1], pltpu.roll(xr, d, 0), pltpu.roll(xi, d, 0))
        cr, ci = carry[0], carry[1]
        xr, xi = _cmul_add(xr, xi, pw_ref[6], pw_ref[7], cr, ci)
        first = lax.broadcasted_iota(jnp.int32, (SUBLANE, W), 0) == 0
        er = jnp.where(first, cr, pltpu.roll(xr, 1, 0))
        ei = jnp.where(first, ci, pltpu.roll(xi, 1, 0))
        last = slice(SUBLANE - 1, SUBLANE)
        carry[0] = jnp.broadcast_to(xr[last, :], (SUBLANE, W))
        carry[1] = jnp.broadcast_to(xi[last, :], (SUBLANE, W))
        for k in range(n):
            sr, si = _cmul_add(sre_ref[sub(k), :], sim_ref[sub(k), :], pos_ref[0, k:k + 1, :], pos_ref[1, k:k + 1, :],
                               er, ei)
            sre_ref[sub(k), :] = sr
            sim_ref[sub(k), :] = si
        yp[...] = (jnp.dot(sre_ref[...].astype(BF16), cre_ref[...], preferred_element_type=F32)
                   - jnp.dot(sim_ref[...].astype(BF16), cim_ref[...], preferred_element_type=F32) + d_ref[...] * uv)
        _to_time_order(yp, ypre_ref, n)
        yg_ref[...] = _gelu(ypre_ref[...])

    ch = pl.BlockSpec((tc, LANE), lambda j, t: (t, j))
    st = pl.BlockSpec((tc, W), lambda j, t: (t, j))
    bsp = pl.BlockSpec((None, LANE, W), lambda j, t: (j, 0, 0))
    csp = pl.BlockSpec((None, W, LANE), lambda j, t: (j, 0, 0))
    return pl.pallas_call(
        body, name=name,
        out_shape=(jax.ShapeDtypeStruct((L, S5_LANES), F32), jax.ShapeDtypeStruct((L, S5_LANES), F32),
                   jax.ShapeDtypeStruct((L, D), F32), jax.ShapeDtypeStruct((L, D), F32)),
        grid=(S5_BLOCKS, nt),
        in_specs=[ch, bsp, bsp, csp, csp, pl.BlockSpec((10, SUBLANE, W), lambda j, t: (0, 0, j)),
                  pl.BlockSpec((2, n, W), lambda j, t: (0, 0, j)), pl.BlockSpec((1, LANE), lambda j, t: (0, j))],
        out_specs=(st, st, ch, ch),
        scratch_shapes=[pltpu.VMEM((2, SUBLANE, W), F32), pltpu.VMEM((tc, LANE), F32), pltpu.VMEM((tc, LANE), F32)],
        compiler_params=_params("parallel", "arbitrary"))(u, bre, bim, cre, cim, pw, pos, dsk)


def _s5_bwd(dy, u, sre, sim, bre, bim, cre, cim, pwr, posr, dsk, *, name, tc=S5_CHUNK):
    L, D = u.shape
    W = S5_LANES // S5_BLOCKS
    nt = L // tc
    n = tc // SUBLANE
    nt_dims = (((1,), (1,)), ((), ()))
    tn_dims = (((0,), (0,)), ((), ()))

    def sub(k):
        return pl.ds(SUBLANE * k, SUBLANE)

    def body(dy_ref, u_ref, sre_ref, sim_ref, bre_ref, bim_ref, cre_ref, cim_ref, pw_ref, pos_ref, d_ref,
             du_ref, dbre_ref, dbim_ref, dcre_ref, dcim_ref, ga_ref, dd_ref, gre, gim, carry, gacc, up, dyp):
        t = pl.program_id(1)

        @pl.when(t == 0)
        def _():
            for r in (carry, gacc, dbre_ref, dbim_ref, dcre_ref, dcim_ref, ga_ref, dd_ref):
                r[...] = jnp.zeros_like(r)

        _to_subchunk_order(dy_ref, dyp, n)
        _to_subchunk_order(u_ref, up, n)
        dyv, uv = dyp[...], up[...]
        dyb, ub = dyv.astype(BF16), uv.astype(BF16)
        gre[...] = lax.dot_general(dyb, cre_ref[...], nt_dims, preferred_element_type=F32)
        gim[...] = -lax.dot_general(dyb, cim_ref[...], nt_dims, preferred_element_type=F32)
        br, bi = pw_ref[8], pw_ref[9]
        xr = jnp.zeros((SUBLANE, W), F32)
        xi = jnp.zeros((SUBLANE, W), F32)
        for k in reversed(range(n)):
            xr, xi = _cmul_add(gre[sub(k), :], gim[sub(k), :], br, bi, xr, xi)
            gre[sub(k), :] = xr
            gim[sub(k), :] = xi
        for q, d in enumerate((1, 2, 4)):
            xr, xi = _cmul_add(xr, xi, pw_ref[2 * q], pw_ref[2 * q + 1], pltpu.roll(xr, SUBLANE - d, 0),
                               pltpu.roll(xi, SUBLANE - d, 0))
        cr, ci = carry[0], carry[1]
        xr, xi = _cmul_add(xr, xi, pw_ref[6], pw_ref[7], cr, ci)
        top = lax.broadcasted_iota(jnp.int32, (SUBLANE, W), 0) == SUBLANE - 1
        er = jnp.where(top, cr, pltpu.roll(xr, SUBLANE - 1, 0))
        ei = jnp.where(top, ci, pltpu.roll(xi, SUBLANE - 1, 0))
        carry[0] = jnp.broadcast_to(xr[0:1, :], (SUBLANE, W))
        carry[1] = jnp.broadcast_to(xi[0:1, :], (SUBLANE, W))
        nr, ni = er, ei
        acc_r = jnp.zeros((SUBLANE, W), F32)
        acc_i = jnp.zeros((SUBLANE, W), F32)
        for k in reversed(range(n)):
            place = slice(n - 1 - k, n - k)
            gr, gi = _cmul_conj_add(gre[sub(k), :], gim[sub(k), :], pos_ref[0, place, :], pos_ref[1, place, :], er, ei)
            gre[sub(k), :] = gr
            gim[sub(k), :] = gi
            sr, si = sre_ref[sub(k), :], sim_ref[sub(k), :]
            acc_r = acc_r + sr * nr + si * ni
            acc_i = acc_i + sr * ni - si * nr
            nr, ni = gr, gi
        gacc[0] += acc_r
        gacc[1] += acc_i
        grb, gib = gre[...].astype(BF16), gim[...].astype(BF16)
        dyp[...] = (lax.dot_general(grb, bre_ref[...], nt_dims, preferred_element_type=F32)
                    + lax.dot_general(gib, bim_ref[...], nt_dims, preferred_element_type=F32) + d_ref[...] * dyv)
        _to_time_order(dyp, up, n)
        du_ref[...] = up[...].astype(du_ref.dtype)
        dbre_ref[...] += lax.dot_general(ub, grb, tn_dims, preferred_element_type=F32)
        dbim_ref[...] += lax.dot_general(ub, gib, tn_dims, preferred_element_type=F32)
        dcre_ref[...] += lax.dot_general(sre_ref[...].astype(BF16), dyb, tn_dims, preferred_element_type=F32)
        dcim_ref[...] -= lax.dot_general(sim_ref[...].astype(BF16), dyb, tn_dims, preferred_element_type=F32)
        dd_ref[0:1, :] += jnp.sum(dyv * uv, axis=0, keepdims=True)

        @pl.when(t == nt - 1)
        def _():
            ga_ref[0:1, :] = jnp.sum(gacc[0], axis=0, keepdims=True)
            ga_ref[1:2, :] = jnp.sum(gacc[1], axis=0, keepdims=True)

    ch = pl.BlockSpec((tc, LANE), lambda j, t: (nt - 1 - t, j))
    st = pl.BlockSpec((tc, W), lambda j, t: (nt - 1 - t, j))
    bsp = pl.BlockSpec((None, LANE, W), lambda j, t: (j, 0, 0))
    csp = pl.BlockSpec((None, W, LANE), lambda j, t: (j, 0, 0))
    return pl.pallas_call(
        body, name=name,
        out_shape=(jax.ShapeDtypeStruct((L, D), BF16),
                   jax.ShapeDtypeStruct((S5_BLOCKS, LANE, W), F32), jax.ShapeDtypeStruct((S5_BLOCKS, LANE, W), F32),
                   jax.ShapeDtypeStruct((S5_BLOCKS, W, LANE), F32), jax.ShapeDtypeStruct((S5_BLOCKS, W, LANE), F32),
                   jax.ShapeDtypeStruct((SUBLANE, S5_LANES), F32), jax.ShapeDtypeStruct((SUBLANE, D), F32)),
        grid=(S5_BLOCKS, nt),
        in_specs=[ch, ch, st, st, bsp, bsp, csp, csp, pl.BlockSpec((10, SUBLANE, W), lambda j, t: (0, 0, j)),
                  pl.BlockSpec((2, n, W), lambda j, t: (0, 0, j)), pl.BlockSpec((1, LANE), lambda j, t: (0, j))],
        out_specs=(ch, bsp, bsp, csp, csp, pl.BlockSpec((SUBLANE, W), lambda j, t: (0, j)),
                   pl.BlockSpec((SUBLANE, LANE), lambda j, t: (0, j))),
        scratch_shapes=[pltpu.VMEM((tc, W), F32), pltpu.VMEM((tc, W), F32), pltpu.VMEM((2, SUBLANE, W), F32),
                        pltpu.VMEM((2, SUBLANE, W), F32), pltpu.VMEM((tc, LANE), F32), pltpu.VMEM((tc, LANE), F32)],
        compiler_params=_params("parallel", "arbitrary"))(dy, u, sre, sim, bre, bim, cre, cim, pwr, posr, dsk)


def _s5_prep(a_re, a_im, log_dt, b_re, b_im, c_re, c_im):
    dt = jnp.exp(log_dt)[:, None]
    mag = jnp.exp(a_re * dt)
    abar_re = mag * jnp.cos(a_im * dt)
    abar_im = mag * jnp.sin(a_im * dt)
    den = a_re * a_re + a_im * a_im
    nr = abar_re - 1.0
    ni = abar_im
    f_re = ((nr * a_re + ni * a_im) / den)[..., None]
    f_im = ((ni * a_re - nr * a_im) / den)[..., None]
    bbar_re = f_re * b_re - f_im * b_im
    bbar_im = f_re * b_im + f_im * b_re
    eye = jnp.eye(S5_GROUPS // S5_BLOCKS, dtype=F32)
    gb = S5_GROUPS // S5_BLOCKS

    def blk_b(bb):
        t = bb.reshape(S5_BLOCKS, gb, S5_STATE, S5_GROUP)
        return jnp.einsum('jgph,gk->jghkp', t, eye).reshape(S5_BLOCKS, gb * S5_GROUP, gb * S5_STATE)

    def blk_c(cc):
        t = cc.reshape(S5_BLOCKS, gb, S5_GROUP, S5_STATE)
        return jnp.einsum('jghp,gk->jgpkh', t, eye).reshape(S5_BLOCKS, gb * S5_STATE, gb * S5_GROUP)

    return (abar_re.reshape(1, S5_LANES), abar_im.reshape(1, S5_LANES), blk_b(bbar_re), blk_b(bbar_im),
            blk_c(c_re), blk_c(c_im))


def _cpowers(ar, ai, count):
    pr, pi, m = ar, ai, 1
    while m < count:
        tr, ti = pr[m - 1:m], pi[m - 1:m]
        pr, pi = jnp.concatenate([pr, pr * tr - pi * ti], 0), jnp.concatenate([pi, pr * ti + pi * tr], 0)
        m *= 2
    return pr, pi


def _s5_power_tables(ar, ai, n):
    pr, pi = _cpowers(ar, ai, n)
    qr, qi = _cpowers(pr[n - 1:n], pi[n - 1:n], SUBLANE)
    row = jnp.arange(SUBLANE)[:, None]
    lanes = ar.shape[1]

    def tables(sign, keep, order):
        out = []
        for d in (1, 2, 4):
            out += [jnp.where(keep(d), qr[d - 1:d], 0.0), jnp.where(keep(d), sign * qi[d - 1:d], 0.0)]
        out += [jnp.concatenate([qr[r:r + 1] for r in order], 0), sign * jnp.concatenate([qi[r:r + 1] for r in order], 0),
                ar, sign * ai]
        return jnp.stack([jnp.broadcast_to(o, (SUBLANE, lanes)) for o in out])

    fwd = tables(1.0, lambda d: row >= d, list(range(SUBLANE)))
    rev = tables(-1.0, lambda d: row + d <= SUBLANE - 1, list(reversed(range(SUBLANE))))
    return fwd, rev, jnp.stack([pr, pi])


ADAMW_PART_BLOCK_BYTES = 2 * 1024 * 1024


def _adamw(w, parts, m, v, *, name):
    n, R, C = w.shape
    assert len(parts) == n
    P = parts[0].shape[0]
    tr = R
    while P * tr * C * parts[0].dtype.itemsize > ADAMW_PART_BLOCK_BYTES and tr % 16 == 0:
        tr //= 2
    c1 = 1.0 / (1.0 - ADAM_B1 ** ADAM_STEP)
    c2 = 1.0 / (1.0 - ADAM_B2 ** ADAM_STEP)

    def body(*refs):
        w_ref, m_ref, v_ref = refs[:3]
        p_refs = refs[3:3 + n]
        g_ref, d_ref, nm_ref, nv_ref = refs[3 + n:]
        layer = pl.program_id(0)
        for q, p_ref in enumerate(p_refs):
            @pl.when(layer == q)
            def _(p_ref=p_ref):
                g = p_ref[0].astype(F32)
                for s in range(1, P):
                    g = g + p_ref[s].astype(F32)
                nm = ADAM_B1 * m_ref[...] + (1.0 - ADAM_B1) * g
                nv = ADAM_B2 * v_ref[...] + (1.0 - ADAM_B2) * (g * g)
                g_ref[...] = g
                nm_ref[...] = nm
                nv_ref[...] = nv
                d_ref[...] = -ADAM_LR * ((nm * c1) / (jnp.sqrt(nv * c2) + ADAM_EPS) + ADAM_WD * w_ref[...])

    row = pl.BlockSpec((None, tr, C), lambda l, i: (l, i, 0))
    part_specs = [pl.BlockSpec((P, tr, C), lambda l, i, q=q: (0, jnp.where(l == q, i, 0), 0)) for q in range(n)]
    out = jax.ShapeDtypeStruct((n, R, C), F32)
    return pl.pallas_call(body, name=name, out_shape=(out, out, out, out), grid=(n, R // tr),
                          in_specs=[row, row, row] + part_specs, out_specs=(row, row, row, row),
                          compiler_params=_params("arbitrary", "arbitrary"))(w, m, v, *parts)


def _all_gather(xs, axis, *, name):
    m = xs.shape[axis]
    out_shape = list(xs.shape)
    out_shape[axis] = N_DEV * m

    def body(x_ref, out_ref, send_sems, recv_sems, local_sem):
        x, y, c = _my_pos()
        me, sibling = (x, y, c), (x, y, 1 - c)
        chips = [(1 - x, y), (x, 1 - y), (1 - x, 1 - y)]

        def blk(px, py, pc):
            idx = [slice(None)] * 3
            idx[axis] = pl.ds((4 * px + 2 * py + pc) * m, m)
            return out_ref.at[tuple(idx)]

        def copy(k, block, to, src=None):
            return pltpu.make_async_remote_copy(src_ref=blk(*block) if src is None else src, dst_ref=blk(*block),
                                                send_sem=send_sems.at[k], recv_sem=recv_sems.at[k],
                                                device_id=to, device_id_type=MESH_ID)

        mine = pltpu.make_async_copy(x_ref, blk(*me), local_sem)
        mine.start()
        first = [copy(0, me, sibling, src=x_ref)]
        first += [copy(1 + j, me, (*chip, c), src=x_ref) for j, chip in enumerate(chips)]
        for cp in first:
            cp.start()
        passed = [copy(4 + j, (*chip, c), sibling) for j, chip in enumerate(chips)]
        for j, chip in enumerate(chips):
            copy(1 + j, (*chip, c), me).wait_recv()
            passed[j].start()
        copy(0, sibling, me).wait_recv()
        for j, chip in enumerate(chips):
            copy(4 + j, (*chip, 1 - c), me).wait_recv()
        for cp in first + passed:
            cp.wait_send()
        mine.wait()

    hbm = pl.BlockSpec(memory_space=pl.ANY)
    return pl.pallas_call(body, name=name, out_shape=jax.ShapeDtypeStruct(tuple(out_shape), xs.dtype),
                          in_specs=[hbm], out_specs=hbm,
                          scratch_shapes=[pltpu.SemaphoreType.DMA((N_DEV - 1,)), pltpu.SemaphoreType.DMA((N_DEV - 1,)),
                                          pltpu.SemaphoreType.DMA],
                          compiler_params=pltpu.CompilerParams(has_side_effects=True))(xs)


NEAR_PEERS = (1, 2, 4, 6)
RELAY_PEERS = (2, 4, 6)


def _block(ref, axis, idx, m):
    return ref.at[pl.ds(idx * m, m), :] if axis == 0 else ref.at[:, pl.ds(idx * m, m)]


def _exchange_copies(metas, src_refs, zone_refs, send_sems, recv_sems, base, phase):
    x, y, c = _my_pos()
    me = 4 * x + 2 * y + c

    def place(r):
        pos = (1 - x if r & 4 else x, 1 - y if r & 2 else y, 1 - c if r & 1 else c)
        return pos, 4 * pos[0] + 2 * pos[1] + pos[2]

    def copies(r, to, src, dst, arrival):
        return tuple(pltpu.make_async_remote_copy(src_ref=src, dst_ref=d, send_sem=send_sems.at[base + r - 1],
                                                  recv_sem=recv_sems.at[base + r - 1], device_id=to,
                                                  device_id_type=MESH_ID) for d in (dst, arrival))

    pairs, own = [], []
    if phase == 'relay':
        sibling, _ = place(1)
        for r in RELAY_PEERS:
            held, comes = place(r)[1], place(r | 1)[1]
            for (kind, axis, m), z_ref in zip(metas, zone_refs):
                pairs.append(copies(r, sibling, _block(z_ref, axis, held, m), _block(z_ref, axis, held, m),
                                    _block(z_ref, axis, comes, m)))
        return pairs, own
    for r in (NEAR_PEERS if phase == 'near' else range(1, N_DEV)):
        pos, peer = place(r)
        for (kind, axis, m), s_ref, z_ref in zip(metas, src_refs, zone_refs):
            if kind == 'gather':
                pairs.append(copies(r, pos, s_ref, _block(z_ref, axis, me, m), _block(z_ref, axis, peer, m)))
            else:
                pairs.append(copies(r, pos, _block(s_ref, axis, peer, m), z_ref.at[me], z_ref.at[peer]))
    for (kind, axis, m), s_ref, z_ref in zip(metas, src_refs, zone_refs):
        src, dst = (s_ref, _block(z_ref, axis, me, m)) if kind == 'gather' else (_block(s_ref, axis, me, m), z_ref.at[me])
        own.append(pltpu.make_async_copy(src, dst, recv_sems.at[base + N_DEV - 1]))
    return pairs, own


def _exchange_start(groups, after, *, name, relayed=False):
    flat = [it for g in groups for it in g]
    n, ng = len(flat), len(groups)
    metas = [it[2] for it in flat]
    bounds = [(sum(len(g) for g in groups[:q]), sum(len(g) for g in groups[:q + 1])) for q in range(ng)]
    phase = 'near' if relayed else 'all'

    def body(*refs):
        src_refs = refs[:n]
        send_sems, recv_sems = refs[n + 1], refs[n + 2]
        zone_refs = refs[2 * n + 3:3 * n + 3]
        token = refs[-1]
        for q, (lo, hi) in enumerate(bounds):
            pairs, own = _exchange_copies(metas[lo:hi], src_refs[lo:hi], zone_refs[lo:hi], send_sems, recv_sems,
                                          q * N_DEV, phase)
            for outgoing, _ in pairs:
                outgoing.start()
            for cp in own:
                cp.start()
        token[...] = jnp.zeros_like(token)

    hbm = pl.BlockSpec(memory_space=pltpu.HBM)
    sem = pl.BlockSpec(memory_space=pltpu.SEMAPHORE)
    srcs = [it[0] for it in flat]
    res = pl.pallas_call(
        body, name=name,
        out_shape=(pltpu.SemaphoreType.DMA((ng * N_DEV,)), pltpu.SemaphoreType.DMA((ng * N_DEV,)),
                   *[pltpu.HBM(a.shape, a.dtype) for a in srcs], *[pltpu.HBM(it[1], it[0].dtype) for it in flat],
                   jax.ShapeDtypeStruct((SUBLANE, LANE), F32)),
        in_specs=[hbm] * n + [pl.BlockSpec(memory_space=pl.ANY)],
        out_specs=(sem, sem, *[hbm] * (2 * n), pl.BlockSpec(memory_space=pltpu.VMEM)),
        input_output_aliases={q: 2 + q for q in range(n)},
        compiler_params=pltpu.CompilerParams(has_side_effects=pltpu.SideEffectType.DATAFLOW_SIDE_EFFECTING),
    )(*[pltpu.with_memory_space_constraint(a, pltpu.HBM) for a in srcs], after)
    handles = [(res[0], res[1], q * N_DEV, phase, list(res[2 + lo:2 + hi]), list(res[2 + n + lo:2 + n + hi]),
                metas[lo:hi]) for q, (lo, hi) in enumerate(bounds)]
    return handles, res[-1]


def _exchange_wait(handle, after, *, name):
    send_sems, recv_sems, base, phase, srcs, zones, metas = handle
    ns, nz = len(srcs), len(zones)

    def body(*refs):
        src_refs, zone_refs = refs[:ns], refs[ns:ns + nz]
        s_sems, r_sems = refs[ns + nz], refs[ns + nz + 1]
        pairs, own = _exchange_copies(metas, src_refs, zone_refs, s_sems, r_sems, base, phase)
        for outgoing, incoming in pairs:
            outgoing.wait_send()
            incoming.wait_recv()
        for cp in own:
            cp.wait()

    hbm = pl.BlockSpec(memory_space=pltpu.HBM)
    sem = pl.BlockSpec(memory_space=pltpu.SEMAPHORE)
    arrays = srcs + zones
    res = pl.pallas_call(
        body, name=name,
        out_shape=tuple(pltpu.HBM(a.shape, a.dtype) for a in arrays),
        in_specs=[hbm] * (ns + nz) + [sem, sem, pl.BlockSpec(memory_space=pl.ANY)],
        out_specs=tuple([hbm] * (ns + nz)),
        input_output_aliases={q: q for q in range(ns + nz)},
        compiler_params=pltpu.CompilerParams(has_side_effects=pltpu.SideEffectType.DATAFLOW_SIDE_EFFECTING),
    )(*arrays, send_sems, recv_sems, after)
    return list(res[ns:])


def _exchange_relay(handle, after, *, name):
    metas = handle[6]
    zones = _exchange_wait(handle, after, name=name + "_in")
    nz = len(zones)

    def body(*refs):
        zone_refs = refs[:nz]
        send_sems, recv_sems = refs[nz], refs[nz + 1]
        pairs, _ = _exchange_copies(metas, (), zone_refs, send_sems, recv_sems, 0, 'relay')
        for outgoing, _ in pairs:
            outgoing.start()
        refs[-1][...] = jnp.zeros_like(refs[-1])

    hbm = pl.BlockSpec(memory_space=pltpu.HBM)
    sem = pl.BlockSpec(memory_space=pltpu.SEMAPHORE)
    res = pl.pallas_call(
        body, name=name + "_out",
        out_shape=(pltpu.SemaphoreType.DMA((N_DEV,)), pltpu.SemaphoreType.DMA((N_DEV,)),
                   *[pltpu.HBM(z.shape, z.dtype) for z in zones], jax.ShapeDtypeStruct((SUBLANE, LANE), F32)),
        in_specs=[hbm] * nz, out_specs=(sem, sem, *[hbm] * nz, pl.BlockSpec(memory_space=pltpu.VMEM)),
        input_output_aliases={q: 2 + q for q in range(nz)},
        compiler_params=pltpu.CompilerParams(has_side_effects=pltpu.SideEffectType.DATAFLOW_SIDE_EFFECTING),
    )(*zones)
    return (res[0], res[1], 0, 'relay', [], list(res[2:2 + nz]), metas), res[-1][0:1, 0:1]


def _pad_rows(a, rows):
    return jnp.pad(a, ((0, rows - a.shape[0]), (0, 0)))


PACK_ROWS = 2 * SUBLANE


def _rows(a):
    flat = a.reshape(-1).astype(F32)
    pad = -flat.shape[0] % (PACK_ROWS * LANE)
    return (jnp.pad(flat, (0, pad)) if pad else flat).reshape(-1, LANE)


def _pack_rows(arrays):
    return jnp.concatenate([_rows(a) for a in arrays], 0)


def _unpack_rows(t, shapes):
    out, off = [], 0
    for shp in shapes:
        size = math.prod(shp)
        rows = -(-size // (PACK_ROWS * LANE)) * PACK_ROWS
        out.append(t[off:off + rows].reshape(-1)[:size].reshape(shp))
        off += rows
    return out


def _stat_row(st, r):
    return st[r:r + 1, :]


def kernel(x, c, ada_w, ada_b, norm1_g, norm2_g, ff_w1, ff_w2, final_g, conv_w_in, conv_w, conv_b, conv_w_out, ssm_w_in, ssm_a_re, ssm_a_im, ssm_log_dt, ssm_b_re, ssm_b_im, ssm_c_re, ssm_c_im, ssm_d, ssm_glu_w, ssm_glu_b, ssm_w_out, sg_w_in, sg_v_g, sg_w_s, sg_b_s, sg_w_out, loss_target, m_ada_w, m_ada_b, m_norm1_g, m_norm2_g, m_ff_w1, m_ff_w2, m_final_g, m_conv_w_in, m_conv_w, m_conv_b, m_conv_w_out, m_ssm_w_in, m_ssm_a_re, m_ssm_a_im, m_ssm_log_dt, m_ssm_b_re, m_ssm_b_im, m_ssm_c_re, m_ssm_c_im, m_ssm_d, m_ssm_glu_w, m_ssm_glu_b, m_ssm_w_out, m_sg_w_in, m_sg_v_g, m_sg_w_s, m_sg_b_s, m_sg_w_out, v_ada_w, v_ada_b, v_norm1_g, v_norm2_g, v_ff_w1, v_ff_w2, v_final_g, v_conv_w_in, v_conv_w, v_conv_b, v_conv_w_out, v_ssm_w_in, v_ssm_a_re, v_ssm_a_im, v_ssm_log_dt, v_ssm_b_re, v_ssm_b_im, v_ssm_c_re, v_ssm_c_im, v_ssm_d, v_ssm_glu_w, v_ssm_glu_b, v_ssm_w_out, v_sg_w_in, v_sg_v_g, v_sg_w_s, v_sg_b_s, v_sg_w_out):
    P = dict(zip(INPUTS, (x, c, ada_w, ada_b, norm1_g, norm2_g, ff_w1, ff_w2, final_g, conv_w_in, conv_w, conv_b, conv_w_out, ssm_w_in, ssm_a_re, ssm_a_im, ssm_log_dt, ssm_b_re, ssm_b_im, ssm_c_re, ssm_c_im, ssm_d, ssm_glu_w, ssm_glu_b, ssm_w_out, sg_w_in, sg_v_g, sg_w_s, sg_b_s, sg_w_out, loss_target, m_ada_w, m_ada_b, m_norm1_g, m_norm2_g, m_ff_w1, m_ff_w2, m_final_g, m_conv_w_in, m_conv_w, m_conv_b, m_conv_w_out, m_ssm_w_in, m_ssm_a_re, m_ssm_a_im, m_ssm_log_dt, m_ssm_b_re, m_ssm_b_im, m_ssm_c_re, m_ssm_c_im, m_ssm_d, m_ssm_glu_w, m_ssm_glu_b, m_ssm_w_out, m_sg_w_in, m_sg_v_g, m_sg_w_s, m_sg_b_s, m_sg_w_out, v_ada_w, v_ada_b, v_norm1_g, v_norm2_g, v_ff_w1, v_ff_w2, v_final_g, v_conv_w_in, v_conv_w, v_conv_b, v_conv_w_out, v_ssm_w_in, v_ssm_a_re, v_ssm_a_im, v_ssm_log_dt, v_ssm_b_re, v_ssm_b_im, v_ssm_c_re, v_ssm_c_im, v_ssm_d, v_ssm_glu_w, v_ssm_glu_b, v_ssm_w_out, v_sg_w_in, v_sg_v_g, v_sg_w_s, v_sg_b_s, v_sg_w_out)))
    L, D = x.shape[1], x.shape[2]
    me = _my_index()
    xs = x[0]
    tgt = loss_target[0]
    n_conv = conv_w_in.shape[0]

    def gather_item(shard, axis):
        full = tuple(N_DEV * s if a == axis else s for a, s in enumerate(shard.shape))
        return shard, full, ('gather', axis, shard.shape[axis])

    def mixer_shards(i):
        kind, j = i % 3, i // 3
        if kind == 0:
            return [(conv_w_in[j], 1), (conv_w_out[j], 0)]
        if kind == 1:
            return [(ssm_w_in[j], 0), (ssm_glu_w[j], 0), (ssm_w_out[j], 0)]
        return [(sg_w_in[j], 1), (sg_w_out[j], 0)]

    gather_groups = [[gather_item(w.astype(BF16), ax) for w, ax in shards]
                     for i in range(DEPTH) for shards in (mixer_shards(i), [(ff_w1[i], 1), (ff_w2[i], 0)])]
    first_gather, first_token = _exchange_start(gather_groups[:1], c, name="gather_start_first", relayed=True)

    c_act = c * (1.0 / (1.0 + jnp.exp(-c))) + first_token[0:1, 0:1]
    vec_rows = jnp.concatenate([c_act.reshape(D // LANE, LANE), conv_w.reshape(-1, LANE), conv_b.reshape(-1, LANE),
                                sg_v_g.reshape(-1, LANE)], 0)
    n_vec = vec_rows.shape[0]
    vec_all = _all_gather(_pad_rows(vec_rows, 24)[None], 0, name="gather_vectors")
    c_all = vec_all[:, :D // LANE, :].reshape(N_DEV, D)
    sharded_full = vec_all[:, D // LANE:n_vec, :].transpose(1, 0, 2).reshape(n_vec - D // LANE, D)
    conv_w_full = sharded_full[:3 * n_conv].reshape(n_conv, 3, D)
    conv_b_full = sharded_full[3 * n_conv:4 * n_conv]
    sg_vg_full = sharded_full[4 * n_conv:4 * n_conv + 1]

    c_pad = _pad_rows(c_all, LANE)
    ncol = ada_w.shape[2]
    mod_part = jnp.stack([_mm(c_pad, ada_w[i], name=f"ada_fwd{i}")[:N_DEV] for i in range(DEPTH)])
    mod_all = _all_gather(mod_part.reshape(1, DEPTH * N_DEV, ncol), 0, name="gather_mod")
    mod_all = mod_all.reshape(N_DEV, DEPTH, N_DEV, ncol)
    mod_me = lax.dynamic_index_in_dim(mod_all, me, 2, keepdims=False)
    mod = mod_me.transpose(1, 0, 2).reshape(DEPTH, N_DEV * ncol) + ada_b
    gathers, gather_token = _exchange_start(gather_groups[1:], mod, name="gather_start", relayed=True)
    gathers = first_gather + gathers
    mod = mod + gather_token[0:1, 0:1]
    relayed = [None] * len(gathers)
    relayed[0], sent = _exchange_relay(gathers[0], mod, name="gather_mix_relay0")

    s5_args = (ssm_a_re[0], ssm_a_im[0], ssm_log_dt[0], ssm_b_re[0], ssm_b_im[0], ssm_c_re[0], ssm_c_im[0])
    (abar_re, abar_im, bblk_re, bblk_im, cblk_re, cblk_im), s5_vjp = jax.vjp(_s5_prep, *s5_args)
    pw_fwd, pw_rev, pos_fwd = _s5_power_tables(abar_re, abar_im, S5_CHUNK // SUBLANE)
    s5_w = tuple(t.astype(BF16) for t in (bblk_re, bblk_im, cblk_re, cblk_im))
    causal = jnp.tril(jnp.ones((SG_CHUNK, SG_CHUNK), dtype=bool))
    ws_m = jnp.where(causal[None], sg_w_s[0], 0.0)
    ws_b = ws_m.astype(BF16)
    wst_b = ws_m.transpose(0, 2, 1).astype(BF16)
    bsb = jnp.broadcast_to(sg_b_s[0][:, :, None], (SG_HEADS, SG_CHUNK, LANE))

    saved = []
    xa = xs
    mods = [[mod[i:i + 1, q * D:(q + 1) * D] for q in range(6)] for i in range(DEPTH)]
    wn1s = [norm1_g[i:i + 1] * (1.0 + mods[i][1]) for i in range(DEPTH)]
    h1 = _normmod_fwd(xa, wn1s[0], mods[0][0] + sent, name="norm1_fwd0")
    for i in range(DEPTH):
        kind, j = i % 3, i // 3
        sh1, sc1, g1, sh2, sc2, g2 = mods[i]
        wn1 = wn1s[i]
        wn2 = norm2_g[i:i + 1] * (1.0 + sc2)
        S = dict(x_in=xa, g1=g1, g2=g2, sc1=sc1, sc2=sc2, wn1=wn1, wn2=wn2)
        w_mix = _exchange_wait(relayed[2 * i], h1, name=f"gather_mix_wait{i}")
        S['h1'] = h1
        if kind == 0:
            bcx = _mm(h1, w_mix[0], name=f"conv_in{i}", out_dtypes=(BF16,), bm=2048)
            wb = _pad_rows(jnp.concatenate([conv_w_full[j], conv_b_full[j:j + 1]], 0), SUBLANE)
            pb = _conv_fwd(bcx, wb, name=f"conv_mix{i}")
            S.update(bcx=bcx, wb=wb, pb=pb)
        elif kind == 1:
            u = _mm(h1, w_mix[0], name=f"ssm_in{i}")
            sre, sim, ypre, yg = _s5_fwd(u, *s5_w, pw_fwd, pos_fwd, ssm_d, name=f"s5_scan{i}")

            def glu_epi(acc, yv, bias):
                t = acc + bias
                return yv * (1.0 / (1.0 + jnp.exp(-t))), t

            pb, tt = _mm(yg, w_mix[1], name=f"ssm_glu{i}", out_dtypes=(BF16, F32), epi=glu_epi,
                         extras=[(yg, 'mn'), (ssm_glu_b, 'n')])
            S.update(u=u, sre=sre, sim=sim, ypre=ypre, yg=yg, pb=pb, tt=tt)
        else:
            uv = _mm(h1, w_mix[0], name=f"sg_in{i}", bm=2048)
            pb = _sg_fwd(uv, sg_vg_full, ws_b, bsb, name=f"sg_mix{i}")
            S.update(uv=uv, pb=pb)
        relayed[2 * i + 1], sent = _exchange_relay(gathers[2 * i + 1], pb, name=f"gather_ff_relay{i}")
        x_mid, y_mix, h2 = _mm(pb, w_mix[-1], name=f"mix_out{i}", out_dtypes=(F32, BF16, BF16), epi=_epi_residual_norm,
                               extras=[(xa, 'mn'), (g1 + sent, 'n'), (wn2, 'n'), (sh2, 'n')])
        w1_full, w2_full = _exchange_wait(relayed[2 * i + 1], h2, name=f"gather_ff_wait{i}")
        S.update(w_mix=w_mix, w1=w1_full, w2=w2_full)
        ra = _mm(h2, w1_full, name=f"ff_up{i}", out_dtypes=(BF16,), epi=lambda acc: (jnp.maximum(acc, 0.0),), bm=2048)
        if i + 1 < DEPTH:
            relayed[2 * i + 2], sent = _exchange_relay(gathers[2 * i + 2], ra, name=f"gather_mix_relay{i + 1}")
            xa, f_out, h1 = _mm(ra, w2_full, name=f"ff_down{i}", out_dtypes=(F32, BF16, BF16), a_fn=_square,
                                epi=_epi_residual_norm, bm=256, bk=w2_full.shape[0],
                                extras=[(x_mid, 'mn'), (g2 + sent, 'n'), (wn1s[i + 1], 'n'), (mods[i + 1][0], 'n')])
        else:
            f_out = None
            dx, dfb, st = _mm(ra, w2_full, name=f"ff_down{i}", out_dtypes=(F32, BF16), epi=_epi_loss_head, a_fn=_square,
                              n_stats=3, bm=256, bk=w2_full.shape[0],
                              extras=[(x_mid, 'mn'), (g2, 'n'), (tgt, 'mn'), (final_g[None], 'n')])
        S.update(x_mid=x_mid, y_mix=y_mix, h2=h2, ra=ra, f_out=f_out)
        saved.append(S)

    loss_tile = st[:, :LANE]
    d_final_g = _stat_row(st, 1)
    dg2_next = _stat_row(st, 2)

    def scatter_item(g, axis):
        m = g.shape[axis] // N_DEV
        shard = tuple(m if a == axis else s for a, s in enumerate(g.shape))
        return g, (N_DEV,) + shard, ('scatter', axis, m)

    dmod = [None] * DEPTH
    dn1g, dn2g = [None] * DEPTH, [None] * DEPTH
    d_conv_w, d_conv_b = [None] * n_conv, [None] * n_conv
    ff_sent, mix_sent = [None] * DEPTH, [None] * DEPTH
    small = {}
    for i in reversed(range(DEPTH)):
        kind, j = i % 3, i // 3
        S = saved[i]
        w_mix = S['w_mix']
        dg2 = dg2_next
        da = _mm(dfb, S['w2'], tb=True, name=f"ff_down_bwd{i}", out_dtypes=(BF16,), bm=2048,
                 epi=lambda acc, rav: (acc * (2.0 * rav.astype(F32)),), extras=[(S['ra'], 'mn')])
        dw2 = _wgrad(S['ra'], dfb, name=f"ff_w2_grad{i}", a_fn=_square, bm=512, bn=1024)
        dw1 = _wgrad(S['h2'], da, name=f"ff_w1_grad{i}")
        (ff_sent[i],), token = _exchange_start([[scatter_item(dw1, 1), scatter_item(dw2, 0)]], dx,
                                               name=f"ff_grads_start{i}")
        dx_mid, dyb, st2 = _mm(da, S['w1'], tb=True, name=f"ff_up_bwd{i}", out_dtypes=(F32, BF16), bm=512,
                               bk=da.shape[1], epi=_epi_norm_bwd(True), n_stats=3,
                               extras=[(S['x_mid'], 'mn'), (S['wn2'] + token[0:1, 0:1], 'n'), (dx, 'mn'),
                                       (S['y_mix'], 'mn'), (S['g1'], 'n')])
        dsc2 = _stat_row(st2, 0) * norm2_g[i:i + 1]
        dn2g[i] = _stat_row(st2, 0) * (1.0 + S['sc2'])
        dsh2 = _stat_row(st2, 1)
        dg1 = _stat_row(st2, 2)
        if kind == 0:
            dp = _mm(dyb, w_mix[1], tb=True, name=f"conv_out_bwd{i}", out_dtypes=(BF16,))
            d_cwo = _wgrad(S['pb'], dyb, name=f"conv_w_out_grad{i}")
            dbcx, stc = _conv_bwd(dp, S['bcx'], S['wb'], name=f"conv_mix_bwd{i}")
            d_conv_w[j] = stc[0:3]
            d_conv_b[j] = stc[3:4]
            dh_operand, dh_name = dbcx, "conv_in_bwd"
            d_cwi = _wgrad(S['h1'], dbcx, name=f"conv_w_in_grad{i}")
            mix_grads = [scatter_item(d_cwi, 1), scatter_item(d_cwo, 0)]
        elif kind == 1:
            dtb, dya, stg = _mm(dyb, w_mix[2], tb=True, name=f"ssm_out_bwd{i}", out_dtypes=(BF16, F32), bm=512,
                                epi=_epi_glu_bwd, n_stats=1, extras=[(S['yg'], 'mn'), (S['tt'], 'mn')])
            d_ssm_out = _wgrad(S['pb'], dyb, name=f"ssm_w_out_grad{i}")
            dypre = _mm(dtb, w_mix[1], tb=True, name=f"ssm_glu_in_bwd{i}",
                        epi=lambda acc, a, yp: ((a + acc) * _gelu_grad(yp),),
                        extras=[(dya, 'mn'), (S['ypre'], 'mn')])
            d_glu = _wgrad(S['yg'], dtb, name=f"ssm_glu_w_grad{i}", bm=512)
            dub, dbre, dbim, dcre, dcim, ga, dd = _s5_bwd(dypre, S['u'], S['sre'], S['sim'], *s5_w, pw_rev, pos_fwd, ssm_d,
                                                           name=f"s5_scan_bwd{i}")
            dh_operand, dh_name = dub, "ssm_in_bwd"
            d_ssm_in = _wgrad(S['h1'], dub, name=f"ssm_w_in_grad{i}")
            da_re, da_im, dlog_dt, db_re, db_im, dc_re, dc_im = s5_vjp((ga[0:1], ga[1:2], dbre, dbim, dcre, dcim))
            s5_small = _pack_rows([da_re, da_im, dlog_dt, db_re, db_im, dc_re, dc_im, dd[0], stg[0]])
            mix_grads = [scatter_item(d_ssm_in, 0), scatter_item(d_glu, 0), scatter_item(d_ssm_out, 0),
                         gather_item(s5_small.astype(BF16), 0)]
        else:
            dp = _mm(dyb, w_mix[1], tb=True, name=f"sg_out_bwd{i}")
            d_sgo = _wgrad(S['pb'], dyb, name=f"sg_w_out_grad{i}")
            duv, dws, dbs, stv = _sg_bwd(dp, S['uv'], sg_vg_full, ws_b, wst_b, bsb, name=f"sg_mix_bwd{i}")
            dh_operand, dh_name = duv, "sg_in_bwd"
            d_sgi = _wgrad(S['h1'], duv, name=f"sg_w_in_grad{i}")
            sg_small = _pack_rows([jnp.where(causal[None], dws, 0.0), jnp.sum(dbs, axis=-1)])
            d_sg_vg = stv[0:1]
            mix_grads = [scatter_item(d_sgi, 1), scatter_item(d_sgo, 0), gather_item(sg_small.astype(BF16), 0)]
        wn1 = S['wn1']
        gate = []
        if i > 0:
            (mix_sent[i],), token = _exchange_start([mix_grads], dx_mid, name=f"mix_grads_start{i}")
            wn1 = wn1 + token[0:1, 0:1]
            gate = [(saved[i - 1]['f_out'], 'mn'), (saved[i - 1]['g2'], 'n')]
        res = _mm(dh_operand, w_mix[0], tb=True, name=f"{dh_name}{i}", out_dtypes=(F32, BF16) if i > 0 else (F32,),
                  bm=512, bk=w_mix[0].shape[1], epi=_epi_norm_bwd(i > 0), n_stats=3 if i > 0 else 2,
                  extras=[(S['x_in'], 'mn'), (wn1, 'n'), (dx_mid, 'mn')] + gate)
        if i > 0:
            dx, dfb, st1 = res
            dg2_next = _stat_row(st1, 2)
        else:
            dx, st1 = res
        dsc1 = _stat_row(st1, 0) * norm1_g[i:i + 1]
        dn1g[i] = _stat_row(st1, 0) * (1.0 + S['sc1'])
        dsh1 = _stat_row(st1, 1)
        dmod[i] = jnp.concatenate([dsh1, dsc1, dg1, dsh2, dsc2, dg2], 1)
    grad_x = dx[None]

    out = {}

    def small_group(names, parts, label):
        shapes = [P[n].shape for n in names]
        w, m, v = (_pack_rows([P[pre + n] for n in names])[None] for pre in ('', 'm_', 'v_'))
        res = [_unpack_rows(t[0], shapes) for t in _adamw(w, [parts], m, v, name=label)]
        for q, n in enumerate(names):
            out[n] = tuple(r[q] for r in res)

    small.update(ada_b=jnp.concatenate(dmod, 0), norm1_g=jnp.concatenate(dn1g, 0), norm2_g=jnp.concatenate(dn2g, 0),
                 final_g=d_final_g, conv_w=jnp.stack(d_conv_w), conv_b=jnp.concatenate(d_conv_b, 0), sg_v_g=d_sg_vg)
    last_pack = _pack_rows([small[n] for n in LAST_SMALL + SMALL_SHARD])
    n_last = _pack_rows([P[n] for n in LAST_SMALL]).shape[0]
    n_pack = last_pack.shape[0]
    pack_all = _all_gather(jnp.concatenate([last_pack, loss_tile], 0)[None], 0, name="gather_small_grads")
    loss = jnp.sum(pack_all[:, n_pack, 0])
    (mix_sent[0],), last_token = _exchange_start([mix_grads], pack_all, name="mix_grads_start0")
    small_group(LAST_SMALL, pack_all[:, :n_last], "adamw_small")
    sh_rows = (n_pack - n_last) // N_DEV
    sh_parts = pack_all[:, n_last:n_pack].reshape(N_DEV, sh_rows, N_DEV, LANE)
    sh_parts = lax.dynamic_index_in_dim(sh_parts, me, 2, keepdims=False)
    sh_parts = jnp.pad(sh_parts, ((0, 0), (0, 16 - sh_rows), (0, 0)))

    def pack_shard(prefix):
        return _pad_rows(jnp.concatenate([P[prefix + n].reshape(-1, LANE) for n in SMALL_SHARD], 0), 16)[None]

    sg_, sd_, sm_, sv_ = _adamw(pack_shard(''), [sh_parts], pack_shard('m_'), pack_shard('v_'), name="adamw_channel")
    off = 0
    for n in SMALL_SHARD:
        rows = math.prod(P[n].shape) // LANE
        out[n] = tuple(t[0, off:off + rows].reshape(P[n].shape) for t in (sg_, sd_, sm_, sv_))
        off += rows

    dmod_all = pack_all[:, :DEPTH * 6 * D // LANE].reshape(N_DEV, DEPTH, 6 * D)
    dmod_cols = lax.dynamic_slice_in_dim(dmod_all, me * ncol, ncol, 2)
    g_ada = [_mm(c_pad, _pad_rows(dmod_cols[:, i], LANE), ta=True, name=f"ada_w_grad{i}")[None] for i in range(DEPTH)]

    def big(name, parts):
        res = _adamw(P[name], parts, P['m_' + name], P['v_' + name], name="adamw_" + name)
        out[name] = res
        return res[1]

    ff_parts = [_exchange_wait(ff_sent[i], last_token, name=f"ff_grads_wait{i}") for i in range(DEPTH)]
    mix_parts = [None] + [_exchange_wait(mix_sent[i], last_token, name=f"mix_grads_wait{i}") for i in range(1, DEPTH)]
    big('ada_w', g_ada)
    big('ff_w1', [p[0] for p in ff_parts])
    big('ff_w2', [p[1] for p in ff_parts])
    done = big('sg_w_in', [mix_parts[2][0]])
    mix_parts[0] = _exchange_wait(mix_sent[0], done, name="mix_grads_wait0")
    big('conv_w_in', [mix_parts[i][0] for i in range(DEPTH) if i % 3 == 0])
    row_names = ['conv_w_out', 'ssm_w_in', 'ssm_glu_w', 'ssm_w_out', 'sg_w_out']
    row_parts = ([mix_parts[i][1] for i in range(DEPTH) if i % 3 == 0] + mix_parts[1][:3] + [mix_parts[2][1]])
    small_group(S5_SMALL, mix_parts[1][3].reshape(N_DEV, -1, LANE), "adamw_s5")
    small_group(SG_SMALL, mix_parts[2][2].reshape(N_DEV, -1, LANE), "adamw_sg")
    row_w, row_m, row_v = (jnp.concatenate([P[pre + n] for n in row_names], 0) for pre in ('', 'm_', 'v_'))
    rw = _adamw(row_w, row_parts, row_m, row_v, name="adamw_row_sharded")
    off = 0
    for n in row_names:
        cnt = P[n].shape[0]
        out[n] = tuple(t[off:off + cnt] for t in rw)
        off += cnt

    return (loss, grad_x, *[out[n][0] for n in WEIGHTS], *[out[n][1] for n in WEIGHTS],
            *[out[n][2] for n in WEIGHTS], *[out[n][3] for n in WEIGHTS])
```

```python
import math

import jax
import jax.numpy as jnp
from jax import lax
from jax.experimental import pallas as pl
from jax.experimental.pallas import tpu as pltpu

F32 = jnp.float32
BF16 = jnp.bfloat16

N_DEV = 8
MESH_ID = pl.DeviceIdType.MESH
DEPTH = 4
EPS = 1e-6
S5_GROUPS, S5_GROUP, S5_STATE = 64, 16, 64
S5_LANES = S5_GROUPS * S5_STATE
S5_BLOCKS = 8
S5_CHUNK = 512
SG_HEADS, SG_CHUNK = 8, 128
LANE = 128
SUBLANE = 8
VMEM_LIMIT = 48 * 1024 * 1024
ADAM_LR, ADAM_B1, ADAM_B2, ADAM_EPS, ADAM_WD, ADAM_STEP = 0.001, 0.9, 0.999, 1e-08, 0.01, 10
GELU_C = math.sqrt(2.0 / math.pi)
GELU_A = 0.044715

WEIGHTS = ['ada_w', 'ada_b', 'norm1_g', 'norm2_g', 'ff_w1', 'ff_w2', 'final_g', 'conv_w_in', 'conv_w', 'conv_b',
           'conv_w_out', 'ssm_w_in', 'ssm_a_re', 'ssm_a_im', 'ssm_log_dt', 'ssm_b_re', 'ssm_b_im', 'ssm_c_re',
           'ssm_c_im', 'ssm_d', 'ssm_glu_w', 'ssm_glu_b', 'ssm_w_out', 'sg_w_in', 'sg_v_g', 'sg_w_s', 'sg_b_s',
           'sg_w_out']
INPUTS = ['x', 'c'] + WEIGHTS + ['loss_target'] + ['m_' + n for n in WEIGHTS] + ['v_' + n for n in WEIGHTS]
S5_SMALL = ['ssm_a_re', 'ssm_a_im', 'ssm_log_dt', 'ssm_b_re', 'ssm_b_im', 'ssm_c_re', 'ssm_c_im', 'ssm_d', 'ssm_glu_b']
SG_SMALL = ['sg_w_s', 'sg_b_s']
LAST_SMALL = ['ada_b', 'norm1_g', 'norm2_g', 'final_g']
SMALL_SHARD = ['conv_w', 'conv_b', 'sg_v_g']


def _params(*sem):
    return pltpu.CompilerParams(dimension_semantics=sem or None, vmem_limit_bytes=VMEM_LIMIT)


def _my_pos():
    return lax.axis_index("x"), lax.axis_index("y"), lax.axis_index("c")


def _my_index():
    x, y, c = _my_pos()
    return 4 * x + 2 * y + c


def _mm(a, b, *, name, ta=False, tb=False, out_dtypes=(F32,), epi=None, extras=(), a_fn=None, n_stats=0, bm=1024,
        bn=1024, bk=1024):
    a_chunks = a.shape[0] if a.ndim == 3 else 0
    b_chunks = b.shape[0] if b.ndim == 3 else 0
    assert not (a_chunks and ta) and not (b_chunks and tb)
    if a_chunks:
        m, k = a.shape[1], a_chunks * a.shape[2]
        bk = k
    else:
        m, k = (a.shape[1], a.shape[0]) if ta else a.shape
    if b_chunks:
        k2, n = b.shape[1], b_chunks * b.shape[2]
        bn = min(bn, b.shape[2])
    else:
        k2, n = (b.shape[1], b.shape[0]) if tb else b.shape
    assert k == k2, (a.shape, b.shape, ta, tb)
    bm, bn, bk = min(bm, m), min(bn, n), min(bk, k)
    assert m % bm == 0 and n % bn == 0 and k % bk == 0, (m, n, k, bm, bn, bk)
    nk = k // bk
    assert nk == 1 or n_stats == 0
    n_ex, n_out = len(extras), len(out_dtypes)
    dims = (((0 if ta else 1,), (1 if tb else 0,)), ((), ()))

    def body(*refs):
        a_ref, b_ref = refs[0], refs[1]
        ex_refs = refs[2:2 + n_ex]
        out_refs = refs[2 + n_ex:2 + n_ex + n_out]

        def finish(acc):
            outs = epi(acc, *[r[...] for r in ex_refs]) if epi is not None else (acc,)
            for r, o in zip(out_refs, outs[:n_out]):
                r[...] = o.astype(r.dtype)
            if n_stats:
                st_ref = refs[2 + n_ex + n_out]

                @pl.when(pl.program_id(0) == 0)
                def _():
                    st_ref[...] = jnp.zeros_like(st_ref)

                for q, row in enumerate(outs[n_out:]):
                    st_ref[q:q + 1, :] += row

        av = jnp.concatenate([a_ref[t] for t in range(a_chunks)], axis=1) if a_chunks else a_ref[...]
        if a_fn is not None:
            av = a_fn(av)
        part = lax.dot_general(av.astype(BF16), b_ref[...].astype(BF16), dims, preferred_element_type=F32)
        if nk == 1:
            finish(part)
            return
        acc_ref = refs[-1]
        kk = pl.program_id(2)

        @pl.when(kk == 0)
        def _():
            acc_ref[...] = part

        @pl.when(kk > 0)
        def _():
            acc_ref[...] += part

        @pl.when(kk == nk - 1)
        def _():
            finish(acc_ref[...])

    if a_chunks:
        a_spec = pl.BlockSpec((a_chunks, bm, a.shape[2]), lambda i, j, q: (0, i, 0))
    elif ta:
        a_spec = pl.BlockSpec((bk, bm), lambda i, j, q: (q, i))
    else:
        a_spec = pl.BlockSpec((bm, bk), lambda i, j, q: (i, q))
    if b_chunks:
        per = b.shape[2] // bn
        b_spec = pl.BlockSpec((None, bk, bn), lambda i, j, q: (j // per, q, j % per))
    else:
        still = dict(pipeline_mode=pl.Buffered(1)) if nk == 1 and n == bn else {}
        b_spec = (pl.BlockSpec((bn, bk), lambda i, j, q: (j, q), **still) if tb
                  else pl.BlockSpec((bk, bn), lambda i, j, q: (q, j), **still))
    ex_specs = []
    for arr, kind in extras:
        if kind == 'mn':
            assert arr.shape == (m, n), (arr.shape, m, n)
            ex_specs.append(pl.BlockSpec((bm, bn), lambda i, j, q: (i, j)))
        else:
            assert arr.shape == (1, n), (arr.shape, n)
            ex_specs.append(pl.BlockSpec((1, bn), lambda i, j, q: (0, j)))
    out_shape = [jax.ShapeDtypeStruct((m, n), d) for d in out_dtypes]
    out_specs = [pl.BlockSpec((bm, bn), lambda i, j, q: (i, j)) for _ in out_dtypes]
    if n_stats:
        assert n_stats <= SUBLANE
        out_shape.append(jax.ShapeDtypeStruct((SUBLANE, n), F32))
        out_specs.append(pl.BlockSpec((SUBLANE, bn), lambda i, j, q: (0, j)))
    outs = pl.pallas_call(
        body, name=name, out_shape=tuple(out_shape), grid=(m // bm, n // bn, nk),
        in_specs=[a_spec, b_spec] + ex_specs, out_specs=tuple(out_specs),
        scratch_shapes=[pltpu.VMEM((bm, bn), F32)] if nk > 1 else [],
        compiler_params=_params(*(["arbitrary"] * 3 if n_stats else ["parallel", "parallel", "arbitrary"])),
    )(a, b, *[arr for arr, _ in extras])
    return outs if len(outs) > 1 else outs[0]


def _epi_residual_norm(acc, res, gate, w, sh):
    xn = res + gate * acc
    return xn, acc, xn * _rstd(xn) * w + sh


def _epi_norm_bwd(gated):
    def epi(dh, xv, w, dres, *gate):
        rstd = _rstd(xv)
        xn = xv * rstd
        dxn = dh * w
        dx = rstd * (dxn - xn * jnp.mean(dxn * xn, axis=-1, keepdims=True)) + dres
        stats = [jnp.sum(dh * xn, axis=0, keepdims=True), jnp.sum(dh, axis=0, keepdims=True)]
        if not gated:
            return (dx, *stats)
        yv, g = gate
        return (dx, dx * g, *stats, jnp.sum(dx * yv.astype(F32), axis=0, keepdims=True))
    return epi


def _epi_loss_head(f, x_mid, g, tgt, fg):
    xv = x_mid + g * f
    rstd = _rstd(xv)
    xn = xv * rstd
    err = xn * fg - tgt
    loss = 0.5 * jnp.sum(jnp.mean(err * err, axis=-1, keepdims=True))
    dout = err * (1.0 / xv.shape[-1])
    dxn = dout * fg
    dx = rstd * (dxn - xn * jnp.mean(dxn * xn, axis=-1, keepdims=True))
    return (dx, dx * g, jnp.full((1, xv.shape[-1]), loss, F32), jnp.sum(dout * xn, axis=0, keepdims=True),
            jnp.sum(dx * f, axis=0, keepdims=True))


def _epi_glu_bwd(dy2, yv, t):
    sig = 1.0 / (1.0 + jnp.exp(-t))
    dt = dy2 * yv * sig * (1.0 - sig)
    return dt, dy2 * sig, jnp.sum(dt, axis=0, keepdims=True)


def _wgrad(acts, cots, *, name, a_fn=None, bm=1024, bn=512):
    return _mm(acts, cots, ta=True, name=name, out_dtypes=(BF16,), a_fn=a_fn, bm=bm, bn=bn, bk=acts.shape[0])


def _square(a):
    af = a.astype(F32)
    return af * af


def _rstd(xv):
    return lax.rsqrt(jnp.mean(xv * xv, axis=-1, keepdims=True) + EPS)


def _normmod_fwd(x, w, sh, *, name, tm=512):
    L, D = x.shape

    def body(x_ref, w_ref, s_ref, h_ref):
        xv = x_ref[...]
        h_ref[...] = (xv * _rstd(xv) * w_ref[...] + s_ref[...]).astype(h_ref.dtype)

    row = pl.BlockSpec((tm, D), lambda i: (i, 0))
    vec = pl.BlockSpec((1, D), lambda i: (0, 0))
    return pl.pallas_call(body, name=name, out_shape=jax.ShapeDtypeStruct((L, D), BF16), grid=(L // tm,),
                          in_specs=[row, vec, vec], out_specs=row, compiler_params=_params("parallel"))(x, w, sh)


def _shift_down(v, k):
    row = lax.broadcasted_iota(jnp.int32, v.shape, 0)
    return jnp.where(row >= k, pltpu.roll(v, k, 0), 0.0)


def _shift_up(v, k):
    n = v.shape[0]
    row = lax.broadcasted_iota(jnp.int32, v.shape, 0)
    return jnp.where(row < n - k, pltpu.roll(v, n - k, 0), 0.0)


def _conv_views(L, D):
    return [pl.BlockSpec((L, LANE), lambda j, s=s: (0, s * (D // LANE) + j)) for s in range(3)]


def _conv_fwd(bcx, wb, *, name):
    L, D = bcx.shape[0], bcx.shape[1] // 3

    def body(b_ref, c_ref, x_ref, wb_ref, p_ref):
        z = c_ref[...].astype(F32) * x_ref[...].astype(F32)
        conv = (wb_ref[0:1, :] * _shift_down(z, 2) + wb_ref[1:2, :] * _shift_down(z, 1)
                + wb_ref[2:3, :] * z + wb_ref[3:4, :])
        p_ref[...] = (b_ref[...].astype(F32) * conv).astype(p_ref.dtype)

    col = pl.BlockSpec((L, LANE), lambda j: (0, j))
    return pl.pallas_call(body, name=name, out_shape=jax.ShapeDtypeStruct((L, D), BF16), grid=(D // LANE,),
                          in_specs=_conv_views(L, D) + [pl.BlockSpec((SUBLANE, LANE), lambda j: (0, j))],
                          out_specs=col, compiler_params=_params("parallel"))(bcx, bcx, bcx, wb)


def _conv_bwd(dp, bcx, wb, *, name):
    L, D = dp.shape

    def body(dp_ref, b_ref, c_ref, x_ref, wb_ref, d3_ref, st_ref):
        cv, xv = c_ref[...].astype(F32), x_ref[...].astype(F32)
        z = cv * xv
        z1, z2 = _shift_down(z, 1), _shift_down(z, 2)
        w0, w1, w2 = wb_ref[0:1, :], wb_ref[1:2, :], wb_ref[2:3, :]
        conv = w0 * z2 + w1 * z1 + w2 * z + wb_ref[3:4, :]
        dpv = dp_ref[...].astype(F32)
        d3_ref[0] = (dpv * conv).astype(d3_ref.dtype)
        dconv = dpv * b_ref[...].astype(F32)
        dz = w2 * dconv + w1 * _shift_up(dconv, 1) + w0 * _shift_up(dconv, 2)
        d3_ref[1] = (dz * xv).astype(d3_ref.dtype)
        d3_ref[2] = (dz * cv).astype(d3_ref.dtype)
        st_ref[...] = jnp.zeros_like(st_ref)
        st_ref[0:1, :] = jnp.sum(dconv * z2, axis=0, keepdims=True)
        st_ref[1:2, :] = jnp.sum(dconv * z1, axis=0, keepdims=True)
        st_ref[2:3, :] = jnp.sum(dconv * z, axis=0, keepdims=True)
        st_ref[3:4, :] = jnp.sum(dconv, axis=0, keepdims=True)

    col = pl.BlockSpec((L, LANE), lambda j: (0, j))
    vec = pl.BlockSpec((SUBLANE, LANE), lambda j: (0, j))
    return pl.pallas_call(body, name=name,
                          out_shape=(jax.ShapeDtypeStruct((3, L, D), BF16), jax.ShapeDtypeStruct((SUBLANE, D), F32)),
                          grid=(D // LANE,), in_specs=[col] + _conv_views(L, D) + [vec],
                          out_specs=(pl.BlockSpec((3, L, LANE), lambda j: (0, 0, j)), vec),
                          compiler_params=_params("parallel"))(dp, bcx, bcx, bcx, wb)


def _sg_fwd(uv, vg, ws, bsb, *, name, tr=512):
    L, D = uv.shape[0], uv.shape[1] // 2

    def body(uv_ref, vg_ref, ws_ref, bsb_ref, p_ref):
        for ci in range(tr // SG_CHUNK):
            rows = slice(ci * SG_CHUNK, (ci + 1) * SG_CHUNK)
            v = uv_ref[rows, D:2 * D]
            vn = (v * _rstd(v) * vg_ref[...]).astype(BF16)
            for h in range(SG_HEADS):
                cols = slice(h * LANE, (h + 1) * LANE)
                vm = jnp.dot(ws_ref[h], vn[:, cols], preferred_element_type=F32) + bsb_ref[h]
                p_ref[rows, cols] = (uv_ref[rows, cols] * vm).astype(p_ref.dtype)

    full3 = pl.BlockSpec((SG_HEADS, SG_CHUNK, LANE), lambda i: (0, 0, 0))
    return pl.pallas_call(body, name=name, out_shape=jax.ShapeDtypeStruct((L, D), BF16), grid=(L // tr,),
                          in_specs=[pl.BlockSpec((tr, 2 * D), lambda i: (i, 0)), pl.BlockSpec((1, D), lambda i: (0, 0)),
                                    full3, full3],
                          out_specs=pl.BlockSpec((tr, D), lambda i: (i, 0)),
                          compiler_params=_params("parallel"))(uv, vg, ws, bsb)


def _sg_bwd(dp, uv, vg, ws, wst, bsb, *, name, tr=512):
    L, D = dp.shape

    def body(dp_ref, uv_ref, vg_ref, ws_ref, wst_ref, bsb_ref, duv_ref, dws_ref, dbs_ref, st_ref, dvn_ref):
        i = pl.program_id(0)

        @pl.when(i == 0)
        def _():
            dws_ref[...] = jnp.zeros_like(dws_ref)
            dbs_ref[...] = jnp.zeros_like(dbs_ref)
            st_ref[...] = jnp.zeros_like(st_ref)

        for ci in range(tr // SG_CHUNK):
            rows = slice(ci * SG_CHUNK, (ci + 1) * SG_CHUNK)
            v = uv_ref[rows, D:2 * D]
            rstd = _rstd(v)
            vhat = v * rstd
            vn = (vhat * vg_ref[...]).astype(BF16)
            for h in range(SG_HEADS):
                cols = slice(h * LANE, (h + 1) * LANE)
                vm = jnp.dot(ws_ref[h], vn[:, cols], preferred_element_type=F32) + bsb_ref[h]
                dph = dp_ref[rows, cols]
                duv_ref[rows, cols] = (dph * vm).astype(duv_ref.dtype)
                dvm = dph * uv_ref[rows, cols]
                dbs_ref[h] += dvm
                dvmb = dvm.astype(BF16)
                dws_ref[h] += lax.dot_general(dvmb, vn[:, cols], (((1,), (1,)), ((), ())),
                                              preferred_element_type=F32)
                dvn_ref[rows, cols] = jnp.dot(wst_ref[h], dvmb, preferred_element_type=F32)
            dvn = dvn_ref[rows, :]
            gv = dvn * vg_ref[...]
            dv = rstd * (gv - vhat * jnp.mean(gv * vhat, axis=-1, keepdims=True))
            duv_ref[rows, D:2 * D] = dv.astype(duv_ref.dtype)
            st_ref[0:1, :] += jnp.sum(dvn * vhat, axis=0, keepdims=True)

    full3 = pl.BlockSpec((SG_HEADS, SG_CHUNK, LANE), lambda i: (0, 0, 0))
    acc3 = jax.ShapeDtypeStruct((SG_HEADS, SG_CHUNK, LANE), F32)
    return pl.pallas_call(
        body, name=name,
        out_shape=(jax.ShapeDtypeStruct((L, 2 * D), BF16), acc3, acc3, jax.ShapeDtypeStruct((SUBLANE, D), F32)),
        grid=(L // tr,),
        in_specs=[pl.BlockSpec((tr, D), lambda i: (i, 0)), pl.BlockSpec((tr, 2 * D), lambda i: (i, 0)),
                  pl.BlockSpec((1, D), lambda i: (0, 0)), full3, full3, full3],
        out_specs=(pl.BlockSpec((tr, 2 * D), lambda i: (i, 0)), full3, full3,
                   pl.BlockSpec((SUBLANE, D), lambda i: (0, 0))),
        scratch_shapes=[pltpu.VMEM((tr, D), F32)],
        compiler_params=_params("arbitrary"))(dp, uv, vg, ws, wst, bsb)


def _gelu(x):
    return 0.5 * x * (1.0 + jnp.tanh(GELU_C * (x + GELU_A * x * x * x)))


def _gelu_grad(x):
    th = jnp.tanh(GELU_C * (x + GELU_A * x * x * x))
    return 0.5 * (1.0 + th) + 0.5 * x * (1.0 - th * th) * GELU_C * (1.0 + 3.0 * GELU_A * x * x)


def _cmul_add(xr, xi, ar, ai, br, bi):
    return xr + ar * br - ai * bi, xi + ar * bi + ai * br


def _cmul_conj_add(xr, xi, ar, ai, br, bi):
    return xr + ar * br + ai * bi, xi + ar * bi - ai * br


def _to_subchunk_order(src_ref, dst_ref, n):
    for k in range(n):
        dst_ref[pl.ds(SUBLANE * k, SUBLANE), :] = src_ref[pl.ds(k, SUBLANE, stride=n), :].astype(dst_ref.dtype)


def _to_time_order(src_ref, dst_ref, n):
    for m in range(n):
        r, k = divmod(SUBLANE * m, n)
        dst_ref[pl.ds(SUBLANE * m, SUBLANE), :] = src_ref[pl.ds(SUBLANE * k + r, SUBLANE, stride=SUBLANE), :]


def _s5_fwd(u, bre, bim, cre, cim, pw, pos, dsk, *, name, tc=S5_CHUNK):
    L, D = u.shape
    W = S5_LANES // S5_BLOCKS
    nt = L // tc
    n = tc // SUBLANE

    def sub(k):
        return pl.ds(SUBLANE * k, SUBLANE)

    def body(u_ref, bre_ref, bim_ref, cre_ref, cim_ref, pw_ref, pos_ref, d_ref, sre_ref, sim_ref, ypre_ref, yg_ref,
             carry, up, yp):
        t = pl.program_id(1)

        @pl.when(t == 0)
        def _():
            carry[...] = jnp.zeros_like(carry)

        _to_subchunk_order(u_ref, up, n)
        uv = up[...]
        ub = uv.astype(BF16)
        sre_ref[...] = jnp.dot(ub, bre_ref[...], preferred_element_type=F32)
        sim_ref[...] = jnp.dot(ub, bim_ref[...], preferred_element_type=F32)

        ar, ai = pw_ref[8], pw_ref[9]
        xr = jnp.zeros((SUBLANE, W), F32)
        xi = jnp.zeros((SUBLANE, W), F32)
        for k in range(n):
            xr, xi = _cmul_add(sre_ref[sub(k), :], sim_ref[sub(k), :], ar, ai, xr, xi)
            sre_ref[sub(k), :] = xr
            sim_ref[sub(k), :] = xi
        for q, d in enumerate((1, 2, 4)):
            xr, xi = _cmul_add(xr, xi, pw_ref[2 * q], pw_ref[2 * q + 1], pltpu.roll(xr, d, 0), pltpu.roll(xi, d, 0))
        cr, ci = carry[0], carry[1]
        xr, xi = _cmul_add(xr, xi, pw_ref[6], pw_ref[7], cr, ci)
        first = lax.broadcasted_iota(jnp.int32, (SUBLANE, W), 0) == 0
        er = jnp.where(first, cr, pltpu.roll(xr, 1, 0))
        ei = jnp.where(first, ci, pltpu.roll(xi, 1, 0))
        last = slice(SUBLANE - 1, SUBLANE)
        carry[0] = jnp.broadcast_to(xr[last, :], (SUBLANE, W))
        carry[1] = jnp.broadcast_to(xi[last, :], (SUBLANE, W))
        for k in range(n):
            sr, si = _cmul_add(sre_ref[sub(k), :], sim_ref[sub(k), :], pos_ref[0, k:k + 1, :], pos_ref[1, k:k + 1, :],
                               er, ei)
            sre_ref[sub(k), :] = sr
            sim_ref[sub(k), :] = si
        yp[...] = (jnp.dot(sre_ref[...].astype(BF16), cre_ref[...], preferred_element_type=F32)
                   - jnp.dot(sim_ref[...].astype(BF16), cim_ref[...], preferred_element_type=F32) + d_ref[...] * uv)
        _to_time_order(yp, ypre_ref, n)
        yg_ref[...] = _gelu(ypre_ref[...])

    ch = pl.BlockSpec((tc, LANE), lambda j, t: (t, j))
    st = pl.BlockSpec((tc, W), lambda j, t: (t, j))
    bsp = pl.BlockSpec((None, LANE, W), lambda j, t: (j, 0, 0))
    csp = pl.BlockSpec((None, W, LANE), lambda j, t: (j, 0, 0))
    return pl.pallas_call(
        body, name=name,
        out_shape=(jax.ShapeDtypeStruct((L, S5_LANES), F32), jax.ShapeDtypeStruct((L, S5_LANES), F32),
                   jax.ShapeDtypeStruct((L, D), F32), jax.ShapeDtypeStruct((L, D), F32)),
        grid=(S5_BLOCKS, nt),
        in_specs=[ch, bsp, bsp, csp, csp, pl.BlockSpec((10, SUBLANE, W), lambda j, t: (0, 0, j)),
                  pl.BlockSpec((2, n, W), lambda j, t: (0, 0, j)), pl.BlockSpec((1, LANE), lambda j, t: (0, j))],
        out_specs=(st, st, ch, ch),
        scratch_shapes=[pltpu.VMEM((2, SUBLANE, W), F32), pltpu.VMEM((tc, LANE), F32), pltpu.VMEM((tc, LANE), F32)],
        compiler_params=_params("parallel", "arbitrary"))(u, bre, bim, cre, cim, pw, pos, dsk)


def _s5_bwd(dy, u, sre, sim, bre, bim, cre, cim, pwr, posr, dsk, *, name, tc=S5_CHUNK):
    L, D = u.shape
    W = S5_LANES // S5_BLOCKS
    nt = L // tc
    n = tc // SUBLANE
    nt_dims = (((1,), (1,)), ((), ()))
    tn_dims = (((0,), (0,)), ((), ()))

    def sub(k):
        return pl.ds(SUBLANE * k, SUBLANE)

    def body(dy_ref, u_ref, sre_ref, sim_ref, bre_ref, bim_ref, cre_ref, cim_ref, pw_ref, pos_ref, d_ref,
             du_ref, dbre_ref, dbim_ref, dcre_ref, dcim_ref, ga_ref, dd_ref, gre, gim, carry, gacc, up, dyp):
        t = pl.program_id(1)

        @pl.when(t == 0)
        def _():
            for r in (carry, gacc, dbre_ref, dbim_ref, dcre_ref, dcim_ref, ga_ref, dd_ref):
                r[...] = jnp.zeros_like(r)

        _to_subchunk_order(dy_ref, dyp, n)
        _to_subchunk_order(u_ref, up, n)
        dyv, uv = dyp[...], up[...]
        dyb, ub = dyv.astype(BF16), uv.astype(BF16)
        gre[...] = lax.dot_general(dyb, cre_ref[...], nt_dims, preferred_element_type=F32)
        gim[...] = -lax.dot_general(dyb, cim_ref[...], nt_dims, preferred_element_type=F32)
        br, bi = pw_ref[8], pw_ref[9]
        xr = jnp.zeros((SUBLANE, W), F32)
        xi = jnp.zeros((SUBLANE, W), F32)
        for k in reversed(range(n)):
            xr, xi = _cmul_add(gre[sub(k), :], gim[sub(k), :], br, bi, xr, xi)
            gre[sub(k), :] = xr
            gim[sub(k), :] = xi
        for q, d in enumerate((1, 2, 4)):
            xr, xi = _cmul_add(xr, xi, pw_ref[2 * q], pw_ref[2 * q + 1], pltpu.roll(xr, SUBLANE - d, 0),
                               pltpu.roll(xi, SUBLANE - d, 0))
        cr, ci = carry[0], carry[1]
        xr, xi = _cmul_add(xr, xi, pw_ref[6], pw_ref[7], cr, ci)
        top = lax.broadcasted_iota(jnp.int32, (SUBLANE, W), 0) == SUBLANE - 1
        er = jnp.where(top, cr, pltpu.roll(xr, SUBLANE - 1, 0))
        ei = jnp.where(top, ci, pltpu.roll(xi, SUBLANE - 1, 0))
        carry[0] = jnp.broadcast_to(xr[0:1, :], (SUBLANE, W))
        carry[1] = jnp.broadcast_to(xi[0:1, :], (SUBLANE, W))
        nr, ni = er, ei
        acc_r = jnp.zeros((SUBLANE, W), F32)
        acc_i = jnp.zeros((SUBLANE, W), F32)
        for k in reversed(range(n)):
            place = slice(n - 1 - k, n - k)
            gr, gi = _cmul_conj_add(gre[sub(k), :], gim[sub(k), :], pos_ref[0, place, :], pos_ref[1, place, :], er, ei)
            gre[sub(k), :] = gr
            gim[sub(k), :] = gi
            sr, si = sre_ref[sub(k), :], sim_ref[sub(k), :]
            acc_r = acc_r + sr * nr + si * ni
            acc_i = acc_i + sr * ni - si * nr
            nr, ni = gr, gi
        gacc[0] += acc_r
        gacc[1] += acc_i
        grb, gib = gre[...].astype(BF16), gim[...].astype(BF16)
        dyp[...] = (lax.dot_general(grb, bre_ref[...], nt_dims, preferred_element_type=F32)
                    + lax.dot_general(gib, bim_ref[...], nt_dims, preferred_element_type=F32) + d_ref[...] * dyv)
        _to_time_order(dyp, up, n)
        du_ref[...] = up[...].astype(du_ref.dtype)
        dbre_ref[...] += lax.dot_general(ub, grb, tn_dims, preferred_element_type=F32)
        dbim_ref[...] += lax.dot_general(ub, gib, tn_dims, preferred_element_type=F32)
        dcre_ref[...] += lax.dot_general(sre_ref[...].astype(BF16), dyb, tn_dims, preferred_element_type=F32)
        dcim_ref[...] -= lax.dot_general(sim_ref[...].astype(BF16), dyb, tn_dims, preferred_element_type=F32)
        dd_ref[0:1, :] += jnp.sum(dyv * uv, axis=0, keepdims=True)

        @pl.when(t == nt - 1)
        def _():
            ga_ref[0:1, :] = jnp.sum(gacc[0], axis=0, keepdims=True)
            ga_ref[1:2, :] = jnp.sum(gacc[1], axis=0, keepdims=True)

    ch = pl.BlockSpec((tc, LANE), lambda j, t: (nt - 1 - t, j))
    st = pl.BlockSpec((tc, W), lambda j, t: (nt - 1 - t, j))
    bsp = pl.BlockSpec((None, LANE, W), lambda j, t: (j, 0, 0))
    csp = pl.BlockSpec((None, W, LANE), lambda j, t: (j, 0, 0))
    return pl.pallas_call(
        body, name=name,
        out_shape=(jax.ShapeDtypeStruct((L, D), BF16),
                   jax.ShapeDtypeStruct((S5_BLOCKS, LANE, W), F32), jax.ShapeDtypeStruct((S5_BLOCKS, LANE, W), F32),
                   jax.ShapeDtypeStruct((S5_BLOCKS, W, LANE), F32), jax.ShapeDtypeStruct((S5_BLOCKS, W, LANE), F32),
                   jax.ShapeDtypeStruct((SUBLANE, S5_LANES), F32), jax.ShapeDtypeStruct((SUBLANE, D), F32)),
        grid=(S5_BLOCKS, nt),
        in_specs=[ch, ch, st, st, bsp, bsp, csp, csp, pl.BlockSpec((10, SUBLANE, W), lambda j, t: (0, 0, j)),
                  pl.BlockSpec((2, n, W), lambda j, t: (0, 0, j)), pl.BlockSpec((1, LANE), lambda j, t: (0, j))],
        out_specs=(ch, bsp, bsp, csp, csp, pl.BlockSpec((SUBLANE, W), lambda j, t: (0, j)),
                   pl.BlockSpec((SUBLANE, LANE), lambda j, t: (0, j))),
        scratch_shapes=[pltpu.VMEM((tc, W), F32), pltpu.VMEM((tc, W), F32), pltpu.VMEM((2, SUBLANE, W), F32),
                        pltpu.VMEM((2, SUBLANE, W), F32), pltpu.VMEM((tc, LANE), F32), pltpu.VMEM((tc, LANE), F32)],
        compiler_params=_params("parallel", "arbitrary"))(dy, u, sre, sim, bre, bim, cre, cim, pwr, posr, dsk)


def _s5_prep(a_re, a_im, log_dt, b_re, b_im, c_re, c_im):
    dt = jnp.exp(log_dt)[:, None]
    mag = jnp.exp(a_re * dt)
    abar_re = mag * jnp.cos(a_im * dt)
    abar_im = mag * jnp.sin(a_im * dt)
    den = a_re * a_re + a_im * a_im
    nr = abar_re - 1.0
    ni = abar_im
    f_re = ((nr * a_re + ni * a_im) / den)[..., None]
    f_im = ((ni * a_re - nr * a_im) / den)[..., None]
    bbar_re = f_re * b_re - f_im * b_im
    bbar_im = f_re * b_im + f_im * b_re
    eye = jnp.eye(S5_GROUPS // S5_BLOCKS, dtype=F32)
    gb = S5_GROUPS // S5_BLOCKS

    def blk_b(bb):
        t = bb.reshape(S5_BLOCKS, gb, S5_STATE, S5_GROUP)
        return jnp.einsum('jgph,gk->jghkp', t, eye).reshape(S5_BLOCKS, gb * S5_GROUP, gb * S5_STATE)

    def blk_c(cc):
        t = cc.reshape(S5_BLOCKS, gb, S5_GROUP, S5_STATE)
        return jnp.einsum('jghp,gk->jgpkh', t, eye).reshape(S5_BLOCKS, gb * S5_STATE, gb * S5_GROUP)

    return (abar_re.reshape(1, S5_LANES), abar_im.reshape(1, S5_LANES), blk_b(bbar_re), blk_b(bbar_im),
            blk_c(c_re), blk_c(c_im))


def _cpowers(ar, ai, count):
    pr, pi, m = ar, ai, 1
    while m < count:
        tr, ti = pr[m - 1:m], pi[m - 1:m]
        pr, pi = jnp.concatenate([pr, pr * tr - pi * ti], 0), jnp.concatenate([pi, pr * ti + pi * tr], 0)
        m *= 2
    return pr, pi


def _s5_power_tables(ar, ai, n):
    pr, pi = _cpowers(ar, ai, n)
    qr, qi = _cpowers(pr[n - 1:n], pi[n - 1:n], SUBLANE)
    row = jnp.arange(SUBLANE)[:, None]
    lanes = ar.shape[1]

    def tables(sign, keep, order):
        out = []
        for d in (1, 2, 4):
            out += [jnp.where(keep(d), qr[d - 1:d], 0.0), jnp.where(keep(d), sign * qi[d - 1:d], 0.0)]
        out += [jnp.concatenate([qr[r:r + 1] for r in order], 0), sign * jnp.concatenate([qi[r:r + 1] for r in order], 0),
                ar, sign * ai]
        return jnp.stack([jnp.broadcast_to(o, (SUBLANE, lanes)) for o in out])

    fwd = tables(1.0, lambda d: row >= d, list(range(SUBLANE)))
    rev = tables(-1.0, lambda d: row + d <= SUBLANE - 1, list(reversed(range(SUBLANE))))
    return fwd, rev, jnp.stack([pr, pi])


ADAMW_PART_BLOCK_BYTES = 2 * 1024 * 1024


def _adamw(w, parts, m, v, *, name):
    n, R, C = w.shape
    assert len(parts) == n
    P = parts[0].shape[0]
    tr = R
    while P * tr * C * parts[0].dtype.itemsize > ADAMW_PART_BLOCK_BYTES and tr % 16 == 0:
        tr //= 2
    c1 = 1.0 / (1.0 - ADAM_B1 ** ADAM_STEP)
    c2 = 1.0 / (1.0 - ADAM_B2 ** ADAM_STEP)

    def body(*refs):
        w_ref, m_ref, v_ref = refs[:3]
        p_refs = refs[3:3 + n]
        g_ref, d_ref, nm_ref, nv_ref = refs[3 + n:]
        layer = pl.program_id(0)
        for q, p_ref in enumerate(p_refs):
            @pl.when(layer == q)
            def _(p_ref=p_ref):
                g = p_ref[0].astype(F32)
                for s in range(1, P):
                    g = g + p_ref[s].astype(F32)
                nm = ADAM_B1 * m_ref[...] + (1.0 - ADAM_B1) * g
                nv = ADAM_B2 * v_ref[...] + (1.0 - ADAM_B2) * (g * g)
                g_ref[...] = g
                nm_ref[...] = nm
                nv_ref[...] = nv
                d_ref[...] = -ADAM_LR * ((nm * c1) / (jnp.sqrt(nv * c2) + ADAM_EPS) + ADAM_WD * w_ref[...])

    row = pl.BlockSpec((None, tr, C), lambda l, i: (l, i, 0))
    part_specs = [pl.BlockSpec((P, tr, C), lambda l, i, q=q: (0, jnp.where(l == q, i, 0), 0)) for q in range(n)]
    out = jax.ShapeDtypeStruct((n, R, C), F32)
    return pl.pallas_call(body, name=name, out_shape=(out, out, out, out), grid=(n, R // tr),
                          in_specs=[row, row, row] + part_specs, out_specs=(row, row, row, row),
                          compiler_params=_params("arbitrary", "arbitrary"))(w, m, v, *parts)


def _all_gather(xs, axis, *, name):
    m = xs.shape[axis]
    out_shape = list(xs.shape)
    out_shape[axis] = N_DEV * m

    def body(x_ref, out_ref, send_sems, recv_sems, local_sem):
        x, y, c = _my_pos()
        me, sibling = (x, y, c), (x, y, 1 - c)
        chips = [(1 - x, y), (x, 1 - y), (1 - x, 1 - y)]

        def blk(px, py, pc):
            idx = [slice(None)] * 3
            idx[axis] = pl.ds((4 * px + 2 * py + pc) * m, m)
            return out_ref.at[tuple(idx)]

        def copy(k, block, to, src=None):
            return pltpu.make_async_remote_copy(src_ref=blk(*block) if src is None else src, dst_ref=blk(*block),
                                                send_sem=send_sems.at[k], recv_sem=recv_sems.at[k],
                                                device_id=to, device_id_type=MESH_ID)

        mine = pltpu.make_async_copy(x_ref, blk(*me), local_sem)
        mine.start()
        first = [copy(0, me, sibling, src=x_ref)]
        first += [copy(1 + j, me, (*chip, c), src=x_ref) for j, chip in enumerate(chips)]
        for cp in first:
            cp.start()
        passed = [copy(4 + j, (*chip, c), sibling) for j, chip in enumerate(chips)]
        for j, chip in enumerate(chips):
            copy(1 + j, (*chip, c), me).wait_recv()
            passed[j].start()
        copy(0, sibling, me).wait_recv()
        for j, chip in enumerate(chips):
            copy(4 + j, (*chip, 1 - c), me).wait_recv()
        for cp in first + passed:
            cp.wait_send()
        mine.wait()

    hbm = pl.BlockSpec(memory_space=pl.ANY)
    return pl.pallas_call(body, name=name, out_shape=jax.ShapeDtypeStruct(tuple(out_shape), xs.dtype),
                          in_specs=[hbm], out_specs=hbm,
                          scratch_shapes=[pltpu.SemaphoreType.DMA((N_DEV - 1,)), pltpu.SemaphoreType.DMA((N_DEV - 1,)),
                                          pltpu.SemaphoreType.DMA],
                          compiler_params=pltpu.CompilerParams(has_side_effects=True))(xs)


NEAR_PEERS = (1, 2, 4, 6)
RELAY_PEERS = (2, 4, 6)


def _block(ref, axis, idx, m):
    return ref.at[pl.ds(idx * m, m), :] if axis == 0 else ref.at[:, pl.ds(idx * m, m)]


def _exchange_copies(metas, src_refs, zone_refs, send_sems, recv_sems, base, phase):
    x, y, c = _my_pos()
    me = 4 * x + 2 * y + c

    def place(r):
        pos = (1 - x if r & 4 else x, 1 - y if r & 2 else y, 1 - c if r & 1 else c)
        return pos, 4 * pos[0] + 2 * pos[1] + pos[2]

    def copies(r, to, src, dst, arrival):
        return tuple(pltpu.make_async_remote_copy(src_ref=src, dst_ref=d, send_sem=send_sems.at[base + r - 1],
                                                  recv_sem=recv_sems.at[base + r - 1], device_id=to,
                                                  device_id_type=MESH_ID) for d in (dst, arrival))

    pairs, own = [], []
    if phase == 'relay':
        sibling, _ = place(1)
        for r in RELAY_PEERS:
            held, comes = place(r)[1], place(r | 1)[1]
            for (kind, axis, m), z_ref in zip(metas, zone_refs):
                pairs.append(copies(r, sibling, _block(z_ref, axis, held, m), _block(z_ref, axis, held, m),
                                    _block(z_ref, axis, comes, m)))
        return pairs, own
    for r in (NEAR_PEERS if phase == 'near' else range(1, N_DEV)):
        pos, peer = place(r)
        for (kind, axis, m), s_ref, z_ref in zip(metas, src_refs, zone_refs):
            if kind == 'gather':
                pairs.append(copies(r, pos, s_ref, _block(z_ref, axis, me, m), _block(z_ref, axis, peer, m)))
            else:
                pairs.append(copies(r, pos, _block(s_ref, axis, peer, m), z_ref.at[me], z_ref.at[peer]))
    for (kind, axis, m), s_ref, z_ref in zip(metas, src_refs, zone_refs):
        src, dst = (s_ref, _block(z_ref, axis, me, m)) if kind == 'gather' else (_block(s_ref, axis, me, m), z_ref.at[me])
        own.append(pltpu.make_async_copy(src, dst, recv_sems.at[base + N_DEV - 1]))
    return pairs, own


def _exchange_start(groups, after, *, name, relayed=False):
    flat = [it for g in groups for it in g]
    n, ng = len(flat), len(groups)
    metas = [it[2] for it in flat]
    bounds = [(sum(len(g) for g in groups[:q]), sum(len(g) for g in groups[:q + 1])) for q in range(ng)]
    phase = 'near' if relayed else 'all'

    def body(*refs):
        src_refs = refs[:n]
        send_sems, recv_sems = refs[n + 1], refs[n + 2]
        zone_refs = refs[2 * n + 3:3 * n + 3]
        token = refs[-1]
        for q, (lo, hi) in enumerate(bounds):
            pairs, own = _exchange_copies(metas[lo:hi], src_refs[lo:hi], zone_refs[lo:hi], send_sems, recv_sems,
                                          q * N_DEV, phase)
            for outgoing, _ in pairs:
                outgoing.start()
            for cp in own:
                cp.start()
        token[...] = jnp.zeros_like(token)

    hbm = pl.BlockSpec(memory_space=pltpu.HBM)
    sem = pl.BlockSpec(memory_space=pltpu.SEMAPHORE)
    srcs = [it[0] for it in flat]
    res = pl.pallas_call(
        body, name=name,
        out_shape=(pltpu.SemaphoreType.DMA((ng * N_DEV,)), pltpu.SemaphoreType.DMA((ng * N_DEV,)),
                   *[pltpu.HBM(a.shape, a.dtype) for a in srcs], *[pltpu.HBM(it[1], it[0].dtype) for it in flat],
                   jax.ShapeDtypeStruct((SUBLANE, LANE), F32)),
        in_specs=[hbm] * n + [pl.BlockSpec(memory_space=pl.ANY)],
        out_specs=(sem, sem, *[hbm] * (2 * n), pl.BlockSpec(memory_space=pltpu.VMEM)),
        input_output_aliases={q: 2 + q for q in range(n)},
        compiler_params=pltpu.CompilerParams(has_side_effects=pltpu.SideEffectType.DATAFLOW_SIDE_EFFECTING),
    )(*[pltpu.with_memory_space_constraint(a, pltpu.HBM) for a in srcs], after)
    handles = [(res[0], res[1], q * N_DEV, phase, list(res[2 + lo:2 + hi]), list(res[2 + n + lo:2 + n + hi]),
                metas[lo:hi]) for q, (lo, hi) in enumerate(bounds)]
    return handles, res[-1]


def _exchange_wait(handle, after, *, name):
    send_sems, recv_sems, base, phase, srcs, zones, metas = handle
    ns, nz = len(srcs), len(zones)

    def body(*refs):
        src_refs, zone_refs = refs[:ns], refs[ns:ns + nz]
        s_sems, r_sems = refs[ns + nz], refs[ns + nz + 1]
        pairs, own = _exchange_copies(metas, src_refs, zone_refs, s_sems, r_sems, base, phase)
        for outgoing, incoming in pairs:
            outgoing.wait_send()
            incoming.wait_recv()
        for cp in own:
            cp.wait()

    hbm = pl.BlockSpec(memory_space=pltpu.HBM)
    sem = pl.BlockSpec(memory_space=pltpu.SEMAPHORE)
    arrays = srcs + zones
    res = pl.pallas_call(
        body, name=name,
        out_shape=tuple(pltpu.HBM(a.shape, a.dtype) for a in arrays),
        in_specs=[hbm] * (ns + nz) + [sem, sem, pl.BlockSpec(memory_space=pl.ANY)],
        out_specs=tuple([hbm] * (ns + nz)),
        input_output_aliases={q: q for q in range(ns + nz)},
        compiler_params=pltpu.CompilerParams(has_side_effects=pltpu.SideEffectType.DATAFLOW_SIDE_EFFECTING),
    )(*arrays, send_sems, recv_sems, after)
    return list(res[ns:])


def _exchange_relay(handle, after, *, name):
    metas = handle[6]
    zones = _exchange_wait(handle, after, name=name + "_in")
    nz = len(zones)

    def body(*refs):
        zone_refs = refs[:nz]
        send_sems, recv_sems = refs[nz], refs[nz + 1]
        pairs, _ = _exchange_copies(metas, (), zone_refs, send_sems, recv_sems, 0, 'relay')
        for outgoing, _ in pairs:
            outgoing.start()
        refs[-1][...] = jnp.zeros_like(refs[-1])

    hbm = pl.BlockSpec(memory_space=pltpu.HBM)
    sem = pl.BlockSpec(memory_space=pltpu.SEMAPHORE)
    res = pl.pallas_call(
        body, name=name + "_out",
        out_shape=(pltpu.SemaphoreType.DMA((N_DEV,)), pltpu.SemaphoreType.DMA((N_DEV,)),
                   *[pltpu.HBM(z.shape, z.dtype) for z in zones], jax.ShapeDtypeStruct((SUBLANE, LANE), F32)),
        in_specs=[hbm] * nz, out_specs=(sem, sem, *[hbm] * nz, pl.BlockSpec(memory_space=pltpu.VMEM)),
        input_output_aliases={q: 2 + q for q in range(nz)},
        compiler_params=pltpu.CompilerParams(has_side_effects=pltpu.SideEffectType.DATAFLOW_SIDE_EFFECTING),
    )(*zones)
    return (res[0], res[1], 0, 'relay', [], list(res[2:2 + nz]), metas), res[-1][0:1, 0:1]


def _pad_rows(a, rows):
    return jnp.pad(a, ((0, rows - a.shape[0]), (0, 0)))


PACK_ROWS = 2 * SUBLANE


def _rows(a):
    flat = a.reshape(-1).astype(F32)
    pad = -flat.shape[0] % (PACK_ROWS * LANE)
    return (jnp.pad(flat, (0, pad)) if pad else flat).reshape(-1, LANE)


def _pack_rows(arrays):
    return jnp.concatenate([_rows(a) for a in arrays], 0)


def _unpack_rows(t, shapes):
    out, off = [], 0
    for shp in shapes:
        size = math.prod(shp)
        rows = -(-size // (PACK_ROWS * LANE)) * PACK_ROWS
        out.append(t[off:off + rows].reshape(-1)[:size].reshape(shp))
        off += rows
    return out


def _stat_row(st, r):
    return st[r:r + 1, :]


def kernel(x, c, ada_w, ada_b, norm1_g, norm2_g, ff_w1, ff_w2, final_g, conv_w_in, conv_w, conv_b, conv_w_out, ssm_w_in, ssm_a_re, ssm_a_im, ssm_log_dt, ssm_b_re, ssm_b_im, ssm_c_re, ssm_c_im, ssm_d, ssm_glu_w, ssm_glu_b, ssm_w_out, sg_w_in, sg_v_g, sg_w_s, sg_b_s, sg_w_out, loss_target, m_ada_w, m_ada_b, m_norm1_g, m_norm2_g, m_ff_w1, m_ff_w2, m_final_g, m_conv_w_in, m_conv_w, m_conv_b, m_conv_w_out, m_ssm_w_in, m_ssm_a_re, m_ssm_a_im, m_ssm_log_dt, m_ssm_b_re, m_ssm_b_im, m_ssm_c_re, m_ssm_c_im, m_ssm_d, m_ssm_glu_w, m_ssm_glu_b, m_ssm_w_out, m_sg_w_in, m_sg_v_g, m_sg_w_s, m_sg_b_s, m_sg_w_out, v_ada_w, v_ada_b, v_norm1_g, v_norm2_g, v_ff_w1, v_ff_w2, v_final_g, v_conv_w_in, v_conv_w, v_conv_b, v_conv_w_out, v_ssm_w_in, v_ssm_a_re, v_ssm_a_im, v_ssm_log_dt, v_ssm_b_re, v_ssm_b_im, v_ssm_c_re, v_ssm_c_im, v_ssm_d, v_ssm_glu_w, v_ssm_glu_b, v_ssm_w_out, v_sg_w_in, v_sg_v_g, v_sg_w_s, v_sg_b_s, v_sg_w_out):
    P = dict(zip(INPUTS, (x, c, ada_w, ada_b, norm1_g, norm2_g, ff_w1, ff_w2, final_g, conv_w_in, conv_w, conv_b, conv_w_out, ssm_w_in, ssm_a_re, ssm_a_im, ssm_log_dt, ssm_b_re, ssm_b_im, ssm_c_re, ssm_c_im, ssm_d, ssm_glu_w, ssm_glu_b, ssm_w_out, sg_w_in, sg_v_g, sg_w_s, sg_b_s, sg_w_out, loss_target, m_ada_w, m_ada_b, m_norm1_g, m_norm2_g, m_ff_w1, m_ff_w2, m_final_g, m_conv_w_in, m_conv_w, m_conv_b, m_conv_w_out, m_ssm_w_in, m_ssm_a_re, m_ssm_a_im, m_ssm_log_dt, m_ssm_b_re, m_ssm_b_im, m_ssm_c_re, m_ssm_c_im, m_ssm_d, m_ssm_glu_w, m_ssm_glu_b, m_ssm_w_out, m_sg_w_in, m_sg_v_g, m_sg_w_s, m_sg_b_s, m_sg_w_out, v_ada_w, v_ada_b, v_norm1_g, v_norm2_g, v_ff_w1, v_ff_w2, v_final_g, v_conv_w_in, v_conv_w, v_conv_b, v_conv_w_out, v_ssm_w_in, v_ssm_a_re, v_ssm_a_im, v_ssm_log_dt, v_ssm_b_re, v_ssm_b_im, v_ssm_c_re, v_ssm_c_im, v_ssm_d, v_ssm_glu_w, v_ssm_glu_b, v_ssm_w_out, v_sg_w_in, v_sg_v_g, v_sg_w_s, v_sg_b_s, v_sg_w_out)))
    L, D = x.shape[1], x.shape[2]
    me = _my_index()
    xs = x[0]
    tgt = loss_target[0]
    n_conv = conv_w_in.shape[0]

    def gather_item(shard, axis):
        full = tuple(N_DEV * s if a == axis else s for a, s in enumerate(shard.shape))
        return shard, full, ('gather', axis, shard.shape[axis])

    def mixer_shards(i):
        kind, j = i % 3, i // 3
        if kind == 0:
            return [(conv_w_in[j], 1), (conv_w_out[j], 0)]
        if kind == 1:
            return [(ssm_w_in[j], 0), (ssm_glu_w[j], 0), (ssm_w_out[j], 0)]
        return [(sg_w_in[j], 1), (sg_w_out[j], 0)]

    gather_groups = [[gather_item(w.astype(BF16), ax) for w, ax in shards]
                     for i in range(DEPTH) for shards in (mixer_shards(i), [(ff_w1[i], 1), (ff_w2[i], 0)])]
    first_gather, first_token = _exchange_start(gather_groups[:1], c, name="gather_start_first", relayed=True)

    c_act = c * (1.0 / (1.0 + jnp.exp(-c))) + first_token[0:1, 0:1]
    vec_rows = jnp.concatenate([c_act.reshape(D // LANE, LANE), conv_w.reshape(-1, LANE), conv_b.reshape(-1, LANE),
                                sg_v_g.reshape(-1, LANE)], 0)
    n_vec = vec_rows.shape[0]
    vec_all = _all_gather(_pad_rows(vec_rows, 24)[None], 0, name="gather_vectors")
    c_all = vec_all[:, :D // LANE, :].reshape(N_DEV, D)
    sharded_full = vec_all[:, D // LANE:n_vec, :].transpose(1, 0, 2).reshape(n_vec - D // LANE, D)
    conv_w_full = sharded_full[:3 * n_conv].reshape(n_conv, 3, D)
    conv_b_full = sharded_full[3 * n_conv:4 * n_conv]
    sg_vg_full = sharded_full[4 * n_conv:4 * n_conv + 1]

    c_pad = _pad_rows(c_all, LANE)
    ncol = ada_w.shape[2]
    mod_part = jnp.stack([_mm(c_pad, ada_w[i], name=f"ada_fwd{i}")[:N_DEV] for i in range(DEPTH)])
    mod_all = _all_gather(mod_part.reshape(1, DEPTH * N_DEV, ncol), 0, name="gather_mod")
    mod_all = mod_all.reshape(N_DEV, DEPTH, N_DEV, ncol)
    mod_me = lax.dynamic_index_in_dim(mod_all, me, 2, keepdims=False)
    mod = mod_me.transpose(1, 0, 2).reshape(DEPTH, N_DEV * ncol) + ada_b
    gathers, gather_token = _exchange_start(gather_groups[1:], mod, name="gather_start", relayed=True)
    gathers = first_gather + gathers
    mod = mod + gather_token[0:1, 0:1]
    relayed = [None] * len(gathers)
    relayed[0], sent = _exchange_relay(gathers[0], mod, name="gather_mix_relay0")

    s5_args = (ssm_a_re[0], ssm_a_im[0], ssm_log_dt[0], ssm_b_re[0], ssm_b_im[0], ssm_c_re[0], ssm_c_im[0])
    (abar_re, abar_im, bblk_re, bblk_im, cblk_re, cblk_im), s5_vjp = jax.vjp(_s5_prep, *s5_args)
    pw_fwd, pw_rev, pos_fwd = _s5_power_tables(abar_re, abar_im, S5_CHUNK // SUBLANE)
    s5_w = tuple(t.astype(BF16) for t in (bblk_re, bblk_im, cblk_re, cblk_im))
    causal = jnp.tril(jnp.ones((SG_CHUNK, SG_CHUNK), dtype=bool))
    ws_m = jnp.where(causal[None], sg_w_s[0], 0.0)
    ws_b = ws_m.astype(BF16)
    wst_b = ws_m.transpose(0, 2, 1).astype(BF16)
    bsb = jnp.broadcast_to(sg_b_s[0][:, :, None], (SG_HEADS, SG_CHUNK, LANE))

    saved = []
    xa = xs
    mods = [[mod[i:i + 1, q * D:(q + 1) * D] for q in range(6)] for i in range(DEPTH)]
    wn1s = [norm1_g[i:i + 1] * (1.0 + mods[i][1]) for i in range(DEPTH)]
    h1 = _normmod_fwd(xa, wn1s[0], mods[0][0] + sent, name="norm1_fwd0")
    for i in range(DEPTH):
        kind, j = i % 3, i // 3
        sh1, sc1, g1, sh2, sc2, g2 = mods[i]
        wn1 = wn1s[i]
        wn2 = norm2_g[i:i + 1] * (1.0 + sc2)
        S = dict(x_in=xa, g1=g1, g2=g2, sc1=sc1, sc2=sc2, wn1=wn1, wn2=wn2)
        w_mix = _exchange_wait(relayed[2 * i], h1, name=f"gather_mix_wait{i}")
        S['h1'] = h1
        if kind == 0:
            bcx = _mm(h1, w_mix[0], name=f"conv_in{i}", out_dtypes=(BF16,), bm=2048)
            wb = _pad_rows(jnp.concatenate([conv_w_full[j], conv_b_full[j:j + 1]], 0), SUBLANE)
            pb = _conv_fwd(bcx, wb, name=f"conv_mix{i}")
            S.update(bcx=bcx, wb=wb, pb=pb)
        elif kind == 1:
            u = _mm(h1, w_mix[0], name=f"ssm_in{i}")
            sre, sim, ypre, yg = _s5_fwd(u, *s5_w, pw_fwd, pos_fwd, ssm_d, name=f"s5_scan{i}")

            def glu_epi(acc, yv, bias):
                t = acc + bias
                return yv * (1.0 / (1.0 + jnp.exp(-t))), t

            pb, tt = _mm(yg, w_mix[1], name=f"ssm_glu{i}", out_dtypes=(BF16, F32), epi=glu_epi,
                         extras=[(yg, 'mn'), (ssm_glu_b, 'n')])
            S.update(u=u, sre=sre, sim=sim, ypre=ypre, yg=yg, pb=pb, tt=tt)
        else:
            uv = _mm(h1, w_mix[0], name=f"sg_in{i}", bm=2048)
            pb = _sg_fwd(uv, sg_vg_full, ws_b, bsb, name=f"sg_mix{i}")
            S.update(uv=uv, pb=pb)
        relayed[2 * i + 1], sent = _exchange_relay(gathers[2 * i + 1], pb, name=f"gather_ff_relay{i}")
        x_mid, y_mix, h2 = _mm(pb, w_mix[-1], name=f"mix_out{i}", out_dtypes=(F32, BF16, BF16), epi=_epi_residual_norm,
                               extras=[(xa, 'mn'), (g1 + sent, 'n'), (wn2, 'n'), (sh2, 'n')])
        w1_full, w2_full = _exchange_wait(relayed[2 * i + 1], h2, name=f"gather_ff_wait{i}")
        S.update(w_mix=w_mix, w1=w1_full, w2=w2_full)
        ra = _mm(h2, w1_full, name=f"ff_up{i}", out_dtypes=(BF16,), epi=lambda acc: (jnp.maximum(acc, 0.0),), bm=2048)
        if i + 1 < DEPTH:
            relayed[2 * i + 2], sent = _exchange_relay(gathers[2 * i + 2], ra, name=f"gather_mix_relay{i + 1}")
            xa, f_out, h1 = _mm(ra, w2_full, name=f"ff_down{i}", out_dtypes=(F32, BF16, BF16), a_fn=_square,
                                epi=_epi_residual_norm, bm=512, bk=w2_full.shape[0],
                                extras=[(x_mid, 'mn'), (g2 + sent, 'n'), (wn1s[i + 1], 'n'), (mods[i + 1][0], 'n')])
        else:
            f_out = None
            dx, dfb, st = _mm(ra, w2_full, name=f"ff_down{i}", out_dtypes=(F32, BF16), epi=_epi_loss_head, a_fn=_square,
                              n_stats=3, bm=512, bk=w2_full.shape[0],
                              extras=[(x_mid, 'mn'), (g2, 'n'), (tgt, 'mn'), (final_g[None], 'n')])
        S.update(x_mid=x_mid, y_mix=y_mix, h2=h2, ra=ra, f_out=f_out)
        saved.append(S)

    loss_tile = st[:, :LANE]
    d_final_g = _stat_row(st, 1)
    dg2_next = _stat_row(st, 2)

    def scatter_item(g, axis):
        m = g.shape[axis] // N_DEV
        shard = tuple(m if a == axis else s for a, s in enumerate(g.shape))
        return g, (N_DEV,) + shard, ('scatter', axis, m)

    dmod = [None] * DEPTH
    dn1g, dn2g = [None] * DEPTH, [None] * DEPTH
    d_conv_w, d_conv_b = [None] * n_conv, [None] * n_conv
    ff_sent, mix_sent = [None] * DEPTH, [None] * DEPTH
    small = {}
    for i in reversed(range(DEPTH)):
        kind, j = i % 3, i // 3
        S = saved[i]
        w_mix = S['w_mix']
        dg2 = dg2_next
        da = _mm(dfb, S['w2'], tb=True, name=f"ff_down_bwd{i}", out_dtypes=(BF16,), bm=2048,
                 epi=lambda acc, rav: (acc * (2.0 * rav.astype(F32)),), extras=[(S['ra'], 'mn')])
        dw2 = _wgrad(S['ra'], dfb, name=f"ff_w2_grad{i}", a_fn=_square, bm=256, bn=1024)
        dw1 = _wgrad(S['h2'], da, name=f"ff_w1_grad{i}")
        (ff_sent[i],), token = _exchange_start([[scatter_item(dw1, 1), scatter_item(dw2, 0)]], dx,
                                               name=f"ff_grads_start{i}")
        dx_mid, dyb, st2 = _mm(da, S['w1'], tb=True, name=f"ff_up_bwd{i}", out_dtypes=(F32, BF16), bm=512,
                               bk=da.shape[1], epi=_epi_norm_bwd(True), n_stats=3,
                               extras=[(S['x_mid'], 'mn'), (S['wn2'] + token[0:1, 0:1], 'n'), (dx, 'mn'),
                                       (S['y_mix'], 'mn'), (S['g1'], 'n')])
        dsc2 = _stat_row(st2, 0) * norm2_g[i:i + 1]
        dn2g[i] = _stat_row(st2, 0) * (1.0 + S['sc2'])
        dsh2 = _stat_row(st2, 1)
        dg1 = _stat_row(st2, 2)
        if kind == 0:
            dp = _mm(dyb, w_mix[1], tb=True, name=f"conv_out_bwd{i}", out_dtypes=(BF16,))
            d_cwo = _wgrad(S['pb'], dyb, name=f"conv_w_out_grad{i}")
            dbcx, stc = _conv_bwd(dp, S['bcx'], S['wb'], name=f"conv_mix_bwd{i}")
            d_conv_w[j] = stc[0:3]
            d_conv_b[j] = stc[3:4]
            dh_operand, dh_name = dbcx, "conv_in_bwd"
            d_cwi = _wgrad(S['h1'], dbcx, name=f"conv_w_in_grad{i}")
            mix_grads = [scatter_item(d_cwi, 1), scatter_item(d_cwo, 0)]
        elif kind == 1:
            dtb, dya, stg = _mm(dyb, w_mix[2], tb=True, name=f"ssm_out_bwd{i}", out_dtypes=(BF16, F32), bm=512,
                                epi=_epi_glu_bwd, n_stats=1, extras=[(S['yg'], 'mn'), (S['tt'], 'mn')])
            d_ssm_out = _wgrad(S['pb'], dyb, name=f"ssm_w_out_grad{i}")
            dypre = _mm(dtb, w_mix[1], tb=True, name=f"ssm_glu_in_bwd{i}",
                        epi=lambda acc, a, yp: ((a + acc) * _gelu_grad(yp),),
                        extras=[(dya, 'mn'), (S['ypre'], 'mn')])
            d_glu = _wgrad(S['yg'], dtb, name=f"ssm_glu_w_grad{i}", bm=512)
            dub, dbre, dbim, dcre, dcim, ga, dd = _s5_bwd(dypre, S['u'], S['sre'], S['sim'], *s5_w, pw_rev, pos_fwd, ssm_d,
                                                           name=f"s5_scan_bwd{i}")
            dh_operand, dh_name = dub, "ssm_in_bwd"
            d_ssm_in = _wgrad(S['h1'], dub, name=f"ssm_w_in_grad{i}")
            da_re, da_im, dlog_dt, db_re, db_im, dc_re, dc_im = s5_vjp((ga[0:1], ga[1:2], dbre, dbim, dcre, dcim))
            s5_small = _pack_rows([da_re, da_im, dlog_dt, db_re, db_im, dc_re, dc_im, dd[0], stg[0]])
            mix_grads = [scatter_item(d_ssm_in, 0), scatter_item(d_glu, 0), scatter_item(d_ssm_out, 0),
                         gather_item(s5_small.astype(BF16), 0)]
        else:
            dp = _mm(dyb, w_mix[1], tb=True, name=f"sg_out_bwd{i}")
            d_sgo = _wgrad(S['pb'], dyb, name=f"sg_w_out_grad{i}")
            duv, dws, dbs, stv = _sg_bwd(dp, S['uv'], sg_vg_full, ws_b, wst_b, bsb, name=f"sg_mix_bwd{i}")
            dh_operand, dh_name = duv, "sg_in_bwd"
            d_sgi = _wgrad(S['h1'], duv, name=f"sg_w_in_grad{i}")
            sg_small = _pack_rows([jnp.where(causal[None], dws, 0.0), jnp.sum(dbs, axis=-1)])
            d_sg_vg = stv[0:1]
            mix_grads = [scatter_item(d_sgi, 1), scatter_item(d_sgo, 0), gather_item(sg_small.astype(BF16), 0)]
        wn1 = S['wn1']
        gate = []
        if i > 0:
            (mix_sent[i],), token = _exchange_start([mix_grads], dx_mid, name=f"mix_grads_start{i}")
            wn1 = wn1 + token[0:1, 0:1]
            gate = [(saved[i - 1]['f_out'], 'mn'), (saved[i - 1]['g2'], 'n')]
        res = _mm(dh_operand, w_mix[0], tb=True, name=f"{dh_name}{i}", out_dtypes=(F32, BF16) if i > 0 else (F32,),
                  bm=512, bk=w_mix[0].shape[1], epi=_epi_norm_bwd(i > 0), n_stats=3 if i > 0 else 2,
                  extras=[(S['x_in'], 'mn'), (wn1, 'n'), (dx_mid, 'mn')] + gate)
        if i > 0:
            dx, dfb, st1 = res
            dg2_next = _stat_row(st1, 2)
        else:
            dx, st1 = res
        dsc1 = _stat_row(st1, 0) * norm1_g[i:i + 1]
        dn1g[i] = _stat_row(st1, 0) * (1.0 + S['sc1'])
        dsh1 = _stat_row(st1, 1)
        dmod[i] = jnp.concatenate([dsh1, dsc1, dg1, dsh2, dsc2, dg2], 1)
    grad_x = dx[None]

    out = {}

    def small_group(names, parts, label):
        shapes = [P[n].shape for n in names]
        w, m, v = (_pack_rows([P[pre + n] for n in names])[None] for pre in ('', 'm_', 'v_'))
        res = [_unpack_rows(t[0], shapes) for t in _adamw(w, [parts], m, v, name=label)]
        for q, n in enumerate(names):
            out[n] = tuple(r[q] for r in res)

    small.update(ada_b=jnp.concatenate(dmod, 0), norm1_g=jnp.concatenate(dn1g, 0), norm2_g=jnp.concatenate(dn2g, 0),
                 final_g=d_final_g, conv_w=jnp.stack(d_conv_w), conv_b=jnp.concatenate(d_conv_b, 0), sg_v_g=d_sg_vg)
    last_pack = _pack_rows([small[n] for n in LAST_SMALL + SMALL_SHARD])
    n_last = _pack_rows([P[n] for n in LAST_SMALL]).shape[0]
    n_pack = last_pack.shape[0]
    pack_all = _all_gather(jnp.concatenate([last_pack, loss_tile], 0)[None], 0, name="gather_small_grads")
    loss = jnp.sum(pack_all[:, n_pack, 0])
    (mix_sent[0],), last_token = _exchange_start([mix_grads], pack_all, name="mix_grads_start0")
    small_group(LAST_SMALL, pack_all[:, :n_last], "adamw_small")
    sh_rows = (n_pack - n_last) // N_DEV
    sh_parts = pack_all[:, n_last:n_pack].reshape(N_DEV, sh_rows, N_DEV, LANE)
    sh_parts = lax.dynamic_index_in_dim(sh_parts, me, 2, keepdims=False)
    sh_parts = jnp.pad(sh_parts, ((0, 0), (0, 16 - sh_rows), (0, 0)))

    def pack_shard(prefix):
        return _pad_rows(jnp.concatenate([P[prefix + n].reshape(-1, LANE) for n in SMALL_SHARD], 0), 16)[None]

    sg_, sd_, sm_, sv_ = _adamw(pack_shard(''), [sh_parts], pack_shard('m_'), pack_shard('v_'), name="adamw_channel")
    off = 0
    for n in SMALL_SHARD:
        rows = math.prod(P[n].shape) // LANE
        out[n] = tuple(t[0, off:off + rows].reshape(P[n].shape) for t in (sg_, sd_, sm_, sv_))
        off += rows

    dmod_all = pack_all[:, :DEPTH * 6 * D // LANE].reshape(N_DEV, DEPTH, 6 * D)
    dmod_cols = lax.dynamic_slice_in_dim(dmod_all, me * ncol, ncol, 2)
    g_ada = [_mm(c_pad, _pad_rows(dmod_cols[:, i], LANE), ta=True, name=f"ada_w_grad{i}")[None] for i in range(DEPTH)]

    def big(name, parts):
        res = _adamw(P[name], parts, P['m_' + name], P['v_' + name], name="adamw_" + name)
        out[name] = res
        return res[1]

    ff_parts = [_exchange_wait(ff_sent[i], last_token, name=f"ff_grads_wait{i}") for i in range(DEPTH)]
    mix_parts = [None] + [_exchange_wait(mix_sent[i], last_token, name=f"mix_grads_wait{i}") for i in range(1, DEPTH)]
    big('ada_w', g_ada)
    big('ff_w1', [p[0] for p in ff_parts])
    big('ff_w2', [p[1] for p in ff_parts])
    done = big('sg_w_in', [mix_parts[2][0]])
    mix_parts[0] = _exchange_wait(mix_sent[0], done, name="mix_grads_wait0")
    big('conv_w_in', [mix_parts[i][0] for i in range(DEPTH) if i % 3 == 0])
    row_names = ['conv_w_out', 'ssm_w_in', 'ssm_glu_w', 'ssm_w_out', 'sg_w_out']
    row_parts = ([mix_parts[i][1] for i in range(DEPTH) if i % 3 == 0] + mix_parts[1][:3] + [mix_parts[2][1]])
    small_group(S5_SMALL, mix_parts[1][3].reshape(N_DEV, -1, LANE), "adamw_s5")
    small_group(SG_SMALL, mix_parts[2][2].reshape(N_DEV, -1, LANE), "adamw_sg")
    row_w, row_m, row_v = (jnp.concatenate([P[pre + n] for n in row_names], 0) for pre in ('', 'm_', 'v_'))
    rw = _adamw(row_w, row_parts, row_m, row_v, name="adamw_row_sharded")
    off = 0
    for n in row_names:
        cnt = P[n].shape[0]
        out[n] = tuple(t[off:off + cnt] for t in rw)
        off += cnt

    return (loss, grad_x, *[out[n][0] for n in WEIGHTS], *[out[n][1] for n in WEIGHTS],
            *[out[n][2] for n in WEIGHTS], *[out[n][3] for n in WEIGHTS])
```

```python
import math

import jax
import jax.numpy as jnp
from jax import lax
from jax.experimental import pallas as pl
from jax.experimental.pallas import tpu as pltpu

F32 = jnp.float32
BF16 = jnp.bfloat16

N_DEV = 8
MESH_ID = pl.DeviceIdType.MESH
DEPTH = 4
EPS = 1e-6
S5_GROUPS, S5_GROUP, S5_STATE = 64, 16, 64
S5_LANES = S5_GROUPS * S5_STATE
S5_BLOCKS = 8
S5_CHUNK = 512
SG_HEADS, SG_CHUNK = 8, 128
LANE = 128
SUBLANE = 8
VMEM_LIMIT = 48 * 1024 * 1024
ADAM_LR, ADAM_B1, ADAM_B2, ADAM_EPS, ADAM_WD, ADAM_STEP = 0.001, 0.9, 0.999, 1e-08, 0.01, 10
GELU_C = math.sqrt(2.0 / math.pi)
GELU_A = 0.044715

WEIGHTS = ['ada_w', 'ada_b', 'norm1_g', 'norm2_g', 'ff_w1', 'ff_w2', 'final_g', 'conv_w_in', 'conv_w', 'conv_b',
           'conv_w_out', 'ssm_w_in', 'ssm_a_re', 'ssm_a_im', 'ssm_log_dt', 'ssm_b_re', 'ssm_b_im', 'ssm_c_re',
           'ssm_c_im', 'ssm_d', 'ssm_glu_w', 'ssm_glu_b', 'ssm_w_out', 'sg_w_in', 'sg_v_g', 'sg_w_s', 'sg_b_s',
           'sg_w_out']
INPUTS = ['x', 'c'] + WEIGHTS + ['loss_target'] + ['m_' + n for n in WEIGHTS] + ['v_' + n for n in WEIGHTS]
S5_SMALL = ['ssm_a_re', 'ssm_a_im', 'ssm_log_dt', 'ssm_b_re', 'ssm_b_im', 'ssm_c_re', 'ssm_c_im', 'ssm_d', 'ssm_glu_b']
SG_SMALL = ['sg_w_s', 'sg_b_s']
LAST_SMALL = ['ada_b', 'norm1_g', 'norm2_g', 'final_g']
SMALL_SHARD = ['conv_w', 'conv_b', 'sg_v_g']


def _params(*sem):
    return pltpu.CompilerParams(dimension_semantics=sem or None, vmem_limit_bytes=VMEM_LIMIT)


def _my_pos():
    return lax.axis_index("x"), lax.axis_index("y"), lax.axis_index("c")


def _my_index():
    x, y, c = _my_pos()
    return 4 * x + 2 * y + c


def _mm(a, b, *, name, ta=False, tb=False, out_dtypes=(F32,), epi=None, extras=(), a_fn=None, n_stats=0, bm=1024,
        bn=1024, bk=1024):
    a_chunks = a.shape[0] if a.ndim == 3 else 0
    b_chunks = b.shape[0] if b.ndim == 3 else 0
    assert not (a_chunks and ta) and not (b_chunks and tb)
    if a_chunks:
        m, k = a.shape[1], a_chunks * a.shape[2]
        bk = k
    else:
        m, k = (a.shape[1], a.shape[0]) if ta else a.shape
    if b_chunks:
        k2, n = b.shape[1], b_chunks * b.shape[2]
        bn = min(bn, b.shape[2])
    else:
        k2, n = (b.shape[1], b.shape[0]) if tb else b.shape
    assert k == k2, (a.shape, b.shape, ta, tb)
    bm, bn, bk = min(bm, m), min(bn, n), min(bk, k)
    assert m % bm == 0 and n % bn == 0 and k % bk == 0, (m, n, k, bm, bn, bk)
    nk = k // bk
    assert nk == 1 or n_stats == 0
    n_ex, n_out = len(extras), len(out_dtypes)
    dims = (((0 if ta else 1,), (1 if tb else 0,)), ((), ()))

    def body(*refs):
        a_ref, b_ref = refs[0], refs[1]
        ex_refs = refs[2:2 + n_ex]
        out_refs = refs[2 + n_ex:2 + n_ex + n_out]

        def finish(acc):
            outs = epi(acc, *[r[...] for r in ex_refs]) if epi is not None else (acc,)
            for r, o in zip(out_refs, outs[:n_out]):
                r[...] = o.astype(r.dtype)
            if n_stats:
                st_ref = refs[2 + n_ex + n_out]

                @pl.when(pl.program_id(0) == 0)
                def _():
                    st_ref[...] = jnp.zeros_like(st_ref)

                for q, row in enumerate(outs[n_out:]):
                    st_ref[q:q + 1, :] += row

        av = jnp.concatenate([a_ref[t] for t in range(a_chunks)], axis=1) if a_chunks else a_ref[...]
        if a_fn is not None:
            av = a_fn(av)
        part = lax.dot_general(av.astype(BF16), b_ref[...].astype(BF16), dims, preferred_element_type=F32)
        if nk == 1:
            finish(part)
            return
        acc_ref = refs[-1]
        kk = pl.program_id(2)

        @pl.when(kk == 0)
        def _():
            acc_ref[...] = part

        @pl.when(kk > 0)
        def _():
            acc_ref[...] += part

        @pl.when(kk == nk - 1)
        def _():
            finish(acc_ref[...])

    if a_chunks:
        a_spec = pl.BlockSpec((a_chunks, bm, a.shape[2]), lambda i, j, q: (0, i, 0))
    elif ta:
        a_spec = pl.BlockSpec((bk, bm), lambda i, j, q: (q, i),
                              **(dict(pipeline_mode=pl.Buffered(1)) if nk == 1 and m == bm else {}))
    else:
        a_spec = pl.BlockSpec((bm, bk), lambda i, j, q: (i, q))
    if b_chunks:
        per = b.shape[2] // bn
        b_spec = pl.BlockSpec((None, bk, bn), lambda i, j, q: (j // per, q, j % per))
    else:
        still = dict(pipeline_mode=pl.Buffered(1)) if nk == 1 and n == bn else {}
        b_spec = (pl.BlockSpec((bn, bk), lambda i, j, q: (j, q), **still) if tb
                  else pl.BlockSpec((bk, bn), lambda i, j, q: (q, j), **still))
    ex_specs = []
    for arr, kind in extras:
        if kind == 'mn':
            assert arr.shape == (m, n), (arr.shape, m, n)
            ex_specs.append(pl.BlockSpec((bm, bn), lambda i, j, q: (i, j)))
        else:
            assert arr.shape == (1, n), (arr.shape, n)
            ex_specs.append(pl.BlockSpec((1, bn), lambda i, j, q: (0, j)))
    out_shape = [jax.ShapeDtypeStruct((m, n), d) for d in out_dtypes]
    out_specs = [pl.BlockSpec((bm, bn), lambda i, j, q: (i, j)) for _ in out_dtypes]
    if n_stats:
        assert n_stats <= SUBLANE
        out_shape.append(jax.ShapeDtypeStruct((SUBLANE, n), F32))
        out_specs.append(pl.BlockSpec((SUBLANE, bn), lambda i, j, q: (0, j)))
    outs = pl.pallas_call(
        body, name=name, out_shape=tuple(out_shape), grid=(m // bm, n // bn, nk),
        in_specs=[a_spec, b_spec] + ex_specs, out_specs=tuple(out_specs),
        scratch_shapes=[pltpu.VMEM((bm, bn), F32)] if nk > 1 else [],
        compiler_params=_params(*(["arbitrary"] * 3 if n_stats else ["parallel", "parallel", "arbitrary"])),
    )(a, b, *[arr for arr, _ in extras])
    return outs if len(outs) > 1 else outs[0]


def _epi_residual_norm(acc, res, gate, w, sh):
    xn = res + gate * acc
    return xn, acc, xn * _rstd(xn) * w + sh


def _epi_norm_bwd(gated):
    def epi(dh, xv, w, dres, *gate):
        rstd = _rstd(xv)
        xn = xv * rstd
        dxn = dh * w
        dx = rstd * (dxn - xn * jnp.mean(dxn * xn, axis=-1, keepdims=True)) + dres
        stats = [jnp.sum(dh * xn, axis=0, keepdims=True), jnp.sum(dh, axis=0, keepdims=True)]
        if not gated:
            return (dx, *stats)
        yv, g = gate
        return (dx, dx * g, *stats, jnp.sum(dx * yv.astype(F32), axis=0, keepdims=True))
    return epi


def _epi_loss_head(f, x_mid, g, tgt, fg):
    xv = x_mid + g * f
    rstd = _rstd(xv)
    xn = xv * rstd
    err = xn * fg - tgt
    loss = 0.5 * jnp.sum(jnp.mean(err * err, axis=-1, keepdims=True))
    dout = err * (1.0 / xv.shape[-1])
    dxn = dout * fg
    dx = rstd * (dxn - xn * jnp.mean(dxn * xn, axis=-1, keepdims=True))
    return (dx, dx * g, jnp.full((1, xv.shape[-1]), loss, F32), jnp.sum(dout * xn, axis=0, keepdims=True),
            jnp.sum(dx * f, axis=0, keepdims=True))


def _epi_glu_bwd(dy2, yv, t):
    sig = 1.0 / (1.0 + jnp.exp(-t))
    dt = dy2 * yv * sig * (1.0 - sig)
    return dt, dy2 * sig, jnp.sum(dt, axis=0, keepdims=True)


def _wgrad(acts, cots, *, name, a_fn=None, bm=1024, bn=512):
    return _mm(acts, cots, ta=True, name=name, out_dtypes=(BF16,), a_fn=a_fn, bm=bm, bn=bn, bk=acts.shape[0])


def _square(a):
    af = a.astype(F32)
    return af * af


def _rstd(xv):
    return lax.rsqrt(jnp.mean(xv * xv, axis=-1, keepdims=True) + EPS)


def _normmod_fwd(x, w, sh, *, name, tm=512):
    L, D = x.shape

    def body(x_ref, w_ref, s_ref, h_ref):
        xv = x_ref[...]
        h_ref[...] = (xv * _rstd(xv) * w_ref[...] + s_ref[...]).astype(h_ref.dtype)

    row = pl.BlockSpec((tm, D), lambda i: (i, 0))
    vec = pl.BlockSpec((1, D), lambda i: (0, 0))
    return pl.pallas_call(body, name=name, out_shape=jax.ShapeDtypeStruct((L, D), BF16), grid=(L // tm,),
                          in_specs=[row, vec, vec], out_specs=row, compiler_params=_params("parallel"))(x, w, sh)


def _shift_down(v, k):
    row = lax.broadcasted_iota(jnp.int32, v.shape, 0)
    return jnp.where(row >= k, pltpu.roll(v, k, 0), 0.0)


def _shift_up(v, k):
    n = v.shape[0]
    row = lax.broadcasted_iota(jnp.int32, v.shape, 0)
    return jnp.where(row < n - k, pltpu.roll(v, n - k, 0), 0.0)


def _conv_views(L, D):
    return [pl.BlockSpec((L, LANE), lambda j, s=s: (0, s * (D // LANE) + j)) for s in range(3)]


def _conv_fwd(bcx, wb, *, name):
    L, D = bcx.shape[0], bcx.shape[1] // 3

    def body(b_ref, c_ref, x_ref, wb_ref, p_ref):
        z = c_ref[...].astype(F32) * x_ref[...].astype(F32)
        conv = (wb_ref[0:1, :] * _shift_down(z, 2) + wb_ref[1:2, :] * _shift_down(z, 1)
                + wb_ref[2:3, :] * z + wb_ref[3:4, :])
        p_ref[...] = (b_ref[...].astype(F32) * conv).astype(p_ref.dtype)

    col = pl.BlockSpec((L, LANE), lambda j: (0, j))
    return pl.pallas_call(body, name=name, out_shape=jax.ShapeDtypeStruct((L, D), BF16), grid=(D // LANE,),
                          in_specs=_conv_views(L, D) + [pl.BlockSpec((SUBLANE, LANE), lambda j: (0, j))],
                          out_specs=col, compiler_params=_params("parallel"))(bcx, bcx, bcx, wb)


def _conv_bwd(dp, bcx, wb, *, name):
    L, D = dp.shape

    def body(dp_ref, b_ref, c_ref, x_ref, wb_ref, d3_ref, st_ref):
        cv, xv = c_ref[...].astype(F32), x_ref[...].astype(F32)
        z = cv * xv
        z1, z2 = _shift_down(z, 1), _shift_down(z, 2)
        w0, w1, w2 = wb_ref[0:1, :], wb_ref[1:2, :], wb_ref[2:3, :]
        conv = w0 * z2 + w1 * z1 + w2 * z + wb_ref[3:4, :]
        dpv = dp_ref[...].astype(F32)
        d3_ref[0] = (dpv * conv).astype(d3_ref.dtype)
        dconv = dpv * b_ref[...].astype(F32)
        dz = w2 * dconv + w1 * _shift_up(dconv, 1) + w0 * _shift_up(dconv, 2)
        d3_ref[1] = (dz * xv).astype(d3_ref.dtype)
        d3_ref[2] = (dz * cv).astype(d3_ref.dtype)
        st_ref[...] = jnp.zeros_like(st_ref)
        st_ref[0:1, :] = jnp.sum(dconv * z2, axis=0, keepdims=True)
        st_ref[1:2, :] = jnp.sum(dconv * z1, axis=0, keepdims=True)
        st_ref[2:3, :] = jnp.sum(dconv * z, axis=0, keepdims=True)
        st_ref[3:4, :] = jnp.sum(dconv, axis=0, keepdims=True)

    col = pl.BlockSpec((L, LANE), lambda j: (0, j))
    vec = pl.BlockSpec((SUBLANE, LANE), lambda j: (0, j))
    return pl.pallas_call(body, name=name,
                          out_shape=(jax.ShapeDtypeStruct((3, L, D), BF16), jax.ShapeDtypeStruct((SUBLANE, D), F32)),
                          grid=(D // LANE,), in_specs=[col] + _conv_views(L, D) + [vec],
                          out_specs=(pl.BlockSpec((3, L, LANE), lambda j: (0, 0, j)), vec),
                          compiler_params=_params("parallel"))(dp, bcx, bcx, bcx, wb)


def _sg_fwd(uv, vg, ws, bsb, *, name, tr=512):
    L, D = uv.shape[0], uv.shape[1] // 2

    def body(uv_ref, vg_ref, ws_ref, bsb_ref, p_ref):
        for ci in range(tr // SG_CHUNK):
            rows = slice(ci * SG_CHUNK, (ci + 1) * SG_CHUNK)
            v = uv_ref[rows, D:2 * D]
            vn = (v * _rstd(v) * vg_ref[...]).astype(BF16)
            for h in range(SG_HEADS):
                cols = slice(h * LANE, (h + 1) * LANE)
                vm = jnp.dot(ws_ref[h], vn[:, cols], preferred_element_type=F32) + bsb_ref[h]
                p_ref[rows, cols] = (uv_ref[rows, cols] * vm).astype(p_ref.dtype)

    full3 = pl.BlockSpec((SG_HEADS, SG_CHUNK, LANE), lambda i: (0, 0, 0))
    return pl.pallas_call(body, name=name, out_shape=jax.ShapeDtypeStruct((L, D), BF16), grid=(L // tr,),
                          in_specs=[pl.BlockSpec((tr, 2 * D), lambda i: (i, 0)), pl.BlockSpec((1, D), lambda i: (0, 0)),
                                    full3, full3],
                          out_specs=pl.BlockSpec((tr, D), lambda i: (i, 0)),
                          compiler_params=_params("parallel"))(uv, vg, ws, bsb)


def _sg_bwd(dp, uv, vg, ws, wst, bsb, *, name, tr=512):
    L, D = dp.shape

    def body(dp_ref, uv_ref, vg_ref, ws_ref, wst_ref, bsb_ref, duv_ref, dws_ref, dbs_ref, st_ref, dvn_ref):
        i = pl.program_id(0)

        @pl.when(i == 0)
        def _():
            dws_ref[...] = jnp.zeros_like(dws_ref)
            dbs_ref[...] = jnp.zeros_like(dbs_ref)
            st_ref[...] = jnp.zeros_like(st_ref)

        for ci in range(tr // SG_CHUNK):
            rows = slice(ci * SG_CHUNK, (ci + 1) * SG_CHUNK)
            v = uv_ref[rows, D:2 * D]
            rstd = _rstd(v)
            vhat = v * rstd
            vn = (vhat * vg_ref[...]).astype(BF16)
            for h in range(SG_HEADS):
                cols = slice(h * LANE, (h + 1) * LANE)
                vm = jnp.dot(ws_ref[h], vn[:, cols], preferred_element_type=F32) + bsb_ref[h]
                dph = dp_ref[rows, cols]
                duv_ref[rows, cols] = (dph * vm).astype(duv_ref.dtype)
                dvm = dph * uv_ref[rows, cols]
                dbs_ref[h] += dvm
                dvmb = dvm.astype(BF16)
                dws_ref[h] += lax.dot_general(dvmb, vn[:, cols], (((1,), (1,)), ((), ())),
                                              preferred_element_type=F32)
                dvn_ref[rows, cols] = jnp.dot(wst_ref[h], dvmb, preferred_element_type=F32)
            dvn = dvn_ref[rows, :]
            gv = dvn * vg_ref[...]
            dv = rstd * (gv - vhat * jnp.mean(gv * vhat, axis=-1, keepdims=True))
            duv_ref[rows, D:2 * D] = dv.astype(duv_ref.dtype)
            st_ref[0:1, :] += jnp.sum(dvn * vhat, axis=0, keepdims=True)

    full3 = pl.BlockSpec((SG_HEADS, SG_CHUNK, LANE), lambda i: (0, 0, 0))
    acc3 = jax.ShapeDtypeStruct((SG_HEADS, SG_CHUNK, LANE), F32)
    return pl.pallas_call(
        body, name=name,
        out_shape=(jax.ShapeDtypeStruct((L, 2 * D), BF16), acc3, acc3, jax.ShapeDtypeStruct((SUBLANE, D), F32)),
        grid=(L // tr,),
        in_specs=[pl.BlockSpec((tr, D), lambda i: (i, 0)), pl.BlockSpec((tr, 2 * D), lambda i: (i, 0)),
                  pl.BlockSpec((1, D), lambda i: (0, 0)), full3, full3, full3],
        out_specs=(pl.BlockSpec((tr, 2 * D), lambda i: (i, 0)), full3, full3,
                   pl.BlockSpec((SUBLANE, D), lambda i: (0, 0))),
        scratch_shapes=[pltpu.VMEM((tr, D), F32)],
        compiler_params=_params("arbitrary"))(dp, uv, vg, ws, wst, bsb)


def _gelu(x):
    return 0.5 * x * (1.0 + jnp.tanh(GELU_C * (x + GELU_A * x * x * x)))


def _gelu_grad(x):
    th = jnp.tanh(GELU_C * (x + GELU_A * x * x * x))
    return 0.5 * (1.0 + th) + 0.5 * x * (1.0 - th * th) * GELU_C * (1.0 + 3.0 * GELU_A * x * x)


def _cmul_add(xr, xi, ar, ai, br, bi):
    return xr + ar * br - ai * bi, xi + ar * bi + ai * br


def _cmul_conj_add(xr, xi, ar, ai, br, bi):
    return xr + ar * br + ai * bi, xi + ar * bi - ai * br


def _to_subchunk_order(src_ref, dst_ref, n):
    for k in range(n):
        dst_ref[pl.ds(SUBLANE * k, SUBLANE), :] = src_ref[pl.ds(k, SUBLANE, stride=n), :].astype(dst_ref.dtype)


def _to_time_order(src_ref, dst_ref, n):
    for m in range(n):
        r, k = divmod(SUBLANE * m, n)
        dst_ref[pl.ds(SUBLANE * m, SUBLANE), :] = src_ref[pl.ds(SUBLANE * k + r, SUBLANE, stride=SUBLANE), :]


def _s5_fwd(u, bre, bim, cre, cim, pw, pos, dsk, *, name, tc=S5_CHUNK):
    L, D = u.shape
    W = S5_LANES // S5_BLOCKS
    nt = L // tc
    n = tc // SUBLANE

    def sub(k):
        return pl.ds(SUBLANE * k, SUBLANE)

    def body(u_ref, bre_ref, bim_ref, cre_ref, cim_ref, pw_ref, pos_ref, d_ref, sre_ref, sim_ref, ypre_ref, yg_ref,
             carry, up, yp):
        t = pl.program_id(1)

        @pl.when(t == 0)
        def _():
            carry[...] = jnp.zeros_like(carry)

        _to_subchunk_order(u_ref, up, n)
        uv = up[...]
        ub = uv.astype(BF16)
        sre_ref[...] = jnp.dot(ub, bre_ref[...], preferred_element_type=F32)
        sim_ref[...] = jnp.dot(ub, bim_ref[...], preferred_element_type=F32)

        ar, ai = pw_ref[8], pw_ref[9]
        xr = jnp.zeros((SUBLANE, W), F32)
        xi = jnp.zeros((SUBLANE, W), F32)
        for k in range(n):
            xr, xi = _cmul_add(sre_ref[sub(k), :], sim_ref[sub(k), :], ar, ai, xr, xi)
            sre_ref[sub(k), :] = xr
            sim_ref[sub(k), :] = xi
        for q, d in enumerate((1, 2, 4)):
            xr, xi = _cmul_add(xr, xi, pw_ref[2 * q], pw_ref[2 * q + 1], pltpu.roll(xr, d, 0), pltpu.roll(xi, d, 0))
        cr, ci = carry[0], carry[1]
        xr, xi = _cmul_add(xr, xi, pw_ref[6], pw_ref[7], cr, ci)
        first = lax.broadcasted_iota(jnp.int32, (SUBLANE, W), 0) == 0
        er = jnp.where(first, cr, pltpu.roll(xr, 1, 0))
        ei = jnp.where(first, ci, pltpu.roll(xi, 1, 0))
        last = slice(SUBLANE - 1, SUBLANE)
        carry[0] = jnp.broadcast_to(xr[last, :], (SUBLANE, W))
        carry[1] = jnp.broadcast_to(xi[last, :], (SUBLANE, W))
        for k in range(n):
            sr, si = _cmul_add(sre_ref[sub(k), :], sim_ref[sub(k), :], pos_ref[0, k:k + 1, :], pos_ref[1, k:k + 1, :],
                               er, ei)
            sre_ref[sub(k), :] = sr
            sim_ref[sub(k), :] = si
        yp[...] = (jnp.dot(sre_ref[...].astype(BF16), cre_ref[...], preferred_element_type=F32)
                   - jnp.dot(sim_ref[...].astype(BF16), cim_ref[...], preferred_element_type=F32) + d_ref[...] * uv)
        _to_time_order(yp, ypre_ref, n)
        yg_ref[...] = _gelu(ypre_ref[...])

    ch = pl.BlockSpec((tc, LANE), lambda j, t: (t, j))
    st = pl.BlockSpec((tc, W), lambda j, t: (t, j))
    bsp = pl.BlockSpec((None, LANE, W), lambda j, t: (j, 0, 0))
    csp = pl.BlockSpec((None, W, LANE), lambda j, t: (j, 0, 0))
    return pl.pallas_call(
        body, name=name,
        out_shape=(jax.ShapeDtypeStruct((L, S5_LANES), F32), jax.ShapeDtypeStruct((L, S5_LANES), F32),
                   jax.ShapeDtypeStruct((L, D), F32), jax.ShapeDtypeStruct((L, D), F32)),
        grid=(S5_BLOCKS, nt),
        in_specs=[ch, bsp, bsp, csp, csp, pl.BlockSpec((10, SUBLANE, W), lambda j, t: (0, 0, j)),
                  pl.BlockSpec((2, n, W), lambda j, t: (0, 0, j)), pl.BlockSpec((1, LANE), lambda j, t: (0, j))],
        out_specs=(st, st, ch, ch),
        scratch_shapes=[pltpu.VMEM((2, SUBLANE, W), F32), pltpu.VMEM((tc, LANE), F32), pltpu.VMEM((tc, LANE), F32)],
        compiler_params=_params("parallel", "arbitrary"))(u, bre, bim, cre, cim, pw, pos, dsk)


def _s5_bwd(dy, u, sre, sim, bre, bim, cre, cim, pwr, posr, dsk, *, name, tc=S5_CHUNK):
    L, D = u.shape
    W = S5_LANES // S5_BLOCKS
    nt = L // tc
    n = tc // SUBLANE
    nt_dims = (((1,), (1,)), ((), ()))
    tn_dims = (((0,), (0,)), ((), ()))

    def sub(k):
        return pl.ds(SUBLANE * k, SUBLANE)

    def body(dy_ref, u_ref, sre_ref, sim_ref, bre_ref, bim_ref, cre_ref, cim_ref, pw_ref, pos_ref, d_ref,
             du_ref, dbre_ref, dbim_ref, dcre_ref, dcim_ref, ga_ref, dd_ref, gre, gim, carry, gacc, up, dyp):
        t = pl.program_id(1)

        @pl.when(t == 0)
        def _():
            for r in (carry, gacc, dbre_ref, dbim_ref, dcre_ref, dcim_ref, ga_ref, dd_ref):
                r[...] = jnp.zeros_like(r)

        _to_subchunk_order(dy_ref, dyp, n)
        _to_subchunk_order(u_ref, up, n)
        dyv, uv = dyp[...], up[...]
        dyb, ub = dyv.astype(BF16), uv.astype(BF16)
        gre[...] = lax.dot_general(dyb, cre_ref[...], nt_dims, preferred_element_type=F32)
        gim[...] = -lax.dot_general(dyb, cim_ref[...], nt_dims, preferred_element_type=F32)
        br, bi = pw_ref[8], pw_ref[9]
        xr = jnp.zeros((SUBLANE, W), F32)
        xi = jnp.zeros((SUBLANE, W), F32)
        for k in reversed(range(n)):
            xr, xi = _cmul_add(gre[sub(k), :], gim[sub(k), :], br, bi, xr, xi)
            gre[sub(k), :] = xr
            gim[sub(k), :] = xi
        for q, d in enumerate((1, 2, 4)):
            xr, xi = _cmul_add(xr, xi, pw_ref[2 * q], pw_ref[2 * q + 1], pltpu.roll(xr, SUBLANE - d, 0),
                               pltpu.roll(xi, SUBLANE - d, 0))
        cr, ci = carry[0], carry[1]
        xr, xi = _cmul_add(xr, xi, pw_ref[6], pw_ref[7], cr, ci)
        top = lax.broadcasted_iota(jnp.int32, (SUBLANE, W), 0) == SUBLANE - 1
        er = jnp.where(top, cr, pltpu.roll(xr, SUBLANE - 1, 0))
        ei = jnp.where(top, ci, pltpu.roll(xi, SUBLANE - 1, 0))
        carry[0] = jnp.broadcast_to(xr[0:1, :], (SUBLANE, W))
        carry[1] = jnp.broadcast_to(xi[0:1, :], (SUBLANE, W))
        nr, ni = er, ei
        acc_r = jnp.zeros((SUBLANE, W), F32)
        acc_i = jnp.zeros((SUBLANE, W), F32)
        for k in reversed(range(n)):
            place = slice(n - 1 - k, n - k)
            gr, gi = _cmul_conj_add(gre[sub(k), :], gim[sub(k), :], pos_ref[0, place, :], pos_ref[1, place, :], er, ei)
            gre[sub(k), :] = gr
            gim[sub(k), :] = gi
            sr, si = sre_ref[sub(k), :], sim_ref[sub(k), :]
            acc_r = acc_r + sr * nr + si * ni
            acc_i = acc_i + sr * ni - si * nr
            nr, ni = gr, gi
        gacc[0] += acc_r
        gacc[1] += acc_i
        grb, gib = gre[...].astype(BF16), gim[...].astype(BF16)
        dyp[...] = (lax.dot_general(grb, bre_ref[...], nt_dims, preferred_element_type=F32)
                    + lax.dot_general(gib, bim_ref[...], nt_dims, preferred_element_type=F32) + d_ref[...] * dyv)
        _to_time_order(dyp, up, n)
        du_ref[...] = up[...].astype(du_ref.dtype)
        dbre_ref[...] += lax.dot_general(ub, grb, tn_dims, preferred_element_type=F32)
        dbim_ref[...] += lax.dot_general(ub, gib, tn_dims, preferred_element_type=F32)
        dcre_ref[...] += lax.dot_general(sre_ref[...].astype(BF16), dyb, tn_dims, preferred_element_type=F32)
        dcim_ref[...] -= lax.dot_general(sim_ref[...].astype(BF16), dyb, tn_dims, preferred_element_type=F32)
        dd_ref[0:1, :] += jnp.sum(dyv * uv, axis=0, keepdims=True)

        @pl.when(t == nt - 1)
        def _():
            ga_ref[0:1, :] = jnp.sum(gacc[0], axis=0, keepdims=True)
            ga_ref[1:2, :] = jnp.sum(gacc[1], axis=0, keepdims=True)

    ch = pl.BlockSpec((tc, LANE), lambda j, t: (nt - 1 - t, j))
    st = pl.BlockSpec((tc, W), lambda j, t: (nt - 1 - t, j))
    bsp = pl.BlockSpec((None, LANE, W), lambda j, t: (j, 0, 0))
    csp = pl.BlockSpec((None, W, LANE), lambda j, t: (j, 0, 0))
    return pl.pallas_call(
        body, name=name,
        out_shape=(jax.ShapeDtypeStruct((L, D), BF16),
                   jax.ShapeDtypeStruct((S5_BLOCKS, LANE, W), F32), jax.ShapeDtypeStruct((S5_BLOCKS, LANE, W), F32),
                   jax.ShapeDtypeStruct((S5_BLOCKS, W, LANE), F32), jax.ShapeDtypeStruct((S5_BLOCKS, W, LANE), F32),
                   jax.ShapeDtypeStruct((SUBLANE, S5_LANES), F32), jax.ShapeDtypeStruct((SUBLANE, D), F32)),
        grid=(S5_BLOCKS, nt),
        in_specs=[ch, ch, st, st, bsp, bsp, csp, csp, pl.BlockSpec((10, SUBLANE, W), lambda j, t: (0, 0, j)),
                  pl.BlockSpec((2, n, W), lambda j, t: (0, 0, j)), pl.BlockSpec((1, LANE), lambda j, t: (0, j))],
        out_specs=(ch, bsp, bsp, csp, csp, pl.BlockSpec((SUBLANE, W), lambda j, t: (0, j)),
                   pl.BlockSpec((SUBLANE, LANE), lambda j, t: (0, j))),
        scratch_shapes=[pltpu.VMEM((tc, W), F32), pltpu.VMEM((tc, W), F32), pltpu.VMEM((2, SUBLANE, W), F32),
                        pltpu.VMEM((2, SUBLANE, W), F32), pltpu.VMEM((tc, LANE), F32), pltpu.VMEM((tc, LANE), F32)],
        compiler_params=_params("parallel", "arbitrary"))(dy, u, sre, sim, bre, bim, cre, cim, pwr, posr, dsk)


def _s5_prep(a_re, a_im, log_dt, b_re, b_im, c_re, c_im):
    dt = jnp.exp(log_dt)[:, None]
    mag = jnp.exp(a_re * dt)
    abar_re = mag * jnp.cos(a_im * dt)
    abar_im = mag * jnp.sin(a_im * dt)
    den = a_re * a_re + a_im * a_im
    nr = abar_re - 1.0
    ni = abar_im
    f_re = ((nr * a_re + ni * a_im) / den)[..., None]
    f_im = ((ni * a_re - nr * a_im) / den)[..., None]
    bbar_re = f_re * b_re - f_im * b_im
    bbar_im = f_re * b_im + f_im * b_re
    eye = jnp.eye(S5_GROUPS // S5_BLOCKS, dtype=F32)
    gb = S5_GROUPS // S5_BLOCKS

    def blk_b(bb):
        t = bb.reshape(S5_BLOCKS, gb, S5_STATE, S5_GROUP)
        return jnp.einsum('jgph,gk->jghkp', t, eye).reshape(S5_BLOCKS, gb * S5_GROUP, gb * S5_STATE)

    def blk_c(cc):
        t = cc.reshape(S5_BLOCKS, gb, S5_GROUP, S5_STATE)
        return jnp.einsum('jghp,gk->jgpkh', t, eye).reshape(S5_BLOCKS, gb * S5_STATE, gb * S5_GROUP)

    return (abar_re.reshape(1, S5_LANES), abar_im.reshape(1, S5_LANES), blk_b(bbar_re), blk_b(bbar_im),
            blk_c(c_re), blk_c(c_im))


def _cpowers(ar, ai, count):
    pr, pi, m = ar, ai, 1
    while m < count:
        tr, ti = pr[m - 1:m], pi[m - 1:m]
        pr, pi = jnp.concatenate([pr, pr * tr - pi * ti], 0), jnp.concatenate([pi, pr * ti + pi * tr], 0)
        m *= 2
    return pr, pi


def _s5_power_tables(ar, ai, n):
    pr, pi = _cpowers(ar, ai, n)
    qr, qi = _cpowers(pr[n - 1:n], pi[n - 1:n], SUBLANE)
    row = jnp.arange(SUBLANE)[:, None]
    lanes = ar.shape[1]

    def tables(sign, keep, order):
        out = []
        for d in (1, 2, 4):
            out += [jnp.where(keep(d), qr[d - 1:d], 0.0), jnp.where(keep(d), sign * qi[d - 1:d], 0.0)]
        out += [jnp.concatenate([qr[r:r + 1] for r in order], 0), sign * jnp.concatenate([qi[r:r + 1] for r in order], 0),
                ar, sign * ai]
        return jnp.stack([jnp.broadcast_to(o, (SUBLANE, lanes)) for o in out])

    fwd = tables(1.0, lambda d: row >= d, list(range(SUBLANE)))
    rev = tables(-1.0, lambda d: row + d <= SUBLANE - 1, list(reversed(range(SUBLANE))))
    return fwd, rev, jnp.stack([pr, pi])


ADAMW_PART_BLOCK_BYTES = 2 * 1024 * 1024


def _adamw(w, parts, m, v, *, name):
    n, R, C = w.shape
    assert len(parts) == n
    P = parts[0].shape[0]
    tr = R
    while P * tr * C * parts[0].dtype.itemsize > ADAMW_PART_BLOCK_BYTES and tr % 16 == 0:
        tr //= 2
    c1 = 1.0 / (1.0 - ADAM_B1 ** ADAM_STEP)
    c2 = 1.0 / (1.0 - ADAM_B2 ** ADAM_STEP)

    def body(*refs):
        w_ref, m_ref, v_ref = refs[:3]
        p_refs = refs[3:3 + n]
        g_ref, d_ref, nm_ref, nv_ref = refs[3 + n:]
        layer = pl.program_id(0)
        for q, p_ref in enumerate(p_refs):
            @pl.when(layer == q)
            def _(p_ref=p_ref):
                g = p_ref[0].astype(F32)
                for s in range(1, P):
                    g = g + p_ref[s].astype(F32)
                nm = ADAM_B1 * m_ref[...] + (1.0 - ADAM_B1) * g
                nv = ADAM_B2 * v_ref[...] + (1.0 - ADAM_B2) * (g * g)
                g_ref[...] = g
                nm_ref[...] = nm
                nv_ref[...] = nv
                d_ref[...] = -ADAM_LR * ((nm * c1) / (jnp.sqrt(nv * c2) + ADAM_EPS) + ADAM_WD * w_ref[...])

    row = pl.BlockSpec((None, tr, C), lambda l, i: (l, i, 0))
    part_specs = [pl.BlockSpec((P, tr, C), lambda l, i, q=q: (0, jnp.where(l == q, i, 0), 0)) for q in range(n)]
    out = jax.ShapeDtypeStruct((n, R, C), F32)
    return pl.pallas_call(body, name=name, out_shape=(out, out, out, out), grid=(n, R // tr),
                          in_specs=[row, row, row] + part_specs, out_specs=(row, row, row, row),
                          compiler_params=_params("arbitrary", "arbitrary"))(w, m, v, *parts)


def _all_gather(xs, axis, *, name):
    m = xs.shape[axis]
    out_shape = list(xs.shape)
    out_shape[axis] = N_DEV * m

    def body(x_ref, out_ref, send_sems, recv_sems, local_sem):
        x, y, c = _my_pos()
        me, sibling = (x, y, c), (x, y, 1 - c)
        chips = [(1 - x, y), (x, 1 - y), (1 - x, 1 - y)]

        def blk(px, py, pc):
            idx = [slice(None)] * 3
            idx[axis] = pl.ds((4 * px + 2 * py + pc) * m, m)
            return out_ref.at[tuple(idx)]

        def copy(k, block, to, src=None):
            return pltpu.make_async_remote_copy(src_ref=blk(*block) if src is None else src, dst_ref=blk(*block),
                                                send_sem=send_sems.at[k], recv_sem=recv_sems.at[k],
                                                device_id=to, device_id_type=MESH_ID)

        mine = pltpu.make_async_copy(x_ref, blk(*me), local_sem)
        mine.start()
        first = [copy(0, me, sibling, src=x_ref)]
        first += [copy(1 + j, me, (*chip, c), src=x_ref) for j, chip in enumerate(chips)]
        for cp in first:
            cp.start()
        passed = [copy(4 + j, (*chip, c), sibling) for j, chip in enumerate(chips)]
        for j, chip in enumerate(chips):
            copy(1 + j, (*chip, c), me).wait_recv()
            passed[j].start()
        copy(0, sibling, me).wait_recv()
        for j, chip in enumerate(chips):
            copy(4 + j, (*chip, 1 - c), me).wait_recv()
        for cp in first + passed:
            cp.wait_send()
        mine.wait()

    hbm = pl.BlockSpec(memory_space=pl.ANY)
    return pl.pallas_call(body, name=name, out_shape=jax.ShapeDtypeStruct(tuple(out_shape), xs.dtype),
                          in_specs=[hbm], out_specs=hbm,
                          scratch_shapes=[pltpu.SemaphoreType.DMA((N_DEV - 1,)), pltpu.SemaphoreType.DMA((N_DEV - 1,)),
                                          pltpu.SemaphoreType.DMA],
                          compiler_params=pltpu.CompilerParams(has_side_effects=True))(xs)


NEAR_PEERS = (1, 2, 4, 6)
RELAY_PEERS = (2, 4, 6)


def _block(ref, axis, idx, m):
    return ref.at[pl.ds(idx * m, m), :] if axis == 0 else ref.at[:, pl.ds(idx * m, m)]


def _exchange_copies(metas, src_refs, zone_refs, send_sems, recv_sems, base, phase):
    x, y, c = _my_pos()
    me = 4 * x + 2 * y + c

    def place(r):
        pos = (1 - x if r & 4 else x, 1 - y if r & 2 else y, 1 - c if r & 1 else c)
        return pos, 4 * pos[0] + 2 * pos[1] + pos[2]

    def copies(r, to, src, dst, arrival):
        return tuple(pltpu.make_async_remote_copy(src_ref=src, dst_ref=d, send_sem=send_sems.at[base + r - 1],
                                                  recv_sem=recv_sems.at[base + r - 1], device_id=to,
                                                  device_id_type=MESH_ID) for d in (dst, arrival))

    pairs, own = [], []
    if phase == 'relay':
        sibling, _ = place(1)
        for r in RELAY_PEERS:
            held, comes = place(r)[1], place(r | 1)[1]
            for (kind, axis, m), z_ref in zip(metas, zone_refs):
                pairs.append(copies(r, sibling, _block(z_ref, axis, held, m), _block(z_ref, axis, held, m),
                                    _block(z_ref, axis, comes, m)))
        return pairs, own
    for r in (NEAR_PEERS if phase == 'near' else range(1, N_DEV)):
        pos, peer = place(r)
        for (kind, axis, m), s_ref, z_ref in zip(metas, src_refs, zone_refs):
            if kind == 'gather':
                pairs.append(copies(r, pos, s_ref, _block(z_ref, axis, me, m), _block(z_ref, axis, peer, m)))
            else:
                pairs.append(copies(r, pos, _block(s_ref, axis, peer, m), z_ref.at[me], z_ref.at[peer]))
    for (kind, axis, m), s_ref, z_ref in zip(metas, src_refs, zone_refs):
        src, dst = (s_ref, _block(z_ref, axis, me, m)) if kind == 'gather' else (_block(s_ref, axis, me, m), z_ref.at[me])
        own.append(pltpu.make_async_copy(src, dst, recv_sems.at[base + N_DEV - 1]))
    return pairs, own


def _exchange_start(groups, after, *, name, relayed=False):
    flat = [it for g in groups for it in g]
    n, ng = len(flat), len(groups)
    metas = [it[2] for it in flat]
    bounds = [(sum(len(g) for g in groups[:q]), sum(len(g) for g in groups[:q + 1])) for q in range(ng)]
    phase = 'near' if relayed else 'all'

    def body(*refs):
        src_refs = refs[:n]
        send_sems, recv_sems = refs[n + 1], refs[n + 2]
        zone_refs = refs[2 * n + 3:3 * n + 3]
        token = refs[-1]
        for q, (lo, hi) in enumerate(bounds):
            pairs, own = _exchange_copies(metas[lo:hi], src_refs[lo:hi], zone_refs[lo:hi], send_sems, recv_sems,
                                          q * N_DEV, phase)
            for outgoing, _ in pairs:
                outgoing.start()
            for cp in own:
                cp.start()
        token[...] = jnp.zeros_like(token)

    hbm = pl.BlockSpec(memory_space=pltpu.HBM)
    sem = pl.BlockSpec(memory_space=pltpu.SEMAPHORE)
    srcs = [it[0] for it in flat]
    res = pl.pallas_call(
        body, name=name,
        out_shape=(pltpu.SemaphoreType.DMA((ng * N_DEV,)), pltpu.SemaphoreType.DMA((ng * N_DEV,)),
                   *[pltpu.HBM(a.shape, a.dtype) for a in srcs], *[pltpu.HBM(it[1], it[0].dtype) for it in flat],
                   jax.ShapeDtypeStruct((SUBLANE, LANE), F32)),
        in_specs=[hbm] * n + [pl.BlockSpec(memory_space=pl.ANY)],
        out_specs=(sem, sem, *[hbm] * (2 * n), pl.BlockSpec(memory_space=pltpu.VMEM)),
        input_output_aliases={q: 2 + q for q in range(n)},
        compiler_params=pltpu.CompilerParams(has_side_effects=pltpu.SideEffectType.DATAFLOW_SIDE_EFFECTING),
    )(*[pltpu.with_memory_space_constraint(a, pltpu.HBM) for a in srcs], after)
    handles = [(res[0], res[1], q * N_DEV, phase, list(res[2 + lo:2 + hi]), list(res[2 + n + lo:2 + n + hi]),
                metas[lo:hi]) for q, (lo, hi) in enumerate(bounds)]
    return handles, res[-1]


def _exchange_wait(handle, after, *, name):
    send_sems, recv_sems, base, phase, srcs, zones, metas = handle
    ns, nz = len(srcs), len(zones)

    def body(*refs):
        src_refs, zone_refs = refs[:ns], refs[ns:ns + nz]
        s_sems, r_sems = refs[ns + nz], refs[ns + nz + 1]
        pairs, own = _exchange_copies(metas, src_refs, zone_refs, s_sems, r_sems, base, phase)
        for outgoing, incoming in pairs:
            outgoing.wait_send()
            incoming.wait_recv()
        for cp in own:
            cp.wait()

    hbm = pl.BlockSpec(memory_space=pltpu.HBM)
    sem = pl.BlockSpec(memory_space=pltpu.SEMAPHORE)
    arrays = srcs + zones
    res = pl.pallas_call(
        body, name=name,
        out_shape=tuple(pltpu.HBM(a.shape, a.dtype) for a in arrays),
        in_specs=[hbm] * (ns + nz) + [sem, sem, pl.BlockSpec(memory_space=pl.ANY)],
        out_specs=tuple([hbm] * (ns + nz)),
        input_output_aliases={q: q for q in range(ns + nz)},
        compiler_params=pltpu.CompilerParams(has_side_effects=pltpu.SideEffectType.DATAFLOW_SIDE_EFFECTING),
    )(*arrays, send_sems, recv_sems, after)
    return list(res[ns:])


def _exchange_relay(handle, after, *, name):
    metas = handle[6]
    zones = _exchange_wait(handle, after, name=name + "_in")
    nz = len(zones)

    def body(*refs):
        zone_refs = refs[:nz]
        send_sems, recv_sems = refs[nz], refs[nz + 1]
        pairs, _ = _exchange_copies(metas, (), zone_refs, send_sems, recv_sems, 0, 'relay')
        for outgoing, _ in pairs:
            outgoing.start()
        refs[-1][...] = jnp.zeros_like(refs[-1])

    hbm = pl.BlockSpec(memory_space=pltpu.HBM)
    sem = pl.BlockSpec(memory_space=pltpu.SEMAPHORE)
    res = pl.pallas_call(
        body, name=name + "_out",
        out_shape=(pltpu.SemaphoreType.DMA((N_DEV,)), pltpu.SemaphoreType.DMA((N_DEV,)),
                   *[pltpu.HBM(z.shape, z.dtype) for z in zones], jax.ShapeDtypeStruct((SUBLANE, LANE), F32)),
        in_specs=[hbm] * nz, out_specs=(sem, sem, *[hbm] * nz, pl.BlockSpec(memory_space=pltpu.VMEM)),
        input_output_aliases={q: 2 + q for q in range(nz)},
        compiler_params=pltpu.CompilerParams(has_side_effects=pltpu.SideEffectType.DATAFLOW_SIDE_EFFECTING),
    )(*zones)
    return (res[0], res[1], 0, 'relay', [], list(res[2:2 + nz]), metas), res[-1][0:1, 0:1]


def _pad_rows(a, rows):
    return jnp.pad(a, ((0, rows - a.shape[0]), (0, 0)))


PACK_ROWS = 2 * SUBLANE


def _rows(a):
    flat = a.reshape(-1).astype(F32)
    pad = -flat.shape[0] % (PACK_ROWS * LANE)
    return (jnp.pad(flat, (0, pad)) if pad else flat).reshape(-1, LANE)


def _pack_rows(arrays):
    return jnp.concatenate([_rows(a) for a in arrays], 0)


def _unpack_rows(t, shapes):
    out, off = [], 0
    for shp in shapes:
        size = math.prod(shp)
        rows = -(-size // (PACK_ROWS * LANE)) * PACK_ROWS
        out.append(t[off:off + rows].reshape(-1)[:size].reshape(shp))
        off += rows
    return out


def _stat_row(st, r):
    return st[r:r + 1, :]


def kernel(x, c, ada_w, ada_b, norm1_g, norm2_g, ff_w1, ff_w2, final_g, conv_w_in, conv_w, conv_b, conv_w_out, ssm_w_in, ssm_a_re, ssm_a_im, ssm_log_dt, ssm_b_re, ssm_b_im, ssm_c_re, ssm_c_im, ssm_d, ssm_glu_w, ssm_glu_b, ssm_w_out, sg_w_in, sg_v_g, sg_w_s, sg_b_s, sg_w_out, loss_target, m_ada_w, m_ada_b, m_norm1_g, m_norm2_g, m_ff_w1, m_ff_w2, m_final_g, m_conv_w_in, m_conv_w, m_conv_b, m_conv_w_out, m_ssm_w_in, m_ssm_a_re, m_ssm_a_im, m_ssm_log_dt, m_ssm_b_re, m_ssm_b_im, m_ssm_c_re, m_ssm_c_im, m_ssm_d, m_ssm_glu_w, m_ssm_glu_b, m_ssm_w_out, m_sg_w_in, m_sg_v_g, m_sg_w_s, m_sg_b_s, m_sg_w_out, v_ada_w, v_ada_b, v_norm1_g, v_norm2_g, v_ff_w1, v_ff_w2, v_final_g, v_conv_w_in, v_conv_w, v_conv_b, v_conv_w_out, v_ssm_w_in, v_ssm_a_re, v_ssm_a_im, v_ssm_log_dt, v_ssm_b_re, v_ssm_b_im, v_ssm_c_re, v_ssm_c_im, v_ssm_d, v_ssm_glu_w, v_ssm_glu_b, v_ssm_w_out, v_sg_w_in, v_sg_v_g, v_sg_w_s, v_sg_b_s, v_sg_w_out):
    P = dict(zip(INPUTS, (x, c, ada_w, ada_b, norm1_g, norm2_g, ff_w1, ff_w2, final_g, conv_w_in, conv_w, conv_b, conv_w_out, ssm_w_in, ssm_a_re, ssm_a_im, ssm_log_dt, ssm_b_re, ssm_b_im, ssm_c_re, ssm_c_im, ssm_d, ssm_glu_w, ssm_glu_b, ssm_w_out, sg_w_in, sg_v_g, sg_w_s, sg_b_s, sg_w_out, loss_target, m_ada_w, m_ada_b, m_norm1_g, m_norm2_g, m_ff_w1, m_ff_w2, m_final_g, m_conv_w_in, m_conv_w, m_conv_b, m_conv_w_out, m_ssm_w_in, m_ssm_a_re, m_ssm_a_im, m_ssm_log_dt, m_ssm_b_re, m_ssm_b_im, m_ssm_c_re, m_ssm_c_im, m_ssm_d, m_ssm_glu_w, m_ssm_glu_b, m_ssm_w_out, m_sg_w_in, m_sg_v_g, m_sg_w_s, m_sg_b_s, m_sg_w_out, v_ada_w, v_ada_b, v_norm1_g, v_norm2_g, v_ff_w1, v_ff_w2, v_final_g, v_conv_w_in, v_conv_w, v_conv_b, v_conv_w_out, v_ssm_w_in, v_ssm_a_re, v_ssm_a_im, v_ssm_log_dt, v_ssm_b_re, v_ssm_b_im, v_ssm_c_re, v_ssm_c_im, v_ssm_d, v_ssm_glu_w, v_ssm_glu_b, v_ssm_w_out, v_sg_w_in, v_sg_v_g, v_sg_w_s, v_sg_b_s, v_sg_w_out)))
    L, D = x.shape[1], x.shape[2]
    me = _my_index()
    xs = x[0]
    tgt = loss_target[0]
    n_conv = conv_w_in.shape[0]

    def gather_item(shard, axis):
        full = tuple(N_DEV * s if a == axis else s for a, s in enumerate(shard.shape))
        return shard, full, ('gather', axis, shard.shape[axis])

    def mixer_shards(i):
        kind, j = i % 3, i // 3
        if kind == 0:
            return [(conv_w_in[j], 1), (conv_w_out[j], 0)]
        if kind == 1:
            return [(ssm_w_in[j], 0), (ssm_glu_w[j], 0), (ssm_w_out[j], 0)]
        return [(sg_w_in[j], 1), (sg_w_out[j], 0)]

    gather_groups = [[gather_item(w.astype(BF16), ax) for w, ax in shards]
                     for i in range(DEPTH) for shards in (mixer_shards(i), [(ff_w1[i], 1), (ff_w2[i], 0)])]
    first_gather, first_token = _exchange_start(gather_groups[:1], c, name="gather_start_first", relayed=True)

    c_act = c * (1.0 / (1.0 + jnp.exp(-c))) + first_token[0:1, 0:1]
    vec_rows = jnp.concatenate([c_act.reshape(D // LANE, LANE), conv_w.reshape(-1, LANE), conv_b.reshape(-1, LANE),
                                sg_v_g.reshape(-1, LANE)], 0)
    n_vec = vec_rows.shape[0]
    vec_all = _all_gather(_pad_rows(vec_rows, 24)[None], 0, name="gather_vectors")
    c_all = vec_all[:, :D // LANE, :].reshape(N_DEV, D)
    sharded_full = vec_all[:, D // LANE:n_vec, :].transpose(1, 0, 2).reshape(n_vec - D // LANE, D)
    conv_w_full = sharded_full[:3 * n_conv].reshape(n_conv, 3, D)
    conv_b_full = sharded_full[3 * n_conv:4 * n_conv]
    sg_vg_full = sharded_full[4 * n_conv:4 * n_conv + 1]

    c_pad = _pad_rows(c_all, LANE)
    ncol = ada_w.shape[2]
    mod_part = jnp.stack([_mm(c_pad, ada_w[i], name=f"ada_fwd{i}")[:N_DEV] for i in range(DEPTH)])
    mod_all = _all_gather(mod_part.reshape(1, DEPTH * N_DEV, ncol), 0, name="gather_mod")
    mod_all = mod_all.reshape(N_DEV, DEPTH, N_DEV, ncol)
    mod_me = lax.dynamic_index_in_dim(mod_all, me, 2, keepdims=False)
    mod = mod_me.transpose(1, 0, 2).reshape(DEPTH, N_DEV * ncol) + ada_b
    gathers, gather_token = _exchange_start(gather_groups[1:], mod, name="gather_start", relayed=True)
    gathers = first_gather + gathers
    mod = mod + gather_token[0:1, 0:1]
    relayed = [None] * len(gathers)
    relayed[0], sent = _exchange_relay(gathers[0], mod, name="gather_mix_relay0")

    s5_args = (ssm_a_re[0], ssm_a_im[0], ssm_log_dt[0], ssm_b_re[0], ssm_b_im[0], ssm_c_re[0], ssm_c_im[0])
    (abar_re, abar_im, bblk_re, bblk_im, cblk_re, cblk_im), s5_vjp = jax.vjp(_s5_prep, *s5_args)
    pw_fwd, pw_rev, pos_fwd = _s5_power_tables(abar_re, abar_im, S5_CHUNK // SUBLANE)
    s5_w = tuple(t.astype(BF16) for t in (bblk_re, bblk_im, cblk_re, cblk_im))
    causal = jnp.tril(jnp.ones((SG_CHUNK, SG_CHUNK), dtype=bool))
    ws_m = jnp.where(causal[None], sg_w_s[0], 0.0)
    ws_b = ws_m.astype(BF16)
    wst_b = ws_m.transpose(0, 2, 1).astype(BF16)
    bsb = jnp.broadcast_to(sg_b_s[0][:, :, None], (SG_HEADS, SG_CHUNK, LANE))

    saved = []
    xa = xs
    mods = [[mod[i:i + 1, q * D:(q + 1) * D] for q in range(6)] for i in range(DEPTH)]
    wn1s = [norm1_g[i:i + 1] * (1.0 + mods[i][1]) for i in range(DEPTH)]
    h1 = _normmod_fwd(xa, wn1s[0], mods[0][0] + sent, name="norm1_fwd0")
    for i in range(DEPTH):
        kind, j = i % 3, i // 3
        sh1, sc1, g1, sh2, sc2, g2 = mods[i]
        wn1 = wn1s[i]
        wn2 = norm2_g[i:i + 1] * (1.0 + sc2)
        S = dict(x_in=xa, g1=g1, g2=g2, sc1=sc1, sc2=sc2, wn1=wn1, wn2=wn2)
        w_mix = _exchange_wait(relayed[2 * i], h1, name=f"gather_mix_wait{i}")
        S['h1'] = h1
        if kind == 0:
            bcx = _mm(h1, w_mix[0], name=f"conv_in{i}", out_dtypes=(BF16,), bm=2048)
            wb = _pad_rows(jnp.concatenate([conv_w_full[j], conv_b_full[j:j + 1]], 0), SUBLANE)
            pb = _conv_fwd(bcx, wb, name=f"conv_mix{i}")
            S.update(bcx=bcx, wb=wb, pb=pb)
        elif kind == 1:
            u = _mm(h1, w_mix[0], name=f"ssm_in{i}")
            sre, sim, ypre, yg = _s5_fwd(u, *s5_w, pw_fwd, pos_fwd, ssm_d, name=f"s5_scan{i}")

            def glu_epi(acc, yv, bias):
                t = acc + bias
                return yv * (1.0 / (1.0 + jnp.exp(-t))), t

            pb, tt = _mm(yg, w_mix[1], name=f"ssm_glu{i}", out_dtypes=(BF16, F32), epi=glu_epi,
                         extras=[(yg, 'mn'), (ssm_glu_b, 'n')])
            S.update(u=u, sre=sre, sim=sim, ypre=ypre, yg=yg, pb=pb, tt=tt)
        else:
            uv = _mm(h1, w_mix[0], name=f"sg_in{i}", bm=2048)
            pb = _sg_fwd(uv, sg_vg_full, ws_b, bsb, name=f"sg_mix{i}")
            S.update(uv=uv, pb=pb)
        relayed[2 * i + 1], sent = _exchange_relay(gathers[2 * i + 1], pb, name=f"gather_ff_relay{i}")
        x_mid, y_mix, h2 = _mm(pb, w_mix[-1], name=f"mix_out{i}", out_dtypes=(F32, BF16, BF16), epi=_epi_residual_norm,
                               extras=[(xa, 'mn'), (g1 + sent, 'n'), (wn2, 'n'), (sh2, 'n')])
        w1_full, w2_full = _exchange_wait(relayed[2 * i + 1], h2, name=f"gather_ff_wait{i}")
        S.update(w_mix=w_mix, w1=w1_full, w2=w2_full)
        ra = _mm(h2, w1_full, name=f"ff_up{i}", out_dtypes=(BF16,), epi=lambda acc: (jnp.maximum(acc, 0.0),), bm=2048)
        if i + 1 < DEPTH:
            relayed[2 * i + 2], sent = _exchange_relay(gathers[2 * i + 2], ra, name=f"gather_mix_relay{i + 1}")
            xa, f_out, h1 = _mm(ra, w2_full, name=f"ff_down{i}", out_dtypes=(F32, BF16, BF16), a_fn=_square,
                                epi=_epi_residual_norm, bm=512, bk=w2_full.shape[0],
                                extras=[(x_mid, 'mn'), (g2 + sent, 'n'), (wn1s[i + 1], 'n'), (mods[i + 1][0], 'n')])
        else:
            f_out = None
            dx, dfb, st = _mm(ra, w2_full, name=f"ff_down{i}", out_dtypes=(F32, BF16), epi=_epi_loss_head, a_fn=_square,
                              n_stats=3, bm=512, bk=w2_full.shape[0],
                              extras=[(x_mid, 'mn'), (g2, 'n'), (tgt, 'mn'), (final_g[None], 'n')])
        S.update(x_mid=x_mid, y_mix=y_mix, h2=h2, ra=ra, f_out=f_out)
        saved.append(S)

    loss_tile = st[:, :LANE]
    d_final_g = _stat_row(st, 1)
    dg2_next = _stat_row(st, 2)

    def scatter_item(g, axis):
        m = g.shape[axis] // N_DEV
        shard = tuple(m if a == axis else s for a, s in enumerate(g.shape))
        return g, (N_DEV,) + shard, ('scatter', axis, m)

    dmod = [None] * DEPTH
    dn1g, dn2g = [None] * DEPTH, [None] * DEPTH
    d_conv_w, d_conv_b = [None] * n_conv, [None] * n_conv
    ff_sent, mix_sent = [None] * DEPTH, [None] * DEPTH
    small = {}
    for i in reversed(range(DEPTH)):
        kind, j = i % 3, i // 3
        S = saved[i]
        w_mix = S['w_mix']
        dg2 = dg2_next
        da = _mm(dfb, S['w2'], tb=True, name=f"ff_down_bwd{i}", out_dtypes=(BF16,), bm=2048,
                 epi=lambda acc, rav: (acc * (2.0 * rav.astype(F32)),), extras=[(S['ra'], 'mn')])
        dw2 = _wgrad(S['ra'], dfb, name=f"ff_w2_grad{i}", a_fn=_square, bm=256, bn=1024)
        dw1 = _wgrad(S['h2'], da, name=f"ff_w1_grad{i}", bn=1024)
        (ff_sent[i],), token = _exchange_start([[scatter_item(dw1, 1), scatter_item(dw2, 0)]], dx,
                                               name=f"ff_grads_start{i}")
        dx_mid, dyb, st2 = _mm(da, S['w1'], tb=True, name=f"ff_up_bwd{i}", out_dtypes=(F32, BF16), bm=512,
                               bk=da.shape[1], epi=_epi_norm_bwd(True), n_stats=3,
                               extras=[(S['x_mid'], 'mn'), (S['wn2'] + token[0:1, 0:1], 'n'), (dx, 'mn'),
                                       (S['y_mix'], 'mn'), (S['g1'], 'n')])
        dsc2 = _stat_row(st2, 0) * norm2_g[i:i + 1]
        dn2g[i] = _stat_row(st2, 0) * (1.0 + S['sc2'])
        dsh2 = _stat_row(st2, 1)
        dg1 = _stat_row(st2, 2)
        if kind == 0:
            dp = _mm(dyb, w_mix[1], tb=True, name=f"conv_out_bwd{i}", out_dtypes=(BF16,))
            d_cwo = _wgrad(S['pb'], dyb, name=f"conv_w_out_grad{i}")
            dbcx, stc = _conv_bwd(dp, S['bcx'], S['wb'], name=f"conv_mix_bwd{i}")
            d_conv_w[j] = stc[0:3]
            d_conv_b[j] = stc[3:4]
            dh_operand, dh_name = dbcx, "conv_in_bwd"
            d_cwi = _wgrad(S['h1'], dbcx, name=f"conv_w_in_grad{i}", bn=1024)
            mix_grads = [scatter_item(d_cwi, 1), scatter_item(d_cwo, 0)]
        elif kind == 1:
            dtb, dya, stg = _mm(dyb, w_mix[2], tb=True, name=f"ssm_out_bwd{i}", out_dtypes=(BF16, F32), bm=512,
                                epi=_epi_glu_bwd, n_stats=1, extras=[(S['yg'], 'mn'), (S['tt'], 'mn')])
            d_ssm_out = _wgrad(S['pb'], dyb, name=f"ssm_w_out_grad{i}")
            dypre = _mm(dtb, w_mix[1], tb=True, name=f"ssm_glu_in_bwd{i}",
                        epi=lambda acc, a, yp: ((a + acc) * _gelu_grad(yp),),
                        extras=[(dya, 'mn'), (S['ypre'], 'mn')])
            d_glu = _wgrad(S['yg'], dtb, name=f"ssm_glu_w_grad{i}", bm=512)
            dub, dbre, dbim, dcre, dcim, ga, dd = _s5_bwd(dypre, S['u'], S['sre'], S['sim'], *s5_w, pw_rev, pos_fwd, ssm_d,
                                                           name=f"s5_scan_bwd{i}")
            dh_operand, dh_name = dub, "ssm_in_bwd"
            d_ssm_in = _wgrad(S['h1'], dub, name=f"ssm_w_in_grad{i}")
            da_re, da_im, dlog_dt, db_re, db_im, dc_re, dc_im = s5_vjp((ga[0:1], ga[1:2], dbre, dbim, dcre, dcim))
            s5_small = _pack_rows([da_re, da_im, dlog_dt, db_re, db_im, dc_re, dc_im, dd[0], stg[0]])
            mix_grads = [scatter_item(d_ssm_in, 0), scatter_item(d_glu, 0), scatter_item(d_ssm_out, 0),
                         gather_item(s5_small.astype(BF16), 0)]
        else:
            dp = _mm(dyb, w_mix[1], tb=True, name=f"sg_out_bwd{i}")
            d_sgo = _wgrad(S['pb'], dyb, name=f"sg_w_out_grad{i}")
            duv, dws, dbs, stv = _sg_bwd(dp, S['uv'], sg_vg_full, ws_b, wst_b, bsb, name=f"sg_mix_bwd{i}")
            dh_operand, dh_name = duv, "sg_in_bwd"
            d_sgi = _wgrad(S['h1'], duv, name=f"sg_w_in_grad{i}")
            sg_small = _pack_rows([jnp.where(causal[None], dws, 0.0), jnp.sum(dbs, axis=-1)])
            d_sg_vg = stv[0:1]
            mix_grads = [scatter_item(d_sgi, 1), scatter_item(d_sgo, 0), gather_item(sg_small.astype(BF16), 0)]
        wn1 = S['wn1']
        gate = []
        if i > 0:
            (mix_sent[i],), token = _exchange_start([mix_grads], dx_mid, name=f"mix_grads_start{i}")
            wn1 = wn1 + token[0:1, 0:1]
            gate = [(saved[i - 1]['f_out'], 'mn'), (saved[i - 1]['g2'], 'n')]
        res = _mm(dh_operand, w_mix[0], tb=True, name=f"{dh_name}{i}", out_dtypes=(F32, BF16) if i > 0 else (F32,),
                  bm=512, bk=w_mix[0].shape[1], epi=_epi_norm_bwd(i > 0), n_stats=3 if i > 0 else 2,
                  extras=[(S['x_in'], 'mn'), (wn1, 'n'), (dx_mid, 'mn')] + gate)
        if i > 0:
            dx, dfb, st1 = res
            dg2_next = _stat_row(st1, 2)
        else:
            dx, st1 = res
        dsc1 = _stat_row(st1, 0) * norm1_g[i:i + 1]
        dn1g[i] = _stat_row(st1, 0) * (1.0 + S['sc1'])
        dsh1 = _stat_row(st1, 1)
        dmod[i] = jnp.concatenate([dsh1, dsc1, dg1, dsh2, dsc2, dg2], 1)
    grad_x = dx[None]

    out = {}

    def small_group(names, parts, label):
        shapes = [P[n].shape for n in names]
        w, m, v = (_pack_rows([P[pre + n] for n in names])[None] for pre in ('', 'm_', 'v_'))
        res = [_unpack_rows(t[0], shapes) for t in _adamw(w, [parts], m, v, name=label)]
        for q, n in enumerate(names):
            out[n] = tuple(r[q] for r in res)

    small.update(ada_b=jnp.concatenate(dmod, 0), norm1_g=jnp.concatenate(dn1g, 0), norm2_g=jnp.concatenate(dn2g, 0),
                 final_g=d_final_g, conv_w=jnp.stack(d_conv_w), conv_b=jnp.concatenate(d_conv_b, 0), sg_v_g=d_sg_vg)
    last_pack = _pack_rows([small[n] for n in LAST_SMALL + SMALL_SHARD])
    n_last = _pack_rows([P[n] for n in LAST_SMALL]).shape[0]
    n_pack = last_pack.shape[0]
    pack_all = _all_gather(jnp.concatenate([last_pack, loss_tile], 0)[None], 0, name="gather_small_grads")
    loss = jnp.sum(pack_all[:, n_pack, 0])
    (mix_sent[0],), last_token = _exchange_start([mix_grads], pack_all, name="mix_grads_start0")
    small_group(LAST_SMALL, pack_all[:, :n_last], "adamw_small")
    sh_rows = (n_pack - n_last) // N_DEV
    sh_parts = pack_all[:, n_last:n_pack].reshape(N_DEV, sh_rows, N_DEV, LANE)
    sh_parts = lax.dynamic_index_in_dim(sh_parts, me, 2, keepdims=False)
    sh_parts = jnp.pad(sh_parts, ((0, 0), (0, 16 - sh_rows), (0, 0)))

    def pack_shard(prefix):
        return _pad_rows(jnp.concatenate([P[prefix + n].reshape(-1, LANE) for n in SMALL_SHARD], 0), 16)[None]

    sg_, sd_, sm_, sv_ = _adamw(pack_shard(''), [sh_parts], pack_shard('m_'), pack_shard('v_'), name="adamw_channel")
    off = 0
    for n in SMALL_SHARD:
        rows = math.prod(P[n].shape) // LANE
        out[n] = tuple(t[0, off:off + rows].reshape(P[n].shape) for t in (sg_, sd_, sm_, sv_))
        off += rows

    dmod_all = pack_all[:, :DEPTH * 6 * D // LANE].reshape(N_DEV, DEPTH, 6 * D)
    dmod_cols = lax.dynamic_slice_in_dim(dmod_all, me * ncol, ncol, 2)
    g_ada = [_mm(c_pad, _pad_rows(dmod_cols[:, i], LANE), ta=True, name=f"ada_w_grad{i}")[None] for i in range(DEPTH)]

    def big(name, parts):
        res = _adamw(P[name], parts, P['m_' + name], P['v_' + name], name="adamw_" + name)
        out[name] = res
        return res[1]

    ff_parts = [_exchange_wait(ff_sent[i], last_token, name=f"ff_grads_wait{i}") for i in range(DEPTH)]
    mix_parts = [None] + [_exchange_wait(mix_sent[i], last_token, name=f"mix_grads_wait{i}") for i in range(1, DEPTH)]
    big('ada_w', g_ada)
    big('ff_w1', [p[0] for p in ff_parts])
    big('ff_w2', [p[1] for p in ff_parts])
    done = big('sg_w_in', [mix_parts[2][0]])
    mix_parts[0] = _exchange_wait(mix_sent[0], done, name="mix_grads_wait0")
    big('conv_w_in', [mix_parts[i][0] for i in range(DEPTH) if i % 3 == 0])
    row_names = ['conv_w_out', 'ssm_w_in', 'ssm_glu_w', 'ssm_w_out', 'sg_w_out']
    row_parts = ([mix_parts[i][1] for i in range(DEPTH) if i % 3 == 0] + mix_parts[1][:3] + [mix_parts[2][1]])
    small_group(S5_SMALL, mix_parts[1][3].reshape(N_DEV, -1, LANE), "adamw_s5")
    small_group(SG_SMALL, mix_parts[2][2].reshape(N_DEV, -1, LANE), "adamw_sg")
    row_w, row_m, row_v = (jnp.concatenate([P[pre + n] for n in row_names], 0) for pre in ('', 'm_', 'v_'))
    rw = _adamw(row_w, row_parts, row_m, row_v, name="adamw_row_sharded")
    off = 0
    for n in row_names:
        cnt = P[n].shape[0]
        out[n] = tuple(t[off:off + cnt] for t in rw)
        off += cnt

    return (loss, grad_x, *[out[n][0] for n in WEIGHTS], *[out[n][1] for n in WEIGHTS],
            *[out[n][2] for n in WEIGHTS], *[out[n][3] for n in WEIGHTS])
```
